```python
import jax, jax.numpy as jnp
from jax import lax
import numpy as np

D_MODEL = 1024
BATCH = 8
SEQ = 16384
DEPTH = 4

D_PLE = 256
D_FF = 4 * D_MODEL
HEAD_DIM = 64
D_A = 3 * D_MODEL // 8
D_B = 3 * D_MODEL // 8
D_C = D_MODEL - D_A - D_B
N_HEADS_A = D_A // HEAD_DIM
N_HEADS_B = D_B // HEAD_DIM
POOL_WINDOWS = (2, 4, 8, 16)
N_POOL_GROUPS = len(POOL_WINDOWS)
D_POOL_GROUP = D_C // N_POOL_GROUPS
CHUNK = 128
CONV_WIDTH = 3
D_IN = 2 * D_A + 3 * D_B + D_C
SPLITS = (D_A, 2 * D_A, 2 * D_A + D_B, 2 * D_A + 2 * D_B, 2 * D_A + 3 * D_B)
RMS_EPS = 1e-6
LN_EPS = 1e-5

kernel_name = "hybrid_sgu_conv_pool_trunk"


def rms_norm(x, g):
    xf = x.astype(jnp.float32)
    y = xf * lax.rsqrt(jnp.mean(xf * xf, axis=-1, keepdims=True) + RMS_EPS)
    return (y * g.astype(jnp.float32)).astype(x.dtype)


def spatial_gating(u, v, w_s, b_s, ln_g, ln_b):
    bsz, t, _ = u.shape
    n = t // CHUNK
    u = jax.nn.gelu(u, approximate=False)
    v = jax.nn.gelu(v, approximate=False)
    vf = v.reshape(bsz, n, CHUNK, N_HEADS_A, HEAD_DIM).astype(jnp.float32)
    mu = jnp.mean(vf, axis=-1, keepdims=True)
    var = jnp.mean(jnp.square(vf - mu), axis=-1, keepdims=True)
    vn = ((vf - mu) * lax.rsqrt(var + LN_EPS)
          * ln_g.reshape(N_HEADS_A, HEAD_DIM).astype(jnp.float32)
          + ln_b.reshape(N_HEADS_A, HEAD_DIM).astype(jnp.float32)).astype(u.dtype)
    mask = jnp.tril(jnp.ones((CHUNK, CHUNK), dtype=bool))
    w = jnp.where(mask[None], w_s, jnp.zeros((), w_s.dtype)).astype(u.dtype)
    mixed = jnp.einsum('hts,bnshd->bnthd', w, vn) + b_s.T.astype(u.dtype)[:, :, None]
    return u * mixed.reshape(bsz, t, D_A)


def short_conv(z, gate_b, gate_c, conv_w):
    h = gate_c * z
    y = lax.conv_general_dilated(
        h, conv_w.astype(h.dtype)[:, None, :], window_strides=(1,),
        padding=[(CONV_WIDTH - 1, 0)],
        dimension_numbers=('NWC', 'WIO', 'NWC'), feature_group_count=D_B)
    return gate_b * y


def multiscale_pool(z, w_pool, pool_scale):
    bsz, t, _ = z.shape
    zf = z.astype(jnp.float32)
    cs = jnp.cumsum(zf, axis=1)
    pos_count = jnp.arange(1, t + 1, dtype=jnp.float32)
    outs = []
    for g, win in enumerate(POOL_WINDOWS):
        sl = slice(g * D_POOL_GROUP, (g + 1) * D_POOL_GROUP)
        c = cs[..., sl]
        lag = jnp.pad(c, ((0, 0), (win, 0), (0, 0)))[:, :t]
        mean = (c - lag) / jnp.minimum(pos_count, float(win))[None, :, None]
        outs.append(mean - zf[..., sl])
    pooled = jnp.stack(outs, axis=2).astype(z.dtype)
    y = jnp.einsum('btgc,gcd->btgd', pooled, w_pool)
    return y.reshape(bsz, t, D_C) * pool_scale


def _fwd_setup_inputs(seed: int = 0) -> dict:
    key = jax.random.key(seed)
    ks = jax.random.split(key, 20)
    f32 = jnp.float32
    nrm = lambda k, shape, scale: jax.random.normal(k, shape, f32) * scale
    return {
        "x": nrm(ks[0], (BATCH, SEQ, D_MODEL), 1.0),
        "p": nrm(ks[1], (DEPTH, BATCH, SEQ, D_PLE), 1.0),
        "norm_mix_g": 1.0 + nrm(ks[2], (DEPTH, D_MODEL), 0.05),
        "w_in": nrm(ks[3], (DEPTH, D_MODEL, D_IN), D_MODEL ** -0.5),
        "sgu_w": nrm(ks[4], (DEPTH, N_HEADS_A, CHUNK, CHUNK), CHUNK ** -0.5),
        "sgu_b": 1.0 + nrm(ks[5], (DEPTH, N_HEADS_A, CHUNK), 0.1),
        "sgu_ln_g": 1.0 + nrm(ks[6], (DEPTH, D_A), 0.05),
        "sgu_ln_b": nrm(ks[7], (DEPTH, D_A), 0.02),
        "conv_w": nrm(ks[8], (DEPTH, CONV_WIDTH, D_B), CONV_WIDTH ** -0.5),
        "pool_w": nrm(ks[9], (DEPTH, N_POOL_GROUPS, D_POOL_GROUP, D_POOL_GROUP), D_POOL_GROUP ** -0.5),
        "pool_scale": 1.0 + nrm(ks[10], (DEPTH, D_C), 0.1),
        "w_out": nrm(ks[11], (DEPTH, D_MODEL, D_MODEL), D_MODEL ** -0.5),
        "norm_ff_g": 1.0 + nrm(ks[12], (DEPTH, D_MODEL), 0.05),
        "w_ff1": nrm(ks[13], (DEPTH, D_MODEL, D_FF), D_MODEL ** -0.5),
        "w_ff2": nrm(ks[14], (DEPTH, D_FF, D_MODEL), D_FF ** -0.5),
        "norm_ple_g": 1.0 + nrm(ks[15], (DEPTH, D_MODEL), 0.05),
        "w_ple_gate": nrm(ks[16], (DEPTH, D_MODEL, D_MODEL), D_MODEL ** -0.5),
        "w_ple_proj": nrm(ks[17], (DEPTH, D_PLE, D_MODEL), D_PLE ** -0.5),
        "final_g": 1.0 + nrm(ks[18], (D_MODEL,), 0.05),
    }


def _fwd_reference(x, p, norm_mix_g, w_in, sgu_w, sgu_b, sgu_ln_g, sgu_ln_b, conv_w,
              pool_w, pool_scale, w_out, norm_ff_g, w_ff1, w_ff2, norm_ple_g,
              w_ple_gate, w_ple_proj, final_g):
    for i in range(DEPTH):
        h = rms_norm(x, norm_mix_g[i])
        proj = h @ w_in[i]
        u_a, v_a, z_b, g_b, g_c, z_c = jnp.split(proj, SPLITS, axis=-1)
        y_a = spatial_gating(u_a, v_a, sgu_w[i], sgu_b[i], sgu_ln_g[i], sgu_ln_b[i])
        y_b = short_conv(z_b, g_b, g_c, conv_w[i])
        y_c = multiscale_pool(z_c, pool_w[i], pool_scale[i])
        x = x + jnp.concatenate([y_a, y_b, y_c], axis=-1) @ w_out[i]
        h = rms_norm(x, norm_ff_g[i])
        x = x + jnp.square(jax.nn.relu(h @ w_ff1[i])) @ w_ff2[i]
        gate = jax.nn.sigmoid(rms_norm(x, norm_ple_g[i]) @ w_ple_gate[i])
        x = x + (p[i] @ w_ple_proj[i]) * gate
    return rms_norm(x, final_g)


import jax as _jax
import jax.numpy as _jnp

TWIN_FORMAT = 'train_step'
FWD_PARAMS = ['x', 'p', 'norm_mix_g', 'w_in', 'sgu_w', 'sgu_b', 'sgu_ln_g', 'sgu_ln_b', 'conv_w', 'pool_w', 'pool_scale', 'w_out', 'norm_ff_g', 'w_ff1', 'w_ff2', 'norm_ple_g', 'w_ple_gate', 'w_ple_proj', 'final_g']
TWIN_WEIGHTS = ['norm_mix_g', 'w_in', 'sgu_w', 'sgu_b', 'sgu_ln_g', 'sgu_ln_b', 'conv_w', 'pool_w', 'pool_scale', 'w_out', 'norm_ff_g', 'w_ff1', 'w_ff2', 'norm_ple_g', 'w_ple_gate', 'w_ple_proj', 'final_g']
TWIN_DIFF_INPUT = 'x'
TWIN_INPUTS = ['x', 'p', 'norm_mix_g', 'w_in', 'sgu_w', 'sgu_b', 'sgu_ln_g', 'sgu_ln_b', 'conv_w', 'pool_w', 'pool_scale', 'w_out', 'norm_ff_g', 'w_ff1', 'w_ff2', 'norm_ple_g', 'w_ple_gate', 'w_ple_proj', 'final_g', 'loss_target', 'm_norm_mix_g', 'm_w_in', 'm_sgu_w', 'm_sgu_b', 'm_sgu_ln_g', 'm_sgu_ln_b', 'm_conv_w', 'm_pool_w', 'm_pool_scale', 'm_w_out', 'm_norm_ff_g', 'm_w_ff1', 'm_w_ff2', 'm_norm_ple_g', 'm_w_ple_gate', 'm_w_ple_proj', 'm_final_g', 'v_norm_mix_g', 'v_w_in', 'v_sgu_w', 'v_sgu_b', 'v_sgu_ln_g', 'v_sgu_ln_b', 'v_conv_w', 'v_pool_w', 'v_pool_scale', 'v_w_out', 'v_norm_ff_g', 'v_w_ff1', 'v_w_ff2', 'v_norm_ple_g', 'v_w_ple_gate', 'v_w_ple_proj', 'v_final_g']
TWIN_OUTPUTS = ['loss', 'grad_x', 'grad_norm_mix_g', 'grad_w_in', 'grad_sgu_w', 'grad_sgu_b', 'grad_sgu_ln_g', 'grad_sgu_ln_b', 'grad_conv_w', 'grad_pool_w', 'grad_pool_scale', 'grad_w_out', 'grad_norm_ff_g', 'grad_w_ff1', 'grad_w_ff2', 'grad_norm_ple_g', 'grad_w_ple_gate', 'grad_w_ple_proj', 'grad_final_g', 'delta_norm_mix_g', 'delta_w_in', 'delta_sgu_w', 'delta_sgu_b', 'delta_sgu_ln_g', 'delta_sgu_ln_b', 'delta_conv_w', 'delta_pool_w', 'delta_pool_scale', 'delta_w_out', 'delta_norm_ff_g', 'delta_w_ff1', 'delta_w_ff2', 'delta_norm_ple_g', 'delta_w_ple_gate', 'delta_w_ple_proj', 'delta_final_g', 'new_m_norm_mix_g', 'new_m_w_in', 'new_m_sgu_w', 'new_m_sgu_b', 'new_m_sgu_ln_g', 'new_m_sgu_ln_b', 'new_m_conv_w', 'new_m_pool_w', 'new_m_pool_scale', 'new_m_w_out', 'new_m_norm_ff_g', 'new_m_w_ff1', 'new_m_w_ff2', 'new_m_norm_ple_g', 'new_m_w_ple_gate', 'new_m_w_ple_proj', 'new_m_final_g', 'new_v_norm_mix_g', 'new_v_w_in', 'new_v_sgu_w', 'new_v_sgu_b', 'new_v_sgu_ln_g', 'new_v_sgu_ln_b', 'new_v_conv_w', 'new_v_pool_w', 'new_v_pool_scale', 'new_v_w_out', 'new_v_norm_ff_g', 'new_v_w_ff1', 'new_v_w_ff2', 'new_v_norm_ple_g', 'new_v_w_ple_gate', 'new_v_w_ple_proj', 'new_v_final_g']
TWIN_LEAF_KINDS = {'loss': 'loss', 'grad_x': 'grad_x', 'grad_norm_mix_g': 'grad_w', 'grad_w_in': 'grad_w', 'grad_sgu_w': 'grad_w', 'grad_sgu_b': 'grad_w', 'grad_sgu_ln_g': 'grad_w', 'grad_sgu_ln_b': 'grad_w', 'grad_conv_w': 'grad_w', 'grad_pool_w': 'grad_w', 'grad_pool_scale': 'grad_w', 'grad_w_out': 'grad_w', 'grad_norm_ff_g': 'grad_w', 'grad_w_ff1': 'grad_w', 'grad_w_ff2': 'grad_w', 'grad_norm_ple_g': 'grad_w', 'grad_w_ple_gate': 'grad_w', 'grad_w_ple_proj': 'grad_w', 'grad_final_g': 'grad_w', 'delta_norm_mix_g': 'delta_w', 'delta_w_in': 'delta_w', 'delta_sgu_w': 'delta_w', 'delta_sgu_b': 'delta_w', 'delta_sgu_ln_g': 'delta_w', 'delta_sgu_ln_b': 'delta_w', 'delta_conv_w': 'delta_w', 'delta_pool_w': 'delta_w', 'delta_pool_scale': 'delta_w', 'delta_w_out': 'delta_w', 'delta_norm_ff_g': 'delta_w', 'delta_w_ff1': 'delta_w', 'delta_w_ff2': 'delta_w', 'delta_norm_ple_g': 'delta_w', 'delta_w_ple_gate': 'delta_w', 'delta_w_ple_proj': 'delta_w', 'delta_final_g': 'delta_w', 'new_m_norm_mix_g': 'new_m', 'new_m_w_in': 'new_m', 'new_m_sgu_w': 'new_m', 'new_m_sgu_b': 'new_m', 'new_m_sgu_ln_g': 'new_m', 'new_m_sgu_ln_b': 'new_m', 'new_m_conv_w': 'new_m', 'new_m_pool_w': 'new_m', 'new_m_pool_scale': 'new_m', 'new_m_w_out': 'new_m', 'new_m_norm_ff_g': 'new_m', 'new_m_w_ff1': 'new_m', 'new_m_w_ff2': 'new_m', 'new_m_norm_ple_g': 'new_m', 'new_m_w_ple_gate': 'new_m', 'new_m_w_ple_proj': 'new_m', 'new_m_final_g': 'new_m', 'new_v_norm_mix_g': 'new_v', 'new_v_w_in': 'new_v', 'new_v_sgu_w': 'new_v', 'new_v_sgu_b': 'new_v', 'new_v_sgu_ln_g': 'new_v', 'new_v_sgu_ln_b': 'new_v', 'new_v_conv_w': 'new_v', 'new_v_pool_w': 'new_v', 'new_v_pool_scale': 'new_v', 'new_v_w_out': 'new_v', 'new_v_norm_ff_g': 'new_v', 'new_v_w_ff1': 'new_v', 'new_v_w_ff2': 'new_v', 'new_v_norm_ple_g': 'new_v', 'new_v_w_ple_gate': 'new_v', 'new_v_w_ple_proj': 'new_v', 'new_v_final_g': 'new_v'}


def _forward(args):
    return _fwd_reference(*[args[k] for k in FWD_PARAMS])


def _output_shape():
    def fwd():
        inp = _fwd_setup_inputs(0)
        return _fwd_reference(*[inp[k] for k in FWD_PARAMS])
    out = _jax.eval_shape(fwd)
    return out.shape, out.dtype

N_MICROBATCH = 1
ADAM_LR = 0.001
ADAM_B1 = 0.9
ADAM_B2 = 0.999
ADAM_EPS = 1e-08
ADAM_WD = 0.01
ADAM_STEP = 10
PER_EXAMPLE_BATCH_AXIS = {'x': 0, 'p': 1, 'loss_target': 0}
SHARED_INPUTS = []
_WEIGHT_DTYPES = {'norm_mix_g': _jnp.float32, 'w_in': _jnp.float32, 'sgu_w': _jnp.float32, 'sgu_b': _jnp.float32, 'sgu_ln_g': _jnp.float32, 'sgu_ln_b': _jnp.float32, 'conv_w': _jnp.float32, 'pool_w': _jnp.float32, 'pool_scale': _jnp.float32, 'w_out': _jnp.float32, 'norm_ff_g': _jnp.float32, 'w_ff1': _jnp.float32, 'w_ff2': _jnp.float32, 'norm_ple_g': _jnp.float32, 'w_ple_gate': _jnp.float32, 'w_ple_proj': _jnp.float32, 'final_g': _jnp.float32}
MOMENT_SCALE = {'norm_mix_g': 3.065255e-01, 'w_in': 2.001146e-01, 'sgu_w': 7.253989e-02, 'sgu_b': 1.084523e-01, 'sgu_ln_g': 1.168372e-01, 'sgu_ln_b': 1.056976e-01, 'conv_w': 2.356727e-01, 'pool_w': 1.988403e-01, 'pool_scale': 2.369403e-01, 'w_out': 2.366399e-01, 'norm_ff_g': 2.875772e-01, 'w_ff1': 1.380147e-01, 'w_ff2': 4.465301e-01, 'norm_ple_g': 3.906434e-02, 'w_ple_gate': 3.880249e-02, 'w_ple_proj': 9.042440e-02, 'final_g': 1.298462e+02}


def _to_microbatches(a, axis):
    t = _jnp.moveaxis(a, axis, 0)
    t = t.reshape((N_MICROBATCH, t.shape[0] // N_MICROBATCH) + t.shape[1:])
    return _jnp.moveaxis(t, 1, axis + 1)


def setup_inputs(seed: int = 0) -> dict:
    inp = _fwd_setup_inputs(seed)
    key = _jax.random.fold_in(_jax.random.key(seed), 7919)
    shape, _ = _output_shape()
    out = dict(inp)
    out["loss_target"] = _jax.random.normal(_jax.random.fold_in(key, 0), shape, _jnp.float32)
    for i, name in enumerate(TWIN_WEIGHTS):
        w = inp[name].astype(_jnp.float32)
        if MOMENT_SCALE is None:
            s = _jnp.sqrt(_jnp.mean(_jnp.square(w)) + 1e-30)
        else:
            s = MOMENT_SCALE[name]
        km, kv = _jax.random.split(_jax.random.fold_in(key, i + 1))
        out[name] = w
        out["m_" + name] = s * _jax.random.normal(km, w.shape, _jnp.float32)
        out["v_" + name] = (s * s) * _jax.random.uniform(kv, w.shape, _jnp.float32, 0.5, 1.5)
    if N_MICROBATCH > 1:
        for name, axis in PER_EXAMPLE_BATCH_AXIS.items():
            out[name] = _to_microbatches(out[name], axis)
    return {'x': out['x'], 'p': out['p'], 'norm_mix_g': out['norm_mix_g'], 'w_in': out['w_in'], 'sgu_w': out['sgu_w'], 'sgu_b': out['sgu_b'], 'sgu_ln_g': out['sgu_ln_g'], 'sgu_ln_b': out['sgu_ln_b'], 'conv_w': out['conv_w'], 'pool_w': out['pool_w'], 'pool_scale': out['pool_scale'], 'w_out': out['w_out'], 'norm_ff_g': out['norm_ff_g'], 'w_ff1': out['w_ff1'], 'w_ff2': out['w_ff2'], 'norm_ple_g': out['norm_ple_g'], 'w_ple_gate': out['w_ple_gate'], 'w_ple_proj': out['w_ple_proj'], 'final_g': out['final_g'], 'loss_target': out['loss_target'], 'm_norm_mix_g': out['m_norm_mix_g'], 'm_w_in': out['m_w_in'], 'm_sgu_w': out['m_sgu_w'], 'm_sgu_b': out['m_sgu_b'], 'm_sgu_ln_g': out['m_sgu_ln_g'], 'm_sgu_ln_b': out['m_sgu_ln_b'], 'm_conv_w': out['m_conv_w'], 'm_pool_w': out['m_pool_w'], 'm_pool_scale': out['m_pool_scale'], 'm_w_out': out['m_w_out'], 'm_norm_ff_g': out['m_norm_ff_g'], 'm_w_ff1': out['m_w_ff1'], 'm_w_ff2': out['m_w_ff2'], 'm_norm_ple_g': out['m_norm_ple_g'], 'm_w_ple_gate': out['m_w_ple_gate'], 'm_w_ple_proj': out['m_w_ple_proj'], 'm_final_g': out['m_final_g'], 'v_norm_mix_g': out['v_norm_mix_g'], 'v_w_in': out['v_w_in'], 'v_sgu_w': out['v_sgu_w'], 'v_sgu_b': out['v_sgu_b'], 'v_sgu_ln_g': out['v_sgu_ln_g'], 'v_sgu_ln_b': out['v_sgu_ln_b'], 'v_conv_w': out['v_conv_w'], 'v_pool_w': out['v_pool_w'], 'v_pool_scale': out['v_pool_scale'], 'v_w_out': out['v_w_out'], 'v_norm_ff_g': out['v_norm_ff_g'], 'v_w_ff1': out['v_w_ff1'], 'v_w_ff2': out['v_w_ff2'], 'v_norm_ple_g': out['v_norm_ple_g'], 'v_w_ple_gate': out['v_w_ple_gate'], 'v_w_ple_proj': out['v_w_ple_proj'], 'v_final_g': out['v_final_g']}


def _loss(weights, diff, rest, loss_target):
    with _jax.named_scope("forward"):
        args = {**rest, TWIN_DIFF_INPUT: diff, **{k: w.astype(_WEIGHT_DTYPES[k]) for k, w in weights.items()}}
        y = _forward(args)
    with _jax.named_scope("loss_head"):
        err = _jnp.square(y.astype(_jnp.float32) - loss_target)
        return 0.5 * _jnp.sum(_jnp.mean(err, axis=-1)) if err.ndim else 0.5 * err


def _adamw(w, g, m, v):
    m = ADAM_B1 * m + (1.0 - ADAM_B1) * g
    v = ADAM_B2 * v + (1.0 - ADAM_B2) * _jnp.square(g)
    m_hat = m / (1.0 - ADAM_B1 ** ADAM_STEP)
    v_hat = v / (1.0 - ADAM_B2 ** ADAM_STEP)
    delta = -ADAM_LR * (m_hat / (_jnp.sqrt(v_hat) + ADAM_EPS) + ADAM_WD * w)
    return delta, m, v


def reference(x, p, norm_mix_g, w_in, sgu_w, sgu_b, sgu_ln_g, sgu_ln_b, conv_w, pool_w, pool_scale, w_out, norm_ff_g, w_ff1, w_ff2, norm_ple_g, w_ple_gate, w_ple_proj, final_g, loss_target, m_norm_mix_g, m_w_in, m_sgu_w, m_sgu_b, m_sgu_ln_g, m_sgu_ln_b, m_conv_w, m_pool_w, m_pool_scale, m_w_out, m_norm_ff_g, m_w_ff1, m_w_ff2, m_norm_ple_g, m_w_ple_gate, m_w_ple_proj, m_final_g, v_norm_mix_g, v_w_in, v_sgu_w, v_sgu_b, v_sgu_ln_g, v_sgu_ln_b, v_conv_w, v_pool_w, v_pool_scale, v_w_out, v_norm_ff_g, v_w_ff1, v_w_ff2, v_norm_ple_g, v_w_ple_gate, v_w_ple_proj, v_final_g):
    given = dict(x=x, p=p, norm_mix_g=norm_mix_g, w_in=w_in, sgu_w=sgu_w, sgu_b=sgu_b, sgu_ln_g=sgu_ln_g, sgu_ln_b=sgu_ln_b, conv_w=conv_w, pool_w=pool_w, pool_scale=pool_scale, w_out=w_out, norm_ff_g=norm_ff_g, w_ff1=w_ff1, w_ff2=w_ff2, norm_ple_g=norm_ple_g, w_ple_gate=w_ple_gate, w_ple_proj=w_ple_proj, final_g=final_g, loss_target=loss_target, m_norm_mix_g=m_norm_mix_g, m_w_in=m_w_in, m_sgu_w=m_sgu_w, m_sgu_b=m_sgu_b, m_sgu_ln_g=m_sgu_ln_g, m_sgu_ln_b=m_sgu_ln_b, m_conv_w=m_conv_w, m_pool_w=m_pool_w, m_pool_scale=m_pool_scale, m_w_out=m_w_out, m_norm_ff_g=m_norm_ff_g, m_w_ff1=m_w_ff1, m_w_ff2=m_w_ff2, m_norm_ple_g=m_norm_ple_g, m_w_ple_gate=m_w_ple_gate, m_w_ple_proj=m_w_ple_proj, m_final_g=m_final_g, v_norm_mix_g=v_norm_mix_g, v_w_in=v_w_in, v_sgu_w=v_sgu_w, v_sgu_b=v_sgu_b, v_sgu_ln_g=v_sgu_ln_g, v_sgu_ln_b=v_sgu_ln_b, v_conv_w=v_conv_w, v_pool_w=v_pool_w, v_pool_scale=v_pool_scale, v_w_out=v_w_out, v_norm_ff_g=v_norm_ff_g, v_w_ff1=v_w_ff1, v_w_ff2=v_w_ff2, v_norm_ple_g=v_norm_ple_g, v_w_ple_gate=v_w_ple_gate, v_w_ple_proj=v_w_ple_proj, v_final_g=v_final_g)
    weights = {n: given[n] for n in TWIN_WEIGHTS}
    shared = {n: given[n] for n in SHARED_INPUTS}
    per_example = {n: given[n] for n in ['x', 'p']}
    grad_fn = _jax.value_and_grad(_loss, argnums=(0, 1))

    def one_microbatch(ex, loss_target):
        ex = dict(ex)
        diff = ex.pop(TWIN_DIFF_INPUT)
        return grad_fn(weights, diff, {**shared, **ex}, loss_target)

    if N_MICROBATCH == 1:
        loss, (grad_w, grad_x) = one_microbatch(per_example, given["loss_target"])
    else:
        def body(carry, xs):
            loss_sum, grad_sum = carry
            l_k, (gw_k, gx_k) = one_microbatch(xs[0], xs[1])
            with _jax.named_scope("update"):
                return (loss_sum + l_k, _jax.tree.map(_jnp.add, grad_sum, gw_k)), gx_k

        init = (_jnp.zeros((), _jnp.float32), _jax.tree.map(_jnp.zeros_like, weights))
        (loss, grad_w), grad_x = _jax.lax.scan(body, init, (per_example, given["loss_target"]))
    with _jax.named_scope("update"):
        delta_w, new_m, new_v = {}, {}, {}
        for n in TWIN_WEIGHTS:
            delta_w[n], new_m[n], new_v[n] = _adamw(weights[n], grad_w[n], given["m_" + n], given["v_" + n])
    return (loss, grad_x, *[grad_w[n] for n in TWIN_WEIGHTS], *[delta_w[n] for n in TWIN_WEIGHTS],
            *[new_m[n] for n in TWIN_WEIGHTS], *[new_v[n] for n in TWIN_WEIGHTS])
```

```python
import functools
import math

import jax
import jax.numpy as jnp
from jax import lax
from jax.experimental import pallas as pl
from jax.experimental.pallas import tpu as pltpu

F32 = jnp.float32
BF16 = jnp.bfloat16

D_MODEL = 1024
DEPTH = 4
D_PLE = 256
D_FF = 4096
HEAD_DIM = 64
D_A = 384
D_B = 384
D_C = 256
D_IN = 2176
CHUNK = 128
HALO = 16
RMS_EPS = 1e-6
LN_EPS = 1e-5
N_CHIPS = 4
LANES = 1024

ADAM_LR = 0.001
ADAM_B1 = 0.9
ADAM_B2 = 0.999
ADAM_EPS = 1e-08
ADAM_WD = 0.01
ADAM_STEP = 10

VMEM_LIMIT_BYTES = 60 * 1024 * 1024

_RSQRT2 = 0.7071067811865476
_INV_SQRT_2PI = 0.3989422804014327

BIG = (
    ("w_in", D_MODEL, D_IN, 1),
    ("w_out", D_MODEL, D_MODEL, 0),
    ("w_ff1", D_MODEL, D_FF, 1),
    ("w_ff2", D_FF, D_MODEL, 0),
    ("w_ple_gate", D_MODEL, D_MODEL, 0),
    ("w_ple_proj", D_PLE, D_MODEL, 1),
)
SMALL = ("norm_mix_g", "sgu_w", "sgu_b", "sgu_ln_g", "sgu_ln_b", "conv_w", "pool_w", "pool_scale",
         "norm_ff_g", "norm_ple_g", "final_g")
WEIGHTS = ("norm_mix_g", "w_in", "sgu_w", "sgu_b", "sgu_ln_g", "sgu_ln_b", "conv_w", "pool_w", "pool_scale",
           "w_out", "norm_ff_g", "w_ff1", "w_ff2", "norm_ple_g", "w_ple_gate", "w_ple_proj", "final_g")


def _dot(a, b):
    return jnp.dot(a, b, preferred_element_type=F32)


def _dot_nt(a, b):
    return lax.dot_general(a, b, (((1,), (1,)), ((), ())), preferred_element_type=F32)


def _dot_tn(a, b):
    return lax.dot_general(a, b, (((0,), (0,)), ((), ())), preferred_element_type=F32)


def _const_spec(shape):
    nd = len(shape)
    return pl.BlockSpec(shape, lambda i: (0,) * nd, pipeline_mode=pl.Buffered(1))


def _acc_spec(shape):
    nd = len(shape)
    return pl.BlockSpec(shape, lambda i: (0,) * nd)


def _params(*sem):
    return pltpu.CompilerParams(dimension_semantics=sem, vmem_limit_bytes=VMEM_LIMIT_BYTES)


def _rms(x, g):
    rs = lax.rsqrt(jnp.mean(x * x, axis=-1, keepdims=True) + RMS_EPS)
    return x * rs, rs


def _rms_bwd(dh, n, rs, g):
    dn = dh * g
    return rs * (dn - n * jnp.mean(dn * n, axis=-1, keepdims=True))


def _gelu(x):
    return x * (0.5 * (1.0 + lax.erf(x * _RSQRT2)))


def _gelu_and_grad(x):
    cdf = 0.5 * (1.0 + lax.erf(x * _RSQRT2))
    return x * cdf, cdf + x * (jnp.exp(-0.5 * x * x) * _INV_SQRT_2PI)


def _group_mean(v, avg):
    hi = v.astype(BF16)
    lo = (v - hi.astype(F32)).astype(BF16)
    return _dot(hi, avg) + _dot(lo, avg)


def _lane_lt(shape, bound):
    return lax.broadcasted_iota(jnp.int32, shape, 1) < bound


def _sgu_mix(vnb2, wcat_j, lo_mask):
    zero = jnp.zeros_like(vnb2)
    stacked = jnp.concatenate([jnp.where(lo_mask, vnb2, zero), jnp.where(lo_mask, zero, vnb2)], axis=0)
    return _dot(wcat_j, stacked)


def _pool_means(ext, tile_rows, first_pos):
    s2 = ext + pltpu.roll(ext, 1, 0)
    s4 = s2 + pltpu.roll(s2, 2, 0)
    s8 = s4 + pltpu.roll(s4, 4, 0)
    s16 = s8 + pltpu.roll(s8, 8, 0)
    pos = (first_pos + lax.broadcasted_iota(jnp.int32, (tile_rows, 1), 0) + 1).astype(F32)
    lane = lax.broadcasted_iota(jnp.int32, (tile_rows, D_C), 1)
    sums = jnp.where(lane < 64, s2[HALO:], jnp.where(lane < 128, s4[HALO:], jnp.where(lane < 192, s8[HALO:], s16[HALO:])))
    inv = _pool_inv_counts(pos, lane)
    return sums * inv, inv


def _pool_inv_counts(pos, lane):
    win = jnp.where(lane < 64, 2.0, jnp.where(lane < 128, 4.0, jnp.where(lane < 192, 8.0, 16.0)))
    return 1.0 / jnp.minimum(pos, win)


def mix_fwd(x, g_mix, w_in, w_out, wcat, bmat, ln_g, ln_b, avg, conv_w, pool_bd, pool_scale, *, tm):
    t = x.shape[0]
    nt = t // tm

    def body(x_ref, g_ref, win_ref, wout_ref, wcat_ref, bmat_ref, lng_ref, lnb_ref, avg_ref, cw_ref, pw_ref, ps_ref,
             proj_ref, ycat_ref, x1_ref, hbuf, zbuf):
        i = pl.program_id(0)

        @pl.when(i == 0)
        def _():
            hbuf[0:HALO, :] = jnp.zeros((HALO, D_B), F32)
            zbuf[0:HALO, :] = jnp.zeros((HALO, D_C), F32)

        xv = x_ref[...]
        n, _ = _rms(xv, None)
        h1 = (n * g_ref[...]).astype(BF16)
        proj_ref[...] = _dot(h1, win_ref[...])

        lo_mask = _lane_lt((CHUNK, CHUNK), HEAD_DIM)
        avg = avg_ref[...]
        for c in range(tm // CHUNK):
            rows = pl.ds(c * CHUNK, CHUNK)
            gu = _gelu(proj_ref[rows, 0:D_A])
            gv = _gelu(proj_ref[rows, D_A:2 * D_A])
            dv = gv - _group_mean(gv, avg)
            var = _group_mean(dv * dv, avg)
            vnb = (dv * lax.rsqrt(var + LN_EPS) * lng_ref[...] + lnb_ref[...]).astype(BF16)
            for j in range(3):
                cols = slice(j * CHUNK, (j + 1) * CHUNK)
                mixed = _sgu_mix(vnb[:, cols], wcat_ref[j], lo_mask) + bmat_ref[:, cols]
                ycat_ref[rows, cols] = (gu[:, cols] * mixed).astype(BF16)

        o = 2 * D_A
        hcur = proj_ref[:, o + 2 * D_B:o + 3 * D_B] * proj_ref[:, o:o + D_B]
        hbuf[HALO:HALO + tm, :] = hcur
        y = (cw_ref[2:3, :] * hcur + cw_ref[1:2, :] * hbuf[pl.ds(HALO - 1, tm), :]
             + cw_ref[0:1, :] * hbuf[pl.ds(HALO - 2, tm), :])
        ycat_ref[:, D_A:D_A + D_B] = (proj_ref[:, o + D_B:o + 2 * D_B] * y).astype(BF16)
        hbuf[0:HALO, :] = hbuf[tm:tm + HALO, :]

        zc = proj_ref[:, o + 3 * D_B:D_IN]
        zbuf[HALO:HALO + tm, :] = zc
        mean, _ = _pool_means(zbuf[...], tm, i * tm)
        pooled = (mean - zc).astype(BF16)
        ycat_ref[:, D_A + D_B:D_MODEL] = (_dot(pooled, pw_ref[...]) * ps_ref[...]).astype(BF16)
        zbuf[0:HALO, :] = zbuf[tm:tm + HALO, :]

        x1_ref[...] = xv + _dot(ycat_ref[...], wout_ref[...])

    row = lambda w: pl.BlockSpec((tm, w), lambda i: (i, 0))
    return pl.pallas_call(
        body, name="mix_fwd", grid=(nt,),
        in_specs=[row(D_MODEL), _const_spec((1, D_MODEL)), _const_spec((D_MODEL, D_IN)), _const_spec((D_MODEL, D_MODEL)),
                  _const_spec((3, CHUNK, 2 * CHUNK)), _const_spec((CHUNK, D_A)), _const_spec((1, D_A)), _const_spec((1, D_A)),
                  _const_spec((D_A, D_A)), _const_spec((8, D_B)), _const_spec((D_C, D_C)), _const_spec((1, D_C))],
        out_specs=[row(D_IN), row(D_MODEL), row(D_MODEL)],
        out_shape=[jax.ShapeDtypeStruct((t, D_IN), F32), jax.ShapeDtypeStruct((t, D_MODEL), BF16),
                   jax.ShapeDtypeStruct((t, D_MODEL), F32)],
        scratch_shapes=[pltpu.VMEM((tm + HALO, D_B), F32), pltpu.VMEM((tm + HALO, D_C), F32)],
        compiler_params=_params("arbitrary"),
    )(x, g_mix, w_in, w_out, wcat, bmat, ln_g, ln_b, avg, conv_w, pool_bd, pool_scale)


def ffn_ple_fwd(x1, p, g_ff, w_ff1, w_ff2, g_ple, w_gate, w_proj, *, tm):
    t = x1.shape[0]
    nt = t // tm
    nc = D_FF // D_MODEL

    def body(x1_ref, p_ref, gff_ref, w1_ref, w2_ref, gple_ref, wg_ref, wp_ref, a_ref, x2_ref, x3_ref):
        x1v = x1_ref[...]
        n2 = x1v * lax.rsqrt(jnp.mean(x1v * x1v, axis=-1, keepdims=True) + RMS_EPS)
        h2 = (n2 * gff_ref[...]).astype(BF16)
        acc = x1v
        for c in range(nc):
            cols = slice(c * D_MODEL, (c + 1) * D_MODEL)
            a = _dot(h2, w1_ref[:, cols])
            a_ref[:, cols] = a.astype(BF16)
            ra = jnp.maximum(a, 0.0)
            acc = acc + _dot((ra * ra).astype(BF16), w2_ref[cols, :])
        x2_ref[...] = acc
        n3 = acc * lax.rsqrt(jnp.mean(acc * acc, axis=-1, keepdims=True) + RMS_EPS)
        h3 = (n3 * gple_ref[...]).astype(BF16)
        gate = jax.nn.sigmoid(_dot(h3, wg_ref[...]))
        pp = _dot(p_ref[...].astype(BF16), wp_ref[...])
        x3_ref[...] = acc + pp * gate

    row = lambda w: pl.BlockSpec((tm, w), lambda i: (i, 0))
    return pl.pallas_call(
        body, name="ffn_ple_fwd", grid=(nt,),
        in_specs=[row(D_MODEL), row(D_PLE), _const_spec((1, D_MODEL)), _const_spec((D_MODEL, D_FF)),
                  _const_spec((D_FF, D_MODEL)), _const_spec((1, D_MODEL)), _const_spec((D_MODEL, D_MODEL)),
                  _const_spec((D_PLE, D_MODEL))],
        out_specs=[row(D_FF), row(D_MODEL), row(D_MODEL)],
        out_shape=[jax.ShapeDtypeStruct((t, D_FF), BF16), jax.ShapeDtypeStruct((t, D_MODEL), F32),
                   jax.ShapeDtypeStruct((t, D_MODEL), F32)],
        compiler_params=_params("parallel"),
    )(x1, p, g_ff, w_ff1, w_ff2, g_ple, w_gate, w_proj)


def loss_head(x, target, g, *, tm):
    t = x.shape[0]
    nt = t // tm

    def body(x_ref, t_ref, g_ref, loss_ref, dg_ref, dx_ref, sq_acc):
        i = pl.program_id(0)

        @pl.when(i == 0)
        def _():
            sq_acc[...] = jnp.zeros_like(sq_acc)
            dg_ref[...] = jnp.zeros_like(dg_ref)

        xv = x_ref[...]
        rs = lax.rsqrt(jnp.mean(xv * xv, axis=-1, keepdims=True) + RMS_EPS)
        n = xv * rs
        gv = g_ref[...]
        err = n * gv - t_ref[...]
        sq_acc[...] += jnp.sum(err * err, axis=0, keepdims=True)
        dy = err * (1.0 / D_MODEL)
        dg_ref[...] += jnp.sum(dy * n, axis=0, keepdims=True)
        dx_ref[...] = _rms_bwd(dy, n, rs, gv)

        @pl.when(i == nt - 1)
        def _():
            total = jnp.sum(sq_acc[...], axis=1, keepdims=True) * (0.5 / D_MODEL)
            loss_ref[...] = jnp.broadcast_to(total, loss_ref.shape)

    row = pl.BlockSpec((tm, D_MODEL), lambda i: (i, 0))
    return pl.pallas_call(
        body, name="loss_head", grid=(nt,),
        in_specs=[row, row, _const_spec((1, D_MODEL))],
        out_specs=[_acc_spec((8, 128)), _acc_spec((1, D_MODEL)), row],
        out_shape=[jax.ShapeDtypeStruct((8, 128), F32), jax.ShapeDtypeStruct((1, D_MODEL), F32),
                   jax.ShapeDtypeStruct((t, D_MODEL), F32)],
        scratch_shapes=[pltpu.VMEM((1, D_MODEL), F32)],
        compiler_params=_params("arbitrary"),
    )(x, target, g)


def ple_bwd(d, x2, p, g_ple, w_gate, w_gate_t, w_proj, *, tm):
    t = d.shape[0]
    nt = t // tm

    def body(d_ref, x2_ref, p_ref, g_ref, wg_ref, wgt_ref, wp_ref, dx2_ref, h3_ref, dpre_ref, dpp_ref, dg_ref):
        i = pl.program_id(0)

        @pl.when(i == 0)
        def _():
            dg_ref[...] = jnp.zeros_like(dg_ref)

        dv = d_ref[...]
        x2v = x2_ref[...]
        rs = lax.rsqrt(jnp.mean(x2v * x2v, axis=-1, keepdims=True) + RMS_EPS)
        n3 = x2v * rs
        gv = g_ref[...]
        h3 = (n3 * gv).astype(BF16)
        h3_ref[...] = h3
        gate = jax.nn.sigmoid(_dot(h3, wg_ref[...]))
        pp = _dot(p_ref[...].astype(BF16), wp_ref[...])
        dpp_ref[...] = (dv * gate).astype(BF16)
        dpre = (dv * pp * gate * (1.0 - gate)).astype(BF16)
        dpre_ref[...] = dpre
        dh3 = _dot(dpre, wgt_ref[...])
        dg_ref[...] += jnp.sum(dh3 * n3, axis=0, keepdims=True)
        dx2_ref[...] = dv + _rms_bwd(dh3, n3, rs, gv)

    row = lambda w: pl.BlockSpec((tm, w), lambda i: (i, 0))
    return pl.pallas_call(
        body, name="ple_bwd", grid=(nt,),
        in_specs=[row(D_MODEL), row(D_MODEL), row(D_PLE), _const_spec((1, D_MODEL)), _const_spec((D_MODEL, D_MODEL)),
                  _const_spec((D_MODEL, D_MODEL)), _const_spec((D_PLE, D_MODEL))],
        out_specs=[row(D_MODEL), row(D_MODEL), row(D_MODEL), row(D_MODEL), _acc_spec((1, D_MODEL))],
        out_shape=[jax.ShapeDtypeStruct((t, D_MODEL), F32), jax.ShapeDtypeStruct((t, D_MODEL), BF16),
                   jax.ShapeDtypeStruct((t, D_MODEL), BF16), jax.ShapeDtypeStruct((t, D_MODEL), BF16),
                   jax.ShapeDtypeStruct((1, D_MODEL), F32)],
        compiler_params=_params("arbitrary"),
    )(d, x2, p, g_ple, w_gate, w_gate_t, w_proj)


def ffn_bwd(dx2, x1, a, g_ff, w_ff1_t, w_ff2_t, *, tm):
    t = dx2.shape[0]
    nt = t // tm
    nc = D_FF // D_MODEL

    def body(dx2_ref, x1_ref, a_ref, g_ref, w1t_ref, w2t_ref, dx1_ref, h2_ref, r_ref, da_ref, dg_ref):
        i = pl.program_id(0)

        @pl.when(i == 0)
        def _():
            dg_ref[...] = jnp.zeros_like(dg_ref)

        dv = dx2_ref[...]
        x1v = x1_ref[...]
        rs = lax.rsqrt(jnp.mean(x1v * x1v, axis=-1, keepdims=True) + RMS_EPS)
        n2 = x1v * rs
        gv = g_ref[...]
        h2_ref[...] = (n2 * gv).astype(BF16)
        dvb = dv.astype(BF16)
        dh2 = jnp.zeros((tm, D_MODEL), F32)
        for c in range(nc):
            cols = slice(c * D_MODEL, (c + 1) * D_MODEL)
            ra = jnp.maximum(a_ref[:, cols].astype(F32), 0.0)
            r_ref[:, cols] = (ra * ra).astype(BF16)
            da = (_dot(dvb, w2t_ref[:, cols]) * (2.0 * ra)).astype(BF16)
            da_ref[:, cols] = da
            dh2 = dh2 + _dot(da, w1t_ref[cols, :])
        dg_ref[...] += jnp.sum(dh2 * n2, axis=0, keepdims=True)
        dx1_ref[...] = dv + _rms_bwd(dh2, n2, rs, gv)

    row = lambda w: pl.BlockSpec((tm, w), lambda i: (i, 0))
    return pl.pallas_call(
        body, name="ffn_bwd", grid=(nt,),
        in_specs=[row(D_MODEL), row(D_MODEL), row(D_FF), _const_spec((1, D_MODEL)), _const_spec((D_FF, D_MODEL)),
                  _const_spec((D_MODEL, D_FF))],
        out_specs=[row(D_MODEL), row(D_MODEL), row(D_FF), row(D_FF), _acc_spec((1, D_MODEL))],
        out_shape=[jax.ShapeDtypeStruct((t, D_MODEL), F32), jax.ShapeDtypeStruct((t, D_MODEL), BF16),
                   jax.ShapeDtypeStruct((t, D_FF), BF16), jax.ShapeDtypeStruct((t, D_FF), BF16),
                   jax.ShapeDtypeStruct((1, D_MODEL), F32)],
        compiler_params=_params("arbitrary"),
    )(dx2, x1, a, g_ff, w_ff1_t, w_ff2_t)


def mix_bwd(dx1, x, proj, g_mix, w_in_t, w_out_t, wcat, wcat_t, bmat, ln_g, ln_b, avg, conv_w, pool_bd, pool_bd_t,
            pool_scale, *, tm):
    t = dx1.shape[0]
    nt = t // tm
    prev_blocks = tm // HALO

    def body(dx1_ref, x_ref, proj_ref, prev_ref, g_ref, wint_ref, woutt_ref, wcat_ref, wcatt_ref, bmat_ref, lng_ref,
             lnb_ref, avg_ref, cw_ref, pw_ref, pwt_ref, ps_ref,
             dx_ref, h1_ref, dproj_ref, dg_ref, dws_ref, dbm_ref, dlng_ref, dlnb_ref, dcw_ref, dpw_ref, dps_ref,
             dyc, dpj, hbuf, zbuf, dybuf, qbuf):
        i = pl.program_id(0)
        ti = nt - 1 - i

        @pl.when(i == 0)
        def _():
            for ref in (dg_ref, dws_ref, dbm_ref, dlng_ref, dlnb_ref, dcw_ref, dpw_ref, dps_ref):
                ref[...] = jnp.zeros_like(ref)
            dybuf[tm:tm + HALO, :] = jnp.zeros((HALO, D_B), F32)
            qbuf[tm:tm + HALO, :] = jnp.zeros((HALO, D_C), F32)

        dx1v = dx1_ref[...]
        dyc[...] = _dot(dx1v.astype(BF16), woutt_ref[...])

        lo_mask = _lane_lt((CHUNK, CHUNK), HEAD_DIM)
        avg = avg_ref[...]
        lng = lng_ref[...]
        for c in range(tm // CHUNK):
            rows = pl.ds(c * CHUNK, CHUNK)
            gu, dgu = _gelu_and_grad(proj_ref[rows, 0:D_A])
            gv, dgv = _gelu_and_grad(proj_ref[rows, D_A:2 * D_A])
            cen = gv - _group_mean(gv, avg)
            rstd = lax.rsqrt(_group_mean(cen * cen, avg) + LN_EPS)
            vhat = cen * rstd
            vnb = (vhat * lng + lnb_ref[...]).astype(BF16)
            dya = dyc[rows, 0:D_A]
            dvn_parts = []
            for j in range(3):
                cols = slice(j * CHUNK, (j + 1) * CHUNK)
                vnb2 = vnb[:, cols]
                mixed = _sgu_mix(vnb2, wcat_ref[j], lo_mask) + bmat_ref[:, cols]
                dya2 = dya[:, cols]
                dpj[rows, cols] = dya2 * mixed * dgu[:, cols]
                dm = dya2 * gu[:, cols]
                dbm_ref[:, cols] += dm
                dmb = dm.astype(BF16)
                zero = jnp.zeros_like(dmb)
                dm_st = jnp.concatenate([jnp.where(lo_mask, dmb, zero), jnp.where(lo_mask, zero, dmb)], axis=0)
                dws_ref[j] += _dot_nt(dm_st, vnb2)
                dvn_st = _dot(wcatt_ref[j], dmb)
                dvn_parts.append(jnp.where(lo_mask, dvn_st[0:CHUNK], dvn_st[CHUNK:2 * CHUNK]))
            dvn = jnp.concatenate(dvn_parts, axis=1)
            dlng_ref[...] += jnp.sum(dvn * vhat, axis=0, keepdims=True)
            dlnb_ref[...] += jnp.sum(dvn, axis=0, keepdims=True)
            dvh = dvn * lng
            dgv_in = rstd * (dvh - _group_mean(dvh, avg) - vhat * _group_mean(dvh * vhat, avg))
            dpj[rows, D_A:2 * D_A] = dgv_in * dgv

        o = 2 * D_A
        live = (ti > 0).astype(F32)
        zb = proj_ref[:, o:o + D_B]
        gb = proj_ref[:, o + D_B:o + 2 * D_B]
        gc = proj_ref[:, o + 2 * D_B:o + 3 * D_B]
        hcur = gc * zb
        hbuf[0:HALO, :] = prev_ref[:, o + 2 * D_B:o + 3 * D_B] * prev_ref[:, o:o + D_B] * live
        hbuf[HALO:HALO + tm, :] = hcur
        hm1 = hbuf[pl.ds(HALO - 1, tm), :]
        hm2 = hbuf[pl.ds(HALO - 2, tm), :]
        y = cw_ref[2:3, :] * hcur + cw_ref[1:2, :] * hm1 + cw_ref[0:1, :] * hm2
        dout = dyc[:, D_A:D_A + D_B]
        dpj[:, o + D_B:o + 2 * D_B] = dout * y
        dy = dout * gb
        dcw_ref[2:3, :] += jnp.sum(dy * hcur, axis=0, keepdims=True)
        dcw_ref[1:2, :] += jnp.sum(dy * hm1, axis=0, keepdims=True)
        dcw_ref[0:1, :] += jnp.sum(dy * hm2, axis=0, keepdims=True)
        dybuf[0:tm, :] = dy
        dh = (cw_ref[2:3, :] * dy + cw_ref[1:2, :] * dybuf[pl.ds(1, tm), :] + cw_ref[0:1, :] * dybuf[pl.ds(2, tm), :])
        dybuf[tm:tm + HALO, :] = dybuf[0:HALO, :]
        dpj[:, o:o + D_B] = dh * gc
        dpj[:, o + 2 * D_B:o + 3 * D_B] = dh * zb

        zc = proj_ref[:, o + 3 * D_B:D_IN]
        zbuf[0:HALO, :] = prev_ref[:, o + 3 * D_B:D_IN] * live
        zbuf[HALO:HALO + tm, :] = zc
        mean, inv = _pool_means(zbuf[...], tm, ti * tm)
        pooled = (mean - zc).astype(BF16)
        dyp = dyc[:, D_A + D_B:D_MODEL]
        ps = ps_ref[...]
        dps_ref[...] += jnp.sum(dyp * _dot(pooled, pw_ref[...]), axis=0, keepdims=True)
        dpw = (dyp * ps).astype(BF16)
        dpw_ref[...] += _dot_tn(pooled, dpw)
        dpooled = _dot(dpw, pwt_ref[...])
        qbuf[0:tm, :] = dpooled * inv
        q = qbuf[...]
        nrows = tm + HALO
        f2 = q + pltpu.roll(q, nrows - 1, 0)
        f4 = f2 + pltpu.roll(f2, nrows - 2, 0)
        f8 = f4 + pltpu.roll(f4, nrows - 4, 0)
        f16 = f8 + pltpu.roll(f8, nrows - 8, 0)
        lane = lax.broadcasted_iota(jnp.int32, (tm, D_C), 1)
        ahead = jnp.where(lane < 64, f2[0:tm], jnp.where(lane < 128, f4[0:tm], jnp.where(lane < 192, f8[0:tm], f16[0:tm])))
        dpj[:, o + 3 * D_B:D_IN] = ahead - dpooled
        qbuf[tm:tm + HALO, :] = qbuf[0:HALO, :]

        dprojb = dpj[...].astype(BF16)
        dproj_ref[...] = dprojb
        dh1 = _dot(dprojb, wint_ref[...])
        xv = x_ref[...]
        rs = lax.rsqrt(jnp.mean(xv * xv, axis=-1, keepdims=True) + RMS_EPS)
        n1 = xv * rs
        gv1 = g_ref[...]
        h1_ref[...] = (n1 * gv1).astype(BF16)
        dg_ref[...] += jnp.sum(dh1 * n1, axis=0, keepdims=True)
        dx_ref[...] = dx1v + _rms_bwd(dh1, n1, rs, gv1)

        @pl.when(i == nt - 1)
        def _():
            tril = (lax.broadcasted_iota(jnp.int32, (2 * CHUNK, CHUNK), 0) % CHUNK
                    >= lax.broadcasted_iota(jnp.int32, (2 * CHUNK, CHUNK), 1))
            for j in range(3):
                dws_ref[j] = jnp.where(tril, dws_ref[j], 0.0)
            dbm_ref[...] = _group_mean(dbm_ref[...], avg) * float(HEAD_DIM)

    rev = lambda w: pl.BlockSpec((tm, w), lambda i: (nt - 1 - i, 0))
    prev = pl.BlockSpec((HALO, D_IN), lambda i: (jnp.maximum((nt - 1 - i) * prev_blocks - 1, 0), 0))
    acc_shapes = [(1, D_MODEL), (3, 2 * CHUNK, CHUNK), (CHUNK, D_A), (1, D_A), (1, D_A), (8, D_B), (D_C, D_C), (1, D_C)]
    return pl.pallas_call(
        body, name="mix_bwd", grid=(nt,),
        in_specs=[rev(D_MODEL), rev(D_MODEL), rev(D_IN), prev, _const_spec((1, D_MODEL)), _const_spec((D_IN, D_MODEL)),
                  _const_spec((D_MODEL, D_MODEL)), _const_spec((3, CHUNK, 2 * CHUNK)), _const_spec((3, 2 * CHUNK, CHUNK)),
                  _const_spec((CHUNK, D_A)), _const_spec((1, D_A)), _const_spec((1, D_A)), _const_spec((D_A, D_A)),
                  _const_spec((8, D_B)), _const_spec((D_C, D_C)), _const_spec((D_C, D_C)), _const_spec((1, D_C))],
        out_specs=[rev(D_MODEL), rev(D_MODEL), rev(D_IN)] + [_acc_spec(s) for s in acc_shapes],
        out_shape=[jax.ShapeDtypeStruct((t, D_MODEL), F32), jax.ShapeDtypeStruct((t, D_MODEL), BF16),
                   jax.ShapeDtypeStruct((t, D_IN), BF16)] + [jax.ShapeDtypeStruct(s, F32) for s in acc_shapes],
        scratch_shapes=[pltpu.VMEM((tm, D_MODEL), F32), pltpu.VMEM((tm, D_IN), F32),
                        pltpu.VMEM((tm + HALO, D_B), F32), pltpu.VMEM((tm + HALO, D_C), F32),
                        pltpu.VMEM((tm + HALO, D_B), F32), pltpu.VMEM((tm + HALO, D_C), F32)],
        compiler_params=_params("arbitrary"),
    )(dx1, x, proj, proj, g_mix, w_in_t, w_out_t, wcat, wcat_t, bmat, ln_g, ln_b, avg, conv_w, pool_bd, pool_bd_t,
      pool_scale)


def wgrad(a, b, *, tk):
    t, m = a.shape
    n = b.shape[1]
    bm = min(m, 1024)
    bn = 1024 if n % 1024 == 0 else n
    nk = t // tk

    def body(a_ref, b_ref, o_ref):
        k = pl.program_id(2)

        @pl.when(k == 0)
        def _():
            o_ref[...] = jnp.zeros_like(o_ref)

        o_ref[...] += _dot_tn(a_ref[...].astype(BF16), b_ref[...].astype(BF16))

    return pl.pallas_call(
        body, name=f"wgrad_{m}x{n}", grid=(m // bm, n // bn, nk),
        in_specs=[pl.BlockSpec((tk, bm), lambda i, j, k: (k, i)), pl.BlockSpec((tk, bn), lambda i, j, k: (k, j))],
        out_specs=pl.BlockSpec((bm, bn), lambda i, j, k: (i, j)),
        out_shape=jax.ShapeDtypeStruct((m, n), F32),
        compiler_params=_params("parallel", "parallel", "arbitrary"),
    )(a, b)


def _row_block(rows, cols):
    target = max(8, (256 * 1024) // cols)
    if rows <= target:
        return rows
    best = rows
    for br in range(8, target + 1, 8):
        if rows % br == 0:
            best = br
    return best


def adamw(w, g, m, v):
    shape = w.shape
    cols = shape[-1]
    rows = math.prod(shape[:-1]) if len(shape) > 1 else 1
    br = _row_block(rows, cols)

    def body(w_ref, g_ref, m_ref, v_ref, d_ref, nm_ref, nv_ref):
        gv = g_ref[...]
        nm = ADAM_B1 * m_ref[...] + (1.0 - ADAM_B1) * gv
        nv = ADAM_B2 * v_ref[...] + (1.0 - ADAM_B2) * jnp.square(gv)
        m_hat = nm / (1.0 - ADAM_B1 ** ADAM_STEP)
        v_hat = nv / (1.0 - ADAM_B2 ** ADAM_STEP)
        d_ref[...] = -ADAM_LR * (m_hat / (jnp.sqrt(v_hat) + ADAM_EPS) + ADAM_WD * w_ref[...])
        nm_ref[...] = nm
        nv_ref[...] = nv

    spec = pl.BlockSpec((br, cols), lambda i: (i, 0))
    outs = pl.pallas_call(
        body, name="adamw", grid=(rows // br,),
        in_specs=[spec] * 4, out_specs=[spec] * 3,
        out_shape=[jax.ShapeDtypeStruct((rows, cols), F32)] * 3,
        compiler_params=pltpu.CompilerParams(dimension_semantics=("parallel",)),
    )(*(a.reshape(rows, cols) for a in (w, g, m, v)))
    return tuple(o.reshape(shape) for o in outs)


MESH = pl.DeviceIdType.MESH
_ANY = pl.BlockSpec(memory_space=pl.ANY)


def _place():
    return lax.axis_index("x"), lax.axis_index("y"), lax.axis_index("c")


def chip_allgather(arrs):
    n = len(arrs)

    def body(*refs):
        ins, outs = refs[:n], refs[n:2 * n]
        send_sems, recv_sems, local_sems = refs[2 * n:]
        x, y, c = _place()
        me = 2 * x + y
        peers = [(1 - x, y), (x, 1 - y), (1 - x, 1 - y)]
        local, sent = [], []
        for k in range(n):
            own = pltpu.make_async_copy(ins[k], outs[k].at[me], local_sems.at[k])
            own.start()
            local.append(own)
            for j, (px, py) in enumerate(peers):
                cp = pltpu.make_async_remote_copy(src_ref=ins[k], dst_ref=outs[k].at[me], send_sem=send_sems.at[k, j],
                                                  recv_sem=recv_sems.at[k, j], device_id=(px, py, c), device_id_type=MESH)
                cp.start()
                sent.append(cp)
        for k in range(n):
            for j, (px, py) in enumerate(peers):
                pltpu.make_async_remote_copy(src_ref=ins[k], dst_ref=outs[k].at[2 * px + py], send_sem=send_sems.at[k, j],
                                             recv_sem=recv_sems.at[k, j], device_id=(px, py, c),
                                             device_id_type=MESH).wait_recv()
        for cp in sent:
            cp.wait_send()
        for own in local:
            own.wait()

    return pl.pallas_call(
        body, name="chip_allgather",
        in_specs=[_ANY] * n, out_specs=[_ANY] * n,
        out_shape=[jax.ShapeDtypeStruct((N_CHIPS,) + a.shape, a.dtype) for a in arrs],
        scratch_shapes=[pltpu.SemaphoreType.DMA((n, 3)), pltpu.SemaphoreType.DMA((n, 3)), pltpu.SemaphoreType.DMA((n,))],
        compiler_params=pltpu.CompilerParams(has_side_effects=True),
    )(*arrs)


def sibling_swap(a, src_half=None):
    shape = a.shape[1:] if src_half else a.shape

    def body(a_ref, o_ref, send_sem, recv_sem):
        x, y, c = _place()
        src = a_ref.at[1 - c] if src_half else a_ref
        cp = pltpu.make_async_remote_copy(src_ref=src, dst_ref=o_ref, send_sem=send_sem, recv_sem=recv_sem,
                                          device_id=(x, y, 1 - c), device_id_type=MESH)
        cp.start()
        cp.wait()

    return pl.pallas_call(
        body, name="sibling_swap_half" if src_half else "sibling_swap",
        in_specs=[_ANY], out_specs=_ANY, out_shape=jax.ShapeDtypeStruct(shape, a.dtype),
        scratch_shapes=[pltpu.SemaphoreType.DMA, pltpu.SemaphoreType.DMA],
        compiler_params=pltpu.CompilerParams(has_side_effects=True),
    )(a)


def chip_scatter(s):
    def body(s_ref, o_ref, send_sems, recv_sems, local_sem):
        x, y, c = _place()
        me = 2 * x + y
        peers = [(1 - x, y), (x, 1 - y), (1 - x, 1 - y)]
        own = pltpu.make_async_copy(s_ref.at[me], o_ref.at[me], local_sem)
        own.start()
        cps = []
        for j, (px, py) in enumerate(peers):
            cp = pltpu.make_async_remote_copy(src_ref=s_ref.at[2 * px + py], dst_ref=o_ref.at[me], send_sem=send_sems.at[j],
                                              recv_sem=recv_sems.at[j], device_id=(px, py, c), device_id_type=MESH)
            cp.start()
            cps.append(cp)
        for j, (px, py) in enumerate(peers):
            pltpu.make_async_remote_copy(src_ref=s_ref.at[me], dst_ref=o_ref.at[2 * px + py], send_sem=send_sems.at[j],
                                         recv_sem=recv_sems.at[j], device_id=(px, py, c), device_id_type=MESH).wait_recv()
        for cp in cps:
            cp.wait_send()
        own.wait()

    return pl.pallas_call(
        body, name="chip_scatter",
        in_specs=[_ANY], out_specs=_ANY, out_shape=jax.ShapeDtypeStruct(s.shape, s.dtype),
        scratch_shapes=[pltpu.SemaphoreType.DMA((3,)), pltpu.SemaphoreType.DMA((3,)), pltpu.SemaphoreType.DMA],
        compiler_params=pltpu.CompilerParams(has_side_effects=True),
    )(s)


def add_own_half(g, other, c_idx, *, br):
    _, nchip, rows, cols = g.shape

    def body(c_ref, g_ref, o_ref, out_ref):
        del c_ref
        out_ref[...] = g_ref[...] + o_ref[...]

    spec = pl.BlockSpec((None, br, cols), lambda j, i, c_ref: (j, i, 0))
    return pl.pallas_call(
        body, name="add_own_half",
        grid_spec=pltpu.PrefetchScalarGridSpec(
            num_scalar_prefetch=1, grid=(nchip, rows // br),
            in_specs=[pl.BlockSpec((None, None, br, cols), lambda j, i, c_ref: (c_ref[0], j, i, 0)), spec],
            out_specs=spec),
        out_shape=jax.ShapeDtypeStruct((nchip, rows, cols), F32),
        compiler_params=pltpu.CompilerParams(dimension_semantics=("parallel", "parallel")),
    )(c_idx, g, other)


def add_chips(parts, *, br):
    _, rows, cols = parts.shape

    def body(p_ref, out_ref):
        out_ref[...] = ((p_ref[0] + p_ref[1]) + p_ref[2]) + p_ref[3]

    return pl.pallas_call(
        body, name="add_chips", grid=(rows // br,),
        in_specs=[pl.BlockSpec((N_CHIPS, br, cols), lambda i: (0, i, 0))],
        out_specs=pl.BlockSpec((br, cols), lambda i: (i, 0)),
        out_shape=jax.ShapeDtypeStruct((rows, cols), F32),
        compiler_params=pltpu.CompilerParams(dimension_semantics=("parallel",)),
    )(parts)


def _shard_dims(k, n, axis):
    return (k // N_CHIPS, n) if axis == 0 else (k, n // N_CHIPS)


def _pack_weight_shards(shards):
    return jnp.concatenate([shards[name].astype(BF16).reshape(-1, LANES) for name, _, _, _ in BIG], axis=0)


def _unpack_gathered_weights(packed):
    out, off = {}, 0
    for name, k, n, axis in BIG:
        ks, ns = _shard_dims(k, n, axis)
        rows = DEPTH * ks * ns // LANES
        seg = packed[:, off:off + rows].reshape(N_CHIPS, DEPTH, ks, ns)
        off += rows
        if axis == 0:
            out[name] = seg.transpose(1, 0, 2, 3).reshape(DEPTH, k, n)
        else:
            out[name] = seg.transpose(1, 2, 0, 3).reshape(DEPTH, k, n)
    return out


def _small_sizes(shapes):
    return [math.prod(shapes[name]) for name in SMALL]


def _pack_grads(big, small_flat):
    parts = []
    for name, k, n, axis in BIG:
        ks, ns = _shard_dims(k, n, axis)
        g = big[name]
        if axis == 0:
            g = g.reshape(2, 2, N_CHIPS, ks, n).transpose(0, 2, 1, 3, 4)
        else:
            g = g.reshape(2, 2, k, N_CHIPS, ns).transpose(0, 3, 1, 2, 4)
        parts.append(g.reshape(2, N_CHIPS, -1, LANES))
    parts.append(small_flat.reshape(N_CHIPS, 2, -1, LANES).transpose(1, 0, 2, 3))
    return jnp.concatenate(parts, axis=2)


def _unpack_grad_shard(halves):
    out, off = {}, 0
    for name, k, n, axis in BIG:
        ks, ns = _shard_dims(k, n, axis)
        rows = 2 * ks * ns // LANES
        out[name] = halves[:, off:off + rows].reshape(DEPTH, ks, ns)
        off += rows
    return out, halves[:, off:].reshape(-1, LANES)


def _round_up(v, m):
    return (v + m - 1) // m * m


def _local_step(x, p, target, full, small, *, tm, tk):
    row = lambda a: a.reshape(1, -1)
    tril = jnp.tril(jnp.ones((CHUNK, CHUNK), bool))
    wm = jnp.where(tril, small["sgu_w"], 0.0).astype(BF16).reshape(DEPTH, 3, 2, CHUNK, CHUNK)
    wcat = wm.transpose(0, 1, 3, 2, 4).reshape(DEPTH, 3, CHUNK, 2 * CHUNK)
    wcat_t = wm.transpose(0, 1, 2, 4, 3).reshape(DEPTH, 3, 2 * CHUNK, CHUNK)
    bmat = jnp.repeat(jnp.swapaxes(small["sgu_b"], 1, 2), HEAD_DIM, axis=2)
    head = jnp.arange(D_A) // HEAD_DIM
    avg = jnp.where(head[:, None] == head[None, :], 1.0 / HEAD_DIM, 0.0).astype(BF16)
    grp = jnp.arange(D_C) // HEAD_DIM
    eye = grp[:, None] == grp[None, :]
    pw_rows = small["pool_w"].reshape(DEPTH, D_C, HEAD_DIM)
    pool_bd = jnp.where(eye[None], jnp.tile(pw_rows, (1, 1, D_C // HEAD_DIM)), 0.0).astype(BF16)
    pool_bd_t = jnp.swapaxes(pool_bd, 1, 2)
    conv8 = jnp.pad(small["conv_w"], ((0, 0), (0, 8 - 3), (0, 0)))
    full_t = {name: jnp.swapaxes(full[name], 1, 2) for name in ("w_in", "w_out", "w_ff1", "w_ff2", "w_ple_gate")}

    saved = []
    h = x
    for l in range(DEPTH):
        proj, ycat, x1 = mix_fwd(h, row(small["norm_mix_g"][l]), full["w_in"][l], full["w_out"][l], wcat[l], bmat[l],
                                 row(small["sgu_ln_g"][l]), row(small["sgu_ln_b"][l]), avg, conv8[l], pool_bd[l],
                                 row(small["pool_scale"][l]), tm=tm)
        a, x2, x3 = ffn_ple_fwd(x1, p[l], row(small["norm_ff_g"][l]), full["w_ff1"][l], full["w_ff2"][l],
                                row(small["norm_ple_g"][l]), full["w_ple_gate"][l], full["w_ple_proj"][l], tm=tm)
        saved.append((h, proj, ycat, x1, a, x2))
        h = x3

    loss_blk, d_final_g, d = loss_head(h, target, row(small["final_g"]), tm=tm)

    gbig = {name: [None] * DEPTH for name, _, _, _ in BIG}
    gsm = {name: [None] * DEPTH for name in SMALL if name != "final_g"}
    for l in reversed(range(DEPTH)):
        xin, proj, ycat, x1, a, x2 = saved[l]
        dx2, h3, dpre, dpp, dg_ple = ple_bwd(d, x2, p[l], row(small["norm_ple_g"][l]), full["w_ple_gate"][l],
                                             full_t["w_ple_gate"][l], full["w_ple_proj"][l], tm=tm)
        gbig["w_ple_gate"][l] = wgrad(h3, dpre, tk=tk)
        gbig["w_ple_proj"][l] = wgrad(p[l], dpp, tk=tk)
        dx1, h2, r, da, dg_ff = ffn_bwd(dx2, x1, a, row(small["norm_ff_g"][l]), full_t["w_ff1"][l], full_t["w_ff2"][l],
                                        tm=tm // 2)
        gbig["w_ff2"][l] = wgrad(r, dx2, tk=tk)
        gbig["w_ff1"][l] = wgrad(h2, da, tk=tk)
        (d, h1, dproj, dg_mix, dws, dbm, dlng, dlnb, dcw, dpw, dps) = mix_bwd(
            dx1, xin, proj, row(small["norm_mix_g"][l]), full_t["w_in"][l], full_t["w_out"][l], wcat[l], wcat_t[l], bmat[l],
            row(small["sgu_ln_g"][l]), row(small["sgu_ln_b"][l]), avg, conv8[l], pool_bd[l], pool_bd_t[l],
            row(small["pool_scale"][l]), tm=tm)
        gbig["w_out"][l] = wgrad(ycat, dx1, tk=tk)
        gbig["w_in"][l] = wgrad(h1, dproj, tk=tk)
        gsm["norm_ple_g"][l] = dg_ple[0]
        gsm["norm_ff_g"][l] = dg_ff[0]
        gsm["norm_mix_g"][l] = dg_mix[0]
        gsm["sgu_w"][l] = dws.reshape(2 * 3, CHUNK, CHUNK)
        gsm["sgu_b"][l] = dbm[:, ::HEAD_DIM].T
        gsm["sgu_ln_g"][l] = dlng[0]
        gsm["sgu_ln_b"][l] = dlnb[0]
        gsm["conv_w"][l] = dcw[0:3]
        gsm["pool_w"][l] = jnp.stack([dpw[g * HEAD_DIM:(g + 1) * HEAD_DIM, g * HEAD_DIM:(g + 1) * HEAD_DIM]
                                      for g in range(D_C // HEAD_DIM)])
        gsm["pool_scale"][l] = dps[0]

    gbig = {name: jnp.stack(v) for name, v in gbig.items()}
    gsm = {name: jnp.stack(v) for name, v in gsm.items()}
    gsm["final_g"] = d_final_g[0]
    return loss_blk[0, 0], d, gbig, gsm


def kernel(x, p, norm_mix_g, w_in, sgu_w, sgu_b, sgu_ln_g, sgu_ln_b, conv_w, pool_w, pool_scale, w_out, norm_ff_g, w_ff1, w_ff2, norm_ple_g, w_ple_gate, w_ple_proj, final_g, loss_target, m_norm_mix_g, m_w_in, m_sgu_w, m_sgu_b, m_sgu_ln_g, m_sgu_ln_b, m_conv_w, m_pool_w, m_pool_scale, m_w_out, m_norm_ff_g, m_w_ff1, m_w_ff2, m_norm_ple_g, m_w_ple_gate, m_w_ple_proj, m_final_g, v_norm_mix_g, v_w_in, v_sgu_w, v_sgu_b, v_sgu_ln_g, v_sgu_ln_b, v_conv_w, v_pool_w, v_pool_scale, v_w_out, v_norm_ff_g, v_w_ff1, v_w_ff2, v_norm_ple_g, v_w_ple_gate, v_w_ple_proj, v_final_g):
    args = dict(locals())
    w = {name: args[name] for name in WEIGHTS}
    m = {name: args["m_" + name] for name in WEIGHTS}
    v = {name: args["v_" + name] for name in WEIGHTS}
    t = x.shape[1]
    tm = min(512, t)
    tk = min(2048, t)
    x_idx, y_idx, c_idx = _place()
    chip = 2 * x_idx + y_idx

    conv_rows = _round_up(DEPTH * 3 * (D_B // N_CHIPS), 8 * 128) // 128
    conv_flat = jnp.pad(w["conv_w"].reshape(-1), (0, conv_rows * 128 - DEPTH * 3 * (D_B // N_CHIPS))).reshape(conv_rows, 128)
    packed_w, conv_all = chip_allgather([_pack_weight_shards(w), conv_flat])
    full = _unpack_gathered_weights(packed_w)
    conv_full = (conv_all.reshape(N_CHIPS, -1)[:, :DEPTH * 3 * (D_B // N_CHIPS)]
                 .reshape(N_CHIPS, DEPTH, 3, D_B // N_CHIPS).transpose(1, 2, 0, 3).reshape(DEPTH, 3, D_B))
    small = {name: w[name] for name in SMALL}
    small["conv_w"] = conv_full

    loss_local, grad_x, gbig, gsm = _local_step(x[0], p[:, 0], loss_target[0], full, small, tm=tm, tk=tk)

    sizes = [gsm[name].size for name in SMALL]
    small_len = _round_up(sum(sizes), 2 * N_CHIPS * 8 * LANES)
    small_flat = jnp.pad(jnp.concatenate([gsm[name].reshape(-1) for name in SMALL]), (0, small_len - sum(sizes)))
    packed_g = _pack_grads(gbig, small_flat)
    rows = packed_g.shape[2]
    br = 256 if rows % 256 == 0 else 8
    from_sibling = sibling_swap(packed_g, src_half=True)
    pair_sum = add_own_half(packed_g, from_sibling, c_idx.reshape(1).astype(jnp.int32), br=br)
    landed = chip_scatter(pair_sum)
    mine = add_chips(landed, br=br)
    other = sibling_swap(mine)
    halves = jnp.stack([jnp.where(c_idx == 0, mine, other), jnp.where(c_idx == 0, other, mine)])
    gshard, small_quarter = _unpack_grad_shard(halves)
    (small_all,) = chip_allgather([small_quarter])
    small_red = small_all.reshape(-1)

    grads = dict(gshard)
    off = 0
    for name, size in zip(SMALL, sizes):
        grads[name] = small_red[off:off + size].reshape(gsm[name].shape)
        off += size
    grads["conv_w"] = lax.dynamic_slice_in_dim(grads["conv_w"], chip * (D_B // N_CHIPS), D_B // N_CHIPS, axis=2)

    loss = lax.psum(loss_local, ("x", "y", "c"))
    delta, new_m, new_v = {}, {}, {}
    for name in WEIGHTS:
        delta[name], new_m[name], new_v[name] = adamw(w[name], grads[name], m[name], v[name])
    return (loss, grad_x[None], *[grads[n] for n in WEIGHTS], *[delta[n] for n in WEIGHTS],
            *[new_m[n] for n in WEIGHTS], *[new_v[n] for n in WEIGHTS])
```

```python
import math

import jax
import jax.numpy as jnp
from jax import lax
from jax.experimental import pallas as pl
from jax.experimental.pallas import tpu as pltpu

F32 = jnp.float32
BF16 = jnp.bfloat16

D_MODEL = 1024
DEPTH = 4
D_PLE = 256
D_FF = 4096
HEAD_DIM = 64
D_A = 384
D_B = 384
D_C = 256
D_IN = 2176
CHUNK = 128
HALO = 16
RMS_EPS = 1e-6
LN_EPS = 1e-5
N_CHIPS = 4
LANES = 1024

ADAM_LR = 0.001
ADAM_B1 = 0.9
ADAM_B2 = 0.999
ADAM_EPS = 1e-08
ADAM_WD = 0.01
ADAM_STEP = 10

VMEM_LIMIT_BYTES = 60 * 1024 * 1024

_RSQRT2 = 0.7071067811865476
_INV_SQRT_2PI = 0.3989422804014327

BIG = (
    ("w_in", D_MODEL, D_IN, 1),
    ("w_out", D_MODEL, D_MODEL, 0),
    ("w_ff1", D_MODEL, D_FF, 1),
    ("w_ff2", D_FF, D_MODEL, 0),
    ("w_ple_gate", D_MODEL, D_MODEL, 0),
    ("w_ple_proj", D_PLE, D_MODEL, 1),
)
SMALL = ("norm_mix_g", "sgu_w", "sgu_b", "sgu_ln_g", "sgu_ln_b", "conv_w", "pool_w", "pool_scale",
         "norm_ff_g", "norm_ple_g", "final_g")
WEIGHTS = ("norm_mix_g", "w_in", "sgu_w", "sgu_b", "sgu_ln_g", "sgu_ln_b", "conv_w", "pool_w", "pool_scale",
           "w_out", "norm_ff_g", "w_ff1", "w_ff2", "norm_ple_g", "w_ple_gate", "w_ple_proj", "final_g")
CONV_SHARD = DEPTH * 3 * (D_B // N_CHIPS)


def _dot(a, b):
    return jnp.dot(a, b, preferred_element_type=F32)


def _dot_nt(a, b):
    return lax.dot_general(a, b, (((1,), (1,)), ((), ())), preferred_element_type=F32)


def _dot_tn(a, b):
    return lax.dot_general(a, b, (((0,), (0,)), ((), ())), preferred_element_type=F32)


def _const_spec(shape):
    nd = len(shape)
    return pl.BlockSpec(shape, lambda i: (0,) * nd, pipeline_mode=pl.Buffered(1))


def _acc_spec(shape):
    nd = len(shape)
    return pl.BlockSpec(shape, lambda i: (0,) * nd)


def _params(*sem):
    return pltpu.CompilerParams(dimension_semantics=sem, vmem_limit_bytes=VMEM_LIMIT_BYTES)


def _rms_bwd(dh, n, rs, g):
    dn = dh * g
    return rs * (dn - n * jnp.mean(dn * n, axis=-1, keepdims=True))


def _gelu(x):
    return x * (0.5 * (1.0 + lax.erf(x * _RSQRT2)))


def _gelu_and_grad(x):
    cdf = 0.5 * (1.0 + lax.erf(x * _RSQRT2))
    return x * cdf, cdf + x * (jnp.exp(-0.5 * x * x) * _INV_SQRT_2PI)


def _group_mean(v, avg):
    hi = v.astype(BF16)
    lo = (v - hi.astype(F32)).astype(BF16)
    return _dot(hi, avg) + _dot(lo, avg)


def _lane_lt(shape, bound):
    return lax.broadcasted_iota(jnp.int32, shape, 1) < bound


def _sgu_mix(vnb2, wcat_j, lo_mask):
    zero = jnp.zeros_like(vnb2)
    stacked = jnp.concatenate([jnp.where(lo_mask, vnb2, zero), jnp.where(lo_mask, zero, vnb2)], axis=0)
    return _dot(wcat_j, stacked)


def _pool_means(ext, tile_rows, first_pos):
    s2 = ext + pltpu.roll(ext, 1, 0)
    s4 = s2 + pltpu.roll(s2, 2, 0)
    s8 = s4 + pltpu.roll(s4, 4, 0)
    s16 = s8 + pltpu.roll(s8, 8, 0)
    pos = (first_pos + lax.broadcasted_iota(jnp.int32, (tile_rows, 1), 0) + 1).astype(F32)
    lane = lax.broadcasted_iota(jnp.int32, (tile_rows, D_C), 1)
    sums = jnp.where(lane < 64, s2[HALO:], jnp.where(lane < 128, s4[HALO:], jnp.where(lane < 192, s8[HALO:], s16[HALO:])))
    win = jnp.where(lane < 64, 2.0, jnp.where(lane < 128, 4.0, jnp.where(lane < 192, 8.0, 16.0)))
    inv = 1.0 / jnp.minimum(pos, win)
    return sums * inv, inv


MESH = pl.DeviceIdType.MESH
_ANY = pl.BlockSpec(memory_space=pl.ANY)


def _place():
    return lax.axis_index("x"), lax.axis_index("y"), lax.axis_index("c")


def _chip_peers(x, y):
    return [(1 - x, y), (x, 1 - y), (1 - x, 1 - y)]


class _Comm:
    def __init__(self, ins, out_shapes, sems, copies):
        self.ins, self.out_shapes, self.sems, self.copies = list(ins), list(out_shapes), list(sems), copies

    def start(self, in_refs, out_refs, sem_refs):
        local, sends, _ = self.copies(in_refs, out_refs, sem_refs)
        for cp in local + sends:
            cp.start()

    def wait(self, in_refs, out_refs, sem_refs):
        local, sends, recvs = self.copies(in_refs, out_refs, sem_refs)
        for cp in recvs:
            cp.wait_recv()
        for cp in sends:
            cp.wait_send()
        for cp in local:
            cp.wait()


def _remote(src, dst, send_sem, recv_sem, device):
    return pltpu.make_async_remote_copy(src_ref=src, dst_ref=dst, send_sem=send_sem, recv_sem=recv_sem,
                                        device_id=device, device_id_type=MESH)


def _gather_comm(shards, layer, conv=None):
    ins = [shards[name] for name, _, _, _ in BIG] + ([conv] if conv is not None else [])
    out_shapes = []
    for name, k, n, axis in BIG:
        shape = (N_CHIPS, k, n // N_CHIPS) if name == "w_in" else (k, n)
        out_shapes.append(jax.ShapeDtypeStruct(shape, BF16))
    if conv is not None:
        out_shapes.append(jax.ShapeDtypeStruct((N_CHIPS,) + conv.shape, conv.dtype))
    n_arr = len(ins)

    def block(a, out_ref, chip):
        if a == len(BIG) or BIG[a][0] == "w_in":
            return out_ref.at[chip]
        _, k, n, axis = BIG[a]
        if axis == 0:
            return out_ref.at[pl.ds(chip * (k // N_CHIPS), k // N_CHIPS), :]
        return out_ref.at[:, pl.ds(chip * (n // N_CHIPS), n // N_CHIPS)]

    def copies(in_refs, out_refs, sem_refs):
        send_sems, recv_sems, local_sems = sem_refs
        x, y, c = _place()
        me = 2 * x + y
        local, sends, recvs = [], [], []
        for a in range(n_arr):
            src = in_refs[a].at[layer] if a < len(BIG) else in_refs[a]
            local.append(pltpu.make_async_copy(src, block(a, out_refs[a], me), local_sems.at[a]))
            for j, (px, py) in enumerate(_chip_peers(x, y)):
                sends.append(_remote(src, block(a, out_refs[a], me), send_sems.at[a, j], recv_sems.at[a, j], (px, py, c)))
                recvs.append(_remote(src, block(a, out_refs[a], 2 * px + py), send_sems.at[a, j], recv_sems.at[a, j],
                                     (px, py, c)))
        return local, sends, recvs

    sems = [pltpu.SemaphoreType.DMA((n_arr, 3)), pltpu.SemaphoreType.DMA((n_arr, 3)), pltpu.SemaphoreType.DMA((n_arr,))]
    return _Comm(ins, out_shapes, sems, copies)


def _allgather_comm(a):
    def copies(in_refs, out_refs, sem_refs):
        send_sems, recv_sems, local_sem = sem_refs
        x, y, c = _place()
        me = 2 * x + y
        local = [pltpu.make_async_copy(in_refs[0], out_refs[0].at[me], local_sem)]
        sends, recvs = [], []
        for j, (px, py) in enumerate(_chip_peers(x, y)):
            sends.append(_remote(in_refs[0], out_refs[0].at[me], send_sems.at[j], recv_sems.at[j], (px, py, c)))
            recvs.append(_remote(in_refs[0], out_refs[0].at[2 * px + py], send_sems.at[j], recv_sems.at[j], (px, py, c)))
        return local, sends, recvs

    sems = [pltpu.SemaphoreType.DMA((3,)), pltpu.SemaphoreType.DMA((3,)), pltpu.SemaphoreType.DMA]
    return _Comm([a], [jax.ShapeDtypeStruct((N_CHIPS,) + a.shape, a.dtype)], sems, copies)


def _swap_comm(a, half=False):
    shape = a.shape[1:] if half else a.shape

    def copies(in_refs, out_refs, sem_refs):
        x, y, c = _place()
        src = in_refs[0].at[1 - c] if half else in_refs[0]
        cp = _remote(src, out_refs[0], sem_refs[0], sem_refs[1], (x, y, 1 - c))
        return [], [cp], [cp]

    return _Comm([a], [jax.ShapeDtypeStruct(shape, a.dtype)], [pltpu.SemaphoreType.DMA, pltpu.SemaphoreType.DMA], copies)


def _scatter_comm(s):
    def copies(in_refs, out_refs, sem_refs):
        send_sems, recv_sems, local_sem = sem_refs
        x, y, c = _place()
        me = 2 * x + y
        local = [pltpu.make_async_copy(in_refs[0].at[me], out_refs[0].at[me], local_sem)]
        sends, recvs = [], []
        for j, (px, py) in enumerate(_chip_peers(x, y)):
            peer = 2 * px + py
            sends.append(_remote(in_refs[0].at[peer], out_refs[0].at[me], send_sems.at[j], recv_sems.at[j], (px, py, c)))
            recvs.append(_remote(in_refs[0].at[me], out_refs[0].at[peer], send_sems.at[j], recv_sems.at[j], (px, py, c)))
        return local, sends, recvs

    sems = [pltpu.SemaphoreType.DMA((3,)), pltpu.SemaphoreType.DMA((3,)), pltpu.SemaphoreType.DMA]
    return _Comm([s], [jax.ShapeDtypeStruct(s.shape, s.dtype)], sems, copies)


def _run_comm(comm, name):
    def body(*refs):
        ni, no = len(comm.ins), len(comm.out_shapes)
        in_refs, out_refs, sem_refs = refs[:ni], refs[ni:ni + no], refs[ni + no:]
        comm.start(in_refs, out_refs, sem_refs)
        comm.wait(in_refs, out_refs, sem_refs)

    return pl.pallas_call(
        body, name=name, in_specs=[_ANY] * len(comm.ins), out_specs=[_ANY] * len(comm.out_shapes),
        out_shape=comm.out_shapes, scratch_shapes=comm.sems,
        compiler_params=pltpu.CompilerParams(has_side_effects=True),
    )(*comm.ins)


def _tile_call(body, name, nt, in_specs, out_specs, out_shape, scratch, args, comm):
    if comm is None:
        outs = pl.pallas_call(body, name=name, grid=(nt,), in_specs=in_specs, out_specs=out_specs, out_shape=out_shape,
                              scratch_shapes=scratch, compiler_params=_params("arbitrary"))(*args)
        return outs, []
    n_in, n_out, n_scr = len(in_specs), len(out_specs), len(scratch)
    ci, co = len(comm.ins), len(comm.out_shapes)

    def hosted(*refs):
        in_refs = refs[:n_in]
        cin = refs[n_in:n_in + ci]
        out_refs = refs[n_in + ci:n_in + ci + n_out]
        cout = refs[n_in + ci + n_out:n_in + ci + n_out + co]
        scr = refs[n_in + ci + n_out + co:n_in + ci + n_out + co + n_scr]
        sems = refs[n_in + ci + n_out + co + n_scr:]
        i = pl.program_id(0)

        @pl.when(i == 0)
        def _():
            comm.start(cin, cout, sems)

        body(*in_refs, *out_refs, *scr)

        @pl.when(i == nt - 1)
        def _():
            comm.wait(cin, cout, sems)

    outs = pl.pallas_call(
        hosted, name=name + "_comm", grid=(nt,),
        in_specs=list(in_specs) + [_ANY] * ci, out_specs=list(out_specs) + [_ANY] * co,
        out_shape=list(out_shape) + comm.out_shapes, scratch_shapes=list(scratch) + comm.sems,
        compiler_params=_params("arbitrary"),
    )(*args, *comm.ins)
    return outs[:n_out], outs[n_out:]


def mix_fwd(x, g_mix, w_in, w_out, wcat, bmat, ln_g, ln_b, avg, conv_w, pool_bd, pool_scale, *, tm):
    t = x.shape[0]
    nt = t // tm

    def body(x_ref, g_ref, win_ref, wout_ref, wcat_ref, bmat_ref, lng_ref, lnb_ref, avg_ref, cw_ref, pw_ref, ps_ref,
             proj_ref, ycat_ref, x1_ref, hbuf, zbuf):
        i = pl.program_id(0)

        @pl.when(i == 0)
        def _():
            hbuf[0:HALO, :] = jnp.zeros((HALO, D_B), F32)
            zbuf[0:HALO, :] = jnp.zeros((HALO, D_C), F32)

        xv = x_ref[...]
        n = xv * lax.rsqrt(jnp.mean(xv * xv, axis=-1, keepdims=True) + RMS_EPS)
        h1 = (n * g_ref[...]).astype(BF16)
        proj_ref[...] = _dot(h1, win_ref[...])

        lo_mask = _lane_lt((CHUNK, CHUNK), HEAD_DIM)
        avg = avg_ref[...]
        for c in range(tm // CHUNK):
            rows = pl.ds(c * CHUNK, CHUNK)
            gu = _gelu(proj_ref[rows, 0:D_A])
            gv = _gelu(proj_ref[rows, D_A:2 * D_A])
            dv = gv - _group_mean(gv, avg)
            var = _group_mean(dv * dv, avg)
            vnb = (dv * lax.rsqrt(var + LN_EPS) * lng_ref[...] + lnb_ref[...]).astype(BF16)
            for j in range(3):
                cols = slice(j * CHUNK, (j + 1) * CHUNK)
                mixed = _sgu_mix(vnb[:, cols], wcat_ref[j], lo_mask) + bmat_ref[:, cols]
                ycat_ref[rows, cols] = (gu[:, cols] * mixed).astype(BF16)

        o = 2 * D_A
        hcur = proj_ref[:, o + 2 * D_B:o + 3 * D_B] * proj_ref[:, o:o + D_B]
        hbuf[HALO:HALO + tm, :] = hcur
        y = (cw_ref[2:3, :] * hcur + cw_ref[1:2, :] * hbuf[pl.ds(HALO - 1, tm), :]
             + cw_ref[0:1, :] * hbuf[pl.ds(HALO - 2, tm), :])
        ycat_ref[:, D_A:D_A + D_B] = (proj_ref[:, o + D_B:o + 2 * D_B] * y).astype(BF16)
        hbuf[0:HALO, :] = hbuf[tm:tm + HALO, :]

        zc = proj_ref[:, o + 3 * D_B:D_IN]
        zbuf[HALO:HALO + tm, :] = zc
        mean, _ = _pool_means(zbuf[...], tm, i * tm)
        pooled = (mean - zc).astype(BF16)
        ycat_ref[:, D_A + D_B:D_MODEL] = (_dot(pooled, pw_ref[...]) * ps_ref[...]).astype(BF16)
        zbuf[0:HALO, :] = zbuf[tm:tm + HALO, :]

        x1_ref[...] = xv + _dot(ycat_ref[...], wout_ref[...])

    row = lambda w: pl.BlockSpec((tm, w), lambda i: (i, 0))
    return pl.pallas_call(
        body, name="mix_fwd", grid=(nt,),
        in_specs=[row(D_MODEL), _const_spec((1, D_MODEL)), _const_spec((D_MODEL, D_IN)), _const_spec((D_MODEL, D_MODEL)),
                  _const_spec((3, CHUNK, 2 * CHUNK)), _const_spec((CHUNK, D_A)), _const_spec((1, D_A)), _const_spec((1, D_A)),
                  _const_spec((D_A, D_A)), _const_spec((8, D_B)), _const_spec((D_C, D_C)), _const_spec((1, D_C))],
        out_specs=[row(D_IN), row(D_MODEL), row(D_MODEL)],
        out_shape=[jax.ShapeDtypeStruct((t, D_IN), F32), jax.ShapeDtypeStruct((t, D_MODEL), BF16),
                   jax.ShapeDtypeStruct((t, D_MODEL), F32)],
        scratch_shapes=[pltpu.VMEM((tm + HALO, D_B), F32), pltpu.VMEM((tm + HALO, D_C), F32)],
        compiler_params=_params("arbitrary"),
    )(x, g_mix, w_in, w_out, wcat, bmat, ln_g, ln_b, avg, conv_w, pool_bd, pool_scale)


def ffn_ple_fwd(x1, p, g_ff, w_ff1, w_ff2, g_ple, w_gate, w_proj, *, tm, comm=None):
    t = x1.shape[0]
    nt = t // tm
    nc = D_FF // D_MODEL

    def body(x1_ref, p_ref, gff_ref, w1_ref, w2_ref, gple_ref, wg_ref, wp_ref, a_ref, x2_ref, x3_ref):
        x1v = x1_ref[...]
        n2 = x1v * lax.rsqrt(jnp.mean(x1v * x1v, axis=-1, keepdims=True) + RMS_EPS)
        h2 = (n2 * gff_ref[...]).astype(BF16)
        acc = x1v
        for c in range(nc):
            cols = slice(c * D_MODEL, (c + 1) * D_MODEL)
            a = _dot(h2, w1_ref[:, cols])
            a_ref[:, cols] = a.astype(BF16)
            ra = jnp.maximum(a, 0.0)
            acc = acc + _dot((ra * ra).astype(BF16), w2_ref[cols, :])
        x2_ref[...] = acc
        n3 = acc * lax.rsqrt(jnp.mean(acc * acc, axis=-1, keepdims=True) + RMS_EPS)
        h3 = (n3 * gple_ref[...]).astype(BF16)
        gate = jax.nn.sigmoid(_dot(h3, wg_ref[...]))
        pp = _dot(p_ref[...].astype(BF16), wp_ref[...])
        x3_ref[...] = acc + pp * gate

    row = lambda w: pl.BlockSpec((tm, w), lambda i: (i, 0))
    return _tile_call(
        body, "ffn_ple_fwd", nt,
        [row(D_MODEL), row(D_PLE), _const_spec((1, D_MODEL)), _const_spec((D_MODEL, D_FF)),
         _const_spec((D_FF, D_MODEL)), _const_spec((1, D_MODEL)), _const_spec((D_MODEL, D_MODEL)),
         _const_spec((D_PLE, D_MODEL))],
        [row(D_FF), row(D_MODEL), row(D_MODEL)],
        [jax.ShapeDtypeStruct((t, D_FF), BF16), jax.ShapeDtypeStruct((t, D_MODEL), F32),
         jax.ShapeDtypeStruct((t, D_MODEL), F32)],
        [], (x1, p, g_ff, w_ff1, w_ff2, g_ple, w_gate, w_proj), comm)


def loss_head(x, target, g, *, tm):
    t = x.shape[0]
    nt = t // tm

    def body(x_ref, t_ref, g_ref, loss_ref, dg_ref, dx_ref, sq_acc):
        i = pl.program_id(0)

        @pl.when(i == 0)
        def _():
            sq_acc[...] = jnp.zeros_like(sq_acc)
            dg_ref[...] = jnp.zeros_like(dg_ref)

        xv = x_ref[...]
        rs = lax.rsqrt(jnp.mean(xv * xv, axis=-1, keepdims=True) + RMS_EPS)
        n = xv * rs
        gv = g_ref[...]
        err = n * gv - t_ref[...]
        sq_acc[...] += jnp.sum(err * err, axis=0, keepdims=True)
        dy = err * (1.0 / D_MODEL)
        dg_ref[...] += jnp.sum(dy * n, axis=0, keepdims=True)
        dx_ref[...] = _rms_bwd(dy, n, rs, gv)

        @pl.when(i == nt - 1)
        def _():
            total = jnp.sum(sq_acc[...], axis=1, keepdims=True) * (0.5 / D_MODEL)
            loss_ref[...] = jnp.broadcast_to(total, loss_ref.shape)

    row = pl.BlockSpec((tm, D_MODEL), lambda i: (i, 0))
    return pl.pallas_call(
        body, name="loss_head", grid=(nt,),
        in_specs=[row, row, _const_spec((1, D_MODEL))],
        out_specs=[_acc_spec((8, 128)), _acc_spec((1, D_MODEL)), row],
        out_shape=[jax.ShapeDtypeStruct((8, 128), F32), jax.ShapeDtypeStruct((1, D_MODEL), F32),
                   jax.ShapeDtypeStruct((t, D_MODEL), F32)],
        scratch_shapes=[pltpu.VMEM((1, D_MODEL), F32)],
        compiler_params=_params("arbitrary"),
    )(x, target, g)


def ple_bwd(d, x2, p, g_ple, w_gate, w_proj, *, tm, comm=None):
    t = d.shape[0]
    nt = t // tm

    def body(d_ref, x2_ref, p_ref, g_ref, wg_ref, wp_ref, dx2_ref, h3_ref, dpre_ref, dpp_ref, dg_ref):
        i = pl.program_id(0)

        @pl.when(i == 0)
        def _():
            dg_ref[...] = jnp.zeros_like(dg_ref)

        dv = d_ref[...]
        x2v = x2_ref[...]
        rs = lax.rsqrt(jnp.mean(x2v * x2v, axis=-1, keepdims=True) + RMS_EPS)
        n3 = x2v * rs
        gv = g_ref[...]
        h3 = (n3 * gv).astype(BF16)
        h3_ref[...] = h3
        gate = jax.nn.sigmoid(_dot(h3, wg_ref[...]))
        pp = _dot(p_ref[...].astype(BF16), wp_ref[...])
        dpp_ref[...] = (dv * gate).astype(BF16)
        dpre = (dv * pp * gate * (1.0 - gate)).astype(BF16)
        dpre_ref[...] = dpre
        dh3 = _dot_nt(dpre, wg_ref[...])
        dg_ref[...] += jnp.sum(dh3 * n3, axis=0, keepdims=True)
        dx2_ref[...] = dv + _rms_bwd(dh3, n3, rs, gv)

    row = lambda w: pl.BlockSpec((tm, w), lambda i: (i, 0))
    return _tile_call(
        body, "ple_bwd", nt,
        [row(D_MODEL), row(D_MODEL), row(D_PLE), _const_spec((1, D_MODEL)), _const_spec((D_MODEL, D_MODEL)),
         _const_spec((D_PLE, D_MODEL))],
        [row(D_MODEL), row(D_MODEL), row(D_MODEL), row(D_MODEL), _acc_spec((1, D_MODEL))],
        [jax.ShapeDtypeStruct((t, D_MODEL), F32), jax.ShapeDtypeStruct((t, D_MODEL), BF16),
         jax.ShapeDtypeStruct((t, D_MODEL), BF16), jax.ShapeDtypeStruct((t, D_MODEL), BF16),
         jax.ShapeDtypeStruct((1, D_MODEL), F32)],
        [], (d, x2, p, g_ple, w_gate, w_proj), comm)


def ffn_bwd(dx2, x1, a, g_ff, w_ff1, w_ff2, *, tm, comm=None):
    t = dx2.shape[0]
    nt = t // tm
    nc = D_FF // D_MODEL

    def body(dx2_ref, x1_ref, a_ref, g_ref, w1_ref, w2_ref, dx1_ref, h2_ref, r_ref, da_ref, dg_ref):
        i = pl.program_id(0)

        @pl.when(i == 0)
        def _():
            dg_ref[...] = jnp.zeros_like(dg_ref)

        dv = dx2_ref[...]
        x1v = x1_ref[...]
        rs = lax.rsqrt(jnp.mean(x1v * x1v, axis=-1, keepdims=True) + RMS_EPS)
        n2 = x1v * rs
        gv = g_ref[...]
        h2_ref[...] = (n2 * gv).astype(BF16)
        dvb = dv.astype(BF16)
        dh2 = jnp.zeros((tm, D_MODEL), F32)
        for c in range(nc):
            cols = slice(c * D_MODEL, (c + 1) * D_MODEL)
            ra = jnp.maximum(a_ref[:, cols].astype(F32), 0.0)
            r_ref[:, cols] = (ra * ra).astype(BF16)
            da = (_dot_nt(dvb, w2_ref[cols, :]) * (2.0 * ra)).astype(BF16)
            da_ref[:, cols] = da
            dh2 = dh2 + _dot_nt(da, w1_ref[:, cols])
        dg_ref[...] += jnp.sum(dh2 * n2, axis=0, keepdims=True)
        dx1_ref[...] = dv + _rms_bwd(dh2, n2, rs, gv)

    row = lambda w: pl.BlockSpec((tm, w), lambda i: (i, 0))
    return _tile_call(
        body, "ffn_bwd", nt,
        [row(D_MODEL), row(D_MODEL), row(D_FF), _const_spec((1, D_MODEL)), _const_spec((D_MODEL, D_FF)),
         _const_spec((D_FF, D_MODEL))],
        [row(D_MODEL), row(D_MODEL), row(D_FF), row(D_FF), _acc_spec((1, D_MODEL))],
        [jax.ShapeDtypeStruct((t, D_MODEL), F32), jax.ShapeDtypeStruct((t, D_MODEL), BF16),
         jax.ShapeDtypeStruct((t, D_FF), BF16), jax.ShapeDtypeStruct((t, D_FF), BF16),
         jax.ShapeDtypeStruct((1, D_MODEL), F32)],
        [], (dx2, x1, a, g_ff, w_ff1, w_ff2), comm)


def mix_bwd(dx1, x, proj, g_mix, w_in, w_out, wcat, wcat_t, bmat, ln_g, ln_b, avg, conv_w, pool_bd, pool_bd_t,
            pool_scale, *, tm, comm=None):
    t = dx1.shape[0]
    nt = t // tm
    prev_blocks = tm // HALO

    def body(dx1_ref, x_ref, proj_ref, prev_ref, g_ref, win_ref, wout_ref, wcat_ref, wcatt_ref, bmat_ref, lng_ref,
             lnb_ref, avg_ref, cw_ref, pw_ref, pwt_ref, ps_ref,
             dx_ref, h1_ref, dproj_ref, dg_ref, dws_ref, dbm_ref, dlng_ref, dlnb_ref, dcw_ref, dpw_ref, dps_ref,
             dyc, dpj, hbuf, zbuf, dybuf, qbuf):
        i = pl.program_id(0)
        ti = nt - 1 - i

        @pl.when(i == 0)
        def _():
            for ref in (dg_ref, dws_ref, dbm_ref, dlng_ref, dlnb_ref, dcw_ref, dpw_ref, dps_ref):
                ref[...] = jnp.zeros_like(ref)
            dybuf[tm:tm + HALO, :] = jnp.zeros((HALO, D_B), F32)
            qbuf[tm:tm + HALO, :] = jnp.zeros((HALO, D_C), F32)

        dx1v = dx1_ref[...]
        dyc[...] = _dot_nt(dx1v.astype(BF16), wout_ref[...])

        lo_mask = _lane_lt((CHUNK, CHUNK), HEAD_DIM)
        avg = avg_ref[...]
        lng = lng_ref[...]
        for c in range(tm // CHUNK):
            rows = pl.ds(c * CHUNK, CHUNK)
            gu, dgu = _gelu_and_grad(proj_ref[rows, 0:D_A])
            gv, dgv = _gelu_and_grad(proj_ref[rows, D_A:2 * D_A])
            cen = gv - _group_mean(gv, avg)
            rstd = lax.rsqrt(_group_mean(cen * cen, avg) + LN_EPS)
            vhat = cen * rstd
            vnb = (vhat * lng + lnb_ref[...]).astype(BF16)
            dya = dyc[rows, 0:D_A]
            dvn_parts = []
            for j in range(3):
                cols = slice(j * CHUNK, (j + 1) * CHUNK)
                vnb2 = vnb[:, cols]
                mixed = _sgu_mix(vnb2, wcat_ref[j], lo_mask) + bmat_ref[:, cols]
                dya2 = dya[:, cols]
                dpj[rows, cols] = dya2 * mixed * dgu[:, cols]
                dm = dya2 * gu[:, cols]
                dbm_ref[:, cols] += dm
                dmb = dm.astype(BF16)
                zero = jnp.zeros_like(dmb)
                dm_st = jnp.concatenate([jnp.where(lo_mask, dmb, zero), jnp.where(lo_mask, zero, dmb)], axis=0)
                dws_ref[j] += _dot_nt(dm_st, vnb2)
                dvn_st = _dot(wcatt_ref[j], dmb)
                dvn_parts.append(jnp.where(lo_mask, dvn_st[0:CHUNK], dvn_st[CHUNK:2 * CHUNK]))
            dvn = jnp.concatenate(dvn_parts, axis=1)
            dlng_ref[...] += jnp.sum(dvn * vhat, axis=0, keepdims=True)
            dlnb_ref[...] += jnp.sum(dvn, axis=0, keepdims=True)
            dvh = dvn * lng
            dgv_in = rstd * (dvh - _group_mean(dvh, avg) - vhat * _group_mean(dvh * vhat, avg))
            dpj[rows, D_A:2 * D_A] = dgv_in * dgv

        o = 2 * D_A
        live = (ti > 0).astype(F32)
        zb = proj_ref[:, o:o + D_B]
        gb = proj_ref[:, o + D_B:o + 2 * D_B]
        gc = proj_ref[:, o + 2 * D_B:o + 3 * D_B]
        hcur = gc * zb
        hbuf[0:HALO, :] = prev_ref[:, o + 2 * D_B:o + 3 * D_B] * prev_ref[:, o:o + D_B] * live
        hbuf[HALO:HALO + tm, :] = hcur
        hm1 = hbuf[pl.ds(HALO - 1, tm), :]
        hm2 = hbuf[pl.ds(HALO - 2, tm), :]
        y = cw_ref[2:3, :] * hcur + cw_ref[1:2, :] * hm1 + cw_ref[0:1, :] * hm2
        dout = dyc[:, D_A:D_A + D_B]
        dpj[:, o + D_B:o + 2 * D_B] = dout * y
        dy = dout * gb
        dcw_ref[2:3, :] += jnp.sum(dy * hcur, axis=0, keepdims=True)
        dcw_ref[1:2, :] += jnp.sum(dy * hm1, axis=0, keepdims=True)
        dcw_ref[0:1, :] += jnp.sum(dy * hm2, axis=0, keepdims=True)
        dybuf[0:tm, :] = dy
        dh = (cw_ref[2:3, :] * dy + cw_ref[1:2, :] * dybuf[pl.ds(1, tm), :] + cw_ref[0:1, :] * dybuf[pl.ds(2, tm), :])
        dybuf[tm:tm + HALO, :] = dybuf[0:HALO, :]
        dpj[:, o:o + D_B] = dh * gc
        dpj[:, o + 2 * D_B:o + 3 * D_B] = dh * zb

        zc = proj_ref[:, o + 3 * D_B:D_IN]
        zbuf[0:HALO, :] = prev_ref[:, o + 3 * D_B:D_IN] * live
        zbuf[HALO:HALO + tm, :] = zc
        mean, inv = _pool_means(zbuf[...], tm, ti * tm)
        pooled = (mean - zc).astype(BF16)
        dyp = dyc[:, D_A + D_B:D_MODEL]
        ps = ps_ref[...]
        dps_ref[...] += jnp.sum(dyp * _dot(pooled, pw_ref[...]), axis=0, keepdims=True)
        dpw = (dyp * ps).astype(BF16)
        dpw_ref[...] += _dot_tn(pooled, dpw)
        dpooled = _dot(dpw, pwt_ref[...])
        qbuf[0:tm, :] = dpooled * inv
        q = qbuf[...]
        nrows = tm + HALO
        f2 = q + pltpu.roll(q, nrows - 1, 0)
        f4 = f2 + pltpu.roll(f2, nrows - 2, 0)
        f8 = f4 + pltpu.roll(f4, nrows - 4, 0)
        f16 = f8 + pltpu.roll(f8, nrows - 8, 0)
        lane = lax.broadcasted_iota(jnp.int32, (tm, D_C), 1)
        ahead = jnp.where(lane < 64, f2[0:tm], jnp.where(lane < 128, f4[0:tm], jnp.where(lane < 192, f8[0:tm], f16[0:tm])))
        dpj[:, o + 3 * D_B:D_IN] = ahead - dpooled
        qbuf[tm:tm + HALO, :] = qbuf[0:HALO, :]

        dprojb = dpj[...].astype(BF16)
        dproj_ref[...] = dprojb
        dh1 = _dot_nt(dprojb, win_ref[...])
        xv = x_ref[...]
        rs = lax.rsqrt(jnp.mean(xv * xv, axis=-1, keepdims=True) + RMS_EPS)
        n1 = xv * rs
        gv1 = g_ref[...]
        h1_ref[...] = (n1 * gv1).astype(BF16)
        dg_ref[...] += jnp.sum(dh1 * n1, axis=0, keepdims=True)
        dx_ref[...] = dx1v + _rms_bwd(dh1, n1, rs, gv1)

        @pl.when(i == nt - 1)
        def _():
            tril = (lax.broadcasted_iota(jnp.int32, (2 * CHUNK, CHUNK), 0) % CHUNK
                    >= lax.broadcasted_iota(jnp.int32, (2 * CHUNK, CHUNK), 1))
            for j in range(3):
                dws_ref[j] = jnp.where(tril, dws_ref[j], 0.0)
            dbm_ref[...] = _group_mean(dbm_ref[...], avg) * float(HEAD_DIM)

    rev = lambda w: pl.BlockSpec((tm, w), lambda i: (nt - 1 - i, 0))
    prev = pl.BlockSpec((HALO, D_IN), lambda i: (jnp.maximum((nt - 1 - i) * prev_blocks - 1, 0), 0))
    acc_shapes = [(1, D_MODEL), (3, 2 * CHUNK, CHUNK), (CHUNK, D_A), (1, D_A), (1, D_A), (8, D_B), (D_C, D_C), (1, D_C)]
    return _tile_call(
        body, "mix_bwd", nt,
        [rev(D_MODEL), rev(D_MODEL), rev(D_IN), prev, _const_spec((1, D_MODEL)), _const_spec((D_MODEL, D_IN)),
         _const_spec((D_MODEL, D_MODEL)), _const_spec((3, CHUNK, 2 * CHUNK)), _const_spec((3, 2 * CHUNK, CHUNK)),
         _const_spec((CHUNK, D_A)), _const_spec((1, D_A)), _const_spec((1, D_A)), _const_spec((D_A, D_A)),
         _const_spec((8, D_B)), _const_spec((D_C, D_C)), _const_spec((D_C, D_C)), _const_spec((1, D_C))],
        [rev(D_MODEL), rev(D_MODEL), rev(D_IN)] + [_acc_spec(s) for s in acc_shapes],
        [jax.ShapeDtypeStruct((t, D_MODEL), F32), jax.ShapeDtypeStruct((t, D_MODEL), BF16),
         jax.ShapeDtypeStruct((t, D_IN), BF16)] + [jax.ShapeDtypeStruct(s, F32) for s in acc_shapes],
        [pltpu.VMEM((tm, D_MODEL), F32), pltpu.VMEM((tm, D_IN), F32),
         pltpu.VMEM((tm + HALO, D_B), F32), pltpu.VMEM((tm + HALO, D_C), F32),
         pltpu.VMEM((tm + HALO, D_B), F32), pltpu.VMEM((tm + HALO, D_C), F32)],
        (dx1, x, proj, proj, g_mix, w_in, w_out, wcat, wcat_t, bmat, ln_g, ln_b, avg, conv_w, pool_bd, pool_bd_t,
         pool_scale), comm)


def wgrad(a, b, *, tk):
    t, m = a.shape
    n = b.shape[1]
    bm = min(m, 1024)
    bn = 1024 if n % 1024 == 0 else n
    nk = t // tk

    def body(a_ref, b_ref, o_ref):
        k = pl.program_id(2)

        @pl.when(k == 0)
        def _():
            o_ref[...] = jnp.zeros_like(o_ref)

        o_ref[...] += _dot_tn(a_ref[...].astype(BF16), b_ref[...].astype(BF16))

    return pl.pallas_call(
        body, name=f"wgrad_{m}x{n}", grid=(m // bm, n // bn, nk),
        in_specs=[pl.BlockSpec((tk, bm), lambda i, j, k: (k, i)), pl.BlockSpec((tk, bn), lambda i, j, k: (k, j))],
        out_specs=pl.BlockSpec((bm, bn), lambda i, j, k: (i, j)),
        out_shape=jax.ShapeDtypeStruct((m, n), F32),
        compiler_params=_params("parallel", "parallel", "arbitrary"),
    )(a, b)


def _row_block(rows, cols, target_bytes):
    target = max(8, target_bytes // (4 * cols))
    if rows <= target:
        return rows
    best = None
    for br in range(8, target + 1, 8):
        if rows % br == 0:
            best = br
    return best if best is not None else rows


def adamw(w, g, m, v):
    shape = w.shape
    cols = shape[-1]
    rows = math.prod(shape[:-1]) if len(shape) > 1 else 1
    br = _row_block(rows, cols, 1 << 20)

    def body(w_ref, g_ref, m_ref, v_ref, d_ref, nm_ref, nv_ref):
        gv = g_ref[...]
        nm = ADAM_B1 * m_ref[...] + (1.0 - ADAM_B1) * gv
        nv = ADAM_B2 * v_ref[...] + (1.0 - ADAM_B2) * jnp.square(gv)
        m_hat = nm / (1.0 - ADAM_B1 ** ADAM_STEP)
        v_hat = nv / (1.0 - ADAM_B2 ** ADAM_STEP)
        d_ref[...] = -ADAM_LR * (m_hat / (jnp.sqrt(v_hat) + ADAM_EPS) + ADAM_WD * w_ref[...])
        nm_ref[...] = nm
        nv_ref[...] = nv

    spec = pl.BlockSpec((br, cols), lambda i: (i, 0))
    outs = pl.pallas_call(
        body, name="adamw", grid=(rows // br,),
        in_specs=[spec] * 4, out_specs=[spec] * 3,
        out_shape=[jax.ShapeDtypeStruct((rows, cols), F32)] * 3,
        compiler_params=pltpu.CompilerParams(dimension_semantics=("parallel",)),
    )(*(a.reshape(rows, cols) for a in (w, g, m, v)))
    return tuple(o.reshape(shape) for o in outs)


def add_own_half(g, other, c_idx):
    _, nchip, rows, cols = g.shape
    br = _row_block(rows, cols, 2 << 20)

    def body(c_ref, g_ref, o_ref, out_ref):
        del c_ref
        out_ref[...] = g_ref[...] + o_ref[...]

    spec = pl.BlockSpec((None, br, cols), lambda j, i, c_ref: (j, i, 0))
    return pl.pallas_call(
        body, name="add_own_half",
        grid_spec=pltpu.PrefetchScalarGridSpec(
            num_scalar_prefetch=1, grid=(nchip, rows // br),
            in_specs=[pl.BlockSpec((None, None, br, cols), lambda j, i, c_ref: (c_ref[0], j, i, 0)), spec],
            out_specs=spec),
        out_shape=jax.ShapeDtypeStruct((nchip, rows, cols), F32),
        compiler_params=pltpu.CompilerParams(dimension_semantics=("parallel", "parallel")),
    )(c_idx, g, other)


def add_chips(parts):
    _, rows, cols = parts.shape
    br = _row_block(rows, cols, 2 << 20)

    def body(p_ref, out_ref):
        out_ref[...] = ((p_ref[0] + p_ref[1]) + p_ref[2]) + p_ref[3]

    return pl.pallas_call(
        body, name="add_chips", grid=(rows // br,),
        in_specs=[pl.BlockSpec((N_CHIPS, br, cols), lambda i: (0, i, 0))],
        out_specs=pl.BlockSpec((br, cols), lambda i: (i, 0)),
        out_shape=jax.ShapeDtypeStruct((rows, cols), F32),
        compiler_params=pltpu.CompilerParams(dimension_semantics=("parallel",)),
    )(parts)


def _shard_dims(k, n, axis):
    return (k // N_CHIPS, n) if axis == 0 else (k, n // N_CHIPS)


def _pack_layer_grads(g, extra=None):
    parts = []
    for name, k, n, axis in BIG:
        ks, ns = _shard_dims(k, n, axis)
        a = g[name]
        if axis == 0:
            a = a.reshape(N_CHIPS, 2, ks // 2, n).transpose(1, 0, 2, 3)
        else:
            a = a.reshape(2, k // 2, N_CHIPS, ns).transpose(0, 2, 1, 3)
        parts.append(a.reshape(2, N_CHIPS, -1, LANES))
    if extra is not None:
        parts.append(extra)
    return jnp.concatenate(parts, axis=2)


def _unpack_layer_shard(halves):
    out, off = {}, 0
    for name, k, n, axis in BIG:
        ks, ns = _shard_dims(k, n, axis)
        rows = ks * ns // 2 // LANES
        out[name] = halves[:, off:off + rows].reshape(ks, ns)
        off += rows
    return out, halves[:, off:]


def _round_up(v, m):
    return (v + m - 1) // m * m


def _prep_small(small):
    tril = jnp.tril(jnp.ones((CHUNK, CHUNK), bool))
    wm = jnp.where(tril, small["sgu_w"], 0.0).astype(BF16).reshape(DEPTH, 3, 2, CHUNK, CHUNK)
    head = jnp.arange(D_A) // HEAD_DIM
    grp = jnp.arange(D_C) // HEAD_DIM
    pw_rows = small["pool_w"].reshape(DEPTH, D_C, HEAD_DIM)
    pool_bd = jnp.where((grp[:, None] == grp[None, :])[None], jnp.tile(pw_rows, (1, 1, D_C // HEAD_DIM)), 0.0).astype(BF16)
    return dict(
        wcat=wm.transpose(0, 1, 3, 2, 4).reshape(DEPTH, 3, CHUNK, 2 * CHUNK),
        wcat_t=wm.transpose(0, 1, 2, 4, 3).reshape(DEPTH, 3, 2 * CHUNK, CHUNK),
        bmat=jnp.repeat(jnp.swapaxes(small["sgu_b"], 1, 2), HEAD_DIM, axis=2),
        avg=jnp.where(head[:, None] == head[None, :], 1.0 / HEAD_DIM, 0.0).astype(BF16),
        pool_bd=pool_bd, pool_bd_t=jnp.swapaxes(pool_bd, 1, 2),
        conv8=jnp.pad(small["conv_w"], ((0, 0), (0, 8 - 3), (0, 0))),
    )


def _row(a):
    return a.reshape(1, -1)


def _fwd_layer(h, p_l, wl, small, prep, l, tm, comm=None):
    proj, ycat, x1 = mix_fwd(h, _row(small["norm_mix_g"][l]), wl["w_in"], wl["w_out"], prep["wcat"][l], prep["bmat"][l],
                             _row(small["sgu_ln_g"][l]), _row(small["sgu_ln_b"][l]), prep["avg"], prep["conv8"][l],
                             prep["pool_bd"][l], _row(small["pool_scale"][l]), tm=tm)
    (a, x2, x3), couts = ffn_ple_fwd(x1, p_l, _row(small["norm_ff_g"][l]), wl["w_ff1"], wl["w_ff2"],
                                     _row(small["norm_ple_g"][l]), wl["w_ple_gate"], wl["w_ple_proj"], tm=tm, comm=comm)
    return (h, proj, ycat, x1, a, x2), x3, couts


def _bwd_layer(d, saved, p_l, wl, small, prep, l, tm, tk, comm_a=None, after_a=None, after_b=None):
    xin, proj, ycat, x1, a, x2 = saved
    (dx2, h3, dpre, dpp, dg_ple), ca = ple_bwd(d, x2, p_l, _row(small["norm_ple_g"][l]), wl["w_ple_gate"],
                                               wl["w_ple_proj"], tm=tm, comm=comm_a)
    gb = {"w_ple_gate": wgrad(h3, dpre, tk=tk), "w_ple_proj": wgrad(p_l, dpp, tk=tk)}
    comm_b = after_a(ca) if after_a is not None else None
    (dx1, h2, r, da, dg_ff), cb = ffn_bwd(dx2, x1, a, _row(small["norm_ff_g"][l]), wl["w_ff1"], wl["w_ff2"],
                                          tm=tm // 2, comm=comm_b)
    gb["w_ff2"] = wgrad(r, dx2, tk=tk)
    gb["w_ff1"] = wgrad(h2, da, tk=tk)
    comm_c = after_b(cb) if after_b is not None else None
    (dprev, h1, dproj, dg_mix, dws, dbm, dlng, dlnb, dcw, dpw, dps), cc = mix_bwd(
        dx1, xin, proj, _row(small["norm_mix_g"][l]), wl["w_in"], wl["w_out"], prep["wcat"][l], prep["wcat_t"][l],
        prep["bmat"][l], _row(small["sgu_ln_g"][l]), _row(small["sgu_ln_b"][l]), prep["avg"], prep["conv8"][l],
        prep["pool_bd"][l], prep["pool_bd_t"][l], _row(small["pool_scale"][l]), tm=tm, comm=comm_c)
    gb["w_out"] = wgrad(ycat, dx1, tk=tk)
    gb["w_in"] = wgrad(h1, dproj, tk=tk)
    gs = {
        "norm_ple_g": dg_ple[0], "norm_ff_g": dg_ff[0], "norm_mix_g": dg_mix[0],
        "sgu_w": dws.reshape(2 * 3, CHUNK, CHUNK), "sgu_b": dbm[:, ::HEAD_DIM].T,
        "sgu_ln_g": dlng[0], "sgu_ln_b": dlnb[0], "conv_w": dcw[0:3], "pool_scale": dps[0],
        "pool_w": jnp.stack([dpw[g * HEAD_DIM:(g + 1) * HEAD_DIM, g * HEAD_DIM:(g + 1) * HEAD_DIM]
                             for g in range(D_C // HEAD_DIM)]),
    }
    return dprev, gb, gs, cc


def _local_step(x, p, target, full, small, *, tm, tk):
    prep = _prep_small(small)
    saved, h = [], x
    for l in range(DEPTH):
        wl = {name: full[name][l] for name in full}
        s, h, _ = _fwd_layer(h, p[l], wl, small, prep, l, tm)
        saved.append(s)
    loss_blk, d_final_g, d = loss_head(h, target, _row(small["final_g"]), tm=tm)
    gbig, gsm = [None] * DEPTH, [None] * DEPTH
    for l in reversed(range(DEPTH)):
        wl = {name: full[name][l] for name in full}
        d, gbig[l], gsm[l], _ = _bwd_layer(d, saved[l], p[l], wl, small, prep, l, tm, tk)
    big = {name: jnp.stack([gbig[l][name] for l in range(DEPTH)]) for name in gbig[0]}
    sm = {name: jnp.stack([gsm[l][name] for l in range(DEPTH)]) for name in gsm[0]}
    sm["final_g"] = d_final_g[0]
    return loss_blk[0, 0], d, big, sm


def _weights_of(gathered):
    wl = {name: gathered[a] for a, (name, _, _, _) in enumerate(BIG)}
    wl["w_in"] = wl["w_in"].transpose(1, 0, 2).reshape(D_MODEL, D_IN)
    return wl


def kernel(x, p, norm_mix_g, w_in, sgu_w, sgu_b, sgu_ln_g, sgu_ln_b, conv_w, pool_w, pool_scale, w_out, norm_ff_g, w_ff1, w_ff2, norm_ple_g, w_ple_gate, w_ple_proj, final_g, loss_target, m_norm_mix_g, m_w_in, m_sgu_w, m_sgu_b, m_sgu_ln_g, m_sgu_ln_b, m_conv_w, m_pool_w, m_pool_scale, m_w_out, m_norm_ff_g, m_w_ff1, m_w_ff2, m_norm_ple_g, m_w_ple_gate, m_w_ple_proj, m_final_g, v_norm_mix_g, v_w_in, v_sgu_w, v_sgu_b, v_sgu_ln_g, v_sgu_ln_b, v_conv_w, v_pool_w, v_pool_scale, v_w_out, v_norm_ff_g, v_w_ff1, v_w_ff2, v_norm_ple_g, v_w_ple_gate, v_w_ple_proj, v_final_g):
    args = dict(locals())
    w = {name: args[name] for name in WEIGHTS}
    m = {name: args["m_" + name] for name in WEIGHTS}
    v = {name: args["v_" + name] for name in WEIGHTS}
    t = x.shape[1]
    tm = min(512, t)
    tk = min(2048, t)
    x_idx, y_idx, c_idx = _place()
    chip = 2 * x_idx + y_idx
    c_arr = c_idx.reshape(1).astype(jnp.int32)
    xs, ps, target = x[0], p[:, 0], loss_target[0]

    shards = {name: w[name].astype(BF16) for name, _, _, _ in BIG}
    conv_rows = _round_up(CONV_SHARD, 8 * 128) // 128
    conv_flat = jnp.pad(w["conv_w"].reshape(-1), (0, conv_rows * 128 - CONV_SHARD)).reshape(conv_rows, 128)
    first = _run_comm(_gather_comm(shards, 0, conv_flat), "gather_first")
    conv_full = (first[len(BIG)].reshape(N_CHIPS, -1)[:, :CONV_SHARD]
                 .reshape(N_CHIPS, DEPTH, 3, D_B // N_CHIPS).transpose(1, 2, 0, 3).reshape(DEPTH, 3, D_B))
    small = {name: w[name] for name in SMALL}
    small["conv_w"] = conv_full
    prep = _prep_small(small)

    wl = [None] * DEPTH
    wl[0] = _weights_of(first)
    saved, h = [], xs
    for l in range(DEPTH):
        comm = _gather_comm(shards, l + 1) if l + 1 < DEPTH else None
        s, h, got = _fwd_layer(h, ps[l], wl[l], small, prep, l, tm, comm)
        saved.append(s)
        if comm is not None:
            wl[l + 1] = _weights_of(got)

    loss_blk, d_final_g, d = loss_head(h, target, _row(small["final_g"]), tm=tm)

    reduced = [None] * DEPTH
    gsm = [None] * DEPTH
    packed = None
    for l in reversed(range(DEPTH)):
        state = {}
        if packed is None:
            comm_a = after_a = after_b = None
        else:
            above = packed
            comm_a = _swap_comm(above, half=True)

            def after_a(res, above=above):
                return _scatter_comm(add_own_half(above, res[0], c_arr))

            def after_b(res, state=state):
                state["mine"] = add_chips(res[0])
                return _swap_comm(state["mine"])

        d, gb, gsm[l], cc = _bwd_layer(d, saved[l], ps[l], wl[l], small, prep, l, tm, tk, comm_a, after_a, after_b)
        if packed is not None:
            reduced[l + 1] = (state["mine"], cc[0])
        extra = None
        if l == 0:
            sm = {name: jnp.stack([gsm[i][name] for i in range(DEPTH)]) for name in gsm[0]}
            sm["final_g"] = d_final_g[0]
            sizes = [sm[name].size for name in SMALL]
            n_big_rows = sum(math.prod(_shard_dims(k, n, axis)) // 2 // LANES for _, k, n, axis in BIG)
            extra_rows = _round_up(n_big_rows + -(-sum(sizes) // (2 * N_CHIPS * LANES)), 256) - n_big_rows
            small_len = 2 * N_CHIPS * extra_rows * LANES
            small_flat = jnp.pad(jnp.concatenate([sm[name].reshape(-1) for name in SMALL]), (0, small_len - sum(sizes)))
            extra = small_flat.reshape(N_CHIPS, 2, extra_rows, LANES).transpose(1, 0, 2, 3)
        packed = _pack_layer_grads(gb, extra)

    from_sibling = _run_comm(_swap_comm(packed, half=True), "reduce_swap_half")[0]
    landed = _run_comm(_scatter_comm(add_own_half(packed, from_sibling, c_arr)), "reduce_scatter")[0]
    mine = add_chips(landed)
    reduced[0] = (mine, _run_comm(_swap_comm(mine), "reduce_swap")[0])

    per_layer, small_quarter = [], None
    for l in range(DEPTH):
        mine, other = reduced[l]
        halves = jnp.stack([jnp.where(c_idx == 0, mine, other), jnp.where(c_idx == 0, other, mine)])
        shard, rest = _unpack_layer_shard(halves)
        per_layer.append(shard)
        if l == 0:
            small_quarter = rest.reshape(-1, LANES)
    grads = {name: jnp.stack([per_layer[l][name] for l in range(DEPTH)]) for name, _, _, _ in BIG}
    small_red = _run_comm(_allgather_comm(small_quarter), "small_allgather")[0].reshape(-1)
    off = 0
    for name, size in zip(SMALL, sizes):
        grads[name] = small_red[off:off + size].reshape(sm[name].shape)
        off += size
    grads["conv_w"] = lax.dynamic_slice_in_dim(grads["conv_w"], chip * (D_B // N_CHIPS), D_B // N_CHIPS, axis=2)

    loss = lax.psum(loss_blk[0, 0], ("x", "y", "c"))
    delta, new_m, new_v = {}, {}, {}
    for name in WEIGHTS:
        delta[name], new_m[name], new_v[name] = adamw(w[name], grads[name], m[name], v[name])
    return (loss, d[None], *[grads[n] for n in WEIGHTS], *[delta[n] for n in WEIGHTS],
            *[new_m[n] for n in WEIGHTS], *[new_v[n] for n in WEIGHTS])
```

```python
import math

import jax
import jax.numpy as jnp
from jax import lax
from jax.experimental import pallas as pl
from jax.experimental.pallas import tpu as pltpu

F32 = jnp.float32
BF16 = jnp.bfloat16

D_MODEL = 1024
DEPTH = 4
D_PLE = 256
D_FF = 4096
HEAD_DIM = 64
D_A = 384
D_B = 384
D_C = 256
D_IN = 2176
CHUNK = 128
HALO = 16
RMS_EPS = 1e-6
LN_EPS = 1e-5
N_CHIPS = 4
LANES = 1024

ADAM_LR = 0.001
ADAM_B1 = 0.9
ADAM_B2 = 0.999
ADAM_EPS = 1e-08
ADAM_WD = 0.01
ADAM_STEP = 10

VMEM_LIMIT_BYTES = 60 * 1024 * 1024

_RSQRT2 = 0.7071067811865476
_INV_SQRT_2PI = 0.3989422804014327

BIG = (
    ("w_in", D_MODEL, D_IN, 1),
    ("w_out", D_MODEL, D_MODEL, 0),
    ("w_ff1", D_MODEL, D_FF, 1),
    ("w_ff2", D_FF, D_MODEL, 0),
    ("w_ple_gate", D_MODEL, D_MODEL, 0),
    ("w_ple_proj", D_PLE, D_MODEL, 1),
)
SMALL = ("norm_mix_g", "sgu_w", "sgu_b", "sgu_ln_g", "sgu_ln_b", "conv_w", "pool_w", "pool_scale",
         "norm_ff_g", "norm_ple_g", "final_g")
WEIGHTS = ("norm_mix_g", "w_in", "sgu_w", "sgu_b", "sgu_ln_g", "sgu_ln_b", "conv_w", "pool_w", "pool_scale",
           "w_out", "norm_ff_g", "w_ff1", "w_ff2", "norm_ple_g", "w_ple_gate", "w_ple_proj", "final_g")
CONV_SHARD = DEPTH * 3 * (D_B // N_CHIPS)


def _dot(a, b):
    return jnp.dot(a, b, preferred_element_type=F32)


def _dot_nt(a, b):
    return lax.dot_general(a, b, (((1,), (1,)), ((), ())), preferred_element_type=F32)


def _dot_tn(a, b):
    return lax.dot_general(a, b, (((0,), (0,)), ((), ())), preferred_element_type=F32)


def _const_spec(shape):
    nd = len(shape)
    return pl.BlockSpec(shape, lambda i: (0,) * nd, pipeline_mode=pl.Buffered(1))


def _acc_spec(shape):
    nd = len(shape)
    return pl.BlockSpec(shape, lambda i: (0,) * nd)


def _params(*sem):
    return pltpu.CompilerParams(dimension_semantics=sem, vmem_limit_bytes=VMEM_LIMIT_BYTES)


def _rms_bwd(dh, n, rs, g):
    dn = dh * g
    return rs * (dn - n * jnp.mean(dn * n, axis=-1, keepdims=True))


def _gelu(x):
    return x * (0.5 * (1.0 + lax.erf(x * _RSQRT2)))


def _gelu_and_grad(x):
    cdf = 0.5 * (1.0 + lax.erf(x * _RSQRT2))
    return x * cdf, cdf + x * (jnp.exp(-0.5 * x * x) * _INV_SQRT_2PI)


def _group_mean(v, avg):
    vb = v.astype(BF16)
    split = 2 * CHUNK
    return jnp.concatenate([_dot(vb[:, :split], avg[:split, :split]), _dot(vb[:, split:], avg[split:, split:])], axis=1)


def _group_mean_split(v, avg):
    hi = v.astype(BF16)
    lo = (v - hi.astype(F32)).astype(BF16)
    return _dot(hi, avg) + _dot(lo, avg)


def _lane_lt(shape, bound):
    return lax.broadcasted_iota(jnp.int32, shape, 1) < bound


def _sgu_mix(vnb2, wcat_j, lo_mask):
    zero = jnp.zeros_like(vnb2)
    stacked = jnp.concatenate([jnp.where(lo_mask, vnb2, zero), jnp.where(lo_mask, zero, vnb2)], axis=0)
    return _dot(wcat_j, stacked)


def _pool_means(ext, tile_rows, first_pos):
    s2 = ext + pltpu.roll(ext, 1, 0)
    s4 = s2 + pltpu.roll(s2, 2, 0)
    s8 = s4 + pltpu.roll(s4, 4, 0)
    s16 = s8 + pltpu.roll(s8, 8, 0)
    pos = (first_pos + lax.broadcasted_iota(jnp.int32, (tile_rows, 1), 0) + 1).astype(F32)
    lane = lax.broadcasted_iota(jnp.int32, (tile_rows, D_C), 1)
    sums = jnp.where(lane < 64, s2[HALO:], jnp.where(lane < 128, s4[HALO:], jnp.where(lane < 192, s8[HALO:], s16[HALO:])))
    win = jnp.where(lane < 64, 2.0, jnp.where(lane < 128, 4.0, jnp.where(lane < 192, 8.0, 16.0)))
    inv = 1.0 / jnp.minimum(pos, win)
    return sums * inv, inv


MESH = pl.DeviceIdType.MESH
_ANY = pl.BlockSpec(memory_space=pl.ANY)


def _place():
    return lax.axis_index("x"), lax.axis_index("y"), lax.axis_index("c")


def _chip_peers(x, y):
    return [(1 - x, y), (x, 1 - y), (1 - x, 1 - y)]


class _Comm:
    def __init__(self, ins, out_shapes, sems, copies):
        self.ins, self.out_shapes, self.sems, self.copies = list(ins), list(out_shapes), list(sems), copies

    def start(self, in_refs, out_refs, sem_refs):
        local, sends, _ = self.copies(in_refs, out_refs, sem_refs)
        for cp in local + sends:
            cp.start()

    def wait(self, in_refs, out_refs, sem_refs):
        local, sends, recvs = self.copies(in_refs, out_refs, sem_refs)
        for cp in recvs:
            cp.wait_recv()
        for cp in sends:
            cp.wait_send()
        for cp in local:
            cp.wait()


def _remote(src, dst, send_sem, recv_sem, device):
    return pltpu.make_async_remote_copy(src_ref=src, dst_ref=dst, send_sem=send_sem, recv_sem=recv_sem,
                                        device_id=device, device_id_type=MESH)


def _gather_comm(shards, layer, names, conv=None):
    mats = [b for b in BIG if b[0] in names]
    ins = [shards[name] for name, _, _, _ in mats] + ([conv] if conv is not None else [])
    out_shapes = []
    for name, k, n, axis in mats:
        shape = (N_CHIPS, k, n // N_CHIPS) if name == "w_in" else (k, n)
        out_shapes.append(jax.ShapeDtypeStruct(shape, BF16))
    if conv is not None:
        out_shapes.append(jax.ShapeDtypeStruct((N_CHIPS,) + conv.shape, conv.dtype))
    n_arr = len(ins)

    def block(a, out_ref, chip):
        if a == len(mats) or mats[a][0] == "w_in":
            return out_ref.at[chip]
        _, k, n, axis = mats[a]
        if axis == 0:
            return out_ref.at[pl.ds(chip * (k // N_CHIPS), k // N_CHIPS), :]
        return out_ref.at[:, pl.ds(chip * (n // N_CHIPS), n // N_CHIPS)]

    def copies(in_refs, out_refs, sem_refs):
        send_sems, recv_sems, local_sems = sem_refs
        x, y, c = _place()
        me = 2 * x + y
        local, sends, recvs = [], [], []
        for a in range(n_arr):
            src = in_refs[a].at[layer] if a < len(mats) else in_refs[a]
            local.append(pltpu.make_async_copy(src, block(a, out_refs[a], me), local_sems.at[a]))
            for j, (px, py) in enumerate(_chip_peers(x, y)):
                sends.append(_remote(src, block(a, out_refs[a], me), send_sems.at[a, j], recv_sems.at[a, j], (px, py, c)))
                recvs.append(_remote(src, block(a, out_refs[a], 2 * px + py), send_sems.at[a, j], recv_sems.at[a, j],
                                     (px, py, c)))
        return local, sends, recvs

    sems = [pltpu.SemaphoreType.DMA((n_arr, 3)), pltpu.SemaphoreType.DMA((n_arr, 3)), pltpu.SemaphoreType.DMA((n_arr,))]
    return _Comm(ins, out_shapes, sems, copies)


def _allgather_comm(a):
    def copies(in_refs, out_refs, sem_refs):
        send_sems, recv_sems, local_sem = sem_refs
        x, y, c = _place()
        me = 2 * x + y
        local = [pltpu.make_async_copy(in_refs[0], out_refs[0].at[me], local_sem)]
        sends, recvs = [], []
        for j, (px, py) in enumerate(_chip_peers(x, y)):
            sends.append(_remote(in_refs[0], out_refs[0].at[me], send_sems.at[j], recv_sems.at[j], (px, py, c)))
            recvs.append(_remote(in_refs[0], out_refs[0].at[2 * px + py], send_sems.at[j], recv_sems.at[j], (px, py, c)))
        return local, sends, recvs

    sems = [pltpu.SemaphoreType.DMA((3,)), pltpu.SemaphoreType.DMA((3,)), pltpu.SemaphoreType.DMA]
    return _Comm([a], [jax.ShapeDtypeStruct((N_CHIPS,) + a.shape, a.dtype)], sems, copies)


def _swap_comm(a, half=False):
    shape = a.shape[1:] if half else a.shape

    def copies(in_refs, out_refs, sem_refs):
        x, y, c = _place()
        src = in_refs[0].at[1 - c] if half else in_refs[0]
        cp = _remote(src, out_refs[0], sem_refs[0], sem_refs[1], (x, y, 1 - c))
        return [], [cp], [cp]

    return _Comm([a], [jax.ShapeDtypeStruct(shape, a.dtype)], [pltpu.SemaphoreType.DMA, pltpu.SemaphoreType.DMA], copies)


def _scatter_comm(s):
    def copies(in_refs, out_refs, sem_refs):
        send_sems, recv_sems, local_sem = sem_refs
        x, y, c = _place()
        me = 2 * x + y
        local = [pltpu.make_async_copy(in_refs[0].at[me], out_refs[0].at[me], local_sem)]
        sends, recvs = [], []
        for j, (px, py) in enumerate(_chip_peers(x, y)):
            peer = 2 * px + py
            sends.append(_remote(in_refs[0].at[peer], out_refs[0].at[me], send_sems.at[j], recv_sems.at[j], (px, py, c)))
            recvs.append(_remote(in_refs[0].at[me], out_refs[0].at[peer], send_sems.at[j], recv_sems.at[j], (px, py, c)))
        return local, sends, recvs

    sems = [pltpu.SemaphoreType.DMA((3,)), pltpu.SemaphoreType.DMA((3,)), pltpu.SemaphoreType.DMA]
    return _Comm([s], [jax.ShapeDtypeStruct(s.shape, s.dtype)], sems, copies)


def _run_comm(comm, name):
    def body(*refs):
        ni, no = len(comm.ins), len(comm.out_shapes)
        in_refs, out_refs, sem_refs = refs[:ni], refs[ni:ni + no], refs[ni + no:]
        comm.start(in_refs, out_refs, sem_refs)
        comm.wait(in_refs, out_refs, sem_refs)

    return pl.pallas_call(
        body, name=name, in_specs=[_ANY] * len(comm.ins), out_specs=[_ANY] * len(comm.out_shapes),
        out_shape=comm.out_shapes, scratch_shapes=comm.sems,
        compiler_params=pltpu.CompilerParams(has_side_effects=True),
    )(*comm.ins)


def _tile_call(body, name, nt, in_specs, out_specs, out_shape, scratch, args, comm):
    if comm is None:
        outs = pl.pallas_call(body, name=name, grid=(nt,), in_specs=in_specs, out_specs=out_specs, out_shape=out_shape,
                              scratch_shapes=scratch, compiler_params=_params("arbitrary"))(*args)
        return outs, []
    n_in, n_out, n_scr = len(in_specs), len(out_specs), len(scratch)
    ci, co = len(comm.ins), len(comm.out_shapes)

    def hosted(*refs):
        in_refs = refs[:n_in]
        cin = refs[n_in:n_in + ci]
        out_refs = refs[n_in + ci:n_in + ci + n_out]
        cout = refs[n_in + ci + n_out:n_in + ci + n_out + co]
        scr = refs[n_in + ci + n_out + co:n_in + ci + n_out + co + n_scr]
        sems = refs[n_in + ci + n_out + co + n_scr:]
        i = pl.program_id(0)

        @pl.when(i == 0)
        def _():
            comm.start(cin, cout, sems)

        body(*in_refs, *out_refs, *scr)

        @pl.when(i == nt - 1)
        def _():
            comm.wait(cin, cout, sems)

    outs = pl.pallas_call(
        hosted, name=name + "_comm", grid=(nt,),
        in_specs=list(in_specs) + [_ANY] * ci, out_specs=list(out_specs) + [_ANY] * co,
        out_shape=list(out_shape) + comm.out_shapes, scratch_shapes=list(scratch) + comm.sems,
        compiler_params=_params("arbitrary"),
    )(*args, *comm.ins)
    return outs[:n_out], outs[n_out:]


def mix_fwd(x, g_mix, w_in, w_out, wcat, bmat, ln_g, ln_b, avg, conv_w, pool_bd, pool_scale, *, tm, comm=None):
    t = x.shape[0]
    nt = t // tm

    def body(x_ref, g_ref, win_ref, wout_ref, wcat_ref, bmat_ref, lng_ref, lnb_ref, avg_ref, cw_ref, pw_ref, ps_ref,
             proj_ref, ycat_ref, x1_ref, hbuf, zbuf):
        i = pl.program_id(0)

        @pl.when(i == 0)
        def _():
            hbuf[0:HALO, :] = jnp.zeros((HALO, D_B), F32)
            zbuf[0:HALO, :] = jnp.zeros((HALO, D_C), F32)

        xv = x_ref[...]
        n = xv * lax.rsqrt(jnp.mean(xv * xv, axis=-1, keepdims=True) + RMS_EPS)
        h1 = (n * g_ref[...]).astype(BF16)
        proj_ref[...] = _dot(h1, win_ref[...])

        lo_mask = _lane_lt((CHUNK, CHUNK), HEAD_DIM)
        avg = avg_ref[...]
        for c in range(tm // CHUNK):
            rows = pl.ds(c * CHUNK, CHUNK)
            gu = _gelu(proj_ref[rows, 0:D_A])
            gv = _gelu(proj_ref[rows, D_A:2 * D_A])
            dv = gv - _group_mean(gv, avg)
            var = _group_mean(dv * dv, avg)
            vnb = (dv * lax.rsqrt(var + LN_EPS) * lng_ref[...] + lnb_ref[...]).astype(BF16)
            for j in range(3):
                cols = slice(j * CHUNK, (j + 1) * CHUNK)
                mixed = _sgu_mix(vnb[:, cols], wcat_ref[j], lo_mask) + bmat_ref[:, cols]
                ycat_ref[rows, cols] = (gu[:, cols] * mixed).astype(BF16)

        o = 2 * D_A
        hcur = proj_ref[:, o + 2 * D_B:o + 3 * D_B] * proj_ref[:, o:o + D_B]
        hbuf[HALO:HALO + tm, :] = hcur
        y = (cw_ref[2:3, :] * hcur + cw_ref[1:2, :] * hbuf[pl.ds(HALO - 1, tm), :]
             + cw_ref[0:1, :] * hbuf[pl.ds(HALO - 2, tm), :])
        ycat_ref[:, D_A:D_A + D_B] = (proj_ref[:, o + D_B:o + 2 * D_B] * y).astype(BF16)
        hbuf[0:HALO, :] = hbuf[tm:tm + HALO, :]

        zc = proj_ref[:, o + 3 * D_B:D_IN]
        zbuf[HALO:HALO + tm, :] = zc
        mean, _ = _pool_means(zbuf[...], tm, i * tm)
        pooled = (mean - zc).astype(BF16)
        ycat_ref[:, D_A + D_B:D_MODEL] = (_dot(pooled, pw_ref[...]) * ps_ref[...]).astype(BF16)
        zbuf[0:HALO, :] = zbuf[tm:tm + HALO, :]

        x1_ref[...] = xv + _dot(ycat_ref[...], wout_ref[...])

    row = lambda w: pl.BlockSpec((tm, w), lambda i: (i, 0))
    return _tile_call(
        body, "mix_fwd", nt,
        [row(D_MODEL), _const_spec((1, D_MODEL)), _const_spec((D_MODEL, D_IN)), _const_spec((D_MODEL, D_MODEL)),
         _const_spec((3, CHUNK, 2 * CHUNK)), _const_spec((CHUNK, D_A)), _const_spec((1, D_A)), _const_spec((1, D_A)),
         _const_spec((D_A, D_A)), _const_spec((8, D_B)), _const_spec((D_C, D_C)), _const_spec((1, D_C))],
        [row(D_IN), row(D_MODEL), row(D_MODEL)],
        [jax.ShapeDtypeStruct((t, D_IN), F32), jax.ShapeDtypeStruct((t, D_MODEL), BF16),
         jax.ShapeDtypeStruct((t, D_MODEL), F32)],
        [pltpu.VMEM((tm + HALO, D_B), F32), pltpu.VMEM((tm + HALO, D_C), F32)],
        (x, g_mix, w_in, w_out, wcat, bmat, ln_g, ln_b, avg, conv_w, pool_bd, pool_scale), comm)


def ffn_ple_fwd(x1, p, g_ff, w_ff1, w_ff2, g_ple, w_gate, w_proj, *, tm, comm=None):
    t = x1.shape[0]
    nt = t // tm
    nc = D_FF // D_MODEL

    def body(x1_ref, p_ref, gff_ref, w1_ref, w2_ref, gple_ref, wg_ref, wp_ref, a_ref, x2_ref, x3_ref):
        x1v = x1_ref[...]
        n2 = x1v * lax.rsqrt(jnp.mean(x1v * x1v, axis=-1, keepdims=True) + RMS_EPS)
        h2 = (n2 * gff_ref[...]).astype(BF16)
        acc = x1v
        for c in range(nc):
            cols = slice(c * D_MODEL, (c + 1) * D_MODEL)
            a = _dot(h2, w1_ref[:, cols])
            a_ref[:, cols] = a.astype(BF16)
            ra = jnp.maximum(a, 0.0)
            acc = acc + _dot((ra * ra).astype(BF16), w2_ref[cols, :])
        x2_ref[...] = acc
        n3 = acc * lax.rsqrt(jnp.mean(acc * acc, axis=-1, keepdims=True) + RMS_EPS)
        h3 = (n3 * gple_ref[...]).astype(BF16)
        gate = jax.nn.sigmoid(_dot(h3, wg_ref[...]))
        pp = _dot(p_ref[...].astype(BF16), wp_ref[...])
        x3_ref[...] = acc + pp * gate

    row = lambda w: pl.BlockSpec((tm, w), lambda i: (i, 0))
    return _tile_call(
        body, "ffn_ple_fwd", nt,
        [row(D_MODEL), row(D_PLE), _const_spec((1, D_MODEL)), _const_spec((D_MODEL, D_FF)),
         _const_spec((D_FF, D_MODEL)), _const_spec((1, D_MODEL)), _const_spec((D_MODEL, D_MODEL)),
         _const_spec((D_PLE, D_MODEL))],
        [row(D_FF), row(D_MODEL), row(D_MODEL)],
        [jax.ShapeDtypeStruct((t, D_FF), BF16), jax.ShapeDtypeStruct((t, D_MODEL), F32),
         jax.ShapeDtypeStruct((t, D_MODEL), F32)],
        [], (x1, p, g_ff, w_ff1, w_ff2, g_ple, w_gate, w_proj), comm)


def loss_head(x, target, g, *, tm):
    t = x.shape[0]
    nt = t // tm

    def body(x_ref, t_ref, g_ref, loss_ref, dg_ref, dx_ref, sq_acc):
        i = pl.program_id(0)

        @pl.when(i == 0)
        def _():
            sq_acc[...] = jnp.zeros_like(sq_acc)
            dg_ref[...] = jnp.zeros_like(dg_ref)

        xv = x_ref[...]
        rs = lax.rsqrt(jnp.mean(xv * xv, axis=-1, keepdims=True) + RMS_EPS)
        n = xv * rs
        gv = g_ref[...]
        err = n * gv - t_ref[...]
        sq_acc[...] += jnp.sum(err * err, axis=0, keepdims=True)
        dy = err * (1.0 / D_MODEL)
        dg_ref[...] += jnp.sum(dy * n, axis=0, keepdims=True)
        dx_ref[...] = _rms_bwd(dy, n, rs, gv)

        @pl.when(i == nt - 1)
        def _():
            total = jnp.sum(sq_acc[...], axis=1, keepdims=True) * (0.5 / D_MODEL)
            loss_ref[...] = jnp.broadcast_to(total, loss_ref.shape)

    row = pl.BlockSpec((tm, D_MODEL), lambda i: (i, 0))
    return pl.pallas_call(
        body, name="loss_head", grid=(nt,),
        in_specs=[row, row, _const_spec((1, D_MODEL))],
        out_specs=[_acc_spec((8, 128)), _acc_spec((1, D_MODEL)), row],
        out_shape=[jax.ShapeDtypeStruct((8, 128), F32), jax.ShapeDtypeStruct((1, D_MODEL), F32),
                   jax.ShapeDtypeStruct((t, D_MODEL), F32)],
        scratch_shapes=[pltpu.VMEM((1, D_MODEL), F32)],
        compiler_params=_params("arbitrary"),
    )(x, target, g)


def ple_bwd(d, x2, p, g_ple, w_gate, w_proj, *, tm, comm=None):
    t = d.shape[0]
    nt = t // tm

    def body(d_ref, x2_ref, p_ref, g_ref, wg_ref, wp_ref, dx2_ref, h3_ref, dpre_ref, dpp_ref, dg_ref):
        i = pl.program_id(0)

        @pl.when(i == 0)
        def _():
            dg_ref[...] = jnp.zeros_like(dg_ref)

        dv = d_ref[...]
        x2v = x2_ref[...]
        rs = lax.rsqrt(jnp.mean(x2v * x2v, axis=-1, keepdims=True) + RMS_EPS)
        n3 = x2v * rs
        gv = g_ref[...]
        h3 = (n3 * gv).astype(BF16)
        h3_ref[...] = h3
        gate = jax.nn.sigmoid(_dot(h3, wg_ref[...]))
        pp = _dot(p_ref[...].astype(BF16), wp_ref[...])
        dpp_ref[...] = (dv * gate).astype(BF16)
        dpre = (dv * pp * gate * (1.0 - gate)).astype(BF16)
        dpre_ref[...] = dpre
        dh3 = _dot_nt(dpre, wg_ref[...])
        dg_ref[...] += jnp.sum(dh3 * n3, axis=0, keepdims=True)
        dx2_ref[...] = dv + _rms_bwd(dh3, n3, rs, gv)

    row = lambda w: pl.BlockSpec((tm, w), lambda i: (i, 0))
    return _tile_call(
        body, "ple_bwd", nt,
        [row(D_MODEL), row(D_MODEL), row(D_PLE), _const_spec((1, D_MODEL)), _const_spec((D_MODEL, D_MODEL)),
         _const_spec((D_PLE, D_MODEL))],
        [row(D_MODEL), row(D_MODEL), row(D_MODEL), row(D_MODEL), _acc_spec((1, D_MODEL))],
        [jax.ShapeDtypeStruct((t, D_MODEL), F32), jax.ShapeDtypeStruct((t, D_MODEL), BF16),
         jax.ShapeDtypeStruct((t, D_MODEL), BF16), jax.ShapeDtypeStruct((t, D_MODEL), BF16),
         jax.ShapeDtypeStruct((1, D_MODEL), F32)],
        [], (d, x2, p, g_ple, w_gate, w_proj), comm)


def ffn_bwd(dx2, x1, a, g_ff, w_ff1, w_ff2, *, tm, comm=None):
    t = dx2.shape[0]
    nt = t // tm
    nc = D_FF // D_MODEL

    def body(dx2_ref, x1_ref, a_ref, g_ref, w1_ref, w2_ref, dx1_ref, h2_ref, r_ref, da_ref, dg_ref):
        i = pl.program_id(0)

        @pl.when(i == 0)
        def _():
            dg_ref[...] = jnp.zeros_like(dg_ref)

        dv = dx2_ref[...]
        x1v = x1_ref[...]
        rs = lax.rsqrt(jnp.mean(x1v * x1v, axis=-1, keepdims=True) + RMS_EPS)
        n2 = x1v * rs
        gv = g_ref[...]
        h2_ref[...] = (n2 * gv).astype(BF16)
        dvb = dv.astype(BF16)
        dh2 = jnp.zeros((tm, D_MODEL), F32)
        for c in range(nc):
            cols = slice(c * D_MODEL, (c + 1) * D_MODEL)
            ra = jnp.maximum(a_ref[:, cols].astype(F32), 0.0)
            r_ref[:, cols] = (ra * ra).astype(BF16)
            da = (_dot_nt(dvb, w2_ref[cols, :]) * (2.0 * ra)).astype(BF16)
            da_ref[:, cols] = da
            dh2 = dh2 + _dot_nt(da, w1_ref[:, cols])
        dg_ref[...] += jnp.sum(dh2 * n2, axis=0, keepdims=True)
        dx1_ref[...] = dv + _rms_bwd(dh2, n2, rs, gv)

    row = lambda w: pl.BlockSpec((tm, w), lambda i: (i, 0))
    return _tile_call(
        body, "ffn_bwd", nt,
        [row(D_MODEL), row(D_MODEL), row(D_FF), _const_spec((1, D_MODEL)), _const_spec((D_MODEL, D_FF)),
         _const_spec((D_FF, D_MODEL))],
        [row(D_MODEL), row(D_MODEL), row(D_FF), row(D_FF), _acc_spec((1, D_MODEL))],
        [jax.ShapeDtypeStruct((t, D_MODEL), F32), jax.ShapeDtypeStruct((t, D_MODEL), BF16),
         jax.ShapeDtypeStruct((t, D_FF), BF16), jax.ShapeDtypeStruct((t, D_FF), BF16),
         jax.ShapeDtypeStruct((1, D_MODEL), F32)],
        [], (dx2, x1, a, g_ff, w_ff1, w_ff2), comm)


def mix_bwd(dx1, x, proj, g_mix, w_in, w_out, wcat, wcat_t, bmat, ln_g, ln_b, avg, conv_w, pool_bd, pool_bd_t,
            pool_scale, *, tm, comm=None):
    t = dx1.shape[0]
    nt = t // tm
    prev_blocks = tm // HALO

    def body(dx1_ref, x_ref, proj_ref, prev_ref, g_ref, win_ref, wout_ref, wcat_ref, wcatt_ref, bmat_ref, lng_ref,
             lnb_ref, avg_ref, cw_ref, pw_ref, pwt_ref, ps_ref,
             dx_ref, h1_ref, dproj_ref, dg_ref, dws_ref, dbm_ref, dlng_ref, dlnb_ref, dcw_ref, dpw_ref, dps_ref,
             dyc, dpj, hbuf, zbuf, dybuf, qbuf):
        i = pl.program_id(0)
        ti = nt - 1 - i

        @pl.when(i == 0)
        def _():
            for ref in (dg_ref, dws_ref, dbm_ref, dlng_ref, dlnb_ref, dcw_ref, dpw_ref, dps_ref):
                ref[...] = jnp.zeros_like(ref)
            dybuf[tm:tm + HALO, :] = jnp.zeros((HALO, D_B), F32)
            qbuf[tm:tm + HALO, :] = jnp.zeros((HALO, D_C), F32)

        dx1v = dx1_ref[...]
        dyc[...] = _dot_nt(dx1v.astype(BF16), wout_ref[...])

        lo_mask = _lane_lt((CHUNK, CHUNK), HEAD_DIM)
        avg = avg_ref[...]
        lng = lng_ref[...]
        for c in range(tm // CHUNK):
            rows = pl.ds(c * CHUNK, CHUNK)
            gu, dgu = _gelu_and_grad(proj_ref[rows, 0:D_A])
            gv, dgv = _gelu_and_grad(proj_ref[rows, D_A:2 * D_A])
            cen = gv - _group_mean(gv, avg)
            rstd = lax.rsqrt(_group_mean(cen * cen, avg) + LN_EPS)
            vhat = cen * rstd
            vnb = (vhat * lng + lnb_ref[...]).astype(BF16)
            dya = dyc[rows, 0:D_A]
            dvn_parts = []
            for j in range(3):
                cols = slice(j * CHUNK, (j + 1) * CHUNK)
                vnb2 = vnb[:, cols]
                mixed = _sgu_mix(vnb2, wcat_ref[j], lo_mask) + bmat_ref[:, cols]
                dya2 = dya[:, cols]
                dpj[rows, cols] = dya2 * mixed * dgu[:, cols]
                dm = dya2 * gu[:, cols]
                dbm_ref[:, cols] += dm
                dmb = dm.astype(BF16)
                zero = jnp.zeros_like(dmb)
                dm_st = jnp.concatenate([jnp.where(lo_mask, dmb, zero), jnp.where(lo_mask, zero, dmb)], axis=0)
                dws_ref[j] += _dot_nt(dm_st, vnb2)
                dvn_st = _dot(wcatt_ref[j], dmb)
                dvn_parts.append(jnp.where(lo_mask, dvn_st[0:CHUNK], dvn_st[CHUNK:2 * CHUNK]))
            dvn = jnp.concatenate(dvn_parts, axis=1)
            dlng_ref[...] += jnp.sum(dvn * vhat, axis=0, keepdims=True)
            dlnb_ref[...] += jnp.sum(dvn, axis=0, keepdims=True)
            dvh = dvn * lng
            dgv_in = rstd * (dvh - _group_mean(dvh, avg) - vhat * _group_mean(dvh * vhat, avg))
            dpj[rows, D_A:2 * D_A] = dgv_in * dgv

        o = 2 * D_A
        live = (ti > 0).astype(F32)
        zb = proj_ref[:, o:o + D_B]
        gb = proj_ref[:, o + D_B:o + 2 * D_B]
        gc = proj_ref[:, o + 2 * D_B:o + 3 * D_B]
        hcur = gc * zb
        hbuf[0:HALO, :] = prev_ref[:, o + 2 * D_B:o + 3 * D_B] * prev_ref[:, o:o + D_B] * live
        hbuf[HALO:HALO + tm, :] = hcur
        hm1 = hbuf[pl.ds(HALO - 1, tm), :]
        hm2 = hbuf[pl.ds(HALO - 2, tm), :]
        y = cw_ref[2:3, :] * hcur + cw_ref[1:2, :] * hm1 + cw_ref[0:1, :] * hm2
        dout = dyc[:, D_A:D_A + D_B]
        dpj[:, o + D_B:o + 2 * D_B] = dout * y
        dy = dout * gb
        dcw_ref[2:3, :] += jnp.sum(dy * hcur, axis=0, keepdims=True)
        dcw_ref[1:2, :] += jnp.sum(dy * hm1, axis=0, keepdims=True)
        dcw_ref[0:1, :] += jnp.sum(dy * hm2, axis=0, keepdims=True)
        dybuf[0:tm, :] = dy
        dh = (cw_ref[2:3, :] * dy + cw_ref[1:2, :] * dybuf[pl.ds(1, tm), :] + cw_ref[0:1, :] * dybuf[pl.ds(2, tm), :])
        dybuf[tm:tm + HALO, :] = dybuf[0:HALO, :]
        dpj[:, o:o + D_B] = dh * gc
        dpj[:, o + 2 * D_B:o + 3 * D_B] = dh * zb

        zc = proj_ref[:, o + 3 * D_B:D_IN]
        zbuf[0:HALO, :] = prev_ref[:, o + 3 * D_B:D_IN] * live
        zbuf[HALO:HALO + tm, :] = zc
        mean, inv = _pool_means(zbuf[...], tm, ti * tm)
        pooled = (mean - zc).astype(BF16)
        dyp = dyc[:, D_A + D_B:D_MODEL]
        ps = ps_ref[...]
        dps_ref[...] += jnp.sum(dyp * _dot(pooled, pw_ref[...]), axis=0, keepdims=True)
        dpw = (dyp * ps).astype(BF16)
        dpw_ref[...] += _dot_tn(pooled, dpw)
        dpooled = _dot(dpw, pwt_ref[...])
        qbuf[0:tm, :] = dpooled * inv
        q = qbuf[...]
        nrows = tm + HALO
        f2 = q + pltpu.roll(q, nrows - 1, 0)
        f4 = f2 + pltpu.roll(f2, nrows - 2, 0)
        f8 = f4 + pltpu.roll(f4, nrows - 4, 0)
        f16 = f8 + pltpu.roll(f8, nrows - 8, 0)
        lane = lax.broadcasted_iota(jnp.int32, (tm, D_C), 1)
        ahead = jnp.where(lane < 64, f2[0:tm], jnp.where(lane < 128, f4[0:tm], jnp.where(lane < 192, f8[0:tm], f16[0:tm])))
        dpj[:, o + 3 * D_B:D_IN] = ahead - dpooled
        qbuf[tm:tm + HALO, :] = qbuf[0:HALO, :]

        dprojb = dpj[...].astype(BF16)
        dproj_ref[...] = dprojb
        dh1 = _dot_nt(dprojb, win_ref[...])
        xv = x_ref[...]
        rs = lax.rsqrt(jnp.mean(xv * xv, axis=-1, keepdims=True) + RMS_EPS)
        n1 = xv * rs
        gv1 = g_ref[...]
        h1_ref[...] = (n1 * gv1).astype(BF16)
        dg_ref[...] += jnp.sum(dh1 * n1, axis=0, keepdims=True)
        dx_ref[...] = dx1v + _rms_bwd(dh1, n1, rs, gv1)

        @pl.when(i == nt - 1)
        def _():
            tril = (lax.broadcasted_iota(jnp.int32, (2 * CHUNK, CHUNK), 0) % CHUNK
                    >= lax.broadcasted_iota(jnp.int32, (2 * CHUNK, CHUNK), 1))
            for j in range(3):
                dws_ref[j] = jnp.where(tril, dws_ref[j], 0.0)
            dbm_ref[...] = _group_mean_split(dbm_ref[...], avg) * float(HEAD_DIM)

    rev = lambda w: pl.BlockSpec((tm, w), lambda i: (nt - 1 - i, 0))
    prev = pl.BlockSpec((HALO, D_IN), lambda i: (jnp.maximum((nt - 1 - i) * prev_blocks - 1, 0), 0))
    acc_shapes = [(1, D_MODEL), (3, 2 * CHUNK, CHUNK), (CHUNK, D_A), (1, D_A), (1, D_A), (8, D_B), (D_C, D_C), (1, D_C)]
    return _tile_call(
        body, "mix_bwd", nt,
        [rev(D_MODEL), rev(D_MODEL), rev(D_IN), prev, _const_spec((1, D_MODEL)), _const_spec((D_MODEL, D_IN)),
         _const_spec((D_MODEL, D_MODEL)), _const_spec((3, CHUNK, 2 * CHUNK)), _const_spec((3, 2 * CHUNK, CHUNK)),
         _const_spec((CHUNK, D_A)), _const_spec((1, D_A)), _const_spec((1, D_A)), _const_spec((D_A, D_A)),
         _const_spec((8, D_B)), _const_spec((D_C, D_C)), _const_spec((D_C, D_C)), _const_spec((1, D_C))],
        [rev(D_MODEL), rev(D_MODEL), rev(D_IN)] + [_acc_spec(s) for s in acc_shapes],
        [jax.ShapeDtypeStruct((t, D_MODEL), F32), jax.ShapeDtypeStruct((t, D_MODEL), BF16),
         jax.ShapeDtypeStruct((t, D_IN), BF16)] + [jax.ShapeDtypeStruct(s, F32) for s in acc_shapes],
        [pltpu.VMEM((tm, D_MODEL), F32), pltpu.VMEM((tm, D_IN), F32),
         pltpu.VMEM((tm + HALO, D_B), F32), pltpu.VMEM((tm + HALO, D_C), F32),
         pltpu.VMEM((tm + HALO, D_B), F32), pltpu.VMEM((tm + HALO, D_C), F32)],
        (dx1, x, proj, proj, g_mix, w_in, w_out, wcat, wcat_t, bmat, ln_g, ln_b, avg, conv_w, pool_bd, pool_bd_t,
         pool_scale), comm)


def wgrad(a, b, *, tk):
    t, m = a.shape
    n = b.shape[1]
    bm = min(m, 1024)
    bn = 1024 if n % 1024 == 0 else n
    nk = t // tk

    def body(a_ref, b_ref, o_ref):
        k = pl.program_id(2)

        @pl.when(k == 0)
        def _():
            o_ref[...] = jnp.zeros_like(o_ref)

        o_ref[...] += _dot_tn(a_ref[...].astype(BF16), b_ref[...].astype(BF16))

    return pl.pallas_call(
        body, name=f"wgrad_{m}x{n}", grid=(m // bm, n // bn, nk),
        in_specs=[pl.BlockSpec((tk, bm), lambda i, j, k: (k, i)), pl.BlockSpec((tk, bn), lambda i, j, k: (k, j))],
        out_specs=pl.BlockSpec((bm, bn), lambda i, j, k: (i, j)),
        out_shape=jax.ShapeDtypeStruct((m, n), F32),
        compiler_params=_params("parallel", "parallel", "arbitrary"),
    )(a, b)


def _row_block(rows, cols, target_bytes):
    target = max(8, target_bytes // (4 * cols))
    if rows <= target:
        return rows
    best = None
    for br in range(8, target + 1, 8):
        if rows % br == 0:
            best = br
    return best if best is not None else rows


def adamw(w, g, m, v):
    shape = w.shape
    cols = shape[-1]
    rows = math.prod(shape[:-1]) if len(shape) > 1 else 1
    br = _row_block(rows, cols, 1 << 20)

    def body(w_ref, g_ref, m_ref, v_ref, d_ref, nm_ref, nv_ref):
        gv = g_ref[...]
        nm = ADAM_B1 * m_ref[...] + (1.0 - ADAM_B1) * gv
        nv = ADAM_B2 * v_ref[...] + (1.0 - ADAM_B2) * jnp.square(gv)
        m_hat = nm / (1.0 - ADAM_B1 ** ADAM_STEP)
        v_hat = nv / (1.0 - ADAM_B2 ** ADAM_STEP)
        d_ref[...] = -ADAM_LR * (m_hat / (jnp.sqrt(v_hat) + ADAM_EPS) + ADAM_WD * w_ref[...])
        nm_ref[...] = nm
        nv_ref[...] = nv

    spec = pl.BlockSpec((br, cols), lambda i: (i, 0))
    outs = pl.pallas_call(
        body, name="adamw", grid=(rows // br,),
        in_specs=[spec] * 4, out_specs=[spec] * 3,
        out_shape=[jax.ShapeDtypeStruct((rows, cols), F32)] * 3,
        compiler_params=pltpu.CompilerParams(dimension_semantics=("parallel",)),
    )(*(a.reshape(rows, cols) for a in (w, g, m, v)))
    return tuple(o.reshape(shape) for o in outs)


def add_own_half(g, other, c_idx):
    _, nchip, rows, cols = g.shape
    br = _row_block(rows, cols, 4 << 20)

    def body(c_ref, g_ref, o_ref, out_ref):
        del c_ref
        out_ref[...] = g_ref[...] + o_ref[...]

    spec = pl.BlockSpec((None, br, cols), lambda j, i, c_ref: (j, i, 0))
    return pl.pallas_call(
        body, name="add_own_half",
        grid_spec=pltpu.PrefetchScalarGridSpec(
            num_scalar_prefetch=1, grid=(nchip, rows // br),
            in_specs=[pl.BlockSpec((None, None, br, cols), lambda j, i, c_ref: (c_ref[0], j, i, 0)), spec],
            out_specs=spec),
        out_shape=jax.ShapeDtypeStruct((nchip, rows, cols), F32),
        compiler_params=_params("parallel", "parallel"),
    )(c_idx, g, other)


def add_chips(parts):
    _, rows, cols = parts.shape
    br = _row_block(rows, cols, 4 << 20)

    def body(p_ref, out_ref):
        out_ref[...] = ((p_ref[0] + p_ref[1]) + p_ref[2]) + p_ref[3]

    return pl.pallas_call(
        body, name="add_chips", grid=(rows // br,),
        in_specs=[pl.BlockSpec((N_CHIPS, br, cols), lambda i: (0, i, 0))],
        out_specs=pl.BlockSpec((br, cols), lambda i: (i, 0)),
        out_shape=jax.ShapeDtypeStruct((rows, cols), F32),
        compiler_params=_params("parallel"),
    )(parts)


def _shard_dims(k, n, axis):
    return (k // N_CHIPS, n) if axis == 0 else (k, n // N_CHIPS)


def _pack_layer_grads(g, extra=None):
    parts = []
    for name, k, n, axis in BIG:
        ks, ns = _shard_dims(k, n, axis)
        a = g[name]
        if axis == 0:
            a = a.reshape(N_CHIPS, 2, ks // 2, n).transpose(1, 0, 2, 3)
        else:
            a = a.reshape(2, k // 2, N_CHIPS, ns).transpose(0, 2, 1, 3)
        parts.append(a.reshape(2, N_CHIPS, -1, LANES))
    if extra is not None:
        parts.append(extra)
    return jnp.concatenate(parts, axis=2)


def _unpack_layer_shard(halves):
    out, off = {}, 0
    for name, k, n, axis in BIG:
        ks, ns = _shard_dims(k, n, axis)
        rows = ks * ns // 2 // LANES
        out[name] = halves[:, off:off + rows].reshape(ks, ns)
        off += rows
    return out, halves[:, off:]


def _round_up(v, m):
    return (v + m - 1) // m * m


def _prep_small(small):
    tril = jnp.tril(jnp.ones((CHUNK, CHUNK), bool))
    wm = jnp.where(tril, small["sgu_w"], 0.0).astype(BF16).reshape(DEPTH, 3, 2, CHUNK, CHUNK)
    head = jnp.arange(D_A) // HEAD_DIM
    grp = jnp.arange(D_C) // HEAD_DIM
    pw_rows = small["pool_w"].reshape(DEPTH, D_C, HEAD_DIM)
    pool_bd = jnp.where((grp[:, None] == grp[None, :])[None], jnp.tile(pw_rows, (1, 1, D_C // HEAD_DIM)), 0.0).astype(BF16)
    return dict(
        wcat=wm.transpose(0, 1, 3, 2, 4).reshape(DEPTH, 3, CHUNK, 2 * CHUNK),
        wcat_t=wm.transpose(0, 1, 2, 4, 3).reshape(DEPTH, 3, 2 * CHUNK, CHUNK),
        bmat=jnp.repeat(jnp.swapaxes(small["sgu_b"], 1, 2), HEAD_DIM, axis=2),
        avg=jnp.where(head[:, None] == head[None, :], 1.0 / HEAD_DIM, 0.0).astype(BF16),
        pool_bd=pool_bd, pool_bd_t=jnp.swapaxes(pool_bd, 1, 2),
        conv8=jnp.pad(small["conv_w"], ((0, 0), (0, 8 - 3), (0, 0))),
    )


def _row(a):
    return a.reshape(1, -1)


MIX_WEIGHTS = ("w_in", "w_out")
MLP_WEIGHTS = ("w_ff1", "w_ff2", "w_ple_gate", "w_ple_proj")
ALL_BIG = MIX_WEIGHTS + MLP_WEIGHTS


def _fwd_layer(h, p_l, wl, small, prep, l, tm, comm_mix=None, comm_mlp=None):
    (proj, ycat, x1), got = mix_fwd(h, _row(small["norm_mix_g"][l]), wl["w_in"], wl["w_out"], prep["wcat"][l],
                                    prep["bmat"][l], _row(small["sgu_ln_g"][l]), _row(small["sgu_ln_b"][l]), prep["avg"],
                                    prep["conv8"][l], prep["pool_bd"][l], _row(small["pool_scale"][l]), tm=tm,
                                    comm=comm_mix)
    if comm_mix is not None:
        wl = {**wl, **_weights_of(got, MLP_WEIGHTS)}
    (a, x2, x3), couts = ffn_ple_fwd(x1, p_l, _row(small["norm_ff_g"][l]), wl["w_ff1"], wl["w_ff2"],
                                     _row(small["norm_ple_g"][l]), wl["w_ple_gate"], wl["w_ple_proj"], tm=tm,
                                     comm=comm_mlp)
    return (h, proj, ycat, x1, a, x2), x3, couts, wl


def _bwd_layer(d, saved, p_l, wl, small, prep, l, tm, tk, comm_a=None, after_a=None, after_b=None):
    xin, proj, ycat, x1, a, x2 = saved
    (dx2, h3, dpre, dpp, dg_ple), ca = ple_bwd(d, x2, p_l, _row(small["norm_ple_g"][l]), wl["w_ple_gate"],
                                               wl["w_ple_proj"], tm=tm, comm=comm_a)
    gb = {"w_ple_gate": wgrad(h3, dpre, tk=tk), "w_ple_proj": wgrad(p_l, dpp, tk=tk)}
    comm_b = after_a(ca) if after_a is not None else None
    (dx1, h2, r, da, dg_ff), cb = ffn_bwd(dx2, x1, a, _row(small["norm_ff_g"][l]), wl["w_ff1"], wl["w_ff2"],
                                          tm=tm // 2, comm=comm_b)
    gb["w_ff2"] = wgrad(r, dx2, tk=tk)
    gb["w_ff1"] = wgrad(h2, da, tk=tk)
    comm_c = after_b(cb) if after_b is not None else None
    (dprev, h1, dproj, dg_mix, dws, dbm, dlng, dlnb, dcw, dpw, dps), cc = mix_bwd(
        dx1, xin, proj, _row(small["norm_mix_g"][l]), wl["w_in"], wl["w_out"], prep["wcat"][l], prep["wcat_t"][l],
        prep["bmat"][l], _row(small["sgu_ln_g"][l]), _row(small["sgu_ln_b"][l]), prep["avg"], prep["conv8"][l],
        prep["pool_bd"][l], prep["pool_bd_t"][l], _row(small["pool_scale"][l]), tm=tm, comm=comm_c)
    gb["w_out"] = wgrad(ycat, dx1, tk=tk)
    gb["w_in"] = wgrad(h1, dproj, tk=tk)
    gs = {
        "norm_ple_g": dg_ple[0], "norm_ff_g": dg_ff[0], "norm_mix_g": dg_mix[0],
        "sgu_w": dws.reshape(2 * 3, CHUNK, CHUNK), "sgu_b": dbm[:, ::HEAD_DIM].T,
        "sgu_ln_g": dlng[0], "sgu_ln_b": dlnb[0], "conv_w": dcw[0:3], "pool_scale": dps[0],
        "pool_w": jnp.stack([dpw[g * HEAD_DIM:(g + 1) * HEAD_DIM, g * HEAD_DIM:(g + 1) * HEAD_DIM]
                             for g in range(D_C // HEAD_DIM)]),
    }
    return dprev, gb, gs, cc


def _local_step(x, p, target, full, small, *, tm, tk):
    prep = _prep_small(small)
    saved, h = [], x
    for l in range(DEPTH):
        wl = {name: full[name][l] for name in full}
        s, h, _, _ = _fwd_layer(h, p[l], wl, small, prep, l, tm)
        saved.append(s)
    loss_blk, d_final_g, d = loss_head(h, target, _row(small["final_g"]), tm=tm)
    gbig, gsm = [None] * DEPTH, [None] * DEPTH
    for l in reversed(range(DEPTH)):
        wl = {name: full[name][l] for name in full}
        d, gbig[l], gsm[l], _ = _bwd_layer(d, saved[l], p[l], wl, small, prep, l, tm, tk)
    big = {name: jnp.stack([gbig[l][name] for l in range(DEPTH)]) for name in gbig[0]}
    sm = {name: jnp.stack([gsm[l][name] for l in range(DEPTH)]) for name in gsm[0]}
    sm["final_g"] = d_final_g[0]
    return loss_blk[0, 0], d, big, sm


def _weights_of(gathered, names):
    wl = dict(zip([b[0] for b in BIG if b[0] in names], gathered))
    if "w_in" in wl:
        wl["w_in"] = wl["w_in"].transpose(1, 0, 2).reshape(D_MODEL, D_IN)
    return wl


def kernel(x, p, norm_mix_g, w_in, sgu_w, sgu_b, sgu_ln_g, sgu_ln_b, conv_w, pool_w, pool_scale, w_out, norm_ff_g, w_ff1, w_ff2, norm_ple_g, w_ple_gate, w_ple_proj, final_g, loss_target, m_norm_mix_g, m_w_in, m_sgu_w, m_sgu_b, m_sgu_ln_g, m_sgu_ln_b, m_conv_w, m_pool_w, m_pool_scale, m_w_out, m_norm_ff_g, m_w_ff1, m_w_ff2, m_norm_ple_g, m_w_ple_gate, m_w_ple_proj, m_final_g, v_norm_mix_g, v_w_in, v_sgu_w, v_sgu_b, v_sgu_ln_g, v_sgu_ln_b, v_conv_w, v_pool_w, v_pool_scale, v_w_out, v_norm_ff_g, v_w_ff1, v_w_ff2, v_norm_ple_g, v_w_ple_gate, v_w_ple_proj, v_final_g):
    args = dict(locals())
    w = {name: args[name] for name in WEIGHTS}
    m = {name: args["m_" + name] for name in WEIGHTS}
    v = {name: args["v_" + name] for name in WEIGHTS}
    t = x.shape[1]
    tm = min(512, t)
    tk = min(2048, t)
    x_idx, y_idx, c_idx = _place()
    chip = 2 * x_idx + y_idx
    c_arr = c_idx.reshape(1).astype(jnp.int32)
    xs, ps, target = x[0], p[:, 0], loss_target[0]

    shards = {name: w[name].astype(BF16) for name, _, _, _ in BIG}
    conv_rows = _round_up(CONV_SHARD, 8 * 128) // 128
    conv_flat = jnp.pad(w["conv_w"].reshape(-1), (0, conv_rows * 128 - CONV_SHARD)).reshape(conv_rows, 128)
    first = _run_comm(_gather_comm(shards, 0, MIX_WEIGHTS, conv_flat), "gather_first")
    conv_full = (first[len(MIX_WEIGHTS)].reshape(N_CHIPS, -1)[:, :CONV_SHARD]
                 .reshape(N_CHIPS, DEPTH, 3, D_B // N_CHIPS).transpose(1, 2, 0, 3).reshape(DEPTH, 3, D_B))
    small = {name: w[name] for name in SMALL}
    small["conv_w"] = conv_full
    prep = _prep_small(small)

    wl = [None] * DEPTH
    wl[0] = _weights_of(first, MIX_WEIGHTS)
    saved, h = [], xs
    for l in range(DEPTH):
        comm_mix = _gather_comm(shards, 0, MLP_WEIGHTS) if l == 0 else None
        comm_mlp = _gather_comm(shards, l + 1, ALL_BIG) if l + 1 < DEPTH else None
        s, h, got, wl[l] = _fwd_layer(h, ps[l], wl[l], small, prep, l, tm, comm_mix, comm_mlp)
        saved.append(s)
        if comm_mlp is not None:
            wl[l + 1] = _weights_of(got, ALL_BIG)

    loss_blk, d_final_g, d = loss_head(h, target, _row(small["final_g"]), tm=tm)

    reduced = [None] * DEPTH
    gsm = [None] * DEPTH
    packed = None
    for l in reversed(range(DEPTH)):
        state = {}
        if packed is None:
            comm_a = after_a = after_b = None
        else:
            above = packed
            comm_a = _swap_comm(above, half=True)

            def after_a(res, above=above):
                return _scatter_comm(add_own_half(above, res[0], c_arr))

            def after_b(res, state=state):
                state["mine"] = add_chips(res[0])
                return _swap_comm(state["mine"])

        d, gb, gsm[l], cc = _bwd_layer(d, saved[l], ps[l], wl[l], small, prep, l, tm, tk, comm_a, after_a, after_b)
        if packed is not None:
            reduced[l + 1] = (state["mine"], cc[0])
        extra = None
        if l == 0:
            sm = {name: jnp.stack([gsm[i][name] for i in range(DEPTH)]) for name in gsm[0]}
            sm["final_g"] = d_final_g[0]
            sizes = [sm[name].size for name in SMALL]
            n_big_rows = sum(math.prod(_shard_dims(k, n, axis)) // 2 // LANES for _, k, n, axis in BIG)
            extra_rows = _round_up(n_big_rows + -(-sum(sizes) // (2 * N_CHIPS * LANES)), 16) - n_big_rows
            small_len = 2 * N_CHIPS * extra_rows * LANES
            small_flat = jnp.pad(jnp.concatenate([sm[name].reshape(-1) for name in SMALL]), (0, small_len - sum(sizes)))
            extra = small_flat.reshape(N_CHIPS, 2, extra_rows, LANES).transpose(1, 0, 2, 3)
        packed = _pack_layer_grads(gb, extra)

    from_sibling = _run_comm(_swap_comm(packed, half=True), "reduce_swap_half")[0]
    landed = _run_comm(_scatter_comm(add_own_half(packed, from_sibling, c_arr)), "reduce_scatter")[0]
    mine = add_chips(landed)
    reduced[0] = (mine, _run_comm(_swap_comm(mine), "reduce_swap")[0])

    per_layer, small_quarter = [], None
    for l in range(DEPTH):
        mine, other = reduced[l]
        halves = jnp.stack([jnp.where(c_idx == 0, mine, other), jnp.where(c_idx == 0, other, mine)])
        shard, rest = _unpack_layer_shard(halves)
        per_layer.append(shard)
        if l == 0:
            small_quarter = rest.reshape(-1, LANES)
    grads = {name: jnp.stack([per_layer[l][name] for l in range(DEPTH)]) for name, _, _, _ in BIG}
    small_red = _run_comm(_allgather_comm(small_quarter), "small_allgather")[0].reshape(-1)
    off = 0
    for name, size in zip(SMALL, sizes):
        grads[name] = small_red[off:off + size].reshape(sm[name].shape)
        off += size
    grads["conv_w"] = lax.dynamic_slice_in_dim(grads["conv_w"], chip * (D_B // N_CHIPS), D_B // N_CHIPS, axis=2)

    loss = lax.psum(loss_blk[0, 0], ("x", "y", "c"))
    delta, new_m, new_v = {}, {}, {}
    for name in WEIGHTS:
        delta[name], new_m[name], new_v[name] = adamw(w[name], grads[name], m[name], v[name])
    return (loss, d[None], *[grads[n] for n in WEIGHTS], *[delta[n] for n in WEIGHTS],
            *[new_m[n] for n in WEIGHTS], *[new_v[n] for n in WEIGHTS])
```

```python
import math

import jax
import jax.numpy as jnp
from jax import lax
from jax.experimental import pallas as pl
from jax.experimental.pallas import tpu as pltpu

F32 = jnp.float32
BF16 = jnp.bfloat16

D_MODEL = 1024
DEPTH = 4
D_PLE = 256
D_FF = 4096
HEAD_DIM = 64
D_A = 384
D_B = 384
D_C = 256
D_IN = 2176
CHUNK = 128
HALO = 16
RMS_EPS = 1e-6
LN_EPS = 1e-5
N_CHIPS = 4
LANES = 1024

ADAM_LR = 0.001
ADAM_B1 = 0.9
ADAM_B2 = 0.999
ADAM_EPS = 1e-08
ADAM_WD = 0.01
ADAM_STEP = 10

VMEM_LIMIT_BYTES = 60 * 1024 * 1024

_RSQRT2 = 0.7071067811865476
_INV_SQRT_2PI = 0.3989422804014327

BIG = (
    ("w_in", D_MODEL, D_IN, 1),
    ("w_out", D_MODEL, D_MODEL, 0),
    ("w_ff1", D_MODEL, D_FF, 1),
    ("w_ff2", D_FF, D_MODEL, 0),
    ("w_ple_gate", D_MODEL, D_MODEL, 0),
    ("w_ple_proj", D_PLE, D_MODEL, 1),
)
SMALL = ("norm_mix_g", "sgu_w", "sgu_b", "sgu_ln_g", "sgu_ln_b", "conv_w", "pool_w", "pool_scale",
         "norm_ff_g", "norm_ple_g", "final_g")
WEIGHTS = ("norm_mix_g", "w_in", "sgu_w", "sgu_b", "sgu_ln_g", "sgu_ln_b", "conv_w", "pool_w", "pool_scale",
           "w_out", "norm_ff_g", "w_ff1", "w_ff2", "norm_ple_g", "w_ple_gate", "w_ple_proj", "final_g")
CONV_SHARD = DEPTH * 3 * (D_B // N_CHIPS)


def _dot(a, b):
    return jnp.dot(a, b, preferred_element_type=F32)


def _dot_nt(a, b):
    return lax.dot_general(a, b, (((1,), (1,)), ((), ())), preferred_element_type=F32)


def _dot_tn(a, b):
    return lax.dot_general(a, b, (((0,), (0,)), ((), ())), preferred_element_type=F32)


def _const_spec(shape):
    nd = len(shape)
    return pl.BlockSpec(shape, lambda i: (0,) * nd, pipeline_mode=pl.Buffered(1))


def _acc_spec(shape):
    nd = len(shape)
    return pl.BlockSpec(shape, lambda i: (0,) * nd)


def _layer_rows(layer, tm):
    return pl.BlockSpec((None, None, tm, D_PLE), lambda i: (layer, 0, i, 0))


def _params(*sem):
    return pltpu.CompilerParams(dimension_semantics=sem, vmem_limit_bytes=VMEM_LIMIT_BYTES)


def _rms_bwd(dh, n, rs, g):
    dn = dh * g
    return rs * (dn - n * jnp.mean(dn * n, axis=-1, keepdims=True))


def _gelu(x):
    return x * (0.5 * (1.0 + lax.erf(x * _RSQRT2)))


def _gelu_and_grad(x):
    cdf = 0.5 * (1.0 + lax.erf(x * _RSQRT2))
    return x * cdf, cdf + x * (jnp.exp(-0.5 * x * x) * _INV_SQRT_2PI)


def _group_mean(v, avg):
    vb = v.astype(BF16)
    split = 2 * CHUNK
    return jnp.concatenate([_dot(vb[:, :split], avg[:split, :split]), _dot(vb[:, split:], avg[split:, split:])], axis=1)


def _group_mean_split(v, avg):
    hi = v.astype(BF16)
    lo = (v - hi.astype(F32)).astype(BF16)
    return _dot(hi, avg) + _dot(lo, avg)


def _lane_lt(shape, bound):
    return lax.broadcasted_iota(jnp.int32, shape, 1) < bound


def _sgu_mix(vnb2, wcat_j, lo_mask):
    zero = jnp.zeros_like(vnb2)
    stacked = jnp.concatenate([jnp.where(lo_mask, vnb2, zero), jnp.where(lo_mask, zero, vnb2)], axis=0)
    return _dot(wcat_j, stacked)


def _pool_means(ext, tile_rows, first_pos):
    s2 = ext + pltpu.roll(ext, 1, 0)
    s4 = s2 + pltpu.roll(s2, 2, 0)
    s8 = s4 + pltpu.roll(s4, 4, 0)
    s16 = s8 + pltpu.roll(s8, 8, 0)
    pos = (first_pos + lax.broadcasted_iota(jnp.int32, (tile_rows, 1), 0) + 1).astype(F32)
    lane = lax.broadcasted_iota(jnp.int32, (tile_rows, D_C), 1)
    sums = jnp.where(lane < 64, s2[HALO:], jnp.where(lane < 128, s4[HALO:], jnp.where(lane < 192, s8[HALO:], s16[HALO:])))
    win = jnp.where(lane < 64, 2.0, jnp.where(lane < 128, 4.0, jnp.where(lane < 192, 8.0, 16.0)))
    inv = 1.0 / jnp.minimum(pos, win)
    return sums * inv, inv


MESH = pl.DeviceIdType.MESH
_ANY = pl.BlockSpec(memory_space=pl.ANY)


def _place():
    return lax.axis_index("x"), lax.axis_index("y"), lax.axis_index("c")


def _chip_peers(x, y):
    return [(1 - x, y), (x, 1 - y), (1 - x, 1 - y)]


class _Comm:
    def __init__(self, ins, out_shapes, sems, copies, aliases=None):
        self.ins, self.out_shapes, self.sems, self.copies = list(ins), list(out_shapes), list(sems), copies
        self.aliases = dict(aliases or {})

    def start(self, in_refs, out_refs, sem_refs):
        local, sends, _ = self.copies(in_refs, out_refs, sem_refs)
        for cp in local + sends:
            cp.start()

    def wait(self, in_refs, out_refs, sem_refs):
        local, sends, recvs = self.copies(in_refs, out_refs, sem_refs)
        for cp in recvs:
            cp.wait_recv()
        for cp in sends:
            cp.wait_send()
        for cp in local:
            cp.wait()


def _remote(src, dst, send_sem, recv_sem, device):
    return pltpu.make_async_remote_copy(src_ref=src, dst_ref=dst, send_sem=send_sem, recv_sem=recv_sem,
                                        device_id=device, device_id_type=MESH)


def _gather_comm(shards, layer, names, conv=None):
    mats = [b for b in BIG if b[0] in names]
    ins = [shards[name] for name, _, _, _ in mats] + ([conv] if conv is not None else [])
    out_shapes = []
    for name, k, n, axis in mats:
        shape = (N_CHIPS, k, n // N_CHIPS) if name == "w_in" else (k, n)
        out_shapes.append(jax.ShapeDtypeStruct(shape, BF16))
    if conv is not None:
        out_shapes.append(jax.ShapeDtypeStruct((N_CHIPS,) + conv.shape, conv.dtype))
    n_arr = len(ins)

    def block(a, out_ref, chip):
        if a == len(mats) or mats[a][0] == "w_in":
            return out_ref.at[chip]
        _, k, n, axis = mats[a]
        if axis == 0:
            return out_ref.at[pl.ds(chip * (k // N_CHIPS), k // N_CHIPS), :]
        return out_ref.at[:, pl.ds(chip * (n // N_CHIPS), n // N_CHIPS)]

    def copies(in_refs, out_refs, sem_refs):
        send_sems, recv_sems, local_sems = sem_refs
        x, y, c = _place()
        me = 2 * x + y
        local, sends, recvs = [], [], []
        for a in range(n_arr):
            src = in_refs[a].at[layer] if a < len(mats) else in_refs[a]
            local.append(pltpu.make_async_copy(src, block(a, out_refs[a], me), local_sems.at[a]))
            for j, (px, py) in enumerate(_chip_peers(x, y)):
                sends.append(_remote(src, block(a, out_refs[a], me), send_sems.at[a, j], recv_sems.at[a, j], (px, py, c)))
                recvs.append(_remote(src, block(a, out_refs[a], 2 * px + py), send_sems.at[a, j], recv_sems.at[a, j],
                                     (px, py, c)))
        return local, sends, recvs

    sems = [pltpu.SemaphoreType.DMA((n_arr, 3)), pltpu.SemaphoreType.DMA((n_arr, 3)), pltpu.SemaphoreType.DMA((n_arr,))]
    return _Comm(ins, out_shapes, sems, copies)


def _allgather_comm(a):
    def copies(in_refs, out_refs, sem_refs):
        send_sems, recv_sems, local_sem = sem_refs
        x, y, c = _place()
        me = 2 * x + y
        local = [pltpu.make_async_copy(in_refs[0], out_refs[0].at[me], local_sem)]
        sends, recvs = [], []
        for j, (px, py) in enumerate(_chip_peers(x, y)):
            sends.append(_remote(in_refs[0], out_refs[0].at[me], send_sems.at[j], recv_sems.at[j], (px, py, c)))
            recvs.append(_remote(in_refs[0], out_refs[0].at[2 * px + py], send_sems.at[j], recv_sems.at[j], (px, py, c)))
        return local, sends, recvs

    sems = [pltpu.SemaphoreType.DMA((3,)), pltpu.SemaphoreType.DMA((3,)), pltpu.SemaphoreType.DMA]
    return _Comm([a], [jax.ShapeDtypeStruct((N_CHIPS,) + a.shape, a.dtype)], sems, copies)


class _Geom:
    def __init__(self, kind, shape, stride=None, width=None):
        self.kind, self.shape, self.stride, self.width = kind, tuple(shape), stride, width
        if kind == "cols":
            k, n = shape
            self.half_shape, self.part_shape, self.final_shape = (k // 2, n), (k // 2, width), (k, width)
        else:
            _, _, h, n = shape
            self.half_shape, self.part_shape, self.final_shape = (N_CHIPS, h, n), (h, n), (2 * h, n)

    def half(self, ref, core):
        if self.kind == "cols":
            return ref.at[pl.ds(core * self.half_shape[0], self.half_shape[0]), :]
        return ref.at[:, core]

    def part(self, ref, chip):
        if self.kind == "cols":
            return ref.at[:, pl.ds(chip * self.stride, self.width)]
        return ref.at[chip]

    def final_half(self, ref, layer, core):
        rows = self.part_shape[0]
        return ref.at[layer, pl.ds(core * rows, rows), :]


def _reduce_a_comm(geoms, arrs):
    n = len(arrs)

    def copies(in_refs, out_refs, sem_refs):
        x, y, c = _place()
        cps = [_remote(geoms[a].half(in_refs[a], 1 - c), out_refs[a], sem_refs[0].at[a], sem_refs[1].at[a], (x, y, 1 - c))
               for a in range(n)]
        return [], cps, cps

    return _Comm(arrs, [jax.ShapeDtypeStruct(g.half_shape, F32) for g in geoms],
                 [pltpu.SemaphoreType.DMA((n,)), pltpu.SemaphoreType.DMA((n,))], copies)


def _reduce_b_comm(geoms, halves):
    n = len(halves)

    def copies(in_refs, out_refs, sem_refs):
        send_sems, recv_sems, local_sems = sem_refs
        x, y, c = _place()
        me = 2 * x + y
        local, sends, recvs = [], [], []
        for a in range(n):
            g = geoms[a]
            local.append(pltpu.make_async_copy(g.part(in_refs[a], me), out_refs[a].at[me], local_sems.at[a]))
            for j, (px, py) in enumerate(_chip_peers(x, y)):
                peer = 2 * px + py
                sends.append(_remote(g.part(in_refs[a], peer), out_refs[a].at[me], send_sems.at[a, j], recv_sems.at[a, j],
                                     (px, py, c)))
                recvs.append(_remote(g.part(in_refs[a], me), out_refs[a].at[peer], send_sems.at[a, j], recv_sems.at[a, j],
                                     (px, py, c)))
        return local, sends, recvs

    sems = [pltpu.SemaphoreType.DMA((n, 3)), pltpu.SemaphoreType.DMA((n, 3)), pltpu.SemaphoreType.DMA((n,))]
    return _Comm(halves, [jax.ShapeDtypeStruct((N_CHIPS,) + g.part_shape, F32) for g in geoms], sems, copies)


def _reduce_c_comm(geoms, parts, finals, layer):
    n = len(parts)

    def copies(in_refs, out_refs, sem_refs):
        send_sems, recv_sems, local_sems = sem_refs
        x, y, c = _place()
        local, sends, recvs = [], [], []
        for a in range(n):
            g = geoms[a]
            local.append(pltpu.make_async_copy(in_refs[a], g.final_half(out_refs[a], layer, c), local_sems.at[a]))
            sends.append(_remote(in_refs[a], g.final_half(out_refs[a], layer, c), send_sems.at[a], recv_sems.at[a],
                                 (x, y, 1 - c)))
            recvs.append(_remote(in_refs[a], g.final_half(out_refs[a], layer, 1 - c), send_sems.at[a], recv_sems.at[a],
                                 (x, y, 1 - c)))
        return local, sends, recvs

    sems = [pltpu.SemaphoreType.DMA((n,)), pltpu.SemaphoreType.DMA((n,)), pltpu.SemaphoreType.DMA((n,))]
    return _Comm(list(parts) + list(finals), [jax.ShapeDtypeStruct(f.shape, f.dtype) for f in finals], sems, copies,
                 aliases={n + a: a for a in range(n)})


def _run_comm(comm, name):
    def body(*refs):
        ni, no = len(comm.ins), len(comm.out_shapes)
        in_refs, out_refs, sem_refs = refs[:ni], refs[ni:ni + no], refs[ni + no:]
        comm.start(in_refs, out_refs, sem_refs)
        comm.wait(in_refs, out_refs, sem_refs)

    return pl.pallas_call(
        body, name=name, in_specs=[_ANY] * len(comm.ins), out_specs=[_ANY] * len(comm.out_shapes),
        out_shape=comm.out_shapes, scratch_shapes=comm.sems, input_output_aliases=comm.aliases,
        compiler_params=pltpu.CompilerParams(has_side_effects=True),
    )(*comm.ins)


def _tile_call(body, name, nt, in_specs, out_specs, out_shape, scratch, args, comm):
    if comm is None:
        outs = pl.pallas_call(body, name=name, grid=(nt,), in_specs=in_specs, out_specs=out_specs, out_shape=out_shape,
                              scratch_shapes=scratch, compiler_params=_params("arbitrary"))(*args)
        return outs, []
    n_in, n_out, n_scr = len(in_specs), len(out_specs), len(scratch)
    ci, co = len(comm.ins), len(comm.out_shapes)

    def hosted(*refs):
        in_refs = refs[:n_in]
        cin = refs[n_in:n_in + ci]
        out_refs = refs[n_in + ci:n_in + ci + n_out]
        cout = refs[n_in + ci + n_out:n_in + ci + n_out + co]
        scr = refs[n_in + ci + n_out + co:n_in + ci + n_out + co + n_scr]
        sems = refs[n_in + ci + n_out + co + n_scr:]
        i = pl.program_id(0)

        @pl.when(i == 0)
        def _():
            comm.start(cin, cout, sems)

        body(*in_refs, *out_refs, *scr)

        @pl.when(i == nt - 1)
        def _():
            comm.wait(cin, cout, sems)

    outs = pl.pallas_call(
        hosted, name=name + "_comm", grid=(nt,),
        in_specs=list(in_specs) + [_ANY] * ci, out_specs=list(out_specs) + [_ANY] * co,
        out_shape=list(out_shape) + comm.out_shapes, scratch_shapes=list(scratch) + comm.sems,
        input_output_aliases={n_in + i: n_out + o for i, o in comm.aliases.items()},
        compiler_params=_params("arbitrary"),
    )(*args, *comm.ins)
    return outs[:n_out], outs[n_out:]


def mix_fwd(x, g_mix, w_in, w_out, wcat, bmat, ln_g, ln_b, avg, conv_w, pool_bd, pool_scale, *, tm, comm=None):
    t = x.shape[0]
    nt = t // tm

    def body(x_ref, g_ref, win_ref, wout_ref, wcat_ref, bmat_ref, lng_ref, lnb_ref, avg_ref, cw_ref, pw_ref, ps_ref,
             proj_ref, ycat_ref, x1_ref, hbuf, zbuf):
        i = pl.program_id(0)

        @pl.when(i == 0)
        def _():
            hbuf[0:HALO, :] = jnp.zeros((HALO, D_B), F32)
            zbuf[0:HALO, :] = jnp.zeros((HALO, D_C), F32)

        xv = x_ref[...]
        n = xv * lax.rsqrt(jnp.mean(xv * xv, axis=-1, keepdims=True) + RMS_EPS)
        h1 = (n * g_ref[...]).astype(BF16)
        proj_ref[...] = _dot(h1, win_ref[...])

        lo_mask = _lane_lt((CHUNK, CHUNK), HEAD_DIM)
        avg = avg_ref[...]
        for c in range(tm // CHUNK):
            rows = pl.ds(c * CHUNK, CHUNK)
            gu = _gelu(proj_ref[rows, 0:D_A])
            gv = _gelu(proj_ref[rows, D_A:2 * D_A])
            dv = gv - _group_mean(gv, avg)
            var = _group_mean(dv * dv, avg)
            vnb = (dv * lax.rsqrt(var + LN_EPS) * lng_ref[...] + lnb_ref[...]).astype(BF16)
            for j in range(3):
                cols = slice(j * CHUNK, (j + 1) * CHUNK)
                mixed = _sgu_mix(vnb[:, cols], wcat_ref[j], lo_mask) + bmat_ref[:, cols]
                ycat_ref[rows, cols] = (gu[:, cols] * mixed).astype(BF16)

        o = 2 * D_A
        hcur = proj_ref[:, o + 2 * D_B:o + 3 * D_B] * proj_ref[:, o:o + D_B]
        hbuf[HALO:HALO + tm, :] = hcur
        y = (cw_ref[2:3, :] * hcur + cw_ref[1:2, :] * hbuf[pl.ds(HALO - 1, tm), :]
             + cw_ref[0:1, :] * hbuf[pl.ds(HALO - 2, tm), :])
        ycat_ref[:, D_A:D_A + D_B] = (proj_ref[:, o + D_B:o + 2 * D_B] * y).astype(BF16)
        hbuf[0:HALO, :] = hbuf[tm:tm + HALO, :]

        zc = proj_ref[:, o + 3 * D_B:D_IN]
        zbuf[HALO:HALO + tm, :] = zc
        mean, _ = _pool_means(zbuf[...], tm, i * tm)
        pooled = (mean - zc).astype(BF16)
        ycat_ref[:, D_A + D_B:D_MODEL] = (_dot(pooled, pw_ref[...]) * ps_ref[...]).astype(BF16)
        zbuf[0:HALO, :] = zbuf[tm:tm + HALO, :]

        x1_ref[...] = xv + _dot(ycat_ref[...], wout_ref[...])

    row = lambda w: pl.BlockSpec((tm, w), lambda i: (i, 0))
    return _tile_call(
        body, "mix_fwd", nt,
        [row(D_MODEL), _const_spec((1, D_MODEL)), _const_spec((D_MODEL, D_IN)), _const_spec((D_MODEL, D_MODEL)),
         _const_spec((3, CHUNK, 2 * CHUNK)), _const_spec((CHUNK, D_A)), _const_spec((1, D_A)), _const_spec((1, D_A)),
         _const_spec((D_A, D_A)), _const_spec((8, D_B)), _const_spec((D_C, D_C)), _const_spec((1, D_C))],
        [row(D_IN), row(D_MODEL), row(D_MODEL)],
        [jax.ShapeDtypeStruct((t, D_IN), F32), jax.ShapeDtypeStruct((t, D_MODEL), BF16),
         jax.ShapeDtypeStruct((t, D_MODEL), F32)],
        [pltpu.VMEM((tm + HALO, D_B), F32), pltpu.VMEM((tm + HALO, D_C), F32)],
        (x, g_mix, w_in, w_out, wcat, bmat, ln_g, ln_b, avg, conv_w, pool_bd, pool_scale), comm)


def ffn_ple_fwd(x1, p, g_ff, w_ff1, w_ff2, g_ple, w_gate, w_proj, *, tm, comm=None):
    t = x1.shape[0]
    nt = t // tm
    nc = D_FF // D_MODEL

    def body(x1_ref, p_ref, gff_ref, w1_ref, w2_ref, gple_ref, wg_ref, wp_ref, a_ref, x2_ref, x3_ref):
        x1v = x1_ref[...]
        n2 = x1v * lax.rsqrt(jnp.mean(x1v * x1v, axis=-1, keepdims=True) + RMS_EPS)
        h2 = (n2 * gff_ref[...]).astype(BF16)
        acc = x1v
        for c in range(nc):
            cols = slice(c * D_MODEL, (c + 1) * D_MODEL)
            a = _dot(h2, w1_ref[:, cols])
            a_ref[:, cols] = a.astype(BF16)
            ra = jnp.maximum(a, 0.0)
            acc = acc + _dot((ra * ra).astype(BF16), w2_ref[cols, :])
        x2_ref[...] = acc
        n3 = acc * lax.rsqrt(jnp.mean(acc * acc, axis=-1, keepdims=True) + RMS_EPS)
        h3 = (n3 * gple_ref[...]).astype(BF16)
        gate = jax.nn.sigmoid(_dot(h3, wg_ref[...]))
        pp = _dot(p_ref[...].astype(BF16), wp_ref[...])
        x3_ref[...] = acc + pp * gate

    row = lambda w: pl.BlockSpec((tm, w), lambda i: (i, 0))
    return _tile_call(
        body, "ffn_ple_fwd", nt,
        [row(D_MODEL), _layer_rows(p[1], tm), _const_spec((1, D_MODEL)), _const_spec((D_MODEL, D_FF)),
         _const_spec((D_FF, D_MODEL)), _const_spec((1, D_MODEL)), _const_spec((D_MODEL, D_MODEL)),
         _const_spec((D_PLE, D_MODEL))],
        [row(D_FF), row(D_MODEL), row(D_MODEL)],
        [jax.ShapeDtypeStruct((t, D_FF), BF16), jax.ShapeDtypeStruct((t, D_MODEL), F32),
         jax.ShapeDtypeStruct((t, D_MODEL), F32)],
        [], (x1, p[0], g_ff, w_ff1, w_ff2, g_ple, w_gate, w_proj), comm)


def loss_head(x, target, g, *, tm):
    t = x.shape[0]
    nt = t // tm

    def body(x_ref, t_ref, g_ref, loss_ref, dg_ref, dx_ref, sq_acc):
        i = pl.program_id(0)

        @pl.when(i == 0)
        def _():
            sq_acc[...] = jnp.zeros_like(sq_acc)
            dg_ref[...] = jnp.zeros_like(dg_ref)

        xv = x_ref[...]
        rs = lax.rsqrt(jnp.mean(xv * xv, axis=-1, keepdims=True) + RMS_EPS)
        n = xv * rs
        gv = g_ref[...]
        err = n * gv - t_ref[...]
        sq_acc[...] += jnp.sum(err * err, axis=0, keepdims=True)
        dy = err * (1.0 / D_MODEL)
        dg_ref[...] += jnp.sum(dy * n, axis=0, keepdims=True)
        dx_ref[...] = _rms_bwd(dy, n, rs, gv)

        @pl.when(i == nt - 1)
        def _():
            total = jnp.sum(sq_acc[...], axis=1, keepdims=True) * (0.5 / D_MODEL)
            loss_ref[...] = jnp.broadcast_to(total, loss_ref.shape)

    row = pl.BlockSpec((tm, D_MODEL), lambda i: (i, 0))
    return pl.pallas_call(
        body, name="loss_head", grid=(nt,),
        in_specs=[row, row, _const_spec((1, D_MODEL))],
        out_specs=[_acc_spec((8, 128)), _acc_spec((1, D_MODEL)), row],
        out_shape=[jax.ShapeDtypeStruct((8, 128), F32), jax.ShapeDtypeStruct((1, D_MODEL), F32),
                   jax.ShapeDtypeStruct((t, D_MODEL), F32)],
        scratch_shapes=[pltpu.VMEM((1, D_MODEL), F32)],
        compiler_params=_params("arbitrary"),
    )(x, target, g)


def ple_bwd(d, x2, p, g_ple, w_gate, w_proj, *, tm, comm=None):
    t = d.shape[0]
    nt = t // tm

    def body(d_ref, x2_ref, p_ref, g_ref, wg_ref, wp_ref, dx2_ref, h3_ref, dpre_ref, dpp_ref, dg_ref):
        i = pl.program_id(0)

        @pl.when(i == 0)
        def _():
            dg_ref[...] = jnp.zeros_like(dg_ref)

        dv = d_ref[...]
        x2v = x2_ref[...]
        rs = lax.rsqrt(jnp.mean(x2v * x2v, axis=-1, keepdims=True) + RMS_EPS)
        n3 = x2v * rs
        gv = g_ref[...]
        h3 = (n3 * gv).astype(BF16)
        h3_ref[...] = h3
        gate = jax.nn.sigmoid(_dot(h3, wg_ref[...]))
        pp = _dot(p_ref[...].astype(BF16), wp_ref[...])
        dpp_ref[...] = (dv * gate).astype(BF16)
        dpre = (dv * pp * gate * (1.0 - gate)).astype(BF16)
        dpre_ref[...] = dpre
        dh3 = _dot_nt(dpre, wg_ref[...])
        dg_ref[...] += jnp.sum(dh3 * n3, axis=0, keepdims=True)
        dx2_ref[...] = dv + _rms_bwd(dh3, n3, rs, gv)

    row = lambda w: pl.BlockSpec((tm, w), lambda i: (i, 0))
    return _tile_call(
        body, "ple_bwd", nt,
        [row(D_MODEL), row(D_MODEL), _layer_rows(p[1], tm), _const_spec((1, D_MODEL)), _const_spec((D_MODEL, D_MODEL)),
         _const_spec((D_PLE, D_MODEL))],
        [row(D_MODEL), row(D_MODEL), row(D_MODEL), row(D_MODEL), _acc_spec((1, D_MODEL))],
        [jax.ShapeDtypeStruct((t, D_MODEL), F32), jax.ShapeDtypeStruct((t, D_MODEL), BF16),
         jax.ShapeDtypeStruct((t, D_MODEL), BF16), jax.ShapeDtypeStruct((t, D_MODEL), BF16),
         jax.ShapeDtypeStruct((1, D_MODEL), F32)],
        [], (d, x2, p[0], g_ple, w_gate, w_proj), comm)


def ffn_bwd(dx2, x1, a, g_ff, w_ff1, w_ff2, *, tm, comm=None):
    t = dx2.shape[0]
    nt = t // tm
    nc = D_FF // D_MODEL

    def body(dx2_ref, x1_ref, a_ref, g_ref, w1_ref, w2_ref, dx1_ref, h2_ref, r_ref, da_ref, dg_ref):
        i = pl.program_id(0)

        @pl.when(i == 0)
        def _():
            dg_ref[...] = jnp.zeros_like(dg_ref)

        dv = dx2_ref[...]
        x1v = x1_ref[...]
        rs = lax.rsqrt(jnp.mean(x1v * x1v, axis=-1, keepdims=True) + RMS_EPS)
        n2 = x1v * rs
        gv = g_ref[...]
        h2_ref[...] = (n2 * gv).astype(BF16)
        dvb = dv.astype(BF16)
        dh2 = jnp.zeros((tm, D_MODEL), F32)
        for c in range(nc):
            cols = slice(c * D_MODEL, (c + 1) * D_MODEL)
            ra = jnp.maximum(a_ref[:, cols].astype(F32), 0.0)
            r_ref[:, cols] = (ra * ra).astype(BF16)
            da = (_dot_nt(dvb, w2_ref[cols, :]) * (2.0 * ra)).astype(BF16)
            da_ref[:, cols] = da
            dh2 = dh2 + _dot_nt(da, w1_ref[:, cols])
        dg_ref[...] += jnp.sum(dh2 * n2, axis=0, keepdims=True)
        dx1_ref[...] = dv + _rms_bwd(dh2, n2, rs, gv)

    row = lambda w: pl.BlockSpec((tm, w), lambda i: (i, 0))
    return _tile_call(
        body, "ffn_bwd", nt,
        [row(D_MODEL), row(D_MODEL), row(D_FF), _const_spec((1, D_MODEL)), _const_spec((D_MODEL, D_FF)),
         _const_spec((D_FF, D_MODEL))],
        [row(D_MODEL), row(D_MODEL), row(D_FF), row(D_FF), _acc_spec((1, D_MODEL))],
        [jax.ShapeDtypeStruct((t, D_MODEL), F32), jax.ShapeDtypeStruct((t, D_MODEL), BF16),
         jax.ShapeDtypeStruct((t, D_FF), BF16), jax.ShapeDtypeStruct((t, D_FF), BF16),
         jax.ShapeDtypeStruct((1, D_MODEL), F32)],
        [], (dx2, x1, a, g_ff, w_ff1, w_ff2), comm)


def mix_bwd(dx1, x, proj, g_mix, w_in, w_out, wcat, wcat_t, bmat, ln_g, ln_b, avg, conv_w, pool_bd, pool_bd_t,
            pool_scale, *, tm, comm=None):
    t = dx1.shape[0]
    nt = t // tm
    prev_blocks = tm // HALO

    def body(dx1_ref, x_ref, proj_ref, prev_ref, g_ref, win_ref, wout_ref, wcat_ref, wcatt_ref, bmat_ref, lng_ref,
             lnb_ref, avg_ref, cw_ref, pw_ref, pwt_ref, ps_ref,
             dx_ref, h1_ref, dproj_ref, dg_ref, dws_ref, dbm_ref, dlng_ref, dlnb_ref, dcw_ref, dpw_ref, dps_ref,
             dyc, dpj, hbuf, zbuf, dybuf, qbuf):
        i = pl.program_id(0)
        ti = nt - 1 - i

        @pl.when(i == 0)
        def _():
            for ref in (dg_ref, dws_ref, dbm_ref, dlng_ref, dlnb_ref, dcw_ref, dpw_ref, dps_ref):
                ref[...] = jnp.zeros_like(ref)
            dybuf[tm:tm + HALO, :] = jnp.zeros((HALO, D_B), F32)
            qbuf[tm:tm + HALO, :] = jnp.zeros((HALO, D_C), F32)

        dx1v = dx1_ref[...]
        dyc[...] = _dot_nt(dx1v.astype(BF16), wout_ref[...])

        lo_mask = _lane_lt((CHUNK, CHUNK), HEAD_DIM)
        avg = avg_ref[...]
        lng = lng_ref[...]
        for c in range(tm // CHUNK):
            rows = pl.ds(c * CHUNK, CHUNK)
            gu, dgu = _gelu_and_grad(proj_ref[rows, 0:D_A])
            gv, dgv = _gelu_and_grad(proj_ref[rows, D_A:2 * D_A])
            cen = gv - _group_mean(gv, avg)
            rstd = lax.rsqrt(_group_mean(cen * cen, avg) + LN_EPS)
            vhat = cen * rstd
            vnb = (vhat * lng + lnb_ref[...]).astype(BF16)
            dya = dyc[rows, 0:D_A]
            dvn_parts = []
            for j in range(3):
                cols = slice(j * CHUNK, (j + 1) * CHUNK)
                vnb2 = vnb[:, cols]
                mixed = _sgu_mix(vnb2, wcat_ref[j], lo_mask) + bmat_ref[:, cols]
                dya2 = dya[:, cols]
                dpj[rows, cols] = dya2 * mixed * dgu[:, cols]
                dm = dya2 * gu[:, cols]
                dbm_ref[:, cols] += dm
                dmb = dm.astype(BF16)
                zero = jnp.zeros_like(dmb)
                dm_st = jnp.concatenate([jnp.where(lo_mask, dmb, zero), jnp.where(lo_mask, zero, dmb)], axis=0)
                dws_ref[j] += _dot_nt(dm_st, vnb2)
                dvn_st = _dot(wcatt_ref[j], dmb)
                dvn_parts.append(jnp.where(lo_mask, dvn_st[0:CHUNK], dvn_st[CHUNK:2 * CHUNK]))
            dvn = jnp.concatenate(dvn_parts, axis=1)
            dlng_ref[...] += jnp.sum(dvn * vhat, axis=0, keepdims=True)
            dlnb_ref[...] += jnp.sum(dvn, axis=0, keepdims=True)
            dvh = dvn * lng
            dgv_in = rstd * (dvh - _group_mean(dvh, avg) - vhat * _group_mean(dvh * vhat, avg))
            dpj[rows, D_A:2 * D_A] = dgv_in * dgv

        o = 2 * D_A
        live = (ti > 0).astype(F32)
        zb = proj_ref[:, o:o + D_B]
        gb = proj_ref[:, o + D_B:o + 2 * D_B]
        gc = proj_ref[:, o + 2 * D_B:o + 3 * D_B]
        hcur = gc * zb
        hbuf[0:HALO, :] = prev_ref[:, o + 2 * D_B:o + 3 * D_B] * prev_ref[:, o:o + D_B] * live
        hbuf[HALO:HALO + tm, :] = hcur
        hm1 = hbuf[pl.ds(HALO - 1, tm), :]
        hm2 = hbuf[pl.ds(HALO - 2, tm), :]
        y = cw_ref[2:3, :] * hcur + cw_ref[1:2, :] * hm1 + cw_ref[0:1, :] * hm2
        dout = dyc[:, D_A:D_A + D_B]
        dpj[:, o + D_B:o + 2 * D_B] = dout * y
        dy = dout * gb
        dcw_ref[2:3, :] += jnp.sum(dy * hcur, axis=0, keepdims=True)
        dcw_ref[1:2, :] += jnp.sum(dy * hm1, axis=0, keepdims=True)
        dcw_ref[0:1, :] += jnp.sum(dy * hm2, axis=0, keepdims=True)
        dybuf[0:tm, :] = dy
        dh = (cw_ref[2:3, :] * dy + cw_ref[1:2, :] * dybuf[pl.ds(1, tm), :] + cw_ref[0:1, :] * dybuf[pl.ds(2, tm), :])
        dybuf[tm:tm + HALO, :] = dybuf[0:HALO, :]
        dpj[:, o:o + D_B] = dh * gc
        dpj[:, o + 2 * D_B:o + 3 * D_B] = dh * zb

        zc = proj_ref[:, o + 3 * D_B:D_IN]
        zbuf[0:HALO, :] = prev_ref[:, o + 3 * D_B:D_IN] * live
        zbuf[HALO:HALO + tm, :] = zc
        mean, inv = _pool_means(zbuf[...], tm, ti * tm)
        pooled = (mean - zc).astype(BF16)
        dyp = dyc[:, D_A + D_B:D_MODEL]
        ps = ps_ref[...]
        dps_ref[...] += jnp.sum(dyp * _dot(pooled, pw_ref[...]), axis=0, keepdims=True)
        dpw = (dyp * ps).astype(BF16)
        dpw_ref[...] += _dot_tn(pooled, dpw)
        dpooled = _dot(dpw, pwt_ref[...])
        qbuf[0:tm, :] = dpooled * inv
        q = qbuf[...]
        nrows = tm + HALO
        f2 = q + pltpu.roll(q, nrows - 1, 0)
        f4 = f2 + pltpu.roll(f2, nrows - 2, 0)
        f8 = f4 + pltpu.roll(f4, nrows - 4, 0)
        f16 = f8 + pltpu.roll(f8, nrows - 8, 0)
        lane = lax.broadcasted_iota(jnp.int32, (tm, D_C), 1)
        ahead = jnp.where(lane < 64, f2[0:tm], jnp.where(lane < 128, f4[0:tm], jnp.where(lane < 192, f8[0:tm], f16[0:tm])))
        dpj[:, o + 3 * D_B:D_IN] = ahead - dpooled
        qbuf[tm:tm + HALO, :] = qbuf[0:HALO, :]

        dprojb = dpj[...].astype(BF16)
        dproj_ref[...] = dprojb
        dh1 = _dot_nt(dprojb, win_ref[...])
        xv = x_ref[...]
        rs = lax.rsqrt(jnp.mean(xv * xv, axis=-1, keepdims=True) + RMS_EPS)
        n1 = xv * rs
        gv1 = g_ref[...]
        h1_ref[...] = (n1 * gv1).astype(BF16)
        dg_ref[...] += jnp.sum(dh1 * n1, axis=0, keepdims=True)
        dx_ref[...] = dx1v + _rms_bwd(dh1, n1, rs, gv1)

        @pl.when(i == nt - 1)
        def _():
            tril = (lax.broadcasted_iota(jnp.int32, (2 * CHUNK, CHUNK), 0) % CHUNK
                    >= lax.broadcasted_iota(jnp.int32, (2 * CHUNK, CHUNK), 1))
            for j in range(3):
                dws_ref[j] = jnp.where(tril, dws_ref[j], 0.0)
            dbm_ref[...] = _group_mean_split(dbm_ref[...], avg) * float(HEAD_DIM)

    rev = lambda w: pl.BlockSpec((tm, w), lambda i: (nt - 1 - i, 0))
    prev = pl.BlockSpec((HALO, D_IN), lambda i: (jnp.maximum((nt - 1 - i) * prev_blocks - 1, 0), 0))
    acc_shapes = [(1, D_MODEL), (3, 2 * CHUNK, CHUNK), (CHUNK, D_A), (1, D_A), (1, D_A), (8, D_B), (D_C, D_C), (1, D_C)]
    return _tile_call(
        body, "mix_bwd", nt,
        [rev(D_MODEL), rev(D_MODEL), rev(D_IN), prev, _const_spec((1, D_MODEL)), _const_spec((D_MODEL, D_IN)),
         _const_spec((D_MODEL, D_MODEL)), _const_spec((3, CHUNK, 2 * CHUNK)), _const_spec((3, 2 * CHUNK, CHUNK)),
         _const_spec((CHUNK, D_A)), _const_spec((1, D_A)), _const_spec((1, D_A)), _const_spec((D_A, D_A)),
         _const_spec((8, D_B)), _const_spec((D_C, D_C)), _const_spec((D_C, D_C)), _const_spec((1, D_C))],
        [rev(D_MODEL), rev(D_MODEL), rev(D_IN)] + [_acc_spec(s) for s in acc_shapes],
        [jax.ShapeDtypeStruct((t, D_MODEL), F32), jax.ShapeDtypeStruct((t, D_MODEL), BF16),
         jax.ShapeDtypeStruct((t, D_IN), BF16)] + [jax.ShapeDtypeStruct(s, F32) for s in acc_shapes],
        [pltpu.VMEM((tm, D_MODEL), F32), pltpu.VMEM((tm, D_IN), F32),
         pltpu.VMEM((tm + HALO, D_B), F32), pltpu.VMEM((tm + HALO, D_C), F32),
         pltpu.VMEM((tm + HALO, D_B), F32), pltpu.VMEM((tm + HALO, D_C), F32)],
        (dx1, x, proj, proj, g_mix, w_in, w_out, wcat, wcat_t, bmat, ln_g, ln_b, avg, conv_w, pool_bd, pool_bd_t,
         pool_scale), comm)


def wgrad(a, b, *, tk, a_layer=None):
    t, m = a.shape[-2:]
    n = b.shape[1]
    bm = min(m, 1024)
    bn = 1024 if n % 1024 == 0 else n
    nk = t // tk
    if a_layer is None:
        a_spec = pl.BlockSpec((tk, bm), lambda i, j, k: (k, i))
    else:
        a_spec = pl.BlockSpec((None, None, tk, bm), lambda i, j, k: (a_layer, 0, k, i))

    def body(a_ref, b_ref, o_ref):
        k = pl.program_id(2)

        @pl.when(k == 0)
        def _():
            o_ref[...] = jnp.zeros_like(o_ref)

        o_ref[...] += _dot_tn(a_ref[...].astype(BF16), b_ref[...].astype(BF16))

    return pl.pallas_call(
        body, name=f"wgrad_{m}x{n}", grid=(m // bm, n // bn, nk),
        in_specs=[a_spec, pl.BlockSpec((tk, bn), lambda i, j, k: (k, j))],
        out_specs=pl.BlockSpec((bm, bn), lambda i, j, k: (i, j)),
        out_shape=jax.ShapeDtypeStruct((m, n), F32),
        compiler_params=_params("parallel", "parallel", "arbitrary"),
    )(a, b)


def _row_block(rows, cols, target_bytes):
    target = max(8, target_bytes // (4 * cols))
    if rows <= target:
        return rows
    best = None
    for br in range(8, target + 1, 8):
        if rows % br == 0:
            best = br
    return best if best is not None else rows


def adamw(w, g, m, v):
    shape = w.shape
    cols = shape[-1]
    rows = math.prod(shape[:-1]) if len(shape) > 1 else 1
    br = _row_block(rows, cols, 1 << 20)

    def body(w_ref, g_ref, m_ref, v_ref, d_ref, nm_ref, nv_ref):
        gv = g_ref[...]
        nm = ADAM_B1 * m_ref[...] + (1.0 - ADAM_B1) * gv
        nv = ADAM_B2 * v_ref[...] + (1.0 - ADAM_B2) * jnp.square(gv)
        m_hat = nm / (1.0 - ADAM_B1 ** ADAM_STEP)
        v_hat = nv / (1.0 - ADAM_B2 ** ADAM_STEP)
        d_ref[...] = -ADAM_LR * (m_hat / (jnp.sqrt(v_hat) + ADAM_EPS) + ADAM_WD * w_ref[...])
        nm_ref[...] = nm
        nv_ref[...] = nv

    spec = pl.BlockSpec((br, cols), lambda i: (i, 0))
    outs = pl.pallas_call(
        body, name="adamw", grid=(rows // br,),
        in_specs=[spec] * 4, out_specs=[spec] * 3,
        out_shape=[jax.ShapeDtypeStruct((rows, cols), F32)] * 3,
        compiler_params=pltpu.CompilerParams(dimension_semantics=("parallel",)),
    )(*(a.reshape(rows, cols) for a in (w, g, m, v)))
    return tuple(o.reshape(shape) for o in outs)


ADD_STEPS = 4


def add_halves(geoms, arrs, received, c_idx):
    n = len(arrs)

    def body(c_ref, *refs):
        del c_ref
        for a in range(n):
            refs[2 * n + a][...] = refs[a][...] + refs[n + a][...]

    own_specs, half_specs = [], []
    for g in geoms:
        if g.kind == "cols":
            rows, cols = g.half_shape[0] // ADD_STEPS, g.half_shape[1]
            own_specs.append(pl.BlockSpec((rows, cols), lambda i, c_ref: (ADD_STEPS * c_ref[0] + i, 0)))
            half_specs.append(pl.BlockSpec((rows, cols), lambda i, c_ref: (i, 0)))
        else:
            _, h, cols = g.half_shape
            own_specs.append(pl.BlockSpec((None, None, h, cols), lambda i, c_ref: (i, c_ref[0], 0, 0)))
            half_specs.append(pl.BlockSpec((None, h, cols), lambda i, c_ref: (i, 0, 0)))
    return pl.pallas_call(
        body, name="add_halves",
        grid_spec=pltpu.PrefetchScalarGridSpec(num_scalar_prefetch=1, grid=(ADD_STEPS,),
                                               in_specs=own_specs + half_specs, out_specs=half_specs),
        out_shape=[jax.ShapeDtypeStruct(g.half_shape, F32) for g in geoms],
        compiler_params=_params("parallel"),
    )(c_idx, *arrs, *received)


def add_parts(geoms, landed):
    n = len(landed)

    def body(*refs):
        for a in range(n):
            p_ref = refs[a]
            refs[n + a][...] = ((p_ref[0] + p_ref[1]) + p_ref[2]) + p_ref[3]

    in_specs, out_specs = [], []
    for g in geoms:
        rows, cols = g.part_shape[0] // ADD_STEPS, g.part_shape[1]
        in_specs.append(pl.BlockSpec((N_CHIPS, rows, cols), lambda i: (0, i, 0)))
        out_specs.append(pl.BlockSpec((rows, cols), lambda i: (i, 0)))
    return pl.pallas_call(
        body, name="add_parts", grid=(ADD_STEPS,), in_specs=in_specs, out_specs=out_specs,
        out_shape=[jax.ShapeDtypeStruct(g.part_shape, F32) for g in geoms],
        compiler_params=_params("parallel"),
    )(*landed)


def _shard_dims(k, n, axis):
    return (k // N_CHIPS, n) if axis == 0 else (k, n // N_CHIPS)


W_IN_STRIDE = 512
W_IN_WINDOW = 640


def _big_geoms():
    geoms = []
    for name, k, n, axis in BIG:
        if axis == 0:
            geoms.append(_Geom("rows", (N_CHIPS, 2, k // N_CHIPS // 2, n)))
        elif name == "w_in":
            geoms.append(_Geom("cols", (k, n), W_IN_STRIDE, W_IN_WINDOW))
        else:
            geoms.append(_Geom("cols", (k, n), n // N_CHIPS, n // N_CHIPS))
    return geoms


def _grad_views(gb, geoms):
    return [gb[name].reshape(g.shape) for (name, _, _, _), g in zip(BIG, geoms)]


def _round_up(v, m):
    return (v + m - 1) // m * m


def _prep_small(small):
    tril = jnp.tril(jnp.ones((CHUNK, CHUNK), bool))
    wm = jnp.where(tril, small["sgu_w"], 0.0).astype(BF16).reshape(DEPTH, 3, 2, CHUNK, CHUNK)
    head = jnp.arange(D_A) // HEAD_DIM
    grp = jnp.arange(D_C) // HEAD_DIM
    pw_rows = small["pool_w"].reshape(DEPTH, D_C, HEAD_DIM)
    pool_bd = jnp.where((grp[:, None] == grp[None, :])[None], jnp.tile(pw_rows, (1, 1, D_C // HEAD_DIM)), 0.0).astype(BF16)
    return dict(
        wcat=wm.transpose(0, 1, 3, 2, 4).reshape(DEPTH, 3, CHUNK, 2 * CHUNK),
        wcat_t=wm.transpose(0, 1, 2, 4, 3).reshape(DEPTH, 3, 2 * CHUNK, CHUNK),
        bmat=jnp.repeat(jnp.swapaxes(small["sgu_b"], 1, 2), HEAD_DIM, axis=2),
        avg=jnp.where(head[:, None] == head[None, :], 1.0 / HEAD_DIM, 0.0).astype(BF16),
        pool_bd=pool_bd, pool_bd_t=jnp.swapaxes(pool_bd, 1, 2),
        conv8=jnp.pad(small["conv_w"], ((0, 0), (0, 8 - 3), (0, 0))),
    )


def _row(a):
    return a.reshape(1, -1)


MIX_WEIGHTS = ("w_in", "w_out")
MLP_WEIGHTS = ("w_ff1", "w_ff2", "w_ple_gate", "w_ple_proj")
ALL_BIG = MIX_WEIGHTS + MLP_WEIGHTS


def _fwd_layer(h, p, wl, small, prep, l, tm, comm_mix=None, comm_mlp=None):
    (proj, ycat, x1), got = mix_fwd(h, _row(small["norm_mix_g"][l]), wl["w_in"], wl["w_out"], prep["wcat"][l],
                                    prep["bmat"][l], _row(small["sgu_ln_g"][l]), _row(small["sgu_ln_b"][l]), prep["avg"],
                                    prep["conv8"][l], prep["pool_bd"][l], _row(small["pool_scale"][l]), tm=tm,
                                    comm=comm_mix)
    if comm_mix is not None:
        wl = {**wl, **_weights_of(got, MLP_WEIGHTS)}
    (a, x2, x3), couts = ffn_ple_fwd(x1, (p, l), _row(small["norm_ff_g"][l]), wl["w_ff1"], wl["w_ff2"],
                                     _row(small["norm_ple_g"][l]), wl["w_ple_gate"], wl["w_ple_proj"], tm=tm,
                                     comm=comm_mlp)
    return (h, proj, ycat, x1, a, x2), x3, couts, wl


class _Reducer:
    def __init__(self, finals, c_arr):
        self.finals, self.c_arr = list(finals), c_arr
        self.pending = None

    def push(self, layer, geoms, arrs):
        self.pending = (layer, geoms, arrs)

    def comm_a(self):
        return None if self.pending is None else _reduce_a_comm(self.pending[1], self.pending[2])

    def comm_b(self, received):
        _, geoms, arrs = self.pending
        return _reduce_b_comm(geoms, add_halves(geoms, arrs, received, self.c_arr))

    def comm_c(self, landed):
        layer, geoms, _ = self.pending
        n = len(geoms)
        return _reduce_c_comm(geoms, add_parts(geoms, landed), self.finals[:n], layer)

    def done(self, finals):
        self.finals[:len(finals)] = list(finals)
        self.pending = None

    def flush(self, tag):
        received = _run_comm(self.comm_a(), "reduce_a_" + tag)
        landed = _run_comm(self.comm_b(received), "reduce_b_" + tag)
        self.done(_run_comm(self.comm_c(landed), "reduce_c_" + tag))


def _bwd_layer(d, saved, p, wl, small, prep, l, tm, tk, red=None, carry_c=True):
    xin, proj, ycat, x1, a, x2 = saved
    busy = red is not None and red.pending is not None
    (dx2, h3, dpre, dpp, dg_ple), ca = ple_bwd(d, x2, (p, l), _row(small["norm_ple_g"][l]), wl["w_ple_gate"],
                                               wl["w_ple_proj"], tm=tm, comm=red.comm_a() if busy else None)
    gb = {"w_ple_gate": wgrad(h3, dpre, tk=tk), "w_ple_proj": wgrad(p, dpp, tk=tk, a_layer=l)}
    (dx1, h2, r, da, dg_ff), cb = ffn_bwd(dx2, x1, a, _row(small["norm_ff_g"][l]), wl["w_ff1"], wl["w_ff2"],
                                          tm=tm // 2, comm=red.comm_b(ca) if busy else None)
    gb["w_ff2"] = wgrad(r, dx2, tk=tk)
    gb["w_ff1"] = wgrad(h2, da, tk=tk)
    comm_c = red.comm_c(cb) if busy else None
    if busy and not carry_c:
        red.done(_run_comm(comm_c, f"reduce_c_{l + 1}"))
        comm_c = None
    (dprev, h1, dproj, dg_mix, dws, dbm, dlng, dlnb, dcw, dpw, dps), cc = mix_bwd(
        dx1, xin, proj, _row(small["norm_mix_g"][l]), wl["w_in"], wl["w_out"], prep["wcat"][l], prep["wcat_t"][l],
        prep["bmat"][l], _row(small["sgu_ln_g"][l]), _row(small["sgu_ln_b"][l]), prep["avg"], prep["conv8"][l],
        prep["pool_bd"][l], prep["pool_bd_t"][l], _row(small["pool_scale"][l]), tm=tm, comm=comm_c)
    if comm_c is not None:
        red.done(cc)
    gb["w_out"] = wgrad(ycat, dx1, tk=tk)
    gb["w_in"] = wgrad(h1, dproj, tk=tk)
    gs = {
        "norm_ple_g": dg_ple[0], "norm_ff_g": dg_ff[0], "norm_mix_g": dg_mix[0],
        "sgu_w": dws.reshape(2 * 3, CHUNK, CHUNK), "sgu_b": dbm[:, ::HEAD_DIM].T,
        "sgu_ln_g": dlng[0], "sgu_ln_b": dlnb[0], "conv_w": dcw[0:3], "pool_scale": dps[0],
        "pool_w": jnp.stack([dpw[g * HEAD_DIM:(g + 1) * HEAD_DIM, g * HEAD_DIM:(g + 1) * HEAD_DIM]
                             for g in range(D_C // HEAD_DIM)]),
    }
    return dprev, gb, gs


def _local_step(x, p, target, full, small, *, tm, tk):
    prep = _prep_small(small)
    p = p[:, None]
    saved, h = [], x
    for l in range(DEPTH):
        wl = {name: full[name][l] for name in full}
        s, h, _, _ = _fwd_layer(h, p, wl, small, prep, l, tm)
        saved.append(s)
    loss_blk, d_final_g, d = loss_head(h, target, _row(small["final_g"]), tm=tm)
    gbig, gsm = [None] * DEPTH, [None] * DEPTH
    for l in reversed(range(DEPTH)):
        wl = {name: full[name][l] for name in full}
        d, gbig[l], gsm[l] = _bwd_layer(d, saved[l], p, wl, small, prep, l, tm, tk)
    big = {name: jnp.stack([gbig[l][name] for l in range(DEPTH)]) for name in gbig[0]}
    sm = {name: jnp.stack([gsm[l][name] for l in range(DEPTH)]) for name in gsm[0]}
    sm["final_g"] = d_final_g[0]
    return loss_blk[0, 0], d, big, sm


def _weights_of(gathered, names):
    wl = dict(zip([b[0] for b in BIG if b[0] in names], gathered))
    if "w_in" in wl:
        wl["w_in"] = wl["w_in"].transpose(1, 0, 2).reshape(D_MODEL, D_IN)
    return wl


def kernel(x, p, norm_mix_g, w_in, sgu_w, sgu_b, sgu_ln_g, sgu_ln_b, conv_w, pool_w, pool_scale, w_out, norm_ff_g, w_ff1, w_ff2, norm_ple_g, w_ple_gate, w_ple_proj, final_g, loss_target, m_norm_mix_g, m_w_in, m_sgu_w, m_sgu_b, m_sgu_ln_g, m_sgu_ln_b, m_conv_w, m_pool_w, m_pool_scale, m_w_out, m_norm_ff_g, m_w_ff1, m_w_ff2, m_norm_ple_g, m_w_ple_gate, m_w_ple_proj, m_final_g, v_norm_mix_g, v_w_in, v_sgu_w, v_sgu_b, v_sgu_ln_g, v_sgu_ln_b, v_conv_w, v_pool_w, v_pool_scale, v_w_out, v_norm_ff_g, v_w_ff1, v_w_ff2, v_norm_ple_g, v_w_ple_gate, v_w_ple_proj, v_final_g):
    args = dict(locals())
    w = {name: args[name] for name in WEIGHTS}
    m = {name: args["m_" + name] for name in WEIGHTS}
    v = {name: args["v_" + name] for name in WEIGHTS}
    t = x.shape[1]
    tm = min(512, t)
    tk = min(2048, t)
    x_idx, y_idx, c_idx = _place()
    chip = 2 * x_idx + y_idx
    c_arr = c_idx.reshape(1).astype(jnp.int32)
    xs, target = x[0], loss_target[0]

    shards = {name: w[name].astype(BF16) for name, _, _, _ in BIG}
    conv_rows = _round_up(CONV_SHARD, 8 * 128) // 128
    conv_flat = jnp.pad(w["conv_w"].reshape(-1), (0, conv_rows * 128 - CONV_SHARD)).reshape(conv_rows, 128)
    first = _run_comm(_gather_comm(shards, 0, MIX_WEIGHTS, conv_flat), "gather_first")
    conv_full = (first[len(MIX_WEIGHTS)].reshape(N_CHIPS, -1)[:, :CONV_SHARD]
                 .reshape(N_CHIPS, DEPTH, 3, D_B // N_CHIPS).transpose(1, 2, 0, 3).reshape(DEPTH, 3, D_B))
    small = {name: w[name] for name in SMALL}
    small["conv_w"] = conv_full
    prep = _prep_small(small)

    wl = [None] * DEPTH
    wl[0] = _weights_of(first, MIX_WEIGHTS)
    saved, h = [], xs
    for l in range(DEPTH):
        comm_mix = _gather_comm(shards, 0, MLP_WEIGHTS) if l == 0 else None
        comm_mlp = _gather_comm(shards, l + 1, ALL_BIG) if l + 1 < DEPTH else None
        s, h, got, wl[l] = _fwd_layer(h, p, wl[l], small, prep, l, tm, comm_mix, comm_mlp)
        saved.append(s)
        if comm_mlp is not None:
            wl[l + 1] = _weights_of(got, ALL_BIG)

    loss_blk, d_final_g, d = loss_head(h, target, _row(small["final_g"]), tm=tm)

    geoms = _big_geoms()
    red = _Reducer([jnp.zeros((DEPTH,) + g.final_shape, F32) for g in geoms], c_arr)
    gsm = [None] * DEPTH
    for l in reversed(range(DEPTH)):
        d, gb, gsm[l] = _bwd_layer(d, saved[l], p, wl[l], small, prep, l, tm, tk, red, carry_c=l > 0)
        red.push(l, geoms, _grad_views(gb, geoms))

    sm = {name: jnp.stack([gsm[i][name] for i in range(DEPTH)]) for name in gsm[0]}
    sm["final_g"] = d_final_g[0]
    sizes = [sm[name].size for name in SMALL]
    small_rows = _round_up(-(-sum(sizes) // (2 * N_CHIPS * LANES)), 8 * ADD_STEPS)
    small_flat = jnp.pad(jnp.concatenate([sm[name].reshape(-1) for name in SMALL]),
                         (0, 2 * N_CHIPS * small_rows * LANES - sum(sizes)))
    small_geom = _Geom("rows", (N_CHIPS, 2, small_rows, LANES))
    red.push(0, geoms + [small_geom], red.pending[2] + [small_flat.reshape(small_geom.shape)])
    red.finals.append(jnp.zeros((1,) + small_geom.final_shape, F32))
    red.flush("tail")

    grads = {name: red.finals[a] for a, (name, _, _, _) in enumerate(BIG)}
    grads["w_in"] = lax.dynamic_slice_in_dim(grads["w_in"], chip * (D_IN // N_CHIPS - W_IN_STRIDE), D_IN // N_CHIPS, axis=2)
    small_red = _run_comm(_allgather_comm(red.finals[len(BIG)][0]), "small_allgather")[0].reshape(-1)
    off = 0
    for name, size in zip(SMALL, sizes):
        grads[name] = small_red[off:off + size].reshape(sm[name].shape)
        off += size
    grads["conv_w"] = lax.dynamic_slice_in_dim(grads["conv_w"], chip * (D_B // N_CHIPS), D_B // N_CHIPS, axis=2)

    loss = lax.psum(loss_blk[0, 0], ("x", "y", "c"))
    delta, new_m, new_v = {}, {}, {}
    for name in WEIGHTS:
        delta[name], new_m[name], new_v[name] = adamw(w[name], grads[name], m[name], v[name])
    return (loss, d[None], *[grads[n] for n in WEIGHTS], *[delta[n] for n in WEIGHTS],
            *[new_m[n] for n in WEIGHTS], *[new_v[n] for n in WEIGHTS])
```

```python
import math

import jax
import jax.numpy as jnp
from jax import lax
from jax.experimental import pallas as pl
from jax.experimental.pallas import tpu as pltpu

F32 = jnp.float32
BF16 = jnp.bfloat16

D_MODEL = 1024
DEPTH = 4
D_PLE = 256
D_FF = 4096
HEAD_DIM = 64
D_A = 384
D_B = 384
D_C = 256
D_IN = 2176
CHUNK = 128
HALO = 16
RMS_EPS = 1e-6
LN_EPS = 1e-5
N_CHIPS = 4
LANES = 1024

ADAM_LR = 0.001
ADAM_B1 = 0.9
ADAM_B2 = 0.999
ADAM_EPS = 1e-08
ADAM_WD = 0.01
ADAM_STEP = 10

VMEM_LIMIT_BYTES = 60 * 1024 * 1024

_RSQRT2 = 0.7071067811865476
_INV_SQRT_2PI = 0.3989422804014327

BIG = (
    ("w_in", D_MODEL, D_IN, 1),
    ("w_out", D_MODEL, D_MODEL, 0),
    ("w_ff1", D_MODEL, D_FF, 1),
    ("w_ff2", D_FF, D_MODEL, 0),
    ("w_ple_gate", D_MODEL, D_MODEL, 0),
    ("w_ple_proj", D_PLE, D_MODEL, 1),
)
SMALL = ("norm_mix_g", "sgu_w", "sgu_b", "sgu_ln_g", "sgu_ln_b", "conv_w", "pool_w", "pool_scale",
         "norm_ff_g", "norm_ple_g", "final_g")
WEIGHTS = ("norm_mix_g", "w_in", "sgu_w", "sgu_b", "sgu_ln_g", "sgu_ln_b", "conv_w", "pool_w", "pool_scale",
           "w_out", "norm_ff_g", "w_ff1", "w_ff2", "norm_ple_g", "w_ple_gate", "w_ple_proj", "final_g")
CONV_SHARD = DEPTH * 3 * (D_B // N_CHIPS)


def _dot(a, b):
    return jnp.dot(a, b, preferred_element_type=F32)


def _dot_nt(a, b):
    return lax.dot_general(a, b, (((1,), (1,)), ((), ())), preferred_element_type=F32)


def _dot_tn(a, b):
    return lax.dot_general(a, b, (((0,), (0,)), ((), ())), preferred_element_type=F32)


def _const_spec(shape):
    nd = len(shape)
    return pl.BlockSpec(shape, lambda i: (0,) * nd, pipeline_mode=pl.Buffered(1))


def _acc_spec(shape):
    nd = len(shape)
    return pl.BlockSpec(shape, lambda i: (0,) * nd)


def _layer_rows(layer, tm):
    return pl.BlockSpec((None, None, tm, D_PLE), lambda i: (layer, 0, i, 0))


def _params(*sem):
    return pltpu.CompilerParams(dimension_semantics=sem, vmem_limit_bytes=VMEM_LIMIT_BYTES)


def _rms_bwd(dh, n, rs, g):
    dn = dh * g
    return rs * (dn - n * jnp.mean(dn * n, axis=-1, keepdims=True))


def _gelu(x):
    return x * (0.5 * (1.0 + lax.erf(x * _RSQRT2)))


def _gelu_and_grad(x):
    cdf = 0.5 * (1.0 + lax.erf(x * _RSQRT2))
    return x * cdf, cdf + x * (jnp.exp(-0.5 * x * x) * _INV_SQRT_2PI)


def _group_mean(v, avg):
    vb = v.astype(BF16)
    split = 2 * CHUNK
    return jnp.concatenate([_dot(vb[:, :split], avg[:split, :split]), _dot(vb[:, split:], avg[split:, split:])], axis=1)


def _group_mean_split(v, avg):
    hi = v.astype(BF16)
    lo = (v - hi.astype(F32)).astype(BF16)
    return _dot(hi, avg) + _dot(lo, avg)


def _lane_lt(shape, bound):
    return lax.broadcasted_iota(jnp.int32, shape, 1) < bound


def _sgu_mix(vnb2, wcat_j, lo_mask):
    zero = jnp.zeros_like(vnb2)
    stacked = jnp.concatenate([jnp.where(lo_mask, vnb2, zero), jnp.where(lo_mask, zero, vnb2)], axis=0)
    return _dot(wcat_j, stacked)


def _pool_means(ext, tile_rows, first_pos):
    s2 = ext + pltpu.roll(ext, 1, 0)
    s4 = s2 + pltpu.roll(s2, 2, 0)
    s8 = s4 + pltpu.roll(s4, 4, 0)
    s16 = s8 + pltpu.roll(s8, 8, 0)
    pos = (first_pos + lax.broadcasted_iota(jnp.int32, (tile_rows, 1), 0) + 1).astype(F32)
    lane = lax.broadcasted_iota(jnp.int32, (tile_rows, D_C), 1)
    sums = jnp.where(lane < 64, s2[HALO:], jnp.where(lane < 128, s4[HALO:], jnp.where(lane < 192, s8[HALO:], s16[HALO:])))
    win = jnp.where(lane < 64, 2.0, jnp.where(lane < 128, 4.0, jnp.where(lane < 192, 8.0, 16.0)))
    inv = 1.0 / jnp.minimum(pos, win)
    return sums * inv, inv


MESH = pl.DeviceIdType.MESH
_ANY = pl.BlockSpec(memory_space=pl.ANY)


def _place():
    return lax.axis_index("x"), lax.axis_index("y"), lax.axis_index("c")


def _chip_peers(x, y):
    return [(1 - x, y), (x, 1 - y), (1 - x, 1 - y)]


class _Comm:
    def __init__(self, ins, out_shapes, sems, copies, aliases=None):
        self.ins, self.out_shapes, self.sems, self.copies = list(ins), list(out_shapes), list(sems), copies
        self.aliases = dict(aliases or {})

    def start(self, in_refs, out_refs, sem_refs):
        local, sends, _ = self.copies(in_refs, out_refs, sem_refs)
        for cp in local + sends:
            cp.start()

    def wait(self, in_refs, out_refs, sem_refs):
        local, sends, recvs = self.copies(in_refs, out_refs, sem_refs)
        for cp in recvs:
            cp.wait_recv()
        for cp in sends:
            cp.wait_send()
        for cp in local:
            cp.wait()


def _remote(src, dst, send_sem, recv_sem, device):
    return pltpu.make_async_remote_copy(src_ref=src, dst_ref=dst, send_sem=send_sem, recv_sem=recv_sem,
                                        device_id=device, device_id_type=MESH)


def _gather_comm(shards, layer, names, conv=None):
    mats = [b for b in BIG if b[0] in names]
    ins = [shards[name] for name, _, _, _ in mats] + ([conv] if conv is not None else [])
    out_shapes = []
    for name, k, n, axis in mats:
        shape = (N_CHIPS, k, n // N_CHIPS) if name == "w_in" else (k, n)
        out_shapes.append(jax.ShapeDtypeStruct(shape, BF16))
    if conv is not None:
        out_shapes.append(jax.ShapeDtypeStruct((N_CHIPS,) + conv.shape, conv.dtype))
    n_arr = len(ins)

    def block(a, out_ref, chip):
        if a == len(mats) or mats[a][0] == "w_in":
            return out_ref.at[chip]
        _, k, n, axis = mats[a]
        if axis == 0:
            return out_ref.at[pl.ds(chip * (k // N_CHIPS), k // N_CHIPS), :]
        return out_ref.at[:, pl.ds(chip * (n // N_CHIPS), n // N_CHIPS)]

    def copies(in_refs, out_refs, sem_refs):
        send_sems, recv_sems, local_sems = sem_refs
        x, y, c = _place()
        me = 2 * x + y
        local, sends, recvs = [], [], []
        for a in range(n_arr):
            src = in_refs[a].at[layer] if a < len(mats) else in_refs[a]
            local.append(pltpu.make_async_copy(src, block(a, out_refs[a], me), local_sems.at[a]))
            for j, (px, py) in enumerate(_chip_peers(x, y)):
                sends.append(_remote(src, block(a, out_refs[a], me), send_sems.at[a, j], recv_sems.at[a, j], (px, py, c)))
                recvs.append(_remote(src, block(a, out_refs[a], 2 * px + py), send_sems.at[a, j], recv_sems.at[a, j],
                                     (px, py, c)))
        return local, sends, recvs

    sems = [pltpu.SemaphoreType.DMA((n_arr, 3)), pltpu.SemaphoreType.DMA((n_arr, 3)), pltpu.SemaphoreType.DMA((n_arr,))]
    return _Comm(ins, out_shapes, sems, copies)


def _allgather_comm(a):
    def copies(in_refs, out_refs, sem_refs):
        send_sems, recv_sems, local_sem = sem_refs
        x, y, c = _place()
        me = 2 * x + y
        local = [pltpu.make_async_copy(in_refs[0], out_refs[0].at[me], local_sem)]
        sends, recvs = [], []
        for j, (px, py) in enumerate(_chip_peers(x, y)):
            sends.append(_remote(in_refs[0], out_refs[0].at[me], send_sems.at[j], recv_sems.at[j], (px, py, c)))
            recvs.append(_remote(in_refs[0], out_refs[0].at[2 * px + py], send_sems.at[j], recv_sems.at[j], (px, py, c)))
        return local, sends, recvs

    sems = [pltpu.SemaphoreType.DMA((3,)), pltpu.SemaphoreType.DMA((3,)), pltpu.SemaphoreType.DMA]
    return _Comm([a], [jax.ShapeDtypeStruct((N_CHIPS,) + a.shape, a.dtype)], sems, copies)


class _Geom:
    def __init__(self, kind, shape, stride=None, width=None):
        self.kind, self.shape, self.stride, self.width = kind, tuple(shape), stride, width
        if kind == "cols":
            k, n = shape
            self.half_shape, self.part_shape, self.final_shape = (k // 2, n), (k // 2, width), (k, width)
        else:
            _, _, h, n = shape
            self.half_shape, self.part_shape, self.final_shape = (N_CHIPS, h, n), (h, n), (2 * h, n)

    def half(self, ref, core):
        if self.kind == "cols":
            return ref.at[pl.ds(core * self.half_shape[0], self.half_shape[0]), :]
        return ref.at[:, core]

    def part(self, ref, chip):
        if self.kind == "cols":
            return ref.at[:, pl.ds(chip * self.stride, self.width)]
        return ref.at[chip]

    def final_half(self, ref, layer, core):
        rows = self.part_shape[0]
        return ref.at[layer, pl.ds(core * rows, rows), :]


def _reduce_a_comm(geoms, arrs):
    n = len(arrs)

    def copies(in_refs, out_refs, sem_refs):
        x, y, c = _place()
        cps = [_remote(geoms[a].half(in_refs[a], 1 - c), out_refs[a], sem_refs[0].at[a], sem_refs[1].at[a], (x, y, 1 - c))
               for a in range(n)]
        return [], cps, cps

    return _Comm(arrs, [jax.ShapeDtypeStruct(g.half_shape, F32) for g in geoms],
                 [pltpu.SemaphoreType.DMA((n,)), pltpu.SemaphoreType.DMA((n,))], copies)


def _reduce_b_comm(geoms, halves):
    n = len(halves)

    def copies(in_refs, out_refs, sem_refs):
        send_sems, recv_sems, local_sems = sem_refs
        x, y, c = _place()
        me = 2 * x + y
        local, sends, recvs = [], [], []
        for a in range(n):
            g = geoms[a]
            local.append(pltpu.make_async_copy(g.part(in_refs[a], me), out_refs[a].at[me], local_sems.at[a]))
            for j, (px, py) in enumerate(_chip_peers(x, y)):
                peer = 2 * px + py
                sends.append(_remote(g.part(in_refs[a], peer), out_refs[a].at[me], send_sems.at[a, j], recv_sems.at[a, j],
                                     (px, py, c)))
                recvs.append(_remote(g.part(in_refs[a], me), out_refs[a].at[peer], send_sems.at[a, j], recv_sems.at[a, j],
                                     (px, py, c)))
        return local, sends, recvs

    sems = [pltpu.SemaphoreType.DMA((n, 3)), pltpu.SemaphoreType.DMA((n, 3)), pltpu.SemaphoreType.DMA((n,))]
    return _Comm(halves, [jax.ShapeDtypeStruct((N_CHIPS,) + g.part_shape, F32) for g in geoms], sems, copies)


def _reduce_c_comm(geoms, finals, layer):
    n = len(finals)

    def copies(in_refs, out_refs, sem_refs):
        send_sems, recv_sems = sem_refs
        x, y, c = _place()
        sends, recvs = [], []
        for a in range(n):
            g = geoms[a]
            sends.append(_remote(g.final_half(in_refs[a], layer, c), g.final_half(out_refs[a], layer, c), send_sems.at[a],
                                 recv_sems.at[a], (x, y, 1 - c)))
            recvs.append(_remote(g.final_half(in_refs[a], layer, c), g.final_half(out_refs[a], layer, 1 - c),
                                 send_sems.at[a], recv_sems.at[a], (x, y, 1 - c)))
        return [], sends, recvs

    sems = [pltpu.SemaphoreType.DMA((n,)), pltpu.SemaphoreType.DMA((n,))]
    return _Comm(finals, [jax.ShapeDtypeStruct(f.shape, f.dtype) for f in finals], sems, copies,
                 aliases={a: a for a in range(n)})


def _run_comm(comm, name):
    def body(*refs):
        ni, no = len(comm.ins), len(comm.out_shapes)
        in_refs, out_refs, sem_refs = refs[:ni], refs[ni:ni + no], refs[ni + no:]
        comm.start(in_refs, out_refs, sem_refs)
        comm.wait(in_refs, out_refs, sem_refs)

    return pl.pallas_call(
        body, name=name, in_specs=[_ANY] * len(comm.ins), out_specs=[_ANY] * len(comm.out_shapes),
        out_shape=comm.out_shapes, scratch_shapes=comm.sems, input_output_aliases=comm.aliases,
        compiler_params=pltpu.CompilerParams(has_side_effects=True),
    )(*comm.ins)


def _tile_call(body, name, nt, in_specs, out_specs, out_shape, scratch, args, comm):
    if comm is None:
        outs = pl.pallas_call(body, name=name, grid=(nt,), in_specs=in_specs, out_specs=out_specs, out_shape=out_shape,
                              scratch_shapes=scratch, compiler_params=_params("arbitrary"))(*args)
        return outs, []
    n_in, n_out, n_scr = len(in_specs), len(out_specs), len(scratch)
    ci, co = len(comm.ins), len(comm.out_shapes)

    def hosted(*refs):
        in_refs = refs[:n_in]
        cin = refs[n_in:n_in + ci]
        out_refs = refs[n_in + ci:n_in + ci + n_out]
        cout = refs[n_in + ci + n_out:n_in + ci + n_out + co]
        scr = refs[n_in + ci + n_out + co:n_in + ci + n_out + co + n_scr]
        sems = refs[n_in + ci + n_out + co + n_scr:]
        i = pl.program_id(0)

        @pl.when(i == 0)
        def _():
            comm.start(cin, cout, sems)

        body(*in_refs, *out_refs, *scr)

        @pl.when(i == nt - 1)
        def _():
            comm.wait(cin, cout, sems)

    outs = pl.pallas_call(
        hosted, name=name + "_comm", grid=(nt,),
        in_specs=list(in_specs) + [_ANY] * ci, out_specs=list(out_specs) + [_ANY] * co,
        out_shape=list(out_shape) + comm.out_shapes, scratch_shapes=list(scratch) + comm.sems,
        input_output_aliases={n_in + i: n_out + o for i, o in comm.aliases.items()},
        compiler_params=_params("arbitrary"),
    )(*args, *comm.ins)
    return outs[:n_out], outs[n_out:]


def mix_fwd(x, g_mix, w_in, w_out, wcat, bmat, ln_g, ln_b, avg, conv_w, pool_bd, pool_scale, *, tm, comm=None):
    t = x.shape[0]
    nt = t // tm

    def body(x_ref, g_ref, win_ref, wout_ref, wcat_ref, bmat_ref, lng_ref, lnb_ref, avg_ref, cw_ref, pw_ref, ps_ref,
             proj_ref, ycat_ref, x1_ref, hbuf, zbuf):
        i = pl.program_id(0)

        @pl.when(i == 0)
        def _():
            hbuf[0:HALO, :] = jnp.zeros((HALO, D_B), F32)
            zbuf[0:HALO, :] = jnp.zeros((HALO, D_C), F32)

        xv = x_ref[...]
        n = xv * lax.rsqrt(jnp.mean(xv * xv, axis=-1, keepdims=True) + RMS_EPS)
        h1 = (n * g_ref[...]).astype(BF16)
        proj_ref[...] = _dot(h1, win_ref[...])

        lo_mask = _lane_lt((CHUNK, CHUNK), HEAD_DIM)
        avg = avg_ref[...]
        for c in range(tm // CHUNK):
            rows = pl.ds(c * CHUNK, CHUNK)
            gu = _gelu(proj_ref[rows, 0:D_A])
            gv = _gelu(proj_ref[rows, D_A:2 * D_A])
            dv = gv - _group_mean(gv, avg)
            var = _group_mean(dv * dv, avg)
            vnb = (dv * lax.rsqrt(var + LN_EPS) * lng_ref[...] + lnb_ref[...]).astype(BF16)
            for j in range(3):
                cols = slice(j * CHUNK, (j + 1) * CHUNK)
                mixed = _sgu_mix(vnb[:, cols], wcat_ref[j], lo_mask) + bmat_ref[:, cols]
                ycat_ref[rows, cols] = (gu[:, cols] * mixed).astype(BF16)

        o = 2 * D_A
        hcur = proj_ref[:, o + 2 * D_B:o + 3 * D_B] * proj_ref[:, o:o + D_B]
        hbuf[HALO:HALO + tm, :] = hcur
        y = (cw_ref[2:3, :] * hcur + cw_ref[1:2, :] * hbuf[pl.ds(HALO - 1, tm), :]
             + cw_ref[0:1, :] * hbuf[pl.ds(HALO - 2, tm), :])
        ycat_ref[:, D_A:D_A + D_B] = (proj_ref[:, o + D_B:o + 2 * D_B] * y).astype(BF16)
        hbuf[0:HALO, :] = hbuf[tm:tm + HALO, :]

        zc = proj_ref[:, o + 3 * D_B:D_IN]
        zbuf[HALO:HALO + tm, :] = zc
        mean, _ = _pool_means(zbuf[...], tm, i * tm)
        pooled = (mean - zc).astype(BF16)
        ycat_ref[:, D_A + D_B:D_MODEL] = (_dot(pooled, pw_ref[...]) * ps_ref[...]).astype(BF16)
        zbuf[0:HALO, :] = zbuf[tm:tm + HALO, :]

        x1_ref[...] = xv + _dot(ycat_ref[...], wout_ref[...])

    row = lambda w: pl.BlockSpec((tm, w), lambda i: (i, 0))
    return _tile_call(
        body, "mix_fwd", nt,
        [row(D_MODEL), _const_spec((1, D_MODEL)), _const_spec((D_MODEL, D_IN)), _const_spec((D_MODEL, D_MODEL)),
         _const_spec((3, CHUNK, 2 * CHUNK)), _const_spec((CHUNK, D_A)), _const_spec((1, D_A)), _const_spec((1, D_A)),
         _const_spec((D_A, D_A)), _const_spec((8, D_B)), _const_spec((D_C, D_C)), _const_spec((1, D_C))],
        [row(D_IN), row(D_MODEL), row(D_MODEL)],
        [jax.ShapeDtypeStruct((t, D_IN), F32), jax.ShapeDtypeStruct((t, D_MODEL), BF16),
         jax.ShapeDtypeStruct((t, D_MODEL), F32)],
        [pltpu.VMEM((tm + HALO, D_B), F32), pltpu.VMEM((tm + HALO, D_C), F32)],
        (x, g_mix, w_in, w_out, wcat, bmat, ln_g, ln_b, avg, conv_w, pool_bd, pool_scale), comm)


def ffn_ple_fwd(x1, p, g_ff, w_ff1, w_ff2, g_ple, w_gate, w_proj, *, tm, comm=None):
    t = x1.shape[0]
    nt = t // tm
    nc = D_FF // D_MODEL

    def body(x1_ref, p_ref, gff_ref, w1_ref, w2_ref, gple_ref, wg_ref, wp_ref, a_ref, x2_ref, x3_ref):
        x1v = x1_ref[...]
        n2 = x1v * lax.rsqrt(jnp.mean(x1v * x1v, axis=-1, keepdims=True) + RMS_EPS)
        h2 = (n2 * gff_ref[...]).astype(BF16)
        acc = x1v
        for c in range(nc):
            cols = slice(c * D_MODEL, (c + 1) * D_MODEL)
            a = _dot(h2, w1_ref[:, cols])
            a_ref[:, cols] = a.astype(BF16)
            ra = jnp.maximum(a, 0.0)
            acc = acc + _dot((ra * ra).astype(BF16), w2_ref[cols, :])
        x2_ref[...] = acc
        n3 = acc * lax.rsqrt(jnp.mean(acc * acc, axis=-1, keepdims=True) + RMS_EPS)
        h3 = (n3 * gple_ref[...]).astype(BF16)
        gate = jax.nn.sigmoid(_dot(h3, wg_ref[...]))
        pp = _dot(p_ref[...].astype(BF16), wp_ref[...])
        x3_ref[...] = acc + pp * gate

    row = lambda w: pl.BlockSpec((tm, w), lambda i: (i, 0))
    return _tile_call(
        body, "ffn_ple_fwd", nt,
        [row(D_MODEL), _layer_rows(p[1], tm), _const_spec((1, D_MODEL)), _const_spec((D_MODEL, D_FF)),
         _const_spec((D_FF, D_MODEL)), _const_spec((1, D_MODEL)), _const_spec((D_MODEL, D_MODEL)),
         _const_spec((D_PLE, D_MODEL))],
        [row(D_FF), row(D_MODEL), row(D_MODEL)],
        [jax.ShapeDtypeStruct((t, D_FF), BF16), jax.ShapeDtypeStruct((t, D_MODEL), F32),
         jax.ShapeDtypeStruct((t, D_MODEL), F32)],
        [], (x1, p[0], g_ff, w_ff1, w_ff2, g_ple, w_gate, w_proj), comm)


def loss_head(x, target, g, *, tm):
    t = x.shape[0]
    nt = t // tm

    def body(x_ref, t_ref, g_ref, loss_ref, dg_ref, dx_ref, sq_acc):
        i = pl.program_id(0)

        @pl.when(i == 0)
        def _():
            sq_acc[...] = jnp.zeros_like(sq_acc)
            dg_ref[...] = jnp.zeros_like(dg_ref)

        xv = x_ref[...]
        rs = lax.rsqrt(jnp.mean(xv * xv, axis=-1, keepdims=True) + RMS_EPS)
        n = xv * rs
        gv = g_ref[...]
        err = n * gv - t_ref[...]
        sq_acc[...] += jnp.sum(err * err, axis=0, keepdims=True)
        dy = err * (1.0 / D_MODEL)
        dg_ref[...] += jnp.sum(dy * n, axis=0, keepdims=True)
        dx_ref[...] = _rms_bwd(dy, n, rs, gv)

        @pl.when(i == nt - 1)
        def _():
            total = jnp.sum(sq_acc[...], axis=1, keepdims=True) * (0.5 / D_MODEL)
            loss_ref[...] = jnp.broadcast_to(total, loss_ref.shape)

    row = pl.BlockSpec((tm, D_MODEL), lambda i: (i, 0))
    return pl.pallas_call(
        body, name="loss_head", grid=(nt,),
        in_specs=[row, row, _const_spec((1, D_MODEL))],
        out_specs=[_acc_spec((8, 128)), _acc_spec((1, D_MODEL)), row],
        out_shape=[jax.ShapeDtypeStruct((8, 128), F32), jax.ShapeDtypeStruct((1, D_MODEL), F32),
                   jax.ShapeDtypeStruct((t, D_MODEL), F32)],
        scratch_shapes=[pltpu.VMEM((1, D_MODEL), F32)],
        compiler_params=_params("arbitrary"),
    )(x, target, g)


def ple_bwd(d, x2, p, g_ple, w_gate, w_proj, *, tm, comm=None):
    t = d.shape[0]
    nt = t // tm

    def body(d_ref, x2_ref, p_ref, g_ref, wg_ref, wp_ref, dx2_ref, h3_ref, dpre_ref, dpp_ref, dg_ref):
        i = pl.program_id(0)

        @pl.when(i == 0)
        def _():
            dg_ref[...] = jnp.zeros_like(dg_ref)

        dv = d_ref[...]
        x2v = x2_ref[...]
        rs = lax.rsqrt(jnp.mean(x2v * x2v, axis=-1, keepdims=True) + RMS_EPS)
        n3 = x2v * rs
        gv = g_ref[...]
        h3 = (n3 * gv).astype(BF16)
        h3_ref[...] = h3
        gate = jax.nn.sigmoid(_dot(h3, wg_ref[...]))
        pp = _dot(p_ref[...].astype(BF16), wp_ref[...])
        dpp_ref[...] = (dv * gate).astype(BF16)
        dpre = (dv * pp * gate * (1.0 - gate)).astype(BF16)
        dpre_ref[...] = dpre
        dh3 = _dot_nt(dpre, wg_ref[...])
        dg_ref[...] += jnp.sum(dh3 * n3, axis=0, keepdims=True)
        dx2_ref[...] = dv + _rms_bwd(dh3, n3, rs, gv)

    row = lambda w: pl.BlockSpec((tm, w), lambda i: (i, 0))
    return _tile_call(
        body, "ple_bwd", nt,
        [row(D_MODEL), row(D_MODEL), _layer_rows(p[1], tm), _const_spec((1, D_MODEL)), _const_spec((D_MODEL, D_MODEL)),
         _const_spec((D_PLE, D_MODEL))],
        [row(D_MODEL), row(D_MODEL), row(D_MODEL), row(D_MODEL), _acc_spec((1, D_MODEL))],
        [jax.ShapeDtypeStruct((t, D_MODEL), F32), jax.ShapeDtypeStruct((t, D_MODEL), BF16),
         jax.ShapeDtypeStruct((t, D_MODEL), BF16), jax.ShapeDtypeStruct((t, D_MODEL), BF16),
         jax.ShapeDtypeStruct((1, D_MODEL), F32)],
        [], (d, x2, p[0], g_ple, w_gate, w_proj), comm)


def ffn_bwd(dx2, x1, a, g_ff, w_ff1, w_ff2, *, tm, comm=None):
    t = dx2.shape[0]
    nt = t // tm
    nc = D_FF // D_MODEL

    def body(dx2_ref, x1_ref, a_ref, g_ref, w1_ref, w2_ref, dx1_ref, h2_ref, r_ref, da_ref, dg_ref):
        i = pl.program_id(0)

        @pl.when(i == 0)
        def _():
            dg_ref[...] = jnp.zeros_like(dg_ref)

        dv = dx2_ref[...]
        x1v = x1_ref[...]
        rs = lax.rsqrt(jnp.mean(x1v * x1v, axis=-1, keepdims=True) + RMS_EPS)
        n2 = x1v * rs
        gv = g_ref[...]
        h2_ref[...] = (n2 * gv).astype(BF16)
        dvb = dv.astype(BF16)
        dh2 = jnp.zeros((tm, D_MODEL), F32)
        for c in range(nc):
            cols = slice(c * D_MODEL, (c + 1) * D_MODEL)
            ra = jnp.maximum(a_ref[:, cols].astype(F32), 0.0)
            r_ref[:, cols] = (ra * ra).astype(BF16)
            da = (_dot_nt(dvb, w2_ref[cols, :]) * (2.0 * ra)).astype(BF16)
            da_ref[:, cols] = da
            dh2 = dh2 + _dot_nt(da, w1_ref[:, cols])
        dg_ref[...] += jnp.sum(dh2 * n2, axis=0, keepdims=True)
        dx1_ref[...] = dv + _rms_bwd(dh2, n2, rs, gv)

    row = lambda w: pl.BlockSpec((tm, w), lambda i: (i, 0))
    return _tile_call(
        body, "ffn_bwd", nt,
        [row(D_MODEL), row(D_MODEL), row(D_FF), _const_spec((1, D_MODEL)), _const_spec((D_MODEL, D_FF)),
         _const_spec((D_FF, D_MODEL))],
        [row(D_MODEL), row(D_MODEL), row(D_FF), row(D_FF), _acc_spec((1, D_MODEL))],
        [jax.ShapeDtypeStruct((t, D_MODEL), F32), jax.ShapeDtypeStruct((t, D_MODEL), BF16),
         jax.ShapeDtypeStruct((t, D_FF), BF16), jax.ShapeDtypeStruct((t, D_FF), BF16),
         jax.ShapeDtypeStruct((1, D_MODEL), F32)],
        [], (dx2, x1, a, g_ff, w_ff1, w_ff2), comm)


def mix_bwd(dx1, x, proj, g_mix, w_in, w_out, wcat, wcat_t, bmat, ln_g, ln_b, avg, conv_w, pool_bd, pool_bd_t,
            pool_scale, *, tm, comm=None):
    t = dx1.shape[0]
    nt = t // tm
    prev_blocks = tm // HALO

    def body(dx1_ref, x_ref, proj_ref, prev_ref, g_ref, win_ref, wout_ref, wcat_ref, wcatt_ref, bmat_ref, lng_ref,
             lnb_ref, avg_ref, cw_ref, pw_ref, pwt_ref, ps_ref,
             dx_ref, h1_ref, dproj_ref, dg_ref, dws_ref, dbm_ref, dlng_ref, dlnb_ref, dcw_ref, dpw_ref, dps_ref,
             dyc, dpj, hbuf, zbuf, dybuf, qbuf):
        i = pl.program_id(0)
        ti = nt - 1 - i

        @pl.when(i == 0)
        def _():
            for ref in (dg_ref, dws_ref, dbm_ref, dlng_ref, dlnb_ref, dcw_ref, dpw_ref, dps_ref):
                ref[...] = jnp.zeros_like(ref)
            dybuf[tm:tm + HALO, :] = jnp.zeros((HALO, D_B), F32)
            qbuf[tm:tm + HALO, :] = jnp.zeros((HALO, D_C), F32)

        dx1v = dx1_ref[...]
        dyc[...] = _dot_nt(dx1v.astype(BF16), wout_ref[...])

        lo_mask = _lane_lt((CHUNK, CHUNK), HEAD_DIM)
        avg = avg_ref[...]
        lng = lng_ref[...]
        for c in range(tm // CHUNK):
            rows = pl.ds(c * CHUNK, CHUNK)
            gu, dgu = _gelu_and_grad(proj_ref[rows, 0:D_A])
            gv, dgv = _gelu_and_grad(proj_ref[rows, D_A:2 * D_A])
            cen = gv - _group_mean(gv, avg)
            rstd = lax.rsqrt(_group_mean(cen * cen, avg) + LN_EPS)
            vhat = cen * rstd
            vnb = (vhat * lng + lnb_ref[...]).astype(BF16)
            dya = dyc[rows, 0:D_A]
            dvn_parts = []
            for j in range(3):
                cols = slice(j * CHUNK, (j + 1) * CHUNK)
                vnb2 = vnb[:, cols]
                mixed = _sgu_mix(vnb2, wcat_ref[j], lo_mask) + bmat_ref[:, cols]
                dya2 = dya[:, cols]
                dpj[rows, cols] = dya2 * mixed * dgu[:, cols]
                dm = dya2 * gu[:, cols]
                dbm_ref[:, cols] += dm
                dmb = dm.astype(BF16)
                zero = jnp.zeros_like(dmb)
                dm_st = jnp.concatenate([jnp.where(lo_mask, dmb, zero), jnp.where(lo_mask, zero, dmb)], axis=0)
                dws_ref[j] += _dot_nt(dm_st, vnb2)
                dvn_st = _dot(wcatt_ref[j], dmb)
                dvn_parts.append(jnp.where(lo_mask, dvn_st[0:CHUNK], dvn_st[CHUNK:2 * CHUNK]))
            dvn = jnp.concatenate(dvn_parts, axis=1)
            dlng_ref[...] += jnp.sum(dvn * vhat, axis=0, keepdims=True)
            dlnb_ref[...] += jnp.sum(dvn, axis=0, keepdims=True)
            dvh = dvn * lng
            dgv_in = rstd * (dvh - _group_mean(dvh, avg) - vhat * _group_mean(dvh * vhat, avg))
            dpj[rows, D_A:2 * D_A] = dgv_in * dgv

        o = 2 * D_A
        live = (ti > 0).astype(F32)
        zb = proj_ref[:, o:o + D_B]
        gb = proj_ref[:, o + D_B:o + 2 * D_B]
        gc = proj_ref[:, o + 2 * D_B:o + 3 * D_B]
        hcur = gc * zb
        hbuf[0:HALO, :] = prev_ref[:, o + 2 * D_B:o + 3 * D_B] * prev_ref[:, o:o + D_B] * live
        hbuf[HALO:HALO + tm, :] = hcur
        hm1 = hbuf[pl.ds(HALO - 1, tm), :]
        hm2 = hbuf[pl.ds(HALO - 2, tm), :]
        y = cw_ref[2:3, :] * hcur + cw_ref[1:2, :] * hm1 + cw_ref[0:1, :] * hm2
        dout = dyc[:, D_A:D_A + D_B]
        dpj[:, o + D_B:o + 2 * D_B] = dout * y
        dy = dout * gb
        dcw_ref[2:3, :] += jnp.sum(dy * hcur, axis=0, keepdims=True)
        dcw_ref[1:2, :] += jnp.sum(dy * hm1, axis=0, keepdims=True)
        dcw_ref[0:1, :] += jnp.sum(dy * hm2, axis=0, keepdims=True)
        dybuf[0:tm, :] = dy
        dh = (cw_ref[2:3, :] * dy + cw_ref[1:2, :] * dybuf[pl.ds(1, tm), :] + cw_ref[0:1, :] * dybuf[pl.ds(2, tm), :])
        dybuf[tm:tm + HALO, :] = dybuf[0:HALO, :]
        dpj[:, o:o + D_B] = dh * gc
        dpj[:, o + 2 * D_B:o + 3 * D_B] = dh * zb

        zc = proj_ref[:, o + 3 * D_B:D_IN]
        zbuf[0:HALO, :] = prev_ref[:, o + 3 * D_B:D_IN] * live
        zbuf[HALO:HALO + tm, :] = zc
        mean, inv = _pool_means(zbuf[...], tm, ti * tm)
        pooled = (mean - zc).astype(BF16)
        dyp = dyc[:, D_A + D_B:D_MODEL]
        ps = ps_ref[...]
        dps_ref[...] += jnp.sum(dyp * _dot(pooled, pw_ref[...]), axis=0, keepdims=True)
        dpw = (dyp * ps).astype(BF16)
        dpw_ref[...] += _dot_tn(pooled, dpw)
        dpooled = _dot(dpw, pwt_ref[...])
        qbuf[0:tm, :] = dpooled * inv
        q = qbuf[...]
        nrows = tm + HALO
        f2 = q + pltpu.roll(q, nrows - 1, 0)
        f4 = f2 + pltpu.roll(f2, nrows - 2, 0)
        f8 = f4 + pltpu.roll(f4, nrows - 4, 0)
        f16 = f8 + pltpu.roll(f8, nrows - 8, 0)
        lane = lax.broadcasted_iota(jnp.int32, (tm, D_C), 1)
        ahead = jnp.where(lane < 64, f2[0:tm], jnp.where(lane < 128, f4[0:tm], jnp.where(lane < 192, f8[0:tm], f16[0:tm])))
        dpj[:, o + 3 * D_B:D_IN] = ahead - dpooled
        qbuf[tm:tm + HALO, :] = qbuf[0:HALO, :]

        dprojb = dpj[...].astype(BF16)
        dproj_ref[...] = dprojb
        dh1 = _dot_nt(dprojb, win_ref[...])
        xv = x_ref[...]
        rs = lax.rsqrt(jnp.mean(xv * xv, axis=-1, keepdims=True) + RMS_EPS)
        n1 = xv * rs
        gv1 = g_ref[...]
        h1_ref[...] = (n1 * gv1).astype(BF16)
        dg_ref[...] += jnp.sum(dh1 * n1, axis=0, keepdims=True)
        dx_ref[...] = dx1v + _rms_bwd(dh1, n1, rs, gv1)

        @pl.when(i == nt - 1)
        def _():
            tril = (lax.broadcasted_iota(jnp.int32, (2 * CHUNK, CHUNK), 0) % CHUNK
                    >= lax.broadcasted_iota(jnp.int32, (2 * CHUNK, CHUNK), 1))
            for j in range(3):
                dws_ref[j] = jnp.where(tril, dws_ref[j], 0.0)
            dbm_ref[...] = _group_mean_split(dbm_ref[...], avg) * float(HEAD_DIM)

    rev = lambda w: pl.BlockSpec((tm, w), lambda i: (nt - 1 - i, 0))
    prev = pl.BlockSpec((HALO, D_IN), lambda i: (jnp.maximum((nt - 1 - i) * prev_blocks - 1, 0), 0))
    acc_shapes = [(1, D_MODEL), (3, 2 * CHUNK, CHUNK), (CHUNK, D_A), (1, D_A), (1, D_A), (8, D_B), (D_C, D_C), (1, D_C)]
    return _tile_call(
        body, "mix_bwd", nt,
        [rev(D_MODEL), rev(D_MODEL), rev(D_IN), prev, _const_spec((1, D_MODEL)), _const_spec((D_MODEL, D_IN)),
         _const_spec((D_MODEL, D_MODEL)), _const_spec((3, CHUNK, 2 * CHUNK)), _const_spec((3, 2 * CHUNK, CHUNK)),
         _const_spec((CHUNK, D_A)), _const_spec((1, D_A)), _const_spec((1, D_A)), _const_spec((D_A, D_A)),
         _const_spec((8, D_B)), _const_spec((D_C, D_C)), _const_spec((D_C, D_C)), _const_spec((1, D_C))],
        [rev(D_MODEL), rev(D_MODEL), rev(D_IN)] + [_acc_spec(s) for s in acc_shapes],
        [jax.ShapeDtypeStruct((t, D_MODEL), F32), jax.ShapeDtypeStruct((t, D_MODEL), BF16),
         jax.ShapeDtypeStruct((t, D_IN), BF16)] + [jax.ShapeDtypeStruct(s, F32) for s in acc_shapes],
        [pltpu.VMEM((tm, D_MODEL), F32), pltpu.VMEM((tm, D_IN), F32),
         pltpu.VMEM((tm + HALO, D_B), F32), pltpu.VMEM((tm + HALO, D_C), F32),
         pltpu.VMEM((tm + HALO, D_B), F32), pltpu.VMEM((tm + HALO, D_C), F32)],
        (dx1, x, proj, proj, g_mix, w_in, w_out, wcat, wcat_t, bmat, ln_g, ln_b, avg, conv_w, pool_bd, pool_bd_t,
         pool_scale), comm)


def wgrad(a, b, *, tk, a_layer=None):
    t, m = a.shape[-2:]
    n = b.shape[1]
    bm = min(m, 1024)
    bn = 1024 if n % 1024 == 0 else n
    nk = t // tk
    if a_layer is None:
        a_spec = pl.BlockSpec((tk, bm), lambda i, j, k: (k, i))
    else:
        a_spec = pl.BlockSpec((None, None, tk, bm), lambda i, j, k: (a_layer, 0, k, i))

    def body(a_ref, b_ref, o_ref):
        k = pl.program_id(2)

        @pl.when(k == 0)
        def _():
            o_ref[...] = jnp.zeros_like(o_ref)

        o_ref[...] += _dot_tn(a_ref[...].astype(BF16), b_ref[...].astype(BF16))

    return pl.pallas_call(
        body, name=f"wgrad_{m}x{n}", grid=(m // bm, n // bn, nk),
        in_specs=[a_spec, pl.BlockSpec((tk, bn), lambda i, j, k: (k, j))],
        out_specs=pl.BlockSpec((bm, bn), lambda i, j, k: (i, j)),
        out_shape=jax.ShapeDtypeStruct((m, n), F32),
        compiler_params=_params("parallel", "parallel", "arbitrary"),
    )(a, b)


def _row_block(rows, cols, target_bytes):
    target = max(8, target_bytes // (4 * cols))
    if rows <= target:
        return rows
    best = None
    for br in range(8, target + 1, 8):
        if rows % br == 0:
            best = br
    return best if best is not None else rows


def adamw(w, g, m, v):
    shape = w.shape
    cols = shape[-1]
    rows = math.prod(shape[:-1]) if len(shape) > 1 else 1
    br = _row_block(rows, cols, 1 << 20)

    def body(w_ref, g_ref, m_ref, v_ref, d_ref, nm_ref, nv_ref):
        gv = g_ref[...]
        nm = ADAM_B1 * m_ref[...] + (1.0 - ADAM_B1) * gv
        nv = ADAM_B2 * v_ref[...] + (1.0 - ADAM_B2) * jnp.square(gv)
        m_hat = nm / (1.0 - ADAM_B1 ** ADAM_STEP)
        v_hat = nv / (1.0 - ADAM_B2 ** ADAM_STEP)
        d_ref[...] = -ADAM_LR * (m_hat / (jnp.sqrt(v_hat) + ADAM_EPS) + ADAM_WD * w_ref[...])
        nm_ref[...] = nm
        nv_ref[...] = nv

    spec = pl.BlockSpec((br, cols), lambda i: (i, 0))
    outs = pl.pallas_call(
        body, name="adamw", grid=(rows // br,),
        in_specs=[spec] * 4, out_specs=[spec] * 3,
        out_shape=[jax.ShapeDtypeStruct((rows, cols), F32)] * 3,
        compiler_params=pltpu.CompilerParams(dimension_semantics=("parallel",)),
    )(*(a.reshape(rows, cols) for a in (w, g, m, v)))
    return tuple(o.reshape(shape) for o in outs)


ADD_STEPS = 4


def add_halves(geoms, arrs, received, c_idx):
    n = len(arrs)

    def body(c_ref, *refs):
        del c_ref
        for a in range(n):
            refs[2 * n + a][...] = refs[a][...] + refs[n + a][...]

    own_specs, half_specs = [], []
    for g in geoms:
        if g.kind == "cols":
            rows, cols = g.half_shape[0] // ADD_STEPS, g.half_shape[1]
            own_specs.append(pl.BlockSpec((rows, cols), lambda i, c_ref: (ADD_STEPS * c_ref[0] + i, 0)))
            half_specs.append(pl.BlockSpec((rows, cols), lambda i, c_ref: (i, 0)))
        else:
            _, h, cols = g.half_shape
            own_specs.append(pl.BlockSpec((None, None, h, cols), lambda i, c_ref: (i, c_ref[0], 0, 0)))
            half_specs.append(pl.BlockSpec((None, h, cols), lambda i, c_ref: (i, 0, 0)))
    return pl.pallas_call(
        body, name="add_halves",
        grid_spec=pltpu.PrefetchScalarGridSpec(num_scalar_prefetch=1, grid=(ADD_STEPS,),
                                               in_specs=own_specs + half_specs, out_specs=half_specs),
        out_shape=[jax.ShapeDtypeStruct(g.half_shape, F32) for g in geoms],
        compiler_params=_params("parallel"),
    )(c_idx, *arrs, *received)


def add_parts(geoms, landed, finals, layer, c_idx):
    n = len(landed)

    def body(c_ref, *refs):
        del c_ref
        for a in range(n):
            p_ref = refs[a]
            refs[2 * n + a][...] = ((p_ref[0] + p_ref[1]) + p_ref[2]) + p_ref[3]

    in_specs, out_specs = [], []
    for g in geoms:
        rows, cols = g.part_shape[0] // ADD_STEPS, g.part_shape[1]
        in_specs.append(pl.BlockSpec((N_CHIPS, rows, cols), lambda i, c_ref: (0, i, 0)))
        out_specs.append(pl.BlockSpec((None, rows, cols), lambda i, c_ref: (layer, ADD_STEPS * c_ref[0] + i, 0)))
    return pl.pallas_call(
        body, name="add_parts",
        grid_spec=pltpu.PrefetchScalarGridSpec(num_scalar_prefetch=1, grid=(ADD_STEPS,),
                                               in_specs=in_specs + [_ANY] * n, out_specs=out_specs),
        out_shape=[jax.ShapeDtypeStruct(f.shape, F32) for f in finals],
        input_output_aliases={1 + n + a: a for a in range(n)},
        compiler_params=_params("parallel"),
    )(c_idx, *landed, *finals)


def _shard_dims(k, n, axis):
    return (k // N_CHIPS, n) if axis == 0 else (k, n // N_CHIPS)


W_IN_STRIDE = 512
W_IN_WINDOW = 640


def _big_geoms():
    geoms = []
    for name, k, n, axis in BIG:
        if axis == 0:
            geoms.append(_Geom("rows", (N_CHIPS, 2, k // N_CHIPS // 2, n)))
        elif name == "w_in":
            geoms.append(_Geom("cols", (k, n), W_IN_STRIDE, W_IN_WINDOW))
        else:
            geoms.append(_Geom("cols", (k, n), n // N_CHIPS, n // N_CHIPS))
    return geoms


def _grad_views(gb, geoms):
    return [gb[name].reshape(g.shape) for (name, _, _, _), g in zip(BIG, geoms)]


def _round_up(v, m):
    return (v + m - 1) // m * m


def _prep_small(small):
    tril = jnp.tril(jnp.ones((CHUNK, CHUNK), bool))
    wm = jnp.where(tril, small["sgu_w"], 0.0).astype(BF16).reshape(DEPTH, 3, 2, CHUNK, CHUNK)
    head = jnp.arange(D_A) // HEAD_DIM
    grp = jnp.arange(D_C) // HEAD_DIM
    pw_rows = small["pool_w"].reshape(DEPTH, D_C, HEAD_DIM)
    pool_bd = jnp.where((grp[:, None] == grp[None, :])[None], jnp.tile(pw_rows, (1, 1, D_C // HEAD_DIM)), 0.0).astype(BF16)
    return dict(
        wcat=wm.transpose(0, 1, 3, 2, 4).reshape(DEPTH, 3, CHUNK, 2 * CHUNK),
        wcat_t=wm.transpose(0, 1, 2, 4, 3).reshape(DEPTH, 3, 2 * CHUNK, CHUNK),
        bmat=jnp.repeat(jnp.swapaxes(small["sgu_b"], 1, 2), HEAD_DIM, axis=2),
        avg=jnp.where(head[:, None] == head[None, :], 1.0 / HEAD_DIM, 0.0).astype(BF16),
        pool_bd=pool_bd, pool_bd_t=jnp.swapaxes(pool_bd, 1, 2),
        conv8=jnp.pad(small["conv_w"], ((0, 0), (0, 8 - 3), (0, 0))),
    )


def _row(a):
    return a.reshape(1, -1)


MIX_WEIGHTS = ("w_in", "w_out")
MLP_WEIGHTS = ("w_ff1", "w_ff2", "w_ple_gate", "w_ple_proj")
ALL_BIG = MIX_WEIGHTS + MLP_WEIGHTS


def _fwd_layer(h, p, wl, small, prep, l, tm, comm_mix=None, comm_mlp=None):
    (proj, ycat, x1), got = mix_fwd(h, _row(small["norm_mix_g"][l]), wl["w_in"], wl["w_out"], prep["wcat"][l],
                                    prep["bmat"][l], _row(small["sgu_ln_g"][l]), _row(small["sgu_ln_b"][l]), prep["avg"],
                                    prep["conv8"][l], prep["pool_bd"][l], _row(small["pool_scale"][l]), tm=tm,
                                    comm=comm_mix)
    if comm_mix is not None:
        wl = {**wl, **_weights_of(got, MLP_WEIGHTS)}
    (a, x2, x3), couts = ffn_ple_fwd(x1, (p, l), _row(small["norm_ff_g"][l]), wl["w_ff1"], wl["w_ff2"],
                                     _row(small["norm_ple_g"][l]), wl["w_ple_gate"], wl["w_ple_proj"], tm=tm,
                                     comm=comm_mlp)
    return (h, proj, ycat, x1, a, x2), x3, couts, wl


class _Reducer:
    def __init__(self, finals, c_arr):
        self.finals, self.c_arr = list(finals), c_arr
        self.pending = None

    def push(self, layer, geoms, arrs):
        self.pending = (layer, geoms, arrs)

    def comm_a(self):
        return None if self.pending is None else _reduce_a_comm(self.pending[1], self.pending[2])

    def comm_b(self, received):
        _, geoms, arrs = self.pending
        return _reduce_b_comm(geoms, add_halves(geoms, arrs, received, self.c_arr))

    def comm_c(self, landed):
        layer, geoms, _ = self.pending
        n = len(geoms)
        return _reduce_c_comm(geoms, add_parts(geoms, landed, self.finals[:n], layer, self.c_arr), layer)

    def done(self, finals):
        self.finals[:len(finals)] = list(finals)
        self.pending = None

    def flush(self, tag):
        received = _run_comm(self.comm_a(), "reduce_a_" + tag)
        landed = _run_comm(self.comm_b(received), "reduce_b_" + tag)
        self.done(_run_comm(self.comm_c(landed), "reduce_c_" + tag))


def _bwd_layer(d, saved, p, wl, small, prep, l, tm, tk, red=None, carry_c=True):
    xin, proj, ycat, x1, a, x2 = saved
    busy = red is not None and red.pending is not None
    (dx2, h3, dpre, dpp, dg_ple), ca = ple_bwd(d, x2, (p, l), _row(small["norm_ple_g"][l]), wl["w_ple_gate"],
                                               wl["w_ple_proj"], tm=tm, comm=red.comm_a() if busy else None)
    gb = {"w_ple_gate": wgrad(h3, dpre, tk=tk), "w_ple_proj": wgrad(p, dpp, tk=tk, a_layer=l)}
    (dx1, h2, r, da, dg_ff), cb = ffn_bwd(dx2, x1, a, _row(small["norm_ff_g"][l]), wl["w_ff1"], wl["w_ff2"],
                                          tm=tm // 2, comm=red.comm_b(ca) if busy else None)
    gb["w_ff2"] = wgrad(r, dx2, tk=tk)
    gb["w_ff1"] = wgrad(h2, da, tk=tk)
    comm_c = red.comm_c(cb) if busy else None
    if busy and not carry_c:
        red.done(_run_comm(comm_c, f"reduce_c_{l + 1}"))
        comm_c = None
    (dprev, h1, dproj, dg_mix, dws, dbm, dlng, dlnb, dcw, dpw, dps), cc = mix_bwd(
        dx1, xin, proj, _row(small["norm_mix_g"][l]), wl["w_in"], wl["w_out"], prep["wcat"][l], prep["wcat_t"][l],
        prep["bmat"][l], _row(small["sgu_ln_g"][l]), _row(small["sgu_ln_b"][l]), prep["avg"], prep["conv8"][l],
        prep["pool_bd"][l], prep["pool_bd_t"][l], _row(small["pool_scale"][l]), tm=tm, comm=comm_c)
    if comm_c is not None:
        red.done(cc)
    gb["w_out"] = wgrad(ycat, dx1, tk=tk)
    gb["w_in"] = wgrad(h1, dproj, tk=tk)
    gs = {
        "norm_ple_g": dg_ple[0], "norm_ff_g": dg_ff[0], "norm_mix_g": dg_mix[0],
        "sgu_w": dws.reshape(2 * 3, CHUNK, CHUNK), "sgu_b": dbm[:, ::HEAD_DIM].T,
        "sgu_ln_g": dlng[0], "sgu_ln_b": dlnb[0], "conv_w": dcw[0:3], "pool_scale": dps[0],
        "pool_w": jnp.stack([dpw[g * HEAD_DIM:(g + 1) * HEAD_DIM, g * HEAD_DIM:(g + 1) * HEAD_DIM]
                             for g in range(D_C // HEAD_DIM)]),
    }
    return dprev, gb, gs


def _local_step(x, p, target, full, small, *, tm, tk):
    prep = _prep_small(small)
    p = p[:, None]
    saved, h = [], x
    for l in range(DEPTH):
        wl = {name: full[name][l] for name in full}
        s, h, _, _ = _fwd_layer(h, p, wl, small, prep, l, tm)
        saved.append(s)
    loss_blk, d_final_g, d = loss_head(h, target, _row(small["final_g"]), tm=tm)
    gbig, gsm = [None] * DEPTH, [None] * DEPTH
    for l in reversed(range(DEPTH)):
        wl = {name: full[name][l] for name in full}
        d, gbig[l], gsm[l] = _bwd_layer(d, saved[l], p, wl, small, prep, l, tm, tk)
    big = {name: jnp.stack([gbig[l][name] for l in range(DEPTH)]) for name in gbig[0]}
    sm = {name: jnp.stack([gsm[l][name] for l in range(DEPTH)]) for name in gsm[0]}
    sm["final_g"] = d_final_g[0]
    return loss_blk[0, 0], d, big, sm


def _weights_of(gathered, names):
    wl = dict(zip([b[0] for b in BIG if b[0] in names], gathered))
    if "w_in" in wl:
        wl["w_in"] = wl["w_in"].transpose(1, 0, 2).reshape(D_MODEL, D_IN)
    return wl


def kernel(x, p, norm_mix_g, w_in, sgu_w, sgu_b, sgu_ln_g, sgu_ln_b, conv_w, pool_w, pool_scale, w_out, norm_ff_g, w_ff1, w_ff2, norm_ple_g, w_ple_gate, w_ple_proj, final_g, loss_target, m_norm_mix_g, m_w_in, m_sgu_w, m_sgu_b, m_sgu_ln_g, m_sgu_ln_b, m_conv_w, m_pool_w, m_pool_scale, m_w_out, m_norm_ff_g, m_w_ff1, m_w_ff2, m_norm_ple_g, m_w_ple_gate, m_w_ple_proj, m_final_g, v_norm_mix_g, v_w_in, v_sgu_w, v_sgu_b, v_sgu_ln_g, v_sgu_ln_b, v_conv_w, v_pool_w, v_pool_scale, v_w_out, v_norm_ff_g, v_w_ff1, v_w_ff2, v_norm_ple_g, v_w_ple_gate, v_w_ple_proj, v_final_g):
    args = dict(locals())
    w = {name: args[name] for name in WEIGHTS}
    m = {name: args["m_" + name] for name in WEIGHTS}
    v = {name: args["v_" + name] for name in WEIGHTS}
    t = x.shape[1]
    tm = min(512, t)
    tk = min(2048, t)
    x_idx, y_idx, c_idx = _place()
    chip = 2 * x_idx + y_idx
    c_arr = c_idx.reshape(1).astype(jnp.int32)
    xs, target = x[0], loss_target[0]

    shards = {name: w[name].astype(BF16) for name, _, _, _ in BIG}
    conv_rows = _round_up(CONV_SHARD, 8 * 128) // 128
    conv_flat = jnp.pad(w["conv_w"].reshape(-1), (0, conv_rows * 128 - CONV_SHARD)).reshape(conv_rows, 128)
    first = _run_comm(_gather_comm(shards, 0, MIX_WEIGHTS, conv_flat), "gather_first")
    conv_full = (first[len(MIX_WEIGHTS)].reshape(N_CHIPS, -1)[:, :CONV_SHARD]
                 .reshape(N_CHIPS, DEPTH, 3, D_B // N_CHIPS).transpose(1, 2, 0, 3).reshape(DEPTH, 3, D_B))
    small = {name: w[name] for name in SMALL}
    small["conv_w"] = conv_full
    prep = _prep_small(small)

    wl = [None] * DEPTH
    wl[0] = _weights_of(first, MIX_WEIGHTS)
    saved, h = [], xs
    for l in range(DEPTH):
        comm_mix = _gather_comm(shards, 0, MLP_WEIGHTS) if l == 0 else None
        comm_mlp = _gather_comm(shards, l + 1, ALL_BIG) if l + 1 < DEPTH else None
        s, h, got, wl[l] = _fwd_layer(h, p, wl[l], small, prep, l, tm, comm_mix, comm_mlp)
        saved.append(s)
        if comm_mlp is not None:
            wl[l + 1] = _weights_of(got, ALL_BIG)

    loss_blk, d_final_g, d = loss_head(h, target, _row(small["final_g"]), tm=tm)

    geoms = _big_geoms()
    red = _Reducer([jnp.zeros((DEPTH,) + g.final_shape, F32) for g in geoms], c_arr)
    gsm = [None] * DEPTH
    for l in reversed(range(DEPTH)):
        d, gb, gsm[l] = _bwd_layer(d, saved[l], p, wl[l], small, prep, l, tm, tk, red, carry_c=l > 0)
        red.push(l, geoms, _grad_views(gb, geoms))

    sm = {name: jnp.stack([gsm[i][name] for i in range(DEPTH)]) for name in gsm[0]}
    sm["final_g"] = d_final_g[0]
    sizes = [sm[name].size for name in SMALL]
    small_rows = _round_up(-(-sum(sizes) // (2 * N_CHIPS * LANES)), 8 * ADD_STEPS)
    small_flat = jnp.pad(jnp.concatenate([sm[name].reshape(-1) for name in SMALL]),
                         (0, 2 * N_CHIPS * small_rows * LANES - sum(sizes)))
    small_geom = _Geom("rows", (N_CHIPS, 2, small_rows, LANES))
    red.push(0, geoms + [small_geom], red.pending[2] + [small_flat.reshape(small_geom.shape)])
    red.finals.append(jnp.zeros((1,) + small_geom.final_shape, F32))
    red.flush("tail")

    grads = {name: red.finals[a] for a, (name, _, _, _) in enumerate(BIG)}
    grads["w_in"] = lax.dynamic_slice_in_dim(grads["w_in"], chip * (D_IN // N_CHIPS - W_IN_STRIDE), D_IN // N_CHIPS, axis=2)
    small_red = _run_comm(_allgather_comm(red.finals[len(BIG)][0]), "small_allgather")[0].reshape(-1)
    off = 0
    for name, size in zip(SMALL, sizes):
        grads[name] = small_red[off:off + size].reshape(sm[name].shape)
        off += size
    grads["conv_w"] = lax.dynamic_slice_in_dim(grads["conv_w"], chip * (D_B // N_CHIPS), D_B // N_CHIPS, axis=2)

    loss = lax.psum(loss_blk[0, 0], ("x", "y", "c"))
    delta, new_m, new_v = {}, {}, {}
    for name in WEIGHTS:
        delta[name], new_m[name], new_v[name] = adamw(w[name], grads[name], m[name], v[name])
    return (loss, d[None], *[grads[n] for n in WEIGHTS], *[delta[n] for n in WEIGHTS],
            *[new_m[n] for n in WEIGHTS], *[new_v[n] for n in WEIGHTS])
```

```python
import math

import jax
import jax.numpy as jnp
from jax import lax
from jax.experimental import pallas as pl
from jax.experimental.pallas import tpu as pltpu

F32 = jnp.float32
BF16 = jnp.bfloat16

D_MODEL = 1024
DEPTH = 4
D_PLE = 256
D_FF = 4096
HEAD_DIM = 64
D_A = 384
D_B = 384
D_C = 256
D_IN = 2176
CHUNK = 128
HALO = 16
RMS_EPS = 1e-6
LN_EPS = 1e-5
N_CHIPS = 4
LANES = 1024

ADAM_LR = 0.001
ADAM_B1 = 0.9
ADAM_B2 = 0.999
ADAM_EPS = 1e-08
ADAM_WD = 0.01
ADAM_STEP = 10

VMEM_LIMIT_BYTES = 60 * 1024 * 1024

_RSQRT2 = 0.7071067811865476
_INV_SQRT_2PI = 0.3989422804014327

BIG = (
    ("w_in", D_MODEL, D_IN, 1),
    ("w_out", D_MODEL, D_MODEL, 0),
    ("w_ff1", D_MODEL, D_FF, 1),
    ("w_ff2", D_FF, D_MODEL, 0),
    ("w_ple_gate", D_MODEL, D_MODEL, 0),
    ("w_ple_proj", D_PLE, D_MODEL, 1),
)
SMALL = ("norm_mix_g", "sgu_w", "sgu_b", "sgu_ln_g", "sgu_ln_b", "conv_w", "pool_w", "pool_scale",
         "norm_ff_g", "norm_ple_g", "final_g")
WEIGHTS = ("norm_mix_g", "w_in", "sgu_w", "sgu_b", "sgu_ln_g", "sgu_ln_b", "conv_w", "pool_w", "pool_scale",
           "w_out", "norm_ff_g", "w_ff1", "w_ff2", "norm_ple_g", "w_ple_gate", "w_ple_proj", "final_g")
CONV_SHARD = DEPTH * 3 * (D_B // N_CHIPS)


def _dot(a, b):
    return jnp.dot(a, b, preferred_element_type=F32)


def _dot_nt(a, b):
    return lax.dot_general(a, b, (((1,), (1,)), ((), ())), preferred_element_type=F32)


def _dot_tn(a, b):
    return lax.dot_general(a, b, (((0,), (0,)), ((), ())), preferred_element_type=F32)


def _const_spec(shape):
    nd = len(shape)
    return pl.BlockSpec(shape, lambda i: (0,) * nd, pipeline_mode=pl.Buffered(1))


def _acc_spec(shape):
    nd = len(shape)
    return pl.BlockSpec(shape, lambda i: (0,) * nd)


def _layer_rows(layer, tm):
    return pl.BlockSpec((None, None, tm, D_PLE), lambda i: (layer, 0, i, 0))


def _params(*sem):
    return pltpu.CompilerParams(dimension_semantics=sem, vmem_limit_bytes=VMEM_LIMIT_BYTES)


def _rms_bwd(dh, n, rs, g):
    dn = dh * g
    return rs * (dn - n * jnp.mean(dn * n, axis=-1, keepdims=True))


def _gelu(x):
    return x * (0.5 * (1.0 + lax.erf(x * _RSQRT2)))


def _gelu_and_grad(x):
    cdf = 0.5 * (1.0 + lax.erf(x * _RSQRT2))
    return x * cdf, cdf + x * (jnp.exp(-0.5 * x * x) * _INV_SQRT_2PI)


def _group_mean(v, avg):
    vb = v.astype(BF16)
    split = 2 * CHUNK
    return jnp.concatenate([_dot(vb[:, :split], avg[:split, :split]), _dot(vb[:, split:], avg[split:, split:])], axis=1)


def _group_mean_split(v, avg):
    hi = v.astype(BF16)
    lo = (v - hi.astype(F32)).astype(BF16)
    return _dot(hi, avg) + _dot(lo, avg)


def _lane_lt(shape, bound):
    return lax.broadcasted_iota(jnp.int32, shape, 1) < bound


def _sgu_mix(vnb2, wcat_j, lo_mask):
    zero = jnp.zeros_like(vnb2)
    stacked = jnp.concatenate([jnp.where(lo_mask, vnb2, zero), jnp.where(lo_mask, zero, vnb2)], axis=0)
    return _dot(wcat_j, stacked)


def _pool_means(ext, tile_rows, first_pos):
    s2 = ext + pltpu.roll(ext, 1, 0)
    s4 = s2 + pltpu.roll(s2, 2, 0)
    s8 = s4 + pltpu.roll(s4, 4, 0)
    s16 = s8 + pltpu.roll(s8, 8, 0)
    pos = (first_pos + lax.broadcasted_iota(jnp.int32, (tile_rows, 1), 0) + 1).astype(F32)
    lane = lax.broadcasted_iota(jnp.int32, (tile_rows, D_C), 1)
    sums = jnp.where(lane < 64, s2[HALO:], jnp.where(lane < 128, s4[HALO:], jnp.where(lane < 192, s8[HALO:], s16[HALO:])))
    win = jnp.where(lane < 64, 2.0, jnp.where(lane < 128, 4.0, jnp.where(lane < 192, 8.0, 16.0)))
    inv = 1.0 / jnp.minimum(pos, win)
    return sums * inv, inv


MESH = pl.DeviceIdType.MESH
_ANY = pl.BlockSpec(memory_space=pl.ANY)


def _place():
    return lax.axis_index("x"), lax.axis_index("y"), lax.axis_index("c")


def _chip_peers(x, y):
    return [(1 - x, y), (x, 1 - y), (1 - x, 1 - y)]


class _Comm:
    def __init__(self, ins, out_shapes, sems, copies, aliases=None):
        self.ins, self.out_shapes, self.sems, self.copies = list(ins), list(out_shapes), list(sems), copies
        self.aliases = dict(aliases or {})

    def start(self, in_refs, out_refs, sem_refs):
        local, sends, _ = self.copies(in_refs, out_refs, sem_refs)
        for cp in local + sends:
            cp.start()

    def wait(self, in_refs, out_refs, sem_refs):
        local, sends, recvs = self.copies(in_refs, out_refs, sem_refs)
        for cp in recvs:
            cp.wait_recv()
        for cp in sends:
            cp.wait_send()
        for cp in local:
            cp.wait()


def _remote(src, dst, send_sem, recv_sem, device):
    return pltpu.make_async_remote_copy(src_ref=src, dst_ref=dst, send_sem=send_sem, recv_sem=recv_sem,
                                        device_id=device, device_id_type=MESH)


def _gather_comm(shards, layer, names, conv=None):
    mats = [b for b in BIG if b[0] in names]
    ins = [shards[name] for name, _, _, _ in mats] + ([conv] if conv is not None else [])
    out_shapes = []
    for name, k, n, axis in mats:
        shape = (N_CHIPS, k, n // N_CHIPS) if name == "w_in" else (k, n)
        out_shapes.append(jax.ShapeDtypeStruct(shape, BF16))
    if conv is not None:
        out_shapes.append(jax.ShapeDtypeStruct((N_CHIPS,) + conv.shape, conv.dtype))
    n_arr = len(ins)

    def block(a, out_ref, chip):
        if a == len(mats) or mats[a][0] == "w_in":
            return out_ref.at[chip]
        _, k, n, axis = mats[a]
        if axis == 0:
            return out_ref.at[pl.ds(chip * (k // N_CHIPS), k // N_CHIPS), :]
        return out_ref.at[:, pl.ds(chip * (n // N_CHIPS), n // N_CHIPS)]

    def copies(in_refs, out_refs, sem_refs):
        send_sems, recv_sems, local_sems = sem_refs
        x, y, c = _place()
        me = 2 * x + y
        local, sends, recvs = [], [], []
        for a in range(n_arr):
            src = in_refs[a].at[layer] if a < len(mats) else in_refs[a]
            local.append(pltpu.make_async_copy(src, block(a, out_refs[a], me), local_sems.at[a]))
            for j, (px, py) in enumerate(_chip_peers(x, y)):
                sends.append(_remote(src, block(a, out_refs[a], me), send_sems.at[a, j], recv_sems.at[a, j], (px, py, c)))
                recvs.append(_remote(src, block(a, out_refs[a], 2 * px + py), send_sems.at[a, j], recv_sems.at[a, j],
                                     (px, py, c)))
        return local, sends, recvs

    sems = [pltpu.SemaphoreType.DMA((n_arr, 3)), pltpu.SemaphoreType.DMA((n_arr, 3)), pltpu.SemaphoreType.DMA((n_arr,))]
    return _Comm(ins, out_shapes, sems, copies)


def _allgather_comm(a):
    def copies(in_refs, out_refs, sem_refs):
        send_sems, recv_sems, local_sem = sem_refs
        x, y, c = _place()
        me = 2 * x + y
        local = [pltpu.make_async_copy(in_refs[0], out_refs[0].at[me], local_sem)]
        sends, recvs = [], []
        for j, (px, py) in enumerate(_chip_peers(x, y)):
            sends.append(_remote(in_refs[0], out_refs[0].at[me], send_sems.at[j], recv_sems.at[j], (px, py, c)))
            recvs.append(_remote(in_refs[0], out_refs[0].at[2 * px + py], send_sems.at[j], recv_sems.at[j], (px, py, c)))
        return local, sends, recvs

    sems = [pltpu.SemaphoreType.DMA((3,)), pltpu.SemaphoreType.DMA((3,)), pltpu.SemaphoreType.DMA]
    return _Comm([a], [jax.ShapeDtypeStruct((N_CHIPS,) + a.shape, a.dtype)], sems, copies)


class _Geom:
    def __init__(self, kind, shape, stride=None, width=None):
        self.kind, self.shape, self.stride, self.width = kind, tuple(shape), stride, width
        if kind == "cols":
            k, n = shape
            self.half_shape, self.part_shape, self.final_shape = (k // 2, n), (k // 2, width), (k, width)
        else:
            _, _, h, n = shape
            self.half_shape, self.part_shape, self.final_shape = (N_CHIPS, h, n), (h, n), (2 * h, n)

    def half(self, ref, core):
        if self.kind == "cols":
            return ref.at[pl.ds(core * self.half_shape[0], self.half_shape[0]), :]
        return ref.at[:, core]

    def part(self, ref, chip):
        if self.kind == "cols":
            return ref.at[:, pl.ds(chip * self.stride, self.width)]
        return ref.at[chip]

    def final_half(self, ref, layer, core):
        rows = self.part_shape[0]
        return ref.at[layer, pl.ds(core * rows, rows), :]


def _reduce_a_comm(geoms, arrs):
    n = len(arrs)

    def copies(in_refs, out_refs, sem_refs):
        x, y, c = _place()
        cps = [_remote(geoms[a].half(in_refs[a], 1 - c), out_refs[a], sem_refs[0].at[a], sem_refs[1].at[a], (x, y, 1 - c))
               for a in range(n)]
        return [], cps, cps

    return _Comm(arrs, [jax.ShapeDtypeStruct(g.half_shape, F32) for g in geoms],
                 [pltpu.SemaphoreType.DMA((n,)), pltpu.SemaphoreType.DMA((n,))], copies)


def _reduce_b_comm(geoms, halves):
    n = len(halves)

    def copies(in_refs, out_refs, sem_refs):
        send_sems, recv_sems, local_sems = sem_refs
        x, y, c = _place()
        me = 2 * x + y
        local, sends, recvs = [], [], []
        for a in range(n):
            g = geoms[a]
            local.append(pltpu.make_async_copy(g.part(in_refs[a], me), out_refs[a].at[me], local_sems.at[a]))
            for j, (px, py) in enumerate(_chip_peers(x, y)):
                peer = 2 * px + py
                sends.append(_remote(g.part(in_refs[a], peer), out_refs[a].at[me], send_sems.at[a, j], recv_sems.at[a, j],
                                     (px, py, c)))
                recvs.append(_remote(g.part(in_refs[a], me), out_refs[a].at[peer], send_sems.at[a, j], recv_sems.at[a, j],
                                     (px, py, c)))
        return local, sends, recvs

    sems = [pltpu.SemaphoreType.DMA((n, 3)), pltpu.SemaphoreType.DMA((n, 3)), pltpu.SemaphoreType.DMA((n,))]
    return _Comm(halves, [jax.ShapeDtypeStruct((N_CHIPS,) + g.part_shape, F32) for g in geoms], sems, copies)


def _reduce_c_comm(geoms, finals, layer):
    n = len(finals)

    def copies(in_refs, out_refs, sem_refs):
        send_sems, recv_sems = sem_refs
        x, y, c = _place()
        sends, recvs = [], []
        for a in range(n):
            g = geoms[a]
            sends.append(_remote(g.final_half(in_refs[a], layer, c), g.final_half(out_refs[a], layer, c), send_sems.at[a],
                                 recv_sems.at[a], (x, y, 1 - c)))
            recvs.append(_remote(g.final_half(in_refs[a], layer, c), g.final_half(out_refs[a], layer, 1 - c),
                                 send_sems.at[a], recv_sems.at[a], (x, y, 1 - c)))
        return [], sends, recvs

    sems = [pltpu.SemaphoreType.DMA((n,)), pltpu.SemaphoreType.DMA((n,))]
    return _Comm(finals, [jax.ShapeDtypeStruct(f.shape, f.dtype) for f in finals], sems, copies,
                 aliases={a: a for a in range(n)})


def _run_comm(comm, name):
    def body(*refs):
        ni, no = len(comm.ins), len(comm.out_shapes)
        in_refs, out_refs, sem_refs = refs[:ni], refs[ni:ni + no], refs[ni + no:]
        comm.start(in_refs, out_refs, sem_refs)
        comm.wait(in_refs, out_refs, sem_refs)

    return pl.pallas_call(
        body, name=name, in_specs=[_ANY] * len(comm.ins), out_specs=[_ANY] * len(comm.out_shapes),
        out_shape=comm.out_shapes, scratch_shapes=comm.sems, input_output_aliases=comm.aliases,
        compiler_params=pltpu.CompilerParams(has_side_effects=True),
    )(*comm.ins)


def _tile_call(body, name, nt, in_specs, out_specs, out_shape, scratch, args, comm):
    if comm is None:
        outs = pl.pallas_call(body, name=name, grid=(nt,), in_specs=in_specs, out_specs=out_specs, out_shape=out_shape,
                              scratch_shapes=scratch, compiler_params=_params("arbitrary"))(*args)
        return outs, []
    n_in, n_out, n_scr = len(in_specs), len(out_specs), len(scratch)
    ci, co = len(comm.ins), len(comm.out_shapes)

    def hosted(*refs):
        in_refs = refs[:n_in]
        cin = refs[n_in:n_in + ci]
        out_refs = refs[n_in + ci:n_in + ci + n_out]
        cout = refs[n_in + ci + n_out:n_in + ci + n_out + co]
        scr = refs[n_in + ci + n_out + co:n_in + ci + n_out + co + n_scr]
        sems = refs[n_in + ci + n_out + co + n_scr:]
        i = pl.program_id(0)

        @pl.when(i == 0)
        def _():
            comm.start(cin, cout, sems)

        body(*in_refs, *out_refs, *scr)

        @pl.when(i == nt - 1)
        def _():
            comm.wait(cin, cout, sems)

    outs = pl.pallas_call(
        hosted, name=name + "_comm", grid=(nt,),
        in_specs=list(in_specs) + [_ANY] * ci, out_specs=list(out_specs) + [_ANY] * co,
        out_shape=list(out_shape) + comm.out_shapes, scratch_shapes=list(scratch) + comm.sems,
        input_output_aliases={n_in + i: n_out + o for i, o in comm.aliases.items()},
        compiler_params=_params("arbitrary"),
    )(*args, *comm.ins)
    return outs[:n_out], outs[n_out:]


def mix_fwd(x, g_mix, w_in, w_out, wcat, bmat, ln_g, ln_b, avg, conv_w, pool_bd, pool_scale, *, tm, comm=None):
    t = x.shape[0]
    nt = t // tm

    def body(x_ref, g_ref, win_ref, wout_ref, wcat_ref, bmat_ref, lng_ref, lnb_ref, avg_ref, cw_ref, pw_ref, ps_ref,
             proj_ref, ycat_ref, x1_ref, hbuf, zbuf):
        i = pl.program_id(0)

        @pl.when(i == 0)
        def _():
            hbuf[0:HALO, :] = jnp.zeros((HALO, D_B), F32)
            zbuf[0:HALO, :] = jnp.zeros((HALO, D_C), F32)

        xv = x_ref[...]
        n = xv * lax.rsqrt(jnp.mean(xv * xv, axis=-1, keepdims=True) + RMS_EPS)
        h1 = (n * g_ref[...]).astype(BF16)
        proj_ref[...] = _dot(h1, win_ref[...])

        lo_mask = _lane_lt((CHUNK, CHUNK), HEAD_DIM)
        avg = avg_ref[...]
        for c in range(tm // CHUNK):
            rows = pl.ds(c * CHUNK, CHUNK)
            gu = _gelu(proj_ref[rows, 0:D_A])
            gv = _gelu(proj_ref[rows, D_A:2 * D_A])
            dv = gv - _group_mean(gv, avg)
            var = _group_mean(dv * dv, avg)
            vnb = (dv * lax.rsqrt(var + LN_EPS) * lng_ref[...] + lnb_ref[...]).astype(BF16)
            for j in range(3):
                cols = slice(j * CHUNK, (j + 1) * CHUNK)
                mixed = _sgu_mix(vnb[:, cols], wcat_ref[j], lo_mask) + bmat_ref[:, cols]
                ycat_ref[rows, cols] = (gu[:, cols] * mixed).astype(BF16)

        o = 2 * D_A
        hcur = proj_ref[:, o + 2 * D_B:o + 3 * D_B] * proj_ref[:, o:o + D_B]
        hbuf[HALO:HALO + tm, :] = hcur
        y = (cw_ref[2:3, :] * hcur + cw_ref[1:2, :] * hbuf[pl.ds(HALO - 1, tm), :]
             + cw_ref[0:1, :] * hbuf[pl.ds(HALO - 2, tm), :])
        ycat_ref[:, D_A:D_A + D_B] = (proj_ref[:, o + D_B:o + 2 * D_B] * y).astype(BF16)
        hbuf[0:HALO, :] = hbuf[tm:tm + HALO, :]

        zc = proj_ref[:, o + 3 * D_B:D_IN]
        zbuf[HALO:HALO + tm, :] = zc
        mean, _ = _pool_means(zbuf[...], tm, i * tm)
        pooled = (mean - zc).astype(BF16)
        ycat_ref[:, D_A + D_B:D_MODEL] = (_dot(pooled, pw_ref[...]) * ps_ref[...]).astype(BF16)
        zbuf[0:HALO, :] = zbuf[tm:tm + HALO, :]

        x1_ref[...] = xv + _dot(ycat_ref[...], wout_ref[...])

    row = lambda w: pl.BlockSpec((tm, w), lambda i: (i, 0))
    return _tile_call(
        body, "mix_fwd", nt,
        [row(D_MODEL), _const_spec((1, D_MODEL)), _const_spec((D_MODEL, D_IN)), _const_spec((D_MODEL, D_MODEL)),
         _const_spec((3, CHUNK, 2 * CHUNK)), _const_spec((CHUNK, D_A)), _const_spec((1, D_A)), _const_spec((1, D_A)),
         _const_spec((D_A, D_A)), _const_spec((8, D_B)), _const_spec((D_C, D_C)), _const_spec((1, D_C))],
        [row(D_IN), row(D_MODEL), row(D_MODEL)],
        [jax.ShapeDtypeStruct((t, D_IN), F32), jax.ShapeDtypeStruct((t, D_MODEL), BF16),
         jax.ShapeDtypeStruct((t, D_MODEL), F32)],
        [pltpu.VMEM((tm + HALO, D_B), F32), pltpu.VMEM((tm + HALO, D_C), F32)],
        (x, g_mix, w_in, w_out, wcat, bmat, ln_g, ln_b, avg, conv_w, pool_bd, pool_scale), comm)


def ffn_ple_fwd(x1, p, g_ff, w_ff1, w_ff2, g_ple, w_gate, w_proj, *, tm, comm=None):
    t = x1.shape[0]
    nt = t // tm
    nc = D_FF // D_MODEL

    def body(x1_ref, p_ref, gff_ref, w1_ref, w2_ref, gple_ref, wg_ref, wp_ref, a_ref, x2_ref, x3_ref):
        x1v = x1_ref[...]
        n2 = x1v * lax.rsqrt(jnp.mean(x1v * x1v, axis=-1, keepdims=True) + RMS_EPS)
        h2 = (n2 * gff_ref[...]).astype(BF16)
        acc = x1v
        for c in range(nc):
            cols = slice(c * D_MODEL, (c + 1) * D_MODEL)
            a = _dot(h2, w1_ref[:, cols])
            a_ref[:, cols] = a.astype(BF16)
            ra = jnp.maximum(a, 0.0)
            acc = acc + _dot((ra * ra).astype(BF16), w2_ref[cols, :])
        x2_ref[...] = acc
        n3 = acc * lax.rsqrt(jnp.mean(acc * acc, axis=-1, keepdims=True) + RMS_EPS)
        h3 = (n3 * gple_ref[...]).astype(BF16)
        gate = jax.nn.sigmoid(_dot(h3, wg_ref[...]))
        pp = _dot(p_ref[...].astype(BF16), wp_ref[...])
        x3_ref[...] = acc + pp * gate

    row = lambda w: pl.BlockSpec((tm, w), lambda i: (i, 0))
    return _tile_call(
        body, "ffn_ple_fwd", nt,
        [row(D_MODEL), _layer_rows(p[1], tm), _const_spec((1, D_MODEL)), _const_spec((D_MODEL, D_FF)),
         _const_spec((D_FF, D_MODEL)), _const_spec((1, D_MODEL)), _const_spec((D_MODEL, D_MODEL)),
         _const_spec((D_PLE, D_MODEL))],
        [row(D_FF), row(D_MODEL), row(D_MODEL)],
        [jax.ShapeDtypeStruct((t, D_FF), BF16), jax.ShapeDtypeStruct((t, D_MODEL), F32),
         jax.ShapeDtypeStruct((t, D_MODEL), F32)],
        [], (x1, p[0], g_ff, w_ff1, w_ff2, g_ple, w_gate, w_proj), comm)


def loss_head(x, target, g, *, tm):
    t = x.shape[0]
    nt = t // tm

    def body(x_ref, t_ref, g_ref, loss_ref, dg_ref, dx_ref, sq_acc):
        i = pl.program_id(0)

        @pl.when(i == 0)
        def _():
            sq_acc[...] = jnp.zeros_like(sq_acc)
            dg_ref[...] = jnp.zeros_like(dg_ref)

        xv = x_ref[...]
        rs = lax.rsqrt(jnp.mean(xv * xv, axis=-1, keepdims=True) + RMS_EPS)
        n = xv * rs
        gv = g_ref[...]
        err = n * gv - t_ref[...]
        sq_acc[...] += jnp.sum(err * err, axis=0, keepdims=True)
        dy = err * (1.0 / D_MODEL)
        dg_ref[...] += jnp.sum(dy * n, axis=0, keepdims=True)
        dx_ref[...] = _rms_bwd(dy, n, rs, gv)

        @pl.when(i == nt - 1)
        def _():
            total = jnp.sum(sq_acc[...], axis=1, keepdims=True) * (0.5 / D_MODEL)
            loss_ref[...] = jnp.broadcast_to(total, loss_ref.shape)

    row = pl.BlockSpec((tm, D_MODEL), lambda i: (i, 0))
    return pl.pallas_call(
        body, name="loss_head", grid=(nt,),
        in_specs=[row, row, _const_spec((1, D_MODEL))],
        out_specs=[_acc_spec((8, 128)), _acc_spec((1, D_MODEL)), row],
        out_shape=[jax.ShapeDtypeStruct((8, 128), F32), jax.ShapeDtypeStruct((1, D_MODEL), F32),
                   jax.ShapeDtypeStruct((t, D_MODEL), F32)],
        scratch_shapes=[pltpu.VMEM((1, D_MODEL), F32)],
        compiler_params=_params("arbitrary"),
    )(x, target, g)


def ple_bwd(d, x2, p, g_ple, w_gate, w_proj, *, tm, sub=None, comm=None):
    t = d.shape[0]
    nt = t // tm
    sub = min(sub or tm, tm)

    def body(d_ref, x2_ref, p_ref, g_ref, wg_ref, wp_ref, dx2_ref, h3_ref, dpre_ref, dpp_ref, dg_ref):
        i = pl.program_id(0)

        @pl.when(i == 0)
        def _():
            dg_ref[...] = jnp.zeros_like(dg_ref)

        gv = g_ref[...]
        dg = jnp.zeros((1, D_MODEL), F32)
        for s in range(tm // sub):
            rows = pl.ds(s * sub, sub)
            dv = d_ref[rows, :]
            x2v = x2_ref[rows, :]
            rs = lax.rsqrt(jnp.mean(x2v * x2v, axis=-1, keepdims=True) + RMS_EPS)
            n3 = x2v * rs
            h3 = (n3 * gv).astype(BF16)
            h3_ref[rows, :] = h3
            gate = jax.nn.sigmoid(_dot(h3, wg_ref[...]))
            pp = _dot(p_ref[rows, :].astype(BF16), wp_ref[...])
            dpp_ref[rows, :] = (dv * gate).astype(BF16)
            dpre = (dv * pp * gate * (1.0 - gate)).astype(BF16)
            dpre_ref[rows, :] = dpre
            dh3 = _dot_nt(dpre, wg_ref[...])
            dg = dg + jnp.sum(dh3 * n3, axis=0, keepdims=True)
            dx2_ref[rows, :] = dv + _rms_bwd(dh3, n3, rs, gv)
        dg_ref[...] += dg

    row = lambda w: pl.BlockSpec((tm, w), lambda i: (i, 0))
    return _tile_call(
        body, "ple_bwd", nt,
        [row(D_MODEL), row(D_MODEL), _layer_rows(p[1], tm), _const_spec((1, D_MODEL)), _const_spec((D_MODEL, D_MODEL)),
         _const_spec((D_PLE, D_MODEL))],
        [row(D_MODEL), row(D_MODEL), row(D_MODEL), row(D_MODEL), _acc_spec((1, D_MODEL))],
        [jax.ShapeDtypeStruct((t, D_MODEL), F32), jax.ShapeDtypeStruct((t, D_MODEL), BF16),
         jax.ShapeDtypeStruct((t, D_MODEL), BF16), jax.ShapeDtypeStruct((t, D_MODEL), BF16),
         jax.ShapeDtypeStruct((1, D_MODEL), F32)],
        [], (d, x2, p[0], g_ple, w_gate, w_proj), comm)


def ffn_bwd(dx2, x1, a, g_ff, w_ff1, w_ff2, *, tm, comm=None):
    t = dx2.shape[0]
    nt = t // tm
    nc = D_FF // D_MODEL

    def body(dx2_ref, x1_ref, a_ref, g_ref, w1_ref, w2_ref, dx1_ref, h2_ref, r_ref, da_ref, dg_ref):
        i = pl.program_id(0)

        @pl.when(i == 0)
        def _():
            dg_ref[...] = jnp.zeros_like(dg_ref)

        dv = dx2_ref[...]
        x1v = x1_ref[...]
        rs = lax.rsqrt(jnp.mean(x1v * x1v, axis=-1, keepdims=True) + RMS_EPS)
        n2 = x1v * rs
        gv = g_ref[...]
        h2_ref[...] = (n2 * gv).astype(BF16)
        dvb = dv.astype(BF16)
        dh2 = jnp.zeros((tm, D_MODEL), F32)
        for c in range(nc):
            cols = slice(c * D_MODEL, (c + 1) * D_MODEL)
            ra = jnp.maximum(a_ref[:, cols].astype(F32), 0.0)
            r_ref[:, cols] = (ra * ra).astype(BF16)
            da = (_dot_nt(dvb, w2_ref[cols, :]) * (2.0 * ra)).astype(BF16)
            da_ref[:, cols] = da
            dh2 = dh2 + _dot_nt(da, w1_ref[:, cols])
        dg_ref[...] += jnp.sum(dh2 * n2, axis=0, keepdims=True)
        dx1_ref[...] = dv + _rms_bwd(dh2, n2, rs, gv)

    row = lambda w: pl.BlockSpec((tm, w), lambda i: (i, 0))
    return _tile_call(
        body, "ffn_bwd", nt,
        [row(D_MODEL), row(D_MODEL), row(D_FF), _const_spec((1, D_MODEL)), _const_spec((D_MODEL, D_FF)),
         _const_spec((D_FF, D_MODEL))],
        [row(D_MODEL), row(D_MODEL), row(D_FF), row(D_FF), _acc_spec((1, D_MODEL))],
        [jax.ShapeDtypeStruct((t, D_MODEL), F32), jax.ShapeDtypeStruct((t, D_MODEL), BF16),
         jax.ShapeDtypeStruct((t, D_FF), BF16), jax.ShapeDtypeStruct((t, D_FF), BF16),
         jax.ShapeDtypeStruct((1, D_MODEL), F32)],
        [], (dx2, x1, a, g_ff, w_ff1, w_ff2), comm)


def mix_bwd(dx1, x, proj, g_mix, w_in, w_out, wcat, wcat_t, bmat, ln_g, ln_b, avg, conv_w, pool_bd, pool_bd_t,
            pool_scale, *, tm, comm=None):
    t = dx1.shape[0]
    nt = t // tm
    prev_blocks = tm // HALO

    def body(dx1_ref, x_ref, proj_ref, prev_ref, g_ref, win_ref, wout_ref, wcat_ref, wcatt_ref, bmat_ref, lng_ref,
             lnb_ref, avg_ref, cw_ref, pw_ref, pwt_ref, ps_ref,
             dx_ref, h1_ref, dproj_ref, dg_ref, dws_ref, dbm_ref, dlng_ref, dlnb_ref, dcw_ref, dpw_ref, dps_ref,
             dyc, dpj, hbuf, zbuf, dybuf, qbuf):
        i = pl.program_id(0)
        ti = nt - 1 - i

        @pl.when(i == 0)
        def _():
            for ref in (dg_ref, dws_ref, dbm_ref, dlng_ref, dlnb_ref, dcw_ref, dpw_ref, dps_ref):
                ref[...] = jnp.zeros_like(ref)
            dybuf[tm:tm + HALO, :] = jnp.zeros((HALO, D_B), F32)
            qbuf[tm:tm + HALO, :] = jnp.zeros((HALO, D_C), F32)

        dx1v = dx1_ref[...]
        dyc[...] = _dot_nt(dx1v.astype(BF16), wout_ref[...])

        lo_mask = _lane_lt((CHUNK, CHUNK), HEAD_DIM)
        avg = avg_ref[...]
        lng = lng_ref[...]
        for c in range(tm // CHUNK):
            rows = pl.ds(c * CHUNK, CHUNK)
            gu, dgu = _gelu_and_grad(proj_ref[rows, 0:D_A])
            gv, dgv = _gelu_and_grad(proj_ref[rows, D_A:2 * D_A])
            cen = gv - _group_mean(gv, avg)
            rstd = lax.rsqrt(_group_mean(cen * cen, avg) + LN_EPS)
            vhat = cen * rstd
            vnb = (vhat * lng + lnb_ref[...]).astype(BF16)
            dya = dyc[rows, 0:D_A]
            dvn_parts = []
            for j in range(3):
                cols = slice(j * CHUNK, (j + 1) * CHUNK)
                vnb2 = vnb[:, cols]
                mixed = _sgu_mix(vnb2, wcat_ref[j], lo_mask) + bmat_ref[:, cols]
                dya2 = dya[:, cols]
                dpj[rows, cols] = dya2 * mixed * dgu[:, cols]
                dm = dya2 * gu[:, cols]
                dbm_ref[:, cols] += dm
                dmb = dm.astype(BF16)
                zero = jnp.zeros_like(dmb)
                dm_st = jnp.concatenate([jnp.where(lo_mask, dmb, zero), jnp.where(lo_mask, zero, dmb)], axis=0)
                dws_ref[j] += _dot_nt(dm_st, vnb2)
                dvn_st = _dot(wcatt_ref[j], dmb)
                dvn_parts.append(jnp.where(lo_mask, dvn_st[0:CHUNK], dvn_st[CHUNK:2 * CHUNK]))
            dvn = jnp.concatenate(dvn_parts, axis=1)
            dlng_ref[...] += jnp.sum(dvn * vhat, axis=0, keepdims=True)
            dlnb_ref[...] += jnp.sum(dvn, axis=0, keepdims=True)
            dvh = dvn * lng
            dgv_in = rstd * (dvh - _group_mean(dvh, avg) - vhat * _group_mean(dvh * vhat, avg))
            dpj[rows, D_A:2 * D_A] = dgv_in * dgv

        o = 2 * D_A
        live = (ti > 0).astype(F32)
        zb = proj_ref[:, o:o + D_B]
        gb = proj_ref[:, o + D_B:o + 2 * D_B]
        gc = proj_ref[:, o + 2 * D_B:o + 3 * D_B]
        hcur = gc * zb
        hbuf[0:HALO, :] = prev_ref[:, o + 2 * D_B:o + 3 * D_B] * prev_ref[:, o:o + D_B] * live
        hbuf[HALO:HALO + tm, :] = hcur
        hm1 = hbuf[pl.ds(HALO - 1, tm), :]
        hm2 = hbuf[pl.ds(HALO - 2, tm), :]
        y = cw_ref[2:3, :] * hcur + cw_ref[1:2, :] * hm1 + cw_ref[0:1, :] * hm2
        dout = dyc[:, D_A:D_A + D_B]
        dpj[:, o + D_B:o + 2 * D_B] = dout * y
        dy = dout * gb
        dcw_ref[2:3, :] += jnp.sum(dy * hcur, axis=0, keepdims=True)
        dcw_ref[1:2, :] += jnp.sum(dy * hm1, axis=0, keepdims=True)
        dcw_ref[0:1, :] += jnp.sum(dy * hm2, axis=0, keepdims=True)
        dybuf[0:tm, :] = dy
        dh = (cw_ref[2:3, :] * dy + cw_ref[1:2, :] * dybuf[pl.ds(1, tm), :] + cw_ref[0:1, :] * dybuf[pl.ds(2, tm), :])
        dybuf[tm:tm + HALO, :] = dybuf[0:HALO, :]
        dpj[:, o:o + D_B] = dh * gc
        dpj[:, o + 2 * D_B:o + 3 * D_B] = dh * zb

        zc = proj_ref[:, o + 3 * D_B:D_IN]
        zbuf[0:HALO, :] = prev_ref[:, o + 3 * D_B:D_IN] * live
        zbuf[HALO:HALO + tm, :] = zc
        mean, inv = _pool_means(zbuf[...], tm, ti * tm)
        pooled = (mean - zc).astype(BF16)
        dyp = dyc[:, D_A + D_B:D_MODEL]
        ps = ps_ref[...]
        dps_ref[...] += jnp.sum(dyp * _dot(pooled, pw_ref[...]), axis=0, keepdims=True)
        dpw = (dyp * ps).astype(BF16)
        dpw_ref[...] += _dot_tn(pooled, dpw)
        dpooled = _dot(dpw, pwt_ref[...])
        qbuf[0:tm, :] = dpooled * inv
        q = qbuf[...]
        nrows = tm + HALO
        f2 = q + pltpu.roll(q, nrows - 1, 0)
        f4 = f2 + pltpu.roll(f2, nrows - 2, 0)
        f8 = f4 + pltpu.roll(f4, nrows - 4, 0)
        f16 = f8 + pltpu.roll(f8, nrows - 8, 0)
        lane = lax.broadcasted_iota(jnp.int32, (tm, D_C), 1)
        ahead = jnp.where(lane < 64, f2[0:tm], jnp.where(lane < 128, f4[0:tm], jnp.where(lane < 192, f8[0:tm], f16[0:tm])))
        dpj[:, o + 3 * D_B:D_IN] = ahead - dpooled
        qbuf[tm:tm + HALO, :] = qbuf[0:HALO, :]

        dprojb = dpj[...].astype(BF16)
        dproj_ref[...] = dprojb
        dh1 = _dot_nt(dprojb, win_ref[...])
        xv = x_ref[...]
        rs = lax.rsqrt(jnp.mean(xv * xv, axis=-1, keepdims=True) + RMS_EPS)
        n1 = xv * rs
        gv1 = g_ref[...]
        h1_ref[...] = (n1 * gv1).astype(BF16)
        dg_ref[...] += jnp.sum(dh1 * n1, axis=0, keepdims=True)
        dx_ref[...] = dx1v + _rms_bwd(dh1, n1, rs, gv1)

        @pl.when(i == nt - 1)
        def _():
            tril = (lax.broadcasted_iota(jnp.int32, (2 * CHUNK, CHUNK), 0) % CHUNK
                    >= lax.broadcasted_iota(jnp.int32, (2 * CHUNK, CHUNK), 1))
            for j in range(3):
                dws_ref[j] = jnp.where(tril, dws_ref[j], 0.0)
            dbm_ref[...] = _group_mean_split(dbm_ref[...], avg) * float(HEAD_DIM)

    rev = lambda w: pl.BlockSpec((tm, w), lambda i: (nt - 1 - i, 0))
    prev = pl.BlockSpec((HALO, D_IN), lambda i: (jnp.maximum((nt - 1 - i) * prev_blocks - 1, 0), 0))
    acc_shapes = [(1, D_MODEL), (3, 2 * CHUNK, CHUNK), (CHUNK, D_A), (1, D_A), (1, D_A), (8, D_B), (D_C, D_C), (1, D_C)]
    return _tile_call(
        body, "mix_bwd", nt,
        [rev(D_MODEL), rev(D_MODEL), rev(D_IN), prev, _const_spec((1, D_MODEL)), _const_spec((D_MODEL, D_IN)),
         _const_spec((D_MODEL, D_MODEL)), _const_spec((3, CHUNK, 2 * CHUNK)), _const_spec((3, 2 * CHUNK, CHUNK)),
         _const_spec((CHUNK, D_A)), _const_spec((1, D_A)), _const_spec((1, D_A)), _const_spec((D_A, D_A)),
         _const_spec((8, D_B)), _const_spec((D_C, D_C)), _const_spec((D_C, D_C)), _const_spec((1, D_C))],
        [rev(D_MODEL), rev(D_MODEL), rev(D_IN)] + [_acc_spec(s) for s in acc_shapes],
        [jax.ShapeDtypeStruct((t, D_MODEL), F32), jax.ShapeDtypeStruct((t, D_MODEL), BF16),
         jax.ShapeDtypeStruct((t, D_IN), BF16)] + [jax.ShapeDtypeStruct(s, F32) for s in acc_shapes],
        [pltpu.VMEM((tm, D_MODEL), F32), pltpu.VMEM((tm, D_IN), F32),
         pltpu.VMEM((tm + HALO, D_B), F32), pltpu.VMEM((tm + HALO, D_C), F32),
         pltpu.VMEM((tm + HALO, D_B), F32), pltpu.VMEM((tm + HALO, D_C), F32)],
        (dx1, x, proj, proj, g_mix, w_in, w_out, wcat, wcat_t, bmat, ln_g, ln_b, avg, conv_w, pool_bd, pool_bd_t,
         pool_scale), comm)


def wgrad(a, b, *, tk, a_layer=None):
    t, m = a.shape[-2:]
    n = b.shape[1]
    bm = min(m, 1024)
    bn = 1024 if n % 1024 == 0 else n
    nk = t // tk
    if a_layer is None:
        a_spec = pl.BlockSpec((tk, bm), lambda i, j, k: (k, i))
    else:
        a_spec = pl.BlockSpec((None, None, tk, bm), lambda i, j, k: (a_layer, 0, k, i))

    def body(a_ref, b_ref, o_ref):
        k = pl.program_id(2)

        @pl.when(k == 0)
        def _():
            o_ref[...] = jnp.zeros_like(o_ref)

        o_ref[...] += _dot_tn(a_ref[...].astype(BF16), b_ref[...].astype(BF16))

    return pl.pallas_call(
        body, name=f"wgrad_{m}x{n}", grid=(m // bm, n // bn, nk),
        in_specs=[a_spec, pl.BlockSpec((tk, bn), lambda i, j, k: (k, j))],
        out_specs=pl.BlockSpec((bm, bn), lambda i, j, k: (i, j)),
        out_shape=jax.ShapeDtypeStruct((m, n), F32),
        compiler_params=_params("parallel", "parallel", "arbitrary"),
    )(a, b)


def _row_block(rows, cols, target_bytes):
    target = max(8, target_bytes // (4 * cols))
    if rows <= target:
        return rows
    best = None
    for br in range(8, target + 1, 8):
        if rows % br == 0:
            best = br
    return best if best is not None else rows


def adamw(w, g, m, v):
    shape = w.shape
    cols = shape[-1]
    rows = math.prod(shape[:-1]) if len(shape) > 1 else 1
    br = _row_block(rows, cols, 1 << 20)

    def body(w_ref, g_ref, m_ref, v_ref, d_ref, nm_ref, nv_ref):
        gv = g_ref[...]
        nm = ADAM_B1 * m_ref[...] + (1.0 - ADAM_B1) * gv
        nv = ADAM_B2 * v_ref[...] + (1.0 - ADAM_B2) * jnp.square(gv)
        m_hat = nm / (1.0 - ADAM_B1 ** ADAM_STEP)
        v_hat = nv / (1.0 - ADAM_B2 ** ADAM_STEP)
        d_ref[...] = -ADAM_LR * (m_hat / (jnp.sqrt(v_hat) + ADAM_EPS) + ADAM_WD * w_ref[...])
        nm_ref[...] = nm
        nv_ref[...] = nv

    spec = pl.BlockSpec((br, cols), lambda i: (i, 0))
    outs = pl.pallas_call(
        body, name="adamw", grid=(rows // br,),
        in_specs=[spec] * 4, out_specs=[spec] * 3,
        out_shape=[jax.ShapeDtypeStruct((rows, cols), F32)] * 3,
        compiler_params=pltpu.CompilerParams(dimension_semantics=("parallel",)),
    )(*(a.reshape(rows, cols) for a in (w, g, m, v)))
    return tuple(o.reshape(shape) for o in outs)


ADD_STEPS = 4


def add_halves(geoms, arrs, received, c_idx):
    n = len(arrs)

    def body(c_ref, *refs):
        del c_ref
        for a in range(n):
            refs[2 * n + a][...] = refs[a][...] + refs[n + a][...]

    own_specs, half_specs = [], []
    for g in geoms:
        if g.kind == "cols":
            rows, cols = g.half_shape[0] // ADD_STEPS, g.half_shape[1]
            own_specs.append(pl.BlockSpec((rows, cols), lambda i, c_ref: (ADD_STEPS * c_ref[0] + i, 0)))
            half_specs.append(pl.BlockSpec((rows, cols), lambda i, c_ref: (i, 0)))
        else:
            _, h, cols = g.half_shape
            own_specs.append(pl.BlockSpec((None, None, h, cols), lambda i, c_ref: (i, c_ref[0], 0, 0)))
            half_specs.append(pl.BlockSpec((None, h, cols), lambda i, c_ref: (i, 0, 0)))
    return pl.pallas_call(
        body, name="add_halves",
        grid_spec=pltpu.PrefetchScalarGridSpec(num_scalar_prefetch=1, grid=(ADD_STEPS,),
                                               in_specs=own_specs + half_specs, out_specs=half_specs),
        out_shape=[jax.ShapeDtypeStruct(g.half_shape, F32) for g in geoms],
        compiler_params=_params("parallel"),
    )(c_idx, *arrs, *received)


def add_parts(geoms, landed, finals, layer, c_idx):
    n = len(landed)

    def body(c_ref, *refs):
        del c_ref
        for a in range(n):
            p_ref = refs[a]
            refs[2 * n + a][...] = ((p_ref[0] + p_ref[1]) + p_ref[2]) + p_ref[3]

    in_specs, out_specs = [], []
    for g in geoms:
        rows, cols = g.part_shape[0] // ADD_STEPS, g.part_shape[1]
        in_specs.append(pl.BlockSpec((N_CHIPS, rows, cols), lambda i, c_ref: (0, i, 0)))
        out_specs.append(pl.BlockSpec((None, rows, cols), lambda i, c_ref: (layer, ADD_STEPS * c_ref[0] + i, 0)))
    return pl.pallas_call(
        body, name="add_parts",
        grid_spec=pltpu.PrefetchScalarGridSpec(num_scalar_prefetch=1, grid=(ADD_STEPS,),
                                               in_specs=in_specs + [_ANY] * n, out_specs=out_specs),
        out_shape=[jax.ShapeDtypeStruct(f.shape, F32) for f in finals],
        input_output_aliases={1 + n + a: a for a in range(n)},
        compiler_params=_params("parallel"),
    )(c_idx, *landed, *finals)


def _shard_dims(k, n, axis):
    return (k // N_CHIPS, n) if axis == 0 else (k, n // N_CHIPS)


W_IN_STRIDE = 512
W_IN_WINDOW = 640


def _big_geoms():
    geoms = []
    for name, k, n, axis in BIG:
        if axis == 0:
            geoms.append(_Geom("rows", (N_CHIPS, 2, k // N_CHIPS // 2, n)))
        elif name == "w_in":
            geoms.append(_Geom("cols", (k, n), W_IN_STRIDE, W_IN_WINDOW))
        else:
            geoms.append(_Geom("cols", (k, n), n // N_CHIPS, n // N_CHIPS))
    return geoms


def _grad_views(gb, geoms):
    return [gb[name].reshape(g.shape) for (name, _, _, _), g in zip(BIG, geoms)]


def _round_up(v, m):
    return (v + m - 1) // m * m


def _prep_small(small):
    tril = jnp.tril(jnp.ones((CHUNK, CHUNK), bool))
    wm = jnp.where(tril, small["sgu_w"], 0.0).astype(BF16).reshape(DEPTH, 3, 2, CHUNK, CHUNK)
    head = jnp.arange(D_A) // HEAD_DIM
    grp = jnp.arange(D_C) // HEAD_DIM
    pw_rows = small["pool_w"].reshape(DEPTH, D_C, HEAD_DIM)
    pool_bd = jnp.where((grp[:, None] == grp[None, :])[None], jnp.tile(pw_rows, (1, 1, D_C // HEAD_DIM)), 0.0).astype(BF16)
    return dict(
        wcat=wm.transpose(0, 1, 3, 2, 4).reshape(DEPTH, 3, CHUNK, 2 * CHUNK),
        wcat_t=wm.transpose(0, 1, 2, 4, 3).reshape(DEPTH, 3, 2 * CHUNK, CHUNK),
        bmat=jnp.repeat(jnp.swapaxes(small["sgu_b"], 1, 2), HEAD_DIM, axis=2),
        avg=jnp.where(head[:, None] == head[None, :], 1.0 / HEAD_DIM, 0.0).astype(BF16),
        pool_bd=pool_bd, pool_bd_t=jnp.swapaxes(pool_bd, 1, 2),
        conv8=jnp.pad(small["conv_w"], ((0, 0), (0, 8 - 3), (0, 0))),
    )


def _row(a):
    return a.reshape(1, -1)


MIX_WEIGHTS = ("w_in", "w_out")
MLP_WEIGHTS = ("w_ff1", "w_ff2", "w_ple_gate", "w_ple_proj")
ALL_BIG = MIX_WEIGHTS + MLP_WEIGHTS


def _fwd_layer(h, p, wl, small, prep, l, tm, comm_mix=None, comm_mlp=None):
    (proj, ycat, x1), got = mix_fwd(h, _row(small["norm_mix_g"][l]), wl["w_in"], wl["w_out"], prep["wcat"][l],
                                    prep["bmat"][l], _row(small["sgu_ln_g"][l]), _row(small["sgu_ln_b"][l]), prep["avg"],
                                    prep["conv8"][l], prep["pool_bd"][l], _row(small["pool_scale"][l]),
                                    tm=min(2 * tm, h.shape[0]), comm=comm_mix)
    if comm_mix is not None:
        wl = {**wl, **_weights_of(got, MLP_WEIGHTS)}
    (a, x2, x3), couts = ffn_ple_fwd(x1, (p, l), _row(small["norm_ff_g"][l]), wl["w_ff1"], wl["w_ff2"],
                                     _row(small["norm_ple_g"][l]), wl["w_ple_gate"], wl["w_ple_proj"], tm=tm,
                                     comm=comm_mlp)
    return (h, proj, ycat, x1, a, x2), x3, couts, wl


def _merge_comms(comms):
    comms = [cm for cm in comms if cm is not None]
    if len(comms) <= 1:
        return comms[0] if comms else None
    spans, ni, no, ns = [], 0, 0, 0
    for cm in comms:
        spans.append((ni, no, ns))
        ni, no, ns = ni + len(cm.ins), no + len(cm.out_shapes), ns + len(cm.sems)

    def copies(in_refs, out_refs, sem_refs):
        local, sends, recvs = [], [], []
        for cm, (i0, o0, s0) in zip(comms, spans):
            got = cm.copies(in_refs[i0:i0 + len(cm.ins)], out_refs[o0:o0 + len(cm.out_shapes)],
                            sem_refs[s0:s0 + len(cm.sems)])
            local, sends, recvs = local + got[0], sends + got[1], recvs + got[2]
        return local, sends, recvs

    aliases = {i0 + i: o0 + o for cm, (i0, o0, _) in zip(comms, spans) for i, o in cm.aliases.items()}
    return _Comm(sum((cm.ins for cm in comms), []), sum((cm.out_shapes for cm in comms), []),
                 sum((cm.sems for cm in comms), []), copies, aliases)


def _split_results(results, comms):
    out, at = [], 0
    for cm in comms:
        if cm is None:
            out.append(None)
        else:
            out.append(results[at:at + len(cm.out_shapes)])
            at += len(cm.out_shapes)
    return out


class _Reduction:
    def __init__(self, layer, names, geoms, arrs, finals, c_arr):
        self.layer, self.names, self.geoms, self.arrs = layer, list(names), list(geoms), list(arrs)
        self.finals, self.c_arr = finals, c_arr

    def comm_a(self):
        return _reduce_a_comm(self.geoms, self.arrs)

    def comm_b(self, received):
        return _reduce_b_comm(self.geoms, add_halves(self.geoms, self.arrs, received, self.c_arr))

    def comm_c(self, landed):
        mine = add_parts(self.geoms, landed, [self.finals[n] for n in self.names], self.layer, self.c_arr)
        return _reduce_c_comm(self.geoms, mine, self.layer)

    def done(self, results):
        self.finals.update(zip(self.names, results))


class _Plan:
    def ple(self):
        return None

    def after_ple(self, results):
        pass

    def ffn(self):
        return None

    def after_ffn(self, results):
        pass

    def before_mix(self, gb):
        pass

    def mix(self):
        return None

    def after_mix(self, results):
        pass


class _CarryPlan(_Plan):
    def __init__(self, above):
        self.above = above

    def ple(self):
        return self.above.comm_a()

    def after_ple(self, results):
        self.received = results

    def ffn(self):
        return self.above.comm_b(self.received)

    def after_ffn(self, results):
        self.landed = results

    def mix(self):
        return self.above.comm_c(self.landed)

    def after_mix(self, results):
        self.above.done(results)


class _LastPlan(_CarryPlan):
    def __init__(self, above, make_early):
        super().__init__(above)
        self.make_early = make_early

    def before_mix(self, gb):
        self.early = self.make_early(gb)
        self.early_received = _run_comm(self.early.comm_a(), "reduce_a_early")

    def mix(self):
        self.parts = [self.above.comm_c(self.landed), self.early.comm_b(self.early_received)]
        return _merge_comms(self.parts)

    def after_mix(self, results):
        above_res, self.early_landed = _split_results(results, self.parts)
        self.above.done(above_res)


def _bwd_layer(d, saved, p, wl, small, prep, l, tm, tk, plan=None):
    plan = plan or _Plan()
    xin, proj, ycat, x1, a, x2 = saved
    (dx2, h3, dpre, dpp, dg_ple), res = ple_bwd(d, x2, (p, l), _row(small["norm_ple_g"][l]), wl["w_ple_gate"],
                                                wl["w_ple_proj"], tm=tm, comm=plan.ple())
    plan.after_ple(res)
    gb = {"w_ple_gate": wgrad(h3, dpre, tk=tk), "w_ple_proj": wgrad(p, dpp, tk=tk, a_layer=l)}
    (dx1, h2, r, da, dg_ff), res = ffn_bwd(dx2, x1, a, _row(small["norm_ff_g"][l]), wl["w_ff1"], wl["w_ff2"],
                                           tm=tm // 2, comm=plan.ffn())
    plan.after_ffn(res)
    gb["w_ff2"] = wgrad(r, dx2, tk=tk)
    gb["w_ff1"] = wgrad(h2, da, tk=tk)
    plan.before_mix(gb)
    (dprev, h1, dproj, dg_mix, dws, dbm, dlng, dlnb, dcw, dpw, dps), res = mix_bwd(
        dx1, xin, proj, _row(small["norm_mix_g"][l]), wl["w_in"], wl["w_out"], prep["wcat"][l], prep["wcat_t"][l],
        prep["bmat"][l], _row(small["sgu_ln_g"][l]), _row(small["sgu_ln_b"][l]), prep["avg"], prep["conv8"][l],
        prep["pool_bd"][l], prep["pool_bd_t"][l], _row(small["pool_scale"][l]), tm=tm, comm=plan.mix())
    plan.after_mix(res)
    gb["w_out"] = wgrad(ycat, dx1, tk=tk)
    gb["w_in"] = wgrad(h1, dproj, tk=tk)
    gs = {
        "norm_ple_g": dg_ple[0], "norm_ff_g": dg_ff[0], "norm_mix_g": dg_mix[0],
        "sgu_w": dws.reshape(2 * 3, CHUNK, CHUNK), "sgu_b": dbm[:, ::HEAD_DIM].T,
        "sgu_ln_g": dlng[0], "sgu_ln_b": dlnb[0], "conv_w": dcw[0:3], "pool_scale": dps[0],
        "pool_w": jnp.stack([dpw[g * HEAD_DIM:(g + 1) * HEAD_DIM, g * HEAD_DIM:(g + 1) * HEAD_DIM]
                             for g in range(D_C // HEAD_DIM)]),
    }
    return dprev, gb, gs


def _local_step(x, p, target, full, small, *, tm, tk):
    prep = _prep_small(small)
    p = p[:, None]
    saved, h = [], x
    for l in range(DEPTH):
        wl = {name: full[name][l] for name in full}
        s, h, _, _ = _fwd_layer(h, p, wl, small, prep, l, tm)
        saved.append(s)
    loss_blk, d_final_g, d = loss_head(h, target, _row(small["final_g"]), tm=tm)
    gbig, gsm = [None] * DEPTH, [None] * DEPTH
    for l in reversed(range(DEPTH)):
        wl = {name: full[name][l] for name in full}
        d, gbig[l], gsm[l] = _bwd_layer(d, saved[l], p, wl, small, prep, l, tm, tk)
    big = {name: jnp.stack([gbig[l][name] for l in range(DEPTH)]) for name in gbig[0]}
    sm = {name: jnp.stack([gsm[l][name] for l in range(DEPTH)]) for name in gsm[0]}
    sm["final_g"] = d_final_g[0]
    return loss_blk[0, 0], d, big, sm


def _weights_of(gathered, names):
    wl = dict(zip([b[0] for b in BIG if b[0] in names], gathered))
    if "w_in" in wl:
        wl["w_in"] = wl["w_in"].transpose(1, 0, 2).reshape(D_MODEL, D_IN)
    return wl


def kernel(x, p, norm_mix_g, w_in, sgu_w, sgu_b, sgu_ln_g, sgu_ln_b, conv_w, pool_w, pool_scale, w_out, norm_ff_g, w_ff1, w_ff2, norm_ple_g, w_ple_gate, w_ple_proj, final_g, loss_target, m_norm_mix_g, m_w_in, m_sgu_w, m_sgu_b, m_sgu_ln_g, m_sgu_ln_b, m_conv_w, m_pool_w, m_pool_scale, m_w_out, m_norm_ff_g, m_w_ff1, m_w_ff2, m_norm_ple_g, m_w_ple_gate, m_w_ple_proj, m_final_g, v_norm_mix_g, v_w_in, v_sgu_w, v_sgu_b, v_sgu_ln_g, v_sgu_ln_b, v_conv_w, v_pool_w, v_pool_scale, v_w_out, v_norm_ff_g, v_w_ff1, v_w_ff2, v_norm_ple_g, v_w_ple_gate, v_w_ple_proj, v_final_g):
    args = dict(locals())
    w = {name: args[name] for name in WEIGHTS}
    m = {name: args["m_" + name] for name in WEIGHTS}
    v = {name: args["v_" + name] for name in WEIGHTS}
    t = x.shape[1]
    tm = min(512, t)
    tk = min(2048, t)
    x_idx, y_idx, c_idx = _place()
    chip = 2 * x_idx + y_idx
    c_arr = c_idx.reshape(1).astype(jnp.int32)
    xs, target = x[0], loss_target[0]

    shards = {name: w[name].astype(BF16) for name, _, _, _ in BIG}
    conv_rows = _round_up(CONV_SHARD, 8 * 128) // 128
    conv_flat = jnp.pad(w["conv_w"].reshape(-1), (0, conv_rows * 128 - CONV_SHARD)).reshape(conv_rows, 128)
    first = _run_comm(_gather_comm(shards, 0, MIX_WEIGHTS, conv_flat), "gather_first")
    conv_full = (first[len(MIX_WEIGHTS)].reshape(N_CHIPS, -1)[:, :CONV_SHARD]
                 .reshape(N_CHIPS, DEPTH, 3, D_B // N_CHIPS).transpose(1, 2, 0, 3).reshape(DEPTH, 3, D_B))
    small = {name: w[name] for name in SMALL}
    small["conv_w"] = conv_full
    prep = _prep_small(small)

    wl = [None] * DEPTH
    wl[0] = _weights_of(first, MIX_WEIGHTS)
    saved, h = [], xs
    for l in range(DEPTH):
        comm_mix = _gather_comm(shards, 0, MLP_WEIGHTS) if l == 0 else None
        comm_mlp = _gather_comm(shards, l + 1, ALL_BIG) if l + 1 < DEPTH else None
        s, h, got, wl[l] = _fwd_layer(h, p, wl[l], small, prep, l, tm, comm_mix, comm_mlp)
        saved.append(s)
        if comm_mlp is not None:
            wl[l + 1] = _weights_of(got, ALL_BIG)

    loss_blk, d_final_g, d = loss_head(h, target, _row(small["final_g"]), tm=tm)

    geoms = dict(zip([b[0] for b in BIG], _big_geoms()))
    finals = {name: jnp.zeros((DEPTH,) + g.final_shape, F32) for name, g in geoms.items()}

    def reduction(layer, names, gb):
        return _Reduction(layer, names, [geoms[n] for n in names], [gb[n].reshape(geoms[n].shape) for n in names],
                          finals, c_arr)

    gsm = [None] * DEPTH
    above = None
    for l in reversed(range(DEPTH)):
        if above is None:
            plan = _Plan()
        elif l > 0:
            plan = _CarryPlan(above)
        else:
            plan = _LastPlan(above, lambda gb: reduction(0, MLP_WEIGHTS, gb))
        d, gb, gsm[l] = _bwd_layer(d, saved[l], p, wl[l], small, prep, l, tm, tk, plan)
        if l > 0:
            above = reduction(l, ALL_BIG, gb)

    sm = {name: jnp.stack([gsm[i][name] for i in range(DEPTH)]) for name in gsm[0]}
    sm["final_g"] = d_final_g[0]
    sizes = [sm[name].size for name in SMALL]
    small_rows = _round_up(-(-sum(sizes) // (2 * N_CHIPS * LANES)), 8 * ADD_STEPS)
    small_flat = jnp.pad(jnp.concatenate([sm[name].reshape(-1) for name in SMALL]),
                         (0, 2 * N_CHIPS * small_rows * LANES - sum(sizes)))
    geoms["small"] = _Geom("rows", (N_CHIPS, 2, small_rows, LANES))
    finals["small"] = jnp.zeros((1,) + geoms["small"].final_shape, F32)
    late = reduction(0, MIX_WEIGHTS + ("small",), {**gb, "small": small_flat})
    late_landed = _run_comm(late.comm_b(_run_comm(late.comm_a(), "reduce_a_late")), "reduce_b_late")
    last = [plan.early.comm_c(plan.early_landed), late.comm_c(late_landed)]
    early_res, late_res = _split_results(_run_comm(_merge_comms(last), "reduce_c_last"), last)
    plan.early.done(early_res)
    late.done(late_res)

    grads = {name: finals[name] for name, _, _, _ in BIG}
    grads["w_in"] = lax.dynamic_slice_in_dim(grads["w_in"], chip * (D_IN // N_CHIPS - W_IN_STRIDE), D_IN // N_CHIPS, axis=2)
    small_red = _run_comm(_allgather_comm(finals["small"][0]), "small_allgather")[0].reshape(-1)
    off = 0
    for name, size in zip(SMALL, sizes):
        grads[name] = small_red[off:off + size].reshape(sm[name].shape)
        off += size
    grads["conv_w"] = lax.dynamic_slice_in_dim(grads["conv_w"], chip * (D_B // N_CHIPS), D_B // N_CHIPS, axis=2)

    loss = lax.psum(loss_blk[0, 0], ("x", "y", "c"))
    delta, new_m, new_v = {}, {}, {}
    for name in WEIGHTS:
        delta[name], new_m[name], new_v[name] = adamw(w[name], grads[name], m[name], v[name])
    return (loss, d[None], *[grads[n] for n in WEIGHTS], *[delta[n] for n in WEIGHTS],
            *[new_m[n] for n in WEIGHTS], *[new_v[n] for n in WEIGHTS])
```

```python
import math

import jax
import jax.numpy as jnp
from jax import lax
from jax.experimental import pallas as pl
from jax.experimental.pallas import tpu as pltpu

F32 = jnp.float32
BF16 = jnp.bfloat16

D_MODEL = 1024
DEPTH = 4
D_PLE = 256
D_FF = 4096
HEAD_DIM = 64
D_A = 384
D_B = 384
D_C = 256
D_IN = 2176
CHUNK = 128
HALO = 16
RMS_EPS = 1e-6
LN_EPS = 1e-5
N_CHIPS = 4
LANES = 1024

ADAM_LR = 0.001
ADAM_B1 = 0.9
ADAM_B2 = 0.999
ADAM_EPS = 1e-08
ADAM_WD = 0.01
ADAM_STEP = 10

VMEM_LIMIT_BYTES = 60 * 1024 * 1024

_RSQRT2 = 0.7071067811865476
_INV_SQRT_2PI = 0.3989422804014327

BIG = (
    ("w_in", D_MODEL, D_IN, 1),
    ("w_out", D_MODEL, D_MODEL, 0),
    ("w_ff1", D_MODEL, D_FF, 1),
    ("w_ff2", D_FF, D_MODEL, 0),
    ("w_ple_gate", D_MODEL, D_MODEL, 0),
    ("w_ple_proj", D_PLE, D_MODEL, 1),
)
SMALL = ("norm_mix_g", "sgu_w", "sgu_b", "sgu_ln_g", "sgu_ln_b", "conv_w", "pool_w", "pool_scale",
         "norm_ff_g", "norm_ple_g", "final_g")
WEIGHTS = ("norm_mix_g", "w_in", "sgu_w", "sgu_b", "sgu_ln_g", "sgu_ln_b", "conv_w", "pool_w", "pool_scale",
           "w_out", "norm_ff_g", "w_ff1", "w_ff2", "norm_ple_g", "w_ple_gate", "w_ple_proj", "final_g")
CONV_SHARD = DEPTH * 3 * (D_B // N_CHIPS)


def _dot(a, b):
    return jnp.dot(a, b, preferred_element_type=F32)


def _dot_nt(a, b):
    return lax.dot_general(a, b, (((1,), (1,)), ((), ())), preferred_element_type=F32)


def _dot_tn(a, b):
    return lax.dot_general(a, b, (((0,), (0,)), ((), ())), preferred_element_type=F32)


def _const_spec(shape):
    nd = len(shape)
    return pl.BlockSpec(shape, lambda i: (0,) * nd, pipeline_mode=pl.Buffered(1))


def _acc_spec(shape):
    nd = len(shape)
    return pl.BlockSpec(shape, lambda i: (0,) * nd)


def _layer_rows(layer, tm):
    return pl.BlockSpec((None, None, tm, D_PLE), lambda i: (layer, 0, i, 0))


def _params(*sem):
    return pltpu.CompilerParams(dimension_semantics=sem, vmem_limit_bytes=VMEM_LIMIT_BYTES)


def _rms_bwd(dh, n, rs, g):
    dn = dh * g
    return rs * (dn - n * jnp.mean(dn * n, axis=-1, keepdims=True))


def _gelu(x):
    return x * (0.5 * (1.0 + lax.erf(x * _RSQRT2)))


def _gelu_and_grad(x):
    cdf = 0.5 * (1.0 + lax.erf(x * _RSQRT2))
    return x * cdf, cdf + x * (jnp.exp(-0.5 * x * x) * _INV_SQRT_2PI)


def _group_mean(v, avg):
    vb = v.astype(BF16)
    split = 2 * CHUNK
    return jnp.concatenate([_dot(vb[:, :split], avg[:split, :split]), _dot(vb[:, split:], avg[split:, split:])], axis=1)


def _group_mean_split(v, avg):
    hi = v.astype(BF16)
    lo = (v - hi.astype(F32)).astype(BF16)
    return _dot(hi, avg) + _dot(lo, avg)


def _lane_lt(shape, bound):
    return lax.broadcasted_iota(jnp.int32, shape, 1) < bound


def _sgu_mix(vnb2, wcat_j, lo_mask):
    zero = jnp.zeros_like(vnb2)
    stacked = jnp.concatenate([jnp.where(lo_mask, vnb2, zero), jnp.where(lo_mask, zero, vnb2)], axis=0)
    return _dot(wcat_j, stacked)


def _pool_means(ext, tile_rows, first_pos):
    s2 = ext + pltpu.roll(ext, 1, 0)
    s4 = s2 + pltpu.roll(s2, 2, 0)
    s8 = s4 + pltpu.roll(s4, 4, 0)
    s16 = s8 + pltpu.roll(s8, 8, 0)
    pos = (first_pos + lax.broadcasted_iota(jnp.int32, (tile_rows, 1), 0) + 1).astype(F32)
    lane = lax.broadcasted_iota(jnp.int32, (tile_rows, D_C), 1)
    sums = jnp.where(lane < 64, s2[HALO:], jnp.where(lane < 128, s4[HALO:], jnp.where(lane < 192, s8[HALO:], s16[HALO:])))
    win = jnp.where(lane < 64, 2.0, jnp.where(lane < 128, 4.0, jnp.where(lane < 192, 8.0, 16.0)))
    inv = 1.0 / jnp.minimum(pos, win)
    return sums * inv, inv


MESH = pl.DeviceIdType.MESH
_ANY = pl.BlockSpec(memory_space=pl.ANY)


def _place():
    return lax.axis_index("x"), lax.axis_index("y"), lax.axis_index("c")


def _chip_peers(x, y):
    return [(1 - x, y), (x, 1 - y), (1 - x, 1 - y)]


class _Comm:
    def __init__(self, ins, out_shapes, sems, copies, aliases=None):
        self.ins, self.out_shapes, self.sems, self.copies = list(ins), list(out_shapes), list(sems), copies
        self.aliases = dict(aliases or {})

    def start(self, in_refs, out_refs, sem_refs):
        local, sends, _ = self.copies(in_refs, out_refs, sem_refs)
        for cp in local + sends:
            cp.start()

    def wait(self, in_refs, out_refs, sem_refs):
        local, sends, recvs = self.copies(in_refs, out_refs, sem_refs)
        for cp in recvs:
            cp.wait_recv()
        for cp in sends:
            cp.wait_send()
        for cp in local:
            cp.wait()


def _remote(src, dst, send_sem, recv_sem, device):
    return pltpu.make_async_remote_copy(src_ref=src, dst_ref=dst, send_sem=send_sem, recv_sem=recv_sem,
                                        device_id=device, device_id_type=MESH)


def _gather_comm(shards, layer, names, conv=None):
    mats = [b for b in BIG if b[0] in names]
    ins = [shards[name] for name, _, _, _ in mats] + ([conv] if conv is not None else [])
    out_shapes = []
    for name, k, n, axis in mats:
        shape = (N_CHIPS, k, n // N_CHIPS) if name == "w_in" else (k, n)
        out_shapes.append(jax.ShapeDtypeStruct(shape, BF16))
    if conv is not None:
        out_shapes.append(jax.ShapeDtypeStruct((N_CHIPS,) + conv.shape, conv.dtype))
    n_arr = len(ins)

    def block(a, out_ref, chip):
        if a == len(mats) or mats[a][0] == "w_in":
            return out_ref.at[chip]
        _, k, n, axis = mats[a]
        if axis == 0:
            return out_ref.at[pl.ds(chip * (k // N_CHIPS), k // N_CHIPS), :]
        return out_ref.at[:, pl.ds(chip * (n // N_CHIPS), n // N_CHIPS)]

    def copies(in_refs, out_refs, sem_refs):
        send_sems, recv_sems, local_sems = sem_refs
        x, y, c = _place()
        me = 2 * x + y
        local, sends, recvs = [], [], []
        for a in range(n_arr):
            src = in_refs[a].at[layer] if a < len(mats) else in_refs[a]
            local.append(pltpu.make_async_copy(src, block(a, out_refs[a], me), local_sems.at[a]))
            for j, (px, py) in enumerate(_chip_peers(x, y)):
                sends.append(_remote(src, block(a, out_refs[a], me), send_sems.at[a, j], recv_sems.at[a, j], (px, py, c)))
                recvs.append(_remote(src, block(a, out_refs[a], 2 * px + py), send_sems.at[a, j], recv_sems.at[a, j],
                                     (px, py, c)))
        return local, sends, recvs

    sems = [pltpu.SemaphoreType.DMA((n_arr, 3)), pltpu.SemaphoreType.DMA((n_arr, 3)), pltpu.SemaphoreType.DMA((n_arr,))]
    return _Comm(ins, out_shapes, sems, copies)


def _allgather_comm(a):
    def copies(in_refs, out_refs, sem_refs):
        send_sems, recv_sems, local_sem = sem_refs
        x, y, c = _place()
        me = 2 * x + y
        local = [pltpu.make_async_copy(in_refs[0], out_refs[0].at[me], local_sem)]
        sends, recvs = [], []
        for j, (px, py) in enumerate(_chip_peers(x, y)):
            sends.append(_remote(in_refs[0], out_refs[0].at[me], send_sems.at[j], recv_sems.at[j], (px, py, c)))
            recvs.append(_remote(in_refs[0], out_refs[0].at[2 * px + py], send_sems.at[j], recv_sems.at[j], (px, py, c)))
        return local, sends, recvs

    sems = [pltpu.SemaphoreType.DMA((3,)), pltpu.SemaphoreType.DMA((3,)), pltpu.SemaphoreType.DMA]
    return _Comm([a], [jax.ShapeDtypeStruct((N_CHIPS,) + a.shape, a.dtype)], sems, copies)


class _Geom:
    def __init__(self, kind, shape, stride=None, width=None):
        self.kind, self.shape, self.stride, self.width = kind, tuple(shape), stride, width
        if kind == "cols":
            k, n = shape
            self.half_shape, self.part_shape, self.final_shape = (k // 2, n), (k // 2, width), (k, width)
        else:
            _, _, h, n = shape
            self.half_shape, self.part_shape, self.final_shape = (N_CHIPS, h, n), (h, n), (2 * h, n)

    def half(self, ref, core):
        if self.kind == "cols":
            return ref.at[pl.ds(core * self.half_shape[0], self.half_shape[0]), :]
        return ref.at[:, core]

    def part(self, ref, chip):
        if self.kind == "cols":
            return ref.at[:, pl.ds(chip * self.stride, self.width)]
        return ref.at[chip]

    def final_half(self, ref, layer, core):
        rows = self.part_shape[0]
        return ref.at[layer, pl.ds(core * rows, rows), :]


def _reduce_a_comm(geoms, arrs):
    n = len(arrs)

    def copies(in_refs, out_refs, sem_refs):
        x, y, c = _place()
        cps = [_remote(geoms[a].half(in_refs[a], 1 - c), out_refs[a], sem_refs[0].at[a], sem_refs[1].at[a], (x, y, 1 - c))
               for a in range(n)]
        return [], cps, cps

    return _Comm(arrs, [jax.ShapeDtypeStruct(g.half_shape, F32) for g in geoms],
                 [pltpu.SemaphoreType.DMA((n,)), pltpu.SemaphoreType.DMA((n,))], copies)


def _reduce_b_comm(geoms, halves):
    n = len(halves)

    def copies(in_refs, out_refs, sem_refs):
        send_sems, recv_sems, local_sems = sem_refs
        x, y, c = _place()
        me = 2 * x + y
        local, sends, recvs = [], [], []
        for a in range(n):
            g = geoms[a]
            local.append(pltpu.make_async_copy(g.part(in_refs[a], me), out_refs[a].at[me], local_sems.at[a]))
            for j, (px, py) in enumerate(_chip_peers(x, y)):
                peer = 2 * px + py
                sends.append(_remote(g.part(in_refs[a], peer), out_refs[a].at[me], send_sems.at[a, j], recv_sems.at[a, j],
                                     (px, py, c)))
                recvs.append(_remote(g.part(in_refs[a], me), out_refs[a].at[peer], send_sems.at[a, j], recv_sems.at[a, j],
                                     (px, py, c)))
        return local, sends, recvs

    sems = [pltpu.SemaphoreType.DMA((n, 3)), pltpu.SemaphoreType.DMA((n, 3)), pltpu.SemaphoreType.DMA((n,))]
    return _Comm(halves, [jax.ShapeDtypeStruct((N_CHIPS,) + g.part_shape, h.dtype) for g, h in zip(geoms, halves)], sems,
                 copies)


def _reduce_c_comm(geoms, finals, layer):
    n = len(finals)

    def copies(in_refs, out_refs, sem_refs):
        send_sems, recv_sems = sem_refs
        x, y, c = _place()
        sends, recvs = [], []
        for a in range(n):
            g = geoms[a]
            sends.append(_remote(g.final_half(in_refs[a], layer, c), g.final_half(out_refs[a], layer, c), send_sems.at[a],
                                 recv_sems.at[a], (x, y, 1 - c)))
            recvs.append(_remote(g.final_half(in_refs[a], layer, c), g.final_half(out_refs[a], layer, 1 - c),
                                 send_sems.at[a], recv_sems.at[a], (x, y, 1 - c)))
        return [], sends, recvs

    sems = [pltpu.SemaphoreType.DMA((n,)), pltpu.SemaphoreType.DMA((n,))]
    return _Comm(finals, [jax.ShapeDtypeStruct(f.shape, f.dtype) for f in finals], sems, copies,
                 aliases={a: a for a in range(n)})


def _run_comm(comm, name):
    def body(*refs):
        ni, no = len(comm.ins), len(comm.out_shapes)
        in_refs, out_refs, sem_refs = refs[:ni], refs[ni:ni + no], refs[ni + no:]
        comm.start(in_refs, out_refs, sem_refs)
        comm.wait(in_refs, out_refs, sem_refs)

    return pl.pallas_call(
        body, name=name, in_specs=[_ANY] * len(comm.ins), out_specs=[_ANY] * len(comm.out_shapes),
        out_shape=comm.out_shapes, scratch_shapes=comm.sems, input_output_aliases=comm.aliases,
        compiler_params=pltpu.CompilerParams(has_side_effects=True),
    )(*comm.ins)


def _tile_call(body, name, nt, in_specs, out_specs, out_shape, scratch, args, comm):
    if comm is None:
        outs = pl.pallas_call(body, name=name, grid=(nt,), in_specs=in_specs, out_specs=out_specs, out_shape=out_shape,
                              scratch_shapes=scratch, compiler_params=_params("arbitrary"))(*args)
        return outs, []
    n_in, n_out, n_scr = len(in_specs), len(out_specs), len(scratch)
    ci, co = len(comm.ins), len(comm.out_shapes)

    def hosted(*refs):
        in_refs = refs[:n_in]
        cin = refs[n_in:n_in + ci]
        out_refs = refs[n_in + ci:n_in + ci + n_out]
        cout = refs[n_in + ci + n_out:n_in + ci + n_out + co]
        scr = refs[n_in + ci + n_out + co:n_in + ci + n_out + co + n_scr]
        sems = refs[n_in + ci + n_out + co + n_scr:]
        i = pl.program_id(0)

        @pl.when(i == 0)
        def _():
            comm.start(cin, cout, sems)

        body(*in_refs, *out_refs, *scr)

        @pl.when(i == nt - 1)
        def _():
            comm.wait(cin, cout, sems)

    outs = pl.pallas_call(
        hosted, name=name + "_comm", grid=(nt,),
        in_specs=list(in_specs) + [_ANY] * ci, out_specs=list(out_specs) + [_ANY] * co,
        out_shape=list(out_shape) + comm.out_shapes, scratch_shapes=list(scratch) + comm.sems,
        input_output_aliases={n_in + i: n_out + o for i, o in comm.aliases.items()},
        compiler_params=_params("arbitrary"),
    )(*args, *comm.ins)
    return outs[:n_out], outs[n_out:]


def mix_fwd(x, g_mix, w_in, w_out, wcat, bmat, ln_g, ln_b, avg, conv_w, pool_bd, pool_scale, *, tm, comm=None):
    t = x.shape[0]
    nt = t // tm

    def body(x_ref, g_ref, win_ref, wout_ref, wcat_ref, bmat_ref, lng_ref, lnb_ref, avg_ref, cw_ref, pw_ref, ps_ref,
             proj_ref, ycat_ref, x1_ref, hbuf, zbuf):
        i = pl.program_id(0)

        @pl.when(i == 0)
        def _():
            hbuf[0:HALO, :] = jnp.zeros((HALO, D_B), F32)
            zbuf[0:HALO, :] = jnp.zeros((HALO, D_C), F32)

        xv = x_ref[...]
        n = xv * lax.rsqrt(jnp.mean(xv * xv, axis=-1, keepdims=True) + RMS_EPS)
        h1 = (n * g_ref[...]).astype(BF16)
        proj_ref[...] = _dot(h1, win_ref[...])

        lo_mask = _lane_lt((CHUNK, CHUNK), HEAD_DIM)
        avg = avg_ref[...]
        for c in range(tm // CHUNK):
            rows = pl.ds(c * CHUNK, CHUNK)
            gu = _gelu(proj_ref[rows, 0:D_A])
            gv = _gelu(proj_ref[rows, D_A:2 * D_A])
            dv = gv - _group_mean(gv, avg)
            var = _group_mean(dv * dv, avg)
            vnb = (dv * lax.rsqrt(var + LN_EPS) * lng_ref[...] + lnb_ref[...]).astype(BF16)
            for j in range(3):
                cols = slice(j * CHUNK, (j + 1) * CHUNK)
                mixed = _sgu_mix(vnb[:, cols], wcat_ref[j], lo_mask) + bmat_ref[:, cols]
                ycat_ref[rows, cols] = (gu[:, cols] * mixed).astype(BF16)

        o = 2 * D_A
        hcur = proj_ref[:, o + 2 * D_B:o + 3 * D_B] * proj_ref[:, o:o + D_B]
        hbuf[HALO:HALO + tm, :] = hcur
        y = (cw_ref[2:3, :] * hcur + cw_ref[1:2, :] * hbuf[pl.ds(HALO - 1, tm), :]
             + cw_ref[0:1, :] * hbuf[pl.ds(HALO - 2, tm), :])
        ycat_ref[:, D_A:D_A + D_B] = (proj_ref[:, o + D_B:o + 2 * D_B] * y).astype(BF16)
        hbuf[0:HALO, :] = hbuf[tm:tm + HALO, :]

        zc = proj_ref[:, o + 3 * D_B:D_IN]
        zbuf[HALO:HALO + tm, :] = zc
        mean, _ = _pool_means(zbuf[...], tm, i * tm)
        pooled = (mean - zc).astype(BF16)
        ycat_ref[:, D_A + D_B:D_MODEL] = (_dot(pooled, pw_ref[...]) * ps_ref[...]).astype(BF16)
        zbuf[0:HALO, :] = zbuf[tm:tm + HALO, :]

        x1_ref[...] = xv + _dot(ycat_ref[...], wout_ref[...])

    row = lambda w: pl.BlockSpec((tm, w), lambda i: (i, 0))
    return _tile_call(
        body, "mix_fwd", nt,
        [row(D_MODEL), _const_spec((1, D_MODEL)), _const_spec((D_MODEL, D_IN)), _const_spec((D_MODEL, D_MODEL)),
         _const_spec((3, CHUNK, 2 * CHUNK)), _const_spec((CHUNK, D_A)), _const_spec((1, D_A)), _const_spec((1, D_A)),
         _const_spec((D_A, D_A)), _const_spec((8, D_B)), _const_spec((D_C, D_C)), _const_spec((1, D_C))],
        [row(D_IN), row(D_MODEL), row(D_MODEL)],
        [jax.ShapeDtypeStruct((t, D_IN), F32), jax.ShapeDtypeStruct((t, D_MODEL), BF16),
         jax.ShapeDtypeStruct((t, D_MODEL), F32)],
        [pltpu.VMEM((tm + HALO, D_B), F32), pltpu.VMEM((tm + HALO, D_C), F32)],
        (x, g_mix, w_in, w_out, wcat, bmat, ln_g, ln_b, avg, conv_w, pool_bd, pool_scale), comm)


def ffn_ple_fwd(x1, p, g_ff, w_ff1, w_ff2, g_ple, w_gate, w_proj, *, tm, comm=None):
    t = x1.shape[0]
    nt = t // tm
    nc = D_FF // D_MODEL

    def body(x1_ref, p_ref, gff_ref, w1_ref, w2_ref, gple_ref, wg_ref, wp_ref, a_ref, x2_ref, x3_ref):
        x1v = x1_ref[...]
        n2 = x1v * lax.rsqrt(jnp.mean(x1v * x1v, axis=-1, keepdims=True) + RMS_EPS)
        h2 = (n2 * gff_ref[...]).astype(BF16)
        acc = x1v
        for c in range(nc):
            cols = slice(c * D_MODEL, (c + 1) * D_MODEL)
            a = _dot(h2, w1_ref[:, cols])
            a_ref[:, cols] = a.astype(BF16)
            ra = jnp.maximum(a, 0.0)
            acc = acc + _dot((ra * ra).astype(BF16), w2_ref[cols, :])
        x2_ref[...] = acc
        n3 = acc * lax.rsqrt(jnp.mean(acc * acc, axis=-1, keepdims=True) + RMS_EPS)
        h3 = (n3 * gple_ref[...]).astype(BF16)
        gate = jax.nn.sigmoid(_dot(h3, wg_ref[...]))
        pp = _dot(p_ref[...].astype(BF16), wp_ref[...])
        x3_ref[...] = acc + pp * gate

    row = lambda w: pl.BlockSpec((tm, w), lambda i: (i, 0))
    return _tile_call(
        body, "ffn_ple_fwd", nt,
        [row(D_MODEL), _layer_rows(p[1], tm), _const_spec((1, D_MODEL)), _const_spec((D_MODEL, D_FF)),
         _const_spec((D_FF, D_MODEL)), _const_spec((1, D_MODEL)), _const_spec((D_MODEL, D_MODEL)),
         _const_spec((D_PLE, D_MODEL))],
        [row(D_FF), row(D_MODEL), row(D_MODEL)],
        [jax.ShapeDtypeStruct((t, D_FF), BF16), jax.ShapeDtypeStruct((t, D_MODEL), F32),
         jax.ShapeDtypeStruct((t, D_MODEL), F32)],
        [], (x1, p[0], g_ff, w_ff1, w_ff2, g_ple, w_gate, w_proj), comm)


def loss_head(x, target, g, *, tm):
    t = x.shape[0]
    nt = t // tm

    def body(x_ref, t_ref, g_ref, loss_ref, dg_ref, dx_ref, sq_acc):
        i = pl.program_id(0)

        @pl.when(i == 0)
        def _():
            sq_acc[...] = jnp.zeros_like(sq_acc)
            dg_ref[...] = jnp.zeros_like(dg_ref)

        xv = x_ref[...]
        rs = lax.rsqrt(jnp.mean(xv * xv, axis=-1, keepdims=True) + RMS_EPS)
        n = xv * rs
        gv = g_ref[...]
        err = n * gv - t_ref[...]
        sq_acc[...] += jnp.sum(err * err, axis=0, keepdims=True)
        dy = err * (1.0 / D_MODEL)
        dg_ref[...] += jnp.sum(dy * n, axis=0, keepdims=True)
        dx_ref[...] = _rms_bwd(dy, n, rs, gv)

        @pl.when(i == nt - 1)
        def _():
            total = jnp.sum(sq_acc[...], axis=1, keepdims=True) * (0.5 / D_MODEL)
            loss_ref[...] = jnp.broadcast_to(total, loss_ref.shape)

    row = pl.BlockSpec((tm, D_MODEL), lambda i: (i, 0))
    return pl.pallas_call(
        body, name="loss_head", grid=(nt,),
        in_specs=[row, row, _const_spec((1, D_MODEL))],
        out_specs=[_acc_spec((8, 128)), _acc_spec((1, D_MODEL)), row],
        out_shape=[jax.ShapeDtypeStruct((8, 128), F32), jax.ShapeDtypeStruct((1, D_MODEL), F32),
                   jax.ShapeDtypeStruct((t, D_MODEL), F32)],
        scratch_shapes=[pltpu.VMEM((1, D_MODEL), F32)],
        compiler_params=_params("arbitrary"),
    )(x, target, g)


def ple_bwd(d, x2, p, g_ple, w_gate, w_proj, *, tm, sub=None, comm=None):
    t = d.shape[0]
    nt = t // tm
    sub = min(sub or tm, tm)

    def body(d_ref, x2_ref, p_ref, g_ref, wg_ref, wp_ref, dx2_ref, h3_ref, dpre_ref, dpp_ref, dg_ref):
        i = pl.program_id(0)

        @pl.when(i == 0)
        def _():
            dg_ref[...] = jnp.zeros_like(dg_ref)

        gv = g_ref[...]
        dg = jnp.zeros((1, D_MODEL), F32)
        for s in range(tm // sub):
            rows = pl.ds(s * sub, sub)
            dv = d_ref[rows, :]
            x2v = x2_ref[rows, :]
            rs = lax.rsqrt(jnp.mean(x2v * x2v, axis=-1, keepdims=True) + RMS_EPS)
            n3 = x2v * rs
            h3 = (n3 * gv).astype(BF16)
            h3_ref[rows, :] = h3
            gate = jax.nn.sigmoid(_dot(h3, wg_ref[...]))
            pp = _dot(p_ref[rows, :].astype(BF16), wp_ref[...])
            dpp_ref[rows, :] = (dv * gate).astype(BF16)
            dpre = (dv * pp * gate * (1.0 - gate)).astype(BF16)
            dpre_ref[rows, :] = dpre
            dh3 = _dot_nt(dpre, wg_ref[...])
            dg = dg + jnp.sum(dh3 * n3, axis=0, keepdims=True)
            dx2_ref[rows, :] = dv + _rms_bwd(dh3, n3, rs, gv)
        dg_ref[...] += dg

    row = lambda w: pl.BlockSpec((tm, w), lambda i: (i, 0))
    return _tile_call(
        body, "ple_bwd", nt,
        [row(D_MODEL), row(D_MODEL), _layer_rows(p[1], tm), _const_spec((1, D_MODEL)), _const_spec((D_MODEL, D_MODEL)),
         _const_spec((D_PLE, D_MODEL))],
        [row(D_MODEL), row(D_MODEL), row(D_MODEL), row(D_MODEL), _acc_spec((1, D_MODEL))],
        [jax.ShapeDtypeStruct((t, D_MODEL), F32), jax.ShapeDtypeStruct((t, D_MODEL), BF16),
         jax.ShapeDtypeStruct((t, D_MODEL), BF16), jax.ShapeDtypeStruct((t, D_MODEL), BF16),
         jax.ShapeDtypeStruct((1, D_MODEL), F32)],
        [], (d, x2, p[0], g_ple, w_gate, w_proj), comm)


def ffn_bwd(dx2, x1, a, g_ff, w_ff1, w_ff2, *, tm, comm=None):
    t = dx2.shape[0]
    nt = t // tm
    nc = D_FF // D_MODEL

    def body(dx2_ref, x1_ref, a_ref, g_ref, w1_ref, w2_ref, dx1_ref, h2_ref, da_ref, dg_ref):
        i = pl.program_id(0)

        @pl.when(i == 0)
        def _():
            dg_ref[...] = jnp.zeros_like(dg_ref)

        dv = dx2_ref[...]
        x1v = x1_ref[...]
        rs = lax.rsqrt(jnp.mean(x1v * x1v, axis=-1, keepdims=True) + RMS_EPS)
        n2 = x1v * rs
        gv = g_ref[...]
        h2_ref[...] = (n2 * gv).astype(BF16)
        dvb = dv.astype(BF16)
        dh2 = jnp.zeros((tm, D_MODEL), F32)
        for c in range(nc):
            cols = slice(c * D_MODEL, (c + 1) * D_MODEL)
            ra = jnp.maximum(a_ref[:, cols].astype(F32), 0.0)
            da = (_dot_nt(dvb, w2_ref[cols, :]) * (2.0 * ra)).astype(BF16)
            da_ref[:, cols] = da
            dh2 = dh2 + _dot_nt(da, w1_ref[:, cols])
        dg_ref[...] += jnp.sum(dh2 * n2, axis=0, keepdims=True)
        dx1_ref[...] = dv + _rms_bwd(dh2, n2, rs, gv)

    row = lambda w: pl.BlockSpec((tm, w), lambda i: (i, 0))
    return _tile_call(
        body, "ffn_bwd", nt,
        [row(D_MODEL), row(D_MODEL), row(D_FF), _const_spec((1, D_MODEL)), _const_spec((D_MODEL, D_FF)),
         _const_spec((D_FF, D_MODEL))],
        [row(D_MODEL), row(D_MODEL), row(D_FF), _acc_spec((1, D_MODEL))],
        [jax.ShapeDtypeStruct((t, D_MODEL), F32), jax.ShapeDtypeStruct((t, D_MODEL), BF16),
         jax.ShapeDtypeStruct((t, D_FF), BF16), jax.ShapeDtypeStruct((1, D_MODEL), F32)],
        [], (dx2, x1, a, g_ff, w_ff1, w_ff2), comm)


def mix_bwd(dx1, x, proj, g_mix, w_in, w_out, wcat, wcat_t, bmat, ln_g, ln_b, avg, conv_w, pool_bd, pool_bd_t,
            pool_scale, *, tm, comm=None):
    t = dx1.shape[0]
    nt = t // tm
    prev_blocks = tm // HALO

    def body(dx1_ref, x_ref, proj_ref, prev_ref, g_ref, win_ref, wout_ref, wcat_ref, wcatt_ref, bmat_ref, lng_ref,
             lnb_ref, avg_ref, cw_ref, pw_ref, pwt_ref, ps_ref,
             dx_ref, h1_ref, dproj_ref, dg_ref, dws_ref, dbm_ref, dlng_ref, dlnb_ref, dcw_ref, dpw_ref, dps_ref,
             dyc, dpj, hbuf, zbuf, dybuf, qbuf):
        i = pl.program_id(0)
        ti = nt - 1 - i

        @pl.when(i == 0)
        def _():
            for ref in (dg_ref, dws_ref, dbm_ref, dlng_ref, dlnb_ref, dcw_ref, dpw_ref, dps_ref):
                ref[...] = jnp.zeros_like(ref)
            dybuf[tm:tm + HALO, :] = jnp.zeros((HALO, D_B), F32)
            qbuf[tm:tm + HALO, :] = jnp.zeros((HALO, D_C), F32)

        dx1v = dx1_ref[...]
        dyc[...] = _dot_nt(dx1v.astype(BF16), wout_ref[...])

        lo_mask = _lane_lt((CHUNK, CHUNK), HEAD_DIM)
        avg = avg_ref[...]
        lng = lng_ref[...]
        for c in range(tm // CHUNK):
            rows = pl.ds(c * CHUNK, CHUNK)
            gu, dgu = _gelu_and_grad(proj_ref[rows, 0:D_A])
            gv, dgv = _gelu_and_grad(proj_ref[rows, D_A:2 * D_A])
            cen = gv - _group_mean(gv, avg)
            rstd = lax.rsqrt(_group_mean(cen * cen, avg) + LN_EPS)
            vhat = cen * rstd
            vnb = (vhat * lng + lnb_ref[...]).astype(BF16)
            dya = dyc[rows, 0:D_A]
            dvn_parts = []
            for j in range(3):
                cols = slice(j * CHUNK, (j + 1) * CHUNK)
                vnb2 = vnb[:, cols]
                mixed = _sgu_mix(vnb2, wcat_ref[j], lo_mask) + bmat_ref[:, cols]
                dya2 = dya[:, cols]
                dpj[rows, cols] = dya2 * mixed * dgu[:, cols]
                dm = dya2 * gu[:, cols]
                dbm_ref[:, cols] += dm
                dmb = dm.astype(BF16)
                zero = jnp.zeros_like(dmb)
                dm_st = jnp.concatenate([jnp.where(lo_mask, dmb, zero), jnp.where(lo_mask, zero, dmb)], axis=0)
                dws_ref[j] += _dot_nt(dm_st, vnb2)
                dvn_st = _dot(wcatt_ref[j], dmb)
                dvn_parts.append(jnp.where(lo_mask, dvn_st[0:CHUNK], dvn_st[CHUNK:2 * CHUNK]))
            dvn = jnp.concatenate(dvn_parts, axis=1)
            dlng_ref[...] += jnp.sum(dvn * vhat, axis=0, keepdims=True)
            dlnb_ref[...] += jnp.sum(dvn, axis=0, keepdims=True)
            dvh = dvn * lng
            dgv_in = rstd * (dvh - _group_mean(dvh, avg) - vhat * _group_mean(dvh * vhat, avg))
            dpj[rows, D_A:2 * D_A] = dgv_in * dgv

        o = 2 * D_A
        live = (ti > 0).astype(F32)
        zb = proj_ref[:, o:o + D_B]
        gb = proj_ref[:, o + D_B:o + 2 * D_B]
        gc = proj_ref[:, o + 2 * D_B:o + 3 * D_B]
        hcur = gc * zb
        hbuf[0:HALO, :] = prev_ref[:, o + 2 * D_B:o + 3 * D_B] * prev_ref[:, o:o + D_B] * live
        hbuf[HALO:HALO + tm, :] = hcur
        hm1 = hbuf[pl.ds(HALO - 1, tm), :]
        hm2 = hbuf[pl.ds(HALO - 2, tm), :]
        y = cw_ref[2:3, :] * hcur + cw_ref[1:2, :] * hm1 + cw_ref[0:1, :] * hm2
        dout = dyc[:, D_A:D_A + D_B]
        dpj[:, o + D_B:o + 2 * D_B] = dout * y
        dy = dout * gb
        dcw_ref[2:3, :] += jnp.sum(dy * hcur, axis=0, keepdims=True)
        dcw_ref[1:2, :] += jnp.sum(dy * hm1, axis=0, keepdims=True)
        dcw_ref[0:1, :] += jnp.sum(dy * hm2, axis=0, keepdims=True)
        dybuf[0:tm, :] = dy
        dh = (cw_ref[2:3, :] * dy + cw_ref[1:2, :] * dybuf[pl.ds(1, tm), :] + cw_ref[0:1, :] * dybuf[pl.ds(2, tm), :])
        dybuf[tm:tm + HALO, :] = dybuf[0:HALO, :]
        dpj[:, o:o + D_B] = dh * gc
        dpj[:, o + 2 * D_B:o + 3 * D_B] = dh * zb

        zc = proj_ref[:, o + 3 * D_B:D_IN]
        zbuf[0:HALO, :] = prev_ref[:, o + 3 * D_B:D_IN] * live
        zbuf[HALO:HALO + tm, :] = zc
        mean, inv = _pool_means(zbuf[...], tm, ti * tm)
        pooled = (mean - zc).astype(BF16)
        dyp = dyc[:, D_A + D_B:D_MODEL]
        ps = ps_ref[...]
        dps_ref[...] += jnp.sum(dyp * _dot(pooled, pw_ref[...]), axis=0, keepdims=True)
        dpw = (dyp * ps).astype(BF16)
        dpw_ref[...] += _dot_tn(pooled, dpw)
        dpooled = _dot(dpw, pwt_ref[...])
        qbuf[0:tm, :] = dpooled * inv
        q = qbuf[...]
        nrows = tm + HALO
        f2 = q + pltpu.roll(q, nrows - 1, 0)
        f4 = f2 + pltpu.roll(f2, nrows - 2, 0)
        f8 = f4 + pltpu.roll(f4, nrows - 4, 0)
        f16 = f8 + pltpu.roll(f8, nrows - 8, 0)
        lane = lax.broadcasted_iota(jnp.int32, (tm, D_C), 1)
        ahead = jnp.where(lane < 64, f2[0:tm], jnp.where(lane < 128, f4[0:tm], jnp.where(lane < 192, f8[0:tm], f16[0:tm])))
        dpj[:, o + 3 * D_B:D_IN] = ahead - dpooled
        qbuf[tm:tm + HALO, :] = qbuf[0:HALO, :]

        dprojb = dpj[...].astype(BF16)
        dproj_ref[...] = dprojb
        dh1 = _dot_nt(dprojb, win_ref[...])
        xv = x_ref[...]
        rs = lax.rsqrt(jnp.mean(xv * xv, axis=-1, keepdims=True) + RMS_EPS)
        n1 = xv * rs
        gv1 = g_ref[...]
        h1_ref[...] = (n1 * gv1).astype(BF16)
        dg_ref[...] += jnp.sum(dh1 * n1, axis=0, keepdims=True)
        dx_ref[...] = dx1v + _rms_bwd(dh1, n1, rs, gv1)

        @pl.when(i == nt - 1)
        def _():
            tril = (lax.broadcasted_iota(jnp.int32, (2 * CHUNK, CHUNK), 0) % CHUNK
                    >= lax.broadcasted_iota(jnp.int32, (2 * CHUNK, CHUNK), 1))
            for j in range(3):
                dws_ref[j] = jnp.where(tril, dws_ref[j], 0.0)
            dbm_ref[...] = _group_mean_split(dbm_ref[...], avg) * float(HEAD_DIM)

    rev = lambda w: pl.BlockSpec((tm, w), lambda i: (nt - 1 - i, 0))
    prev = pl.BlockSpec((HALO, D_IN), lambda i: (jnp.maximum((nt - 1 - i) * prev_blocks - 1, 0), 0))
    acc_shapes = [(1, D_MODEL), (3, 2 * CHUNK, CHUNK), (CHUNK, D_A), (1, D_A), (1, D_A), (8, D_B), (D_C, D_C), (1, D_C)]
    return _tile_call(
        body, "mix_bwd", nt,
        [rev(D_MODEL), rev(D_MODEL), rev(D_IN), prev, _const_spec((1, D_MODEL)), _const_spec((D_MODEL, D_IN)),
         _const_spec((D_MODEL, D_MODEL)), _const_spec((3, CHUNK, 2 * CHUNK)), _const_spec((3, 2 * CHUNK, CHUNK)),
         _const_spec((CHUNK, D_A)), _const_spec((1, D_A)), _const_spec((1, D_A)), _const_spec((D_A, D_A)),
         _const_spec((8, D_B)), _const_spec((D_C, D_C)), _const_spec((D_C, D_C)), _const_spec((1, D_C))],
        [rev(D_MODEL), rev(D_MODEL), rev(D_IN)] + [_acc_spec(s) for s in acc_shapes],
        [jax.ShapeDtypeStruct((t, D_MODEL), F32), jax.ShapeDtypeStruct((t, D_MODEL), BF16),
         jax.ShapeDtypeStruct((t, D_IN), BF16)] + [jax.ShapeDtypeStruct(s, F32) for s in acc_shapes],
        [pltpu.VMEM((tm, D_MODEL), F32), pltpu.VMEM((tm, D_IN), F32),
         pltpu.VMEM((tm + HALO, D_B), F32), pltpu.VMEM((tm + HALO, D_C), F32),
         pltpu.VMEM((tm + HALO, D_B), F32), pltpu.VMEM((tm + HALO, D_C), F32)],
        (dx1, x, proj, proj, g_mix, w_in, w_out, wcat, wcat_t, bmat, ln_g, ln_b, avg, conv_w, pool_bd, pool_bd_t,
         pool_scale), comm)


def wgrad(a, b, *, tk, a_layer=None, relu_sq=False):
    t, m = a.shape[-2:]
    n = b.shape[1]
    bm = min(m, 1024)
    bn = 1024 if n % 1024 == 0 else n
    nk = t // tk
    if a_layer is None:
        a_spec = pl.BlockSpec((tk, bm), lambda i, j, k: (k, i))
    else:
        a_spec = pl.BlockSpec((None, None, tk, bm), lambda i, j, k: (a_layer, 0, k, i))

    def body(a_ref, b_ref, o_ref):
        k = pl.program_id(2)

        @pl.when(k == 0)
        def _():
            o_ref[...] = jnp.zeros_like(o_ref)

        av = a_ref[...]
        if relu_sq:
            ra = jnp.maximum(av.astype(F32), 0.0)
            av = ra * ra
        o_ref[...] += _dot_tn(av.astype(BF16), b_ref[...].astype(BF16))

    return pl.pallas_call(
        body, name=f"wgrad_{m}x{n}" + ("_relu_sq" if relu_sq else ""), grid=(m // bm, n // bn, nk),
        in_specs=[a_spec, pl.BlockSpec((tk, bn), lambda i, j, k: (k, j))],
        out_specs=pl.BlockSpec((bm, bn), lambda i, j, k: (i, j)),
        out_shape=jax.ShapeDtypeStruct((m, n), F32),
        compiler_params=_params("parallel", "parallel", "arbitrary"),
    )(a, b)


def _row_block(rows, cols, target_bytes):
    target = max(8, target_bytes // (4 * cols))
    if rows <= target:
        return rows
    best = None
    for br in range(8, target + 1, 8):
        if rows % br == 0:
            best = br
    return best if best is not None else rows


def adamw(w, g, m, v):
    shape = w.shape
    cols = shape[-1]
    rows = math.prod(shape[:-1]) if len(shape) > 1 else 1
    br = _row_block(rows, cols, 1 << 20)

    def body(w_ref, g_ref, m_ref, v_ref, d_ref, nm_ref, nv_ref):
        gv = g_ref[...]
        nm = ADAM_B1 * m_ref[...] + (1.0 - ADAM_B1) * gv
        nv = ADAM_B2 * v_ref[...] + (1.0 - ADAM_B2) * jnp.square(gv)
        m_hat = nm / (1.0 - ADAM_B1 ** ADAM_STEP)
        v_hat = nv / (1.0 - ADAM_B2 ** ADAM_STEP)
        d_ref[...] = -ADAM_LR * (m_hat / (jnp.sqrt(v_hat) + ADAM_EPS) + ADAM_WD * w_ref[...])
        nm_ref[...] = nm
        nv_ref[...] = nv

    spec = pl.BlockSpec((br, cols), lambda i: (i, 0))
    outs = pl.pallas_call(
        body, name="adamw", grid=(rows // br,),
        in_specs=[spec] * 4, out_specs=[spec] * 3,
        out_shape=[jax.ShapeDtypeStruct((rows, cols), F32)] * 3,
        compiler_params=pltpu.CompilerParams(dimension_semantics=("parallel",)),
    )(*(a.reshape(rows, cols) for a in (w, g, m, v)))
    return tuple(o.reshape(shape) for o in outs)


ADD_STEPS = 4


def add_halves(geoms, arrs, received, c_idx, dtypes):
    n = len(arrs)

    def body(c_ref, *refs):
        del c_ref
        for a in range(n):
            refs[2 * n + a][...] = (refs[a][...] + refs[n + a][...]).astype(dtypes[a])

    own_specs, half_specs = [], []
    for g in geoms:
        if g.kind == "cols":
            rows, cols = g.half_shape[0] // ADD_STEPS, g.half_shape[1]
            own_specs.append(pl.BlockSpec((rows, cols), lambda i, c_ref: (ADD_STEPS * c_ref[0] + i, 0)))
            half_specs.append(pl.BlockSpec((rows, cols), lambda i, c_ref: (i, 0)))
        else:
            _, h, cols = g.half_shape
            own_specs.append(pl.BlockSpec((None, None, h, cols), lambda i, c_ref: (i, c_ref[0], 0, 0)))
            half_specs.append(pl.BlockSpec((None, h, cols), lambda i, c_ref: (i, 0, 0)))
    return pl.pallas_call(
        body, name="add_halves",
        grid_spec=pltpu.PrefetchScalarGridSpec(num_scalar_prefetch=1, grid=(ADD_STEPS,),
                                               in_specs=own_specs + half_specs, out_specs=half_specs),
        out_shape=[jax.ShapeDtypeStruct(g.half_shape, dt) for g, dt in zip(geoms, dtypes)],
        compiler_params=_params("parallel"),
    )(c_idx, *arrs, *received)


def add_parts(geoms, landed, finals, layer, c_idx):
    n = len(landed)

    def body(c_ref, *refs):
        del c_ref
        for a in range(n):
            p_ref = refs[a]
            parts = [p_ref[j].astype(F32) for j in range(N_CHIPS)]
            refs[2 * n + a][...] = ((parts[0] + parts[1]) + parts[2]) + parts[3]

    in_specs, out_specs = [], []
    for g in geoms:
        rows, cols = g.part_shape[0] // ADD_STEPS, g.part_shape[1]
        in_specs.append(pl.BlockSpec((N_CHIPS, rows, cols), lambda i, c_ref: (0, i, 0)))
        out_specs.append(pl.BlockSpec((None, rows, cols), lambda i, c_ref: (layer, ADD_STEPS * c_ref[0] + i, 0)))
    return pl.pallas_call(
        body, name="add_parts",
        grid_spec=pltpu.PrefetchScalarGridSpec(num_scalar_prefetch=1, grid=(ADD_STEPS,),
                                               in_specs=in_specs + [_ANY] * n, out_specs=out_specs),
        out_shape=[jax.ShapeDtypeStruct(f.shape, F32) for f in finals],
        input_output_aliases={1 + n + a: a for a in range(n)},
        compiler_params=_params("parallel"),
    )(c_idx, *landed, *finals)


def _shard_dims(k, n, axis):
    return (k // N_CHIPS, n) if axis == 0 else (k, n // N_CHIPS)


W_IN_STRIDE = 512
W_IN_WINDOW = 640


def _big_geoms():
    geoms = []
    for name, k, n, axis in BIG:
        if axis == 0:
            geoms.append(_Geom("rows", (N_CHIPS, 2, k // N_CHIPS // 2, n)))
        elif name == "w_in":
            geoms.append(_Geom("cols", (k, n), W_IN_STRIDE, W_IN_WINDOW))
        else:
            geoms.append(_Geom("cols", (k, n), n // N_CHIPS, n // N_CHIPS))
    return geoms


def _grad_views(gb, geoms):
    return [gb[name].reshape(g.shape) for (name, _, _, _), g in zip(BIG, geoms)]


def _round_up(v, m):
    return (v + m - 1) // m * m


def _prep_small(small):
    tril = jnp.tril(jnp.ones((CHUNK, CHUNK), bool))
    wm = jnp.where(tril, small["sgu_w"], 0.0).astype(BF16).reshape(DEPTH, 3, 2, CHUNK, CHUNK)
    head = jnp.arange(D_A) // HEAD_DIM
    grp = jnp.arange(D_C) // HEAD_DIM
    pw_rows = small["pool_w"].reshape(DEPTH, D_C, HEAD_DIM)
    pool_bd = jnp.where((grp[:, None] == grp[None, :])[None], jnp.tile(pw_rows, (1, 1, D_C // HEAD_DIM)), 0.0).astype(BF16)
    return dict(
        wcat=wm.transpose(0, 1, 3, 2, 4).reshape(DEPTH, 3, CHUNK, 2 * CHUNK),
        wcat_t=wm.transpose(0, 1, 2, 4, 3).reshape(DEPTH, 3, 2 * CHUNK, CHUNK),
        bmat=jnp.repeat(jnp.swapaxes(small["sgu_b"], 1, 2), HEAD_DIM, axis=2),
        avg=jnp.where(head[:, None] == head[None, :], 1.0 / HEAD_DIM, 0.0).astype(BF16),
        pool_bd=pool_bd, pool_bd_t=jnp.swapaxes(pool_bd, 1, 2),
        conv8=jnp.pad(small["conv_w"], ((0, 0), (0, 8 - 3), (0, 0))),
    )


def _row(a):
    return a.reshape(1, -1)


MIX_WEIGHTS = ("w_in", "w_out")
MLP_WEIGHTS = ("w_ff1", "w_ff2", "w_ple_gate", "w_ple_proj")
ALL_BIG = MIX_WEIGHTS + MLP_WEIGHTS
EARLY_GRADS = MLP_WEIGHTS + ("w_out",)
FFN_BWD_TILE = 512


def _fwd_layer(h, p, wl, small, prep, l, tm, comm_mix=None, comm_mlp=None):
    (proj, ycat, x1), got = mix_fwd(h, _row(small["norm_mix_g"][l]), wl["w_in"], wl["w_out"], prep["wcat"][l],
                                    prep["bmat"][l], _row(small["sgu_ln_g"][l]), _row(small["sgu_ln_b"][l]), prep["avg"],
                                    prep["conv8"][l], prep["pool_bd"][l], _row(small["pool_scale"][l]),
                                    tm=min(2 * tm, h.shape[0]), comm=comm_mix)
    if comm_mix is not None:
        wl = {**wl, **_weights_of(got, MLP_WEIGHTS)}
    (a, x2, x3), couts = ffn_ple_fwd(x1, (p, l), _row(small["norm_ff_g"][l]), wl["w_ff1"], wl["w_ff2"],
                                     _row(small["norm_ple_g"][l]), wl["w_ple_gate"], wl["w_ple_proj"], tm=tm,
                                     comm=comm_mlp)
    return (h, proj, ycat, x1, a, x2), x3, couts, wl


def _merge_comms(comms):
    comms = [cm for cm in comms if cm is not None]
    if len(comms) <= 1:
        return comms[0] if comms else None
    spans, ni, no, ns = [], 0, 0, 0
    for cm in comms:
        spans.append((ni, no, ns))
        ni, no, ns = ni + len(cm.ins), no + len(cm.out_shapes), ns + len(cm.sems)

    def copies(in_refs, out_refs, sem_refs):
        local, sends, recvs = [], [], []
        for cm, (i0, o0, s0) in zip(comms, spans):
            got = cm.copies(in_refs[i0:i0 + len(cm.ins)], out_refs[o0:o0 + len(cm.out_shapes)],
                            sem_refs[s0:s0 + len(cm.sems)])
            local, sends, recvs = local + got[0], sends + got[1], recvs + got[2]
        return local, sends, recvs

    aliases = {i0 + i: o0 + o for cm, (i0, o0, _) in zip(comms, spans) for i, o in cm.aliases.items()}
    return _Comm(sum((cm.ins for cm in comms), []), sum((cm.out_shapes for cm in comms), []),
                 sum((cm.sems for cm in comms), []), copies, aliases)


def _split_results(results, comms):
    out, at = [], 0
    for cm in comms:
        if cm is None:
            out.append(None)
        else:
            out.append(results[at:at + len(cm.out_shapes)])
            at += len(cm.out_shapes)
    return out


class _Reduction:
    def __init__(self, layer, names, geoms, arrs, finals, c_arr, narrow=()):
        self.layer, self.names, self.geoms, self.arrs = layer, list(names), list(geoms), list(arrs)
        self.finals, self.c_arr = finals, c_arr
        self.dtypes = [BF16 if n in narrow else F32 for n in self.names]

    def comm_a(self):
        return _reduce_a_comm(self.geoms, self.arrs)

    def comm_b(self, received):
        return _reduce_b_comm(self.geoms, add_halves(self.geoms, self.arrs, received, self.c_arr, self.dtypes))

    def comm_c(self, landed):
        mine = add_parts(self.geoms, landed, [self.finals[n] for n in self.names], self.layer, self.c_arr)
        return _reduce_c_comm(self.geoms, mine, self.layer)

    def done(self, results):
        self.finals.update(zip(self.names, results))


class _Plan:
    def ple(self):
        return None

    def after_ple(self, results):
        pass

    def ffn(self):
        return None

    def after_ffn(self, results):
        pass

    def before_mix(self, gb):
        pass

    def mix(self):
        return None

    def after_mix(self, results):
        pass


class _CarryPlan(_Plan):
    def __init__(self, above):
        self.above = above

    def ple(self):
        return self.above.comm_a()

    def after_ple(self, results):
        self.received = results

    def ffn(self):
        return self.above.comm_b(self.received)

    def after_ffn(self, results):
        self.landed = results

    def mix(self):
        return self.above.comm_c(self.landed)

    def after_mix(self, results):
        self.above.done(results)


class _LastPlan(_CarryPlan):
    def __init__(self, above, make_early):
        super().__init__(above)
        self.make_early = make_early

    def before_mix(self, gb):
        self.early = self.make_early(gb)
        self.early_received = _run_comm(self.early.comm_a(), "reduce_a_early")

    def mix(self):
        self.parts = [self.above.comm_c(self.landed), self.early.comm_b(self.early_received)]
        return _merge_comms(self.parts)

    def after_mix(self, results):
        above_res, self.early_landed = _split_results(results, self.parts)
        self.above.done(above_res)


def _bwd_layer(d, saved, p, wl, small, prep, l, tm, tk, plan=None):
    plan = plan or _Plan()
    xin, proj, ycat, x1, a, x2 = saved
    (dx2, h3, dpre, dpp, dg_ple), res = ple_bwd(d, x2, (p, l), _row(small["norm_ple_g"][l]), wl["w_ple_gate"],
                                                wl["w_ple_proj"], tm=tm, comm=plan.ple())
    plan.after_ple(res)
    gb = {"w_ple_gate": wgrad(h3, dpre, tk=tk), "w_ple_proj": wgrad(p, dpp, tk=tk, a_layer=l)}
    (dx1, h2, da, dg_ff), res = ffn_bwd(dx2, x1, a, _row(small["norm_ff_g"][l]), wl["w_ff1"], wl["w_ff2"],
                                        tm=FFN_BWD_TILE if tm >= FFN_BWD_TILE else tm, comm=plan.ffn())
    plan.after_ffn(res)
    gb["w_ff2"] = wgrad(a, dx2, tk=tk, relu_sq=True)
    gb["w_ff1"] = wgrad(h2, da, tk=tk)
    gb["w_out"] = wgrad(ycat, dx1, tk=tk)
    plan.before_mix(gb)
    (dprev, h1, dproj, dg_mix, dws, dbm, dlng, dlnb, dcw, dpw, dps), res = mix_bwd(
        dx1, xin, proj, _row(small["norm_mix_g"][l]), wl["w_in"], wl["w_out"], prep["wcat"][l], prep["wcat_t"][l],
        prep["bmat"][l], _row(small["sgu_ln_g"][l]), _row(small["sgu_ln_b"][l]), prep["avg"], prep["conv8"][l],
        prep["pool_bd"][l], prep["pool_bd_t"][l], _row(small["pool_scale"][l]), tm=tm, comm=plan.mix())
    plan.after_mix(res)
    gb["w_in"] = wgrad(h1, dproj, tk=tk)
    gs = {
        "norm_ple_g": dg_ple[0], "norm_ff_g": dg_ff[0], "norm_mix_g": dg_mix[0],
        "sgu_w": dws.reshape(2 * 3, CHUNK, CHUNK), "sgu_b": dbm[:, ::HEAD_DIM].T,
        "sgu_ln_g": dlng[0], "sgu_ln_b": dlnb[0], "conv_w": dcw[0:3], "pool_scale": dps[0],
        "pool_w": jnp.stack([dpw[g * HEAD_DIM:(g + 1) * HEAD_DIM, g * HEAD_DIM:(g + 1) * HEAD_DIM]
                             for g in range(D_C // HEAD_DIM)]),
    }
    return dprev, gb, gs


def _local_step(x, p, target, full, small, *, tm, tk):
    prep = _prep_small(small)
    p = p[:, None]
    saved, h = [], x
    for l in range(DEPTH):
        wl = {name: full[name][l] for name in full}
        s, h, _, _ = _fwd_layer(h, p, wl, small, prep, l, tm)
        saved.append(s)
    loss_blk, d_final_g, d = loss_head(h, target, _row(small["final_g"]), tm=tm)
    gbig, gsm = [None] * DEPTH, [None] * DEPTH
    for l in reversed(range(DEPTH)):
        wl = {name: full[name][l] for name in full}
        d, gbig[l], gsm[l] = _bwd_layer(d, saved[l], p, wl, small, prep, l, tm, tk)
    big = {name: jnp.stack([gbig[l][name] for l in range(DEPTH)]) for name in gbig[0]}
    sm = {name: jnp.stack([gsm[l][name] for l in range(DEPTH)]) for name in gsm[0]}
    sm["final_g"] = d_final_g[0]
    return loss_blk[0, 0], d, big, sm


def _weights_of(gathered, names):
    wl = dict(zip([b[0] for b in BIG if b[0] in names], gathered))
    if "w_in" in wl:
        wl["w_in"] = wl["w_in"].transpose(1, 0, 2).reshape(D_MODEL, D_IN)
    return wl


def kernel(x, p, norm_mix_g, w_in, sgu_w, sgu_b, sgu_ln_g, sgu_ln_b, conv_w, pool_w, pool_scale, w_out, norm_ff_g, w_ff1, w_ff2, norm_ple_g, w_ple_gate, w_ple_proj, final_g, loss_target, m_norm_mix_g, m_w_in, m_sgu_w, m_sgu_b, m_sgu_ln_g, m_sgu_ln_b, m_conv_w, m_pool_w, m_pool_scale, m_w_out, m_norm_ff_g, m_w_ff1, m_w_ff2, m_norm_ple_g, m_w_ple_gate, m_w_ple_proj, m_final_g, v_norm_mix_g, v_w_in, v_sgu_w, v_sgu_b, v_sgu_ln_g, v_sgu_ln_b, v_conv_w, v_pool_w, v_pool_scale, v_w_out, v_norm_ff_g, v_w_ff1, v_w_ff2, v_norm_ple_g, v_w_ple_gate, v_w_ple_proj, v_final_g):
    args = dict(locals())
    w = {name: args[name] for name in WEIGHTS}
    m = {name: args["m_" + name] for name in WEIGHTS}
    v = {name: args["v_" + name] for name in WEIGHTS}
    t = x.shape[1]
    tm = min(512, t)
    tk = min(2048, t)
    x_idx, y_idx, c_idx = _place()
    chip = 2 * x_idx + y_idx
    c_arr = c_idx.reshape(1).astype(jnp.int32)
    xs, target = x[0], loss_target[0]

    shards = {name: w[name].astype(BF16) for name, _, _, _ in BIG}
    conv_rows = _round_up(CONV_SHARD, 8 * 128) // 128
    conv_flat = jnp.pad(w["conv_w"].reshape(-1), (0, conv_rows * 128 - CONV_SHARD)).reshape(conv_rows, 128)
    first = _run_comm(_gather_comm(shards, 0, MIX_WEIGHTS, conv_flat), "gather_first")
    conv_full = (first[len(MIX_WEIGHTS)].reshape(N_CHIPS, -1)[:, :CONV_SHARD]
                 .reshape(N_CHIPS, DEPTH, 3, D_B // N_CHIPS).transpose(1, 2, 0, 3).reshape(DEPTH, 3, D_B))
    small = {name: w[name] for name in SMALL}
    small["conv_w"] = conv_full
    prep = _prep_small(small)

    wl = [None] * DEPTH
    wl[0] = _weights_of(first, MIX_WEIGHTS)
    saved, h = [], xs
    for l in range(DEPTH):
        comm_mix = _gather_comm(shards, 0, MLP_WEIGHTS) if l == 0 else None
        comm_mlp = _gather_comm(shards, l + 1, ALL_BIG) if l + 1 < DEPTH else None
        s, h, got, wl[l] = _fwd_layer(h, p, wl[l], small, prep, l, tm, comm_mix, comm_mlp)
        saved.append(s)
        if comm_mlp is not None:
            wl[l + 1] = _weights_of(got, ALL_BIG)

    loss_blk, d_final_g, d = loss_head(h, target, _row(small["final_g"]), tm=tm)

    geoms = dict(zip([b[0] for b in BIG], _big_geoms()))
    finals = {name: jnp.zeros((DEPTH,) + g.final_shape, F32) for name, g in geoms.items()}

    def reduction(layer, names, gb, narrow=()):
        return _Reduction(layer, names, [geoms[n] for n in names], [gb[n].reshape(geoms[n].shape) for n in names],
                          finals, c_arr, narrow)

    gsm = [None] * DEPTH
    above = None
    for l in reversed(range(DEPTH)):
        if above is None:
            plan = _Plan()
        elif l > 0:
            plan = _CarryPlan(above)
        else:
            plan = _LastPlan(above, lambda gb: reduction(0, EARLY_GRADS, gb))
        d, gb, gsm[l] = _bwd_layer(d, saved[l], p, wl[l], small, prep, l, tm, tk, plan)
        if l > 0:
            above = reduction(l, ALL_BIG, gb)

    sm = {name: jnp.stack([gsm[i][name] for i in range(DEPTH)]) for name in gsm[0]}
    sm["final_g"] = d_final_g[0]
    sizes = [sm[name].size for name in SMALL]
    small_rows = _round_up(-(-sum(sizes) // (2 * N_CHIPS * LANES)), 8 * ADD_STEPS)
    small_flat = jnp.pad(jnp.concatenate([sm[name].reshape(-1) for name in SMALL]),
                         (0, 2 * N_CHIPS * small_rows * LANES - sum(sizes)))
    geoms["small"] = _Geom("rows", (N_CHIPS, 2, small_rows, LANES))
    finals["small"] = jnp.zeros((1,) + geoms["small"].final_shape, F32)
    late = reduction(0, ("w_in", "small"), {**gb, "small": small_flat}, narrow=("w_in",))
    late_landed = _run_comm(late.comm_b(_run_comm(late.comm_a(), "reduce_a_late")), "reduce_b_late")
    last = [plan.early.comm_c(plan.early_landed), late.comm_c(late_landed)]
    early_res, late_res = _split_results(_run_comm(_merge_comms(last), "reduce_c_last"), last)
    plan.early.done(early_res)
    late.done(late_res)

    grads = {name: finals[name] for name, _, _, _ in BIG}
    grads["w_in"] = lax.dynamic_slice_in_dim(grads["w_in"], chip * (D_IN // N_CHIPS - W_IN_STRIDE), D_IN // N_CHIPS, axis=2)
    small_red = _run_comm(_allgather_comm(finals["small"][0]), "small_allgather")[0].reshape(-1)
    off = 0
    for name, size in zip(SMALL, sizes):
        grads[name] = small_red[off:off + size].reshape(sm[name].shape)
        off += size
    grads["conv_w"] = lax.dynamic_slice_in_dim(grads["conv_w"], chip * (D_B // N_CHIPS), D_B // N_CHIPS, axis=2)

    loss = lax.psum(loss_blk[0, 0], ("x", "y", "c"))
    delta, new_m, new_v = {}, {}, {}
    for name in WEIGHTS:
        delta[name], new_m[name], new_v[name] = adamw(w[name], grads[name], m[name], v[name])
    return (loss, d[None], *[grads[n] for n in WEIGHTS], *[delta[n] for n in WEIGHTS],
            *[new_m[n] for n in WEIGHTS], *[new_v[n] for n in WEIGHTS])
```

```python
import math

import jax
import jax.numpy as jnp
from jax import lax
from jax.experimental import pallas as pl
from jax.experimental.pallas import tpu as pltpu

F32 = jnp.float32
BF16 = jnp.bfloat16

D_MODEL = 1024
DEPTH = 4
D_PLE = 256
D_FF = 4096
HEAD_DIM = 64
D_A = 384
D_B = 384
D_C = 256
D_IN = 2176
CHUNK = 128
HALO = 16
RMS_EPS = 1e-6
LN_EPS = 1e-5
N_CHIPS = 4
LANES = 1024

ADAM_LR = 0.001
ADAM_B1 = 0.9
ADAM_B2 = 0.999
ADAM_EPS = 1e-08
ADAM_WD = 0.01
ADAM_STEP = 10

VMEM_LIMIT_BYTES = 60 * 1024 * 1024

_RSQRT2 = 0.7071067811865476
_INV_SQRT_2PI = 0.3989422804014327

BIG = (
    ("w_in", D_MODEL, D_IN, 1),
    ("w_out", D_MODEL, D_MODEL, 0),
    ("w_ff1", D_MODEL, D_FF, 1),
    ("w_ff2", D_FF, D_MODEL, 0),
    ("w_ple_gate", D_MODEL, D_MODEL, 0),
    ("w_ple_proj", D_PLE, D_MODEL, 1),
)
SMALL = ("norm_mix_g", "sgu_w", "sgu_b", "sgu_ln_g", "sgu_ln_b", "conv_w", "pool_w", "pool_scale",
         "norm_ff_g", "norm_ple_g", "final_g")
WEIGHTS = ("norm_mix_g", "w_in", "sgu_w", "sgu_b", "sgu_ln_g", "sgu_ln_b", "conv_w", "pool_w", "pool_scale",
           "w_out", "norm_ff_g", "w_ff1", "w_ff2", "norm_ple_g", "w_ple_gate", "w_ple_proj", "final_g")
CONV_SHARD = DEPTH * 3 * (D_B // N_CHIPS)


def _dot(a, b):
    return jnp.dot(a, b, preferred_element_type=F32)


def _dot_nt(a, b):
    return lax.dot_general(a, b, (((1,), (1,)), ((), ())), preferred_element_type=F32)


def _dot_tn(a, b):
    return lax.dot_general(a, b, (((0,), (0,)), ((), ())), preferred_element_type=F32)


def _const_spec(shape):
    nd = len(shape)
    return pl.BlockSpec(shape, lambda i: (0,) * nd, pipeline_mode=pl.Buffered(1))


def _acc_spec(shape):
    nd = len(shape)
    return pl.BlockSpec(shape, lambda i: (0,) * nd)


def _layer_rows(layer, tm):
    return pl.BlockSpec((None, None, tm, D_PLE), lambda i: (layer, 0, i, 0))


def _params(*sem):
    return pltpu.CompilerParams(dimension_semantics=sem, vmem_limit_bytes=VMEM_LIMIT_BYTES)


def _rms_bwd(dh, n, rs, g):
    dn = dh * g
    return rs * (dn - n * jnp.mean(dn * n, axis=-1, keepdims=True))


def _gelu(x):
    return x * (0.5 * (1.0 + lax.erf(x * _RSQRT2)))


def _gelu_and_grad(x):
    cdf = 0.5 * (1.0 + lax.erf(x * _RSQRT2))
    return x * cdf, cdf + x * (jnp.exp(-0.5 * x * x) * _INV_SQRT_2PI)


def _group_mean(v, avg):
    vb = v.astype(BF16)
    split = 2 * CHUNK
    return jnp.concatenate([_dot(vb[:, :split], avg[:split, :split]), _dot(vb[:, split:], avg[split:, split:])], axis=1)


def _group_mean_split(v, avg):
    hi = v.astype(BF16)
    lo = (v - hi.astype(F32)).astype(BF16)
    return _dot(hi, avg) + _dot(lo, avg)


def _lane_lt(shape, bound):
    return lax.broadcasted_iota(jnp.int32, shape, 1) < bound


def _sgu_mix(vnb2, wcat_j, lo_mask):
    zero = jnp.zeros_like(vnb2)
    stacked = jnp.concatenate([jnp.where(lo_mask, vnb2, zero), jnp.where(lo_mask, zero, vnb2)], axis=0)
    return _dot(wcat_j, stacked)


def _pool_means(ext, tile_rows, first_pos):
    s2 = ext + pltpu.roll(ext, 1, 0)
    s4 = s2 + pltpu.roll(s2, 2, 0)
    s8 = s4 + pltpu.roll(s4, 4, 0)
    s16 = s8 + pltpu.roll(s8, 8, 0)
    pos = (first_pos + lax.broadcasted_iota(jnp.int32, (tile_rows, 1), 0) + 1).astype(F32)
    lane = lax.broadcasted_iota(jnp.int32, (tile_rows, D_C), 1)
    sums = jnp.where(lane < 64, s2[HALO:], jnp.where(lane < 128, s4[HALO:], jnp.where(lane < 192, s8[HALO:], s16[HALO:])))
    win = jnp.where(lane < 64, 2.0, jnp.where(lane < 128, 4.0, jnp.where(lane < 192, 8.0, 16.0)))
    inv = 1.0 / jnp.minimum(pos, win)
    return sums * inv, inv


MESH = pl.DeviceIdType.MESH
_ANY = pl.BlockSpec(memory_space=pl.ANY)


def _place():
    return lax.axis_index("x"), lax.axis_index("y"), lax.axis_index("c")


def _chip_peers(x, y):
    return [(1 - x, y), (x, 1 - y), (1 - x, 1 - y)]


class _Comm:
    def __init__(self, ins, out_shapes, sems, copies, aliases=None):
        self.ins, self.out_shapes, self.sems, self.copies = list(ins), list(out_shapes), list(sems), copies
        self.aliases = dict(aliases or {})

    def start(self, in_refs, out_refs, sem_refs):
        local, sends, _ = self.copies(in_refs, out_refs, sem_refs)
        for cp in local + sends:
            cp.start()

    def wait(self, in_refs, out_refs, sem_refs):
        local, sends, recvs = self.copies(in_refs, out_refs, sem_refs)
        for cp in recvs:
            cp.wait_recv()
        for cp in sends:
            cp.wait_send()
        for cp in local:
            cp.wait()


def _remote(src, dst, send_sem, recv_sem, device):
    return pltpu.make_async_remote_copy(src_ref=src, dst_ref=dst, send_sem=send_sem, recv_sem=recv_sem,
                                        device_id=device, device_id_type=MESH)


def _gather_comm(shards, layer, names, conv=None):
    mats = [b for b in BIG if b[0] in names]
    ins = [shards[name] for name, _, _, _ in mats] + ([conv] if conv is not None else [])
    out_shapes = []
    for name, k, n, axis in mats:
        shape = (N_CHIPS, k, n // N_CHIPS) if name == "w_in" else (k, n)
        out_shapes.append(jax.ShapeDtypeStruct(shape, BF16))
    if conv is not None:
        out_shapes.append(jax.ShapeDtypeStruct((N_CHIPS,) + conv.shape, conv.dtype))
    n_arr = len(ins)

    def block(a, out_ref, chip):
        if a == len(mats) or mats[a][0] == "w_in":
            return out_ref.at[chip]
        _, k, n, axis = mats[a]
        if axis == 0:
            return out_ref.at[pl.ds(chip * (k // N_CHIPS), k // N_CHIPS), :]
        return out_ref.at[:, pl.ds(chip * (n // N_CHIPS), n // N_CHIPS)]

    def copies(in_refs, out_refs, sem_refs):
        send_sems, recv_sems, local_sems = sem_refs
        x, y, c = _place()
        me = 2 * x + y
        local, sends, recvs = [], [], []
        for a in range(n_arr):
            src = in_refs[a].at[layer] if a < len(mats) else in_refs[a]
            local.append(pltpu.make_async_copy(src, block(a, out_refs[a], me), local_sems.at[a]))
            for j, (px, py) in enumerate(_chip_peers(x, y)):
                sends.append(_remote(src, block(a, out_refs[a], me), send_sems.at[a, j], recv_sems.at[a, j], (px, py, c)))
                recvs.append(_remote(src, block(a, out_refs[a], 2 * px + py), send_sems.at[a, j], recv_sems.at[a, j],
                                     (px, py, c)))
        return local, sends, recvs

    sems = [pltpu.SemaphoreType.DMA((n_arr, 3)), pltpu.SemaphoreType.DMA((n_arr, 3)), pltpu.SemaphoreType.DMA((n_arr,))]
    return _Comm(ins, out_shapes, sems, copies)


def _allgather_comm(a):
    def copies(in_refs, out_refs, sem_refs):
        send_sems, recv_sems, local_sem = sem_refs
        x, y, c = _place()
        me = 2 * x + y
        local = [pltpu.make_async_copy(in_refs[0], out_refs[0].at[me], local_sem)]
        sends, recvs = [], []
        for j, (px, py) in enumerate(_chip_peers(x, y)):
            sends.append(_remote(in_refs[0], out_refs[0].at[me], send_sems.at[j], recv_sems.at[j], (px, py, c)))
            recvs.append(_remote(in_refs[0], out_refs[0].at[2 * px + py], send_sems.at[j], recv_sems.at[j], (px, py, c)))
        return local, sends, recvs

    sems = [pltpu.SemaphoreType.DMA((3,)), pltpu.SemaphoreType.DMA((3,)), pltpu.SemaphoreType.DMA]
    return _Comm([a], [jax.ShapeDtypeStruct((N_CHIPS,) + a.shape, a.dtype)], sems, copies)


class _Geom:
    def __init__(self, kind, shape, stride=None, width=None):
        self.kind, self.shape, self.stride, self.width = kind, tuple(shape), stride, width
        if kind == "cols":
            k, n = shape
            self.half_shape, self.part_shape, self.final_shape = (k // 2, n), (k // 2, width), (k, width)
        else:
            _, _, h, n = shape
            self.half_shape, self.part_shape, self.final_shape = (N_CHIPS, h, n), (h, n), (2 * h, n)

    def half(self, ref, core):
        if self.kind == "cols":
            return ref.at[pl.ds(core * self.half_shape[0], self.half_shape[0]), :]
        return ref.at[:, core]

    def part(self, ref, chip):
        if self.kind == "cols":
            return ref.at[:, pl.ds(chip * self.stride, self.width)]
        return ref.at[chip]

    def final_half(self, ref, layer, core):
        rows = self.part_shape[0]
        return ref.at[layer, pl.ds(core * rows, rows), :]


def _reduce_a_comm(geoms, arrs):
    n = len(arrs)

    def copies(in_refs, out_refs, sem_refs):
        x, y, c = _place()
        cps = [_remote(geoms[a].half(in_refs[a], 1 - c), out_refs[a], sem_refs[0].at[a], sem_refs[1].at[a], (x, y, 1 - c))
               for a in range(n)]
        return [], cps, cps

    return _Comm(arrs, [jax.ShapeDtypeStruct(g.half_shape, F32) for g in geoms],
                 [pltpu.SemaphoreType.DMA((n,)), pltpu.SemaphoreType.DMA((n,))], copies)


def _reduce_b_comm(geoms, halves):
    n = len(halves)

    def copies(in_refs, out_refs, sem_refs):
        send_sems, recv_sems, local_sems = sem_refs
        x, y, c = _place()
        me = 2 * x + y
        local, sends, recvs = [], [], []
        for a in range(n):
            g = geoms[a]
            local.append(pltpu.make_async_copy(g.part(in_refs[a], me), out_refs[a].at[me], local_sems.at[a]))
            for j, (px, py) in enumerate(_chip_peers(x, y)):
                peer = 2 * px + py
                sends.append(_remote(g.part(in_refs[a], peer), out_refs[a].at[me], send_sems.at[a, j], recv_sems.at[a, j],
                                     (px, py, c)))
                recvs.append(_remote(g.part(in_refs[a], me), out_refs[a].at[peer], send_sems.at[a, j], recv_sems.at[a, j],
                                     (px, py, c)))
        return local, sends, recvs

    sems = [pltpu.SemaphoreType.DMA((n, 3)), pltpu.SemaphoreType.DMA((n, 3)), pltpu.SemaphoreType.DMA((n,))]
    return _Comm(halves, [jax.ShapeDtypeStruct((N_CHIPS,) + g.part_shape, h.dtype) for g, h in zip(geoms, halves)], sems,
                 copies)


def _reduce_c_comm(geoms, finals, layer):
    n = len(finals)

    def copies(in_refs, out_refs, sem_refs):
        send_sems, recv_sems = sem_refs
        x, y, c = _place()
        sends, recvs = [], []
        for a in range(n):
            g = geoms[a]
            sends.append(_remote(g.final_half(in_refs[a], layer, c), g.final_half(out_refs[a], layer, c), send_sems.at[a],
                                 recv_sems.at[a], (x, y, 1 - c)))
            recvs.append(_remote(g.final_half(in_refs[a], layer, c), g.final_half(out_refs[a], layer, 1 - c),
                                 send_sems.at[a], recv_sems.at[a], (x, y, 1 - c)))
        return [], sends, recvs

    sems = [pltpu.SemaphoreType.DMA((n,)), pltpu.SemaphoreType.DMA((n,))]
    return _Comm(finals, [jax.ShapeDtypeStruct(f.shape, f.dtype) for f in finals], sems, copies,
                 aliases={a: a for a in range(n)})


def _run_comm(comm, name):
    def body(*refs):
        ni, no = len(comm.ins), len(comm.out_shapes)
        in_refs, out_refs, sem_refs = refs[:ni], refs[ni:ni + no], refs[ni + no:]
        comm.start(in_refs, out_refs, sem_refs)
        comm.wait(in_refs, out_refs, sem_refs)

    return pl.pallas_call(
        body, name=name, in_specs=[_ANY] * len(comm.ins), out_specs=[_ANY] * len(comm.out_shapes),
        out_shape=comm.out_shapes, scratch_shapes=comm.sems, input_output_aliases=comm.aliases,
        compiler_params=pltpu.CompilerParams(has_side_effects=True),
    )(*comm.ins)


def _tile_call(body, name, nt, in_specs, out_specs, out_shape, scratch, args, comm):
    if comm is None:
        outs = pl.pallas_call(body, name=name, grid=(nt,), in_specs=in_specs, out_specs=out_specs, out_shape=out_shape,
                              scratch_shapes=scratch, compiler_params=_params("arbitrary"))(*args)
        return outs, []
    n_in, n_out, n_scr = len(in_specs), len(out_specs), len(scratch)
    ci, co = len(comm.ins), len(comm.out_shapes)

    def hosted(*refs):
        in_refs = refs[:n_in]
        cin = refs[n_in:n_in + ci]
        out_refs = refs[n_in + ci:n_in + ci + n_out]
        cout = refs[n_in + ci + n_out:n_in + ci + n_out + co]
        scr = refs[n_in + ci + n_out + co:n_in + ci + n_out + co + n_scr]
        sems = refs[n_in + ci + n_out + co + n_scr:]
        i = pl.program_id(0)

        @pl.when(i == 0)
        def _():
            comm.start(cin, cout, sems)

        body(*in_refs, *out_refs, *scr)

        @pl.when(i == nt - 1)
        def _():
            comm.wait(cin, cout, sems)

    outs = pl.pallas_call(
        hosted, name=name + "_comm", grid=(nt,),
        in_specs=list(in_specs) + [_ANY] * ci, out_specs=list(out_specs) + [_ANY] * co,
        out_shape=list(out_shape) + comm.out_shapes, scratch_shapes=list(scratch) + comm.sems,
        input_output_aliases={n_in + i: n_out + o for i, o in comm.aliases.items()},
        compiler_params=_params("arbitrary"),
    )(*args, *comm.ins)
    return outs[:n_out], outs[n_out:]


def mix_fwd(x, g_mix, w_in, w_out, wcat, bmat, ln_g, ln_b, avg, conv_w, pool_bd, pool_scale, *, tm, comm=None):
    t = x.shape[0]
    nt = t // tm

    def body(x_ref, g_ref, win_ref, wout_ref, wcat_ref, bmat_ref, lng_ref, lnb_ref, avg_ref, cw_ref, pw_ref, ps_ref,
             proj_ref, ycat_ref, x1_ref, hbuf, zbuf):
        i = pl.program_id(0)

        @pl.when(i == 0)
        def _():
            hbuf[0:HALO, :] = jnp.zeros((HALO, D_B), F32)
            zbuf[0:HALO, :] = jnp.zeros((HALO, D_C), F32)

        xv = x_ref[...]
        n = xv * lax.rsqrt(jnp.mean(xv * xv, axis=-1, keepdims=True) + RMS_EPS)
        h1 = (n * g_ref[...]).astype(BF16)
        proj_ref[...] = _dot(h1, win_ref[...])

        lo_mask = _lane_lt((CHUNK, CHUNK), HEAD_DIM)
        avg = avg_ref[...]
        gu = _gelu(proj_ref[:, 0:D_A])
        gv = _gelu(proj_ref[:, D_A:2 * D_A])
        dv = gv - _group_mean(gv, avg)
        var = _group_mean(dv * dv, avg)
        vnb = (dv * lax.rsqrt(var + LN_EPS) * lng_ref[...] + lnb_ref[...]).astype(BF16)
        for c in range(tm // CHUNK):
            rows = slice(c * CHUNK, (c + 1) * CHUNK)
            for j in range(3):
                cols = slice(j * CHUNK, (j + 1) * CHUNK)
                mixed = _sgu_mix(vnb[rows, cols], wcat_ref[j], lo_mask) + bmat_ref[:, cols]
                ycat_ref[rows, cols] = (gu[rows, cols] * mixed).astype(BF16)

        o = 2 * D_A
        hcur = proj_ref[:, o + 2 * D_B:o + 3 * D_B] * proj_ref[:, o:o + D_B]
        hbuf[HALO:HALO + tm, :] = hcur
        y = (cw_ref[2:3, :] * hcur + cw_ref[1:2, :] * hbuf[pl.ds(HALO - 1, tm), :]
             + cw_ref[0:1, :] * hbuf[pl.ds(HALO - 2, tm), :])
        ycat_ref[:, D_A:D_A + D_B] = (proj_ref[:, o + D_B:o + 2 * D_B] * y).astype(BF16)
        hbuf[0:HALO, :] = hbuf[tm:tm + HALO, :]

        zc = proj_ref[:, o + 3 * D_B:D_IN]
        zbuf[HALO:HALO + tm, :] = zc
        mean, _ = _pool_means(zbuf[...], tm, i * tm)
        pooled = (mean - zc).astype(BF16)
        ycat_ref[:, D_A + D_B:D_MODEL] = (_dot(pooled, pw_ref[...]) * ps_ref[...]).astype(BF16)
        zbuf[0:HALO, :] = zbuf[tm:tm + HALO, :]

        x1_ref[...] = xv + _dot(ycat_ref[...], wout_ref[...])

    row = lambda w: pl.BlockSpec((tm, w), lambda i: (i, 0))
    return _tile_call(
        body, "mix_fwd", nt,
        [row(D_MODEL), _const_spec((1, D_MODEL)), _const_spec((D_MODEL, D_IN)), _const_spec((D_MODEL, D_MODEL)),
         _const_spec((3, CHUNK, 2 * CHUNK)), _const_spec((CHUNK, D_A)), _const_spec((1, D_A)), _const_spec((1, D_A)),
         _const_spec((D_A, D_A)), _const_spec((8, D_B)), _const_spec((D_C, D_C)), _const_spec((1, D_C))],
        [row(D_IN), row(D_MODEL), row(D_MODEL)],
        [jax.ShapeDtypeStruct((t, D_IN), F32), jax.ShapeDtypeStruct((t, D_MODEL), BF16),
         jax.ShapeDtypeStruct((t, D_MODEL), F32)],
        [pltpu.VMEM((tm + HALO, D_B), F32), pltpu.VMEM((tm + HALO, D_C), F32)],
        (x, g_mix, w_in, w_out, wcat, bmat, ln_g, ln_b, avg, conv_w, pool_bd, pool_scale), comm)


def ffn_ple_fwd(x1, p, g_ff, w_ff1, w_ff2, g_ple, w_gate, w_proj, *, tm, comm=None):
    t = x1.shape[0]
    nt = t // tm
    nc = D_FF // D_MODEL

    def body(x1_ref, p_ref, gff_ref, w1_ref, w2_ref, gple_ref, wg_ref, wp_ref, a_ref, x2_ref, x3_ref):
        x1v = x1_ref[...]
        n2 = x1v * lax.rsqrt(jnp.mean(x1v * x1v, axis=-1, keepdims=True) + RMS_EPS)
        h2 = (n2 * gff_ref[...]).astype(BF16)
        acc = x1v
        for c in range(nc):
            cols = slice(c * D_MODEL, (c + 1) * D_MODEL)
            a = _dot(h2, w1_ref[:, cols])
            a_ref[:, cols] = a.astype(BF16)
            ra = jnp.maximum(a, 0.0)
            acc = acc + _dot((ra * ra).astype(BF16), w2_ref[cols, :])
        x2_ref[...] = acc
        n3 = acc * lax.rsqrt(jnp.mean(acc * acc, axis=-1, keepdims=True) + RMS_EPS)
        h3 = (n3 * gple_ref[...]).astype(BF16)
        gate = jax.nn.sigmoid(_dot(h3, wg_ref[...]))
        pp = _dot(p_ref[...].astype(BF16), wp_ref[...])
        x3_ref[...] = acc + pp * gate

    row = lambda w: pl.BlockSpec((tm, w), lambda i: (i, 0))
    return _tile_call(
        body, "ffn_ple_fwd", nt,
        [row(D_MODEL), _layer_rows(p[1], tm), _const_spec((1, D_MODEL)), _const_spec((D_MODEL, D_FF)),
         _const_spec((D_FF, D_MODEL)), _const_spec((1, D_MODEL)), _const_spec((D_MODEL, D_MODEL)),
         _const_spec((D_PLE, D_MODEL))],
        [row(D_FF), row(D_MODEL), row(D_MODEL)],
        [jax.ShapeDtypeStruct((t, D_FF), BF16), jax.ShapeDtypeStruct((t, D_MODEL), F32),
         jax.ShapeDtypeStruct((t, D_MODEL), F32)],
        [], (x1, p[0], g_ff, w_ff1, w_ff2, g_ple, w_gate, w_proj), comm)


def loss_head(x, target, g, *, tm):
    t = x.shape[0]
    nt = t // tm

    def body(x_ref, t_ref, g_ref, loss_ref, dg_ref, dx_ref, sq_acc):
        i = pl.program_id(0)

        @pl.when(i == 0)
        def _():
            sq_acc[...] = jnp.zeros_like(sq_acc)
            dg_ref[...] = jnp.zeros_like(dg_ref)

        xv = x_ref[...]
        rs = lax.rsqrt(jnp.mean(xv * xv, axis=-1, keepdims=True) + RMS_EPS)
        n = xv * rs
        gv = g_ref[...]
        err = n * gv - t_ref[...]
        sq_acc[...] += jnp.sum(err * err, axis=0, keepdims=True)
        dy = err * (1.0 / D_MODEL)
        dg_ref[...] += jnp.sum(dy * n, axis=0, keepdims=True)
        dx_ref[...] = _rms_bwd(dy, n, rs, gv)

        @pl.when(i == nt - 1)
        def _():
            total = jnp.sum(sq_acc[...], axis=1, keepdims=True) * (0.5 / D_MODEL)
            loss_ref[...] = jnp.broadcast_to(total, loss_ref.shape)

    row = pl.BlockSpec((tm, D_MODEL), lambda i: (i, 0))
    return pl.pallas_call(
        body, name="loss_head", grid=(nt,),
        in_specs=[row, row, _const_spec((1, D_MODEL))],
        out_specs=[_acc_spec((8, 128)), _acc_spec((1, D_MODEL)), row],
        out_shape=[jax.ShapeDtypeStruct((8, 128), F32), jax.ShapeDtypeStruct((1, D_MODEL), F32),
                   jax.ShapeDtypeStruct((t, D_MODEL), F32)],
        scratch_shapes=[pltpu.VMEM((1, D_MODEL), F32)],
        compiler_params=_params("arbitrary"),
    )(x, target, g)


def ple_bwd(d, x2, p, g_ple, w_gate, w_proj, *, tm, sub=None, comm=None):
    t = d.shape[0]
    nt = t // tm
    sub = min(sub or tm, tm)

    def body(d_ref, x2_ref, p_ref, g_ref, wg_ref, wp_ref, dx2_ref, h3_ref, dpre_ref, dpp_ref, dg_ref):
        i = pl.program_id(0)

        @pl.when(i == 0)
        def _():
            dg_ref[...] = jnp.zeros_like(dg_ref)

        gv = g_ref[...]
        dg = jnp.zeros((1, D_MODEL), F32)
        for s in range(tm // sub):
            rows = pl.ds(s * sub, sub)
            dv = d_ref[rows, :]
            x2v = x2_ref[rows, :]
            rs = lax.rsqrt(jnp.mean(x2v * x2v, axis=-1, keepdims=True) + RMS_EPS)
            n3 = x2v * rs
            h3 = (n3 * gv).astype(BF16)
            h3_ref[rows, :] = h3
            gate = jax.nn.sigmoid(_dot(h3, wg_ref[...]))
            pp = _dot(p_ref[rows, :].astype(BF16), wp_ref[...])
            dpp_ref[rows, :] = (dv * gate).astype(BF16)
            dpre = (dv * pp * gate * (1.0 - gate)).astype(BF16)
            dpre_ref[rows, :] = dpre
            dh3 = _dot_nt(dpre, wg_ref[...])
            dg = dg + jnp.sum(dh3 * n3, axis=0, keepdims=True)
            dx2_ref[rows, :] = dv + _rms_bwd(dh3, n3, rs, gv)
        dg_ref[...] += dg

    row = lambda w: pl.BlockSpec((tm, w), lambda i: (i, 0))
    return _tile_call(
        body, "ple_bwd", nt,
        [row(D_MODEL), row(D_MODEL), _layer_rows(p[1], tm), _const_spec((1, D_MODEL)), _const_spec((D_MODEL, D_MODEL)),
         _const_spec((D_PLE, D_MODEL))],
        [row(D_MODEL), row(D_MODEL), row(D_MODEL), row(D_MODEL), _acc_spec((1, D_MODEL))],
        [jax.ShapeDtypeStruct((t, D_MODEL), F32), jax.ShapeDtypeStruct((t, D_MODEL), BF16),
         jax.ShapeDtypeStruct((t, D_MODEL), BF16), jax.ShapeDtypeStruct((t, D_MODEL), BF16),
         jax.ShapeDtypeStruct((1, D_MODEL), F32)],
        [], (d, x2, p[0], g_ple, w_gate, w_proj), comm)


def ffn_bwd(dx2, x1, a, g_ff, w_ff1, w_ff2, *, tm, comm=None):
    t = dx2.shape[0]
    nt = t // tm
    nc = D_FF // D_MODEL

    def body(dx2_ref, x1_ref, a_ref, g_ref, w1_ref, w2_ref, dx1_ref, h2_ref, da_ref, dg_ref):
        i = pl.program_id(0)

        @pl.when(i == 0)
        def _():
            dg_ref[...] = jnp.zeros_like(dg_ref)

        dv = dx2_ref[...]
        x1v = x1_ref[...]
        rs = lax.rsqrt(jnp.mean(x1v * x1v, axis=-1, keepdims=True) + RMS_EPS)
        n2 = x1v * rs
        gv = g_ref[...]
        h2_ref[...] = (n2 * gv).astype(BF16)
        dvb = dv.astype(BF16)
        dh2 = jnp.zeros((tm, D_MODEL), F32)
        for c in range(nc):
            cols = slice(c * D_MODEL, (c + 1) * D_MODEL)
            ra = jnp.maximum(a_ref[:, cols].astype(F32), 0.0)
            da = (_dot_nt(dvb, w2_ref[cols, :]) * (2.0 * ra)).astype(BF16)
            da_ref[:, cols] = da
            dh2 = dh2 + _dot_nt(da, w1_ref[:, cols])
        dg_ref[...] += jnp.sum(dh2 * n2, axis=0, keepdims=True)
        dx1_ref[...] = dv + _rms_bwd(dh2, n2, rs, gv)

    row = lambda w: pl.BlockSpec((tm, w), lambda i: (i, 0))
    return _tile_call(
        body, "ffn_bwd", nt,
        [row(D_MODEL), row(D_MODEL), row(D_FF), _const_spec((1, D_MODEL)), _const_spec((D_MODEL, D_FF)),
         _const_spec((D_FF, D_MODEL))],
        [row(D_MODEL), row(D_MODEL), row(D_FF), _acc_spec((1, D_MODEL))],
        [jax.ShapeDtypeStruct((t, D_MODEL), F32), jax.ShapeDtypeStruct((t, D_MODEL), BF16),
         jax.ShapeDtypeStruct((t, D_FF), BF16), jax.ShapeDtypeStruct((1, D_MODEL), F32)],
        [], (dx2, x1, a, g_ff, w_ff1, w_ff2), comm)


def mix_bwd(dx1, x, proj, g_mix, w_in, w_out, wcat, wcat_t, bmat, ln_g, ln_b, avg, conv_w, pool_bd, pool_bd_t,
            pool_scale, *, tm, comm=None):
    t = dx1.shape[0]
    nt = t // tm
    prev_blocks = tm // HALO

    def body(dx1_ref, x_ref, proj_ref, prev_ref, g_ref, win_ref, wout_ref, wcat_ref, wcatt_ref, bmat_ref, lng_ref,
             lnb_ref, avg_ref, cw_ref, pw_ref, pwt_ref, ps_ref,
             dx_ref, h1_ref, dproj_ref, dg_ref, dws_ref, dbm_ref, dlng_ref, dlnb_ref, dcw_ref, dpw_ref, dps_ref,
             dyc, dpj, hbuf, zbuf, dybuf, qbuf):
        i = pl.program_id(0)
        ti = nt - 1 - i

        @pl.when(i == 0)
        def _():
            for ref in (dg_ref, dws_ref, dbm_ref, dlng_ref, dlnb_ref, dcw_ref, dpw_ref, dps_ref):
                ref[...] = jnp.zeros_like(ref)
            dybuf[tm:tm + HALO, :] = jnp.zeros((HALO, D_B), F32)
            qbuf[tm:tm + HALO, :] = jnp.zeros((HALO, D_C), F32)

        dx1v = dx1_ref[...]
        dyc[...] = _dot_nt(dx1v.astype(BF16), wout_ref[...])

        lo_mask = _lane_lt((CHUNK, CHUNK), HEAD_DIM)
        avg = avg_ref[...]
        lng = lng_ref[...]
        gu, dgu = _gelu_and_grad(proj_ref[:, 0:D_A])
        gv, dgv = _gelu_and_grad(proj_ref[:, D_A:2 * D_A])
        cen = gv - _group_mean(gv, avg)
        rstd = lax.rsqrt(_group_mean(cen * cen, avg) + LN_EPS)
        vhat = cen * rstd
        vnb = (vhat * lng + lnb_ref[...]).astype(BF16)
        dya = dyc[:, 0:D_A]
        dm = dya * gu
        dmb = dm.astype(BF16)
        dvn_rows = []
        for c in range(tm // CHUNK):
            rows = slice(c * CHUNK, (c + 1) * CHUNK)
            dbm_ref[...] += dm[rows]
            dvn_parts = []
            for j in range(3):
                cols = slice(j * CHUNK, (j + 1) * CHUNK)
                vnb2 = vnb[rows, cols]
                mixed = _sgu_mix(vnb2, wcat_ref[j], lo_mask) + bmat_ref[:, cols]
                dpj[rows, cols] = dya[rows, cols] * mixed * dgu[rows, cols]
                dmb2 = dmb[rows, cols]
                zero = jnp.zeros_like(dmb2)
                dm_st = jnp.concatenate([jnp.where(lo_mask, dmb2, zero), jnp.where(lo_mask, zero, dmb2)], axis=0)
                dws_ref[j] += _dot_nt(dm_st, vnb2)
                dvn_st = _dot(wcatt_ref[j], dmb2)
                dvn_parts.append(jnp.where(lo_mask, dvn_st[0:CHUNK], dvn_st[CHUNK:2 * CHUNK]))
            dvn_rows.append(jnp.concatenate(dvn_parts, axis=1))
        dvn = jnp.concatenate(dvn_rows, axis=0)
        dlng_ref[...] += jnp.sum(dvn * vhat, axis=0, keepdims=True)
        dlnb_ref[...] += jnp.sum(dvn, axis=0, keepdims=True)
        dvh = dvn * lng
        dgv_in = rstd * (dvh - _group_mean(dvh, avg) - vhat * _group_mean(dvh * vhat, avg))
        dpj[:, D_A:2 * D_A] = dgv_in * dgv

        o = 2 * D_A
        live = (ti > 0).astype(F32)
        zb = proj_ref[:, o:o + D_B]
        gb = proj_ref[:, o + D_B:o + 2 * D_B]
        gc = proj_ref[:, o + 2 * D_B:o + 3 * D_B]
        hcur = gc * zb
        hbuf[0:HALO, :] = prev_ref[:, o + 2 * D_B:o + 3 * D_B] * prev_ref[:, o:o + D_B] * live
        hbuf[HALO:HALO + tm, :] = hcur
        hm1 = hbuf[pl.ds(HALO - 1, tm), :]
        hm2 = hbuf[pl.ds(HALO - 2, tm), :]
        y = cw_ref[2:3, :] * hcur + cw_ref[1:2, :] * hm1 + cw_ref[0:1, :] * hm2
        dout = dyc[:, D_A:D_A + D_B]
        dpj[:, o + D_B:o + 2 * D_B] = dout * y
        dy = dout * gb
        dcw_ref[2:3, :] += jnp.sum(dy * hcur, axis=0, keepdims=True)
        dcw_ref[1:2, :] += jnp.sum(dy * hm1, axis=0, keepdims=True)
        dcw_ref[0:1, :] += jnp.sum(dy * hm2, axis=0, keepdims=True)
        dybuf[0:tm, :] = dy
        dh = (cw_ref[2:3, :] * dy + cw_ref[1:2, :] * dybuf[pl.ds(1, tm), :] + cw_ref[0:1, :] * dybuf[pl.ds(2, tm), :])
        dybuf[tm:tm + HALO, :] = dybuf[0:HALO, :]
        dpj[:, o:o + D_B] = dh * gc
        dpj[:, o + 2 * D_B:o + 3 * D_B] = dh * zb

        zc = proj_ref[:, o + 3 * D_B:D_IN]
        zbuf[0:HALO, :] = prev_ref[:, o + 3 * D_B:D_IN] * live
        zbuf[HALO:HALO + tm, :] = zc
        mean, inv = _pool_means(zbuf[...], tm, ti * tm)
        pooled = (mean - zc).astype(BF16)
        dyp = dyc[:, D_A + D_B:D_MODEL]
        ps = ps_ref[...]
        dps_ref[...] += jnp.sum(dyp * _dot(pooled, pw_ref[...]), axis=0, keepdims=True)
        dpw = (dyp * ps).astype(BF16)
        dpw_ref[...] += _dot_tn(pooled, dpw)
        dpooled = _dot(dpw, pwt_ref[...])
        qbuf[0:tm, :] = dpooled * inv
        q = qbuf[...]
        nrows = tm + HALO
        f2 = q + pltpu.roll(q, nrows - 1, 0)
        f4 = f2 + pltpu.roll(f2, nrows - 2, 0)
        f8 = f4 + pltpu.roll(f4, nrows - 4, 0)
        f16 = f8 + pltpu.roll(f8, nrows - 8, 0)
        lane = lax.broadcasted_iota(jnp.int32, (tm, D_C), 1)
        ahead = jnp.where(lane < 64, f2[0:tm], jnp.where(lane < 128, f4[0:tm], jnp.where(lane < 192, f8[0:tm], f16[0:tm])))
        dpj[:, o + 3 * D_B:D_IN] = ahead - dpooled
        qbuf[tm:tm + HALO, :] = qbuf[0:HALO, :]

        dprojb = dpj[...].astype(BF16)
        dproj_ref[...] = dprojb
        dh1 = _dot_nt(dprojb, win_ref[...])
        xv = x_ref[...]
        rs = lax.rsqrt(jnp.mean(xv * xv, axis=-1, keepdims=True) + RMS_EPS)
        n1 = xv * rs
        gv1 = g_ref[...]
        h1_ref[...] = (n1 * gv1).astype(BF16)
        dg_ref[...] += jnp.sum(dh1 * n1, axis=0, keepdims=True)
        dx_ref[...] = dx1v + _rms_bwd(dh1, n1, rs, gv1)

        @pl.when(i == nt - 1)
        def _():
            tril = (lax.broadcasted_iota(jnp.int32, (2 * CHUNK, CHUNK), 0) % CHUNK
                    >= lax.broadcasted_iota(jnp.int32, (2 * CHUNK, CHUNK), 1))
            for j in range(3):
                dws_ref[j] = jnp.where(tril, dws_ref[j], 0.0)
            dbm_ref[...] = _group_mean_split(dbm_ref[...], avg) * float(HEAD_DIM)

    rev = lambda w: pl.BlockSpec((tm, w), lambda i: (nt - 1 - i, 0))
    prev = pl.BlockSpec((HALO, D_IN), lambda i: (jnp.maximum((nt - 1 - i) * prev_blocks - 1, 0), 0))
    acc_shapes = [(1, D_MODEL), (3, 2 * CHUNK, CHUNK), (CHUNK, D_A), (1, D_A), (1, D_A), (8, D_B), (D_C, D_C), (1, D_C)]
    return _tile_call(
        body, "mix_bwd", nt,
        [rev(D_MODEL), rev(D_MODEL), rev(D_IN), prev, _const_spec((1, D_MODEL)), _const_spec((D_MODEL, D_IN)),
         _const_spec((D_MODEL, D_MODEL)), _const_spec((3, CHUNK, 2 * CHUNK)), _const_spec((3, 2 * CHUNK, CHUNK)),
         _const_spec((CHUNK, D_A)), _const_spec((1, D_A)), _const_spec((1, D_A)), _const_spec((D_A, D_A)),
         _const_spec((8, D_B)), _const_spec((D_C, D_C)), _const_spec((D_C, D_C)), _const_spec((1, D_C))],
        [rev(D_MODEL), rev(D_MODEL), rev(D_IN)] + [_acc_spec(s) for s in acc_shapes],
        [jax.ShapeDtypeStruct((t, D_MODEL), F32), jax.ShapeDtypeStruct((t, D_MODEL), BF16),
         jax.ShapeDtypeStruct((t, D_IN), BF16)] + [jax.ShapeDtypeStruct(s, F32) for s in acc_shapes],
        [pltpu.VMEM((tm, D_MODEL), F32), pltpu.VMEM((tm, D_IN), F32),
         pltpu.VMEM((tm + HALO, D_B), F32), pltpu.VMEM((tm + HALO, D_C), F32),
         pltpu.VMEM((tm + HALO, D_B), F32), pltpu.VMEM((tm + HALO, D_C), F32)],
        (dx1, x, proj, proj, g_mix, w_in, w_out, wcat, wcat_t, bmat, ln_g, ln_b, avg, conv_w, pool_bd, pool_bd_t,
         pool_scale), comm)


def wgrad(a, b, *, tk, a_layer=None, relu_sq=False):
    t, m = a.shape[-2:]
    n = b.shape[1]
    bm = min(m, 1024)
    bn = 1024 if n % 1024 == 0 else n
    nk = t // tk
    if a_layer is None:
        a_spec = pl.BlockSpec((tk, bm), lambda i, j, k: (k, i))
    else:
        a_spec = pl.BlockSpec((None, None, tk, bm), lambda i, j, k: (a_layer, 0, k, i))

    def body(a_ref, b_ref, o_ref):
        k = pl.program_id(2)

        @pl.when(k == 0)
        def _():
            o_ref[...] = jnp.zeros_like(o_ref)

        av = a_ref[...]
        if relu_sq:
            ra = jnp.maximum(av.astype(F32), 0.0)
            av = ra * ra
        o_ref[...] += _dot_tn(av.astype(BF16), b_ref[...].astype(BF16))

    return pl.pallas_call(
        body, name=f"wgrad_{m}x{n}" + ("_relu_sq" if relu_sq else ""), grid=(m // bm, n // bn, nk),
        in_specs=[a_spec, pl.BlockSpec((tk, bn), lambda i, j, k: (k, j))],
        out_specs=pl.BlockSpec((bm, bn), lambda i, j, k: (i, j)),
        out_shape=jax.ShapeDtypeStruct((m, n), F32),
        compiler_params=_params("parallel", "parallel", "arbitrary"),
    )(a, b)


def _row_block(rows, cols, target_bytes):
    target = max(8, target_bytes // (4 * cols))
    if rows <= target:
        return rows
    best = None
    for br in range(8, target + 1, 8):
        if rows % br == 0:
            best = br
    return best if best is not None else rows


def adamw(w, g, m, v):
    shape = w.shape
    cols = shape[-1]
    rows = math.prod(shape[:-1]) if len(shape) > 1 else 1
    br = _row_block(rows, cols, 1 << 20)

    def body(w_ref, g_ref, m_ref, v_ref, d_ref, nm_ref, nv_ref):
        gv = g_ref[...]
        nm = ADAM_B1 * m_ref[...] + (1.0 - ADAM_B1) * gv
        nv = ADAM_B2 * v_ref[...] + (1.0 - ADAM_B2) * jnp.square(gv)
        m_hat = nm / (1.0 - ADAM_B1 ** ADAM_STEP)
        v_hat = nv / (1.0 - ADAM_B2 ** ADAM_STEP)
        d_ref[...] = -ADAM_LR * (m_hat / (jnp.sqrt(v_hat) + ADAM_EPS) + ADAM_WD * w_ref[...])
        nm_ref[...] = nm
        nv_ref[...] = nv

    spec = pl.BlockSpec((br, cols), lambda i: (i, 0))
    outs = pl.pallas_call(
        body, name="adamw", grid=(rows // br,),
        in_specs=[spec] * 4, out_specs=[spec] * 3,
        out_shape=[jax.ShapeDtypeStruct((rows, cols), F32)] * 3,
        compiler_params=pltpu.CompilerParams(dimension_semantics=("parallel",)),
    )(*(a.reshape(rows, cols) for a in (w, g, m, v)))
    return tuple(o.reshape(shape) for o in outs)


ADD_STEPS = 4


def add_halves(geoms, arrs, received, c_idx, dtypes):
    n = len(arrs)

    def body(c_ref, *refs):
        del c_ref
        for a in range(n):
            refs[2 * n + a][...] = (refs[a][...] + refs[n + a][...]).astype(dtypes[a])

    own_specs, half_specs = [], []
    for g in geoms:
        if g.kind == "cols":
            rows, cols = g.half_shape[0] // ADD_STEPS, g.half_shape[1]
            own_specs.append(pl.BlockSpec((rows, cols), lambda i, c_ref: (ADD_STEPS * c_ref[0] + i, 0)))
            half_specs.append(pl.BlockSpec((rows, cols), lambda i, c_ref: (i, 0)))
        else:
            _, h, cols = g.half_shape
            own_specs.append(pl.BlockSpec((None, None, h, cols), lambda i, c_ref: (i, c_ref[0], 0, 0)))
            half_specs.append(pl.BlockSpec((None, h, cols), lambda i, c_ref: (i, 0, 0)))
    return pl.pallas_call(
        body, name="add_halves",
        grid_spec=pltpu.PrefetchScalarGridSpec(num_scalar_prefetch=1, grid=(ADD_STEPS,),
                                               in_specs=own_specs + half_specs, out_specs=half_specs),
        out_shape=[jax.ShapeDtypeStruct(g.half_shape, dt) for g, dt in zip(geoms, dtypes)],
        compiler_params=_params("parallel"),
    )(c_idx, *arrs, *received)


def add_parts(geoms, landed, finals, layer, c_idx):
    n = len(landed)

    def body(c_ref, *refs):
        del c_ref
        for a in range(n):
            p_ref = refs[a]
            parts = [p_ref[j].astype(F32) for j in range(N_CHIPS)]
            refs[2 * n + a][...] = ((parts[0] + parts[1]) + parts[2]) + parts[3]

    in_specs, out_specs = [], []
    for g in geoms:
        rows, cols = g.part_shape[0] // ADD_STEPS, g.part_shape[1]
        in_specs.append(pl.BlockSpec((N_CHIPS, rows, cols), lambda i, c_ref: (0, i, 0)))
        out_specs.append(pl.BlockSpec((None, rows, cols), lambda i, c_ref: (layer, ADD_STEPS * c_ref[0] + i, 0)))
    return pl.pallas_call(
        body, name="add_parts",
        grid_spec=pltpu.PrefetchScalarGridSpec(num_scalar_prefetch=1, grid=(ADD_STEPS,),
                                               in_specs=in_specs + [_ANY] * n, out_specs=out_specs),
        out_shape=[jax.ShapeDtypeStruct(f.shape, F32) for f in finals],
        input_output_aliases={1 + n + a: a for a in range(n)},
        compiler_params=_params("parallel"),
    )(c_idx, *landed, *finals)


def _shard_dims(k, n, axis):
    return (k // N_CHIPS, n) if axis == 0 else (k, n // N_CHIPS)


W_IN_STRIDE = 512
W_IN_WINDOW = 640


def _big_geoms():
    geoms = []
    for name, k, n, axis in BIG:
        if axis == 0:
            geoms.append(_Geom("rows", (N_CHIPS, 2, k // N_CHIPS // 2, n)))
        elif name == "w_in":
            geoms.append(_Geom("cols", (k, n), W_IN_STRIDE, W_IN_WINDOW))
        else:
            geoms.append(_Geom("cols", (k, n), n // N_CHIPS, n // N_CHIPS))
    return geoms


def _grad_views(gb, geoms):
    return [gb[name].reshape(g.shape) for (name, _, _, _), g in zip(BIG, geoms)]


def _round_up(v, m):
    return (v + m - 1) // m * m


def _prep_small(small):
    tril = jnp.tril(jnp.ones((CHUNK, CHUNK), bool))
    wm = jnp.where(tril, small["sgu_w"], 0.0).astype(BF16).reshape(DEPTH, 3, 2, CHUNK, CHUNK)
    head = jnp.arange(D_A) // HEAD_DIM
    grp = jnp.arange(D_C) // HEAD_DIM
    pw_rows = small["pool_w"].reshape(DEPTH, D_C, HEAD_DIM)
    pool_bd = jnp.where((grp[:, None] == grp[None, :])[None], jnp.tile(pw_rows, (1, 1, D_C // HEAD_DIM)), 0.0).astype(BF16)
    return dict(
        wcat=wm.transpose(0, 1, 3, 2, 4).reshape(DEPTH, 3, CHUNK, 2 * CHUNK),
        wcat_t=wm.transpose(0, 1, 2, 4, 3).reshape(DEPTH, 3, 2 * CHUNK, CHUNK),
        bmat=jnp.repeat(jnp.swapaxes(small["sgu_b"], 1, 2), HEAD_DIM, axis=2),
        avg=jnp.where(head[:, None] == head[None, :], 1.0 / HEAD_DIM, 0.0).astype(BF16),
        pool_bd=pool_bd, pool_bd_t=jnp.swapaxes(pool_bd, 1, 2),
        conv8=jnp.pad(small["conv_w"], ((0, 0), (0, 8 - 3), (0, 0))),
    )


def _row(a):
    return a.reshape(1, -1)


MIX_WEIGHTS = ("w_in", "w_out")
MLP_WEIGHTS = ("w_ff1", "w_ff2", "w_ple_gate", "w_ple_proj")
ALL_BIG = MIX_WEIGHTS + MLP_WEIGHTS
EARLY_GRADS = MLP_WEIGHTS + ("w_out",)
FFN_BWD_TILE = 512


def _fwd_layer(h, p, wl, small, prep, l, tm, comm_mix=None, comm_mlp=None):
    (proj, ycat, x1), got = mix_fwd(h, _row(small["norm_mix_g"][l]), wl["w_in"], wl["w_out"], prep["wcat"][l],
                                    prep["bmat"][l], _row(small["sgu_ln_g"][l]), _row(small["sgu_ln_b"][l]), prep["avg"],
                                    prep["conv8"][l], prep["pool_bd"][l], _row(small["pool_scale"][l]),
                                    tm=min(2 * tm, h.shape[0]), comm=comm_mix)
    if comm_mix is not None:
        wl = {**wl, **_weights_of(got, MLP_WEIGHTS)}
    (a, x2, x3), couts = ffn_ple_fwd(x1, (p, l), _row(small["norm_ff_g"][l]), wl["w_ff1"], wl["w_ff2"],
                                     _row(small["norm_ple_g"][l]), wl["w_ple_gate"], wl["w_ple_proj"], tm=tm,
                                     comm=comm_mlp)
    return (h, proj, ycat, x1, a, x2), x3, couts, wl


def _merge_comms(comms):
    comms = [cm for cm in comms if cm is not None]
    if len(comms) <= 1:
        return comms[0] if comms else None
    spans, ni, no, ns = [], 0, 0, 0
    for cm in comms:
        spans.append((ni, no, ns))
        ni, no, ns = ni + len(cm.ins), no + len(cm.out_shapes), ns + len(cm.sems)

    def copies(in_refs, out_refs, sem_refs):
        local, sends, recvs = [], [], []
        for cm, (i0, o0, s0) in zip(comms, spans):
            got = cm.copies(in_refs[i0:i0 + len(cm.ins)], out_refs[o0:o0 + len(cm.out_shapes)],
                            sem_refs[s0:s0 + len(cm.sems)])
            local, sends, recvs = local + got[0], sends + got[1], recvs + got[2]
        return local, sends, recvs

    aliases = {i0 + i: o0 + o for cm, (i0, o0, _) in zip(comms, spans) for i, o in cm.aliases.items()}
    return _Comm(sum((cm.ins for cm in comms), []), sum((cm.out_shapes for cm in comms), []),
                 sum((cm.sems for cm in comms), []), copies, aliases)


def _split_results(results, comms):
    out, at = [], 0
    for cm in comms:
        if cm is None:
            out.append(None)
        else:
            out.append(results[at:at + len(cm.out_shapes)])
            at += len(cm.out_shapes)
    return out


class _Reduction:
    def __init__(self, layer, names, geoms, arrs, finals, c_arr, narrow=()):
        self.layer, self.names, self.geoms, self.arrs = layer, list(names), list(geoms), list(arrs)
        self.finals, self.c_arr = finals, c_arr
        self.dtypes = [BF16 if n in narrow else F32 for n in self.names]

    def comm_a(self):
        return _reduce_a_comm(self.geoms, self.arrs)

    def comm_b(self, received):
        return _reduce_b_comm(self.geoms, add_halves(self.geoms, self.arrs, received, self.c_arr, self.dtypes))

    def comm_c(self, landed):
        mine = add_parts(self.geoms, landed, [self.finals[n] for n in self.names], self.layer, self.c_arr)
        return _reduce_c_comm(self.geoms, mine, self.layer)

    def done(self, results):
        self.finals.update(zip(self.names, results))


class _Plan:
    def ple(self):
        return None

    def after_ple(self, results):
        pass

    def ffn(self):
        return None

    def after_ffn(self, results):
        pass

    def before_mix(self, gb):
        pass

    def mix(self):
        return None

    def after_mix(self, results):
        pass


class _CarryPlan(_Plan):
    def __init__(self, above):
        self.above = above

    def ple(self):
        return self.above.comm_a()

    def after_ple(self, results):
        self.received = results

    def ffn(self):
        return self.above.comm_b(self.received)

    def after_ffn(self, results):
        self.landed = results

    def mix(self):
        return self.above.comm_c(self.landed)

    def after_mix(self, results):
        self.above.done(results)


class _LastPlan(_CarryPlan):
    def __init__(self, above, make_early):
        super().__init__(above)
        self.make_early = make_early

    def before_mix(self, gb):
        self.early = self.make_early(gb)
        self.early_received = _run_comm(self.early.comm_a(), "reduce_a_early")

    def mix(self):
        self.parts = [self.above.comm_c(self.landed), self.early.comm_b(self.early_received)]
        return _merge_comms(self.parts)

    def after_mix(self, results):
        above_res, self.early_landed = _split_results(results, self.parts)
        self.above.done(above_res)


def _bwd_layer(d, saved, p, wl, small, prep, l, tm, tk, plan=None):
    plan = plan or _Plan()
    xin, proj, ycat, x1, a, x2 = saved
    (dx2, h3, dpre, dpp, dg_ple), res = ple_bwd(d, x2, (p, l), _row(small["norm_ple_g"][l]), wl["w_ple_gate"],
                                                wl["w_ple_proj"], tm=tm, comm=plan.ple())
    plan.after_ple(res)
    gb = {"w_ple_gate": wgrad(h3, dpre, tk=tk), "w_ple_proj": wgrad(p, dpp, tk=tk, a_layer=l)}
    (dx1, h2, da, dg_ff), res = ffn_bwd(dx2, x1, a, _row(small["norm_ff_g"][l]), wl["w_ff1"], wl["w_ff2"],
                                        tm=FFN_BWD_TILE if tm >= FFN_BWD_TILE else tm, comm=plan.ffn())
    plan.after_ffn(res)
    gb["w_ff2"] = wgrad(a, dx2, tk=tk, relu_sq=True)
    gb["w_ff1"] = wgrad(h2, da, tk=tk)
    gb["w_out"] = wgrad(ycat, dx1, tk=tk)
    plan.before_mix(gb)
    (dprev, h1, dproj, dg_mix, dws, dbm, dlng, dlnb, dcw, dpw, dps), res = mix_bwd(
        dx1, xin, proj, _row(small["norm_mix_g"][l]), wl["w_in"], wl["w_out"], prep["wcat"][l], prep["wcat_t"][l],
        prep["bmat"][l], _row(small["sgu_ln_g"][l]), _row(small["sgu_ln_b"][l]), prep["avg"], prep["conv8"][l],
        prep["pool_bd"][l], prep["pool_bd_t"][l], _row(small["pool_scale"][l]), tm=tm, comm=plan.mix())
    plan.after_mix(res)
    gb["w_in"] = wgrad(h1, dproj, tk=tk)
    gs = {
        "norm_ple_g": dg_ple[0], "norm_ff_g": dg_ff[0], "norm_mix_g": dg_mix[0],
        "sgu_w": dws.reshape(2 * 3, CHUNK, CHUNK), "sgu_b": dbm[:, ::HEAD_DIM].T,
        "sgu_ln_g": dlng[0], "sgu_ln_b": dlnb[0], "conv_w": dcw[0:3], "pool_scale": dps[0],
        "pool_w": jnp.stack([dpw[g * HEAD_DIM:(g + 1) * HEAD_DIM, g * HEAD_DIM:(g + 1) * HEAD_DIM]
                             for g in range(D_C // HEAD_DIM)]),
    }
    return dprev, gb, gs


def _local_step(x, p, target, full, small, *, tm, tk):
    prep = _prep_small(small)
    p = p[:, None]
    saved, h = [], x
    for l in range(DEPTH):
        wl = {name: full[name][l] for name in full}
        s, h, _, _ = _fwd_layer(h, p, wl, small, prep, l, tm)
        saved.append(s)
    loss_blk, d_final_g, d = loss_head(h, target, _row(small["final_g"]), tm=tm)
    gbig, gsm = [None] * DEPTH, [None] * DEPTH
    for l in reversed(range(DEPTH)):
        wl = {name: full[name][l] for name in full}
        d, gbig[l], gsm[l] = _bwd_layer(d, saved[l], p, wl, small, prep, l, tm, tk)
    big = {name: jnp.stack([gbig[l][name] for l in range(DEPTH)]) for name in gbig[0]}
    sm = {name: jnp.stack([gsm[l][name] for l in range(DEPTH)]) for name in gsm[0]}
    sm["final_g"] = d_final_g[0]
    return loss_blk[0, 0], d, big, sm


def _weights_of(gathered, names):
    wl = dict(zip([b[0] for b in BIG if b[0] in names], gathered))
    if "w_in" in wl:
        wl["w_in"] = wl["w_in"].transpose(1, 0, 2).reshape(D_MODEL, D_IN)
    return wl


def kernel(x, p, norm_mix_g, w_in, sgu_w, sgu_b, sgu_ln_g, sgu_ln_b, conv_w, pool_w, pool_scale, w_out, norm_ff_g, w_ff1, w_ff2, norm_ple_g, w_ple_gate, w_ple_proj, final_g, loss_target, m_norm_mix_g, m_w_in, m_sgu_w, m_sgu_b, m_sgu_ln_g, m_sgu_ln_b, m_conv_w, m_pool_w, m_pool_scale, m_w_out, m_norm_ff_g, m_w_ff1, m_w_ff2, m_norm_ple_g, m_w_ple_gate, m_w_ple_proj, m_final_g, v_norm_mix_g, v_w_in, v_sgu_w, v_sgu_b, v_sgu_ln_g, v_sgu_ln_b, v_conv_w, v_pool_w, v_pool_scale, v_w_out, v_norm_ff_g, v_w_ff1, v_w_ff2, v_norm_ple_g, v_w_ple_gate, v_w_ple_proj, v_final_g):
    args = dict(locals())
    w = {name: args[name] for name in WEIGHTS}
    m = {name: args["m_" + name] for name in WEIGHTS}
    v = {name: args["v_" + name] for name in WEIGHTS}
    t = x.shape[1]
    tm = min(512, t)
    tk = min(2048, t)
    x_idx, y_idx, c_idx = _place()
    chip = 2 * x_idx + y_idx
    c_arr = c_idx.reshape(1).astype(jnp.int32)
    xs, target = x[0], loss_target[0]

    shards = {name: w[name].astype(BF16) for name, _, _, _ in BIG}
    conv_rows = _round_up(CONV_SHARD, 8 * 128) // 128
    conv_flat = jnp.pad(w["conv_w"].reshape(-1), (0, conv_rows * 128 - CONV_SHARD)).reshape(conv_rows, 128)
    first = _run_comm(_gather_comm(shards, 0, MIX_WEIGHTS, conv_flat), "gather_first")
    conv_full = (first[len(MIX_WEIGHTS)].reshape(N_CHIPS, -1)[:, :CONV_SHARD]
                 .reshape(N_CHIPS, DEPTH, 3, D_B // N_CHIPS).transpose(1, 2, 0, 3).reshape(DEPTH, 3, D_B))
    small = {name: w[name] for name in SMALL}
    small["conv_w"] = conv_full
    prep = _prep_small(small)

    wl = [None] * DEPTH
    wl[0] = _weights_of(first, MIX_WEIGHTS)
    saved, h = [], xs
    for l in range(DEPTH):
        comm_mix = _gather_comm(shards, 0, MLP_WEIGHTS) if l == 0 else None
        comm_mlp = _gather_comm(shards, l + 1, ALL_BIG) if l + 1 < DEPTH else None
        s, h, got, wl[l] = _fwd_layer(h, p, wl[l], small, prep, l, tm, comm_mix, comm_mlp)
        saved.append(s)
        if comm_mlp is not None:
            wl[l + 1] = _weights_of(got, ALL_BIG)

    loss_blk, d_final_g, d = loss_head(h, target, _row(small["final_g"]), tm=tm)

    geoms = dict(zip([b[0] for b in BIG], _big_geoms()))
    finals = {name: jnp.zeros((DEPTH,) + g.final_shape, F32) for name, g in geoms.items()}

    def reduction(layer, names, gb, narrow=()):
        return _Reduction(layer, names, [geoms[n] for n in names], [gb[n].reshape(geoms[n].shape) for n in names],
                          finals, c_arr, narrow)

    gsm = [None] * DEPTH
    above = None
    for l in reversed(range(DEPTH)):
        if above is None:
            plan = _Plan()
        elif l > 0:
            plan = _CarryPlan(above)
        else:
            plan = _LastPlan(above, lambda gb: reduction(0, EARLY_GRADS, gb))
        d, gb, gsm[l] = _bwd_layer(d, saved[l], p, wl[l], small, prep, l, tm, tk, plan)
        if l > 0:
            above = reduction(l, ALL_BIG, gb)

    sm = {name: jnp.stack([gsm[i][name] for i in range(DEPTH)]) for name in gsm[0]}
    sm["final_g"] = d_final_g[0]
    sizes = [sm[name].size for name in SMALL]
    small_rows = _round_up(-(-sum(sizes) // (2 * N_CHIPS * LANES)), 8 * ADD_STEPS)
    small_flat = jnp.pad(jnp.concatenate([sm[name].reshape(-1) for name in SMALL]),
                         (0, 2 * N_CHIPS * small_rows * LANES - sum(sizes)))
    geoms["small"] = _Geom("rows", (N_CHIPS, 2, small_rows, LANES))
    finals["small"] = jnp.zeros((1,) + geoms["small"].final_shape, F32)
    late = reduction(0, ("w_in", "small"), {**gb, "small": small_flat}, narrow=("w_in",))
    late_landed = _run_comm(late.comm_b(_run_comm(late.comm_a(), "reduce_a_late")), "reduce_b_late")
    last = [plan.early.comm_c(plan.early_landed), late.comm_c(late_landed)]
    early_res, late_res = _split_results(_run_comm(_merge_comms(last), "reduce_c_last"), last)
    plan.early.done(early_res)
    late.done(late_res)

    grads = {name: finals[name] for name, _, _, _ in BIG}
    grads["w_in"] = lax.dynamic_slice_in_dim(grads["w_in"], chip * (D_IN // N_CHIPS - W_IN_STRIDE), D_IN // N_CHIPS, axis=2)
    small_red = _run_comm(_allgather_comm(finals["small"][0]), "small_allgather")[0].reshape(-1)
    off = 0
    for name, size in zip(SMALL, sizes):
        grads[name] = small_red[off:off + size].reshape(sm[name].shape)
        off += size
    grads["conv_w"] = lax.dynamic_slice_in_dim(grads["conv_w"], chip * (D_B // N_CHIPS), D_B // N_CHIPS, axis=2)

    loss = lax.psum(loss_blk[0, 0], ("x", "y", "c"))
    delta, new_m, new_v = {}, {}, {}
    for name in WEIGHTS:
        delta[name], new_m[name], new_v[name] = adamw(w[name], grads[name], m[name], v[name])
    return (loss, d[None], *[grads[n] for n in WEIGHTS], *[delta[n] for n in WEIGHTS],
            *[new_m[n] for n in WEIGHTS], *[new_v[n] for n in WEIGHTS])
```

```python
import math

import jax
import jax.numpy as jnp
from jax import lax
from jax.experimental import pallas as pl
from jax.experimental.pallas import tpu as pltpu

F32 = jnp.float32
BF16 = jnp.bfloat16

D_MODEL = 1024
DEPTH = 4
D_PLE = 256
D_FF = 4096
HEAD_DIM = 64
D_A = 384
D_B = 384
D_C = 256
D_IN = 2176
CHUNK = 128
HALO = 16
RMS_EPS = 1e-6
LN_EPS = 1e-5
N_CHIPS = 4
LANES = 1024

ADAM_LR = 0.001
ADAM_B1 = 0.9
ADAM_B2 = 0.999
ADAM_EPS = 1e-08
ADAM_WD = 0.01
ADAM_STEP = 10

VMEM_LIMIT_BYTES = 60 * 1024 * 1024

_RSQRT2 = 0.7071067811865476
_INV_SQRT_2PI = 0.3989422804014327

BIG = (
    ("w_in", D_MODEL, D_IN, 1),
    ("w_out", D_MODEL, D_MODEL, 0),
    ("w_ff1", D_MODEL, D_FF, 1),
    ("w_ff2", D_FF, D_MODEL, 0),
    ("w_ple_gate", D_MODEL, D_MODEL, 0),
    ("w_ple_proj", D_PLE, D_MODEL, 1),
)
SMALL = ("norm_mix_g", "sgu_w", "sgu_b", "sgu_ln_g", "sgu_ln_b", "conv_w", "pool_w", "pool_scale",
         "norm_ff_g", "norm_ple_g", "final_g")
WEIGHTS = ("norm_mix_g", "w_in", "sgu_w", "sgu_b", "sgu_ln_g", "sgu_ln_b", "conv_w", "pool_w", "pool_scale",
           "w_out", "norm_ff_g", "w_ff1", "w_ff2", "norm_ple_g", "w_ple_gate", "w_ple_proj", "final_g")
CONV_SHARD = DEPTH * 3 * (D_B // N_CHIPS)


def _dot(a, b):
    return jnp.dot(a, b, preferred_element_type=F32)


def _dot_nt(a, b):
    return lax.dot_general(a, b, (((1,), (1,)), ((), ())), preferred_element_type=F32)


def _dot_tn(a, b):
    return lax.dot_general(a, b, (((0,), (0,)), ((), ())), preferred_element_type=F32)


def _const_spec(shape):
    nd = len(shape)
    return pl.BlockSpec(shape, lambda i: (0,) * nd, pipeline_mode=pl.Buffered(1))


def _acc_spec(shape):
    nd = len(shape)
    return pl.BlockSpec(shape, lambda i: (0,) * nd)


def _layer_rows(layer, tm):
    return pl.BlockSpec((None, None, tm, D_PLE), lambda i: (layer, 0, i, 0))


def _params(*sem):
    return pltpu.CompilerParams(dimension_semantics=sem, vmem_limit_bytes=VMEM_LIMIT_BYTES)


def _rms_bwd(dh, n, rs, g):
    dn = dh * g
    return rs * (dn - n * jnp.mean(dn * n, axis=-1, keepdims=True))


def _gelu(x):
    return x * (0.5 * (1.0 + lax.erf(x * _RSQRT2)))


def _gelu_and_grad(x):
    cdf = 0.5 * (1.0 + lax.erf(x * _RSQRT2))
    return x * cdf, cdf + x * (jnp.exp(-0.5 * x * x) * _INV_SQRT_2PI)


def _group_mean(v, avg):
    vb = v.astype(BF16)
    split = 2 * CHUNK
    return jnp.concatenate([_dot(vb[:, :split], avg[:split, :split]), _dot(vb[:, split:], avg[split:, split:])], axis=1)


def _group_mean_split(v, avg):
    hi = v.astype(BF16)
    lo = (v - hi.astype(F32)).astype(BF16)
    return _dot(hi, avg) + _dot(lo, avg)


def _lane_lt(shape, bound):
    return lax.broadcasted_iota(jnp.int32, shape, 1) < bound


def _sgu_mix(vnb2, wcat_j, lo_mask):
    zero = jnp.zeros_like(vnb2)
    stacked = jnp.concatenate([jnp.where(lo_mask, vnb2, zero), jnp.where(lo_mask, zero, vnb2)], axis=0)
    return _dot(wcat_j, stacked)


def _pool_means(ext, tile_rows, first_pos):
    s2 = ext + pltpu.roll(ext, 1, 0)
    s4 = s2 + pltpu.roll(s2, 2, 0)
    s8 = s4 + pltpu.roll(s4, 4, 0)
    s16 = s8 + pltpu.roll(s8, 8, 0)
    pos = (first_pos + lax.broadcasted_iota(jnp.int32, (tile_rows, 1), 0) + 1).astype(F32)
    lane = lax.broadcasted_iota(jnp.int32, (tile_rows, D_C), 1)
    sums = jnp.where(lane < 64, s2[HALO:], jnp.where(lane < 128, s4[HALO:], jnp.where(lane < 192, s8[HALO:], s16[HALO:])))
    win = jnp.where(lane < 64, 2.0, jnp.where(lane < 128, 4.0, jnp.where(lane < 192, 8.0, 16.0)))
    inv = 1.0 / jnp.minimum(pos, win)
    return sums * inv, inv


MESH = pl.DeviceIdType.MESH
_ANY = pl.BlockSpec(memory_space=pl.ANY)


def _place():
    return lax.axis_index("x"), lax.axis_index("y"), lax.axis_index("c")


def _chip_peers(x, y):
    return [(1 - x, y), (x, 1 - y), (1 - x, 1 - y)]


class _Comm:
    def __init__(self, ins, out_shapes, sems, copies, aliases=None):
        self.ins, self.out_shapes, self.sems, self.copies = list(ins), list(out_shapes), list(sems), copies
        self.aliases = dict(aliases or {})

    def start(self, in_refs, out_refs, sem_refs):
        local, sends, _ = self.copies(in_refs, out_refs, sem_refs)
        for cp in local + sends:
            cp.start()

    def wait(self, in_refs, out_refs, sem_refs):
        local, sends, recvs = self.copies(in_refs, out_refs, sem_refs)
        for cp in recvs:
            cp.wait_recv()
        for cp in sends:
            cp.wait_send()
        for cp in local:
            cp.wait()


def _remote(src, dst, send_sem, recv_sem, device):
    return pltpu.make_async_remote_copy(src_ref=src, dst_ref=dst, send_sem=send_sem, recv_sem=recv_sem,
                                        device_id=device, device_id_type=MESH)


def _gather_comm(shards, layer, names, conv=None):
    mats = [b for b in BIG if b[0] in names]
    ins = [shards[name] for name, _, _, _ in mats] + ([conv] if conv is not None else [])
    out_shapes = []
    for name, k, n, axis in mats:
        shape = (N_CHIPS, k, n // N_CHIPS) if name == "w_in" else (k, n)
        out_shapes.append(jax.ShapeDtypeStruct(shape, BF16))
    if conv is not None:
        out_shapes.append(jax.ShapeDtypeStruct((N_CHIPS,) + conv.shape, conv.dtype))
    n_arr = len(ins)

    def block(a, out_ref, chip):
        if a == len(mats) or mats[a][0] == "w_in":
            return out_ref.at[chip]
        _, k, n, axis = mats[a]
        if axis == 0:
            return out_ref.at[pl.ds(chip * (k // N_CHIPS), k // N_CHIPS), :]
        return out_ref.at[:, pl.ds(chip * (n // N_CHIPS), n // N_CHIPS)]

    def copies(in_refs, out_refs, sem_refs):
        send_sems, recv_sems, local_sems = sem_refs
        x, y, c = _place()
        me = 2 * x + y
        local, sends, recvs = [], [], []
        for a in range(n_arr):
            src = in_refs[a].at[layer] if a < len(mats) else in_refs[a]
            local.append(pltpu.make_async_copy(src, block(a, out_refs[a], me), local_sems.at[a]))
            for j, (px, py) in enumerate(_chip_peers(x, y)):
                sends.append(_remote(src, block(a, out_refs[a], me), send_sems.at[a, j], recv_sems.at[a, j], (px, py, c)))
                recvs.append(_remote(src, block(a, out_refs[a], 2 * px + py), send_sems.at[a, j], recv_sems.at[a, j],
                                     (px, py, c)))
        return local, sends, recvs

    sems = [pltpu.SemaphoreType.DMA((n_arr, 3)), pltpu.SemaphoreType.DMA((n_arr, 3)), pltpu.SemaphoreType.DMA((n_arr,))]
    return _Comm(ins, out_shapes, sems, copies)


def _allgather_comm(a):
    def copies(in_refs, out_refs, sem_refs):
        send_sems, recv_sems, local_sem = sem_refs
        x, y, c = _place()
        me = 2 * x + y
        local = [pltpu.make_async_copy(in_refs[0], out_refs[0].at[me], local_sem)]
        sends, recvs = [], []
        for j, (px, py) in enumerate(_chip_peers(x, y)):
            sends.append(_remote(in_refs[0], out_refs[0].at[me], send_sems.at[j], recv_sems.at[j], (px, py, c)))
            recvs.append(_remote(in_refs[0], out_refs[0].at[2 * px + py], send_sems.at[j], recv_sems.at[j], (px, py, c)))
        return local, sends, recvs

    sems = [pltpu.SemaphoreType.DMA((3,)), pltpu.SemaphoreType.DMA((3,)), pltpu.SemaphoreType.DMA]
    return _Comm([a], [jax.ShapeDtypeStruct((N_CHIPS,) + a.shape, a.dtype)], sems, copies)


class _Geom:
    def __init__(self, kind, shape, stride=None, width=None):
        self.kind, self.shape, self.stride, self.width = kind, tuple(shape), stride, width
        if kind == "cols":
            k, n = shape
            self.half_shape, self.part_shape, self.final_shape = (k // 2, n), (k // 2, width), (k, width)
        else:
            _, _, h, n = shape
            self.half_shape, self.part_shape, self.final_shape = (N_CHIPS, h, n), (h, n), (2 * h, n)

    def half(self, ref, core):
        if self.kind == "cols":
            return ref.at[pl.ds(core * self.half_shape[0], self.half_shape[0]), :]
        return ref.at[:, core]

    def part(self, ref, chip):
        if self.kind == "cols":
            return ref.at[:, pl.ds(chip * self.stride, self.width)]
        return ref.at[chip]

    def final_half(self, ref, layer, core):
        rows = self.part_shape[0]
        return ref.at[layer, pl.ds(core * rows, rows), :]


def _reduce_a_comm(geoms, arrs):
    n = len(arrs)

    def copies(in_refs, out_refs, sem_refs):
        x, y, c = _place()
        cps = [_remote(geoms[a].half(in_refs[a], 1 - c), out_refs[a], sem_refs[0].at[a], sem_refs[1].at[a], (x, y, 1 - c))
               for a in range(n)]
        return [], cps, cps

    return _Comm(arrs, [jax.ShapeDtypeStruct(g.half_shape, F32) for g in geoms],
                 [pltpu.SemaphoreType.DMA((n,)), pltpu.SemaphoreType.DMA((n,))], copies)


def _reduce_b_comm(geoms, halves):
    n = len(halves)

    def copies(in_refs, out_refs, sem_refs):
        send_sems, recv_sems, local_sems = sem_refs
        x, y, c = _place()
        me = 2 * x + y
        local, sends, recvs = [], [], []
        for a in range(n):
            g = geoms[a]
            local.append(pltpu.make_async_copy(g.part(in_refs[a], me), out_refs[a].at[me], local_sems.at[a]))
            for j, (px, py) in enumerate(_chip_peers(x, y)):
                peer = 2 * px + py
                sends.append(_remote(g.part(in_refs[a], peer), out_refs[a].at[me], send_sems.at[a, j], recv_sems.at[a, j],
                                     (px, py, c)))
                recvs.append(_remote(g.part(in_refs[a], me), out_refs[a].at[peer], send_sems.at[a, j], recv_sems.at[a, j],
                                     (px, py, c)))
        return local, sends, recvs

    sems = [pltpu.SemaphoreType.DMA((n, 3)), pltpu.SemaphoreType.DMA((n, 3)), pltpu.SemaphoreType.DMA((n,))]
    return _Comm(halves, [jax.ShapeDtypeStruct((N_CHIPS,) + g.part_shape, h.dtype) for g, h in zip(geoms, halves)], sems,
                 copies)


def _reduce_c_comm(geoms, finals, layer):
    n = len(finals)

    def copies(in_refs, out_refs, sem_refs):
        send_sems, recv_sems = sem_refs
        x, y, c = _place()
        sends, recvs = [], []
        for a in range(n):
            g = geoms[a]
            sends.append(_remote(g.final_half(in_refs[a], layer, c), g.final_half(out_refs[a], layer, c), send_sems.at[a],
                                 recv_sems.at[a], (x, y, 1 - c)))
            recvs.append(_remote(g.final_half(in_refs[a], layer, c), g.final_half(out_refs[a], layer, 1 - c),
                                 send_sems.at[a], recv_sems.at[a], (x, y, 1 - c)))
        return [], sends, recvs

    sems = [pltpu.SemaphoreType.DMA((n,)), pltpu.SemaphoreType.DMA((n,))]
    return _Comm(finals, [jax.ShapeDtypeStruct(f.shape, f.dtype) for f in finals], sems, copies,
                 aliases={a: a for a in range(n)})


def _run_comm(comm, name):
    def body(*refs):
        ni, no = len(comm.ins), len(comm.out_shapes)
        in_refs, out_refs, sem_refs = refs[:ni], refs[ni:ni + no], refs[ni + no:]
        comm.start(in_refs, out_refs, sem_refs)
        comm.wait(in_refs, out_refs, sem_refs)

    return pl.pallas_call(
        body, name=name, in_specs=[_ANY] * len(comm.ins), out_specs=[_ANY] * len(comm.out_shapes),
        out_shape=comm.out_shapes, scratch_shapes=comm.sems, input_output_aliases=comm.aliases,
        compiler_params=pltpu.CompilerParams(has_side_effects=True),
    )(*comm.ins)


def _tile_call(body, name, nt, in_specs, out_specs, out_shape, scratch, args, comm):
    if comm is None:
        outs = pl.pallas_call(body, name=name, grid=(nt,), in_specs=in_specs, out_specs=out_specs, out_shape=out_shape,
                              scratch_shapes=scratch, compiler_params=_params("arbitrary"))(*args)
        return outs, []
    n_in, n_out, n_scr = len(in_specs), len(out_specs), len(scratch)
    ci, co = len(comm.ins), len(comm.out_shapes)

    def hosted(*refs):
        in_refs = refs[:n_in]
        cin = refs[n_in:n_in + ci]
        out_refs = refs[n_in + ci:n_in + ci + n_out]
        cout = refs[n_in + ci + n_out:n_in + ci + n_out + co]
        scr = refs[n_in + ci + n_out + co:n_in + ci + n_out + co + n_scr]
        sems = refs[n_in + ci + n_out + co + n_scr:]
        i = pl.program_id(0)

        @pl.when(i == 0)
        def _():
            comm.start(cin, cout, sems)

        body(*in_refs, *out_refs, *scr)

        @pl.when(i == nt - 1)
        def _():
            comm.wait(cin, cout, sems)

    outs = pl.pallas_call(
        hosted, name=name + "_comm", grid=(nt,),
        in_specs=list(in_specs) + [_ANY] * ci, out_specs=list(out_specs) + [_ANY] * co,
        out_shape=list(out_shape) + comm.out_shapes, scratch_shapes=list(scratch) + comm.sems,
        input_output_aliases={n_in + i: n_out + o for i, o in comm.aliases.items()},
        compiler_params=_params("arbitrary"),
    )(*args, *comm.ins)
    return outs[:n_out], outs[n_out:]


def mix_fwd(x, g_mix, w_in, w_out, wcat, bmat, ln_g, ln_b, avg, conv_w, pool_bd, pool_scale, *, tm, comm=None):
    t = x.shape[0]
    nt = t // tm

    def body(x_ref, g_ref, win_ref, wout_ref, wcat_ref, bmat_ref, lng_ref, lnb_ref, avg_ref, cw_ref, pw_ref, ps_ref,
             proj_ref, ycat_ref, x1_ref, hbuf, zbuf):
        i = pl.program_id(0)

        @pl.when(i == 0)
        def _():
            hbuf[0:HALO, :] = jnp.zeros((HALO, D_B), F32)
            zbuf[0:HALO, :] = jnp.zeros((HALO, D_C), F32)

        xv = x_ref[...]
        n = xv * lax.rsqrt(jnp.mean(xv * xv, axis=-1, keepdims=True) + RMS_EPS)
        h1 = (n * g_ref[...]).astype(BF16)
        proj_ref[...] = _dot(h1, win_ref[...])

        lo_mask = _lane_lt((CHUNK, CHUNK), HEAD_DIM)
        avg = avg_ref[...]
        gu = _gelu(proj_ref[:, 0:D_A])
        gv = _gelu(proj_ref[:, D_A:2 * D_A])
        dv = gv - _group_mean(gv, avg)
        var = _group_mean(dv * dv, avg)
        vnb = (dv * lax.rsqrt(var + LN_EPS) * lng_ref[...] + lnb_ref[...]).astype(BF16)
        for c in range(tm // CHUNK):
            rows = slice(c * CHUNK, (c + 1) * CHUNK)
            for j in range(3):
                cols = slice(j * CHUNK, (j + 1) * CHUNK)
                mixed = _sgu_mix(vnb[rows, cols], wcat_ref[j], lo_mask) + bmat_ref[:, cols]
                ycat_ref[rows, cols] = (gu[rows, cols] * mixed).astype(BF16)

        o = 2 * D_A
        hcur = proj_ref[:, o + 2 * D_B:o + 3 * D_B] * proj_ref[:, o:o + D_B]
        hbuf[HALO:HALO + tm, :] = hcur
        y = (cw_ref[2:3, :] * hcur + cw_ref[1:2, :] * hbuf[pl.ds(HALO - 1, tm), :]
             + cw_ref[0:1, :] * hbuf[pl.ds(HALO - 2, tm), :])
        ycat_ref[:, D_A:D_A + D_B] = (proj_ref[:, o + D_B:o + 2 * D_B] * y).astype(BF16)
        hbuf[0:HALO, :] = hbuf[tm:tm + HALO, :]

        zc = proj_ref[:, o + 3 * D_B:D_IN]
        zbuf[HALO:HALO + tm, :] = zc
        mean, _ = _pool_means(zbuf[...], tm, i * tm)
        pooled = (mean - zc).astype(BF16)
        ycat_ref[:, D_A + D_B:D_MODEL] = (_dot(pooled, pw_ref[...]) * ps_ref[...]).astype(BF16)
        zbuf[0:HALO, :] = zbuf[tm:tm + HALO, :]

        x1_ref[...] = xv + _dot(ycat_ref[...], wout_ref[...])

    row = lambda w: pl.BlockSpec((tm, w), lambda i: (i, 0))
    return _tile_call(
        body, "mix_fwd", nt,
        [row(D_MODEL), _const_spec((1, D_MODEL)), _const_spec((D_MODEL, D_IN)), _const_spec((D_MODEL, D_MODEL)),
         _const_spec((3, CHUNK, 2 * CHUNK)), _const_spec((CHUNK, D_A)), _const_spec((1, D_A)), _const_spec((1, D_A)),
         _const_spec((D_A, D_A)), _const_spec((8, D_B)), _const_spec((D_C, D_C)), _const_spec((1, D_C))],
        [row(D_IN), row(D_MODEL), row(D_MODEL)],
        [jax.ShapeDtypeStruct((t, D_IN), F32), jax.ShapeDtypeStruct((t, D_MODEL), BF16),
         jax.ShapeDtypeStruct((t, D_MODEL), F32)],
        [pltpu.VMEM((tm + HALO, D_B), F32), pltpu.VMEM((tm + HALO, D_C), F32)],
        (x, g_mix, w_in, w_out, wcat, bmat, ln_g, ln_b, avg, conv_w, pool_bd, pool_scale), comm)


def ffn_ple_fwd(x1, p, g_ff, w_ff1, w_ff2, g_ple, w_gate, w_proj, *, tm, comm=None):
    t = x1.shape[0]
    nt = t // tm
    nc = D_FF // D_MODEL

    def body(x1_ref, p_ref, gff_ref, w1_ref, w2_ref, gple_ref, wg_ref, wp_ref, a_ref, x2_ref, x3_ref):
        x1v = x1_ref[...]
        n2 = x1v * lax.rsqrt(jnp.mean(x1v * x1v, axis=-1, keepdims=True) + RMS_EPS)
        h2 = (n2 * gff_ref[...]).astype(BF16)
        acc = x1v
        for c in range(nc):
            cols = slice(c * D_MODEL, (c + 1) * D_MODEL)
            a = _dot(h2, w1_ref[:, cols])
            a_ref[:, cols] = a.astype(BF16)
            ra = jnp.maximum(a, 0.0)
            acc = acc + _dot((ra * ra).astype(BF16), w2_ref[cols, :])
        x2_ref[...] = acc
        n3 = acc * lax.rsqrt(jnp.mean(acc * acc, axis=-1, keepdims=True) + RMS_EPS)
        h3 = (n3 * gple_ref[...]).astype(BF16)
        gate = jax.nn.sigmoid(_dot(h3, wg_ref[...]))
        pp = _dot(p_ref[...].astype(BF16), wp_ref[...])
        x3_ref[...] = acc + pp * gate

    row = lambda w: pl.BlockSpec((tm, w), lambda i: (i, 0))
    return _tile_call(
        body, "ffn_ple_fwd", nt,
        [row(D_MODEL), _layer_rows(p[1], tm), _const_spec((1, D_MODEL)), _const_spec((D_MODEL, D_FF)),
         _const_spec((D_FF, D_MODEL)), _const_spec((1, D_MODEL)), _const_spec((D_MODEL, D_MODEL)),
         _const_spec((D_PLE, D_MODEL))],
        [row(D_FF), row(D_MODEL), row(D_MODEL)],
        [jax.ShapeDtypeStruct((t, D_FF), BF16), jax.ShapeDtypeStruct((t, D_MODEL), F32),
         jax.ShapeDtypeStruct((t, D_MODEL), F32)],
        [], (x1, p[0], g_ff, w_ff1, w_ff2, g_ple, w_gate, w_proj), comm)


def loss_head(x, target, g, *, tm):
    t = x.shape[0]
    nt = t // tm

    def body(x_ref, t_ref, g_ref, loss_ref, dg_ref, dx_ref, sq_acc):
        i = pl.program_id(0)

        @pl.when(i == 0)
        def _():
            sq_acc[...] = jnp.zeros_like(sq_acc)
            dg_ref[...] = jnp.zeros_like(dg_ref)

        xv = x_ref[...]
        rs = lax.rsqrt(jnp.mean(xv * xv, axis=-1, keepdims=True) + RMS_EPS)
        n = xv * rs
        gv = g_ref[...]
        err = n * gv - t_ref[...]
        sq_acc[...] += jnp.sum(err * err, axis=0, keepdims=True)
        dy = err * (1.0 / D_MODEL)
        dg_ref[...] += jnp.sum(dy * n, axis=0, keepdims=True)
        dx_ref[...] = _rms_bwd(dy, n, rs, gv)

        @pl.when(i == nt - 1)
        def _():
            total = jnp.sum(sq_acc[...], axis=1, keepdims=True) * (0.5 / D_MODEL)
            loss_ref[...] = jnp.broadcast_to(total, loss_ref.shape)

    row = pl.BlockSpec((tm, D_MODEL), lambda i: (i, 0))
    return pl.pallas_call(
        body, name="loss_head", grid=(nt,),
        in_specs=[row, row, _const_spec((1, D_MODEL))],
        out_specs=[_acc_spec((8, 128)), _acc_spec((1, D_MODEL)), row],
        out_shape=[jax.ShapeDtypeStruct((8, 128), F32), jax.ShapeDtypeStruct((1, D_MODEL), F32),
                   jax.ShapeDtypeStruct((t, D_MODEL), F32)],
        scratch_shapes=[pltpu.VMEM((1, D_MODEL), F32)],
        compiler_params=_params("arbitrary"),
    )(x, target, g)


def ple_bwd(d, x2, p, g_ple, w_gate, w_proj, *, tm, comm=None):
    t = d.shape[0]
    nt = t // tm

    def body(d_ref, x2_ref, p_ref, g_ref, wg_ref, wp_ref, dx2_ref, dg_ref, dwg_ref, dwp_ref):
        i = pl.program_id(0)

        @pl.when(i == 0)
        def _():
            dg_ref[...] = jnp.zeros_like(dg_ref)
            dwg_ref[...] = jnp.zeros_like(dwg_ref)
            dwp_ref[...] = jnp.zeros_like(dwp_ref)

        dv = d_ref[...]
        x2v = x2_ref[...]
        rs = lax.rsqrt(jnp.mean(x2v * x2v, axis=-1, keepdims=True) + RMS_EPS)
        n3 = x2v * rs
        gv = g_ref[...]
        h3 = (n3 * gv).astype(BF16)
        gate = jax.nn.sigmoid(_dot(h3, wg_ref[...]))
        pb = p_ref[...].astype(BF16)
        pp = _dot(pb, wp_ref[...])
        dwp_ref[...] += _dot_tn(pb, (dv * gate).astype(BF16))
        dpre = (dv * pp * gate * (1.0 - gate)).astype(BF16)
        dwg_ref[...] += _dot_tn(h3, dpre)
        dh3 = _dot_nt(dpre, wg_ref[...])
        dg_ref[...] += jnp.sum(dh3 * n3, axis=0, keepdims=True)
        dx2_ref[...] = dv + _rms_bwd(dh3, n3, rs, gv)

    row = lambda w: pl.BlockSpec((tm, w), lambda i: (i, 0))
    return _tile_call(
        body, "ple_bwd", nt,
        [row(D_MODEL), row(D_MODEL), _layer_rows(p[1], tm), _const_spec((1, D_MODEL)), _const_spec((D_MODEL, D_MODEL)),
         _const_spec((D_PLE, D_MODEL))],
        [row(D_MODEL), _acc_spec((1, D_MODEL)), _acc_spec((D_MODEL, D_MODEL)), _acc_spec((D_PLE, D_MODEL))],
        [jax.ShapeDtypeStruct((t, D_MODEL), F32), jax.ShapeDtypeStruct((1, D_MODEL), F32),
         jax.ShapeDtypeStruct((D_MODEL, D_MODEL), F32), jax.ShapeDtypeStruct((D_PLE, D_MODEL), F32)],
        [], (d, x2, p[0], g_ple, w_gate, w_proj), comm)


def ffn_bwd(dx2, x1, a, g_ff, w_ff1, w_ff2, *, tm, comm=None):
    t = dx2.shape[0]
    nt = t // tm
    nc = D_FF // D_MODEL

    def body(dx2_ref, x1_ref, a_ref, g_ref, w1_ref, w2_ref, dx1_ref, h2_ref, da_ref, dg_ref):
        i = pl.program_id(0)

        @pl.when(i == 0)
        def _():
            dg_ref[...] = jnp.zeros_like(dg_ref)

        dv = dx2_ref[...]
        x1v = x1_ref[...]
        rs = lax.rsqrt(jnp.mean(x1v * x1v, axis=-1, keepdims=True) + RMS_EPS)
        n2 = x1v * rs
        gv = g_ref[...]
        h2_ref[...] = (n2 * gv).astype(BF16)
        dvb = dv.astype(BF16)
        dh2 = jnp.zeros((tm, D_MODEL), F32)
        for c in range(nc):
            cols = slice(c * D_MODEL, (c + 1) * D_MODEL)
            ra = jnp.maximum(a_ref[:, cols].astype(F32), 0.0)
            da = (_dot_nt(dvb, w2_ref[cols, :]) * (2.0 * ra)).astype(BF16)
            da_ref[:, cols] = da
            dh2 = dh2 + _dot_nt(da, w1_ref[:, cols])
        dg_ref[...] += jnp.sum(dh2 * n2, axis=0, keepdims=True)
        dx1_ref[...] = dv + _rms_bwd(dh2, n2, rs, gv)

    row = lambda w: pl.BlockSpec((tm, w), lambda i: (i, 0))
    return _tile_call(
        body, "ffn_bwd", nt,
        [row(D_MODEL), row(D_MODEL), row(D_FF), _const_spec((1, D_MODEL)), _const_spec((D_MODEL, D_FF)),
         _const_spec((D_FF, D_MODEL))],
        [row(D_MODEL), row(D_MODEL), row(D_FF), _acc_spec((1, D_MODEL))],
        [jax.ShapeDtypeStruct((t, D_MODEL), F32), jax.ShapeDtypeStruct((t, D_MODEL), BF16),
         jax.ShapeDtypeStruct((t, D_FF), BF16), jax.ShapeDtypeStruct((1, D_MODEL), F32)],
        [], (dx2, x1, a, g_ff, w_ff1, w_ff2), comm)


def mix_bwd(dx1, x, proj, g_mix, w_in, w_out, wcat, wcat_t, bmat, ln_g, ln_b, avg, conv_w, pool_bd, pool_bd_t,
            pool_scale, *, tm, comm=None):
    t = dx1.shape[0]
    nt = t // tm
    prev_blocks = tm // HALO

    def body(dx1_ref, x_ref, proj_ref, prev_ref, g_ref, win_ref, wout_ref, wcat_ref, wcatt_ref, bmat_ref, lng_ref,
             lnb_ref, avg_ref, cw_ref, pw_ref, pwt_ref, ps_ref,
             dx_ref, h1_ref, dproj_ref, dg_ref, dws_ref, dbm_ref, dlng_ref, dlnb_ref, dcw_ref, dpw_ref, dps_ref,
             dyc, dpj, hbuf, zbuf, dybuf, qbuf):
        i = pl.program_id(0)
        ti = nt - 1 - i

        @pl.when(i == 0)
        def _():
            for ref in (dg_ref, dws_ref, dbm_ref, dlng_ref, dlnb_ref, dcw_ref, dpw_ref, dps_ref):
                ref[...] = jnp.zeros_like(ref)
            dybuf[tm:tm + HALO, :] = jnp.zeros((HALO, D_B), F32)
            qbuf[tm:tm + HALO, :] = jnp.zeros((HALO, D_C), F32)

        dx1v = dx1_ref[...]
        dyc[...] = _dot_nt(dx1v.astype(BF16), wout_ref[...])

        lo_mask = _lane_lt((CHUNK, CHUNK), HEAD_DIM)
        avg = avg_ref[...]
        lng = lng_ref[...]
        gu, dgu = _gelu_and_grad(proj_ref[:, 0:D_A])
        gv, dgv = _gelu_and_grad(proj_ref[:, D_A:2 * D_A])
        cen = gv - _group_mean(gv, avg)
        rstd = lax.rsqrt(_group_mean(cen * cen, avg) + LN_EPS)
        vhat = cen * rstd
        vnb = (vhat * lng + lnb_ref[...]).astype(BF16)
        dya = dyc[:, 0:D_A]
        dm = dya * gu
        dmb = dm.astype(BF16)
        dvn_rows = []
        for c in range(tm // CHUNK):
            rows = slice(c * CHUNK, (c + 1) * CHUNK)
            dbm_ref[...] += dm[rows]
            dvn_parts = []
            for j in range(3):
                cols = slice(j * CHUNK, (j + 1) * CHUNK)
                vnb2 = vnb[rows, cols]
                mixed = _sgu_mix(vnb2, wcat_ref[j], lo_mask) + bmat_ref[:, cols]
                dpj[rows, cols] = dya[rows, cols] * mixed * dgu[rows, cols]
                dmb2 = dmb[rows, cols]
                zero = jnp.zeros_like(dmb2)
                dm_st = jnp.concatenate([jnp.where(lo_mask, dmb2, zero), jnp.where(lo_mask, zero, dmb2)], axis=0)
                dws_ref[j] += _dot_nt(dm_st, vnb2)
                dvn_st = _dot(wcatt_ref[j], dmb2)
                dvn_parts.append(jnp.where(lo_mask, dvn_st[0:CHUNK], dvn_st[CHUNK:2 * CHUNK]))
            dvn_rows.append(jnp.concatenate(dvn_parts, axis=1))
        dvn = jnp.concatenate(dvn_rows, axis=0)
        dlng_ref[...] += jnp.sum(dvn * vhat, axis=0, keepdims=True)
        dlnb_ref[...] += jnp.sum(dvn, axis=0, keepdims=True)
        dvh = dvn * lng
        dgv_in = rstd * (dvh - _group_mean(dvh, avg) - vhat * _group_mean(dvh * vhat, avg))
        dpj[:, D_A:2 * D_A] = dgv_in * dgv

        o = 2 * D_A
        live = (ti > 0).astype(F32)
        zb = proj_ref[:, o:o + D_B]
        gb = proj_ref[:, o + D_B:o + 2 * D_B]
        gc = proj_ref[:, o + 2 * D_B:o + 3 * D_B]
        hcur = gc * zb
        hbuf[0:HALO, :] = prev_ref[:, o + 2 * D_B:o + 3 * D_B] * prev_ref[:, o:o + D_B] * live
        hbuf[HALO:HALO + tm, :] = hcur
        hm1 = hbuf[pl.ds(HALO - 1, tm), :]
        hm2 = hbuf[pl.ds(HALO - 2, tm), :]
        y = cw_ref[2:3, :] * hcur + cw_ref[1:2, :] * hm1 + cw_ref[0:1, :] * hm2
        dout = dyc[:, D_A:D_A + D_B]
        dpj[:, o + D_B:o + 2 * D_B] = dout * y
        dy = dout * gb
        dcw_ref[2:3, :] += jnp.sum(dy * hcur, axis=0, keepdims=True)
        dcw_ref[1:2, :] += jnp.sum(dy * hm1, axis=0, keepdims=True)
        dcw_ref[0:1, :] += jnp.sum(dy * hm2, axis=0, keepdims=True)
        dybuf[0:tm, :] = dy
        dh = (cw_ref[2:3, :] * dy + cw_ref[1:2, :] * dybuf[pl.ds(1, tm), :] + cw_ref[0:1, :] * dybuf[pl.ds(2, tm), :])
        dybuf[tm:tm + HALO, :] = dybuf[0:HALO, :]
        dpj[:, o:o + D_B] = dh * gc
        dpj[:, o + 2 * D_B:o + 3 * D_B] = dh * zb

        zc = proj_ref[:, o + 3 * D_B:D_IN]
        zbuf[0:HALO, :] = prev_ref[:, o + 3 * D_B:D_IN] * live
        zbuf[HALO:HALO + tm, :] = zc
        mean, inv = _pool_means(zbuf[...], tm, ti * tm)
        pooled = (mean - zc).astype(BF16)
        dyp = dyc[:, D_A + D_B:D_MODEL]
        ps = ps_ref[...]
        dps_ref[...] += jnp.sum(dyp * _dot(pooled, pw_ref[...]), axis=0, keepdims=True)
        dpw = (dyp * ps).astype(BF16)
        dpw_ref[...] += _dot_tn(pooled, dpw)
        dpooled = _dot(dpw, pwt_ref[...])
        qbuf[0:tm, :] = dpooled * inv
        q = qbuf[...]
        nrows = tm + HALO
        f2 = q + pltpu.roll(q, nrows - 1, 0)
        f4 = f2 + pltpu.roll(f2, nrows - 2, 0)
        f8 = f4 + pltpu.roll(f4, nrows - 4, 0)
        f16 = f8 + pltpu.roll(f8, nrows - 8, 0)
        lane = lax.broadcasted_iota(jnp.int32, (tm, D_C), 1)
        ahead = jnp.where(lane < 64, f2[0:tm], jnp.where(lane < 128, f4[0:tm], jnp.where(lane < 192, f8[0:tm], f16[0:tm])))
        dpj[:, o + 3 * D_B:D_IN] = ahead - dpooled
        qbuf[tm:tm + HALO, :] = qbuf[0:HALO, :]

        dprojb = dpj[...].astype(BF16)
        dproj_ref[...] = dprojb
        dh1 = _dot_nt(dprojb, win_ref[...])
        xv = x_ref[...]
        rs = lax.rsqrt(jnp.mean(xv * xv, axis=-1, keepdims=True) + RMS_EPS)
        n1 = xv * rs
        gv1 = g_ref[...]
        h1_ref[...] = (n1 * gv1).astype(BF16)
        dg_ref[...] += jnp.sum(dh1 * n1, axis=0, keepdims=True)
        dx_ref[...] = dx1v + _rms_bwd(dh1, n1, rs, gv1)

        @pl.when(i == nt - 1)
        def _():
            tril = (lax.broadcasted_iota(jnp.int32, (2 * CHUNK, CHUNK), 0) % CHUNK
                    >= lax.broadcasted_iota(jnp.int32, (2 * CHUNK, CHUNK), 1))
            for j in range(3):
                dws_ref[j] = jnp.where(tril, dws_ref[j], 0.0)
            dbm_ref[...] = _group_mean_split(dbm_ref[...], avg) * float(HEAD_DIM)

    rev = lambda w: pl.BlockSpec((tm, w), lambda i: (nt - 1 - i, 0))
    prev = pl.BlockSpec((HALO, D_IN), lambda i: (jnp.maximum((nt - 1 - i) * prev_blocks - 1, 0), 0))
    acc_shapes = [(1, D_MODEL), (3, 2 * CHUNK, CHUNK), (CHUNK, D_A), (1, D_A), (1, D_A), (8, D_B), (D_C, D_C), (1, D_C)]
    return _tile_call(
        body, "mix_bwd", nt,
        [rev(D_MODEL), rev(D_MODEL), rev(D_IN), prev, _const_spec((1, D_MODEL)), _const_spec((D_MODEL, D_IN)),
         _const_spec((D_MODEL, D_MODEL)), _const_spec((3, CHUNK, 2 * CHUNK)), _const_spec((3, 2 * CHUNK, CHUNK)),
         _const_spec((CHUNK, D_A)), _const_spec((1, D_A)), _const_spec((1, D_A)), _const_spec((D_A, D_A)),
         _const_spec((8, D_B)), _const_spec((D_C, D_C)), _const_spec((D_C, D_C)), _const_spec((1, D_C))],
        [rev(D_MODEL), rev(D_MODEL), rev(D_IN)] + [_acc_spec(s) for s in acc_shapes],
        [jax.ShapeDtypeStruct((t, D_MODEL), F32), jax.ShapeDtypeStruct((t, D_MODEL), BF16),
         jax.ShapeDtypeStruct((t, D_IN), BF16)] + [jax.ShapeDtypeStruct(s, F32) for s in acc_shapes],
        [pltpu.VMEM((tm, D_MODEL), F32), pltpu.VMEM((tm, D_IN), F32),
         pltpu.VMEM((tm + HALO, D_B), F32), pltpu.VMEM((tm + HALO, D_C), F32),
         pltpu.VMEM((tm + HALO, D_B), F32), pltpu.VMEM((tm + HALO, D_C), F32)],
        (dx1, x, proj, proj, g_mix, w_in, w_out, wcat, wcat_t, bmat, ln_g, ln_b, avg, conv_w, pool_bd, pool_bd_t,
         pool_scale), comm)


def wgrad(a, b, *, tk, a_layer=None, relu_sq=False):
    t, m = a.shape[-2:]
    n = b.shape[1]
    bm = min(m, 1024)
    bn = 1024 if n % 1024 == 0 else n
    nk = t // tk
    if a_layer is None:
        a_spec = pl.BlockSpec((tk, bm), lambda i, j, k: (k, i))
    else:
        a_spec = pl.BlockSpec((None, None, tk, bm), lambda i, j, k: (a_layer, 0, k, i))

    def body(a_ref, b_ref, o_ref):
        k = pl.program_id(2)

        @pl.when(k == 0)
        def _():
            o_ref[...] = jnp.zeros_like(o_ref)

        av = a_ref[...]
        if relu_sq:
            ra = jnp.maximum(av.astype(F32), 0.0)
            av = ra * ra
        o_ref[...] += _dot_tn(av.astype(BF16), b_ref[...].astype(BF16))

    return pl.pallas_call(
        body, name=f"wgrad_{m}x{n}" + ("_relu_sq" if relu_sq else ""), grid=(m // bm, n // bn, nk),
        in_specs=[a_spec, pl.BlockSpec((tk, bn), lambda i, j, k: (k, j))],
        out_specs=pl.BlockSpec((bm, bn), lambda i, j, k: (i, j)),
        out_shape=jax.ShapeDtypeStruct((m, n), F32),
        compiler_params=_params("parallel", "parallel", "arbitrary"),
    )(a, b)


def _row_block(rows, cols, target_bytes):
    target = max(8, target_bytes // (4 * cols))
    if rows <= target:
        return rows
    best = None
    for br in range(8, target + 1, 8):
        if rows % br == 0:
            best = br
    return best if best is not None else rows


def adamw(w, g, m, v):
    shape = w.shape
    cols = shape[-1]
    rows = math.prod(shape[:-1]) if len(shape) > 1 else 1
    br = _row_block(rows, cols, 1 << 20)

    def body(w_ref, g_ref, m_ref, v_ref, d_ref, nm_ref, nv_ref):
        gv = g_ref[...]
        nm = ADAM_B1 * m_ref[...] + (1.0 - ADAM_B1) * gv
        nv = ADAM_B2 * v_ref[...] + (1.0 - ADAM_B2) * jnp.square(gv)
        m_hat = nm / (1.0 - ADAM_B1 ** ADAM_STEP)
        v_hat = nv / (1.0 - ADAM_B2 ** ADAM_STEP)
        d_ref[...] = -ADAM_LR * (m_hat / (jnp.sqrt(v_hat) + ADAM_EPS) + ADAM_WD * w_ref[...])
        nm_ref[...] = nm
        nv_ref[...] = nv

    spec = pl.BlockSpec((br, cols), lambda i: (i, 0))
    outs = pl.pallas_call(
        body, name="adamw", grid=(rows // br,),
        in_specs=[spec] * 4, out_specs=[spec] * 3,
        out_shape=[jax.ShapeDtypeStruct((rows, cols), F32)] * 3,
        compiler_params=pltpu.CompilerParams(dimension_semantics=("parallel",)),
    )(*(a.reshape(rows, cols) for a in (w, g, m, v)))
    return tuple(o.reshape(shape) for o in outs)


ADD_STEPS = 4


def add_halves(geoms, arrs, received, c_idx, dtypes):
    n = len(arrs)

    def body(c_ref, *refs):
        del c_ref
        for a in range(n):
            refs[2 * n + a][...] = (refs[a][...] + refs[n + a][...]).astype(dtypes[a])

    own_specs, half_specs = [], []
    for g in geoms:
        if g.kind == "cols":
            rows, cols = g.half_shape[0] // ADD_STEPS, g.half_shape[1]
            own_specs.append(pl.BlockSpec((rows, cols), lambda i, c_ref: (ADD_STEPS * c_ref[0] + i, 0)))
            half_specs.append(pl.BlockSpec((rows, cols), lambda i, c_ref: (i, 0)))
        else:
            _, h, cols = g.half_shape
            own_specs.append(pl.BlockSpec((None, None, h, cols), lambda i, c_ref: (i, c_ref[0], 0, 0)))
            half_specs.append(pl.BlockSpec((None, h, cols), lambda i, c_ref: (i, 0, 0)))
    return pl.pallas_call(
        body, name="add_halves",
        grid_spec=pltpu.PrefetchScalarGridSpec(num_scalar_prefetch=1, grid=(ADD_STEPS,),
                                               in_specs=own_specs + half_specs, out_specs=half_specs),
        out_shape=[jax.ShapeDtypeStruct(g.half_shape, dt) for g, dt in zip(geoms, dtypes)],
        compiler_params=_params("parallel"),
    )(c_idx, *arrs, *received)


def add_parts(geoms, landed, finals, layer, c_idx):
    n = len(landed)

    def body(c_ref, *refs):
        del c_ref
        for a in range(n):
            p_ref = refs[a]
            parts = [p_ref[j].astype(F32) for j in range(N_CHIPS)]
            refs[2 * n + a][...] = ((parts[0] + parts[1]) + parts[2]) + parts[3]

    in_specs, out_specs = [], []
    for g in geoms:
        rows, cols = g.part_shape[0] // ADD_STEPS, g.part_shape[1]
        in_specs.append(pl.BlockSpec((N_CHIPS, rows, cols), lambda i, c_ref: (0, i, 0)))
        out_specs.append(pl.BlockSpec((None, rows, cols), lambda i, c_ref: (layer, ADD_STEPS * c_ref[0] + i, 0)))
    return pl.pallas_call(
        body, name="add_parts",
        grid_spec=pltpu.PrefetchScalarGridSpec(num_scalar_prefetch=1, grid=(ADD_STEPS,),
                                               in_specs=in_specs + [_ANY] * n, out_specs=out_specs),
        out_shape=[jax.ShapeDtypeStruct(f.shape, F32) for f in finals],
        input_output_aliases={1 + n + a: a for a in range(n)},
        compiler_params=_params("parallel"),
    )(c_idx, *landed, *finals)


def _shard_dims(k, n, axis):
    return (k // N_CHIPS, n) if axis == 0 else (k, n // N_CHIPS)


W_IN_STRIDE = 512
W_IN_WINDOW = 640


def _big_geoms():
    geoms = []
    for name, k, n, axis in BIG:
        if axis == 0:
            geoms.append(_Geom("rows", (N_CHIPS, 2, k // N_CHIPS // 2, n)))
        elif name == "w_in":
            geoms.append(_Geom("cols", (k, n), W_IN_STRIDE, W_IN_WINDOW))
        else:
            geoms.append(_Geom("cols", (k, n), n // N_CHIPS, n // N_CHIPS))
    return geoms


def _grad_views(gb, geoms):
    return [gb[name].reshape(g.shape) for (name, _, _, _), g in zip(BIG, geoms)]


def _round_up(v, m):
    return (v + m - 1) // m * m


def _prep_small(small):
    tril = jnp.tril(jnp.ones((CHUNK, CHUNK), bool))
    wm = jnp.where(tril, small["sgu_w"], 0.0).astype(BF16).reshape(DEPTH, 3, 2, CHUNK, CHUNK)
    head = jnp.arange(D_A) // HEAD_DIM
    grp = jnp.arange(D_C) // HEAD_DIM
    pw_rows = small["pool_w"].reshape(DEPTH, D_C, HEAD_DIM)
    pool_bd = jnp.where((grp[:, None] == grp[None, :])[None], jnp.tile(pw_rows, (1, 1, D_C // HEAD_DIM)), 0.0).astype(BF16)
    return dict(
        wcat=wm.transpose(0, 1, 3, 2, 4).reshape(DEPTH, 3, CHUNK, 2 * CHUNK),
        wcat_t=wm.transpose(0, 1, 2, 4, 3).reshape(DEPTH, 3, 2 * CHUNK, CHUNK),
        bmat=jnp.repeat(jnp.swapaxes(small["sgu_b"], 1, 2), HEAD_DIM, axis=2),
        avg=jnp.where(head[:, None] == head[None, :], 1.0 / HEAD_DIM, 0.0).astype(BF16),
        pool_bd=pool_bd, pool_bd_t=jnp.swapaxes(pool_bd, 1, 2),
        conv8=jnp.pad(small["conv_w"], ((0, 0), (0, 8 - 3), (0, 0))),
    )


def _row(a):
    return a.reshape(1, -1)


MIX_WEIGHTS = ("w_in", "w_out")
MLP_WEIGHTS = ("w_ff1", "w_ff2", "w_ple_gate", "w_ple_proj")
ALL_BIG = MIX_WEIGHTS + MLP_WEIGHTS
EARLY_GRADS = MLP_WEIGHTS + ("w_out",)
FFN_BWD_TILE = 512
PLE_BWD_TILE = 1024


def _fwd_layer(h, p, wl, small, prep, l, tm, comm_mix=None, comm_mlp=None):
    (proj, ycat, x1), got = mix_fwd(h, _row(small["norm_mix_g"][l]), wl["w_in"], wl["w_out"], prep["wcat"][l],
                                    prep["bmat"][l], _row(small["sgu_ln_g"][l]), _row(small["sgu_ln_b"][l]), prep["avg"],
                                    prep["conv8"][l], prep["pool_bd"][l], _row(small["pool_scale"][l]),
                                    tm=min(2 * tm, h.shape[0]), comm=comm_mix)
    if comm_mix is not None:
        wl = {**wl, **_weights_of(got, MLP_WEIGHTS)}
    (a, x2, x3), couts = ffn_ple_fwd(x1, (p, l), _row(small["norm_ff_g"][l]), wl["w_ff1"], wl["w_ff2"],
                                     _row(small["norm_ple_g"][l]), wl["w_ple_gate"], wl["w_ple_proj"], tm=tm,
                                     comm=comm_mlp)
    return (h, proj, ycat, x1, a, x2), x3, couts, wl


def _merge_comms(comms):
    comms = [cm for cm in comms if cm is not None]
    if len(comms) <= 1:
        return comms[0] if comms else None
    spans, ni, no, ns = [], 0, 0, 0
    for cm in comms:
        spans.append((ni, no, ns))
        ni, no, ns = ni + len(cm.ins), no + len(cm.out_shapes), ns + len(cm.sems)

    def copies(in_refs, out_refs, sem_refs):
        local, sends, recvs = [], [], []
        for cm, (i0, o0, s0) in zip(comms, spans):
            got = cm.copies(in_refs[i0:i0 + len(cm.ins)], out_refs[o0:o0 + len(cm.out_shapes)],
                            sem_refs[s0:s0 + len(cm.sems)])
            local, sends, recvs = local + got[0], sends + got[1], recvs + got[2]
        return local, sends, recvs

    aliases = {i0 + i: o0 + o for cm, (i0, o0, _) in zip(comms, spans) for i, o in cm.aliases.items()}
    return _Comm(sum((cm.ins for cm in comms), []), sum((cm.out_shapes for cm in comms), []),
                 sum((cm.sems for cm in comms), []), copies, aliases)


def _split_results(results, comms):
    out, at = [], 0
    for cm in comms:
        if cm is None:
            out.append(None)
        else:
            out.append(results[at:at + len(cm.out_shapes)])
            at += len(cm.out_shapes)
    return out


class _Reduction:
    def __init__(self, layer, names, geoms, arrs, finals, c_arr, narrow=()):
        self.layer, self.names, self.geoms, self.arrs = layer, list(names), list(geoms), list(arrs)
        self.finals, self.c_arr = finals, c_arr
        self.dtypes = [BF16 if n in narrow else F32 for n in self.names]

    def comm_a(self):
        return _reduce_a_comm(self.geoms, self.arrs)

    def comm_b(self, received):
        return _reduce_b_comm(self.geoms, add_halves(self.geoms, self.arrs, received, self.c_arr, self.dtypes))

    def comm_c(self, landed):
        mine = add_parts(self.geoms, landed, [self.finals[n] for n in self.names], self.layer, self.c_arr)
        return _reduce_c_comm(self.geoms, mine, self.layer)

    def done(self, results):
        self.finals.update(zip(self.names, results))


class _Plan:
    def ple(self):
        return None

    def after_ple(self, results):
        pass

    def ffn(self):
        return None

    def after_ffn(self, results):
        pass

    def before_mix(self, gb):
        pass

    def mix(self):
        return None

    def after_mix(self, results):
        pass


class _CarryPlan(_Plan):
    def __init__(self, above):
        self.above = above

    def ple(self):
        return self.above.comm_a()

    def after_ple(self, results):
        self.received = results

    def ffn(self):
        return self.above.comm_b(self.received)

    def after_ffn(self, results):
        self.landed = results

    def mix(self):
        return self.above.comm_c(self.landed)

    def after_mix(self, results):
        self.above.done(results)


class _LastPlan(_CarryPlan):
    def __init__(self, above, make_early):
        super().__init__(above)
        self.make_early = make_early

    def before_mix(self, gb):
        self.early = self.make_early(gb)
        self.early_received = _run_comm(self.early.comm_a(), "reduce_a_early")

    def mix(self):
        self.parts = [self.above.comm_c(self.landed), self.early.comm_b(self.early_received)]
        return _merge_comms(self.parts)

    def after_mix(self, results):
        above_res, self.early_landed = _split_results(results, self.parts)
        self.above.done(above_res)


def _bwd_layer(d, saved, p, wl, small, prep, l, tm, tk, plan=None):
    plan = plan or _Plan()
    xin, proj, ycat, x1, a, x2 = saved
    (dx2, dg_ple, dw_gate, dw_proj), res = ple_bwd(d, x2, (p, l), _row(small["norm_ple_g"][l]), wl["w_ple_gate"],
                                                   wl["w_ple_proj"], tm=min(PLE_BWD_TILE, d.shape[0]), comm=plan.ple())
    plan.after_ple(res)
    gb = {"w_ple_gate": dw_gate, "w_ple_proj": dw_proj}
    (dx1, h2, da, dg_ff), res = ffn_bwd(dx2, x1, a, _row(small["norm_ff_g"][l]), wl["w_ff1"], wl["w_ff2"],
                                        tm=FFN_BWD_TILE if tm >= FFN_BWD_TILE else tm, comm=plan.ffn())
    plan.after_ffn(res)
    gb["w_ff2"] = wgrad(a, dx2, tk=tk, relu_sq=True)
    gb["w_ff1"] = wgrad(h2, da, tk=tk)
    gb["w_out"] = wgrad(ycat, dx1, tk=tk)
    plan.before_mix(gb)
    (dprev, h1, dproj, dg_mix, dws, dbm, dlng, dlnb, dcw, dpw, dps), res = mix_bwd(
        dx1, xin, proj, _row(small["norm_mix_g"][l]), wl["w_in"], wl["w_out"], prep["wcat"][l], prep["wcat_t"][l],
        prep["bmat"][l], _row(small["sgu_ln_g"][l]), _row(small["sgu_ln_b"][l]), prep["avg"], prep["conv8"][l],
        prep["pool_bd"][l], prep["pool_bd_t"][l], _row(small["pool_scale"][l]), tm=tm, comm=plan.mix())
    plan.after_mix(res)
    gb["w_in"] = wgrad(h1, dproj, tk=tk)
    gs = {
        "norm_ple_g": dg_ple[0], "norm_ff_g": dg_ff[0], "norm_mix_g": dg_mix[0],
        "sgu_w": dws.reshape(2 * 3, CHUNK, CHUNK), "sgu_b": dbm[:, ::HEAD_DIM].T,
        "sgu_ln_g": dlng[0], "sgu_ln_b": dlnb[0], "conv_w": dcw[0:3], "pool_scale": dps[0],
        "pool_w": jnp.stack([dpw[g * HEAD_DIM:(g + 1) * HEAD_DIM, g * HEAD_DIM:(g + 1) * HEAD_DIM]
                             for g in range(D_C // HEAD_DIM)]),
    }
    return dprev, gb, gs


def _local_step(x, p, target, full, small, *, tm, tk):
    prep = _prep_small(small)
    p = p[:, None]
    saved, h = [], x
    for l in range(DEPTH):
        wl = {name: full[name][l] for name in full}
        s, h, _, _ = _fwd_layer(h, p, wl, small, prep, l, tm)
        saved.append(s)
    loss_blk, d_final_g, d = loss_head(h, target, _row(small["final_g"]), tm=tm)
    gbig, gsm = [None] * DEPTH, [None] * DEPTH
    for l in reversed(range(DEPTH)):
        wl = {name: full[name][l] for name in full}
        d, gbig[l], gsm[l] = _bwd_layer(d, saved[l], p, wl, small, prep, l, tm, tk)
    big = {name: jnp.stack([gbig[l][name] for l in range(DEPTH)]) for name in gbig[0]}
    sm = {name: jnp.stack([gsm[l][name] for l in range(DEPTH)]) for name in gsm[0]}
    sm["final_g"] = d_final_g[0]
    return loss_blk[0, 0], d, big, sm


def _weights_of(gathered, names):
    wl = dict(zip([b[0] for b in BIG if b[0] in names], gathered))
    if "w_in" in wl:
        wl["w_in"] = wl["w_in"].transpose(1, 0, 2).reshape(D_MODEL, D_IN)
    return wl


def kernel(x, p, norm_mix_g, w_in, sgu_w, sgu_b, sgu_ln_g, sgu_ln_b, conv_w, pool_w, pool_scale, w_out, norm_ff_g, w_ff1, w_ff2, norm_ple_g, w_ple_gate, w_ple_proj, final_g, loss_target, m_norm_mix_g, m_w_in, m_sgu_w, m_sgu_b, m_sgu_ln_g, m_sgu_ln_b, m_conv_w, m_pool_w, m_pool_scale, m_w_out, m_norm_ff_g, m_w_ff1, m_w_ff2, m_norm_ple_g, m_w_ple_gate, m_w_ple_proj, m_final_g, v_norm_mix_g, v_w_in, v_sgu_w, v_sgu_b, v_sgu_ln_g, v_sgu_ln_b, v_conv_w, v_pool_w, v_pool_scale, v_w_out, v_norm_ff_g, v_w_ff1, v_w_ff2, v_norm_ple_g, v_w_ple_gate, v_w_ple_proj, v_final_g):
    args = dict(locals())
    w = {name: args[name] for name in WEIGHTS}
    m = {name: args["m_" + name] for name in WEIGHTS}
    v = {name: args["v_" + name] for name in WEIGHTS}
    t = x.shape[1]
    tm = min(512, t)
    tk = min(2048, t)
    x_idx, y_idx, c_idx = _place()
    chip = 2 * x_idx + y_idx
    c_arr = c_idx.reshape(1).astype(jnp.int32)
    xs, target = x[0], loss_target[0]

    shards = {name: w[name].astype(BF16) for name, _, _, _ in BIG}
    conv_rows = _round_up(CONV_SHARD, 8 * 128) // 128
    conv_flat = jnp.pad(w["conv_w"].reshape(-1), (0, conv_rows * 128 - CONV_SHARD)).reshape(conv_rows, 128)
    first = _run_comm(_gather_comm(shards, 0, MIX_WEIGHTS, conv_flat), "gather_first")
    conv_full = (first[len(MIX_WEIGHTS)].reshape(N_CHIPS, -1)[:, :CONV_SHARD]
                 .reshape(N_CHIPS, DEPTH, 3, D_B // N_CHIPS).transpose(1, 2, 0, 3).reshape(DEPTH, 3, D_B))
    small = {name: w[name] for name in SMALL}
    small["conv_w"] = conv_full
    prep = _prep_small(small)

    wl = [None] * DEPTH
    wl[0] = _weights_of(first, MIX_WEIGHTS)
    saved, h = [], xs
    for l in range(DEPTH):
        comm_mix = _gather_comm(shards, 0, MLP_WEIGHTS) if l == 0 else None
        comm_mlp = _gather_comm(shards, l + 1, ALL_BIG) if l + 1 < DEPTH else None
        s, h, got, wl[l] = _fwd_layer(h, p, wl[l], small, prep, l, tm, comm_mix, comm_mlp)
        saved.append(s)
        if comm_mlp is not None:
            wl[l + 1] = _weights_of(got, ALL_BIG)

    loss_blk, d_final_g, d = loss_head(h, target, _row(small["final_g"]), tm=tm)

    geoms = dict(zip([b[0] for b in BIG], _big_geoms()))
    finals = {name: jnp.zeros((DEPTH,) + g.final_shape, F32) for name, g in geoms.items()}

    def reduction(layer, names, gb, narrow=()):
        return _Reduction(layer, names, [geoms[n] for n in names], [gb[n].reshape(geoms[n].shape) for n in names],
                          finals, c_arr, narrow)

    gsm = [None] * DEPTH
    above = None
    for l in reversed(range(DEPTH)):
        if above is None:
            plan = _Plan()
        elif l > 0:
            plan = _CarryPlan(above)
        else:
            plan = _LastPlan(above, lambda gb: reduction(0, EARLY_GRADS, gb))
        d, gb, gsm[l] = _bwd_layer(d, saved[l], p, wl[l], small, prep, l, tm, tk, plan)
        if l > 0:
            above = reduction(l, ALL_BIG, gb)

    sm = {name: jnp.stack([gsm[i][name] for i in range(DEPTH)]) for name in gsm[0]}
    sm["final_g"] = d_final_g[0]
    sizes = [sm[name].size for name in SMALL]
    small_rows = _round_up(-(-sum(sizes) // (2 * N_CHIPS * LANES)), 8 * ADD_STEPS)
    small_flat = jnp.pad(jnp.concatenate([sm[name].reshape(-1) for name in SMALL]),
                         (0, 2 * N_CHIPS * small_rows * LANES - sum(sizes)))
    geoms["small"] = _Geom("rows", (N_CHIPS, 2, small_rows, LANES))
    finals["small"] = jnp.zeros((1,) + geoms["small"].final_shape, F32)
    late = reduction(0, ("w_in", "small"), {**gb, "small": small_flat}, narrow=("w_in",))
    late_landed = _run_comm(late.comm_b(_run_comm(late.comm_a(), "reduce_a_late")), "reduce_b_late")
    last = [plan.early.comm_c(plan.early_landed), late.comm_c(late_landed)]
    early_res, late_res = _split_results(_run_comm(_merge_comms(last), "reduce_c_last"), last)
    plan.early.done(early_res)
    late.done(late_res)

    grads = {name: finals[name] for name, _, _, _ in BIG}
    grads["w_in"] = lax.dynamic_slice_in_dim(grads["w_in"], chip * (D_IN // N_CHIPS - W_IN_STRIDE), D_IN // N_CHIPS, axis=2)
    small_red = _run_comm(_allgather_comm(finals["small"][0]), "small_allgather")[0].reshape(-1)
    off = 0
    for name, size in zip(SMALL, sizes):
        grads[name] = small_red[off:off + size].reshape(sm[name].shape)
        off += size
    grads["conv_w"] = lax.dynamic_slice_in_dim(grads["conv_w"], chip * (D_B // N_CHIPS), D_B // N_CHIPS, axis=2)

    loss = lax.psum(loss_blk[0, 0], ("x", "y", "c"))
    delta, new_m, new_v = {}, {}, {}
    for name in WEIGHTS:
        delta[name], new_m[name], new_v[name] = adamw(w[name], grads[name], m[name], v[name])
    return (loss, d[None], *[grads[n] for n in WEIGHTS], *[delta[n] for n in WEIGHTS],
            *[new_m[n] for n in WEIGHTS], *[new_v[n] for n in WEIGHTS])
```

```python
import math

import jax
import jax.numpy as jnp
from jax import lax
from jax.experimental import pallas as pl
from jax.experimental.pallas import tpu as pltpu

F32 = jnp.float32
BF16 = jnp.bfloat16

D_MODEL = 1024
DEPTH = 4
D_PLE = 256
D_FF = 4096
HEAD_DIM = 64
D_A = 384
D_B = 384
D_C = 256
D_IN = 2176
CHUNK = 128
HALO = 16
RMS_EPS = 1e-6
LN_EPS = 1e-5
N_CHIPS = 4
LANES = 1024

ADAM_LR = 0.001
ADAM_B1 = 0.9
ADAM_B2 = 0.999
ADAM_EPS = 1e-08
ADAM_WD = 0.01
ADAM_STEP = 10

VMEM_LIMIT_BYTES = 60 * 1024 * 1024

_RSQRT2 = 0.7071067811865476
_INV_SQRT_2PI = 0.3989422804014327

BIG = (
    ("w_in", D_MODEL, D_IN, 1),
    ("w_out", D_MODEL, D_MODEL, 0),
    ("w_ff1", D_MODEL, D_FF, 1),
    ("w_ff2", D_FF, D_MODEL, 0),
    ("w_ple_gate", D_MODEL, D_MODEL, 0),
    ("w_ple_proj", D_PLE, D_MODEL, 1),
)
SMALL = ("norm_mix_g", "sgu_w", "sgu_b", "sgu_ln_g", "sgu_ln_b", "conv_w", "pool_w", "pool_scale",
         "norm_ff_g", "norm_ple_g", "final_g")
WEIGHTS = ("norm_mix_g", "w_in", "sgu_w", "sgu_b", "sgu_ln_g", "sgu_ln_b", "conv_w", "pool_w", "pool_scale",
           "w_out", "norm_ff_g", "w_ff1", "w_ff2", "norm_ple_g", "w_ple_gate", "w_ple_proj", "final_g")
CONV_SHARD = DEPTH * 3 * (D_B // N_CHIPS)


def _dot(a, b):
    return jnp.dot(a, b, preferred_element_type=F32)


def _dot_nt(a, b):
    return lax.dot_general(a, b, (((1,), (1,)), ((), ())), preferred_element_type=F32)


def _dot_tn(a, b):
    return lax.dot_general(a, b, (((0,), (0,)), ((), ())), preferred_element_type=F32)


def _const_spec(shape):
    nd = len(shape)
    return pl.BlockSpec(shape, lambda i: (0,) * nd, pipeline_mode=pl.Buffered(1))


def _acc_spec(shape):
    nd = len(shape)
    return pl.BlockSpec(shape, lambda i: (0,) * nd)


def _layer_rows(layer, tm):
    return pl.BlockSpec((None, None, tm, D_PLE), lambda i: (layer, 0, i, 0))


def _params(*sem):
    return pltpu.CompilerParams(dimension_semantics=sem, vmem_limit_bytes=VMEM_LIMIT_BYTES)


def _rms_bwd(dh, n, rs, g):
    dn = dh * g
    return rs * (dn - n * jnp.mean(dn * n, axis=-1, keepdims=True))


def _gelu(x):
    return x * (0.5 * (1.0 + lax.erf(x * _RSQRT2)))


def _gelu_and_grad(x):
    cdf = 0.5 * (1.0 + lax.erf(x * _RSQRT2))
    return x * cdf, cdf + x * (jnp.exp(-0.5 * x * x) * _INV_SQRT_2PI)


def _group_mean(v, avg):
    vb = v.astype(BF16)
    split = 2 * CHUNK
    return jnp.concatenate([_dot(vb[:, :split], avg[:split, :split]), _dot(vb[:, split:], avg[split:, split:])], axis=1)


def _group_mean_split(v, avg):
    hi = v.astype(BF16)
    lo = (v - hi.astype(F32)).astype(BF16)
    return _dot(hi, avg) + _dot(lo, avg)


def _lane_lt(shape, bound):
    return lax.broadcasted_iota(jnp.int32, shape, 1) < bound


def _sgu_mix(vnb2, wcat_j, lo_mask):
    zero = jnp.zeros_like(vnb2)
    stacked = jnp.concatenate([jnp.where(lo_mask, vnb2, zero), jnp.where(lo_mask, zero, vnb2)], axis=0)
    return _dot(wcat_j, stacked)


def _pool_means(ext, tile_rows, first_pos):
    s2 = ext + pltpu.roll(ext, 1, 0)
    s4 = s2 + pltpu.roll(s2, 2, 0)
    s8 = s4 + pltpu.roll(s4, 4, 0)
    s16 = s8 + pltpu.roll(s8, 8, 0)
    pos = (first_pos + lax.broadcasted_iota(jnp.int32, (tile_rows, 1), 0) + 1).astype(F32)
    lane = lax.broadcasted_iota(jnp.int32, (tile_rows, D_C), 1)
    sums = jnp.where(lane < 64, s2[HALO:], jnp.where(lane < 128, s4[HALO:], jnp.where(lane < 192, s8[HALO:], s16[HALO:])))
    win = jnp.where(lane < 64, 2.0, jnp.where(lane < 128, 4.0, jnp.where(lane < 192, 8.0, 16.0)))
    inv = 1.0 / jnp.minimum(pos, win)
    return sums * inv, inv


MESH = pl.DeviceIdType.MESH
_ANY = pl.BlockSpec(memory_space=pl.ANY)


def _place():
    return lax.axis_index("x"), lax.axis_index("y"), lax.axis_index("c")


def _chip_peers(x, y):
    return [(1 - x, y), (x, 1 - y), (1 - x, 1 - y)]


class _Comm:
    def __init__(self, ins, out_shapes, sems, copies, aliases=None, forwards=None):
        self.ins, self.out_shapes, self.sems, self.copies = list(ins), list(out_shapes), list(sems), copies
        self.aliases = dict(aliases or {})
        self.forwards = forwards

    def start(self, in_refs, out_refs, sem_refs):
        local, sends, _ = self.copies(in_refs, out_refs, sem_refs)
        for cp in local + sends:
            cp.start()

    def wait(self, in_refs, out_refs, sem_refs):
        local, sends, recvs = self.copies(in_refs, out_refs, sem_refs)
        for cp in recvs:
            cp.wait_recv()
        passed, passed_in = self.forwards(in_refs, out_refs, sem_refs) if self.forwards else ([], [])
        for cp in passed:
            cp.start()
        for cp in sends:
            cp.wait_send()
        for cp in local:
            cp.wait()
        for cp in passed_in:
            cp.wait_recv()
        for cp in passed:
            cp.wait_send()


def _remote(src, dst, send_sem, recv_sem, device):
    return pltpu.make_async_remote_copy(src_ref=src, dst_ref=dst, send_sem=send_sem, recv_sem=recv_sem,
                                        device_id=device, device_id_type=MESH)


def _gather_comm(shards, layer, names, conv=None):
    mats = [b for b in BIG if b[0] in names]
    ins = [shards[name] for name, _, _, _ in mats] + ([conv] if conv is not None else [])
    out_shapes = []
    for name, k, n, axis in mats:
        shape = (N_CHIPS, k, n // N_CHIPS) if name == "w_in" else (k, n)
        out_shapes.append(jax.ShapeDtypeStruct(shape, BF16))
    if conv is not None:
        out_shapes.append(jax.ShapeDtypeStruct((N_CHIPS,) + conv.shape, conv.dtype))
    n_arr = len(ins)

    def block(a, out_ref, chip):
        if a == len(mats) or mats[a][0] == "w_in":
            return out_ref.at[chip]
        _, k, n, axis = mats[a]
        if axis == 0:
            return out_ref.at[pl.ds(chip * (k // N_CHIPS), k // N_CHIPS), :]
        return out_ref.at[:, pl.ds(chip * (n // N_CHIPS), n // N_CHIPS)]

    def copies(in_refs, out_refs, sem_refs):
        send_sems, recv_sems, local_sems = sem_refs
        x, y, c = _place()
        me = 2 * x + y
        local, sends, recvs = [], [], []
        for a in range(n_arr):
            src = in_refs[a].at[layer] if a < len(mats) else in_refs[a]
            local.append(pltpu.make_async_copy(src, block(a, out_refs[a], me), local_sems.at[a]))
            for j, (px, py) in enumerate(_chip_peers(x, y)):
                sends.append(_remote(src, block(a, out_refs[a], me), send_sems.at[a, j], recv_sems.at[a, j], (px, py, c)))
                recvs.append(_remote(src, block(a, out_refs[a], 2 * px + py), send_sems.at[a, j], recv_sems.at[a, j],
                                     (px, py, c)))
        return local, sends, recvs

    sems = [pltpu.SemaphoreType.DMA((n_arr, 3)), pltpu.SemaphoreType.DMA((n_arr, 3)), pltpu.SemaphoreType.DMA((n_arr,))]
    return _Comm(ins, out_shapes, sems, copies)


def _place_own(shards, layer, names, chip, conv=None):
    placed = []
    for name, k, n, axis in (b for b in BIG if b[0] in names):
        shard = shards[name][layer]
        if name == "w_in":
            placed.append(lax.dynamic_update_slice(jnp.zeros((N_CHIPS, k, n // N_CHIPS), BF16), shard[None], (chip, 0, 0)))
        elif axis == 0:
            placed.append(lax.dynamic_update_slice(jnp.zeros((k, n), BF16), shard, (chip * (k // N_CHIPS), 0)))
        else:
            placed.append(lax.dynamic_update_slice(jnp.zeros((k, n), BF16), shard, (0, chip * (n // N_CHIPS))))
    if conv is not None:
        placed.append(lax.dynamic_update_slice(jnp.zeros((N_CHIPS,) + conv.shape, conv.dtype), conv[None], (chip, 0, 0)))
    return placed


def _gather_halved_comm(shards, layer, names, placed, conv=None):
    mats = [b for b in BIG if b[0] in names]
    ins = [shards[name] for name, _, _, _ in mats] + ([conv] if conv is not None else [])
    n_arr = len(ins)

    def src_half(a, in_ref, core):
        if a == len(mats):
            rows = conv.shape[0] // 2
            return in_ref.at[pl.ds(core * rows, rows), :]
        rows = shards[mats[a][0]].shape[1] // 2
        return in_ref.at[layer, pl.ds(core * rows, rows), :]

    def half(a, out_ref, chip, core):
        if a == len(mats):
            rows = conv.shape[0] // 2
            return out_ref.at[chip, pl.ds(core * rows, rows), :]
        name, k, n, axis = mats[a]
        if name == "w_in":
            return out_ref.at[chip, pl.ds(core * (k // 2), k // 2), :]
        if axis == 0:
            rows = k // N_CHIPS // 2
            return out_ref.at[pl.ds(chip * 2 * rows + core * rows, rows), :]
        return out_ref.at[pl.ds(core * (k // 2), k // 2), pl.ds(chip * (n // N_CHIPS), n // N_CHIPS)]

    def copies(in_refs, out_refs, sem_refs):
        send_sems, recv_sems = sem_refs[0], sem_refs[1]
        x, y, c = _place()
        me = 2 * x + y
        sends, recvs = [], []
        for a in range(n_arr):
            for j, (px, py) in enumerate(_chip_peers(x, y)):
                sends.append(_remote(src_half(a, in_refs[a], c), half(a, out_refs[a], me, c), send_sems.at[a, j],
                                     recv_sems.at[a, j], (px, py, c)))
                recvs.append(_remote(src_half(a, in_refs[a], c), half(a, out_refs[a], 2 * px + py, c), send_sems.at[a, j],
                                     recv_sems.at[a, j], (px, py, c)))
        return [], sends, recvs

    def forwards(in_refs, out_refs, sem_refs):
        send_sems, recv_sems = sem_refs[2], sem_refs[3]
        x, y, c = _place()
        sends, recvs = [], []
        for a in range(n_arr):
            for j, (px, py) in enumerate(_chip_peers(x, y)):
                peer = 2 * px + py
                sends.append(_remote(half(a, out_refs[a], peer, c), half(a, out_refs[a], peer, c), send_sems.at[a, j],
                                     recv_sems.at[a, j], (x, y, 1 - c)))
                recvs.append(_remote(half(a, out_refs[a], peer, c), half(a, out_refs[a], peer, 1 - c), send_sems.at[a, j],
                                     recv_sems.at[a, j], (x, y, 1 - c)))
        return sends, recvs

    sems = [pltpu.SemaphoreType.DMA((n_arr, 3))] * 4
    return _Comm(ins + list(placed), [jax.ShapeDtypeStruct(p.shape, p.dtype) for p in placed], sems, copies,
                 aliases={n_arr + a: a for a in range(n_arr)}, forwards=forwards)


def _allgather_comm(a):
    def copies(in_refs, out_refs, sem_refs):
        send_sems, recv_sems, local_sem = sem_refs
        x, y, c = _place()
        me = 2 * x + y
        local = [pltpu.make_async_copy(in_refs[0], out_refs[0].at[me], local_sem)]
        sends, recvs = [], []
        for j, (px, py) in enumerate(_chip_peers(x, y)):
            sends.append(_remote(in_refs[0], out_refs[0].at[me], send_sems.at[j], recv_sems.at[j], (px, py, c)))
            recvs.append(_remote(in_refs[0], out_refs[0].at[2 * px + py], send_sems.at[j], recv_sems.at[j], (px, py, c)))
        return local, sends, recvs

    sems = [pltpu.SemaphoreType.DMA((3,)), pltpu.SemaphoreType.DMA((3,)), pltpu.SemaphoreType.DMA]
    return _Comm([a], [jax.ShapeDtypeStruct((N_CHIPS,) + a.shape, a.dtype)], sems, copies)


class _Geom:
    def __init__(self, kind, shape, stride=None, width=None):
        self.kind, self.shape, self.stride, self.width = kind, tuple(shape), stride, width
        if kind == "cols":
            k, n = shape
            self.half_shape, self.part_shape, self.final_shape = (k // 2, n), (k // 2, width), (k, width)
        else:
            _, _, h, n = shape
            self.half_shape, self.part_shape, self.final_shape = (N_CHIPS, h, n), (h, n), (2 * h, n)

    def half(self, ref, core):
        if self.kind == "cols":
            return ref.at[pl.ds(core * self.half_shape[0], self.half_shape[0]), :]
        return ref.at[:, core]

    def part(self, ref, chip):
        if self.kind == "cols":
            return ref.at[:, pl.ds(chip * self.stride, self.width)]
        return ref.at[chip]

    def final_half(self, ref, layer, core):
        rows = self.part_shape[0]
        return ref.at[layer, pl.ds(core * rows, rows), :]


def _reduce_a_comm(geoms, arrs):
    n = len(arrs)

    def copies(in_refs, out_refs, sem_refs):
        x, y, c = _place()
        cps = [_remote(geoms[a].half(in_refs[a], 1 - c), out_refs[a], sem_refs[0].at[a], sem_refs[1].at[a], (x, y, 1 - c))
               for a in range(n)]
        return [], cps, cps

    return _Comm(arrs, [jax.ShapeDtypeStruct(g.half_shape, F32) for g in geoms],
                 [pltpu.SemaphoreType.DMA((n,)), pltpu.SemaphoreType.DMA((n,))], copies)


def _reduce_b_comm(geoms, halves):
    n = len(halves)

    def copies(in_refs, out_refs, sem_refs):
        send_sems, recv_sems, local_sems = sem_refs
        x, y, c = _place()
        me = 2 * x + y
        local, sends, recvs = [], [], []
        for a in range(n):
            g = geoms[a]
            local.append(pltpu.make_async_copy(g.part(in_refs[a], me), out_refs[a].at[me], local_sems.at[a]))
            for j, (px, py) in enumerate(_chip_peers(x, y)):
                peer = 2 * px + py
                sends.append(_remote(g.part(in_refs[a], peer), out_refs[a].at[me], send_sems.at[a, j], recv_sems.at[a, j],
                                     (px, py, c)))
                recvs.append(_remote(g.part(in_refs[a], me), out_refs[a].at[peer], send_sems.at[a, j], recv_sems.at[a, j],
                                     (px, py, c)))
        return local, sends, recvs

    sems = [pltpu.SemaphoreType.DMA((n, 3)), pltpu.SemaphoreType.DMA((n, 3)), pltpu.SemaphoreType.DMA((n,))]
    return _Comm(halves, [jax.ShapeDtypeStruct((N_CHIPS,) + g.part_shape, h.dtype) for g, h in zip(geoms, halves)], sems,
                 copies)


def _reduce_c_comm(geoms, finals, layer):
    n = len(finals)

    def copies(in_refs, out_refs, sem_refs):
        send_sems, recv_sems = sem_refs
        x, y, c = _place()
        sends, recvs = [], []
        for a in range(n):
            g = geoms[a]
            sends.append(_remote(g.final_half(in_refs[a], layer, c), g.final_half(out_refs[a], layer, c), send_sems.at[a],
                                 recv_sems.at[a], (x, y, 1 - c)))
            recvs.append(_remote(g.final_half(in_refs[a], layer, c), g.final_half(out_refs[a], layer, 1 - c),
                                 send_sems.at[a], recv_sems.at[a], (x, y, 1 - c)))
        return [], sends, recvs

    sems = [pltpu.SemaphoreType.DMA((n,)), pltpu.SemaphoreType.DMA((n,))]
    return _Comm(finals, [jax.ShapeDtypeStruct(f.shape, f.dtype) for f in finals], sems, copies,
                 aliases={a: a for a in range(n)})


def _run_comm(comm, name):
    def body(*refs):
        ni, no = len(comm.ins), len(comm.out_shapes)
        in_refs, out_refs, sem_refs = refs[:ni], refs[ni:ni + no], refs[ni + no:]
        comm.start(in_refs, out_refs, sem_refs)
        comm.wait(in_refs, out_refs, sem_refs)

    return pl.pallas_call(
        body, name=name, in_specs=[_ANY] * len(comm.ins), out_specs=[_ANY] * len(comm.out_shapes),
        out_shape=comm.out_shapes, scratch_shapes=comm.sems, input_output_aliases=comm.aliases,
        compiler_params=pltpu.CompilerParams(has_side_effects=True),
    )(*comm.ins)


def _tile_call(body, name, nt, in_specs, out_specs, out_shape, scratch, args, comm):
    if comm is None:
        outs = pl.pallas_call(body, name=name, grid=(nt,), in_specs=in_specs, out_specs=out_specs, out_shape=out_shape,
                              scratch_shapes=scratch, compiler_params=_params("arbitrary"))(*args)
        return outs, []
    n_in, n_out, n_scr = len(in_specs), len(out_specs), len(scratch)
    ci, co = len(comm.ins), len(comm.out_shapes)

    def hosted(*refs):
        in_refs = refs[:n_in]
        cin = refs[n_in:n_in + ci]
        out_refs = refs[n_in + ci:n_in + ci + n_out]
        cout = refs[n_in + ci + n_out:n_in + ci + n_out + co]
        scr = refs[n_in + ci + n_out + co:n_in + ci + n_out + co + n_scr]
        sems = refs[n_in + ci + n_out + co + n_scr:]
        i = pl.program_id(0)

        @pl.when(i == 0)
        def _():
            comm.start(cin, cout, sems)

        body(*in_refs, *out_refs, *scr)

        @pl.when(i == nt - 1)
        def _():
            comm.wait(cin, cout, sems)

    outs = pl.pallas_call(
        hosted, name=name + "_comm", grid=(nt,),
        in_specs=list(in_specs) + [_ANY] * ci, out_specs=list(out_specs) + [_ANY] * co,
        out_shape=list(out_shape) + comm.out_shapes, scratch_shapes=list(scratch) + comm.sems,
        input_output_aliases={n_in + i: n_out + o for i, o in comm.aliases.items()},
        compiler_params=_params("arbitrary"),
    )(*args, *comm.ins)
    return outs[:n_out], outs[n_out:]


def mix_fwd(x, g_mix, w_in, w_out, wcat, bmat, ln_g, ln_b, avg, conv_w, pool_bd, pool_scale, *, tm, comm=None):
    t = x.shape[0]
    nt = t // tm

    def body(x_ref, g_ref, win_ref, wout_ref, wcat_ref, bmat_ref, lng_ref, lnb_ref, avg_ref, cw_ref, pw_ref, ps_ref,
             proj_ref, ycat_ref, x1_ref, hbuf, zbuf):
        i = pl.program_id(0)

        @pl.when(i == 0)
        def _():
            hbuf[0:HALO, :] = jnp.zeros((HALO, D_B), F32)
            zbuf[0:HALO, :] = jnp.zeros((HALO, D_C), F32)

        xv = x_ref[...]
        n = xv * lax.rsqrt(jnp.mean(xv * xv, axis=-1, keepdims=True) + RMS_EPS)
        h1 = (n * g_ref[...]).astype(BF16)
        proj_ref[...] = _dot(h1, win_ref[...])

        lo_mask = _lane_lt((CHUNK, CHUNK), HEAD_DIM)
        avg = avg_ref[...]
        gu = _gelu(proj_ref[:, 0:D_A])
        gv = _gelu(proj_ref[:, D_A:2 * D_A])
        dv = gv - _group_mean(gv, avg)
        var = _group_mean(dv * dv, avg)
        vnb = (dv * lax.rsqrt(var + LN_EPS) * lng_ref[...] + lnb_ref[...]).astype(BF16)
        for c in range(tm // CHUNK):
            rows = slice(c * CHUNK, (c + 1) * CHUNK)
            for j in range(3):
                cols = slice(j * CHUNK, (j + 1) * CHUNK)
                mixed = _sgu_mix(vnb[rows, cols], wcat_ref[j], lo_mask) + bmat_ref[:, cols]
                ycat_ref[rows, cols] = (gu[rows, cols] * mixed).astype(BF16)

        o = 2 * D_A
        hcur = proj_ref[:, o + 2 * D_B:o + 3 * D_B] * proj_ref[:, o:o + D_B]
        hbuf[HALO:HALO + tm, :] = hcur
        y = (cw_ref[2:3, :] * hcur + cw_ref[1:2, :] * hbuf[pl.ds(HALO - 1, tm), :]
             + cw_ref[0:1, :] * hbuf[pl.ds(HALO - 2, tm), :])
        ycat_ref[:, D_A:D_A + D_B] = (proj_ref[:, o + D_B:o + 2 * D_B] * y).astype(BF16)
        hbuf[0:HALO, :] = hbuf[tm:tm + HALO, :]

        zc = proj_ref[:, o + 3 * D_B:D_IN]
        zbuf[HALO:HALO + tm, :] = zc
        mean, _ = _pool_means(zbuf[...], tm, i * tm)
        pooled = (mean - zc).astype(BF16)
        ycat_ref[:, D_A + D_B:D_MODEL] = (_dot(pooled, pw_ref[...]) * ps_ref[...]).astype(BF16)
        zbuf[0:HALO, :] = zbuf[tm:tm + HALO, :]

        x1_ref[...] = xv + _dot(ycat_ref[...], wout_ref[...])

    row = lambda w: pl.BlockSpec((tm, w), lambda i: (i, 0))
    return _tile_call(
        body, "mix_fwd", nt,
        [row(D_MODEL), _const_spec((1, D_MODEL)), _const_spec((D_MODEL, D_IN)), _const_spec((D_MODEL, D_MODEL)),
         _const_spec((3, CHUNK, 2 * CHUNK)), _const_spec((CHUNK, D_A)), _const_spec((1, D_A)), _const_spec((1, D_A)),
         _const_spec((D_A, D_A)), _const_spec((8, D_B)), _const_spec((D_C, D_C)), _const_spec((1, D_C))],
        [row(D_IN), row(D_MODEL), row(D_MODEL)],
        [jax.ShapeDtypeStruct((t, D_IN), F32), jax.ShapeDtypeStruct((t, D_MODEL), BF16),
         jax.ShapeDtypeStruct((t, D_MODEL), F32)],
        [pltpu.VMEM((tm + HALO, D_B), F32), pltpu.VMEM((tm + HALO, D_C), F32)],
        (x, g_mix, w_in, w_out, wcat, bmat, ln_g, ln_b, avg, conv_w, pool_bd, pool_scale), comm)


def ffn_ple_fwd(x1, p, g_ff, w_ff1, w_ff2, g_ple, w_gate, w_proj, *, tm, comm=None):
    t = x1.shape[0]
    nt = t // tm
    nc = D_FF // D_MODEL

    def body(x1_ref, p_ref, gff_ref, w1_ref, w2_ref, gple_ref, wg_ref, wp_ref, a_ref, x2_ref, x3_ref):
        x1v = x1_ref[...]
        n2 = x1v * lax.rsqrt(jnp.mean(x1v * x1v, axis=-1, keepdims=True) + RMS_EPS)
        h2 = (n2 * gff_ref[...]).astype(BF16)
        acc = x1v
        for c in range(nc):
            cols = slice(c * D_MODEL, (c + 1) * D_MODEL)
            a = _dot(h2, w1_ref[:, cols])
            a_ref[:, cols] = a.astype(BF16)
            ra = jnp.maximum(a, 0.0)
            acc = acc + _dot((ra * ra).astype(BF16), w2_ref[cols, :])
        x2_ref[...] = acc
        n3 = acc * lax.rsqrt(jnp.mean(acc * acc, axis=-1, keepdims=True) + RMS_EPS)
        h3 = (n3 * gple_ref[...]).astype(BF16)
        gate = jax.nn.sigmoid(_dot(h3, wg_ref[...]))
        pp = _dot(p_ref[...].astype(BF16), wp_ref[...])
        x3_ref[...] = acc + pp * gate

    row = lambda w: pl.BlockSpec((tm, w), lambda i: (i, 0))
    return _tile_call(
        body, "ffn_ple_fwd", nt,
        [row(D_MODEL), _layer_rows(p[1], tm), _const_spec((1, D_MODEL)), _const_spec((D_MODEL, D_FF)),
         _const_spec((D_FF, D_MODEL)), _const_spec((1, D_MODEL)), _const_spec((D_MODEL, D_MODEL)),
         _const_spec((D_PLE, D_MODEL))],
        [row(D_FF), row(D_MODEL), row(D_MODEL)],
        [jax.ShapeDtypeStruct((t, D_FF), BF16), jax.ShapeDtypeStruct((t, D_MODEL), F32),
         jax.ShapeDtypeStruct((t, D_MODEL), F32)],
        [], (x1, p[0], g_ff, w_ff1, w_ff2, g_ple, w_gate, w_proj), comm)


def loss_head(x, target, g, *, tm):
    t = x.shape[0]
    nt = t // tm

    def body(x_ref, t_ref, g_ref, loss_ref, dg_ref, dx_ref, sq_acc):
        i = pl.program_id(0)

        @pl.when(i == 0)
        def _():
            sq_acc[...] = jnp.zeros_like(sq_acc)
            dg_ref[...] = jnp.zeros_like(dg_ref)

        xv = x_ref[...]
        rs = lax.rsqrt(jnp.mean(xv * xv, axis=-1, keepdims=True) + RMS_EPS)
        n = xv * rs
        gv = g_ref[...]
        err = n * gv - t_ref[...]
        sq_acc[...] += jnp.sum(err * err, axis=0, keepdims=True)
        dy = err * (1.0 / D_MODEL)
        dg_ref[...] += jnp.sum(dy * n, axis=0, keepdims=True)
        dx_ref[...] = _rms_bwd(dy, n, rs, gv)

        @pl.when(i == nt - 1)
        def _():
            total = jnp.sum(sq_acc[...], axis=1, keepdims=True) * (0.5 / D_MODEL)
            loss_ref[...] = jnp.broadcast_to(total, loss_ref.shape)

    row = pl.BlockSpec((tm, D_MODEL), lambda i: (i, 0))
    return pl.pallas_call(
        body, name="loss_head", grid=(nt,),
        in_specs=[row, row, _const_spec((1, D_MODEL))],
        out_specs=[_acc_spec((8, 128)), _acc_spec((1, D_MODEL)), row],
        out_shape=[jax.ShapeDtypeStruct((8, 128), F32), jax.ShapeDtypeStruct((1, D_MODEL), F32),
                   jax.ShapeDtypeStruct((t, D_MODEL), F32)],
        scratch_shapes=[pltpu.VMEM((1, D_MODEL), F32)],
        compiler_params=_params("arbitrary"),
    )(x, target, g)


def ple_bwd(d, x2, p, g_ple, w_gate, w_proj, *, tm, comm=None):
    t = d.shape[0]
    nt = t // tm

    def body(d_ref, x2_ref, p_ref, g_ref, wg_ref, wp_ref, dx2_ref, dg_ref, dwg_ref, dwp_ref):
        i = pl.program_id(0)

        @pl.when(i == 0)
        def _():
            dg_ref[...] = jnp.zeros_like(dg_ref)
            dwg_ref[...] = jnp.zeros_like(dwg_ref)
            dwp_ref[...] = jnp.zeros_like(dwp_ref)

        dv = d_ref[...]
        x2v = x2_ref[...]
        rs = lax.rsqrt(jnp.mean(x2v * x2v, axis=-1, keepdims=True) + RMS_EPS)
        n3 = x2v * rs
        gv = g_ref[...]
        h3 = (n3 * gv).astype(BF16)
        gate = jax.nn.sigmoid(_dot(h3, wg_ref[...]))
        pb = p_ref[...].astype(BF16)
        pp = _dot(pb, wp_ref[...])
        dwp_ref[...] += _dot_tn(pb, (dv * gate).astype(BF16))
        dpre = (dv * pp * gate * (1.0 - gate)).astype(BF16)
        dwg_ref[...] += _dot_tn(h3, dpre)
        dh3 = _dot_nt(dpre, wg_ref[...])
        dg_ref[...] += jnp.sum(dh3 * n3, axis=0, keepdims=True)
        dx2_ref[...] = dv + _rms_bwd(dh3, n3, rs, gv)

    row = lambda w: pl.BlockSpec((tm, w), lambda i: (i, 0))
    return _tile_call(
        body, "ple_bwd", nt,
        [row(D_MODEL), row(D_MODEL), _layer_rows(p[1], tm), _const_spec((1, D_MODEL)), _const_spec((D_MODEL, D_MODEL)),
         _const_spec((D_PLE, D_MODEL))],
        [row(D_MODEL), _acc_spec((1, D_MODEL)), _acc_spec((D_MODEL, D_MODEL)), _acc_spec((D_PLE, D_MODEL))],
        [jax.ShapeDtypeStruct((t, D_MODEL), F32), jax.ShapeDtypeStruct((1, D_MODEL), F32),
         jax.ShapeDtypeStruct((D_MODEL, D_MODEL), F32), jax.ShapeDtypeStruct((D_PLE, D_MODEL), F32)],
        [], (d, x2, p[0], g_ple, w_gate, w_proj), comm)


def ffn_bwd(dx2, x1, a, g_ff, w_ff1, w_ff2, *, tm, comm=None):
    t = dx2.shape[0]
    nt = t // tm
    nc = D_FF // D_MODEL

    def body(dx2_ref, x1_ref, a_ref, g_ref, w1_ref, w2_ref, dx1_ref, h2_ref, da_ref, dg_ref):
        i = pl.program_id(0)

        @pl.when(i == 0)
        def _():
            dg_ref[...] = jnp.zeros_like(dg_ref)

        dv = dx2_ref[...]
        x1v = x1_ref[...]
        rs = lax.rsqrt(jnp.mean(x1v * x1v, axis=-1, keepdims=True) + RMS_EPS)
        n2 = x1v * rs
        gv = g_ref[...]
        h2_ref[...] = (n2 * gv).astype(BF16)
        dvb = dv.astype(BF16)
        dh2 = jnp.zeros((tm, D_MODEL), F32)
        for c in range(nc):
            cols = slice(c * D_MODEL, (c + 1) * D_MODEL)
            ra = jnp.maximum(a_ref[:, cols].astype(F32), 0.0)
            da = (_dot_nt(dvb, w2_ref[cols, :]) * (2.0 * ra)).astype(BF16)
            da_ref[:, cols] = da
            dh2 = dh2 + _dot_nt(da, w1_ref[:, cols])
        dg_ref[...] += jnp.sum(dh2 * n2, axis=0, keepdims=True)
        dx1_ref[...] = dv + _rms_bwd(dh2, n2, rs, gv)

    row = lambda w: pl.BlockSpec((tm, w), lambda i: (i, 0))
    return _tile_call(
        body, "ffn_bwd", nt,
        [row(D_MODEL), row(D_MODEL), row(D_FF), _const_spec((1, D_MODEL)), _const_spec((D_MODEL, D_FF)),
         _const_spec((D_FF, D_MODEL))],
        [row(D_MODEL), row(D_MODEL), row(D_FF), _acc_spec((1, D_MODEL))],
        [jax.ShapeDtypeStruct((t, D_MODEL), F32), jax.ShapeDtypeStruct((t, D_MODEL), BF16),
         jax.ShapeDtypeStruct((t, D_FF), BF16), jax.ShapeDtypeStruct((1, D_MODEL), F32)],
        [], (dx2, x1, a, g_ff, w_ff1, w_ff2), comm)


def mix_bwd(dx1, x, proj, g_mix, w_in, w_out, wcat, wcat_t, bmat, ln_g, ln_b, avg, conv_w, pool_bd, pool_bd_t,
            pool_scale, *, tm, comm=None):
    t = dx1.shape[0]
    nt = t // tm
    prev_blocks = tm // HALO

    def body(dx1_ref, x_ref, proj_ref, prev_ref, g_ref, win_ref, wout_ref, wcat_ref, wcatt_ref, bmat_ref, lng_ref,
             lnb_ref, avg_ref, cw_ref, pw_ref, pwt_ref, ps_ref,
             dx_ref, h1_ref, dproj_ref, dg_ref, dws_ref, dbm_ref, dlng_ref, dlnb_ref, dcw_ref, dpw_ref, dps_ref,
             dyc, dpj, hbuf, zbuf, dybuf, qbuf):
        i = pl.program_id(0)
        ti = nt - 1 - i

        @pl.when(i == 0)
        def _():
            for ref in (dg_ref, dws_ref, dbm_ref, dlng_ref, dlnb_ref, dcw_ref, dpw_ref, dps_ref):
                ref[...] = jnp.zeros_like(ref)
            dybuf[tm:tm + HALO, :] = jnp.zeros((HALO, D_B), F32)
            qbuf[tm:tm + HALO, :] = jnp.zeros((HALO, D_C), F32)

        dx1v = dx1_ref[...]
        dyc[...] = _dot_nt(dx1v.astype(BF16), wout_ref[...])

        lo_mask = _lane_lt((CHUNK, CHUNK), HEAD_DIM)
        avg = avg_ref[...]
        lng = lng_ref[...]
        gu, dgu = _gelu_and_grad(proj_ref[:, 0:D_A])
        gv, dgv = _gelu_and_grad(proj_ref[:, D_A:2 * D_A])
        cen = gv - _group_mean(gv, avg)
        rstd = lax.rsqrt(_group_mean(cen * cen, avg) + LN_EPS)
        vhat = cen * rstd
        vnb = (vhat * lng + lnb_ref[...]).astype(BF16)
        dya = dyc[:, 0:D_A]
        dm = dya * gu
        dmb = dm.astype(BF16)
        dvn_rows = []
        for c in range(tm // CHUNK):
            rows = slice(c * CHUNK, (c + 1) * CHUNK)
            dbm_ref[...] += dm[rows]
            dvn_parts = []
            for j in range(3):
                cols = slice(j * CHUNK, (j + 1) * CHUNK)
                vnb2 = vnb[rows, cols]
                mixed = _sgu_mix(vnb2, wcat_ref[j], lo_mask) + bmat_ref[:, cols]
                dpj[rows, cols] = dya[rows, cols] * mixed * dgu[rows, cols]
                dmb2 = dmb[rows, cols]
                zero = jnp.zeros_like(dmb2)
                dm_st = jnp.concatenate([jnp.where(lo_mask, dmb2, zero), jnp.where(lo_mask, zero, dmb2)], axis=0)
                dws_ref[j] += _dot_nt(dm_st, vnb2)
                dvn_st = _dot(wcatt_ref[j], dmb2)
                dvn_parts.append(jnp.where(lo_mask, dvn_st[0:CHUNK], dvn_st[CHUNK:2 * CHUNK]))
            dvn_rows.append(jnp.concatenate(dvn_parts, axis=1))
        dvn = jnp.concatenate(dvn_rows, axis=0)
        dlng_ref[...] += jnp.sum(dvn * vhat, axis=0, keepdims=True)
        dlnb_ref[...] += jnp.sum(dvn, axis=0, keepdims=True)
        dvh = dvn * lng
        dgv_in = rstd * (dvh - _group_mean(dvh, avg) - vhat * _group_mean(dvh * vhat, avg))
        dpj[:, D_A:2 * D_A] = dgv_in * dgv

        o = 2 * D_A
        live = (ti > 0).astype(F32)
        zb = proj_ref[:, o:o + D_B]
        gb = proj_ref[:, o + D_B:o + 2 * D_B]
        gc = proj_ref[:, o + 2 * D_B:o + 3 * D_B]
        hcur = gc * zb
        hbuf[0:HALO, :] = prev_ref[:, o + 2 * D_B:o + 3 * D_B] * prev_ref[:, o:o + D_B] * live
        hbuf[HALO:HALO + tm, :] = hcur
        hm1 = hbuf[pl.ds(HALO - 1, tm), :]
        hm2 = hbuf[pl.ds(HALO - 2, tm), :]
        y = cw_ref[2:3, :] * hcur + cw_ref[1:2, :] * hm1 + cw_ref[0:1, :] * hm2
        dout = dyc[:, D_A:D_A + D_B]
        dpj[:, o + D_B:o + 2 * D_B] = dout * y
        dy = dout * gb
        dcw_ref[2:3, :] += jnp.sum(dy * hcur, axis=0, keepdims=True)
        dcw_ref[1:2, :] += jnp.sum(dy * hm1, axis=0, keepdims=True)
        dcw_ref[0:1, :] += jnp.sum(dy * hm2, axis=0, keepdims=True)
        dybuf[0:tm, :] = dy
        dh = (cw_ref[2:3, :] * dy + cw_ref[1:2, :] * dybuf[pl.ds(1, tm), :] + cw_ref[0:1, :] * dybuf[pl.ds(2, tm), :])
        dybuf[tm:tm + HALO, :] = dybuf[0:HALO, :]
        dpj[:, o:o + D_B] = dh * gc
        dpj[:, o + 2 * D_B:o + 3 * D_B] = dh * zb

        zc = proj_ref[:, o + 3 * D_B:D_IN]
        zbuf[0:HALO, :] = prev_ref[:, o + 3 * D_B:D_IN] * live
        zbuf[HALO:HALO + tm, :] = zc
        mean, inv = _pool_means(zbuf[...], tm, ti * tm)
        pooled = (mean - zc).astype(BF16)
        dyp = dyc[:, D_A + D_B:D_MODEL]
        ps = ps_ref[...]
        dps_ref[...] += jnp.sum(dyp * _dot(pooled, pw_ref[...]), axis=0, keepdims=True)
        dpw = (dyp * ps).astype(BF16)
        dpw_ref[...] += _dot_tn(pooled, dpw)
        dpooled = _dot(dpw, pwt_ref[...])
        qbuf[0:tm, :] = dpooled * inv
        q = qbuf[...]
        nrows = tm + HALO
        f2 = q + pltpu.roll(q, nrows - 1, 0)
        f4 = f2 + pltpu.roll(f2, nrows - 2, 0)
        f8 = f4 + pltpu.roll(f4, nrows - 4, 0)
        f16 = f8 + pltpu.roll(f8, nrows - 8, 0)
        lane = lax.broadcasted_iota(jnp.int32, (tm, D_C), 1)
        ahead = jnp.where(lane < 64, f2[0:tm], jnp.where(lane < 128, f4[0:tm], jnp.where(lane < 192, f8[0:tm], f16[0:tm])))
        dpj[:, o + 3 * D_B:D_IN] = ahead - dpooled
        qbuf[tm:tm + HALO, :] = qbuf[0:HALO, :]

        dprojb = dpj[...].astype(BF16)
        dproj_ref[...] = dprojb
        dh1 = _dot_nt(dprojb, win_ref[...])
        xv = x_ref[...]
        rs = lax.rsqrt(jnp.mean(xv * xv, axis=-1, keepdims=True) + RMS_EPS)
        n1 = xv * rs
        gv1 = g_ref[...]
        h1_ref[...] = (n1 * gv1).astype(BF16)
        dg_ref[...] += jnp.sum(dh1 * n1, axis=0, keepdims=True)
        dx_ref[...] = dx1v + _rms_bwd(dh1, n1, rs, gv1)

        @pl.when(i == nt - 1)
        def _():
            tril = (lax.broadcasted_iota(jnp.int32, (2 * CHUNK, CHUNK), 0) % CHUNK
                    >= lax.broadcasted_iota(jnp.int32, (2 * CHUNK, CHUNK), 1))
            for j in range(3):
                dws_ref[j] = jnp.where(tril, dws_ref[j], 0.0)
            dbm_ref[...] = _group_mean_split(dbm_ref[...], avg) * float(HEAD_DIM)

    rev = lambda w: pl.BlockSpec((tm, w), lambda i: (nt - 1 - i, 0))
    prev = pl.BlockSpec((HALO, D_IN), lambda i: (jnp.maximum((nt - 1 - i) * prev_blocks - 1, 0), 0))
    acc_shapes = [(1, D_MODEL), (3, 2 * CHUNK, CHUNK), (CHUNK, D_A), (1, D_A), (1, D_A), (8, D_B), (D_C, D_C), (1, D_C)]
    return _tile_call(
        body, "mix_bwd", nt,
        [rev(D_MODEL), rev(D_MODEL), rev(D_IN), prev, _const_spec((1, D_MODEL)), _const_spec((D_MODEL, D_IN)),
         _const_spec((D_MODEL, D_MODEL)), _const_spec((3, CHUNK, 2 * CHUNK)), _const_spec((3, 2 * CHUNK, CHUNK)),
         _const_spec((CHUNK, D_A)), _const_spec((1, D_A)), _const_spec((1, D_A)), _const_spec((D_A, D_A)),
         _const_spec((8, D_B)), _const_spec((D_C, D_C)), _const_spec((D_C, D_C)), _const_spec((1, D_C))],
        [rev(D_MODEL), rev(D_MODEL), rev(D_IN)] + [_acc_spec(s) for s in acc_shapes],
        [jax.ShapeDtypeStruct((t, D_MODEL), F32), jax.ShapeDtypeStruct((t, D_MODEL), BF16),
         jax.ShapeDtypeStruct((t, D_IN), BF16)] + [jax.ShapeDtypeStruct(s, F32) for s in acc_shapes],
        [pltpu.VMEM((tm, D_MODEL), F32), pltpu.VMEM((tm, D_IN), F32),
         pltpu.VMEM((tm + HALO, D_B), F32), pltpu.VMEM((tm + HALO, D_C), F32),
         pltpu.VMEM((tm + HALO, D_B), F32), pltpu.VMEM((tm + HALO, D_C), F32)],
        (dx1, x, proj, proj, g_mix, w_in, w_out, wcat, wcat_t, bmat, ln_g, ln_b, avg, conv_w, pool_bd, pool_bd_t,
         pool_scale), comm)


def wgrad(a, b, *, tk, a_layer=None, relu_sq=False):
    t, m = a.shape[-2:]
    n = b.shape[1]
    bm = min(m, 1024)
    bn = 1024 if n % 1024 == 0 else n
    nk = t // tk
    if a_layer is None:
        a_spec = pl.BlockSpec((tk, bm), lambda i, j, k: (k, i))
    else:
        a_spec = pl.BlockSpec((None, None, tk, bm), lambda i, j, k: (a_layer, 0, k, i))

    def body(a_ref, b_ref, o_ref):
        k = pl.program_id(2)

        @pl.when(k == 0)
        def _():
            o_ref[...] = jnp.zeros_like(o_ref)

        av = a_ref[...]
        if relu_sq:
            ra = jnp.maximum(av.astype(F32), 0.0)
            av = ra * ra
        o_ref[...] += _dot_tn(av.astype(BF16), b_ref[...].astype(BF16))

    return pl.pallas_call(
        body, name=f"wgrad_{m}x{n}" + ("_relu_sq" if relu_sq else ""), grid=(m // bm, n // bn, nk),
        in_specs=[a_spec, pl.BlockSpec((tk, bn), lambda i, j, k: (k, j))],
        out_specs=pl.BlockSpec((bm, bn), lambda i, j, k: (i, j)),
        out_shape=jax.ShapeDtypeStruct((m, n), F32),
        compiler_params=_params("parallel", "parallel", "arbitrary"),
    )(a, b)


def _row_block(rows, cols, target_bytes):
    target = max(8, target_bytes // (4 * cols))
    if rows <= target:
        return rows
    best = None
    for br in range(8, target + 1, 8):
        if rows % br == 0:
            best = br
    return best if best is not None else rows


def adamw(w, g, m, v):
    shape = w.shape
    cols = shape[-1]
    rows = math.prod(shape[:-1]) if len(shape) > 1 else 1
    br = _row_block(rows, cols, 1 << 20)

    def body(w_ref, g_ref, m_ref, v_ref, d_ref, nm_ref, nv_ref):
        gv = g_ref[...]
        nm = ADAM_B1 * m_ref[...] + (1.0 - ADAM_B1) * gv
        nv = ADAM_B2 * v_ref[...] + (1.0 - ADAM_B2) * jnp.square(gv)
        m_hat = nm / (1.0 - ADAM_B1 ** ADAM_STEP)
        v_hat = nv / (1.0 - ADAM_B2 ** ADAM_STEP)
        d_ref[...] = -ADAM_LR * (m_hat / (jnp.sqrt(v_hat) + ADAM_EPS) + ADAM_WD * w_ref[...])
        nm_ref[...] = nm
        nv_ref[...] = nv

    spec = pl.BlockSpec((br, cols), lambda i: (i, 0))
    outs = pl.pallas_call(
        body, name="adamw", grid=(rows // br,),
        in_specs=[spec] * 4, out_specs=[spec] * 3,
        out_shape=[jax.ShapeDtypeStruct((rows, cols), F32)] * 3,
        compiler_params=pltpu.CompilerParams(dimension_semantics=("parallel",)),
    )(*(a.reshape(rows, cols) for a in (w, g, m, v)))
    return tuple(o.reshape(shape) for o in outs)


ADD_STEPS = 4


def add_halves(geoms, arrs, received, c_idx, dtypes):
    n = len(arrs)

    def body(c_ref, *refs):
        del c_ref
        for a in range(n):
            refs[2 * n + a][...] = (refs[a][...] + refs[n + a][...]).astype(dtypes[a])

    own_specs, half_specs = [], []
    for g in geoms:
        if g.kind == "cols":
            rows, cols = g.half_shape[0] // ADD_STEPS, g.half_shape[1]
            own_specs.append(pl.BlockSpec((rows, cols), lambda i, c_ref: (ADD_STEPS * c_ref[0] + i, 0)))
            half_specs.append(pl.BlockSpec((rows, cols), lambda i, c_ref: (i, 0)))
        else:
            _, h, cols = g.half_shape
            own_specs.append(pl.BlockSpec((None, None, h, cols), lambda i, c_ref: (i, c_ref[0], 0, 0)))
            half_specs.append(pl.BlockSpec((None, h, cols), lambda i, c_ref: (i, 0, 0)))
    return pl.pallas_call(
        body, name="add_halves",
        grid_spec=pltpu.PrefetchScalarGridSpec(num_scalar_prefetch=1, grid=(ADD_STEPS,),
                                               in_specs=own_specs + half_specs, out_specs=half_specs),
        out_shape=[jax.ShapeDtypeStruct(g.half_shape, dt) for g, dt in zip(geoms, dtypes)],
        compiler_params=_params("parallel"),
    )(c_idx, *arrs, *received)


def add_parts(geoms, landed, finals, layer, c_idx):
    n = len(landed)

    def body(c_ref, *refs):
        del c_ref
        for a in range(n):
            p_ref = refs[a]
            parts = [p_ref[j].astype(F32) for j in range(N_CHIPS)]
            refs[2 * n + a][...] = ((parts[0] + parts[1]) + parts[2]) + parts[3]

    in_specs, out_specs = [], []
    for g in geoms:
        rows, cols = g.part_shape[0] // ADD_STEPS, g.part_shape[1]
        in_specs.append(pl.BlockSpec((N_CHIPS, rows, cols), lambda i, c_ref: (0, i, 0)))
        out_specs.append(pl.BlockSpec((None, rows, cols), lambda i, c_ref: (layer, ADD_STEPS * c_ref[0] + i, 0)))
    return pl.pallas_call(
        body, name="add_parts",
        grid_spec=pltpu.PrefetchScalarGridSpec(num_scalar_prefetch=1, grid=(ADD_STEPS,),
                                               in_specs=in_specs + [_ANY] * n, out_specs=out_specs),
        out_shape=[jax.ShapeDtypeStruct(f.shape, F32) for f in finals],
        input_output_aliases={1 + n + a: a for a in range(n)},
        compiler_params=_params("parallel"),
    )(c_idx, *landed, *finals)


def _shard_dims(k, n, axis):
    return (k // N_CHIPS, n) if axis == 0 else (k, n // N_CHIPS)


W_IN_STRIDE = 512
W_IN_WINDOW = 640


def _big_geoms():
    geoms = []
    for name, k, n, axis in BIG:
        if axis == 0:
            geoms.append(_Geom("rows", (N_CHIPS, 2, k // N_CHIPS // 2, n)))
        elif name == "w_in":
            geoms.append(_Geom("cols", (k, n), W_IN_STRIDE, W_IN_WINDOW))
        else:
            geoms.append(_Geom("cols", (k, n), n // N_CHIPS, n // N_CHIPS))
    return geoms


def _grad_views(gb, geoms):
    return [gb[name].reshape(g.shape) for (name, _, _, _), g in zip(BIG, geoms)]


def _round_up(v, m):
    return (v + m - 1) // m * m


def _prep_small(small):
    tril = jnp.tril(jnp.ones((CHUNK, CHUNK), bool))
    wm = jnp.where(tril, small["sgu_w"], 0.0).astype(BF16).reshape(DEPTH, 3, 2, CHUNK, CHUNK)
    head = jnp.arange(D_A) // HEAD_DIM
    grp = jnp.arange(D_C) // HEAD_DIM
    pw_rows = small["pool_w"].reshape(DEPTH, D_C, HEAD_DIM)
    pool_bd = jnp.where((grp[:, None] == grp[None, :])[None], jnp.tile(pw_rows, (1, 1, D_C // HEAD_DIM)), 0.0).astype(BF16)
    return dict(
        wcat=wm.transpose(0, 1, 3, 2, 4).reshape(DEPTH, 3, CHUNK, 2 * CHUNK),
        wcat_t=wm.transpose(0, 1, 2, 4, 3).reshape(DEPTH, 3, 2 * CHUNK, CHUNK),
        bmat=jnp.repeat(jnp.swapaxes(small["sgu_b"], 1, 2), HEAD_DIM, axis=2),
        avg=jnp.where(head[:, None] == head[None, :], 1.0 / HEAD_DIM, 0.0).astype(BF16),
        pool_bd=pool_bd, pool_bd_t=jnp.swapaxes(pool_bd, 1, 2),
        conv8=jnp.pad(small["conv_w"], ((0, 0), (0, 8 - 3), (0, 0))),
    )


def _row(a):
    return a.reshape(1, -1)


MIX_WEIGHTS = ("w_in", "w_out")
MLP_WEIGHTS = ("w_ff1", "w_ff2", "w_ple_gate", "w_ple_proj")
ALL_BIG = MIX_WEIGHTS + MLP_WEIGHTS
EARLY_GRADS = MLP_WEIGHTS + ("w_out",)
FFN_BWD_TILE = 512
PLE_BWD_TILE = 1024


def _fwd_layer(h, p, wl, small, prep, l, tm, comm_mix=None, comm_mlp=None):
    (proj, ycat, x1), got = mix_fwd(h, _row(small["norm_mix_g"][l]), wl["w_in"], wl["w_out"], prep["wcat"][l],
                                    prep["bmat"][l], _row(small["sgu_ln_g"][l]), _row(small["sgu_ln_b"][l]), prep["avg"],
                                    prep["conv8"][l], prep["pool_bd"][l], _row(small["pool_scale"][l]),
                                    tm=min(2 * tm, h.shape[0]), comm=comm_mix)
    if comm_mix is not None:
        wl = {**wl, **_weights_of(got, MLP_WEIGHTS)}
    (a, x2, x3), couts = ffn_ple_fwd(x1, (p, l), _row(small["norm_ff_g"][l]), wl["w_ff1"], wl["w_ff2"],
                                     _row(small["norm_ple_g"][l]), wl["w_ple_gate"], wl["w_ple_proj"], tm=tm,
                                     comm=comm_mlp)
    return (h, proj, ycat, x1, a, x2), x3, couts, wl


def _merge_comms(comms):
    comms = [cm for cm in comms if cm is not None]
    if len(comms) <= 1:
        return comms[0] if comms else None
    spans, ni, no, ns = [], 0, 0, 0
    for cm in comms:
        spans.append((ni, no, ns))
        ni, no, ns = ni + len(cm.ins), no + len(cm.out_shapes), ns + len(cm.sems)

    def copies(in_refs, out_refs, sem_refs):
        local, sends, recvs = [], [], []
        for cm, (i0, o0, s0) in zip(comms, spans):
            got = cm.copies(in_refs[i0:i0 + len(cm.ins)], out_refs[o0:o0 + len(cm.out_shapes)],
                            sem_refs[s0:s0 + len(cm.sems)])
            local, sends, recvs = local + got[0], sends + got[1], recvs + got[2]
        return local, sends, recvs

    aliases = {i0 + i: o0 + o for cm, (i0, o0, _) in zip(comms, spans) for i, o in cm.aliases.items()}
    assert all(cm.forwards is None for cm in comms)
    return _Comm(sum((cm.ins for cm in comms), []), sum((cm.out_shapes for cm in comms), []),
                 sum((cm.sems for cm in comms), []), copies, aliases)


def _split_results(results, comms):
    out, at = [], 0
    for cm in comms:
        if cm is None:
            out.append(None)
        else:
            out.append(results[at:at + len(cm.out_shapes)])
            at += len(cm.out_shapes)
    return out


class _Reduction:
    def __init__(self, layer, names, geoms, arrs, finals, c_arr, narrow=()):
        self.layer, self.names, self.geoms, self.arrs = layer, list(names), list(geoms), list(arrs)
        self.finals, self.c_arr = finals, c_arr
        self.dtypes = [BF16 if n in narrow else F32 for n in self.names]

    def comm_a(self):
        return _reduce_a_comm(self.geoms, self.arrs)

    def comm_b(self, received):
        return _reduce_b_comm(self.geoms, add_halves(self.geoms, self.arrs, received, self.c_arr, self.dtypes))

    def comm_c(self, landed):
        mine = add_parts(self.geoms, landed, [self.finals[n] for n in self.names], self.layer, self.c_arr)
        return _reduce_c_comm(self.geoms, mine, self.layer)

    def done(self, results):
        self.finals.update(zip(self.names, results))


class _Plan:
    def ple(self):
        return None

    def after_ple(self, results):
        pass

    def ffn(self):
        return None

    def after_ffn(self, results):
        pass

    def before_mix(self, gb):
        pass

    def mix(self):
        return None

    def after_mix(self, results):
        pass


class _CarryPlan(_Plan):
    def __init__(self, above):
        self.above = above

    def ple(self):
        return self.above.comm_a()

    def after_ple(self, results):
        self.received = results

    def ffn(self):
        return self.above.comm_b(self.received)

    def after_ffn(self, results):
        self.landed = results

    def mix(self):
        return self.above.comm_c(self.landed)

    def after_mix(self, results):
        self.above.done(results)


class _LastPlan(_CarryPlan):
    def __init__(self, above, make_early):
        super().__init__(above)
        self.make_early = make_early

    def before_mix(self, gb):
        self.early = self.make_early(gb)
        self.early_received = _run_comm(self.early.comm_a(), "reduce_a_early")

    def mix(self):
        self.parts = [self.above.comm_c(self.landed), self.early.comm_b(self.early_received)]
        return _merge_comms(self.parts)

    def after_mix(self, results):
        above_res, self.early_landed = _split_results(results, self.parts)
        self.above.done(above_res)


def _bwd_layer(d, saved, p, wl, small, prep, l, tm, tk, plan=None):
    plan = plan or _Plan()
    xin, proj, ycat, x1, a, x2 = saved
    (dx2, dg_ple, dw_gate, dw_proj), res = ple_bwd(d, x2, (p, l), _row(small["norm_ple_g"][l]), wl["w_ple_gate"],
                                                   wl["w_ple_proj"], tm=min(PLE_BWD_TILE, d.shape[0]), comm=plan.ple())
    plan.after_ple(res)
    gb = {"w_ple_gate": dw_gate, "w_ple_proj": dw_proj}
    (dx1, h2, da, dg_ff), res = ffn_bwd(dx2, x1, a, _row(small["norm_ff_g"][l]), wl["w_ff1"], wl["w_ff2"],
                                        tm=FFN_BWD_TILE if tm >= FFN_BWD_TILE else tm, comm=plan.ffn())
    plan.after_ffn(res)
    gb["w_ff2"] = wgrad(a, dx2, tk=tk, relu_sq=True)
    gb["w_ff1"] = wgrad(h2, da, tk=tk)
    gb["w_out"] = wgrad(ycat, dx1, tk=tk)
    plan.before_mix(gb)
    (dprev, h1, dproj, dg_mix, dws, dbm, dlng, dlnb, dcw, dpw, dps), res = mix_bwd(
        dx1, xin, proj, _row(small["norm_mix_g"][l]), wl["w_in"], wl["w_out"], prep["wcat"][l], prep["wcat_t"][l],
        prep["bmat"][l], _row(small["sgu_ln_g"][l]), _row(small["sgu_ln_b"][l]), prep["avg"], prep["conv8"][l],
        prep["pool_bd"][l], prep["pool_bd_t"][l], _row(small["pool_scale"][l]), tm=tm, comm=plan.mix())
    plan.after_mix(res)
    gb["w_in"] = wgrad(h1, dproj, tk=tk)
    gs = {
        "norm_ple_g": dg_ple[0], "norm_ff_g": dg_ff[0], "norm_mix_g": dg_mix[0],
        "sgu_w": dws.reshape(2 * 3, CHUNK, CHUNK), "sgu_b": dbm[:, ::HEAD_DIM].T,
        "sgu_ln_g": dlng[0], "sgu_ln_b": dlnb[0], "conv_w": dcw[0:3], "pool_scale": dps[0],
        "pool_w": jnp.stack([dpw[g * HEAD_DIM:(g + 1) * HEAD_DIM, g * HEAD_DIM:(g + 1) * HEAD_DIM]
                             for g in range(D_C // HEAD_DIM)]),
    }
    return dprev, gb, gs


def _local_step(x, p, target, full, small, *, tm, tk):
    prep = _prep_small(small)
    p = p[:, None]
    saved, h = [], x
    for l in range(DEPTH):
        wl = {name: full[name][l] for name in full}
        s, h, _, _ = _fwd_layer(h, p, wl, small, prep, l, tm)
        saved.append(s)
    loss_blk, d_final_g, d = loss_head(h, target, _row(small["final_g"]), tm=tm)
    gbig, gsm = [None] * DEPTH, [None] * DEPTH
    for l in reversed(range(DEPTH)):
        wl = {name: full[name][l] for name in full}
        d, gbig[l], gsm[l] = _bwd_layer(d, saved[l], p, wl, small, prep, l, tm, tk)
    big = {name: jnp.stack([gbig[l][name] for l in range(DEPTH)]) for name in gbig[0]}
    sm = {name: jnp.stack([gsm[l][name] for l in range(DEPTH)]) for name in gsm[0]}
    sm["final_g"] = d_final_g[0]
    return loss_blk[0, 0], d, big, sm


def _weights_of(gathered, names):
    wl = dict(zip([b[0] for b in BIG if b[0] in names], gathered))
    if "w_in" in wl:
        wl["w_in"] = wl["w_in"].transpose(1, 0, 2).reshape(D_MODEL, D_IN)
    return wl


def kernel(x, p, norm_mix_g, w_in, sgu_w, sgu_b, sgu_ln_g, sgu_ln_b, conv_w, pool_w, pool_scale, w_out, norm_ff_g, w_ff1, w_ff2, norm_ple_g, w_ple_gate, w_ple_proj, final_g, loss_target, m_norm_mix_g, m_w_in, m_sgu_w, m_sgu_b, m_sgu_ln_g, m_sgu_ln_b, m_conv_w, m_pool_w, m_pool_scale, m_w_out, m_norm_ff_g, m_w_ff1, m_w_ff2, m_norm_ple_g, m_w_ple_gate, m_w_ple_proj, m_final_g, v_norm_mix_g, v_w_in, v_sgu_w, v_sgu_b, v_sgu_ln_g, v_sgu_ln_b, v_conv_w, v_pool_w, v_pool_scale, v_w_out, v_norm_ff_g, v_w_ff1, v_w_ff2, v_norm_ple_g, v_w_ple_gate, v_w_ple_proj, v_final_g):
    args = dict(locals())
    w = {name: args[name] for name in WEIGHTS}
    m = {name: args["m_" + name] for name in WEIGHTS}
    v = {name: args["v_" + name] for name in WEIGHTS}
    t = x.shape[1]
    tm = min(512, t)
    tk = min(2048, t)
    x_idx, y_idx, c_idx = _place()
    chip = 2 * x_idx + y_idx
    c_arr = c_idx.reshape(1).astype(jnp.int32)
    xs, target = x[0], loss_target[0]

    shards = {name: w[name].astype(BF16) for name, _, _, _ in BIG}
    conv_rows = _round_up(CONV_SHARD, 8 * 128) // 128
    conv_flat = jnp.pad(w["conv_w"].reshape(-1), (0, conv_rows * 128 - CONV_SHARD)).reshape(conv_rows, 128)
    first = _run_comm(_gather_halved_comm(shards, 0, MIX_WEIGHTS, _place_own(shards, 0, MIX_WEIGHTS, chip, conv_flat),
                                          conv_flat), "gather_first")
    conv_full = (first[len(MIX_WEIGHTS)].reshape(N_CHIPS, -1)[:, :CONV_SHARD]
                 .reshape(N_CHIPS, DEPTH, 3, D_B // N_CHIPS).transpose(1, 2, 0, 3).reshape(DEPTH, 3, D_B))
    small = {name: w[name] for name in SMALL}
    small["conv_w"] = conv_full
    prep = _prep_small(small)

    wl = [None] * DEPTH
    wl[0] = _weights_of(first, MIX_WEIGHTS)
    saved, h = [], xs
    for l in range(DEPTH):
        comm_mix = None
        if l == 0:
            comm_mix = _gather_halved_comm(shards, 0, MLP_WEIGHTS, _place_own(shards, 0, MLP_WEIGHTS, chip))
        comm_mlp = _gather_comm(shards, l + 1, ALL_BIG) if l + 1 < DEPTH else None
        s, h, got, wl[l] = _fwd_layer(h, p, wl[l], small, prep, l, tm, comm_mix, comm_mlp)
        saved.append(s)
        if comm_mlp is not None:
            wl[l + 1] = _weights_of(got, ALL_BIG)

    loss_blk, d_final_g, d = loss_head(h, target, _row(small["final_g"]), tm=tm)

    geoms = dict(zip([b[0] for b in BIG], _big_geoms()))
    finals = {name: jnp.zeros((DEPTH,) + g.final_shape, F32) for name, g in geoms.items()}

    def reduction(layer, names, gb, narrow=()):
        return _Reduction(layer, names, [geoms[n] for n in names], [gb[n].reshape(geoms[n].shape) for n in names],
                          finals, c_arr, narrow)

    gsm = [None] * DEPTH
    above = None
    for l in reversed(range(DEPTH)):
        if above is None:
            plan = _Plan()
        elif l > 0:
            plan = _CarryPlan(above)
        else:
            plan = _LastPlan(above, lambda gb: reduction(0, EARLY_GRADS, gb))
        d, gb, gsm[l] = _bwd_layer(d, saved[l], p, wl[l], small, prep, l, tm, tk, plan)
        if l > 0:
            above = reduction(l, ALL_BIG, gb)

    sm = {name: jnp.stack([gsm[i][name] for i in range(DEPTH)]) for name in gsm[0]}
    sm["final_g"] = d_final_g[0]
    sizes = [sm[name].size for name in SMALL]
    small_rows = _round_up(-(-sum(sizes) // (2 * N_CHIPS * LANES)), 8 * ADD_STEPS)
    small_flat = jnp.pad(jnp.concatenate([sm[name].reshape(-1) for name in SMALL]),
                         (0, 2 * N_CHIPS * small_rows * LANES - sum(sizes)))
    geoms["small"] = _Geom("rows", (N_CHIPS, 2, small_rows, LANES))
    finals["small"] = jnp.zeros((1,) + geoms["small"].final_shape, F32)
    late = reduction(0, ("w_in", "small"), {**gb, "small": small_flat}, narrow=("w_in",))
    late_landed = _run_comm(late.comm_b(_run_comm(late.comm_a(), "reduce_a_late")), "reduce_b_late")
    last = [plan.early.comm_c(plan.early_landed), late.comm_c(late_landed)]
    early_res, late_res = _split_results(_run_comm(_merge_comms(last), "reduce_c_last"), last)
    plan.early.done(early_res)
    late.done(late_res)

    grads = {name: finals[name] for name, _, _, _ in BIG}
    grads["w_in"] = lax.dynamic_slice_in_dim(grads["w_in"], chip * (D_IN // N_CHIPS - W_IN_STRIDE), D_IN // N_CHIPS, axis=2)
    small_red = _run_comm(_allgather_comm(finals["small"][0]), "small_allgather")[0].reshape(-1)
    off = 0
    for name, size in zip(SMALL, sizes):
        grads[name] = small_red[off:off + size].reshape(sm[name].shape)
        off += size
    grads["conv_w"] = lax.dynamic_slice_in_dim(grads["conv_w"], chip * (D_B // N_CHIPS), D_B // N_CHIPS, axis=2)

    loss = lax.psum(loss_blk[0, 0], ("x", "y", "c"))
    delta, new_m, new_v = {}, {}, {}
    for name in WEIGHTS:
        delta[name], new_m[name], new_v[name] = adamw(w[name], grads[name], m[name], v[name])
    return (loss, d[None], *[grads[n] for n in WEIGHTS], *[delta[n] for n in WEIGHTS],
            *[new_m[n] for n in WEIGHTS], *[new_v[n] for n in WEIGHTS])
```

```python
import math

import jax
import jax.numpy as jnp
from jax import lax
from jax.experimental import pallas as pl
from jax.experimental.pallas import tpu as pltpu

F32 = jnp.float32
BF16 = jnp.bfloat16

D_MODEL = 1024
DEPTH = 4
D_PLE = 256
D_FF = 4096
HEAD_DIM = 64
D_A = 384
D_B = 384
D_C = 256
D_IN = 2176
CHUNK = 128
HALO = 16
RMS_EPS = 1e-6
LN_EPS = 1e-5
N_CHIPS = 4
LANES = 1024

ADAM_LR = 0.001
ADAM_B1 = 0.9
ADAM_B2 = 0.999
ADAM_EPS = 1e-08
ADAM_WD = 0.01
ADAM_STEP = 10

VMEM_LIMIT_BYTES = 60 * 1024 * 1024

_RSQRT2 = 0.7071067811865476
_INV_SQRT_2PI = 0.3989422804014327

BIG = (
    ("w_in", D_MODEL, D_IN, 1),
    ("w_out", D_MODEL, D_MODEL, 0),
    ("w_ff1", D_MODEL, D_FF, 1),
    ("w_ff2", D_FF, D_MODEL, 0),
    ("w_ple_gate", D_MODEL, D_MODEL, 0),
    ("w_ple_proj", D_PLE, D_MODEL, 1),
)
SMALL = ("norm_mix_g", "sgu_w", "sgu_b", "sgu_ln_g", "sgu_ln_b", "conv_w", "pool_w", "pool_scale",
         "norm_ff_g", "norm_ple_g", "final_g")
WEIGHTS = ("norm_mix_g", "w_in", "sgu_w", "sgu_b", "sgu_ln_g", "sgu_ln_b", "conv_w", "pool_w", "pool_scale",
           "w_out", "norm_ff_g", "w_ff1", "w_ff2", "norm_ple_g", "w_ple_gate", "w_ple_proj", "final_g")
CONV_SHARD = DEPTH * 3 * (D_B // N_CHIPS)


def _dot(a, b):
    return jnp.dot(a, b, preferred_element_type=F32)


def _dot_nt(a, b):
    return lax.dot_general(a, b, (((1,), (1,)), ((), ())), preferred_element_type=F32)


def _dot_tn(a, b):
    return lax.dot_general(a, b, (((0,), (0,)), ((), ())), preferred_element_type=F32)


def _const_spec(shape):
    nd = len(shape)
    return pl.BlockSpec(shape, lambda i: (0,) * nd, pipeline_mode=pl.Buffered(1))


def _acc_spec(shape):
    nd = len(shape)
    return pl.BlockSpec(shape, lambda i: (0,) * nd)


def _layer_rows(layer, tm):
    return pl.BlockSpec((None, None, tm, D_PLE), lambda i: (layer, 0, i, 0))


def _params(*sem):
    return pltpu.CompilerParams(dimension_semantics=sem, vmem_limit_bytes=VMEM_LIMIT_BYTES)


def _rms_bwd(dh, n, rs, g):
    dn = dh * g
    return rs * (dn - n * jnp.mean(dn * n, axis=-1, keepdims=True))


def _gelu(x):
    return x * (0.5 * (1.0 + lax.erf(x * _RSQRT2)))


def _gelu_and_grad(x):
    cdf = 0.5 * (1.0 + lax.erf(x * _RSQRT2))
    return x * cdf, cdf + x * (jnp.exp(-0.5 * x * x) * _INV_SQRT_2PI)


def _group_mean(v, avg):
    vb = v.astype(BF16)
    split = 2 * CHUNK
    return jnp.concatenate([_dot(vb[:, :split], avg[:split, :split]), _dot(vb[:, split:], avg[split:, split:])], axis=1)


def _group_mean_split(v, avg):
    hi = v.astype(BF16)
    lo = (v - hi.astype(F32)).astype(BF16)
    return _dot(hi, avg) + _dot(lo, avg)


def _lane_lt(shape, bound):
    return lax.broadcasted_iota(jnp.int32, shape, 1) < bound


def _sgu_mix(vnb2, wcat_j, lo_mask):
    zero = jnp.zeros_like(vnb2)
    stacked = jnp.concatenate([jnp.where(lo_mask, vnb2, zero), jnp.where(lo_mask, zero, vnb2)], axis=0)
    return _dot(wcat_j, stacked)


def _pool_means(ext, tile_rows, first_pos):
    s2 = ext + pltpu.roll(ext, 1, 0)
    s4 = s2 + pltpu.roll(s2, 2, 0)
    s8 = s4 + pltpu.roll(s4, 4, 0)
    s16 = s8 + pltpu.roll(s8, 8, 0)
    pos = (first_pos + lax.broadcasted_iota(jnp.int32, (tile_rows, 1), 0) + 1).astype(F32)
    lane = lax.broadcasted_iota(jnp.int32, (tile_rows, D_C), 1)
    sums = jnp.where(lane < 64, s2[HALO:], jnp.where(lane < 128, s4[HALO:], jnp.where(lane < 192, s8[HALO:], s16[HALO:])))
    win = jnp.where(lane < 64, 2.0, jnp.where(lane < 128, 4.0, jnp.where(lane < 192, 8.0, 16.0)))
    inv = 1.0 / jnp.minimum(pos, win)
    return sums * inv, inv


MESH = pl.DeviceIdType.MESH
_ANY = pl.BlockSpec(memory_space=pl.ANY)


def _place():
    return lax.axis_index("x"), lax.axis_index("y"), lax.axis_index("c")


def _chip_peers(x, y):
    return [(1 - x, y), (x, 1 - y), (1 - x, 1 - y)]


class _Comm:
    def __init__(self, ins, out_shapes, sems, copies, aliases=None, forwards=None):
        self.ins, self.out_shapes, self.sems, self.copies = list(ins), list(out_shapes), list(sems), copies
        self.aliases = dict(aliases or {})
        self.forwards = forwards

    def start(self, in_refs, out_refs, sem_refs):
        local, sends, _ = self.copies(in_refs, out_refs, sem_refs)
        for cp in local + sends:
            cp.start()

    def wait(self, in_refs, out_refs, sem_refs):
        local, sends, recvs = self.copies(in_refs, out_refs, sem_refs)
        for cp in recvs:
            cp.wait_recv()
        passed, passed_in = self.forwards(in_refs, out_refs, sem_refs) if self.forwards else ([], [])
        for cp in passed:
            cp.start()
        for cp in sends:
            cp.wait_send()
        for cp in local:
            cp.wait()
        for cp in passed_in:
            cp.wait_recv()
        for cp in passed:
            cp.wait_send()


def _remote(src, dst, send_sem, recv_sem, device):
    return pltpu.make_async_remote_copy(src_ref=src, dst_ref=dst, send_sem=send_sem, recv_sem=recv_sem,
                                        device_id=device, device_id_type=MESH)


def _gather_comm(shards, layer, names, conv=None):
    mats = [b for b in BIG if b[0] in names]
    ins = [shards[name] for name, _, _, _ in mats] + ([conv] if conv is not None else [])
    out_shapes = []
    for name, k, n, axis in mats:
        shape = (N_CHIPS, k, n // N_CHIPS) if name == "w_in" else (k, n)
        out_shapes.append(jax.ShapeDtypeStruct(shape, BF16))
    if conv is not None:
        out_shapes.append(jax.ShapeDtypeStruct((N_CHIPS,) + conv.shape, conv.dtype))
    n_arr = len(ins)

    def block(a, out_ref, chip):
        if a == len(mats) or mats[a][0] == "w_in":
            return out_ref.at[chip]
        _, k, n, axis = mats[a]
        if axis == 0:
            return out_ref.at[pl.ds(chip * (k // N_CHIPS), k // N_CHIPS), :]
        return out_ref.at[:, pl.ds(chip * (n // N_CHIPS), n // N_CHIPS)]

    def copies(in_refs, out_refs, sem_refs):
        send_sems, recv_sems, local_sems = sem_refs
        x, y, c = _place()
        me = 2 * x + y
        local, sends, recvs = [], [], []
        for a in range(n_arr):
            src = in_refs[a].at[layer] if a < len(mats) else in_refs[a]
            local.append(pltpu.make_async_copy(src, block(a, out_refs[a], me), local_sems.at[a]))
            for j, (px, py) in enumerate(_chip_peers(x, y)):
                sends.append(_remote(src, block(a, out_refs[a], me), send_sems.at[a, j], recv_sems.at[a, j], (px, py, c)))
                recvs.append(_remote(src, block(a, out_refs[a], 2 * px + py), send_sems.at[a, j], recv_sems.at[a, j],
                                     (px, py, c)))
        return local, sends, recvs

    sems = [pltpu.SemaphoreType.DMA((n_arr, 3)), pltpu.SemaphoreType.DMA((n_arr, 3)), pltpu.SemaphoreType.DMA((n_arr,))]
    return _Comm(ins, out_shapes, sems, copies)


def _gather_halved_comm(shards, layer, names, conv=None):
    mats = [b for b in BIG if b[0] in names]
    ins = [shards[name] for name, _, _, _ in mats] + ([conv] if conv is not None else [])
    out_shapes = []
    for name, k, n, axis in mats:
        out_shapes.append(jax.ShapeDtypeStruct((N_CHIPS, k, n // N_CHIPS) if name == "w_in" else (k, n), BF16))
    if conv is not None:
        out_shapes.append(jax.ShapeDtypeStruct((N_CHIPS,) + conv.shape, conv.dtype))
    n_arr = len(ins)

    def whole(a, ref, chip):
        if a == len(mats) or mats[a][0] == "w_in":
            return ref.at[chip]
        _, k, n, axis = mats[a]
        if axis == 0:
            return ref.at[pl.ds(chip * (k // N_CHIPS), k // N_CHIPS), :]
        return ref.at[:, pl.ds(chip * (n // N_CHIPS), n // N_CHIPS)]

    def src_half(a, in_ref, core):
        if a == len(mats):
            rows = conv.shape[0] // 2
            return in_ref.at[pl.ds(core * rows, rows), :]
        rows = shards[mats[a][0]].shape[1] // 2
        return in_ref.at[layer, pl.ds(core * rows, rows), :]

    def half(a, out_ref, chip, core):
        if a == len(mats):
            rows = conv.shape[0] // 2
            return out_ref.at[chip, pl.ds(core * rows, rows), :]
        name, k, n, axis = mats[a]
        if name == "w_in":
            return out_ref.at[chip, pl.ds(core * (k // 2), k // 2), :]
        if axis == 0:
            rows = k // N_CHIPS // 2
            return out_ref.at[pl.ds(chip * 2 * rows + core * rows, rows), :]
        return out_ref.at[pl.ds(core * (k // 2), k // 2), pl.ds(chip * (n // N_CHIPS), n // N_CHIPS)]

    def copies(in_refs, out_refs, sem_refs):
        send_sems, recv_sems, own_send, own_recv = sem_refs[0], sem_refs[1], sem_refs[4], sem_refs[5]
        x, y, c = _place()
        me = 2 * x + y
        sends, recvs = [], []
        for a in range(n_arr):
            own = in_refs[a].at[layer] if a < len(mats) else in_refs[a]
            cp = _remote(own, whole(a, out_refs[a], me), own_send.at[a], own_recv.at[a], (x, y, 1 - c))
            sends.append(cp)
            recvs.append(cp)
            for j, (px, py) in enumerate(_chip_peers(x, y)):
                sends.append(_remote(src_half(a, in_refs[a], c), half(a, out_refs[a], me, c), send_sems.at[a, j],
                                     recv_sems.at[a, j], (px, py, c)))
                recvs.append(_remote(src_half(a, in_refs[a], c), half(a, out_refs[a], 2 * px + py, c), send_sems.at[a, j],
                                     recv_sems.at[a, j], (px, py, c)))
        return [], sends, recvs

    def forwards(in_refs, out_refs, sem_refs):
        send_sems, recv_sems = sem_refs[2], sem_refs[3]
        x, y, c = _place()
        sends, recvs = [], []
        for a in range(n_arr):
            for j, (px, py) in enumerate(_chip_peers(x, y)):
                peer = 2 * px + py
                sends.append(_remote(half(a, out_refs[a], peer, c), half(a, out_refs[a], peer, c), send_sems.at[a, j],
                                     recv_sems.at[a, j], (x, y, 1 - c)))
                recvs.append(_remote(half(a, out_refs[a], peer, c), half(a, out_refs[a], peer, 1 - c), send_sems.at[a, j],
                                     recv_sems.at[a, j], (x, y, 1 - c)))
        return sends, recvs

    sems = [pltpu.SemaphoreType.DMA((n_arr, 3))] * 4 + [pltpu.SemaphoreType.DMA((n_arr,))] * 2
    return _Comm(ins, out_shapes, sems, copies, forwards=forwards)


def _allgather_comm(a):
    def copies(in_refs, out_refs, sem_refs):
        send_sems, recv_sems, local_sem = sem_refs
        x, y, c = _place()
        me = 2 * x + y
        local = [pltpu.make_async_copy(in_refs[0], out_refs[0].at[me], local_sem)]
        sends, recvs = [], []
        for j, (px, py) in enumerate(_chip_peers(x, y)):
            sends.append(_remote(in_refs[0], out_refs[0].at[me], send_sems.at[j], recv_sems.at[j], (px, py, c)))
            recvs.append(_remote(in_refs[0], out_refs[0].at[2 * px + py], send_sems.at[j], recv_sems.at[j], (px, py, c)))
        return local, sends, recvs

    sems = [pltpu.SemaphoreType.DMA((3,)), pltpu.SemaphoreType.DMA((3,)), pltpu.SemaphoreType.DMA]
    return _Comm([a], [jax.ShapeDtypeStruct((N_CHIPS,) + a.shape, a.dtype)], sems, copies)


class _Geom:
    def __init__(self, kind, shape, stride=None, width=None):
        self.kind, self.shape, self.stride, self.width = kind, tuple(shape), stride, width
        if kind == "cols":
            k, n = shape
            self.half_shape, self.part_shape, self.final_shape = (k // 2, n), (k // 2, width), (k, width)
        else:
            _, _, h, n = shape
            self.half_shape, self.part_shape, self.final_shape = (N_CHIPS, h, n), (h, n), (2 * h, n)

    def half(self, ref, core):
        if self.kind == "cols":
            return ref.at[pl.ds(core * self.half_shape[0], self.half_shape[0]), :]
        return ref.at[:, core]

    def part(self, ref, chip):
        if self.kind == "cols":
            return ref.at[:, pl.ds(chip * self.stride, self.width)]
        return ref.at[chip]

    def final_half(self, ref, layer, core):
        rows = self.part_shape[0]
        return ref.at[layer, pl.ds(core * rows, rows), :]


def _reduce_a_comm(geoms, arrs):
    n = len(arrs)

    def copies(in_refs, out_refs, sem_refs):
        x, y, c = _place()
        cps = [_remote(geoms[a].half(in_refs[a], 1 - c), out_refs[a], sem_refs[0].at[a], sem_refs[1].at[a], (x, y, 1 - c))
               for a in range(n)]
        return [], cps, cps

    return _Comm(arrs, [jax.ShapeDtypeStruct(g.half_shape, F32) for g in geoms],
                 [pltpu.SemaphoreType.DMA((n,)), pltpu.SemaphoreType.DMA((n,))], copies)


def _reduce_b_comm(geoms, halves):
    n = len(halves)

    def copies(in_refs, out_refs, sem_refs):
        send_sems, recv_sems, local_sems = sem_refs
        x, y, c = _place()
        me = 2 * x + y
        local, sends, recvs = [], [], []
        for a in range(n):
            g = geoms[a]
            local.append(pltpu.make_async_copy(g.part(in_refs[a], me), out_refs[a].at[me], local_sems.at[a]))
            for j, (px, py) in enumerate(_chip_peers(x, y)):
                peer = 2 * px + py
                sends.append(_remote(g.part(in_refs[a], peer), out_refs[a].at[me], send_sems.at[a, j], recv_sems.at[a, j],
                                     (px, py, c)))
                recvs.append(_remote(g.part(in_refs[a], me), out_refs[a].at[peer], send_sems.at[a, j], recv_sems.at[a, j],
                                     (px, py, c)))
        return local, sends, recvs

    sems = [pltpu.SemaphoreType.DMA((n, 3)), pltpu.SemaphoreType.DMA((n, 3)), pltpu.SemaphoreType.DMA((n,))]
    return _Comm(halves, [jax.ShapeDtypeStruct((N_CHIPS,) + g.part_shape, h.dtype) for g, h in zip(geoms, halves)], sems,
                 copies)


def _reduce_c_comm(geoms, finals, layer):
    n = len(finals)

    def copies(in_refs, out_refs, sem_refs):
        send_sems, recv_sems = sem_refs
        x, y, c = _place()
        sends, recvs = [], []
        for a in range(n):
            g = geoms[a]
            sends.append(_remote(g.final_half(in_refs[a], layer, c), g.final_half(out_refs[a], layer, c), send_sems.at[a],
                                 recv_sems.at[a], (x, y, 1 - c)))
            recvs.append(_remote(g.final_half(in_refs[a], layer, c), g.final_half(out_refs[a], layer, 1 - c),
                                 send_sems.at[a], recv_sems.at[a], (x, y, 1 - c)))
        return [], sends, recvs

    sems = [pltpu.SemaphoreType.DMA((n,)), pltpu.SemaphoreType.DMA((n,))]
    return _Comm(finals, [jax.ShapeDtypeStruct(f.shape, f.dtype) for f in finals], sems, copies,
                 aliases={a: a for a in range(n)})


def _run_comm(comm, name):
    def body(*refs):
        ni, no = len(comm.ins), len(comm.out_shapes)
        in_refs, out_refs, sem_refs = refs[:ni], refs[ni:ni + no], refs[ni + no:]
        comm.start(in_refs, out_refs, sem_refs)
        comm.wait(in_refs, out_refs, sem_refs)

    return pl.pallas_call(
        body, name=name, in_specs=[_ANY] * len(comm.ins), out_specs=[_ANY] * len(comm.out_shapes),
        out_shape=comm.out_shapes, scratch_shapes=comm.sems, input_output_aliases=comm.aliases,
        compiler_params=pltpu.CompilerParams(has_side_effects=True),
    )(*comm.ins)


def _tile_call(body, name, nt, in_specs, out_specs, out_shape, scratch, args, comm):
    if comm is None:
        outs = pl.pallas_call(body, name=name, grid=(nt,), in_specs=in_specs, out_specs=out_specs, out_shape=out_shape,
                              scratch_shapes=scratch, compiler_params=_params("arbitrary"))(*args)
        return outs, []
    n_in, n_out, n_scr = len(in_specs), len(out_specs), len(scratch)
    ci, co = len(comm.ins), len(comm.out_shapes)

    def hosted(*refs):
        in_refs = refs[:n_in]
        cin = refs[n_in:n_in + ci]
        out_refs = refs[n_in + ci:n_in + ci + n_out]
        cout = refs[n_in + ci + n_out:n_in + ci + n_out + co]
        scr = refs[n_in + ci + n_out + co:n_in + ci + n_out + co + n_scr]
        sems = refs[n_in + ci + n_out + co + n_scr:]
        i = pl.program_id(0)

        @pl.when(i == 0)
        def _():
            comm.start(cin, cout, sems)

        body(*in_refs, *out_refs, *scr)

        @pl.when(i == nt - 1)
        def _():
            comm.wait(cin, cout, sems)

    outs = pl.pallas_call(
        hosted, name=name + "_comm", grid=(nt,),
        in_specs=list(in_specs) + [_ANY] * ci, out_specs=list(out_specs) + [_ANY] * co,
        out_shape=list(out_shape) + comm.out_shapes, scratch_shapes=list(scratch) + comm.sems,
        input_output_aliases={n_in + i: n_out + o for i, o in comm.aliases.items()},
        compiler_params=_params("arbitrary"),
    )(*args, *comm.ins)
    return outs[:n_out], outs[n_out:]


def mix_fwd(x, g_mix, w_in, w_out, wcat, bmat, ln_g, ln_b, avg, conv_w, pool_bd, pool_scale, *, tm, comm=None):
    t = x.shape[0]
    nt = t // tm

    def body(x_ref, g_ref, win_ref, wout_ref, wcat_ref, bmat_ref, lng_ref, lnb_ref, avg_ref, cw_ref, pw_ref, ps_ref,
             proj_ref, ycat_ref, x1_ref, hbuf, zbuf):
        i = pl.program_id(0)

        @pl.when(i == 0)
        def _():
            hbuf[0:HALO, :] = jnp.zeros((HALO, D_B), F32)
            zbuf[0:HALO, :] = jnp.zeros((HALO, D_C), F32)

        xv = x_ref[...]
        n = xv * lax.rsqrt(jnp.mean(xv * xv, axis=-1, keepdims=True) + RMS_EPS)
        h1 = (n * g_ref[...]).astype(BF16)
        proj_ref[...] = _dot(h1, win_ref[...])

        lo_mask = _lane_lt((CHUNK, CHUNK), HEAD_DIM)
        avg = avg_ref[...]
        gu = _gelu(proj_ref[:, 0:D_A])
        gv = _gelu(proj_ref[:, D_A:2 * D_A])
        dv = gv - _group_mean(gv, avg)
        var = _group_mean(dv * dv, avg)
        vnb = (dv * lax.rsqrt(var + LN_EPS) * lng_ref[...] + lnb_ref[...]).astype(BF16)
        for c in range(tm // CHUNK):
            rows = slice(c * CHUNK, (c + 1) * CHUNK)
            for j in range(3):
                cols = slice(j * CHUNK, (j + 1) * CHUNK)
                mixed = _sgu_mix(vnb[rows, cols], wcat_ref[j], lo_mask) + bmat_ref[:, cols]
                ycat_ref[rows, cols] = (gu[rows, cols] * mixed).astype(BF16)

        o = 2 * D_A
        hcur = proj_ref[:, o + 2 * D_B:o + 3 * D_B] * proj_ref[:, o:o + D_B]
        hbuf[HALO:HALO + tm, :] = hcur
        y = (cw_ref[2:3, :] * hcur + cw_ref[1:2, :] * hbuf[pl.ds(HALO - 1, tm), :]
             + cw_ref[0:1, :] * hbuf[pl.ds(HALO - 2, tm), :])
        ycat_ref[:, D_A:D_A + D_B] = (proj_ref[:, o + D_B:o + 2 * D_B] * y).astype(BF16)
        hbuf[0:HALO, :] = hbuf[tm:tm + HALO, :]

        zc = proj_ref[:, o + 3 * D_B:D_IN]
        zbuf[HALO:HALO + tm, :] = zc
        mean, _ = _pool_means(zbuf[...], tm, i * tm)
        pooled = (mean - zc).astype(BF16)
        ycat_ref[:, D_A + D_B:D_MODEL] = (_dot(pooled, pw_ref[...]) * ps_ref[...]).astype(BF16)
        zbuf[0:HALO, :] = zbuf[tm:tm + HALO, :]

        x1_ref[...] = xv + _dot(ycat_ref[...], wout_ref[...])

    row = lambda w: pl.BlockSpec((tm, w), lambda i: (i, 0))
    return _tile_call(
        body, "mix_fwd", nt,
        [row(D_MODEL), _const_spec((1, D_MODEL)), _const_spec((D_MODEL, D_IN)), _const_spec((D_MODEL, D_MODEL)),
         _const_spec((3, CHUNK, 2 * CHUNK)), _const_spec((CHUNK, D_A)), _const_spec((1, D_A)), _const_spec((1, D_A)),
         _const_spec((D_A, D_A)), _const_spec((8, D_B)), _const_spec((D_C, D_C)), _const_spec((1, D_C))],
        [row(D_IN), row(D_MODEL), row(D_MODEL)],
        [jax.ShapeDtypeStruct((t, D_IN), F32), jax.ShapeDtypeStruct((t, D_MODEL), BF16),
         jax.ShapeDtypeStruct((t, D_MODEL), F32)],
        [pltpu.VMEM((tm + HALO, D_B), F32), pltpu.VMEM((tm + HALO, D_C), F32)],
        (x, g_mix, w_in, w_out, wcat, bmat, ln_g, ln_b, avg, conv_w, pool_bd, pool_scale), comm)


def ffn_ple_fwd(x1, p, g_ff, w_ff1, w_ff2, g_ple, w_gate, w_proj, *, tm, comm=None):
    t = x1.shape[0]
    nt = t // tm
    nc = D_FF // D_MODEL

    def body(x1_ref, p_ref, gff_ref, w1_ref, w2_ref, gple_ref, wg_ref, wp_ref, a_ref, x2_ref, x3_ref):
        x1v = x1_ref[...]
        n2 = x1v * lax.rsqrt(jnp.mean(x1v * x1v, axis=-1, keepdims=True) + RMS_EPS)
        h2 = (n2 * gff_ref[...]).astype(BF16)
        acc = x1v
        for c in range(nc):
            cols = slice(c * D_MODEL, (c + 1) * D_MODEL)
            a = _dot(h2, w1_ref[:, cols])
            a_ref[:, cols] = a.astype(BF16)
            ra = jnp.maximum(a, 0.0)
            acc = acc + _dot((ra * ra).astype(BF16), w2_ref[cols, :])
        x2_ref[...] = acc
        n3 = acc * lax.rsqrt(jnp.mean(acc * acc, axis=-1, keepdims=True) + RMS_EPS)
        h3 = (n3 * gple_ref[...]).astype(BF16)
        gate = jax.nn.sigmoid(_dot(h3, wg_ref[...]))
        pp = _dot(p_ref[...].astype(BF16), wp_ref[...])
        x3_ref[...] = acc + pp * gate

    row = lambda w: pl.BlockSpec((tm, w), lambda i: (i, 0))
    return _tile_call(
        body, "ffn_ple_fwd", nt,
        [row(D_MODEL), _layer_rows(p[1], tm), _const_spec((1, D_MODEL)), _const_spec((D_MODEL, D_FF)),
         _const_spec((D_FF, D_MODEL)), _const_spec((1, D_MODEL)), _const_spec((D_MODEL, D_MODEL)),
         _const_spec((D_PLE, D_MODEL))],
        [row(D_FF), row(D_MODEL), row(D_MODEL)],
        [jax.ShapeDtypeStruct((t, D_FF), BF16), jax.ShapeDtypeStruct((t, D_MODEL), F32),
         jax.ShapeDtypeStruct((t, D_MODEL), F32)],
        [], (x1, p[0], g_ff, w_ff1, w_ff2, g_ple, w_gate, w_proj), comm)


def loss_head(x, target, g, *, tm):
    t = x.shape[0]
    nt = t // tm

    def body(x_ref, t_ref, g_ref, loss_ref, dg_ref, dx_ref, sq_acc):
        i = pl.program_id(0)

        @pl.when(i == 0)
        def _():
            sq_acc[...] = jnp.zeros_like(sq_acc)
            dg_ref[...] = jnp.zeros_like(dg_ref)

        xv = x_ref[...]
        rs = lax.rsqrt(jnp.mean(xv * xv, axis=-1, keepdims=True) + RMS_EPS)
        n = xv * rs
        gv = g_ref[...]
        err = n * gv - t_ref[...]
        sq_acc[...] += jnp.sum(err * err, axis=0, keepdims=True)
        dy = err * (1.0 / D_MODEL)
        dg_ref[...] += jnp.sum(dy * n, axis=0, keepdims=True)
        dx_ref[...] = _rms_bwd(dy, n, rs, gv)

        @pl.when(i == nt - 1)
        def _():
            total = jnp.sum(sq_acc[...], axis=1, keepdims=True) * (0.5 / D_MODEL)
            loss_ref[...] = jnp.broadcast_to(total, loss_ref.shape)

    row = pl.BlockSpec((tm, D_MODEL), lambda i: (i, 0))
    return pl.pallas_call(
        body, name="loss_head", grid=(nt,),
        in_specs=[row, row, _const_spec((1, D_MODEL))],
        out_specs=[_acc_spec((8, 128)), _acc_spec((1, D_MODEL)), row],
        out_shape=[jax.ShapeDtypeStruct((8, 128), F32), jax.ShapeDtypeStruct((1, D_MODEL), F32),
                   jax.ShapeDtypeStruct((t, D_MODEL), F32)],
        scratch_shapes=[pltpu.VMEM((1, D_MODEL), F32)],
        compiler_params=_params("arbitrary"),
    )(x, target, g)


def ple_bwd(d, x2, p, g_ple, w_gate, w_proj, *, tm, comm=None):
    t = d.shape[0]
    nt = t // tm

    def body(d_ref, x2_ref, p_ref, g_ref, wg_ref, wp_ref, dx2_ref, dg_ref, dwg_ref, dwp_ref):
        i = pl.program_id(0)

        @pl.when(i == 0)
        def _():
            dg_ref[...] = jnp.zeros_like(dg_ref)
            dwg_ref[...] = jnp.zeros_like(dwg_ref)
            dwp_ref[...] = jnp.zeros_like(dwp_ref)

        dv = d_ref[...]
        x2v = x2_ref[...]
        rs = lax.rsqrt(jnp.mean(x2v * x2v, axis=-1, keepdims=True) + RMS_EPS)
        n3 = x2v * rs
        gv = g_ref[...]
        h3 = (n3 * gv).astype(BF16)
        gate = jax.nn.sigmoid(_dot(h3, wg_ref[...]))
        pb = p_ref[...].astype(BF16)
        pp = _dot(pb, wp_ref[...])
        dwp_ref[...] += _dot_tn(pb, (dv * gate).astype(BF16))
        dpre = (dv * pp * gate * (1.0 - gate)).astype(BF16)
        dwg_ref[...] += _dot_tn(h3, dpre)
        dh3 = _dot_nt(dpre, wg_ref[...])
        dg_ref[...] += jnp.sum(dh3 * n3, axis=0, keepdims=True)
        dx2_ref[...] = dv + _rms_bwd(dh3, n3, rs, gv)

    row = lambda w: pl.BlockSpec((tm, w), lambda i: (i, 0))
    return _tile_call(
        body, "ple_bwd", nt,
        [row(D_MODEL), row(D_MODEL), _layer_rows(p[1], tm), _const_spec((1, D_MODEL)), _const_spec((D_MODEL, D_MODEL)),
         _const_spec((D_PLE, D_MODEL))],
        [row(D_MODEL), _acc_spec((1, D_MODEL)), _acc_spec((D_MODEL, D_MODEL)), _acc_spec((D_PLE, D_MODEL))],
        [jax.ShapeDtypeStruct((t, D_MODEL), F32), jax.ShapeDtypeStruct((1, D_MODEL), F32),
         jax.ShapeDtypeStruct((D_MODEL, D_MODEL), F32), jax.ShapeDtypeStruct((D_PLE, D_MODEL), F32)],
        [], (d, x2, p[0], g_ple, w_gate, w_proj), comm)


def ffn_bwd(dx2, x1, a, g_ff, w_ff1, w_ff2, *, tm, comm=None):
    t = dx2.shape[0]
    nt = t // tm
    nc = D_FF // D_MODEL

    def body(dx2_ref, x1_ref, a_ref, g_ref, w1_ref, w2_ref, dx1_ref, h2_ref, da_ref, dg_ref):
        i = pl.program_id(0)

        @pl.when(i == 0)
        def _():
            dg_ref[...] = jnp.zeros_like(dg_ref)

        dv = dx2_ref[...]
        x1v = x1_ref[...]
        rs = lax.rsqrt(jnp.mean(x1v * x1v, axis=-1, keepdims=True) + RMS_EPS)
        n2 = x1v * rs
        gv = g_ref[...]
        h2_ref[...] = (n2 * gv).astype(BF16)
        dvb = dv.astype(BF16)
        dh2 = jnp.zeros((tm, D_MODEL), F32)
        for c in range(nc):
            cols = slice(c * D_MODEL, (c + 1) * D_MODEL)
            ra = jnp.maximum(a_ref[:, cols].astype(F32), 0.0)
            da = (_dot_nt(dvb, w2_ref[cols, :]) * (2.0 * ra)).astype(BF16)
            da_ref[:, cols] = da
            dh2 = dh2 + _dot_nt(da, w1_ref[:, cols])
        dg_ref[...] += jnp.sum(dh2 * n2, axis=0, keepdims=True)
        dx1_ref[...] = dv + _rms_bwd(dh2, n2, rs, gv)

    row = lambda w: pl.BlockSpec((tm, w), lambda i: (i, 0))
    return _tile_call(
        body, "ffn_bwd", nt,
        [row(D_MODEL), row(D_MODEL), row(D_FF), _const_spec((1, D_MODEL)), _const_spec((D_MODEL, D_FF)),
         _const_spec((D_FF, D_MODEL))],
        [row(D_MODEL), row(D_MODEL), row(D_FF), _acc_spec((1, D_MODEL))],
        [jax.ShapeDtypeStruct((t, D_MODEL), F32), jax.ShapeDtypeStruct((t, D_MODEL), BF16),
         jax.ShapeDtypeStruct((t, D_FF), BF16), jax.ShapeDtypeStruct((1, D_MODEL), F32)],
        [], (dx2, x1, a, g_ff, w_ff1, w_ff2), comm)


def mix_bwd(dx1, x, proj, g_mix, w_in, w_out, wcat, wcat_t, bmat, ln_g, ln_b, avg, conv_w, pool_bd, pool_bd_t,
            pool_scale, *, tm, comm=None):
    t = dx1.shape[0]
    nt = t // tm
    prev_blocks = tm // HALO

    def body(dx1_ref, x_ref, proj_ref, prev_ref, g_ref, win_ref, wout_ref, wcat_ref, wcatt_ref, bmat_ref, lng_ref,
             lnb_ref, avg_ref, cw_ref, pw_ref, pwt_ref, ps_ref,
             dx_ref, h1_ref, dproj_ref, dg_ref, dws_ref, dbm_ref, dlng_ref, dlnb_ref, dcw_ref, dpw_ref, dps_ref,
             dyc, dpj, hbuf, zbuf, dybuf, qbuf):
        i = pl.program_id(0)
        ti = nt - 1 - i

        @pl.when(i == 0)
        def _():
            for ref in (dg_ref, dws_ref, dbm_ref, dlng_ref, dlnb_ref, dcw_ref, dpw_ref, dps_ref):
                ref[...] = jnp.zeros_like(ref)
            dybuf[tm:tm + HALO, :] = jnp.zeros((HALO, D_B), F32)
            qbuf[tm:tm + HALO, :] = jnp.zeros((HALO, D_C), F32)

        dx1v = dx1_ref[...]
        dyc[...] = _dot_nt(dx1v.astype(BF16), wout_ref[...])

        lo_mask = _lane_lt((CHUNK, CHUNK), HEAD_DIM)
        avg = avg_ref[...]
        lng = lng_ref[...]
        gu, dgu = _gelu_and_grad(proj_ref[:, 0:D_A])
        gv, dgv = _gelu_and_grad(proj_ref[:, D_A:2 * D_A])
        cen = gv - _group_mean(gv, avg)
        rstd = lax.rsqrt(_group_mean(cen * cen, avg) + LN_EPS)
        vhat = cen * rstd
        vnb = (vhat * lng + lnb_ref[...]).astype(BF16)
        dya = dyc[:, 0:D_A]
        dm = dya * gu
        dmb = dm.astype(BF16)
        dvn_rows = []
        for c in range(tm // CHUNK):
            rows = slice(c * CHUNK, (c + 1) * CHUNK)
            dbm_ref[...] += dm[rows]
            dvn_parts = []
            for j in range(3):
                cols = slice(j * CHUNK, (j + 1) * CHUNK)
                vnb2 = vnb[rows, cols]
                mixed = _sgu_mix(vnb2, wcat_ref[j], lo_mask) + bmat_ref[:, cols]
                dpj[rows, cols] = dya[rows, cols] * mixed * dgu[rows, cols]
                dmb2 = dmb[rows, cols]
                zero = jnp.zeros_like(dmb2)
                dm_st = jnp.concatenate([jnp.where(lo_mask, dmb2, zero), jnp.where(lo_mask, zero, dmb2)], axis=0)
                dws_ref[j] += _dot_nt(dm_st, vnb2)
                dvn_st = _dot(wcatt_ref[j], dmb2)
                dvn_parts.append(jnp.where(lo_mask, dvn_st[0:CHUNK], dvn_st[CHUNK:2 * CHUNK]))
            dvn_rows.append(jnp.concatenate(dvn_parts, axis=1))
        dvn = jnp.concatenate(dvn_rows, axis=0)
        dlng_ref[...] += jnp.sum(dvn * vhat, axis=0, keepdims=True)
        dlnb_ref[...] += jnp.sum(dvn, axis=0, keepdims=True)
        dvh = dvn * lng
        dgv_in = rstd * (dvh - _group_mean(dvh, avg) - vhat * _group_mean(dvh * vhat, avg))
        dpj[:, D_A:2 * D_A] = dgv_in * dgv

        o = 2 * D_A
        live = (ti > 0).astype(F32)
        zb = proj_ref[:, o:o + D_B]
        gb = proj_ref[:, o + D_B:o + 2 * D_B]
        gc = proj_ref[:, o + 2 * D_B:o + 3 * D_B]
        hcur = gc * zb
        hbuf[0:HALO, :] = prev_ref[:, o + 2 * D_B:o + 3 * D_B] * prev_ref[:, o:o + D_B] * live
        hbuf[HALO:HALO + tm, :] = hcur
        hm1 = hbuf[pl.ds(HALO - 1, tm), :]
        hm2 = hbuf[pl.ds(HALO - 2, tm), :]
        y = cw_ref[2:3, :] * hcur + cw_ref[1:2, :] * hm1 + cw_ref[0:1, :] * hm2
        dout = dyc[:, D_A:D_A + D_B]
        dpj[:, o + D_B:o + 2 * D_B] = dout * y
        dy = dout * gb
        dcw_ref[2:3, :] += jnp.sum(dy * hcur, axis=0, keepdims=True)
        dcw_ref[1:2, :] += jnp.sum(dy * hm1, axis=0, keepdims=True)
        dcw_ref[0:1, :] += jnp.sum(dy * hm2, axis=0, keepdims=True)
        dybuf[0:tm, :] = dy
        dh = (cw_ref[2:3, :] * dy + cw_ref[1:2, :] * dybuf[pl.ds(1, tm), :] + cw_ref[0:1, :] * dybuf[pl.ds(2, tm), :])
        dybuf[tm:tm + HALO, :] = dybuf[0:HALO, :]
        dpj[:, o:o + D_B] = dh * gc
        dpj[:, o + 2 * D_B:o + 3 * D_B] = dh * zb

        zc = proj_ref[:, o + 3 * D_B:D_IN]
        zbuf[0:HALO, :] = prev_ref[:, o + 3 * D_B:D_IN] * live
        zbuf[HALO:HALO + tm, :] = zc
        mean, inv = _pool_means(zbuf[...], tm, ti * tm)
        pooled = (mean - zc).astype(BF16)
        dyp = dyc[:, D_A + D_B:D_MODEL]
        ps = ps_ref[...]
        dps_ref[...] += jnp.sum(dyp * _dot(pooled, pw_ref[...]), axis=0, keepdims=True)
        dpw = (dyp * ps).astype(BF16)
        dpw_ref[...] += _dot_tn(pooled, dpw)
        dpooled = _dot(dpw, pwt_ref[...])
        qbuf[0:tm, :] = dpooled * inv
        q = qbuf[...]
        nrows = tm + HALO
        f2 = q + pltpu.roll(q, nrows - 1, 0)
        f4 = f2 + pltpu.roll(f2, nrows - 2, 0)
        f8 = f4 + pltpu.roll(f4, nrows - 4, 0)
        f16 = f8 + pltpu.roll(f8, nrows - 8, 0)
        lane = lax.broadcasted_iota(jnp.int32, (tm, D_C), 1)
        ahead = jnp.where(lane < 64, f2[0:tm], jnp.where(lane < 128, f4[0:tm], jnp.where(lane < 192, f8[0:tm], f16[0:tm])))
        dpj[:, o + 3 * D_B:D_IN] = ahead - dpooled
        qbuf[tm:tm + HALO, :] = qbuf[0:HALO, :]

        dprojb = dpj[...].astype(BF16)
        dproj_ref[...] = dprojb
        dh1 = _dot_nt(dprojb, win_ref[...])
        xv = x_ref[...]
        rs = lax.rsqrt(jnp.mean(xv * xv, axis=-1, keepdims=True) + RMS_EPS)
        n1 = xv * rs
        gv1 = g_ref[...]
        h1_ref[...] = (n1 * gv1).astype(BF16)
        dg_ref[...] += jnp.sum(dh1 * n1, axis=0, keepdims=True)
        dx_ref[...] = dx1v + _rms_bwd(dh1, n1, rs, gv1)

        @pl.when(i == nt - 1)
        def _():
            tril = (lax.broadcasted_iota(jnp.int32, (2 * CHUNK, CHUNK), 0) % CHUNK
                    >= lax.broadcasted_iota(jnp.int32, (2 * CHUNK, CHUNK), 1))
            for j in range(3):
                dws_ref[j] = jnp.where(tril, dws_ref[j], 0.0)
            dbm_ref[...] = _group_mean_split(dbm_ref[...], avg) * float(HEAD_DIM)

    rev = lambda w: pl.BlockSpec((tm, w), lambda i: (nt - 1 - i, 0))
    prev = pl.BlockSpec((HALO, D_IN), lambda i: (jnp.maximum((nt - 1 - i) * prev_blocks - 1, 0), 0))
    acc_shapes = [(1, D_MODEL), (3, 2 * CHUNK, CHUNK), (CHUNK, D_A), (1, D_A), (1, D_A), (8, D_B), (D_C, D_C), (1, D_C)]
    return _tile_call(
        body, "mix_bwd", nt,
        [rev(D_MODEL), rev(D_MODEL), rev(D_IN), prev, _const_spec((1, D_MODEL)), _const_spec((D_MODEL, D_IN)),
         _const_spec((D_MODEL, D_MODEL)), _const_spec((3, CHUNK, 2 * CHUNK)), _const_spec((3, 2 * CHUNK, CHUNK)),
         _const_spec((CHUNK, D_A)), _const_spec((1, D_A)), _const_spec((1, D_A)), _const_spec((D_A, D_A)),
         _const_spec((8, D_B)), _const_spec((D_C, D_C)), _const_spec((D_C, D_C)), _const_spec((1, D_C))],
        [rev(D_MODEL), rev(D_MODEL), rev(D_IN)] + [_acc_spec(s) for s in acc_shapes],
        [jax.ShapeDtypeStruct((t, D_MODEL), F32), jax.ShapeDtypeStruct((t, D_MODEL), BF16),
         jax.ShapeDtypeStruct((t, D_IN), BF16)] + [jax.ShapeDtypeStruct(s, F32) for s in acc_shapes],
        [pltpu.VMEM((tm, D_MODEL), F32), pltpu.VMEM((tm, D_IN), F32),
         pltpu.VMEM((tm + HALO, D_B), F32), pltpu.VMEM((tm + HALO, D_C), F32),
         pltpu.VMEM((tm + HALO, D_B), F32), pltpu.VMEM((tm + HALO, D_C), F32)],
        (dx1, x, proj, proj, g_mix, w_in, w_out, wcat, wcat_t, bmat, ln_g, ln_b, avg, conv_w, pool_bd, pool_bd_t,
         pool_scale), comm)


def wgrad(a, b, *, tk, a_layer=None, relu_sq=False):
    t, m = a.shape[-2:]
    n = b.shape[1]
    bm = min(m, 1024)
    bn = 1024 if n % 1024 == 0 else n
    nk = t // tk
    if a_layer is None:
        a_spec = pl.BlockSpec((tk, bm), lambda i, j, k: (k, i))
    else:
        a_spec = pl.BlockSpec((None, None, tk, bm), lambda i, j, k: (a_layer, 0, k, i))

    def body(a_ref, b_ref, o_ref):
        k = pl.program_id(2)

        @pl.when(k == 0)
        def _():
            o_ref[...] = jnp.zeros_like(o_ref)

        av = a_ref[...]
        if relu_sq:
            ra = jnp.maximum(av.astype(F32), 0.0)
            av = ra * ra
        o_ref[...] += _dot_tn(av.astype(BF16), b_ref[...].astype(BF16))

    return pl.pallas_call(
        body, name=f"wgrad_{m}x{n}" + ("_relu_sq" if relu_sq else ""), grid=(m // bm, n // bn, nk),
        in_specs=[a_spec, pl.BlockSpec((tk, bn), lambda i, j, k: (k, j))],
        out_specs=pl.BlockSpec((bm, bn), lambda i, j, k: (i, j)),
        out_shape=jax.ShapeDtypeStruct((m, n), F32),
        compiler_params=_params("parallel", "parallel", "arbitrary"),
    )(a, b)


def _row_block(rows, cols, target_bytes):
    target = max(8, target_bytes // (4 * cols))
    if rows <= target:
        return rows
    best = None
    for br in range(8, target + 1, 8):
        if rows % br == 0:
            best = br
    return best if best is not None else rows


def adamw(w, g, m, v):
    shape = w.shape
    cols = shape[-1]
    rows = math.prod(shape[:-1]) if len(shape) > 1 else 1
    br = _row_block(rows, cols, 1 << 20)

    def body(w_ref, g_ref, m_ref, v_ref, d_ref, nm_ref, nv_ref):
        gv = g_ref[...]
        nm = ADAM_B1 * m_ref[...] + (1.0 - ADAM_B1) * gv
        nv = ADAM_B2 * v_ref[...] + (1.0 - ADAM_B2) * jnp.square(gv)
        m_hat = nm / (1.0 - ADAM_B1 ** ADAM_STEP)
        v_hat = nv / (1.0 - ADAM_B2 ** ADAM_STEP)
        d_ref[...] = -ADAM_LR * (m_hat / (jnp.sqrt(v_hat) + ADAM_EPS) + ADAM_WD * w_ref[...])
        nm_ref[...] = nm
        nv_ref[...] = nv

    spec = pl.BlockSpec((br, cols), lambda i: (i, 0))
    outs = pl.pallas_call(
        body, name="adamw", grid=(rows // br,),
        in_specs=[spec] * 4, out_specs=[spec] * 3,
        out_shape=[jax.ShapeDtypeStruct((rows, cols), F32)] * 3,
        compiler_params=pltpu.CompilerParams(dimension_semantics=("parallel",)),
    )(*(a.reshape(rows, cols) for a in (w, g, m, v)))
    return tuple(o.reshape(shape) for o in outs)


ADD_STEPS = 4


def add_halves(geoms, arrs, received, c_idx, dtypes):
    n = len(arrs)

    def body(c_ref, *refs):
        del c_ref
        for a in range(n):
            refs[2 * n + a][...] = (refs[a][...] + refs[n + a][...]).astype(dtypes[a])

    own_specs, half_specs = [], []
    for g in geoms:
        if g.kind == "cols":
            rows, cols = g.half_shape[0] // ADD_STEPS, g.half_shape[1]
            own_specs.append(pl.BlockSpec((rows, cols), lambda i, c_ref: (ADD_STEPS * c_ref[0] + i, 0)))
            half_specs.append(pl.BlockSpec((rows, cols), lambda i, c_ref: (i, 0)))
        else:
            _, h, cols = g.half_shape
            own_specs.append(pl.BlockSpec((None, None, h, cols), lambda i, c_ref: (i, c_ref[0], 0, 0)))
            half_specs.append(pl.BlockSpec((None, h, cols), lambda i, c_ref: (i, 0, 0)))
    return pl.pallas_call(
        body, name="add_halves",
        grid_spec=pltpu.PrefetchScalarGridSpec(num_scalar_prefetch=1, grid=(ADD_STEPS,),
                                               in_specs=own_specs + half_specs, out_specs=half_specs),
        out_shape=[jax.ShapeDtypeStruct(g.half_shape, dt) for g, dt in zip(geoms, dtypes)],
        compiler_params=_params("parallel"),
    )(c_idx, *arrs, *received)


def add_parts(geoms, landed, finals, layer, c_idx):
    n = len(landed)

    def body(c_ref, *refs):
        del c_ref
        for a in range(n):
            p_ref = refs[a]
            parts = [p_ref[j].astype(F32) for j in range(N_CHIPS)]
            refs[2 * n + a][...] = ((parts[0] + parts[1]) + parts[2]) + parts[3]

    in_specs, out_specs = [], []
    for g in geoms:
        rows, cols = g.part_shape[0] // ADD_STEPS, g.part_shape[1]
        in_specs.append(pl.BlockSpec((N_CHIPS, rows, cols), lambda i, c_ref: (0, i, 0)))
        out_specs.append(pl.BlockSpec((None, rows, cols), lambda i, c_ref: (layer, ADD_STEPS * c_ref[0] + i, 0)))
    return pl.pallas_call(
        body, name="add_parts",
        grid_spec=pltpu.PrefetchScalarGridSpec(num_scalar_prefetch=1, grid=(ADD_STEPS,),
                                               in_specs=in_specs + [_ANY] * n, out_specs=out_specs),
        out_shape=[jax.ShapeDtypeStruct(f.shape, F32) for f in finals],
        input_output_aliases={1 + n + a: a for a in range(n)},
        compiler_params=_params("parallel"),
    )(c_idx, *landed, *finals)


def _shard_dims(k, n, axis):
    return (k // N_CHIPS, n) if axis == 0 else (k, n // N_CHIPS)


W_IN_STRIDE = 512
W_IN_WINDOW = 640


def _big_geoms():
    geoms = []
    for name, k, n, axis in BIG:
        if axis == 0:
            geoms.append(_Geom("rows", (N_CHIPS, 2, k // N_CHIPS // 2, n)))
        elif name == "w_in":
            geoms.append(_Geom("cols", (k, n), W_IN_STRIDE, W_IN_WINDOW))
        else:
            geoms.append(_Geom("cols", (k, n), n // N_CHIPS, n // N_CHIPS))
    return geoms


def _grad_views(gb, geoms):
    return [gb[name].reshape(g.shape) for (name, _, _, _), g in zip(BIG, geoms)]


def _round_up(v, m):
    return (v + m - 1) // m * m


def _prep_small(small):
    tril = jnp.tril(jnp.ones((CHUNK, CHUNK), bool))
    wm = jnp.where(tril, small["sgu_w"], 0.0).astype(BF16).reshape(DEPTH, 3, 2, CHUNK, CHUNK)
    head = jnp.arange(D_A) // HEAD_DIM
    grp = jnp.arange(D_C) // HEAD_DIM
    pw_rows = small["pool_w"].reshape(DEPTH, D_C, HEAD_DIM)
    pool_bd = jnp.where((grp[:, None] == grp[None, :])[None], jnp.tile(pw_rows, (1, 1, D_C // HEAD_DIM)), 0.0).astype(BF16)
    return dict(
        wcat=wm.transpose(0, 1, 3, 2, 4).reshape(DEPTH, 3, CHUNK, 2 * CHUNK),
        wcat_t=wm.transpose(0, 1, 2, 4, 3).reshape(DEPTH, 3, 2 * CHUNK, CHUNK),
        bmat=jnp.repeat(jnp.swapaxes(small["sgu_b"], 1, 2), HEAD_DIM, axis=2),
        avg=jnp.where(head[:, None] == head[None, :], 1.0 / HEAD_DIM, 0.0).astype(BF16),
        pool_bd=pool_bd, pool_bd_t=jnp.swapaxes(pool_bd, 1, 2),
        conv8=jnp.pad(small["conv_w"], ((0, 0), (0, 8 - 3), (0, 0))),
    )


def _row(a):
    return a.reshape(1, -1)


MIX_WEIGHTS = ("w_in", "w_out")
MLP_WEIGHTS = ("w_ff1", "w_ff2", "w_ple_gate", "w_ple_proj")
ALL_BIG = MIX_WEIGHTS + MLP_WEIGHTS
EARLY_GRADS = MLP_WEIGHTS + ("w_out",)
FFN_BWD_TILE = 512
PLE_BWD_TILE = 1024


def _fwd_layer(h, p, wl, small, prep, l, tm, comm_mix=None, comm_mlp=None):
    (proj, ycat, x1), got = mix_fwd(h, _row(small["norm_mix_g"][l]), wl["w_in"], wl["w_out"], prep["wcat"][l],
                                    prep["bmat"][l], _row(small["sgu_ln_g"][l]), _row(small["sgu_ln_b"][l]), prep["avg"],
                                    prep["conv8"][l], prep["pool_bd"][l], _row(small["pool_scale"][l]),
                                    tm=min(2 * tm, h.shape[0]), comm=comm_mix)
    if comm_mix is not None:
        wl = {**wl, **_weights_of(got, MLP_WEIGHTS)}
    (a, x2, x3), couts = ffn_ple_fwd(x1, (p, l), _row(small["norm_ff_g"][l]), wl["w_ff1"], wl["w_ff2"],
                                     _row(small["norm_ple_g"][l]), wl["w_ple_gate"], wl["w_ple_proj"], tm=tm,
                                     comm=comm_mlp)
    return (h, proj, ycat, x1, a, x2), x3, couts, wl


def _merge_comms(comms):
    comms = [cm for cm in comms if cm is not None]
    if len(comms) <= 1:
        return comms[0] if comms else None
    spans, ni, no, ns = [], 0, 0, 0
    for cm in comms:
        spans.append((ni, no, ns))
        ni, no, ns = ni + len(cm.ins), no + len(cm.out_shapes), ns + len(cm.sems)

    def copies(in_refs, out_refs, sem_refs):
        local, sends, recvs = [], [], []
        for cm, (i0, o0, s0) in zip(comms, spans):
            got = cm.copies(in_refs[i0:i0 + len(cm.ins)], out_refs[o0:o0 + len(cm.out_shapes)],
                            sem_refs[s0:s0 + len(cm.sems)])
            local, sends, recvs = local + got[0], sends + got[1], recvs + got[2]
        return local, sends, recvs

    aliases = {i0 + i: o0 + o for cm, (i0, o0, _) in zip(comms, spans) for i, o in cm.aliases.items()}
    assert all(cm.forwards is None for cm in comms)
    return _Comm(sum((cm.ins for cm in comms), []), sum((cm.out_shapes for cm in comms), []),
                 sum((cm.sems for cm in comms), []), copies, aliases)


def _split_results(results, comms):
    out, at = [], 0
    for cm in comms:
        if cm is None:
            out.append(None)
        else:
            out.append(results[at:at + len(cm.out_shapes)])
            at += len(cm.out_shapes)
    return out


class _Reduction:
    def __init__(self, layer, names, geoms, arrs, finals, c_arr, narrow=()):
        self.layer, self.names, self.geoms, self.arrs = layer, list(names), list(geoms), list(arrs)
        self.finals, self.c_arr = finals, c_arr
        self.dtypes = [BF16 if n in narrow else F32 for n in self.names]

    def comm_a(self):
        return _reduce_a_comm(self.geoms, self.arrs)

    def comm_b(self, received):
        return _reduce_b_comm(self.geoms, add_halves(self.geoms, self.arrs, received, self.c_arr, self.dtypes))

    def comm_c(self, landed):
        mine = add_parts(self.geoms, landed, [self.finals[n] for n in self.names], self.layer, self.c_arr)
        return _reduce_c_comm(self.geoms, mine, self.layer)

    def done(self, results):
        self.finals.update(zip(self.names, results))


class _Plan:
    def ple(self):
        return None

    def after_ple(self, results):
        pass

    def ffn(self):
        return None

    def after_ffn(self, results):
        pass

    def before_mix(self, gb):
        pass

    def mix(self):
        return None

    def after_mix(self, results):
        pass


class _CarryPlan(_Plan):
    def __init__(self, above):
        self.above = above

    def ple(self):
        return self.above.comm_a()

    def after_ple(self, results):
        self.received = results

    def ffn(self):
        return self.above.comm_b(self.received)

    def after_ffn(self, results):
        self.landed = results

    def mix(self):
        return self.above.comm_c(self.landed)

    def after_mix(self, results):
        self.above.done(results)


class _LastPlan(_CarryPlan):
    def __init__(self, above, make_early):
        super().__init__(above)
        self.make_early = make_early

    def before_mix(self, gb):
        self.early = self.make_early(gb)
        self.early_received = _run_comm(self.early.comm_a(), "reduce_a_early")

    def mix(self):
        self.parts = [self.above.comm_c(self.landed), self.early.comm_b(self.early_received)]
        return _merge_comms(self.parts)

    def after_mix(self, results):
        above_res, self.early_landed = _split_results(results, self.parts)
        self.above.done(above_res)


def _bwd_layer(d, saved, p, wl, small, prep, l, tm, tk, plan=None):
    plan = plan or _Plan()
    xin, proj, ycat, x1, a, x2 = saved
    (dx2, dg_ple, dw_gate, dw_proj), res = ple_bwd(d, x2, (p, l), _row(small["norm_ple_g"][l]), wl["w_ple_gate"],
                                                   wl["w_ple_proj"], tm=min(PLE_BWD_TILE, d.shape[0]), comm=plan.ple())
    plan.after_ple(res)
    gb = {"w_ple_gate": dw_gate, "w_ple_proj": dw_proj}
    (dx1, h2, da, dg_ff), res = ffn_bwd(dx2, x1, a, _row(small["norm_ff_g"][l]), wl["w_ff1"], wl["w_ff2"],
                                        tm=FFN_BWD_TILE if tm >= FFN_BWD_TILE else tm, comm=plan.ffn())
    plan.after_ffn(res)
    gb["w_ff2"] = wgrad(a, dx2, tk=tk, relu_sq=True)
    gb["w_ff1"] = wgrad(h2, da, tk=tk)
    gb["w_out"] = wgrad(ycat, dx1, tk=tk)
    plan.before_mix(gb)
    (dprev, h1, dproj, dg_mix, dws, dbm, dlng, dlnb, dcw, dpw, dps), res = mix_bwd(
        dx1, xin, proj, _row(small["norm_mix_g"][l]), wl["w_in"], wl["w_out"], prep["wcat"][l], prep["wcat_t"][l],
        prep["bmat"][l], _row(small["sgu_ln_g"][l]), _row(small["sgu_ln_b"][l]), prep["avg"], prep["conv8"][l],
        prep["pool_bd"][l], prep["pool_bd_t"][l], _row(small["pool_scale"][l]), tm=tm, comm=plan.mix())
    plan.after_mix(res)
    gb["w_in"] = wgrad(h1, dproj, tk=tk)
    gs = {
        "norm_ple_g": dg_ple[0], "norm_ff_g": dg_ff[0], "norm_mix_g": dg_mix[0],
        "sgu_w": dws.reshape(2 * 3, CHUNK, CHUNK), "sgu_b": dbm[:, ::HEAD_DIM].T,
        "sgu_ln_g": dlng[0], "sgu_ln_b": dlnb[0], "conv_w": dcw[0:3], "pool_scale": dps[0],
        "pool_w": jnp.stack([dpw[g * HEAD_DIM:(g + 1) * HEAD_DIM, g * HEAD_DIM:(g + 1) * HEAD_DIM]
                             for g in range(D_C // HEAD_DIM)]),
    }
    return dprev, gb, gs


def _local_step(x, p, target, full, small, *, tm, tk):
    prep = _prep_small(small)
    p = p[:, None]
    saved, h = [], x
    for l in range(DEPTH):
        wl = {name: full[name][l] for name in full}
        s, h, _, _ = _fwd_layer(h, p, wl, small, prep, l, tm)
        saved.append(s)
    loss_blk, d_final_g, d = loss_head(h, target, _row(small["final_g"]), tm=tm)
    gbig, gsm = [None] * DEPTH, [None] * DEPTH
    for l in reversed(range(DEPTH)):
        wl = {name: full[name][l] for name in full}
        d, gbig[l], gsm[l] = _bwd_layer(d, saved[l], p, wl, small, prep, l, tm, tk)
    big = {name: jnp.stack([gbig[l][name] for l in range(DEPTH)]) for name in gbig[0]}
    sm = {name: jnp.stack([gsm[l][name] for l in range(DEPTH)]) for name in gsm[0]}
    sm["final_g"] = d_final_g[0]
    return loss_blk[0, 0], d, big, sm


def _weights_of(gathered, names):
    wl = dict(zip([b[0] for b in BIG if b[0] in names], gathered))
    if "w_in" in wl:
        wl["w_in"] = wl["w_in"].transpose(1, 0, 2).reshape(D_MODEL, D_IN)
    return wl


def kernel(x, p, norm_mix_g, w_in, sgu_w, sgu_b, sgu_ln_g, sgu_ln_b, conv_w, pool_w, pool_scale, w_out, norm_ff_g, w_ff1, w_ff2, norm_ple_g, w_ple_gate, w_ple_proj, final_g, loss_target, m_norm_mix_g, m_w_in, m_sgu_w, m_sgu_b, m_sgu_ln_g, m_sgu_ln_b, m_conv_w, m_pool_w, m_pool_scale, m_w_out, m_norm_ff_g, m_w_ff1, m_w_ff2, m_norm_ple_g, m_w_ple_gate, m_w_ple_proj, m_final_g, v_norm_mix_g, v_w_in, v_sgu_w, v_sgu_b, v_sgu_ln_g, v_sgu_ln_b, v_conv_w, v_pool_w, v_pool_scale, v_w_out, v_norm_ff_g, v_w_ff1, v_w_ff2, v_norm_ple_g, v_w_ple_gate, v_w_ple_proj, v_final_g):
    args = dict(locals())
    w = {name: args[name] for name in WEIGHTS}
    m = {name: args["m_" + name] for name in WEIGHTS}
    v = {name: args["v_" + name] for name in WEIGHTS}
    t = x.shape[1]
    tm = min(512, t)
    tk = min(2048, t)
    x_idx, y_idx, c_idx = _place()
    chip = 2 * x_idx + y_idx
    c_arr = c_idx.reshape(1).astype(jnp.int32)
    xs, target = x[0], loss_target[0]

    shards = {name: w[name].astype(BF16) for name, _, _, _ in BIG}
    conv_rows = _round_up(CONV_SHARD, 8 * 128) // 128
    conv_flat = jnp.pad(w["conv_w"].reshape(-1), (0, conv_rows * 128 - CONV_SHARD)).reshape(conv_rows, 128)
    first = _run_comm(_gather_halved_comm(shards, 0, MIX_WEIGHTS, conv_flat), "gather_first")
    conv_full = (first[len(MIX_WEIGHTS)].reshape(N_CHIPS, -1)[:, :CONV_SHARD]
                 .reshape(N_CHIPS, DEPTH, 3, D_B // N_CHIPS).transpose(1, 2, 0, 3).reshape(DEPTH, 3, D_B))
    small = {name: w[name] for name in SMALL}
    small["conv_w"] = conv_full
    prep = _prep_small(small)

    wl = [None] * DEPTH
    wl[0] = _weights_of(first, MIX_WEIGHTS)
    saved, h = [], xs
    for l in range(DEPTH):
        comm_mix = None
        if l == 0:
            comm_mix = _gather_halved_comm(shards, 0, MLP_WEIGHTS)
        comm_mlp = _gather_comm(shards, l + 1, ALL_BIG) if l + 1 < DEPTH else None
        s, h, got, wl[l] = _fwd_layer(h, p, wl[l], small, prep, l, tm, comm_mix, comm_mlp)
        saved.append(s)
        if comm_mlp is not None:
            wl[l + 1] = _weights_of(got, ALL_BIG)

    loss_blk, d_final_g, d = loss_head(h, target, _row(small["final_g"]), tm=tm)

    geoms = dict(zip([b[0] for b in BIG], _big_geoms()))
    finals = {name: jnp.zeros((DEPTH,) + g.final_shape, F32) for name, g in geoms.items()}

    def reduction(layer, names, gb, narrow=()):
        return _Reduction(layer, names, [geoms[n] for n in names], [gb[n].reshape(geoms[n].shape) for n in names],
                          finals, c_arr, narrow)

    gsm = [None] * DEPTH
    above = None
    for l in reversed(range(DEPTH)):
        if above is None:
            plan = _Plan()
        elif l > 0:
            plan = _CarryPlan(above)
        else:
            plan = _LastPlan(above, lambda gb: reduction(0, EARLY_GRADS, gb))
        d, gb, gsm[l] = _bwd_layer(d, saved[l], p, wl[l], small, prep, l, tm, tk, plan)
        if l > 0:
            above = reduction(l, ALL_BIG, gb)

    sm = {name: jnp.stack([gsm[i][name] for i in range(DEPTH)]) for name in gsm[0]}
    sm["final_g"] = d_final_g[0]
    sizes = [sm[name].size for name in SMALL]
    small_rows = _round_up(-(-sum(sizes) // (2 * N_CHIPS * LANES)), 8 * ADD_STEPS)
    small_flat = jnp.pad(jnp.concatenate([sm[name].reshape(-1) for name in SMALL]),
                         (0, 2 * N_CHIPS * small_rows * LANES - sum(sizes)))
    geoms["small"] = _Geom("rows", (N_CHIPS, 2, small_rows, LANES))
    finals["small"] = jnp.zeros((1,) + geoms["small"].final_shape, F32)
    late = reduction(0, ("w_in", "small"), {**gb, "small": small_flat}, narrow=("w_in",))
    late_landed = _run_comm(late.comm_b(_run_comm(late.comm_a(), "reduce_a_late")), "reduce_b_late")
    last = [plan.early.comm_c(plan.early_landed), late.comm_c(late_landed)]
    early_res, late_res = _split_results(_run_comm(_merge_comms(last), "reduce_c_last"), last)
    plan.early.done(early_res)
    late.done(late_res)

    grads = {name: finals[name] for name, _, _, _ in BIG}
    grads["w_in"] = lax.dynamic_slice_in_dim(grads["w_in"], chip * (D_IN // N_CHIPS - W_IN_STRIDE), D_IN // N_CHIPS, axis=2)
    small_red = _run_comm(_allgather_comm(finals["small"][0]), "small_allgather")[0].reshape(-1)
    off = 0
    for name, size in zip(SMALL, sizes):
        grads[name] = small_red[off:off + size].reshape(sm[name].shape)
        off += size
    grads["conv_w"] = lax.dynamic_slice_in_dim(grads["conv_w"], chip * (D_B // N_CHIPS), D_B // N_CHIPS, axis=2)

    loss = lax.psum(loss_blk[0, 0], ("x", "y", "c"))
    delta, new_m, new_v = {}, {}, {}
    for name in WEIGHTS:
        delta[name], new_m[name], new_v[name] = adamw(w[name], grads[name], m[name], v[name])
    return (loss, d[None], *[grads[n] for n in WEIGHTS], *[delta[n] for n in WEIGHTS],
            *[new_m[n] for n in WEIGHTS], *[new_v[n] for n in WEIGHTS])
```

```python
import math

import jax
import jax.numpy as jnp
from jax import lax
from jax.experimental import pallas as pl
from jax.experimental.pallas import tpu as pltpu

F32 = jnp.float32
BF16 = jnp.bfloat16

D_MODEL = 1024
DEPTH = 4
D_PLE = 256
D_FF = 4096
HEAD_DIM = 64
D_A = 384
D_B = 384
D_C = 256
D_IN = 2176
CHUNK = 128
HALO = 16
RMS_EPS = 1e-6
LN_EPS = 1e-5
N_CHIPS = 4
LANES = 1024

ADAM_LR = 0.001
ADAM_B1 = 0.9
ADAM_B2 = 0.999
ADAM_EPS = 1e-08
ADAM_WD = 0.01
ADAM_STEP = 10

VMEM_LIMIT_BYTES = 60 * 1024 * 1024

_RSQRT2 = 0.7071067811865476
_INV_SQRT_2PI = 0.3989422804014327

BIG = (
    ("w_in", D_MODEL, D_IN, 1),
    ("w_out", D_MODEL, D_MODEL, 0),
    ("w_ff1", D_MODEL, D_FF, 1),
    ("w_ff2", D_FF, D_MODEL, 0),
    ("w_ple_gate", D_MODEL, D_MODEL, 0),
    ("w_ple_proj", D_PLE, D_MODEL, 1),
)
SMALL = ("norm_mix_g", "sgu_w", "sgu_b", "sgu_ln_g", "sgu_ln_b", "conv_w", "pool_w", "pool_scale",
         "norm_ff_g", "norm_ple_g", "final_g")
WEIGHTS = ("norm_mix_g", "w_in", "sgu_w", "sgu_b", "sgu_ln_g", "sgu_ln_b", "conv_w", "pool_w", "pool_scale",
           "w_out", "norm_ff_g", "w_ff1", "w_ff2", "norm_ple_g", "w_ple_gate", "w_ple_proj", "final_g")
CONV_SHARD = DEPTH * 3 * (D_B // N_CHIPS)


def _dot(a, b):
    return jnp.dot(a, b, preferred_element_type=F32)


def _dot_nt(a, b):
    return lax.dot_general(a, b, (((1,), (1,)), ((), ())), preferred_element_type=F32)


def _dot_tn(a, b):
    return lax.dot_general(a, b, (((0,), (0,)), ((), ())), preferred_element_type=F32)


def _const_spec(shape):
    nd = len(shape)
    return pl.BlockSpec(shape, lambda i: (0,) * nd, pipeline_mode=pl.Buffered(1))


def _acc_spec(shape):
    nd = len(shape)
    return pl.BlockSpec(shape, lambda i: (0,) * nd)


def _layer_rows(layer, tm):
    return pl.BlockSpec((None, None, tm, D_PLE), lambda i: (layer, 0, i, 0))


def _params(*sem):
    return pltpu.CompilerParams(dimension_semantics=sem, vmem_limit_bytes=VMEM_LIMIT_BYTES)


def _rms_bwd(dh, n, rs, g):
    dn = dh * g
    return rs * (dn - n * jnp.mean(dn * n, axis=-1, keepdims=True))


def _gelu(x):
    return x * (0.5 * (1.0 + lax.erf(x * _RSQRT2)))


def _gelu_and_grad(x):
    cdf = 0.5 * (1.0 + lax.erf(x * _RSQRT2))
    return x * cdf, cdf + x * (jnp.exp(-0.5 * x * x) * _INV_SQRT_2PI)


def _group_mean(v, avg):
    vb = v.astype(BF16)
    split = 2 * CHUNK
    return jnp.concatenate([_dot(vb[:, :split], avg[:split, :split]), _dot(vb[:, split:], avg[split:, split:])], axis=1)


def _group_mean_split(v, avg):
    hi = v.astype(BF16)
    lo = (v - hi.astype(F32)).astype(BF16)
    return _dot(hi, avg) + _dot(lo, avg)


def _lane_lt(shape, bound):
    return lax.broadcasted_iota(jnp.int32, shape, 1) < bound


def _sgu_mix(vnb2, wcat_j, lo_mask):
    zero = jnp.zeros_like(vnb2)
    stacked = jnp.concatenate([jnp.where(lo_mask, vnb2, zero), jnp.where(lo_mask, zero, vnb2)], axis=0)
    return _dot(wcat_j, stacked)


def _pool_means(ext, tile_rows, first_pos):
    s2 = ext + pltpu.roll(ext, 1, 0)
    s4 = s2 + pltpu.roll(s2, 2, 0)
    s8 = s4 + pltpu.roll(s4, 4, 0)
    s16 = s8 + pltpu.roll(s8, 8, 0)
    pos = (first_pos + lax.broadcasted_iota(jnp.int32, (tile_rows, 1), 0) + 1).astype(F32)
    lane = lax.broadcasted_iota(jnp.int32, (tile_rows, D_C), 1)
    sums = jnp.where(lane < 64, s2[HALO:], jnp.where(lane < 128, s4[HALO:], jnp.where(lane < 192, s8[HALO:], s16[HALO:])))
    win = jnp.where(lane < 64, 2.0, jnp.where(lane < 128, 4.0, jnp.where(lane < 192, 8.0, 16.0)))
    inv = 1.0 / jnp.minimum(pos, win)
    return sums * inv, inv


MESH = pl.DeviceIdType.MESH
_ANY = pl.BlockSpec(memory_space=pl.ANY)


def _place():
    return lax.axis_index("x"), lax.axis_index("y"), lax.axis_index("c")


def _chip_peers(x, y):
    return [(1 - x, y), (x, 1 - y), (1 - x, 1 - y)]


class _Comm:
    def __init__(self, ins, out_shapes, sems, copies, aliases=None, forwards=None):
        self.ins, self.out_shapes, self.sems, self.copies = list(ins), list(out_shapes), list(sems), copies
        self.aliases = dict(aliases or {})
        self.forwards = forwards

    def start(self, in_refs, out_refs, sem_refs):
        local, sends, _ = self.copies(in_refs, out_refs, sem_refs)
        for cp in local + sends:
            cp.start()

    def wait(self, in_refs, out_refs, sem_refs):
        local, sends, recvs = self.copies(in_refs, out_refs, sem_refs)
        for cp in recvs:
            cp.wait_recv()
        passed, passed_in = self.forwards(in_refs, out_refs, sem_refs) if self.forwards else ([], [])
        for cp in passed:
            cp.start()
        for cp in sends:
            cp.wait_send()
        for cp in local:
            cp.wait()
        for cp in passed_in:
            cp.wait_recv()
        for cp in passed:
            cp.wait_send()


def _remote(src, dst, send_sem, recv_sem, device):
    return pltpu.make_async_remote_copy(src_ref=src, dst_ref=dst, send_sem=send_sem, recv_sem=recv_sem,
                                        device_id=device, device_id_type=MESH)


def _gather_comm(shards, layer, names, conv=None):
    mats = [b for b in BIG if b[0] in names]
    ins = [shards[name] for name, _, _, _ in mats] + ([conv] if conv is not None else [])
    out_shapes = []
    for name, k, n, axis in mats:
        shape = (N_CHIPS, k, n // N_CHIPS) if name == "w_in" else (k, n)
        out_shapes.append(jax.ShapeDtypeStruct(shape, BF16))
    if conv is not None:
        out_shapes.append(jax.ShapeDtypeStruct((N_CHIPS,) + conv.shape, conv.dtype))
    n_arr = len(ins)

    def block(a, out_ref, chip):
        if a == len(mats) or mats[a][0] == "w_in":
            return out_ref.at[chip]
        _, k, n, axis = mats[a]
        if axis == 0:
            return out_ref.at[pl.ds(chip * (k // N_CHIPS), k // N_CHIPS), :]
        return out_ref.at[:, pl.ds(chip * (n // N_CHIPS), n // N_CHIPS)]

    def copies(in_refs, out_refs, sem_refs):
        send_sems, recv_sems, local_sems = sem_refs
        x, y, c = _place()
        me = 2 * x + y
        local, sends, recvs = [], [], []
        for a in range(n_arr):
            src = in_refs[a].at[layer] if a < len(mats) else in_refs[a]
            local.append(pltpu.make_async_copy(src, block(a, out_refs[a], me), local_sems.at[a]))
            for j, (px, py) in enumerate(_chip_peers(x, y)):
                sends.append(_remote(src, block(a, out_refs[a], me), send_sems.at[a, j], recv_sems.at[a, j], (px, py, c)))
                recvs.append(_remote(src, block(a, out_refs[a], 2 * px + py), send_sems.at[a, j], recv_sems.at[a, j],
                                     (px, py, c)))
        return local, sends, recvs

    sems = [pltpu.SemaphoreType.DMA((n_arr, 3)), pltpu.SemaphoreType.DMA((n_arr, 3)), pltpu.SemaphoreType.DMA((n_arr,))]
    return _Comm(ins, out_shapes, sems, copies)


def _gather_halved_comm(shards, layer, names, conv=None):
    mats = [b for b in BIG if b[0] in names]
    ins = [shards[name] for name, _, _, _ in mats] + ([conv] if conv is not None else [])
    out_shapes = []
    for name, k, n, axis in mats:
        out_shapes.append(jax.ShapeDtypeStruct((N_CHIPS, k, n // N_CHIPS) if name == "w_in" else (k, n), BF16))
    if conv is not None:
        out_shapes.append(jax.ShapeDtypeStruct((N_CHIPS,) + conv.shape, conv.dtype))
    n_arr = len(ins)

    def whole(a, ref, chip):
        if a == len(mats) or mats[a][0] == "w_in":
            return ref.at[chip]
        _, k, n, axis = mats[a]
        if axis == 0:
            return ref.at[pl.ds(chip * (k // N_CHIPS), k // N_CHIPS), :]
        return ref.at[:, pl.ds(chip * (n // N_CHIPS), n // N_CHIPS)]

    def src_half(a, in_ref, core):
        if a == len(mats):
            rows = conv.shape[0] // 2
            return in_ref.at[pl.ds(core * rows, rows), :]
        rows = shards[mats[a][0]].shape[1] // 2
        return in_ref.at[layer, pl.ds(core * rows, rows), :]

    def half(a, out_ref, chip, core):
        if a == len(mats):
            rows = conv.shape[0] // 2
            return out_ref.at[chip, pl.ds(core * rows, rows), :]
        name, k, n, axis = mats[a]
        if name == "w_in":
            return out_ref.at[chip, pl.ds(core * (k // 2), k // 2), :]
        if axis == 0:
            rows = k // N_CHIPS // 2
            return out_ref.at[pl.ds(chip * 2 * rows + core * rows, rows), :]
        return out_ref.at[pl.ds(core * (k // 2), k // 2), pl.ds(chip * (n // N_CHIPS), n // N_CHIPS)]

    def copies(in_refs, out_refs, sem_refs):
        send_sems, recv_sems, own_send, own_recv = sem_refs[0], sem_refs[1], sem_refs[4], sem_refs[5]
        x, y, c = _place()
        me = 2 * x + y
        sends, recvs = [], []
        for a in range(n_arr):
            own = in_refs[a].at[layer] if a < len(mats) else in_refs[a]
            cp = _remote(own, whole(a, out_refs[a], me), own_send.at[a], own_recv.at[a], (x, y, 1 - c))
            sends.append(cp)
            recvs.append(cp)
            for j, (px, py) in enumerate(_chip_peers(x, y)):
                sends.append(_remote(src_half(a, in_refs[a], c), half(a, out_refs[a], me, c), send_sems.at[a, j],
                                     recv_sems.at[a, j], (px, py, c)))
                recvs.append(_remote(src_half(a, in_refs[a], c), half(a, out_refs[a], 2 * px + py, c), send_sems.at[a, j],
                                     recv_sems.at[a, j], (px, py, c)))
        return [], sends, recvs

    def forwards(in_refs, out_refs, sem_refs):
        send_sems, recv_sems = sem_refs[2], sem_refs[3]
        x, y, c = _place()
        sends, recvs = [], []
        for a in range(n_arr):
            for j, (px, py) in enumerate(_chip_peers(x, y)):
                peer = 2 * px + py
                sends.append(_remote(half(a, out_refs[a], peer, c), half(a, out_refs[a], peer, c), send_sems.at[a, j],
                                     recv_sems.at[a, j], (x, y, 1 - c)))
                recvs.append(_remote(half(a, out_refs[a], peer, c), half(a, out_refs[a], peer, 1 - c), send_sems.at[a, j],
                                     recv_sems.at[a, j], (x, y, 1 - c)))
        return sends, recvs

    sems = [pltpu.SemaphoreType.DMA((n_arr, 3))] * 4 + [pltpu.SemaphoreType.DMA((n_arr,))] * 2
    return _Comm(ins, out_shapes, sems, copies, forwards=forwards)


def _allgather_comm(a):
    def copies(in_refs, out_refs, sem_refs):
        send_sems, recv_sems, local_sem = sem_refs
        x, y, c = _place()
        me = 2 * x + y
        local = [pltpu.make_async_copy(in_refs[0], out_refs[0].at[me], local_sem)]
        sends, recvs = [], []
        for j, (px, py) in enumerate(_chip_peers(x, y)):
            sends.append(_remote(in_refs[0], out_refs[0].at[me], send_sems.at[j], recv_sems.at[j], (px, py, c)))
            recvs.append(_remote(in_refs[0], out_refs[0].at[2 * px + py], send_sems.at[j], recv_sems.at[j], (px, py, c)))
        return local, sends, recvs

    sems = [pltpu.SemaphoreType.DMA((3,)), pltpu.SemaphoreType.DMA((3,)), pltpu.SemaphoreType.DMA]
    return _Comm([a], [jax.ShapeDtypeStruct((N_CHIPS,) + a.shape, a.dtype)], sems, copies)


class _Geom:
    def __init__(self, kind, shape, stride=None, width=None):
        self.kind, self.shape, self.stride, self.width = kind, tuple(shape), stride, width
        if kind == "cols":
            k, n = shape
            self.half_shape, self.part_shape, self.final_shape = (k // 2, n), (k // 2, width), (k, width)
        else:
            _, _, h, n = shape
            self.half_shape, self.part_shape, self.final_shape = (N_CHIPS, h, n), (h, n), (2 * h, n)

    def half(self, ref, core):
        if self.kind == "cols":
            return ref.at[pl.ds(core * self.half_shape[0], self.half_shape[0]), :]
        return ref.at[:, core]

    def part(self, ref, chip):
        if self.kind == "cols":
            return ref.at[:, pl.ds(chip * self.stride, self.width)]
        return ref.at[chip]

    def final_half(self, ref, layer, core):
        rows = self.part_shape[0]
        return ref.at[layer, pl.ds(core * rows, rows), :]


def _reduce_a_comm(geoms, arrs):
    n = len(arrs)

    def copies(in_refs, out_refs, sem_refs):
        x, y, c = _place()
        cps = [_remote(geoms[a].half(in_refs[a], 1 - c), out_refs[a], sem_refs[0].at[a], sem_refs[1].at[a], (x, y, 1 - c))
               for a in range(n)]
        return [], cps, cps

    return _Comm(arrs, [jax.ShapeDtypeStruct(g.half_shape, F32) for g in geoms],
                 [pltpu.SemaphoreType.DMA((n,)), pltpu.SemaphoreType.DMA((n,))], copies)


def _reduce_b_comm(geoms, halves):
    n = len(halves)

    def copies(in_refs, out_refs, sem_refs):
        send_sems, recv_sems, local_sems = sem_refs
        x, y, c = _place()
        me = 2 * x + y
        local, sends, recvs = [], [], []
        for a in range(n):
            g = geoms[a]
            local.append(pltpu.make_async_copy(g.part(in_refs[a], me), out_refs[a].at[me], local_sems.at[a]))
            for j, (px, py) in enumerate(_chip_peers(x, y)):
                peer = 2 * px + py
                sends.append(_remote(g.part(in_refs[a], peer), out_refs[a].at[me], send_sems.at[a, j], recv_sems.at[a, j],
                                     (px, py, c)))
                recvs.append(_remote(g.part(in_refs[a], me), out_refs[a].at[peer], send_sems.at[a, j], recv_sems.at[a, j],
                                     (px, py, c)))
        return local, sends, recvs

    sems = [pltpu.SemaphoreType.DMA((n, 3)), pltpu.SemaphoreType.DMA((n, 3)), pltpu.SemaphoreType.DMA((n,))]
    return _Comm(halves, [jax.ShapeDtypeStruct((N_CHIPS,) + g.part_shape, h.dtype) for g, h in zip(geoms, halves)], sems,
                 copies)


def _reduce_c_comm(geoms, finals, layer):
    n = len(finals)

    def copies(in_refs, out_refs, sem_refs):
        send_sems, recv_sems = sem_refs
        x, y, c = _place()
        sends, recvs = [], []
        for a in range(n):
            g = geoms[a]
            sends.append(_remote(g.final_half(in_refs[a], layer, c), g.final_half(out_refs[a], layer, c), send_sems.at[a],
                                 recv_sems.at[a], (x, y, 1 - c)))
            recvs.append(_remote(g.final_half(in_refs[a], layer, c), g.final_half(out_refs[a], layer, 1 - c),
                                 send_sems.at[a], recv_sems.at[a], (x, y, 1 - c)))
        return [], sends, recvs

    sems = [pltpu.SemaphoreType.DMA((n,)), pltpu.SemaphoreType.DMA((n,))]
    return _Comm(finals, [jax.ShapeDtypeStruct(f.shape, f.dtype) for f in finals], sems, copies,
                 aliases={a: a for a in range(n)})


def _run_comm(comm, name):
    def body(*refs):
        ni, no = len(comm.ins), len(comm.out_shapes)
        in_refs, out_refs, sem_refs = refs[:ni], refs[ni:ni + no], refs[ni + no:]
        comm.start(in_refs, out_refs, sem_refs)
        comm.wait(in_refs, out_refs, sem_refs)

    return pl.pallas_call(
        body, name=name, in_specs=[_ANY] * len(comm.ins), out_specs=[_ANY] * len(comm.out_shapes),
        out_shape=comm.out_shapes, scratch_shapes=comm.sems, input_output_aliases=comm.aliases,
        compiler_params=pltpu.CompilerParams(has_side_effects=True),
    )(*comm.ins)


def _tile_call(body, name, nt, in_specs, out_specs, out_shape, scratch, args, comm):
    if comm is None:
        outs = pl.pallas_call(body, name=name, grid=(nt,), in_specs=in_specs, out_specs=out_specs, out_shape=out_shape,
                              scratch_shapes=scratch, compiler_params=_params("arbitrary"))(*args)
        return outs, []
    n_in, n_out, n_scr = len(in_specs), len(out_specs), len(scratch)
    ci, co = len(comm.ins), len(comm.out_shapes)

    def hosted(*refs):
        in_refs = refs[:n_in]
        cin = refs[n_in:n_in + ci]
        out_refs = refs[n_in + ci:n_in + ci + n_out]
        cout = refs[n_in + ci + n_out:n_in + ci + n_out + co]
        scr = refs[n_in + ci + n_out + co:n_in + ci + n_out + co + n_scr]
        sems = refs[n_in + ci + n_out + co + n_scr:]
        i = pl.program_id(0)

        @pl.when(i == 0)
        def _():
            comm.start(cin, cout, sems)

        body(*in_refs, *out_refs, *scr)

        @pl.when(i == nt - 1)
        def _():
            comm.wait(cin, cout, sems)

    outs = pl.pallas_call(
        hosted, name=name + "_comm", grid=(nt,),
        in_specs=list(in_specs) + [_ANY] * ci, out_specs=list(out_specs) + [_ANY] * co,
        out_shape=list(out_shape) + comm.out_shapes, scratch_shapes=list(scratch) + comm.sems,
        input_output_aliases={n_in + i: n_out + o for i, o in comm.aliases.items()},
        compiler_params=_params("arbitrary"),
    )(*args, *comm.ins)
    return outs[:n_out], outs[n_out:]


def mix_fwd(x, g_mix, w_in, w_out, wcat, bmat, ln_g, ln_b, avg, conv_w, pool_bd, pool_scale, *, tm, comm=None):
    t = x.shape[0]
    nt = t // tm

    def body(x_ref, g_ref, win_ref, wout_ref, wcat_ref, bmat_ref, lng_ref, lnb_ref, avg_ref, cw_ref, pw_ref, ps_ref,
             proj_ref, ycat_ref, x1_ref, hbuf, zbuf):
        i = pl.program_id(0)

        @pl.when(i == 0)
        def _():
            hbuf[0:HALO, :] = jnp.zeros((HALO, D_B), F32)
            zbuf[0:HALO, :] = jnp.zeros((HALO, D_C), F32)

        xv = x_ref[...]
        n = xv * lax.rsqrt(jnp.mean(xv * xv, axis=-1, keepdims=True) + RMS_EPS)
        h1 = (n * g_ref[...]).astype(BF16)
        proj_ref[...] = _dot(h1, win_ref[...])

        lo_mask = _lane_lt((CHUNK, CHUNK), HEAD_DIM)
        avg = avg_ref[...]
        gu = _gelu(proj_ref[:, 0:D_A])
        gv = _gelu(proj_ref[:, D_A:2 * D_A])
        dv = gv - _group_mean(gv, avg)
        var = _group_mean(dv * dv, avg)
        vnb = (dv * lax.rsqrt(var + LN_EPS) * lng_ref[...] + lnb_ref[...]).astype(BF16)
        for c in range(tm // CHUNK):
            rows = slice(c * CHUNK, (c + 1) * CHUNK)
            for j in range(3):
                cols = slice(j * CHUNK, (j + 1) * CHUNK)
                mixed = _sgu_mix(vnb[rows, cols], wcat_ref[j], lo_mask) + bmat_ref[:, cols]
                ycat_ref[rows, cols] = (gu[rows, cols] * mixed).astype(BF16)

        o = 2 * D_A
        hcur = proj_ref[:, o + 2 * D_B:o + 3 * D_B] * proj_ref[:, o:o + D_B]
        hbuf[HALO:HALO + tm, :] = hcur
        y = (cw_ref[2:3, :] * hcur + cw_ref[1:2, :] * hbuf[pl.ds(HALO - 1, tm), :]
             + cw_ref[0:1, :] * hbuf[pl.ds(HALO - 2, tm), :])
        ycat_ref[:, D_A:D_A + D_B] = (proj_ref[:, o + D_B:o + 2 * D_B] * y).astype(BF16)
        hbuf[0:HALO, :] = hbuf[tm:tm + HALO, :]

        zc = proj_ref[:, o + 3 * D_B:D_IN]
        zbuf[HALO:HALO + tm, :] = zc
        mean, _ = _pool_means(zbuf[...], tm, i * tm)
        pooled = (mean - zc).astype(BF16)
        ycat_ref[:, D_A + D_B:D_MODEL] = (_dot(pooled, pw_ref[...]) * ps_ref[...]).astype(BF16)
        zbuf[0:HALO, :] = zbuf[tm:tm + HALO, :]

        x1_ref[...] = xv + _dot(ycat_ref[...], wout_ref[...])

    row = lambda w: pl.BlockSpec((tm, w), lambda i: (i, 0))
    return _tile_call(
        body, "mix_fwd", nt,
        [row(D_MODEL), _const_spec((1, D_MODEL)), _const_spec((D_MODEL, D_IN)), _const_spec((D_MODEL, D_MODEL)),
         _const_spec((3, CHUNK, 2 * CHUNK)), _const_spec((CHUNK, D_A)), _const_spec((1, D_A)), _const_spec((1, D_A)),
         _const_spec((D_A, D_A)), _const_spec((8, D_B)), _const_spec((D_C, D_C)), _const_spec((1, D_C))],
        [row(D_IN), row(D_MODEL), row(D_MODEL)],
        [jax.ShapeDtypeStruct((t, D_IN), F32), jax.ShapeDtypeStruct((t, D_MODEL), BF16),
         jax.ShapeDtypeStruct((t, D_MODEL), F32)],
        [pltpu.VMEM((tm + HALO, D_B), F32), pltpu.VMEM((tm + HALO, D_C), F32)],
        (x, g_mix, w_in, w_out, wcat, bmat, ln_g, ln_b, avg, conv_w, pool_bd, pool_scale), comm)


def ffn_ple_fwd(x1, p, g_ff, w_ff1, w_ff2, g_ple, w_gate, w_proj, *, tm, comm=None):
    t = x1.shape[0]
    nt = t // tm
    nc = D_FF // D_MODEL

    def body(x1_ref, p_ref, gff_ref, w1_ref, w2_ref, gple_ref, wg_ref, wp_ref, a_ref, x2_ref, x3_ref):
        x1v = x1_ref[...]
        n2 = x1v * lax.rsqrt(jnp.mean(x1v * x1v, axis=-1, keepdims=True) + RMS_EPS)
        h2 = (n2 * gff_ref[...]).astype(BF16)
        acc = x1v
        for c in range(nc):
            cols = slice(c * D_MODEL, (c + 1) * D_MODEL)
            a = _dot(h2, w1_ref[:, cols])
            a_ref[:, cols] = a.astype(BF16)
            ra = jnp.maximum(a, 0.0)
            acc = acc + _dot((ra * ra).astype(BF16), w2_ref[cols, :])
        x2_ref[...] = acc
        n3 = acc * lax.rsqrt(jnp.mean(acc * acc, axis=-1, keepdims=True) + RMS_EPS)
        h3 = (n3 * gple_ref[...]).astype(BF16)
        gate = jax.nn.sigmoid(_dot(h3, wg_ref[...]))
        pp = _dot(p_ref[...].astype(BF16), wp_ref[...])
        x3_ref[...] = acc + pp * gate

    row = lambda w: pl.BlockSpec((tm, w), lambda i: (i, 0))
    return _tile_call(
        body, "ffn_ple_fwd", nt,
        [row(D_MODEL), _layer_rows(p[1], tm), _const_spec((1, D_MODEL)), _const_spec((D_MODEL, D_FF)),
         _const_spec((D_FF, D_MODEL)), _const_spec((1, D_MODEL)), _const_spec((D_MODEL, D_MODEL)),
         _const_spec((D_PLE, D_MODEL))],
        [row(D_FF), row(D_MODEL), row(D_MODEL)],
        [jax.ShapeDtypeStruct((t, D_FF), BF16), jax.ShapeDtypeStruct((t, D_MODEL), F32),
         jax.ShapeDtypeStruct((t, D_MODEL), F32)],
        [], (x1, p[0], g_ff, w_ff1, w_ff2, g_ple, w_gate, w_proj), comm)


def ffn_ple_loss_fwd(x1, p, g_ff, w_ff1, w_ff2, g_ple, w_gate, w_proj, target, g_final, *, tm):
    t = x1.shape[0]
    nt = t // tm
    nc = D_FF // D_MODEL

    def body(x1_ref, p_ref, gff_ref, w1_ref, w2_ref, gple_ref, wg_ref, wp_ref, t_ref, gfin_ref,
             a_ref, x2_ref, loss_ref, dg_ref, dx_ref, sq_acc):
        i = pl.program_id(0)

        @pl.when(i == 0)
        def _():
            sq_acc[...] = jnp.zeros_like(sq_acc)
            dg_ref[...] = jnp.zeros_like(dg_ref)

        x1v = x1_ref[...]
        n2 = x1v * lax.rsqrt(jnp.mean(x1v * x1v, axis=-1, keepdims=True) + RMS_EPS)
        h2 = (n2 * gff_ref[...]).astype(BF16)
        acc = x1v
        for c in range(nc):
            cols = slice(c * D_MODEL, (c + 1) * D_MODEL)
            a = _dot(h2, w1_ref[:, cols])
            a_ref[:, cols] = a.astype(BF16)
            ra = jnp.maximum(a, 0.0)
            acc = acc + _dot((ra * ra).astype(BF16), w2_ref[cols, :])
        x2_ref[...] = acc
        n3 = acc * lax.rsqrt(jnp.mean(acc * acc, axis=-1, keepdims=True) + RMS_EPS)
        h3 = (n3 * gple_ref[...]).astype(BF16)
        gate = jax.nn.sigmoid(_dot(h3, wg_ref[...]))
        pp = _dot(p_ref[...].astype(BF16), wp_ref[...])
        x3 = acc + pp * gate

        rs = lax.rsqrt(jnp.mean(x3 * x3, axis=-1, keepdims=True) + RMS_EPS)
        n = x3 * rs
        gv = gfin_ref[...]
        err = n * gv - t_ref[...]
        sq_acc[...] += jnp.sum(err * err, axis=0, keepdims=True)
        dy = err * (1.0 / D_MODEL)
        dg_ref[...] += jnp.sum(dy * n, axis=0, keepdims=True)
        dx_ref[...] = _rms_bwd(dy, n, rs, gv)

        @pl.when(i == nt - 1)
        def _():
            total = jnp.sum(sq_acc[...], axis=1, keepdims=True) * (0.5 / D_MODEL)
            loss_ref[...] = jnp.broadcast_to(total, loss_ref.shape)

    row = lambda w: pl.BlockSpec((tm, w), lambda i: (i, 0))
    outs, _ = _tile_call(
        body, "ffn_ple_loss_fwd", nt,
        [row(D_MODEL), _layer_rows(p[1], tm), _const_spec((1, D_MODEL)), _const_spec((D_MODEL, D_FF)),
         _const_spec((D_FF, D_MODEL)), _const_spec((1, D_MODEL)), _const_spec((D_MODEL, D_MODEL)),
         _const_spec((D_PLE, D_MODEL)), row(D_MODEL), _const_spec((1, D_MODEL))],
        [row(D_FF), row(D_MODEL), _acc_spec((8, 128)), _acc_spec((1, D_MODEL)), row(D_MODEL)],
        [jax.ShapeDtypeStruct((t, D_FF), BF16), jax.ShapeDtypeStruct((t, D_MODEL), F32),
         jax.ShapeDtypeStruct((8, 128), F32), jax.ShapeDtypeStruct((1, D_MODEL), F32),
         jax.ShapeDtypeStruct((t, D_MODEL), F32)],
        [pltpu.VMEM((1, D_MODEL), F32)], (x1, p[0], g_ff, w_ff1, w_ff2, g_ple, w_gate, w_proj, target, g_final), None)
    return outs


def ple_bwd(d, x2, p, g_ple, w_gate, w_proj, *, tm, comm=None):
    t = d.shape[0]
    nt = t // tm

    def body(d_ref, x2_ref, p_ref, g_ref, wg_ref, wp_ref, dx2_ref, dg_ref, dwg_ref, dwp_ref):
        i = pl.program_id(0)

        @pl.when(i == 0)
        def _():
            dg_ref[...] = jnp.zeros_like(dg_ref)
            dwg_ref[...] = jnp.zeros_like(dwg_ref)
            dwp_ref[...] = jnp.zeros_like(dwp_ref)

        dv = d_ref[...]
        x2v = x2_ref[...]
        rs = lax.rsqrt(jnp.mean(x2v * x2v, axis=-1, keepdims=True) + RMS_EPS)
        n3 = x2v * rs
        gv = g_ref[...]
        h3 = (n3 * gv).astype(BF16)
        gate = jax.nn.sigmoid(_dot(h3, wg_ref[...]))
        pb = p_ref[...].astype(BF16)
        pp = _dot(pb, wp_ref[...])
        dwp_ref[...] += _dot_tn(pb, (dv * gate).astype(BF16))
        dpre = (dv * pp * gate * (1.0 - gate)).astype(BF16)
        dwg_ref[...] += _dot_tn(h3, dpre)
        dh3 = _dot_nt(dpre, wg_ref[...])
        dg_ref[...] += jnp.sum(dh3 * n3, axis=0, keepdims=True)
        dx2_ref[...] = dv + _rms_bwd(dh3, n3, rs, gv)

    row = lambda w: pl.BlockSpec((tm, w), lambda i: (i, 0))
    return _tile_call(
        body, "ple_bwd", nt,
        [row(D_MODEL), row(D_MODEL), _layer_rows(p[1], tm), _const_spec((1, D_MODEL)), _const_spec((D_MODEL, D_MODEL)),
         _const_spec((D_PLE, D_MODEL))],
        [row(D_MODEL), _acc_spec((1, D_MODEL)), _acc_spec((D_MODEL, D_MODEL)), _acc_spec((D_PLE, D_MODEL))],
        [jax.ShapeDtypeStruct((t, D_MODEL), F32), jax.ShapeDtypeStruct((1, D_MODEL), F32),
         jax.ShapeDtypeStruct((D_MODEL, D_MODEL), F32), jax.ShapeDtypeStruct((D_PLE, D_MODEL), F32)],
        [], (d, x2, p[0], g_ple, w_gate, w_proj), comm)


def ffn_bwd(dx2, x1, a, g_ff, w_ff1, w_ff2, *, tm, comm=None):
    t = dx2.shape[0]
    nt = t // tm
    nc = D_FF // D_MODEL

    def body(dx2_ref, x1_ref, a_ref, g_ref, w1_ref, w2_ref, dx1_ref, h2_ref, da_ref, dg_ref):
        i = pl.program_id(0)

        @pl.when(i == 0)
        def _():
            dg_ref[...] = jnp.zeros_like(dg_ref)

        dv = dx2_ref[...]
        x1v = x1_ref[...]
        rs = lax.rsqrt(jnp.mean(x1v * x1v, axis=-1, keepdims=True) + RMS_EPS)
        n2 = x1v * rs
        gv = g_ref[...]
        h2_ref[...] = (n2 * gv).astype(BF16)
        dvb = dv.astype(BF16)
        dh2 = jnp.zeros((tm, D_MODEL), F32)
        for c in range(nc):
            cols = slice(c * D_MODEL, (c + 1) * D_MODEL)
            ra = jnp.maximum(a_ref[:, cols].astype(F32), 0.0)
            da = (_dot_nt(dvb, w2_ref[cols, :]) * (2.0 * ra)).astype(BF16)
            da_ref[:, cols] = da
            dh2 = dh2 + _dot_nt(da, w1_ref[:, cols])
        dg_ref[...] += jnp.sum(dh2 * n2, axis=0, keepdims=True)
        dx1_ref[...] = dv + _rms_bwd(dh2, n2, rs, gv)

    row = lambda w: pl.BlockSpec((tm, w), lambda i: (i, 0))
    return _tile_call(
        body, "ffn_bwd", nt,
        [row(D_MODEL), row(D_MODEL), row(D_FF), _const_spec((1, D_MODEL)), _const_spec((D_MODEL, D_FF)),
         _const_spec((D_FF, D_MODEL))],
        [row(D_MODEL), row(D_MODEL), row(D_FF), _acc_spec((1, D_MODEL))],
        [jax.ShapeDtypeStruct((t, D_MODEL), F32), jax.ShapeDtypeStruct((t, D_MODEL), BF16),
         jax.ShapeDtypeStruct((t, D_FF), BF16), jax.ShapeDtypeStruct((1, D_MODEL), F32)],
        [], (dx2, x1, a, g_ff, w_ff1, w_ff2), comm)


def mix_bwd(dx1, x, proj, g_mix, w_in, w_out, wcat, wcat_t, bmat, ln_g, ln_b, avg, conv_w, pool_bd, pool_bd_t,
            pool_scale, *, tm, comm=None):
    t = dx1.shape[0]
    nt = t // tm
    prev_blocks = tm // HALO

    def body(dx1_ref, x_ref, proj_ref, prev_ref, g_ref, win_ref, wout_ref, wcat_ref, wcatt_ref, bmat_ref, lng_ref,
             lnb_ref, avg_ref, cw_ref, pw_ref, pwt_ref, ps_ref,
             dx_ref, h1_ref, dproj_ref, dg_ref, dws_ref, dbm_ref, dlng_ref, dlnb_ref, dcw_ref, dpw_ref, dps_ref,
             dyc, dpj, hbuf, zbuf, dybuf, qbuf):
        i = pl.program_id(0)
        ti = nt - 1 - i

        @pl.when(i == 0)
        def _():
            for ref in (dg_ref, dws_ref, dbm_ref, dlng_ref, dlnb_ref, dcw_ref, dpw_ref, dps_ref):
                ref[...] = jnp.zeros_like(ref)
            dybuf[tm:tm + HALO, :] = jnp.zeros((HALO, D_B), F32)
            qbuf[tm:tm + HALO, :] = jnp.zeros((HALO, D_C), F32)

        dx1v = dx1_ref[...]
        dyc[...] = _dot_nt(dx1v.astype(BF16), wout_ref[...])

        lo_mask = _lane_lt((CHUNK, CHUNK), HEAD_DIM)
        avg = avg_ref[...]
        lng = lng_ref[...]
        gu, dgu = _gelu_and_grad(proj_ref[:, 0:D_A])
        gv, dgv = _gelu_and_grad(proj_ref[:, D_A:2 * D_A])
        cen = gv - _group_mean(gv, avg)
        rstd = lax.rsqrt(_group_mean(cen * cen, avg) + LN_EPS)
        vhat = cen * rstd
        vnb = (vhat * lng + lnb_ref[...]).astype(BF16)
        dya = dyc[:, 0:D_A]
        dm = dya * gu
        dmb = dm.astype(BF16)
        dvn_rows = []
        for c in range(tm // CHUNK):
            rows = slice(c * CHUNK, (c + 1) * CHUNK)
            dbm_ref[...] += dm[rows]
            dvn_parts = []
            for j in range(3):
                cols = slice(j * CHUNK, (j + 1) * CHUNK)
                vnb2 = vnb[rows, cols]
                mixed = _sgu_mix(vnb2, wcat_ref[j], lo_mask) + bmat_ref[:, cols]
                dpj[rows, cols] = dya[rows, cols] * mixed * dgu[rows, cols]
                dmb2 = dmb[rows, cols]
                zero = jnp.zeros_like(dmb2)
                dm_st = jnp.concatenate([jnp.where(lo_mask, dmb2, zero), jnp.where(lo_mask, zero, dmb2)], axis=0)
                dws_ref[j] += _dot_nt(dm_st, vnb2)
                dvn_st = _dot(wcatt_ref[j], dmb2)
                dvn_parts.append(jnp.where(lo_mask, dvn_st[0:CHUNK], dvn_st[CHUNK:2 * CHUNK]))
            dvn_rows.append(jnp.concatenate(dvn_parts, axis=1))
        dvn = jnp.concatenate(dvn_rows, axis=0)
        dlng_ref[...] += jnp.sum(dvn * vhat, axis=0, keepdims=True)
        dlnb_ref[...] += jnp.sum(dvn, axis=0, keepdims=True)
        dvh = dvn * lng
        dgv_in = rstd * (dvh - _group_mean(dvh, avg) - vhat * _group_mean(dvh * vhat, avg))
        dpj[:, D_A:2 * D_A] = dgv_in * dgv

        o = 2 * D_A
        live = (ti > 0).astype(F32)
        zb = proj_ref[:, o:o + D_B]
        gb = proj_ref[:, o + D_B:o + 2 * D_B]
        gc = proj_ref[:, o + 2 * D_B:o + 3 * D_B]
        hcur = gc * zb
        hbuf[0:HALO, :] = prev_ref[:, o + 2 * D_B:o + 3 * D_B] * prev_ref[:, o:o + D_B] * live
        hbuf[HALO:HALO + tm, :] = hcur
        hm1 = hbuf[pl.ds(HALO - 1, tm), :]
        hm2 = hbuf[pl.ds(HALO - 2, tm), :]
        y = cw_ref[2:3, :] * hcur + cw_ref[1:2, :] * hm1 + cw_ref[0:1, :] * hm2
        dout = dyc[:, D_A:D_A + D_B]
        dpj[:, o + D_B:o + 2 * D_B] = dout * y
        dy = dout * gb
        dcw_ref[2:3, :] += jnp.sum(dy * hcur, axis=0, keepdims=True)
        dcw_ref[1:2, :] += jnp.sum(dy * hm1, axis=0, keepdims=True)
        dcw_ref[0:1, :] += jnp.sum(dy * hm2, axis=0, keepdims=True)
        dybuf[0:tm, :] = dy
        dh = (cw_ref[2:3, :] * dy + cw_ref[1:2, :] * dybuf[pl.ds(1, tm), :] + cw_ref[0:1, :] * dybuf[pl.ds(2, tm), :])
        dybuf[tm:tm + HALO, :] = dybuf[0:HALO, :]
        dpj[:, o:o + D_B] = dh * gc
        dpj[:, o + 2 * D_B:o + 3 * D_B] = dh * zb

        zc = proj_ref[:, o + 3 * D_B:D_IN]
        zbuf[0:HALO, :] = prev_ref[:, o + 3 * D_B:D_IN] * live
        zbuf[HALO:HALO + tm, :] = zc
        mean, inv = _pool_means(zbuf[...], tm, ti * tm)
        pooled = (mean - zc).astype(BF16)
        dyp = dyc[:, D_A + D_B:D_MODEL]
        ps = ps_ref[...]
        dps_ref[...] += jnp.sum(dyp * _dot(pooled, pw_ref[...]), axis=0, keepdims=True)
        dpw = (dyp * ps).astype(BF16)
        dpw_ref[...] += _dot_tn(pooled, dpw)
        dpooled = _dot(dpw, pwt_ref[...])
        qbuf[0:tm, :] = dpooled * inv
        q = qbuf[...]
        nrows = tm + HALO
        f2 = q + pltpu.roll(q, nrows - 1, 0)
        f4 = f2 + pltpu.roll(f2, nrows - 2, 0)
        f8 = f4 + pltpu.roll(f4, nrows - 4, 0)
        f16 = f8 + pltpu.roll(f8, nrows - 8, 0)
        lane = lax.broadcasted_iota(jnp.int32, (tm, D_C), 1)
        ahead = jnp.where(lane < 64, f2[0:tm], jnp.where(lane < 128, f4[0:tm], jnp.where(lane < 192, f8[0:tm], f16[0:tm])))
        dpj[:, o + 3 * D_B:D_IN] = ahead - dpooled
        qbuf[tm:tm + HALO, :] = qbuf[0:HALO, :]

        dprojb = dpj[...].astype(BF16)
        dproj_ref[...] = dprojb
        dh1 = _dot_nt(dprojb, win_ref[...])
        xv = x_ref[...]
        rs = lax.rsqrt(jnp.mean(xv * xv, axis=-1, keepdims=True) + RMS_EPS)
        n1 = xv * rs
        gv1 = g_ref[...]
        h1_ref[...] = (n1 * gv1).astype(BF16)
        dg_ref[...] += jnp.sum(dh1 * n1, axis=0, keepdims=True)
        dx_ref[...] = dx1v + _rms_bwd(dh1, n1, rs, gv1)

        @pl.when(i == nt - 1)
        def _():
            tril = (lax.broadcasted_iota(jnp.int32, (2 * CHUNK, CHUNK), 0) % CHUNK
                    >= lax.broadcasted_iota(jnp.int32, (2 * CHUNK, CHUNK), 1))
            for j in range(3):
                dws_ref[j] = jnp.where(tril, dws_ref[j], 0.0)
            dbm_ref[...] = _group_mean_split(dbm_ref[...], avg) * float(HEAD_DIM)

    rev = lambda w: pl.BlockSpec((tm, w), lambda i: (nt - 1 - i, 0))
    prev = pl.BlockSpec((HALO, D_IN), lambda i: (jnp.maximum((nt - 1 - i) * prev_blocks - 1, 0), 0))
    acc_shapes = [(1, D_MODEL), (3, 2 * CHUNK, CHUNK), (CHUNK, D_A), (1, D_A), (1, D_A), (8, D_B), (D_C, D_C), (1, D_C)]
    return _tile_call(
        body, "mix_bwd", nt,
        [rev(D_MODEL), rev(D_MODEL), rev(D_IN), prev, _const_spec((1, D_MODEL)), _const_spec((D_MODEL, D_IN)),
         _const_spec((D_MODEL, D_MODEL)), _const_spec((3, CHUNK, 2 * CHUNK)), _const_spec((3, 2 * CHUNK, CHUNK)),
         _const_spec((CHUNK, D_A)), _const_spec((1, D_A)), _const_spec((1, D_A)), _const_spec((D_A, D_A)),
         _const_spec((8, D_B)), _const_spec((D_C, D_C)), _const_spec((D_C, D_C)), _const_spec((1, D_C))],
        [rev(D_MODEL), rev(D_MODEL), rev(D_IN)] + [_acc_spec(s) for s in acc_shapes],
        [jax.ShapeDtypeStruct((t, D_MODEL), F32), jax.ShapeDtypeStruct((t, D_MODEL), BF16),
         jax.ShapeDtypeStruct((t, D_IN), BF16)] + [jax.ShapeDtypeStruct(s, F32) for s in acc_shapes],
        [pltpu.VMEM((tm, D_MODEL), F32), pltpu.VMEM((tm, D_IN), F32),
         pltpu.VMEM((tm + HALO, D_B), F32), pltpu.VMEM((tm + HALO, D_C), F32),
         pltpu.VMEM((tm + HALO, D_B), F32), pltpu.VMEM((tm + HALO, D_C), F32)],
        (dx1, x, proj, proj, g_mix, w_in, w_out, wcat, wcat_t, bmat, ln_g, ln_b, avg, conv_w, pool_bd, pool_bd_t,
         pool_scale), comm)


def wgrad(a, b, *, tk, a_layer=None, relu_sq=False, comm=None):
    t, m = a.shape[-2:]
    n = b.shape[1]
    bm = min(m, 1024)
    bn = 1024 if n % 1024 == 0 else n
    nk = t // tk
    if a_layer is None:
        a_spec = pl.BlockSpec((tk, bm), lambda i, j, k: (k, i))
    else:
        a_spec = pl.BlockSpec((None, None, tk, bm), lambda i, j, k: (a_layer, 0, k, i))

    def body(a_ref, b_ref, o_ref):
        k = pl.program_id(2)

        @pl.when(k == 0)
        def _():
            o_ref[...] = jnp.zeros_like(o_ref)

        av = a_ref[...]
        if relu_sq:
            ra = jnp.maximum(av.astype(F32), 0.0)
            av = ra * ra
        o_ref[...] += _dot_tn(av.astype(BF16), b_ref[...].astype(BF16))

    name = f"wgrad_{m}x{n}" + ("_relu_sq" if relu_sq else "")
    grid = (m // bm, n // bn, nk)
    specs = [a_spec, pl.BlockSpec((tk, bn), lambda i, j, k: (k, j))]
    out_spec = pl.BlockSpec((bm, bn), lambda i, j, k: (i, j))
    if comm is None:
        return pl.pallas_call(body, name=name, grid=grid, in_specs=specs, out_specs=out_spec,
                              out_shape=jax.ShapeDtypeStruct((m, n), F32),
                              compiler_params=_params("parallel", "parallel", "arbitrary"))(a, b)
    ci, co = len(comm.ins), len(comm.out_shapes)

    def hosted(*refs):
        cin, cout, sems = refs[2:2 + ci], refs[3 + ci:3 + ci + co], refs[3 + ci + co:]
        step = (pl.program_id(0) * grid[1] + pl.program_id(1)) * grid[2] + pl.program_id(2)

        @pl.when(step == 0)
        def _():
            comm.start(cin, cout, sems)

        body(refs[0], refs[1], refs[2 + ci])

        @pl.when(step == grid[0] * grid[1] * grid[2] - 1)
        def _():
            comm.wait(cin, cout, sems)

    outs = pl.pallas_call(
        hosted, name=name + "_comm", grid=grid, in_specs=specs + [_ANY] * ci, out_specs=[out_spec] + [_ANY] * co,
        out_shape=[jax.ShapeDtypeStruct((m, n), F32)] + comm.out_shapes, scratch_shapes=comm.sems,
        input_output_aliases={2 + i: 1 + o for i, o in comm.aliases.items()},
        compiler_params=_params("arbitrary", "arbitrary", "arbitrary"),
    )(a, b, *comm.ins)
    return outs[0], outs[1:]


def _row_block(rows, cols, target_bytes):
    target = max(8, target_bytes // (4 * cols))
    if rows <= target:
        return rows
    best = None
    for br in range(8, target + 1, 8):
        if rows % br == 0:
            best = br
    return best if best is not None else rows


def adamw(w, g, m, v):
    shape = w.shape
    cols = shape[-1]
    rows = math.prod(shape[:-1]) if len(shape) > 1 else 1
    br = _row_block(rows, cols, 1 << 20)

    def body(w_ref, g_ref, m_ref, v_ref, d_ref, nm_ref, nv_ref):
        gv = g_ref[...]
        nm = ADAM_B1 * m_ref[...] + (1.0 - ADAM_B1) * gv
        nv = ADAM_B2 * v_ref[...] + (1.0 - ADAM_B2) * jnp.square(gv)
        m_hat = nm / (1.0 - ADAM_B1 ** ADAM_STEP)
        v_hat = nv / (1.0 - ADAM_B2 ** ADAM_STEP)
        d_ref[...] = -ADAM_LR * (m_hat / (jnp.sqrt(v_hat) + ADAM_EPS) + ADAM_WD * w_ref[...])
        nm_ref[...] = nm
        nv_ref[...] = nv

    spec = pl.BlockSpec((br, cols), lambda i: (i, 0))
    outs = pl.pallas_call(
        body, name="adamw", grid=(rows // br,),
        in_specs=[spec] * 4, out_specs=[spec] * 3,
        out_shape=[jax.ShapeDtypeStruct((rows, cols), F32)] * 3,
        compiler_params=pltpu.CompilerParams(dimension_semantics=("parallel",)),
    )(*(a.reshape(rows, cols) for a in (w, g, m, v)))
    return tuple(o.reshape(shape) for o in outs)


ADD_STEPS = 4


def add_halves(geoms, arrs, received, c_idx, dtypes):
    n = len(arrs)

    def body(c_ref, *refs):
        del c_ref
        for a in range(n):
            refs[2 * n + a][...] = (refs[a][...] + refs[n + a][...]).astype(dtypes[a])

    own_specs, half_specs = [], []
    for g in geoms:
        if g.kind == "cols":
            rows, cols = g.half_shape[0] // ADD_STEPS, g.half_shape[1]
            own_specs.append(pl.BlockSpec((rows, cols), lambda i, c_ref: (ADD_STEPS * c_ref[0] + i, 0)))
            half_specs.append(pl.BlockSpec((rows, cols), lambda i, c_ref: (i, 0)))
        else:
            _, h, cols = g.half_shape
            own_specs.append(pl.BlockSpec((None, None, h, cols), lambda i, c_ref: (i, c_ref[0], 0, 0)))
            half_specs.append(pl.BlockSpec((None, h, cols), lambda i, c_ref: (i, 0, 0)))
    return pl.pallas_call(
        body, name="add_halves",
        grid_spec=pltpu.PrefetchScalarGridSpec(num_scalar_prefetch=1, grid=(ADD_STEPS,),
                                               in_specs=own_specs + half_specs, out_specs=half_specs),
        out_shape=[jax.ShapeDtypeStruct(g.half_shape, dt) for g, dt in zip(geoms, dtypes)],
        compiler_params=_params("parallel"),
    )(c_idx, *arrs, *received)


def add_parts(geoms, landed, finals, layer, c_idx):
    n = len(landed)

    def body(c_ref, *refs):
        del c_ref
        for a in range(n):
            p_ref = refs[a]
            parts = [p_ref[j].astype(F32) for j in range(N_CHIPS)]
            refs[2 * n + a][...] = ((parts[0] + parts[1]) + parts[2]) + parts[3]

    in_specs, out_specs = [], []
    for g in geoms:
        rows, cols = g.part_shape[0] // ADD_STEPS, g.part_shape[1]
        in_specs.append(pl.BlockSpec((N_CHIPS, rows, cols), lambda i, c_ref: (0, i, 0)))
        out_specs.append(pl.BlockSpec((None, rows, cols), lambda i, c_ref: (layer, ADD_STEPS * c_ref[0] + i, 0)))
    return pl.pallas_call(
        body, name="add_parts",
        grid_spec=pltpu.PrefetchScalarGridSpec(num_scalar_prefetch=1, grid=(ADD_STEPS,),
                                               in_specs=in_specs + [_ANY] * n, out_specs=out_specs),
        out_shape=[jax.ShapeDtypeStruct(f.shape, F32) for f in finals],
        input_output_aliases={1 + n + a: a for a in range(n)},
        compiler_params=_params("parallel"),
    )(c_idx, *landed, *finals)


def _shard_dims(k, n, axis):
    return (k // N_CHIPS, n) if axis == 0 else (k, n // N_CHIPS)


W_IN_STRIDE = 512
W_IN_WINDOW = 640


def _big_geoms():
    geoms = []
    for name, k, n, axis in BIG:
        if axis == 0:
            geoms.append(_Geom("rows", (N_CHIPS, 2, k // N_CHIPS // 2, n)))
        elif name == "w_in":
            geoms.append(_Geom("cols", (k, n), W_IN_STRIDE, W_IN_WINDOW))
        else:
            geoms.append(_Geom("cols", (k, n), n // N_CHIPS, n // N_CHIPS))
    return geoms


def _grad_views(gb, geoms):
    return [gb[name].reshape(g.shape) for (name, _, _, _), g in zip(BIG, geoms)]


def _round_up(v, m):
    return (v + m - 1) // m * m


def _prep_small(small):
    tril = jnp.tril(jnp.ones((CHUNK, CHUNK), bool))
    wm = jnp.where(tril, small["sgu_w"], 0.0).astype(BF16).reshape(DEPTH, 3, 2, CHUNK, CHUNK)
    head = jnp.arange(D_A) // HEAD_DIM
    grp = jnp.arange(D_C) // HEAD_DIM
    pw_rows = small["pool_w"].reshape(DEPTH, D_C, HEAD_DIM)
    pool_bd = jnp.where((grp[:, None] == grp[None, :])[None], jnp.tile(pw_rows, (1, 1, D_C // HEAD_DIM)), 0.0).astype(BF16)
    return dict(
        wcat=wm.transpose(0, 1, 3, 2, 4).reshape(DEPTH, 3, CHUNK, 2 * CHUNK),
        wcat_t=wm.transpose(0, 1, 2, 4, 3).reshape(DEPTH, 3, 2 * CHUNK, CHUNK),
        bmat=jnp.repeat(jnp.swapaxes(small["sgu_b"], 1, 2), HEAD_DIM, axis=2),
        avg=jnp.where(head[:, None] == head[None, :], 1.0 / HEAD_DIM, 0.0).astype(BF16),
        pool_bd=pool_bd, pool_bd_t=jnp.swapaxes(pool_bd, 1, 2),
        conv8=jnp.pad(small["conv_w"], ((0, 0), (0, 8 - 3), (0, 0))),
    )


def _row(a):
    return a.reshape(1, -1)


MIX_WEIGHTS = ("w_in", "w_out")
MLP_WEIGHTS = ("w_ff1", "w_ff2", "w_ple_gate", "w_ple_proj")
ALL_BIG = MIX_WEIGHTS + MLP_WEIGHTS
FFN_BWD_TILE = 512
PLE_BWD_TILE = 1024


def _fwd_layer(h, p, wl, small, prep, l, tm, comm_mix=None, comm_mlp=None, target=None):
    (proj, ycat, x1), got = mix_fwd(h, _row(small["norm_mix_g"][l]), wl["w_in"], wl["w_out"], prep["wcat"][l],
                                    prep["bmat"][l], _row(small["sgu_ln_g"][l]), _row(small["sgu_ln_b"][l]), prep["avg"],
                                    prep["conv8"][l], prep["pool_bd"][l], _row(small["pool_scale"][l]),
                                    tm=min(2 * tm, h.shape[0]), comm=comm_mix)
    if comm_mix is not None:
        wl = {**wl, **_weights_of(got, MLP_WEIGHTS)}
    if target is not None:
        a, x2, loss_blk, d_final_g, d = ffn_ple_loss_fwd(
            x1, (p, l), _row(small["norm_ff_g"][l]), wl["w_ff1"], wl["w_ff2"], _row(small["norm_ple_g"][l]),
            wl["w_ple_gate"], wl["w_ple_proj"], target, _row(small["final_g"]), tm=tm)
        return (h, proj, ycat, x1, a, x2), (loss_blk, d_final_g, d), [], wl
    (a, x2, x3), couts = ffn_ple_fwd(x1, (p, l), _row(small["norm_ff_g"][l]), wl["w_ff1"], wl["w_ff2"],
                                     _row(small["norm_ple_g"][l]), wl["w_ple_gate"], wl["w_ple_proj"], tm=tm,
                                     comm=comm_mlp)
    return (h, proj, ycat, x1, a, x2), x3, couts, wl


def _merge_comms(comms):
    comms = [cm for cm in comms if cm is not None]
    if len(comms) <= 1:
        return comms[0] if comms else None
    spans, ni, no, ns = [], 0, 0, 0
    for cm in comms:
        spans.append((ni, no, ns))
        ni, no, ns = ni + len(cm.ins), no + len(cm.out_shapes), ns + len(cm.sems)

    def copies(in_refs, out_refs, sem_refs):
        local, sends, recvs = [], [], []
        for cm, (i0, o0, s0) in zip(comms, spans):
            got = cm.copies(in_refs[i0:i0 + len(cm.ins)], out_refs[o0:o0 + len(cm.out_shapes)],
                            sem_refs[s0:s0 + len(cm.sems)])
            local, sends, recvs = local + got[0], sends + got[1], recvs + got[2]
        return local, sends, recvs

    aliases = {i0 + i: o0 + o for cm, (i0, o0, _) in zip(comms, spans) for i, o in cm.aliases.items()}
    assert all(cm.forwards is None for cm in comms)
    return _Comm(sum((cm.ins for cm in comms), []), sum((cm.out_shapes for cm in comms), []),
                 sum((cm.sems for cm in comms), []), copies, aliases)


def _split_results(results, comms):
    out, at = [], 0
    for cm in comms:
        if cm is None:
            out.append(None)
        else:
            out.append(results[at:at + len(cm.out_shapes)])
            at += len(cm.out_shapes)
    return out


class _Reduction:
    def __init__(self, layer, names, geoms, arrs, finals, c_arr, narrow=()):
        self.layer, self.names, self.geoms, self.arrs = layer, list(names), list(geoms), list(arrs)
        self.finals, self.c_arr = finals, c_arr
        self.dtypes = [BF16 if n in narrow else F32 for n in self.names]

    def comm_a(self):
        return _reduce_a_comm(self.geoms, self.arrs)

    def comm_b(self, received):
        return _reduce_b_comm(self.geoms, add_halves(self.geoms, self.arrs, received, self.c_arr, self.dtypes))

    def comm_c(self, landed):
        mine = add_parts(self.geoms, landed, [self.finals[n] for n in self.names], self.layer, self.c_arr)
        return _reduce_c_comm(self.geoms, mine, self.layer)

    def done(self, results):
        self.finals.update(zip(self.names, results))


class _Plan:
    def ple(self):
        return None

    def after_ple(self, results):
        pass

    def ffn(self):
        return None

    def after_ffn(self, results):
        pass

    def out_grad(self, gb):
        return None

    def after_out_grad(self, results):
        pass

    def before_mix(self, gb):
        pass

    def mix(self):
        return None

    def after_mix(self, results):
        pass


class _CarryPlan(_Plan):
    def __init__(self, above):
        self.above = above

    def ple(self):
        return self.above.comm_a()

    def after_ple(self, results):
        self.received = results

    def ffn(self):
        return self.above.comm_b(self.received)

    def after_ffn(self, results):
        self.landed = results

    def mix(self):
        return self.above.comm_c(self.landed)

    def after_mix(self, results):
        self.above.done(results)


class _LastPlan(_CarryPlan):
    def __init__(self, above, make):
        super().__init__(above)
        self.make = make

    def out_grad(self, gb):
        self.early = [self.make(MLP_WEIGHTS, gb)]
        return self.early[0].comm_a()

    def after_out_grad(self, results):
        self.early_received = [results]

    def before_mix(self, gb):
        self.early.append(self.make(("w_out",), gb))
        self.early_received.append(_run_comm(self.early[1].comm_a(), "reduce_a_early"))

    def mix(self):
        self.parts = [self.above.comm_c(self.landed)] + [r.comm_b(got) for r, got in zip(self.early, self.early_received)]
        return _merge_comms(self.parts)

    def after_mix(self, results):
        above_res, *self.early_landed = _split_results(results, self.parts)
        self.above.done(above_res)


def _bwd_layer(d, saved, p, wl, small, prep, l, tm, tk, plan=None):
    plan = plan or _Plan()
    xin, proj, ycat, x1, a, x2 = saved
    (dx2, dg_ple, dw_gate, dw_proj), res = ple_bwd(d, x2, (p, l), _row(small["norm_ple_g"][l]), wl["w_ple_gate"],
                                                   wl["w_ple_proj"], tm=min(PLE_BWD_TILE, d.shape[0]), comm=plan.ple())
    plan.after_ple(res)
    gb = {"w_ple_gate": dw_gate, "w_ple_proj": dw_proj}
    (dx1, h2, da, dg_ff), res = ffn_bwd(dx2, x1, a, _row(small["norm_ff_g"][l]), wl["w_ff1"], wl["w_ff2"],
                                        tm=FFN_BWD_TILE if tm >= FFN_BWD_TILE else tm, comm=plan.ffn())
    plan.after_ffn(res)
    gb["w_ff2"] = wgrad(a, dx2, tk=tk, relu_sq=True)
    gb["w_ff1"] = wgrad(h2, da, tk=tk)
    comm = plan.out_grad(gb)
    if comm is None:
        gb["w_out"] = wgrad(ycat, dx1, tk=tk)
    else:
        gb["w_out"], res = wgrad(ycat, dx1, tk=tk, comm=comm)
        plan.after_out_grad(res)
    plan.before_mix(gb)
    (dprev, h1, dproj, dg_mix, dws, dbm, dlng, dlnb, dcw, dpw, dps), res = mix_bwd(
        dx1, xin, proj, _row(small["norm_mix_g"][l]), wl["w_in"], wl["w_out"], prep["wcat"][l], prep["wcat_t"][l],
        prep["bmat"][l], _row(small["sgu_ln_g"][l]), _row(small["sgu_ln_b"][l]), prep["avg"], prep["conv8"][l],
        prep["pool_bd"][l], prep["pool_bd_t"][l], _row(small["pool_scale"][l]), tm=tm, comm=plan.mix())
    plan.after_mix(res)
    gb["w_in"] = wgrad(h1, dproj, tk=tk)
    gs = {
        "norm_ple_g": dg_ple[0], "norm_ff_g": dg_ff[0], "norm_mix_g": dg_mix[0],
        "sgu_w": dws.reshape(2 * 3, CHUNK, CHUNK), "sgu_b": dbm[:, ::HEAD_DIM].T,
        "sgu_ln_g": dlng[0], "sgu_ln_b": dlnb[0], "conv_w": dcw[0:3], "pool_scale": dps[0],
        "pool_w": jnp.stack([dpw[g * HEAD_DIM:(g + 1) * HEAD_DIM, g * HEAD_DIM:(g + 1) * HEAD_DIM]
                             for g in range(D_C // HEAD_DIM)]),
    }
    return dprev, gb, gs


def _local_step(x, p, target, full, small, *, tm, tk):
    prep = _prep_small(small)
    p = p[:, None]
    saved, h = [], x
    for l in range(DEPTH):
        wl = {name: full[name][l] for name in full}
        s, h, _, _ = _fwd_layer(h, p, wl, small, prep, l, tm, target=target if l == DEPTH - 1 else None)
        saved.append(s)
    loss_blk, d_final_g, d = h
    gbig, gsm = [None] * DEPTH, [None] * DEPTH
    for l in reversed(range(DEPTH)):
        wl = {name: full[name][l] for name in full}
        d, gbig[l], gsm[l] = _bwd_layer(d, saved[l], p, wl, small, prep, l, tm, tk)
    big = {name: jnp.stack([gbig[l][name] for l in range(DEPTH)]) for name in gbig[0]}
    sm = {name: jnp.stack([gsm[l][name] for l in range(DEPTH)]) for name in gsm[0]}
    sm["final_g"] = d_final_g[0]
    return loss_blk[0, 0], d, big, sm


def _weights_of(gathered, names):
    wl = dict(zip([b[0] for b in BIG if b[0] in names], gathered))
    if "w_in" in wl:
        wl["w_in"] = wl["w_in"].transpose(1, 0, 2).reshape(D_MODEL, D_IN)
    return wl


def kernel(x, p, norm_mix_g, w_in, sgu_w, sgu_b, sgu_ln_g, sgu_ln_b, conv_w, pool_w, pool_scale, w_out, norm_ff_g, w_ff1, w_ff2, norm_ple_g, w_ple_gate, w_ple_proj, final_g, loss_target, m_norm_mix_g, m_w_in, m_sgu_w, m_sgu_b, m_sgu_ln_g, m_sgu_ln_b, m_conv_w, m_pool_w, m_pool_scale, m_w_out, m_norm_ff_g, m_w_ff1, m_w_ff2, m_norm_ple_g, m_w_ple_gate, m_w_ple_proj, m_final_g, v_norm_mix_g, v_w_in, v_sgu_w, v_sgu_b, v_sgu_ln_g, v_sgu_ln_b, v_conv_w, v_pool_w, v_pool_scale, v_w_out, v_norm_ff_g, v_w_ff1, v_w_ff2, v_norm_ple_g, v_w_ple_gate, v_w_ple_proj, v_final_g):
    args = dict(locals())
    w = {name: args[name] for name in WEIGHTS}
    m = {name: args["m_" + name] for name in WEIGHTS}
    v = {name: args["v_" + name] for name in WEIGHTS}
    t = x.shape[1]
    tm = min(512, t)
    tk = min(2048, t)
    x_idx, y_idx, c_idx = _place()
    chip = 2 * x_idx + y_idx
    c_arr = c_idx.reshape(1).astype(jnp.int32)
    xs, target = x[0], loss_target[0]

    shards = {name: w[name].astype(BF16) for name, _, _, _ in BIG}
    conv_rows = _round_up(CONV_SHARD, 8 * 128) // 128
    conv_flat = jnp.pad(w["conv_w"].reshape(-1), (0, conv_rows * 128 - CONV_SHARD)).reshape(conv_rows, 128)
    first = _run_comm(_gather_halved_comm(shards, 0, MIX_WEIGHTS, conv_flat), "gather_first")
    conv_full = (first[len(MIX_WEIGHTS)].reshape(N_CHIPS, -1)[:, :CONV_SHARD]
                 .reshape(N_CHIPS, DEPTH, 3, D_B // N_CHIPS).transpose(1, 2, 0, 3).reshape(DEPTH, 3, D_B))
    small = {name: w[name] for name in SMALL}
    small["conv_w"] = conv_full
    prep = _prep_small(small)

    wl = [None] * DEPTH
    wl[0] = _weights_of(first, MIX_WEIGHTS)
    saved, h = [], xs
    for l in range(DEPTH):
        comm_mix = None
        if l == 0:
            comm_mix = _gather_halved_comm(shards, 0, MLP_WEIGHTS)
        comm_mlp = _gather_comm(shards, l + 1, ALL_BIG) if l + 1 < DEPTH else None
        s, h, got, wl[l] = _fwd_layer(h, p, wl[l], small, prep, l, tm, comm_mix, comm_mlp,
                                      target=target if l == DEPTH - 1 else None)
        saved.append(s)
        if comm_mlp is not None:
            wl[l + 1] = _weights_of(got, ALL_BIG)
    loss_blk, d_final_g, d = h

    geoms = dict(zip([b[0] for b in BIG], _big_geoms()))
    finals = {name: jnp.zeros((DEPTH,) + g.final_shape, F32) for name, g in geoms.items()}

    def reduction(layer, names, gb, narrow=()):
        return _Reduction(layer, names, [geoms[n] for n in names], [gb[n].reshape(geoms[n].shape) for n in names],
                          finals, c_arr, narrow)

    gsm = [None] * DEPTH
    above = None
    for l in reversed(range(DEPTH)):
        if above is None:
            plan = _Plan()
        elif l > 0:
            plan = _CarryPlan(above)
        else:
            plan = _LastPlan(above, lambda names, gb: reduction(0, names, gb))
        d, gb, gsm[l] = _bwd_layer(d, saved[l], p, wl[l], small, prep, l, tm, tk, plan)
        if l > 0:
            above = reduction(l, ALL_BIG, gb)

    sm = {name: jnp.stack([gsm[i][name] for i in range(DEPTH)]) for name in gsm[0]}
    sm["final_g"] = d_final_g[0]
    sizes = [sm[name].size for name in SMALL]
    small_rows = _round_up(-(-sum(sizes) // (2 * N_CHIPS * LANES)), 8 * ADD_STEPS)
    small_flat = jnp.pad(jnp.concatenate([sm[name].reshape(-1) for name in SMALL]),
                         (0, 2 * N_CHIPS * small_rows * LANES - sum(sizes)))
    geoms["small"] = _Geom("rows", (N_CHIPS, 2, small_rows, LANES))
    finals["small"] = jnp.zeros((1,) + geoms["small"].final_shape, F32)
    late = reduction(0, ("w_in", "small"), {**gb, "small": small_flat}, narrow=("w_in",))
    late_landed = _run_comm(late.comm_b(_run_comm(late.comm_a(), "reduce_a_late")), "reduce_b_late")
    finishing = plan.early + [late]
    last = [r.comm_c(got) for r, got in zip(finishing, plan.early_landed + [late_landed])]
    for r, res in zip(finishing, _split_results(_run_comm(_merge_comms(last), "reduce_c_last"), last)):
        r.done(res)

    grads = {name: finals[name] for name, _, _, _ in BIG}
    grads["w_in"] = lax.dynamic_slice_in_dim(grads["w_in"], chip * (D_IN // N_CHIPS - W_IN_STRIDE), D_IN // N_CHIPS, axis=2)
    small_red = _run_comm(_allgather_comm(finals["small"][0]), "small_allgather")[0].reshape(-1)
    off = 0
    for name, size in zip(SMALL, sizes):
        grads[name] = small_red[off:off + size].reshape(sm[name].shape)
        off += size
    grads["conv_w"] = lax.dynamic_slice_in_dim(grads["conv_w"], chip * (D_B // N_CHIPS), D_B // N_CHIPS, axis=2)

    loss = lax.psum(loss_blk[0, 0], ("x", "y", "c"))
    delta, new_m, new_v = {}, {}, {}
    for name in WEIGHTS:
        delta[name], new_m[name], new_v[name] = adamw(w[name], grads[name], m[name], v[name])
    return (loss, d[None], *[grads[n] for n in WEIGHTS], *[delta[n] for n in WEIGHTS],
            *[new_m[n] for n in WEIGHTS], *[new_v[n] for n in WEIGHTS])
```

```python
import math

import jax
import jax.numpy as jnp
from jax import lax
from jax.experimental import pallas as pl
from jax.experimental.pallas import tpu as pltpu

F32 = jnp.float32
BF16 = jnp.bfloat16

D_MODEL = 1024
DEPTH = 4
D_PLE = 256
D_FF = 4096
HEAD_DIM = 64
D_A = 384
D_B = 384
D_C = 256
D_IN = 2176
CHUNK = 128
HALO = 16
RMS_EPS = 1e-6
LN_EPS = 1e-5
N_CHIPS = 4
LANES = 1024

ADAM_LR = 0.001
ADAM_B1 = 0.9
ADAM_B2 = 0.999
ADAM_EPS = 1e-08
ADAM_WD = 0.01
ADAM_STEP = 10

VMEM_LIMIT_BYTES = 60 * 1024 * 1024

_RSQRT2 = 0.7071067811865476
_INV_SQRT_2PI = 0.3989422804014327

BIG = (
    ("w_in", D_MODEL, D_IN, 1),
    ("w_out", D_MODEL, D_MODEL, 0),
    ("w_ff1", D_MODEL, D_FF, 1),
    ("w_ff2", D_FF, D_MODEL, 0),
    ("w_ple_gate", D_MODEL, D_MODEL, 0),
    ("w_ple_proj", D_PLE, D_MODEL, 1),
)
SMALL = ("norm_mix_g", "sgu_w", "sgu_b", "sgu_ln_g", "sgu_ln_b", "conv_w", "pool_w", "pool_scale",
         "norm_ff_g", "norm_ple_g", "final_g")
WEIGHTS = ("norm_mix_g", "w_in", "sgu_w", "sgu_b", "sgu_ln_g", "sgu_ln_b", "conv_w", "pool_w", "pool_scale",
           "w_out", "norm_ff_g", "w_ff1", "w_ff2", "norm_ple_g", "w_ple_gate", "w_ple_proj", "final_g")
CONV_SHARD = DEPTH * 3 * (D_B // N_CHIPS)


def _dot(a, b):
    return jnp.dot(a, b, preferred_element_type=F32)


def _dot_nt(a, b):
    return lax.dot_general(a, b, (((1,), (1,)), ((), ())), preferred_element_type=F32)


def _dot_tn(a, b):
    return lax.dot_general(a, b, (((0,), (0,)), ((), ())), preferred_element_type=F32)


def _const_spec(shape):
    nd = len(shape)
    return pl.BlockSpec(shape, lambda i: (0,) * nd, pipeline_mode=pl.Buffered(1))


def _acc_spec(shape):
    nd = len(shape)
    return pl.BlockSpec(shape, lambda i: (0,) * nd)


def _layer_rows(layer, tm):
    return pl.BlockSpec((None, None, tm, D_PLE), lambda i: (layer, 0, i, 0))


def _params(*sem):
    return pltpu.CompilerParams(dimension_semantics=sem, vmem_limit_bytes=VMEM_LIMIT_BYTES)


def _rms_bwd(dh, n, rs, g):
    dn = dh * g
    return rs * (dn - n * jnp.mean(dn * n, axis=-1, keepdims=True))


def _gelu(x):
    return x * (0.5 * (1.0 + lax.erf(x * _RSQRT2)))


def _gelu_and_grad(x):
    cdf = 0.5 * (1.0 + lax.erf(x * _RSQRT2))
    return x * cdf, cdf + x * (jnp.exp(-0.5 * x * x) * _INV_SQRT_2PI)


def _group_mean(v, avg):
    vb = v.astype(BF16)
    split = 2 * CHUNK
    return jnp.concatenate([_dot(vb[:, :split], avg[:split, :split]), _dot(vb[:, split:], avg[split:, split:])], axis=1)


def _group_mean_split(v, avg):
    hi = v.astype(BF16)
    lo = (v - hi.astype(F32)).astype(BF16)
    return _dot(hi, avg) + _dot(lo, avg)


def _lane_lt(shape, bound):
    return lax.broadcasted_iota(jnp.int32, shape, 1) < bound


def _sgu_mix(vnb2, wcat_j, lo_mask):
    zero = jnp.zeros_like(vnb2)
    stacked = jnp.concatenate([jnp.where(lo_mask, vnb2, zero), jnp.where(lo_mask, zero, vnb2)], axis=0)
    return _dot(wcat_j, stacked)


def _pool_means(ext, tile_rows, first_pos):
    s2 = ext + pltpu.roll(ext, 1, 0)
    s4 = s2 + pltpu.roll(s2, 2, 0)
    s8 = s4 + pltpu.roll(s4, 4, 0)
    s16 = s8 + pltpu.roll(s8, 8, 0)
    pos = (first_pos + lax.broadcasted_iota(jnp.int32, (tile_rows, 1), 0) + 1).astype(F32)
    lane = lax.broadcasted_iota(jnp.int32, (tile_rows, D_C), 1)
    sums = jnp.where(lane < 64, s2[HALO:], jnp.where(lane < 128, s4[HALO:], jnp.where(lane < 192, s8[HALO:], s16[HALO:])))
    win = jnp.where(lane < 64, 2.0, jnp.where(lane < 128, 4.0, jnp.where(lane < 192, 8.0, 16.0)))
    inv = 1.0 / jnp.minimum(pos, win)
    return sums * inv, inv


MESH = pl.DeviceIdType.MESH
_ANY = pl.BlockSpec(memory_space=pl.ANY)


def _place():
    return lax.axis_index("x"), lax.axis_index("y"), lax.axis_index("c")


def _chip_peers(x, y):
    return [(1 - x, y), (x, 1 - y), (1 - x, 1 - y)]


class _Comm:
    def __init__(self, ins, out_shapes, sems, copies, aliases=None, forwards=None):
        self.ins, self.out_shapes, self.sems, self.copies = list(ins), list(out_shapes), list(sems), copies
        self.aliases = dict(aliases or {})
        self.forwards = forwards

    def start(self, in_refs, out_refs, sem_refs):
        local, sends, _ = self.copies(in_refs, out_refs, sem_refs)
        for cp in local + sends:
            cp.start()

    def wait(self, in_refs, out_refs, sem_refs):
        local, sends, recvs = self.copies(in_refs, out_refs, sem_refs)
        for cp in recvs:
            cp.wait_recv()
        passed, passed_in = self.forwards(in_refs, out_refs, sem_refs) if self.forwards else ([], [])
        for cp in passed:
            cp.start()
        for cp in sends:
            cp.wait_send()
        for cp in local:
            cp.wait()
        for cp in passed_in:
            cp.wait_recv()
        for cp in passed:
            cp.wait_send()


def _remote(src, dst, send_sem, recv_sem, device):
    return pltpu.make_async_remote_copy(src_ref=src, dst_ref=dst, send_sem=send_sem, recv_sem=recv_sem,
                                        device_id=device, device_id_type=MESH)


def _gather_comm(shards, layer, names, conv=None):
    mats = [b for b in BIG if b[0] in names]
    ins = [shards[name] for name, _, _, _ in mats] + ([conv] if conv is not None else [])
    out_shapes = []
    for name, k, n, axis in mats:
        shape = (N_CHIPS, k, n // N_CHIPS) if name == "w_in" else (k, n)
        out_shapes.append(jax.ShapeDtypeStruct(shape, BF16))
    if conv is not None:
        out_shapes.append(jax.ShapeDtypeStruct((N_CHIPS,) + conv.shape, conv.dtype))
    n_arr = len(ins)

    def block(a, out_ref, chip):
        if a == len(mats) or mats[a][0] == "w_in":
            return out_ref.at[chip]
        _, k, n, axis = mats[a]
        if axis == 0:
            return out_ref.at[pl.ds(chip * (k // N_CHIPS), k // N_CHIPS), :]
        return out_ref.at[:, pl.ds(chip * (n // N_CHIPS), n // N_CHIPS)]

    def copies(in_refs, out_refs, sem_refs):
        send_sems, recv_sems, local_sems = sem_refs
        x, y, c = _place()
        me = 2 * x + y
        local, sends, recvs = [], [], []
        for a in range(n_arr):
            src = in_refs[a].at[layer] if a < len(mats) else in_refs[a]
            local.append(pltpu.make_async_copy(src, block(a, out_refs[a], me), local_sems.at[a]))
            for j, (px, py) in enumerate(_chip_peers(x, y)):
                sends.append(_remote(src, block(a, out_refs[a], me), send_sems.at[a, j], recv_sems.at[a, j], (px, py, c)))
                recvs.append(_remote(src, block(a, out_refs[a], 2 * px + py), send_sems.at[a, j], recv_sems.at[a, j],
                                     (px, py, c)))
        return local, sends, recvs

    sems = [pltpu.SemaphoreType.DMA((n_arr, 3)), pltpu.SemaphoreType.DMA((n_arr, 3)), pltpu.SemaphoreType.DMA((n_arr,))]
    return _Comm(ins, out_shapes, sems, copies)


def _gather_halved_comm(shards, layer, names, conv=None):
    mats = [b for b in BIG if b[0] in names]
    ins = [shards[name] for name, _, _, _ in mats] + ([conv] if conv is not None else [])
    out_shapes = []
    for name, k, n, axis in mats:
        out_shapes.append(jax.ShapeDtypeStruct((N_CHIPS, k, n // N_CHIPS) if name == "w_in" else (k, n), BF16))
    if conv is not None:
        out_shapes.append(jax.ShapeDtypeStruct((N_CHIPS,) + conv.shape, conv.dtype))
    n_arr = len(ins)

    def whole(a, ref, chip):
        if a == len(mats) or mats[a][0] == "w_in":
            return ref.at[chip]
        _, k, n, axis = mats[a]
        if axis == 0:
            return ref.at[pl.ds(chip * (k // N_CHIPS), k // N_CHIPS), :]
        return ref.at[:, pl.ds(chip * (n // N_CHIPS), n // N_CHIPS)]

    def src_half(a, in_ref, core):
        if a == len(mats):
            rows = conv.shape[0] // 2
            return in_ref.at[pl.ds(core * rows, rows), :]
        rows = shards[mats[a][0]].shape[1] // 2
        return in_ref.at[layer, pl.ds(core * rows, rows), :]

    def half(a, out_ref, chip, core):
        if a == len(mats):
            rows = conv.shape[0] // 2
            return out_ref.at[chip, pl.ds(core * rows, rows), :]
        name, k, n, axis = mats[a]
        if name == "w_in":
            return out_ref.at[chip, pl.ds(core * (k // 2), k // 2), :]
        if axis == 0:
            rows = k // N_CHIPS // 2
            return out_ref.at[pl.ds(chip * 2 * rows + core * rows, rows), :]
        return out_ref.at[pl.ds(core * (k // 2), k // 2), pl.ds(chip * (n // N_CHIPS), n // N_CHIPS)]

    def copies(in_refs, out_refs, sem_refs):
        send_sems, recv_sems, own_send, own_recv = sem_refs[0], sem_refs[1], sem_refs[4], sem_refs[5]
        x, y, c = _place()
        me = 2 * x + y
        sends, recvs = [], []
        for a in range(n_arr):
            own = in_refs[a].at[layer] if a < len(mats) else in_refs[a]
            cp = _remote(own, whole(a, out_refs[a], me), own_send.at[a], own_recv.at[a], (x, y, 1 - c))
            sends.append(cp)
            recvs.append(cp)
            for j, (px, py) in enumerate(_chip_peers(x, y)):
                sends.append(_remote(src_half(a, in_refs[a], c), half(a, out_refs[a], me, c), send_sems.at[a, j],
                                     recv_sems.at[a, j], (px, py, c)))
                recvs.append(_remote(src_half(a, in_refs[a], c), half(a, out_refs[a], 2 * px + py, c), send_sems.at[a, j],
                                     recv_sems.at[a, j], (px, py, c)))
        return [], sends, recvs

    def forwards(in_refs, out_refs, sem_refs):
        send_sems, recv_sems = sem_refs[2], sem_refs[3]
        x, y, c = _place()
        sends, recvs = [], []
        for a in range(n_arr):
            for j, (px, py) in enumerate(_chip_peers(x, y)):
                peer = 2 * px + py
                sends.append(_remote(half(a, out_refs[a], peer, c), half(a, out_refs[a], peer, c), send_sems.at[a, j],
                                     recv_sems.at[a, j], (x, y, 1 - c)))
                recvs.append(_remote(half(a, out_refs[a], peer, c), half(a, out_refs[a], peer, 1 - c), send_sems.at[a, j],
                                     recv_sems.at[a, j], (x, y, 1 - c)))
        return sends, recvs

    sems = [pltpu.SemaphoreType.DMA((n_arr, 3))] * 4 + [pltpu.SemaphoreType.DMA((n_arr,))] * 2
    return _Comm(ins, out_shapes, sems, copies, forwards=forwards)


def _allgather_comm(a):
    def copies(in_refs, out_refs, sem_refs):
        send_sems, recv_sems, local_sem = sem_refs
        x, y, c = _place()
        me = 2 * x + y
        local = [pltpu.make_async_copy(in_refs[0], out_refs[0].at[me], local_sem)]
        sends, recvs = [], []
        for j, (px, py) in enumerate(_chip_peers(x, y)):
            sends.append(_remote(in_refs[0], out_refs[0].at[me], send_sems.at[j], recv_sems.at[j], (px, py, c)))
            recvs.append(_remote(in_refs[0], out_refs[0].at[2 * px + py], send_sems.at[j], recv_sems.at[j], (px, py, c)))
        return local, sends, recvs

    sems = [pltpu.SemaphoreType.DMA((3,)), pltpu.SemaphoreType.DMA((3,)), pltpu.SemaphoreType.DMA]
    return _Comm([a], [jax.ShapeDtypeStruct((N_CHIPS,) + a.shape, a.dtype)], sems, copies)


class _Geom:
    def __init__(self, kind, shape, stride=None, width=None):
        self.kind, self.shape, self.stride, self.width = kind, tuple(shape), stride, width
        if kind == "cols":
            k, n = shape
            self.half_shape, self.part_shape, self.final_shape = (k // 2, n), (k // 2, width), (k, width)
        else:
            _, _, h, n = shape
            self.half_shape, self.part_shape, self.final_shape = (N_CHIPS, h, n), (h, n), (2 * h, n)

    def half(self, ref, core):
        if self.kind == "cols":
            return ref.at[pl.ds(core * self.half_shape[0], self.half_shape[0]), :]
        return ref.at[:, core]

    def part(self, ref, chip):
        if self.kind == "cols":
            return ref.at[:, pl.ds(chip * self.stride, self.width)]
        return ref.at[chip]

    def final_half(self, ref, layer, core):
        rows = self.part_shape[0]
        return ref.at[layer, pl.ds(core * rows, rows), :]


def _reduce_a_comm(geoms, arrs):
    n = len(arrs)

    def copies(in_refs, out_refs, sem_refs):
        x, y, c = _place()
        cps = [_remote(geoms[a].half(in_refs[a], 1 - c), out_refs[a], sem_refs[0].at[a], sem_refs[1].at[a], (x, y, 1 - c))
               for a in range(n)]
        return [], cps, cps

    return _Comm(arrs, [jax.ShapeDtypeStruct(g.half_shape, F32) for g in geoms],
                 [pltpu.SemaphoreType.DMA((n,)), pltpu.SemaphoreType.DMA((n,))], copies)


def _reduce_b_comm(geoms, halves):
    n = len(halves)

    def copies(in_refs, out_refs, sem_refs):
        send_sems, recv_sems, local_sems = sem_refs
        x, y, c = _place()
        me = 2 * x + y
        local, sends, recvs = [], [], []
        for a in range(n):
            g = geoms[a]
            local.append(pltpu.make_async_copy(g.part(in_refs[a], me), out_refs[a].at[me], local_sems.at[a]))
            for j, (px, py) in enumerate(_chip_peers(x, y)):
                peer = 2 * px + py
                sends.append(_remote(g.part(in_refs[a], peer), out_refs[a].at[me], send_sems.at[a, j], recv_sems.at[a, j],
                                     (px, py, c)))
                recvs.append(_remote(g.part(in_refs[a], me), out_refs[a].at[peer], send_sems.at[a, j], recv_sems.at[a, j],
                                     (px, py, c)))
        return local, sends, recvs

    sems = [pltpu.SemaphoreType.DMA((n, 3)), pltpu.SemaphoreType.DMA((n, 3)), pltpu.SemaphoreType.DMA((n,))]
    return _Comm(halves, [jax.ShapeDtypeStruct((N_CHIPS,) + g.part_shape, h.dtype) for g, h in zip(geoms, halves)], sems,
                 copies)


def _reduce_c_comm(geoms, finals, layer):
    n = len(finals)

    def copies(in_refs, out_refs, sem_refs):
        send_sems, recv_sems = sem_refs
        x, y, c = _place()
        sends, recvs = [], []
        for a in range(n):
            g = geoms[a]
            sends.append(_remote(g.final_half(in_refs[a], layer, c), g.final_half(out_refs[a], layer, c), send_sems.at[a],
                                 recv_sems.at[a], (x, y, 1 - c)))
            recvs.append(_remote(g.final_half(in_refs[a], layer, c), g.final_half(out_refs[a], layer, 1 - c),
                                 send_sems.at[a], recv_sems.at[a], (x, y, 1 - c)))
        return [], sends, recvs

    sems = [pltpu.SemaphoreType.DMA((n,)), pltpu.SemaphoreType.DMA((n,))]
    return _Comm(finals, [jax.ShapeDtypeStruct(f.shape, f.dtype) for f in finals], sems, copies,
                 aliases={a: a for a in range(n)})


def _run_comm(comm, name):
    def body(*refs):
        ni, no = len(comm.ins), len(comm.out_shapes)
        in_refs, out_refs, sem_refs = refs[:ni], refs[ni:ni + no], refs[ni + no:]
        comm.start(in_refs, out_refs, sem_refs)
        comm.wait(in_refs, out_refs, sem_refs)

    return pl.pallas_call(
        body, name=name, in_specs=[_ANY] * len(comm.ins), out_specs=[_ANY] * len(comm.out_shapes),
        out_shape=comm.out_shapes, scratch_shapes=comm.sems, input_output_aliases=comm.aliases,
        compiler_params=pltpu.CompilerParams(has_side_effects=True),
    )(*comm.ins)


def _tile_call(body, name, nt, in_specs, out_specs, out_shape, scratch, args, comm):
    if comm is None:
        outs = pl.pallas_call(body, name=name, grid=(nt,), in_specs=in_specs, out_specs=out_specs, out_shape=out_shape,
                              scratch_shapes=scratch, compiler_params=_params("arbitrary"))(*args)
        return outs, []
    n_in, n_out, n_scr = len(in_specs), len(out_specs), len(scratch)
    ci, co = len(comm.ins), len(comm.out_shapes)

    def hosted(*refs):
        in_refs = refs[:n_in]
        cin = refs[n_in:n_in + ci]
        out_refs = refs[n_in + ci:n_in + ci + n_out]
        cout = refs[n_in + ci + n_out:n_in + ci + n_out + co]
        scr = refs[n_in + ci + n_out + co:n_in + ci + n_out + co + n_scr]
        sems = refs[n_in + ci + n_out + co + n_scr:]
        i = pl.program_id(0)

        @pl.when(i == 0)
        def _():
            comm.start(cin, cout, sems)

        body(*in_refs, *out_refs, *scr)

        @pl.when(i == nt - 1)
        def _():
            comm.wait(cin, cout, sems)

    outs = pl.pallas_call(
        hosted, name=name + "_comm", grid=(nt,),
        in_specs=list(in_specs) + [_ANY] * ci, out_specs=list(out_specs) + [_ANY] * co,
        out_shape=list(out_shape) + comm.out_shapes, scratch_shapes=list(scratch) + comm.sems,
        input_output_aliases={n_in + i: n_out + o for i, o in comm.aliases.items()},
        compiler_params=_params("arbitrary"),
    )(*args, *comm.ins)
    return outs[:n_out], outs[n_out:]


def mix_fwd(x, g_mix, w_in, w_out, wcat, bmat, ln_g, ln_b, avg, conv_w, pool_bd, pool_scale, *, tm, comm=None):
    t = x.shape[0]
    nt = t // tm

    def body(x_ref, g_ref, win_ref, wout_ref, wcat_ref, bmat_ref, lng_ref, lnb_ref, avg_ref, cw_ref, pw_ref, ps_ref,
             proj_ref, ycat_ref, x1_ref, hbuf, zbuf):
        i = pl.program_id(0)

        @pl.when(i == 0)
        def _():
            hbuf[0:HALO, :] = jnp.zeros((HALO, D_B), F32)
            zbuf[0:HALO, :] = jnp.zeros((HALO, D_C), F32)

        xv = x_ref[...]
        n = xv * lax.rsqrt(jnp.mean(xv * xv, axis=-1, keepdims=True) + RMS_EPS)
        h1 = (n * g_ref[...]).astype(BF16)
        proj_ref[...] = _dot(h1, win_ref[...])

        lo_mask = _lane_lt((CHUNK, CHUNK), HEAD_DIM)
        avg = avg_ref[...]
        gu = _gelu(proj_ref[:, 0:D_A])
        gv = _gelu(proj_ref[:, D_A:2 * D_A])
        dv = gv - _group_mean(gv, avg)
        var = _group_mean(dv * dv, avg)
        vnb = (dv * lax.rsqrt(var + LN_EPS) * lng_ref[...] + lnb_ref[...]).astype(BF16)
        for c in range(tm // CHUNK):
            rows = slice(c * CHUNK, (c + 1) * CHUNK)
            for j in range(3):
                cols = slice(j * CHUNK, (j + 1) * CHUNK)
                mixed = _sgu_mix(vnb[rows, cols], wcat_ref[j], lo_mask) + bmat_ref[:, cols]
                ycat_ref[rows, cols] = (gu[rows, cols] * mixed).astype(BF16)

        o = 2 * D_A
        hcur = proj_ref[:, o + 2 * D_B:o + 3 * D_B] * proj_ref[:, o:o + D_B]
        hbuf[HALO:HALO + tm, :] = hcur
        y = (cw_ref[2:3, :] * hcur + cw_ref[1:2, :] * hbuf[pl.ds(HALO - 1, tm), :]
             + cw_ref[0:1, :] * hbuf[pl.ds(HALO - 2, tm), :])
        ycat_ref[:, D_A:D_A + D_B] = (proj_ref[:, o + D_B:o + 2 * D_B] * y).astype(BF16)
        hbuf[0:HALO, :] = hbuf[tm:tm + HALO, :]

        zc = proj_ref[:, o + 3 * D_B:D_IN]
        zbuf[HALO:HALO + tm, :] = zc
        mean, _ = _pool_means(zbuf[...], tm, i * tm)
        pooled = (mean - zc).astype(BF16)
        ycat_ref[:, D_A + D_B:D_MODEL] = (_dot(pooled, pw_ref[...]) * ps_ref[...]).astype(BF16)
        zbuf[0:HALO, :] = zbuf[tm:tm + HALO, :]

        x1_ref[...] = xv + _dot(ycat_ref[...], wout_ref[...])

    row = lambda w: pl.BlockSpec((tm, w), lambda i: (i, 0))
    return _tile_call(
        body, "mix_fwd", nt,
        [row(D_MODEL), _const_spec((1, D_MODEL)), _const_spec((D_MODEL, D_IN)), _const_spec((D_MODEL, D_MODEL)),
         _const_spec((3, CHUNK, 2 * CHUNK)), _const_spec((CHUNK, D_A)), _const_spec((1, D_A)), _const_spec((1, D_A)),
         _const_spec((D_A, D_A)), _const_spec((8, D_B)), _const_spec((D_C, D_C)), _const_spec((1, D_C))],
        [row(D_IN), row(D_MODEL), row(D_MODEL)],
        [jax.ShapeDtypeStruct((t, D_IN), F32), jax.ShapeDtypeStruct((t, D_MODEL), BF16),
         jax.ShapeDtypeStruct((t, D_MODEL), F32)],
        [pltpu.VMEM((tm + HALO, D_B), F32), pltpu.VMEM((tm + HALO, D_C), F32)],
        (x, g_mix, w_in, w_out, wcat, bmat, ln_g, ln_b, avg, conv_w, pool_bd, pool_scale), comm)


def ffn_ple_fwd(x1, p, g_ff, w_ff1, w_ff2, g_ple, w_gate, w_proj, *, tm, comm=None):
    t = x1.shape[0]
    nt = t // tm
    nc = D_FF // D_MODEL

    def body(x1_ref, p_ref, gff_ref, w1_ref, w2_ref, gple_ref, wg_ref, wp_ref, a_ref, x2_ref, x3_ref):
        x1v = x1_ref[...]
        n2 = x1v * lax.rsqrt(jnp.mean(x1v * x1v, axis=-1, keepdims=True) + RMS_EPS)
        h2 = (n2 * gff_ref[...]).astype(BF16)
        acc = x1v
        for c in range(nc):
            cols = slice(c * D_MODEL, (c + 1) * D_MODEL)
            a = _dot(h2, w1_ref[:, cols])
            a_ref[:, cols] = a.astype(BF16)
            ra = jnp.maximum(a, 0.0)
            acc = acc + _dot((ra * ra).astype(BF16), w2_ref[cols, :])
        x2_ref[...] = acc
        n3 = acc * lax.rsqrt(jnp.mean(acc * acc, axis=-1, keepdims=True) + RMS_EPS)
        h3 = (n3 * gple_ref[...]).astype(BF16)
        gate = jax.nn.sigmoid(_dot(h3, wg_ref[...]))
        pp = _dot(p_ref[...].astype(BF16), wp_ref[...])
        x3_ref[...] = acc + pp * gate

    row = lambda w: pl.BlockSpec((tm, w), lambda i: (i, 0))
    return _tile_call(
        body, "ffn_ple_fwd", nt,
        [row(D_MODEL), _layer_rows(p[1], tm), _const_spec((1, D_MODEL)), _const_spec((D_MODEL, D_FF)),
         _const_spec((D_FF, D_MODEL)), _const_spec((1, D_MODEL)), _const_spec((D_MODEL, D_MODEL)),
         _const_spec((D_PLE, D_MODEL))],
        [row(D_FF), row(D_MODEL), row(D_MODEL)],
        [jax.ShapeDtypeStruct((t, D_FF), BF16), jax.ShapeDtypeStruct((t, D_MODEL), F32),
         jax.ShapeDtypeStruct((t, D_MODEL), F32)],
        [], (x1, p[0], g_ff, w_ff1, w_ff2, g_ple, w_gate, w_proj), comm)


def ffn_ple_loss_fwd(x1, p, g_ff, w_ff1, w_ff2, g_ple, w_gate, w_proj, target, g_final, *, tm):
    t = x1.shape[0]
    nt = t // tm
    nc = D_FF // D_MODEL

    def body(x1_ref, p_ref, gff_ref, w1_ref, w2_ref, gple_ref, wg_ref, wp_ref, t_ref, gfin_ref,
             a_ref, x2_ref, loss_ref, dg_ref, dx_ref, sq_acc):
        i = pl.program_id(0)

        @pl.when(i == 0)
        def _():
            sq_acc[...] = jnp.zeros_like(sq_acc)
            dg_ref[...] = jnp.zeros_like(dg_ref)

        x1v = x1_ref[...]
        n2 = x1v * lax.rsqrt(jnp.mean(x1v * x1v, axis=-1, keepdims=True) + RMS_EPS)
        h2 = (n2 * gff_ref[...]).astype(BF16)
        acc = x1v
        for c in range(nc):
            cols = slice(c * D_MODEL, (c + 1) * D_MODEL)
            a = _dot(h2, w1_ref[:, cols])
            a_ref[:, cols] = a.astype(BF16)
            ra = jnp.maximum(a, 0.0)
            acc = acc + _dot((ra * ra).astype(BF16), w2_ref[cols, :])
        x2_ref[...] = acc
        n3 = acc * lax.rsqrt(jnp.mean(acc * acc, axis=-1, keepdims=True) + RMS_EPS)
        h3 = (n3 * gple_ref[...]).astype(BF16)
        gate = jax.nn.sigmoid(_dot(h3, wg_ref[...]))
        pp = _dot(p_ref[...].astype(BF16), wp_ref[...])
        x3 = acc + pp * gate

        rs = lax.rsqrt(jnp.mean(x3 * x3, axis=-1, keepdims=True) + RMS_EPS)
        n = x3 * rs
        gv = gfin_ref[...]
        err = n * gv - t_ref[...]
        sq_acc[...] += jnp.sum(err * err, axis=0, keepdims=True)
        dy = err * (1.0 / D_MODEL)
        dg_ref[...] += jnp.sum(dy * n, axis=0, keepdims=True)
        dx_ref[...] = _rms_bwd(dy, n, rs, gv)

        @pl.when(i == nt - 1)
        def _():
            total = jnp.sum(sq_acc[...], axis=1, keepdims=True) * (0.5 / D_MODEL)
            loss_ref[...] = jnp.broadcast_to(total, loss_ref.shape)

    row = lambda w: pl.BlockSpec((tm, w), lambda i: (i, 0))
    outs, _ = _tile_call(
        body, "ffn_ple_loss_fwd", nt,
        [row(D_MODEL), _layer_rows(p[1], tm), _const_spec((1, D_MODEL)), _const_spec((D_MODEL, D_FF)),
         _const_spec((D_FF, D_MODEL)), _const_spec((1, D_MODEL)), _const_spec((D_MODEL, D_MODEL)),
         _const_spec((D_PLE, D_MODEL)), row(D_MODEL), _const_spec((1, D_MODEL))],
        [row(D_FF), row(D_MODEL), _acc_spec((8, 128)), _acc_spec((1, D_MODEL)), row(D_MODEL)],
        [jax.ShapeDtypeStruct((t, D_FF), BF16), jax.ShapeDtypeStruct((t, D_MODEL), F32),
         jax.ShapeDtypeStruct((8, 128), F32), jax.ShapeDtypeStruct((1, D_MODEL), F32),
         jax.ShapeDtypeStruct((t, D_MODEL), F32)],
        [pltpu.VMEM((1, D_MODEL), F32)], (x1, p[0], g_ff, w_ff1, w_ff2, g_ple, w_gate, w_proj, target, g_final), None)
    return outs


def ple_bwd(d, x2, p, g_ple, w_gate, w_proj, *, tm, comm=None):
    t = d.shape[0]
    nt = t // tm

    def body(d_ref, x2_ref, p_ref, g_ref, wg_ref, wp_ref, dx2_ref, dg_ref, dwg_ref, dwp_ref):
        i = pl.program_id(0)

        @pl.when(i == 0)
        def _():
            dg_ref[...] = jnp.zeros_like(dg_ref)
            dwg_ref[...] = jnp.zeros_like(dwg_ref)
            dwp_ref[...] = jnp.zeros_like(dwp_ref)

        dv = d_ref[...]
        x2v = x2_ref[...]
        rs = lax.rsqrt(jnp.mean(x2v * x2v, axis=-1, keepdims=True) + RMS_EPS)
        n3 = x2v * rs
        gv = g_ref[...]
        h3 = (n3 * gv).astype(BF16)
        gate = jax.nn.sigmoid(_dot(h3, wg_ref[...]))
        pb = p_ref[...].astype(BF16)
        pp = _dot(pb, wp_ref[...])
        dwp_ref[...] += _dot_tn(pb, (dv * gate).astype(BF16))
        dpre = (dv * pp * gate * (1.0 - gate)).astype(BF16)
        dwg_ref[...] += _dot_tn(h3, dpre)
        dh3 = _dot_nt(dpre, wg_ref[...])
        dg_ref[...] += jnp.sum(dh3 * n3, axis=0, keepdims=True)
        dx2_ref[...] = dv + _rms_bwd(dh3, n3, rs, gv)

    row = lambda w: pl.BlockSpec((tm, w), lambda i: (i, 0))
    return _tile_call(
        body, "ple_bwd", nt,
        [row(D_MODEL), row(D_MODEL), _layer_rows(p[1], tm), _const_spec((1, D_MODEL)), _const_spec((D_MODEL, D_MODEL)),
         _const_spec((D_PLE, D_MODEL))],
        [row(D_MODEL), _acc_spec((1, D_MODEL)), _acc_spec((D_MODEL, D_MODEL)), _acc_spec((D_PLE, D_MODEL))],
        [jax.ShapeDtypeStruct((t, D_MODEL), F32), jax.ShapeDtypeStruct((1, D_MODEL), F32),
         jax.ShapeDtypeStruct((D_MODEL, D_MODEL), F32), jax.ShapeDtypeStruct((D_PLE, D_MODEL), F32)],
        [], (d, x2, p[0], g_ple, w_gate, w_proj), comm)


def ffn_bwd(dx2, x1, a, g_ff, w_ff1, w_ff2, *, tm, comm=None):
    t = dx2.shape[0]
    nt = t // tm
    nc = D_FF // D_MODEL

    def body(dx2_ref, x1_ref, a_ref, g_ref, w1_ref, w2_ref, dx1_ref, h2_ref, da_ref, dg_ref):
        i = pl.program_id(0)

        @pl.when(i == 0)
        def _():
            dg_ref[...] = jnp.zeros_like(dg_ref)

        dv = dx2_ref[...]
        x1v = x1_ref[...]
        rs = lax.rsqrt(jnp.mean(x1v * x1v, axis=-1, keepdims=True) + RMS_EPS)
        n2 = x1v * rs
        gv = g_ref[...]
        h2_ref[...] = (n2 * gv).astype(BF16)
        dvb = dv.astype(BF16)
        dh2 = jnp.zeros((tm, D_MODEL), F32)
        for c in range(nc):
            cols = slice(c * D_MODEL, (c + 1) * D_MODEL)
            ra = jnp.maximum(a_ref[:, cols].astype(F32), 0.0)
            da = (_dot_nt(dvb, w2_ref[cols, :]) * (2.0 * ra)).astype(BF16)
            da_ref[:, cols] = da
            dh2 = dh2 + _dot_nt(da, w1_ref[:, cols])
        dg_ref[...] += jnp.sum(dh2 * n2, axis=0, keepdims=True)
        dx1_ref[...] = dv + _rms_bwd(dh2, n2, rs, gv)

    row = lambda w: pl.BlockSpec((tm, w), lambda i: (i, 0))
    return _tile_call(
        body, "ffn_bwd", nt,
        [row(D_MODEL), row(D_MODEL), row(D_FF), _const_spec((1, D_MODEL)), _const_spec((D_MODEL, D_FF)),
         _const_spec((D_FF, D_MODEL))],
        [row(D_MODEL), row(D_MODEL), row(D_FF), _acc_spec((1, D_MODEL))],
        [jax.ShapeDtypeStruct((t, D_MODEL), F32), jax.ShapeDtypeStruct((t, D_MODEL), BF16),
         jax.ShapeDtypeStruct((t, D_FF), BF16), jax.ShapeDtypeStruct((1, D_MODEL), F32)],
        [], (dx2, x1, a, g_ff, w_ff1, w_ff2), comm)


def mix_bwd(dx1, x, proj, g_mix, w_in, w_out, wcat, wcat_t, bmat, ln_g, ln_b, avg, conv_w, pool_bd, pool_bd_t,
            pool_scale, *, tm, comm=None):
    t = dx1.shape[0]
    nt = t // tm
    prev_blocks = tm // HALO

    def body(dx1_ref, x_ref, proj_ref, prev_ref, g_ref, win_ref, wout_ref, wcat_ref, wcatt_ref, bmat_ref, lng_ref,
             lnb_ref, avg_ref, cw_ref, pw_ref, pwt_ref, ps_ref,
             dx_ref, h1_ref, dproj_ref, dg_ref, dws_ref, dbm_ref, dlng_ref, dlnb_ref, dcw_ref, dpw_ref, dps_ref,
             dyc, dpj, hbuf, zbuf, dybuf, qbuf):
        i = pl.program_id(0)
        ti = nt - 1 - i

        @pl.when(i == 0)
        def _():
            for ref in (dg_ref, dws_ref, dbm_ref, dlng_ref, dlnb_ref, dcw_ref, dpw_ref, dps_ref):
                ref[...] = jnp.zeros_like(ref)
            dybuf[tm:tm + HALO, :] = jnp.zeros((HALO, D_B), F32)
            qbuf[tm:tm + HALO, :] = jnp.zeros((HALO, D_C), F32)

        dx1v = dx1_ref[...]
        dyc[...] = _dot_nt(dx1v.astype(BF16), wout_ref[...])

        lo_mask = _lane_lt((CHUNK, CHUNK), HEAD_DIM)
        avg = avg_ref[...]
        lng = lng_ref[...]
        gu, dgu = _gelu_and_grad(proj_ref[:, 0:D_A])
        gv, dgv = _gelu_and_grad(proj_ref[:, D_A:2 * D_A])
        cen = gv - _group_mean(gv, avg)
        rstd = lax.rsqrt(_group_mean(cen * cen, avg) + LN_EPS)
        vhat = cen * rstd
        vnb = (vhat * lng + lnb_ref[...]).astype(BF16)
        dya = dyc[:, 0:D_A]
        dm = dya * gu
        dmb = dm.astype(BF16)
        dvn_rows = []
        for c in range(tm // CHUNK):
            rows = slice(c * CHUNK, (c + 1) * CHUNK)
            dbm_ref[...] += dm[rows]
            dvn_parts = []
            for j in range(3):
                cols = slice(j * CHUNK, (j + 1) * CHUNK)
                vnb2 = vnb[rows, cols]
                mixed = _sgu_mix(vnb2, wcat_ref[j], lo_mask) + bmat_ref[:, cols]
                dpj[rows, cols] = dya[rows, cols] * mixed * dgu[rows, cols]
                dmb2 = dmb[rows, cols]
                zero = jnp.zeros_like(dmb2)
                dm_st = jnp.concatenate([jnp.where(lo_mask, dmb2, zero), jnp.where(lo_mask, zero, dmb2)], axis=0)
                dws_ref[j] += _dot_nt(dm_st, vnb2)
                dvn_st = _dot(wcatt_ref[j], dmb2)
                dvn_parts.append(jnp.where(lo_mask, dvn_st[0:CHUNK], dvn_st[CHUNK:2 * CHUNK]))
            dvn_rows.append(jnp.concatenate(dvn_parts, axis=1))
        dvn = jnp.concatenate(dvn_rows, axis=0)
        dlng_ref[...] += jnp.sum(dvn * vhat, axis=0, keepdims=True)
        dlnb_ref[...] += jnp.sum(dvn, axis=0, keepdims=True)
        dvh = dvn * lng
        dgv_in = rstd * (dvh - _group_mean(dvh, avg) - vhat * _group_mean(dvh * vhat, avg))
        dpj[:, D_A:2 * D_A] = dgv_in * dgv

        o = 2 * D_A
        live = (ti > 0).astype(F32)
        zb = proj_ref[:, o:o + D_B]
        gb = proj_ref[:, o + D_B:o + 2 * D_B]
        gc = proj_ref[:, o + 2 * D_B:o + 3 * D_B]
        hcur = gc * zb
        hbuf[0:HALO, :] = prev_ref[:, o + 2 * D_B:o + 3 * D_B] * prev_ref[:, o:o + D_B] * live
        hbuf[HALO:HALO + tm, :] = hcur
        hm1 = hbuf[pl.ds(HALO - 1, tm), :]
        hm2 = hbuf[pl.ds(HALO - 2, tm), :]
        y = cw_ref[2:3, :] * hcur + cw_ref[1:2, :] * hm1 + cw_ref[0:1, :] * hm2
        dout = dyc[:, D_A:D_A + D_B]
        dpj[:, o + D_B:o + 2 * D_B] = dout * y
        dy = dout * gb
        dcw_ref[2:3, :] += jnp.sum(dy * hcur, axis=0, keepdims=True)
        dcw_ref[1:2, :] += jnp.sum(dy * hm1, axis=0, keepdims=True)
        dcw_ref[0:1, :] += jnp.sum(dy * hm2, axis=0, keepdims=True)
        dybuf[0:tm, :] = dy
        dh = (cw_ref[2:3, :] * dy + cw_ref[1:2, :] * dybuf[pl.ds(1, tm), :] + cw_ref[0:1, :] * dybuf[pl.ds(2, tm), :])
        dybuf[tm:tm + HALO, :] = dybuf[0:HALO, :]
        dpj[:, o:o + D_B] = dh * gc
        dpj[:, o + 2 * D_B:o + 3 * D_B] = dh * zb

        zc = proj_ref[:, o + 3 * D_B:D_IN]
        zbuf[0:HALO, :] = prev_ref[:, o + 3 * D_B:D_IN] * live
        zbuf[HALO:HALO + tm, :] = zc
        mean, inv = _pool_means(zbuf[...], tm, ti * tm)
        pooled = (mean - zc).astype(BF16)
        dyp = dyc[:, D_A + D_B:D_MODEL]
        ps = ps_ref[...]
        dps_ref[...] += jnp.sum(dyp * _dot(pooled, pw_ref[...]), axis=0, keepdims=True)
        dpw = (dyp * ps).astype(BF16)
        dpw_ref[...] += _dot_tn(pooled, dpw)
        dpooled = _dot(dpw, pwt_ref[...])
        qbuf[0:tm, :] = dpooled * inv
        q = qbuf[...]
        nrows = tm + HALO
        f2 = q + pltpu.roll(q, nrows - 1, 0)
        f4 = f2 + pltpu.roll(f2, nrows - 2, 0)
        f8 = f4 + pltpu.roll(f4, nrows - 4, 0)
        f16 = f8 + pltpu.roll(f8, nrows - 8, 0)
        lane = lax.broadcasted_iota(jnp.int32, (tm, D_C), 1)
        ahead = jnp.where(lane < 64, f2[0:tm], jnp.where(lane < 128, f4[0:tm], jnp.where(lane < 192, f8[0:tm], f16[0:tm])))
        dpj[:, o + 3 * D_B:D_IN] = ahead - dpooled
        qbuf[tm:tm + HALO, :] = qbuf[0:HALO, :]

        dprojb = dpj[...].astype(BF16)
        dproj_ref[...] = dprojb
        dh1 = _dot_nt(dprojb, win_ref[...])
        xv = x_ref[...]
        rs = lax.rsqrt(jnp.mean(xv * xv, axis=-1, keepdims=True) + RMS_EPS)
        n1 = xv * rs
        gv1 = g_ref[...]
        h1_ref[...] = (n1 * gv1).astype(BF16)
        dg_ref[...] += jnp.sum(dh1 * n1, axis=0, keepdims=True)
        dx_ref[...] = dx1v + _rms_bwd(dh1, n1, rs, gv1)

        @pl.when(i == nt - 1)
        def _():
            tril = (lax.broadcasted_iota(jnp.int32, (2 * CHUNK, CHUNK), 0) % CHUNK
                    >= lax.broadcasted_iota(jnp.int32, (2 * CHUNK, CHUNK), 1))
            for j in range(3):
                dws_ref[j] = jnp.where(tril, dws_ref[j], 0.0)
            dbm_ref[...] = _group_mean_split(dbm_ref[...], avg) * float(HEAD_DIM)

    rev = lambda w: pl.BlockSpec((tm, w), lambda i: (nt - 1 - i, 0))
    prev = pl.BlockSpec((HALO, D_IN), lambda i: (jnp.maximum((nt - 1 - i) * prev_blocks - 1, 0), 0))
    acc_shapes = [(1, D_MODEL), (3, 2 * CHUNK, CHUNK), (CHUNK, D_A), (1, D_A), (1, D_A), (8, D_B), (D_C, D_C), (1, D_C)]
    return _tile_call(
        body, "mix_bwd", nt,
        [rev(D_MODEL), rev(D_MODEL), rev(D_IN), prev, _const_spec((1, D_MODEL)), _const_spec((D_MODEL, D_IN)),
         _const_spec((D_MODEL, D_MODEL)), _const_spec((3, CHUNK, 2 * CHUNK)), _const_spec((3, 2 * CHUNK, CHUNK)),
         _const_spec((CHUNK, D_A)), _const_spec((1, D_A)), _const_spec((1, D_A)), _const_spec((D_A, D_A)),
         _const_spec((8, D_B)), _const_spec((D_C, D_C)), _const_spec((D_C, D_C)), _const_spec((1, D_C))],
        [rev(D_MODEL), rev(D_MODEL), rev(D_IN)] + [_acc_spec(s) for s in acc_shapes],
        [jax.ShapeDtypeStruct((t, D_MODEL), F32), jax.ShapeDtypeStruct((t, D_MODEL), BF16),
         jax.ShapeDtypeStruct((t, D_IN), BF16)] + [jax.ShapeDtypeStruct(s, F32) for s in acc_shapes],
        [pltpu.VMEM((tm, D_MODEL), F32), pltpu.VMEM((tm, D_IN), F32),
         pltpu.VMEM((tm + HALO, D_B), F32), pltpu.VMEM((tm + HALO, D_C), F32),
         pltpu.VMEM((tm + HALO, D_B), F32), pltpu.VMEM((tm + HALO, D_C), F32)],
        (dx1, x, proj, proj, g_mix, w_in, w_out, wcat, wcat_t, bmat, ln_g, ln_b, avg, conv_w, pool_bd, pool_bd_t,
         pool_scale), comm)


def wgrad(a, b, *, tk, a_layer=None, relu_sq=False, comm=None):
    t, m = a.shape[-2:]
    n = b.shape[1]
    bm = min(m, 1024)
    bn = 1024 if n % 1024 == 0 else n
    nk = t // tk
    if a_layer is None:
        a_spec = pl.BlockSpec((tk, bm), lambda i, j, k: (k, i))
    else:
        a_spec = pl.BlockSpec((None, None, tk, bm), lambda i, j, k: (a_layer, 0, k, i))

    def body(a_ref, b_ref, o_ref):
        k = pl.program_id(2)

        @pl.when(k == 0)
        def _():
            o_ref[...] = jnp.zeros_like(o_ref)

        av = a_ref[...]
        if relu_sq:
            ra = jnp.maximum(av.astype(F32), 0.0)
            av = ra * ra
        o_ref[...] += _dot_tn(av.astype(BF16), b_ref[...].astype(BF16))

    name = f"wgrad_{m}x{n}" + ("_relu_sq" if relu_sq else "")
    grid = (m // bm, n // bn, nk)
    specs = [a_spec, pl.BlockSpec((tk, bn), lambda i, j, k: (k, j))]
    out_spec = pl.BlockSpec((bm, bn), lambda i, j, k: (i, j))
    if comm is None:
        return pl.pallas_call(body, name=name, grid=grid, in_specs=specs, out_specs=out_spec,
                              out_shape=jax.ShapeDtypeStruct((m, n), F32),
                              compiler_params=_params("parallel", "parallel", "arbitrary"))(a, b)
    ci, co = len(comm.ins), len(comm.out_shapes)

    def hosted(*refs):
        cin, cout, sems = refs[2:2 + ci], refs[3 + ci:3 + ci + co], refs[3 + ci + co:]
        step = (pl.program_id(0) * grid[1] + pl.program_id(1)) * grid[2] + pl.program_id(2)

        @pl.when(step == 0)
        def _():
            comm.start(cin, cout, sems)

        body(refs[0], refs[1], refs[2 + ci])

        @pl.when(step == grid[0] * grid[1] * grid[2] - 1)
        def _():
            comm.wait(cin, cout, sems)

    outs = pl.pallas_call(
        hosted, name=name + "_comm", grid=grid, in_specs=specs + [_ANY] * ci, out_specs=[out_spec] + [_ANY] * co,
        out_shape=[jax.ShapeDtypeStruct((m, n), F32)] + comm.out_shapes, scratch_shapes=comm.sems,
        input_output_aliases={2 + i: 1 + o for i, o in comm.aliases.items()},
        compiler_params=_params("arbitrary", "arbitrary", "arbitrary"),
    )(a, b, *comm.ins)
    return outs[0], outs[1:]


def _row_block(rows, cols, target_bytes):
    target = max(8, target_bytes // (4 * cols))
    if rows <= target:
        return rows
    best = None
    for br in range(8, target + 1, 8):
        if rows % br == 0:
            best = br
    return best if best is not None else rows


def adamw(w, g, m, v):
    shape = w.shape
    rows, cols = (shape[-2], shape[-1]) if len(shape) > 1 else (1, shape[-1])
    lead = math.prod(shape[:-2]) if len(shape) > 2 else 1
    br = _row_block(rows, cols, 1 << 20)
    bl = 1
    if br == rows:
        for cand in range(1, lead + 1):
            if lead % cand == 0 and cand * rows * cols * 4 <= (1 << 20):
                bl = cand

    def body(w_ref, g_ref, m_ref, v_ref, d_ref, nm_ref, nv_ref):
        gv = g_ref[...]
        nm = ADAM_B1 * m_ref[...] + (1.0 - ADAM_B1) * gv
        nv = ADAM_B2 * v_ref[...] + (1.0 - ADAM_B2) * jnp.square(gv)
        m_hat = nm / (1.0 - ADAM_B1 ** ADAM_STEP)
        v_hat = nv / (1.0 - ADAM_B2 ** ADAM_STEP)
        d_ref[...] = -ADAM_LR * (m_hat / (jnp.sqrt(v_hat) + ADAM_EPS) + ADAM_WD * w_ref[...])
        nm_ref[...] = nm
        nv_ref[...] = nv

    spec = pl.BlockSpec((bl, br, cols), lambda l, i: (l, i, 0))
    outs = pl.pallas_call(
        body, name="adamw", grid=(lead // bl, rows // br),
        in_specs=[spec] * 4, out_specs=[spec] * 3,
        out_shape=[jax.ShapeDtypeStruct((lead, rows, cols), F32)] * 3,
        compiler_params=pltpu.CompilerParams(dimension_semantics=("parallel", "parallel")),
    )(*(a.reshape(lead, rows, cols) for a in (w, g, m, v)))
    return tuple(o.reshape(shape) for o in outs)


ADD_STEPS = 4


def add_halves(geoms, arrs, received, c_idx, dtypes):
    n = len(arrs)

    def body(c_ref, *refs):
        del c_ref
        for a in range(n):
            refs[2 * n + a][...] = (refs[a][...] + refs[n + a][...]).astype(dtypes[a])

    own_specs, half_specs = [], []
    for g in geoms:
        if g.kind == "cols":
            rows, cols = g.half_shape[0] // ADD_STEPS, g.half_shape[1]
            own_specs.append(pl.BlockSpec((rows, cols), lambda i, c_ref: (ADD_STEPS * c_ref[0] + i, 0)))
            half_specs.append(pl.BlockSpec((rows, cols), lambda i, c_ref: (i, 0)))
        else:
            _, h, cols = g.half_shape
            own_specs.append(pl.BlockSpec((None, None, h, cols), lambda i, c_ref: (i, c_ref[0], 0, 0)))
            half_specs.append(pl.BlockSpec((None, h, cols), lambda i, c_ref: (i, 0, 0)))
    return pl.pallas_call(
        body, name="add_halves",
        grid_spec=pltpu.PrefetchScalarGridSpec(num_scalar_prefetch=1, grid=(ADD_STEPS,),
                                               in_specs=own_specs + half_specs, out_specs=half_specs),
        out_shape=[jax.ShapeDtypeStruct(g.half_shape, dt) for g, dt in zip(geoms, dtypes)],
        compiler_params=_params("parallel"),
    )(c_idx, *arrs, *received)


def add_parts(geoms, landed, finals, layer, c_idx):
    n = len(landed)

    def body(c_ref, *refs):
        del c_ref
        for a in range(n):
            p_ref = refs[a]
            parts = [p_ref[j].astype(F32) for j in range(N_CHIPS)]
            refs[2 * n + a][...] = ((parts[0] + parts[1]) + parts[2]) + parts[3]

    in_specs, out_specs = [], []
    for g in geoms:
        rows, cols = g.part_shape[0] // ADD_STEPS, g.part_shape[1]
        in_specs.append(pl.BlockSpec((N_CHIPS, rows, cols), lambda i, c_ref: (0, i, 0)))
        out_specs.append(pl.BlockSpec((None, rows, cols), lambda i, c_ref: (layer, ADD_STEPS * c_ref[0] + i, 0)))
    return pl.pallas_call(
        body, name="add_parts",
        grid_spec=pltpu.PrefetchScalarGridSpec(num_scalar_prefetch=1, grid=(ADD_STEPS,),
                                               in_specs=in_specs + [_ANY] * n, out_specs=out_specs),
        out_shape=[jax.ShapeDtypeStruct(f.shape, F32) for f in finals],
        input_output_aliases={1 + n + a: a for a in range(n)},
        compiler_params=_params("parallel"),
    )(c_idx, *landed, *finals)


def _shard_dims(k, n, axis):
    return (k // N_CHIPS, n) if axis == 0 else (k, n // N_CHIPS)


W_IN_STRIDE = 512
W_IN_WINDOW = 640


def _big_geoms():
    geoms = []
    for name, k, n, axis in BIG:
        if axis == 0:
            geoms.append(_Geom("rows", (N_CHIPS, 2, k // N_CHIPS // 2, n)))
        elif name == "w_in":
            geoms.append(_Geom("cols", (k, n), W_IN_STRIDE, W_IN_WINDOW))
        else:
            geoms.append(_Geom("cols", (k, n), n // N_CHIPS, n // N_CHIPS))
    return geoms


def _grad_views(gb, geoms):
    return [gb[name].reshape(g.shape) for (name, _, _, _), g in zip(BIG, geoms)]


def _round_up(v, m):
    return (v + m - 1) // m * m


def _prep_small(small):
    tril = jnp.tril(jnp.ones((CHUNK, CHUNK), bool))
    wm = jnp.where(tril, small["sgu_w"], 0.0).astype(BF16).reshape(DEPTH, 3, 2, CHUNK, CHUNK)
    head = jnp.arange(D_A) // HEAD_DIM
    grp = jnp.arange(D_C) // HEAD_DIM
    pw_rows = small["pool_w"].reshape(DEPTH, D_C, HEAD_DIM)
    pool_bd = jnp.where((grp[:, None] == grp[None, :])[None], jnp.tile(pw_rows, (1, 1, D_C // HEAD_DIM)), 0.0).astype(BF16)
    return dict(
        wcat=wm.transpose(0, 1, 3, 2, 4).reshape(DEPTH, 3, CHUNK, 2 * CHUNK),
        wcat_t=wm.transpose(0, 1, 2, 4, 3).reshape(DEPTH, 3, 2 * CHUNK, CHUNK),
        bmat=jnp.repeat(jnp.swapaxes(small["sgu_b"], 1, 2), HEAD_DIM, axis=2),
        avg=jnp.where(head[:, None] == head[None, :], 1.0 / HEAD_DIM, 0.0).astype(BF16),
        pool_bd=pool_bd, pool_bd_t=jnp.swapaxes(pool_bd, 1, 2),
        conv8=jnp.pad(small["conv_w"], ((0, 0), (0, 8 - 3), (0, 0))),
    )


def _row(a):
    return a.reshape(1, -1)


MIX_WEIGHTS = ("w_in", "w_out")
MLP_WEIGHTS = ("w_ff1", "w_ff2", "w_ple_gate", "w_ple_proj")
ALL_BIG = MIX_WEIGHTS + MLP_WEIGHTS
FFN_BWD_TILE = 512
PLE_BWD_TILE = 1024


def _fwd_layer(h, p, wl, small, prep, l, tm, comm_mix=None, comm_mlp=None, target=None):
    (proj, ycat, x1), got = mix_fwd(h, _row(small["norm_mix_g"][l]), wl["w_in"], wl["w_out"], prep["wcat"][l],
                                    prep["bmat"][l], _row(small["sgu_ln_g"][l]), _row(small["sgu_ln_b"][l]), prep["avg"],
                                    prep["conv8"][l], prep["pool_bd"][l], _row(small["pool_scale"][l]),
                                    tm=min(2 * tm, h.shape[0]), comm=comm_mix)
    if comm_mix is not None:
        wl = {**wl, **_weights_of(got, MLP_WEIGHTS)}
    if target is not None:
        a, x2, loss_blk, d_final_g, d = ffn_ple_loss_fwd(
            x1, (p, l), _row(small["norm_ff_g"][l]), wl["w_ff1"], wl["w_ff2"], _row(small["norm_ple_g"][l]),
            wl["w_ple_gate"], wl["w_ple_proj"], target, _row(small["final_g"]), tm=tm)
        return (h, proj, ycat, x1, a, x2), (loss_blk, d_final_g, d), [], wl
    (a, x2, x3), couts = ffn_ple_fwd(x1, (p, l), _row(small["norm_ff_g"][l]), wl["w_ff1"], wl["w_ff2"],
                                     _row(small["norm_ple_g"][l]), wl["w_ple_gate"], wl["w_ple_proj"], tm=tm,
                                     comm=comm_mlp)
    return (h, proj, ycat, x1, a, x2), x3, couts, wl


def _merge_comms(comms):
    comms = [cm for cm in comms if cm is not None]
    if len(comms) <= 1:
        return comms[0] if comms else None
    spans, ni, no, ns = [], 0, 0, 0
    for cm in comms:
        spans.append((ni, no, ns))
        ni, no, ns = ni + len(cm.ins), no + len(cm.out_shapes), ns + len(cm.sems)

    def copies(in_refs, out_refs, sem_refs):
        local, sends, recvs = [], [], []
        for cm, (i0, o0, s0) in zip(comms, spans):
            got = cm.copies(in_refs[i0:i0 + len(cm.ins)], out_refs[o0:o0 + len(cm.out_shapes)],
                            sem_refs[s0:s0 + len(cm.sems)])
            local, sends, recvs = local + got[0], sends + got[1], recvs + got[2]
        return local, sends, recvs

    aliases = {i0 + i: o0 + o for cm, (i0, o0, _) in zip(comms, spans) for i, o in cm.aliases.items()}
    assert all(cm.forwards is None for cm in comms)
    return _Comm(sum((cm.ins for cm in comms), []), sum((cm.out_shapes for cm in comms), []),
                 sum((cm.sems for cm in comms), []), copies, aliases)


def _split_results(results, comms):
    out, at = [], 0
    for cm in comms:
        if cm is None:
            out.append(None)
        else:
            out.append(results[at:at + len(cm.out_shapes)])
            at += len(cm.out_shapes)
    return out


class _Reduction:
    def __init__(self, layer, names, geoms, arrs, finals, c_arr, narrow=()):
        self.layer, self.names, self.geoms, self.arrs = layer, list(names), list(geoms), list(arrs)
        self.finals, self.c_arr = finals, c_arr
        self.dtypes = [BF16 if n in narrow else F32 for n in self.names]

    def comm_a(self):
        return _reduce_a_comm(self.geoms, self.arrs)

    def comm_b(self, received):
        return _reduce_b_comm(self.geoms, add_halves(self.geoms, self.arrs, received, self.c_arr, self.dtypes))

    def comm_c(self, landed):
        mine = add_parts(self.geoms, landed, [self.finals[n] for n in self.names], self.layer, self.c_arr)
        return _reduce_c_comm(self.geoms, mine, self.layer)

    def done(self, results):
        self.finals.update(zip(self.names, results))


class _Plan:
    def ple(self):
        return None

    def after_ple(self, results):
        pass

    def ffn(self):
        return None

    def after_ffn(self, results):
        pass

    def out_grad(self, gb):
        return None

    def after_out_grad(self, results):
        pass

    def before_mix(self, gb):
        pass

    def mix(self):
        return None

    def after_mix(self, results):
        pass


class _CarryPlan(_Plan):
    def __init__(self, above):
        self.above = above

    def ple(self):
        return self.above.comm_a()

    def after_ple(self, results):
        self.received = results

    def ffn(self):
        return self.above.comm_b(self.received)

    def after_ffn(self, results):
        self.landed = results

    def mix(self):
        return self.above.comm_c(self.landed)

    def after_mix(self, results):
        self.above.done(results)


class _LastPlan(_CarryPlan):
    def __init__(self, above, make):
        super().__init__(above)
        self.make = make

    def out_grad(self, gb):
        self.early = [self.make(MLP_WEIGHTS, gb)]
        return self.early[0].comm_a()

    def after_out_grad(self, results):
        self.early_received = [results]

    def before_mix(self, gb):
        self.early.append(self.make(("w_out",), gb))
        self.early_received.append(_run_comm(self.early[1].comm_a(), "reduce_a_early"))

    def mix(self):
        self.parts = [self.above.comm_c(self.landed)] + [r.comm_b(got) for r, got in zip(self.early, self.early_received)]
        return _merge_comms(self.parts)

    def after_mix(self, results):
        above_res, *self.early_landed = _split_results(results, self.parts)
        self.above.done(above_res)


def _bwd_layer(d, saved, p, wl, small, prep, l, tm, tk, plan=None):
    plan = plan or _Plan()
    xin, proj, ycat, x1, a, x2 = saved
    (dx2, dg_ple, dw_gate, dw_proj), res = ple_bwd(d, x2, (p, l), _row(small["norm_ple_g"][l]), wl["w_ple_gate"],
                                                   wl["w_ple_proj"], tm=min(PLE_BWD_TILE, d.shape[0]), comm=plan.ple())
    plan.after_ple(res)
    gb = {"w_ple_gate": dw_gate, "w_ple_proj": dw_proj}
    (dx1, h2, da, dg_ff), res = ffn_bwd(dx2, x1, a, _row(small["norm_ff_g"][l]), wl["w_ff1"], wl["w_ff2"],
                                        tm=FFN_BWD_TILE if tm >= FFN_BWD_TILE else tm, comm=plan.ffn())
    plan.after_ffn(res)
    gb["w_ff2"] = wgrad(a, dx2, tk=tk, relu_sq=True)
    gb["w_ff1"] = wgrad(h2, da, tk=tk)
    comm = plan.out_grad(gb)
    if comm is None:
        gb["w_out"] = wgrad(ycat, dx1, tk=tk)
    else:
        gb["w_out"], res = wgrad(ycat, dx1, tk=tk, comm=comm)
        plan.after_out_grad(res)
    plan.before_mix(gb)
    (dprev, h1, dproj, dg_mix, dws, dbm, dlng, dlnb, dcw, dpw, dps), res = mix_bwd(
        dx1, xin, proj, _row(small["norm_mix_g"][l]), wl["w_in"], wl["w_out"], prep["wcat"][l], prep["wcat_t"][l],
        prep["bmat"][l], _row(small["sgu_ln_g"][l]), _row(small["sgu_ln_b"][l]), prep["avg"], prep["conv8"][l],
        prep["pool_bd"][l], prep["pool_bd_t"][l], _row(small["pool_scale"][l]), tm=tm, comm=plan.mix())
    plan.after_mix(res)
    gb["w_in"] = wgrad(h1, dproj, tk=tk)
    gs = {
        "norm_ple_g": dg_ple[0], "norm_ff_g": dg_ff[0], "norm_mix_g": dg_mix[0],
        "sgu_w": dws.reshape(2 * 3, CHUNK, CHUNK), "sgu_b": dbm[:, ::HEAD_DIM].T,
        "sgu_ln_g": dlng[0], "sgu_ln_b": dlnb[0], "conv_w": dcw[0:3], "pool_scale": dps[0],
        "pool_w": jnp.stack([dpw[g * HEAD_DIM:(g + 1) * HEAD_DIM, g * HEAD_DIM:(g + 1) * HEAD_DIM]
                             for g in range(D_C // HEAD_DIM)]),
    }
    return dprev, gb, gs


def _local_step(x, p, target, full, small, *, tm, tk):
    prep = _prep_small(small)
    p = p[:, None]
    saved, h = [], x
    for l in range(DEPTH):
        wl = {name: full[name][l] for name in full}
        s, h, _, _ = _fwd_layer(h, p, wl, small, prep, l, tm, target=target if l == DEPTH - 1 else None)
        saved.append(s)
    loss_blk, d_final_g, d = h
    gbig, gsm = [None] * DEPTH, [None] * DEPTH
    for l in reversed(range(DEPTH)):
        wl = {name: full[name][l] for name in full}
        d, gbig[l], gsm[l] = _bwd_layer(d, saved[l], p, wl, small, prep, l, tm, tk)
    big = {name: jnp.stack([gbig[l][name] for l in range(DEPTH)]) for name in gbig[0]}
    sm = {name: jnp.stack([gsm[l][name] for l in range(DEPTH)]) for name in gsm[0]}
    sm["final_g"] = d_final_g[0]
    return loss_blk[0, 0], d, big, sm


def _weights_of(gathered, names):
    wl = dict(zip([b[0] for b in BIG if b[0] in names], gathered))
    if "w_in" in wl:
        wl["w_in"] = wl["w_in"].transpose(1, 0, 2).reshape(D_MODEL, D_IN)
    return wl


def kernel(x, p, norm_mix_g, w_in, sgu_w, sgu_b, sgu_ln_g, sgu_ln_b, conv_w, pool_w, pool_scale, w_out, norm_ff_g, w_ff1, w_ff2, norm_ple_g, w_ple_gate, w_ple_proj, final_g, loss_target, m_norm_mix_g, m_w_in, m_sgu_w, m_sgu_b, m_sgu_ln_g, m_sgu_ln_b, m_conv_w, m_pool_w, m_pool_scale, m_w_out, m_norm_ff_g, m_w_ff1, m_w_ff2, m_norm_ple_g, m_w_ple_gate, m_w_ple_proj, m_final_g, v_norm_mix_g, v_w_in, v_sgu_w, v_sgu_b, v_sgu_ln_g, v_sgu_ln_b, v_conv_w, v_pool_w, v_pool_scale, v_w_out, v_norm_ff_g, v_w_ff1, v_w_ff2, v_norm_ple_g, v_w_ple_gate, v_w_ple_proj, v_final_g):
    args = dict(locals())
    w = {name: args[name] for name in WEIGHTS}
    m = {name: args["m_" + name] for name in WEIGHTS}
    v = {name: args["v_" + name] for name in WEIGHTS}
    t = x.shape[1]
    tm = min(512, t)
    tk = min(2048, t)
    x_idx, y_idx, c_idx = _place()
    chip = 2 * x_idx + y_idx
    c_arr = c_idx.reshape(1).astype(jnp.int32)
    xs, target = x[0], loss_target[0]

    shards = {name: w[name].astype(BF16) for name, _, _, _ in BIG}
    conv_rows = _round_up(CONV_SHARD, 8 * 128) // 128
    conv_flat = jnp.pad(w["conv_w"].reshape(-1), (0, conv_rows * 128 - CONV_SHARD)).reshape(conv_rows, 128)
    first = _run_comm(_gather_halved_comm(shards, 0, MIX_WEIGHTS, conv_flat), "gather_first")
    conv_full = (first[len(MIX_WEIGHTS)].reshape(N_CHIPS, -1)[:, :CONV_SHARD]
                 .reshape(N_CHIPS, DEPTH, 3, D_B // N_CHIPS).transpose(1, 2, 0, 3).reshape(DEPTH, 3, D_B))
    small = {name: w[name] for name in SMALL}
    small["conv_w"] = conv_full
    prep = _prep_small(small)

    wl = [None] * DEPTH
    wl[0] = _weights_of(first, MIX_WEIGHTS)
    saved, h = [], xs
    for l in range(DEPTH):
        comm_mix = None
        if l == 0:
            comm_mix = _gather_halved_comm(shards, 0, MLP_WEIGHTS)
        comm_mlp = _gather_comm(shards, l + 1, ALL_BIG) if l + 1 < DEPTH else None
        s, h, got, wl[l] = _fwd_layer(h, p, wl[l], small, prep, l, tm, comm_mix, comm_mlp,
                                      target=target if l == DEPTH - 1 else None)
        saved.append(s)
        if comm_mlp is not None:
            wl[l + 1] = _weights_of(got, ALL_BIG)
    loss_blk, d_final_g, d = h

    geoms = dict(zip([b[0] for b in BIG], _big_geoms()))
    finals = {name: jnp.zeros((DEPTH,) + g.final_shape, F32) for name, g in geoms.items()}

    def reduction(layer, names, gb, narrow=()):
        return _Reduction(layer, names, [geoms[n] for n in names], [gb[n].reshape(geoms[n].shape) for n in names],
                          finals, c_arr, narrow)

    gsm = [None] * DEPTH
    above = None
    for l in reversed(range(DEPTH)):
        if above is None:
            plan = _Plan()
        elif l > 0:
            plan = _CarryPlan(above)
        else:
            plan = _LastPlan(above, lambda names, gb: reduction(0, names, gb))
        d, gb, gsm[l] = _bwd_layer(d, saved[l], p, wl[l], small, prep, l, tm, tk, plan)
        if l > 0:
            above = reduction(l, ALL_BIG, gb)

    sm = {name: jnp.stack([gsm[i][name] for i in range(DEPTH)]) for name in gsm[0]}
    sm["final_g"] = d_final_g[0]
    sizes = [sm[name].size for name in SMALL]
    small_rows = _round_up(-(-sum(sizes) // (2 * N_CHIPS * LANES)), 8 * ADD_STEPS)
    small_flat = jnp.pad(jnp.concatenate([sm[name].reshape(-1) for name in SMALL]),
                         (0, 2 * N_CHIPS * small_rows * LANES - sum(sizes)))
    geoms["small"] = _Geom("rows", (N_CHIPS, 2, small_rows, LANES))
    finals["small"] = jnp.zeros((1,) + geoms["small"].final_shape, F32)
    late = reduction(0, ("w_in", "small"), {**gb, "small": small_flat}, narrow=("w_in",))
    late_landed = _run_comm(late.comm_b(_run_comm(late.comm_a(), "reduce_a_late")), "reduce_b_late")
    finishing = plan.early + [late]
    last = [r.comm_c(got) for r, got in zip(finishing, plan.early_landed + [late_landed])]
    for r, res in zip(finishing, _split_results(_run_comm(_merge_comms(last), "reduce_c_last"), last)):
        r.done(res)

    grads = {name: finals[name] for name, _, _, _ in BIG}
    grads["w_in"] = lax.dynamic_slice_in_dim(grads["w_in"], chip * (D_IN // N_CHIPS - W_IN_STRIDE), D_IN // N_CHIPS, axis=2)
    small_red = _run_comm(_allgather_comm(finals["small"][0]), "small_allgather")[0].reshape(-1)
    off = 0
    for name, size in zip(SMALL, sizes):
        grads[name] = small_red[off:off + size].reshape(sm[name].shape)
        off += size
    grads["conv_w"] = lax.dynamic_slice_in_dim(grads["conv_w"], chip * (D_B // N_CHIPS), D_B // N_CHIPS, axis=2)

    loss = lax.psum(loss_blk[0, 0], ("x", "y", "c"))
    delta, new_m, new_v = {}, {}, {}
    for name in WEIGHTS:
        delta[name], new_m[name], new_v[name] = adamw(w[name], grads[name], m[name], v[name])
    return (loss, d[None], *[grads[n] for n in WEIGHTS], *[delta[n] for n in WEIGHTS],
            *[new_m[n] for n in WEIGHTS], *[new_v[n] for n in WEIGHTS])
```

```python
import math

import jax
import jax.numpy as jnp
from jax import lax
from jax.experimental import pallas as pl
from jax.experimental.pallas import tpu as pltpu

F32 = jnp.float32
BF16 = jnp.bfloat16

D_MODEL = 1024
DEPTH = 4
D_PLE = 256
D_FF = 4096
HEAD_DIM = 64
D_A = 384
D_B = 384
D_C = 256
D_IN = 2176
CHUNK = 128
HALO = 16
RMS_EPS = 1e-6
LN_EPS = 1e-5
N_CHIPS = 4
LANES = 1024

ADAM_LR = 0.001
ADAM_B1 = 0.9
ADAM_B2 = 0.999
ADAM_EPS = 1e-08
ADAM_WD = 0.01
ADAM_STEP = 10

VMEM_LIMIT_BYTES = 60 * 1024 * 1024

_RSQRT2 = 0.7071067811865476
_INV_SQRT_2PI = 0.3989422804014327

BIG = (
    ("w_in", D_MODEL, D_IN, 1),
    ("w_out", D_MODEL, D_MODEL, 0),
    ("w_ff1", D_MODEL, D_FF, 1),
    ("w_ff2", D_FF, D_MODEL, 0),
    ("w_ple_gate", D_MODEL, D_MODEL, 0),
    ("w_ple_proj", D_PLE, D_MODEL, 1),
)
SMALL = ("norm_mix_g", "sgu_w", "sgu_b", "sgu_ln_g", "sgu_ln_b", "conv_w", "pool_w", "pool_scale",
         "norm_ff_g", "norm_ple_g", "final_g")
WEIGHTS = ("norm_mix_g", "w_in", "sgu_w", "sgu_b", "sgu_ln_g", "sgu_ln_b", "conv_w", "pool_w", "pool_scale",
           "w_out", "norm_ff_g", "w_ff1", "w_ff2", "norm_ple_g", "w_ple_gate", "w_ple_proj", "final_g")
CONV_SHARD = DEPTH * 3 * (D_B // N_CHIPS)


def _dot(a, b):
    return jnp.dot(a, b, preferred_element_type=F32)


def _dot_nt(a, b):
    return lax.dot_general(a, b, (((1,), (1,)), ((), ())), preferred_element_type=F32)


def _dot_tn(a, b):
    return lax.dot_general(a, b, (((0,), (0,)), ((), ())), preferred_element_type=F32)


def _const_spec(shape):
    nd = len(shape)
    return pl.BlockSpec(shape, lambda i: (0,) * nd, pipeline_mode=pl.Buffered(1))


def _acc_spec(shape):
    nd = len(shape)
    return pl.BlockSpec(shape, lambda i: (0,) * nd)


def _layer_rows(layer, tm):
    return pl.BlockSpec((None, None, tm, D_PLE), lambda i: (layer, 0, i, 0))


def _params(*sem):
    return pltpu.CompilerParams(dimension_semantics=sem, vmem_limit_bytes=VMEM_LIMIT_BYTES)


def _rms_bwd(dh, n, rs, g):
    dn = dh * g
    return rs * (dn - n * jnp.mean(dn * n, axis=-1, keepdims=True))


def _gelu(x):
    return x * (0.5 * (1.0 + lax.erf(x * _RSQRT2)))


def _gelu_and_grad(x):
    cdf = 0.5 * (1.0 + lax.erf(x * _RSQRT2))
    return x * cdf, cdf + x * (jnp.exp(-0.5 * x * x) * _INV_SQRT_2PI)


def _group_mean(v, avg):
    vb = v.astype(BF16)
    split = 2 * CHUNK
    return jnp.concatenate([_dot(vb[:, :split], avg[:split, :split]), _dot(vb[:, split:], avg[split:, split:])], axis=1)


def _group_mean_split(v, avg):
    hi = v.astype(BF16)
    lo = (v - hi.astype(F32)).astype(BF16)
    return _dot(hi, avg) + _dot(lo, avg)


def _lane_lt(shape, bound):
    return lax.broadcasted_iota(jnp.int32, shape, 1) < bound


def _sgu_mix(vnb2, wcat_j, lo_mask):
    zero = jnp.zeros_like(vnb2)
    stacked = jnp.concatenate([jnp.where(lo_mask, vnb2, zero), jnp.where(lo_mask, zero, vnb2)], axis=0)
    return _dot(wcat_j, stacked)


def _pool_means(ext, tile_rows, first_pos):
    s2 = ext + pltpu.roll(ext, 1, 0)
    s4 = s2 + pltpu.roll(s2, 2, 0)
    s8 = s4 + pltpu.roll(s4, 4, 0)
    s16 = s8 + pltpu.roll(s8, 8, 0)
    pos = (first_pos + lax.broadcasted_iota(jnp.int32, (tile_rows, 1), 0) + 1).astype(F32)
    lane = lax.broadcasted_iota(jnp.int32, (tile_rows, D_C), 1)
    sums = jnp.where(lane < 64, s2[HALO:], jnp.where(lane < 128, s4[HALO:], jnp.where(lane < 192, s8[HALO:], s16[HALO:])))
    win = jnp.where(lane < 64, 2.0, jnp.where(lane < 128, 4.0, jnp.where(lane < 192, 8.0, 16.0)))
    inv = 1.0 / jnp.minimum(pos, win)
    return sums * inv, inv


MESH = pl.DeviceIdType.MESH
_ANY = pl.BlockSpec(memory_space=pl.ANY)


def _place():
    return lax.axis_index("x"), lax.axis_index("y"), lax.axis_index("c")


def _chip_peers(x, y):
    return [(1 - x, y), (x, 1 - y), (1 - x, 1 - y)]


class _Comm:
    def __init__(self, ins, out_shapes, sems, copies, aliases=None, forwards=None):
        self.ins, self.out_shapes, self.sems, self.copies = list(ins), list(out_shapes), list(sems), copies
        self.aliases = dict(aliases or {})
        self.forwards = forwards

    def start(self, in_refs, out_refs, sem_refs):
        local, sends, _ = self.copies(in_refs, out_refs, sem_refs)
        for cp in local + sends:
            cp.start()

    def wait(self, in_refs, out_refs, sem_refs):
        local, sends, recvs = self.copies(in_refs, out_refs, sem_refs)
        for cp in recvs:
            cp.wait_recv()
        passed, passed_in = self.forwards(in_refs, out_refs, sem_refs) if self.forwards else ([], [])
        for cp in passed:
            cp.start()
        for cp in sends:
            cp.wait_send()
        for cp in local:
            cp.wait()
        for cp in passed_in:
            cp.wait_recv()
        for cp in passed:
            cp.wait_send()


def _remote(src, dst, send_sem, recv_sem, device):
    return pltpu.make_async_remote_copy(src_ref=src, dst_ref=dst, send_sem=send_sem, recv_sem=recv_sem,
                                        device_id=device, device_id_type=MESH)


def _gather_comm(shards, layer, names, conv=None):
    mats = [b for b in BIG if b[0] in names]
    ins = [shards[name] for name, _, _, _ in mats] + ([conv] if conv is not None else [])
    out_shapes = []
    for name, k, n, axis in mats:
        shape = (N_CHIPS, k, n // N_CHIPS) if name == "w_in" else (k, n)
        out_shapes.append(jax.ShapeDtypeStruct(shape, BF16))
    if conv is not None:
        out_shapes.append(jax.ShapeDtypeStruct((N_CHIPS,) + conv.shape, conv.dtype))
    n_arr = len(ins)

    def block(a, out_ref, chip):
        if a == len(mats) or mats[a][0] == "w_in":
            return out_ref.at[chip]
        _, k, n, axis = mats[a]
        if axis == 0:
            return out_ref.at[pl.ds(chip * (k // N_CHIPS), k // N_CHIPS), :]
        return out_ref.at[:, pl.ds(chip * (n // N_CHIPS), n // N_CHIPS)]

    def copies(in_refs, out_refs, sem_refs):
        send_sems, recv_sems, local_sems = sem_refs
        x, y, c = _place()
        me = 2 * x + y
        local, sends, recvs = [], [], []
        for a in range(n_arr):
            src = in_refs[a].at[layer] if a < len(mats) else in_refs[a]
            local.append(pltpu.make_async_copy(src, block(a, out_refs[a], me), local_sems.at[a]))
            for j, (px, py) in enumerate(_chip_peers(x, y)):
                sends.append(_remote(src, block(a, out_refs[a], me), send_sems.at[a, j], recv_sems.at[a, j], (px, py, c)))
                recvs.append(_remote(src, block(a, out_refs[a], 2 * px + py), send_sems.at[a, j], recv_sems.at[a, j],
                                     (px, py, c)))
        return local, sends, recvs

    sems = [pltpu.SemaphoreType.DMA((n_arr, 3)), pltpu.SemaphoreType.DMA((n_arr, 3)), pltpu.SemaphoreType.DMA((n_arr,))]
    return _Comm(ins, out_shapes, sems, copies)


def _gather_halved_comm(shards, layer, names, conv=None):
    mats = [b for b in BIG if b[0] in names]
    ins = [shards[name] for name, _, _, _ in mats] + ([conv] if conv is not None else [])
    out_shapes = []
    for name, k, n, axis in mats:
        out_shapes.append(jax.ShapeDtypeStruct((N_CHIPS, k, n // N_CHIPS) if name == "w_in" else (k, n), BF16))
    if conv is not None:
        out_shapes.append(jax.ShapeDtypeStruct((N_CHIPS,) + conv.shape, conv.dtype))
    n_arr = len(ins)

    def whole(a, ref, chip):
        if a == len(mats) or mats[a][0] == "w_in":
            return ref.at[chip]
        _, k, n, axis = mats[a]
        if axis == 0:
            return ref.at[pl.ds(chip * (k // N_CHIPS), k // N_CHIPS), :]
        return ref.at[:, pl.ds(chip * (n // N_CHIPS), n // N_CHIPS)]

    def src_half(a, in_ref, core):
        if a == len(mats):
            rows = conv.shape[0] // 2
            return in_ref.at[pl.ds(core * rows, rows), :]
        rows = shards[mats[a][0]].shape[1] // 2
        return in_ref.at[layer, pl.ds(core * rows, rows), :]

    def half(a, out_ref, chip, core):
        if a == len(mats):
            rows = conv.shape[0] // 2
            return out_ref.at[chip, pl.ds(core * rows, rows), :]
        name, k, n, axis = mats[a]
        if name == "w_in":
            return out_ref.at[chip, pl.ds(core * (k // 2), k // 2), :]
        if axis == 0:
            rows = k // N_CHIPS // 2
            return out_ref.at[pl.ds(chip * 2 * rows + core * rows, rows), :]
        return out_ref.at[pl.ds(core * (k // 2), k // 2), pl.ds(chip * (n // N_CHIPS), n // N_CHIPS)]

    def copies(in_refs, out_refs, sem_refs):
        send_sems, recv_sems, own_send, own_recv = sem_refs[0], sem_refs[1], sem_refs[4], sem_refs[5]
        x, y, c = _place()
        me = 2 * x + y
        sends, recvs = [], []
        for a in range(n_arr):
            own = in_refs[a].at[layer] if a < len(mats) else in_refs[a]
            cp = _remote(own, whole(a, out_refs[a], me), own_send.at[a], own_recv.at[a], (x, y, 1 - c))
            sends.append(cp)
            recvs.append(cp)
            for j, (px, py) in enumerate(_chip_peers(x, y)):
                sends.append(_remote(src_half(a, in_refs[a], c), half(a, out_refs[a], me, c), send_sems.at[a, j],
                                     recv_sems.at[a, j], (px, py, c)))
                recvs.append(_remote(src_half(a, in_refs[a], c), half(a, out_refs[a], 2 * px + py, c), send_sems.at[a, j],
                                     recv_sems.at[a, j], (px, py, c)))
        return [], sends, recvs

    def forwards(in_refs, out_refs, sem_refs):
        send_sems, recv_sems = sem_refs[2], sem_refs[3]
        x, y, c = _place()
        sends, recvs = [], []
        for a in range(n_arr):
            for j, (px, py) in enumerate(_chip_peers(x, y)):
                peer = 2 * px + py
                sends.append(_remote(half(a, out_refs[a], peer, c), half(a, out_refs[a], peer, c), send_sems.at[a, j],
                                     recv_sems.at[a, j], (x, y, 1 - c)))
                recvs.append(_remote(half(a, out_refs[a], peer, c), half(a, out_refs[a], peer, 1 - c), send_sems.at[a, j],
                                     recv_sems.at[a, j], (x, y, 1 - c)))
        return sends, recvs

    sems = [pltpu.SemaphoreType.DMA((n_arr, 3))] * 4 + [pltpu.SemaphoreType.DMA((n_arr,))] * 2
    return _Comm(ins, out_shapes, sems, copies, forwards=forwards)


class _Geom:
    def __init__(self, kind, shape, stride=None, width=None, everywhere=False):
        self.kind, self.shape, self.stride, self.width = kind, tuple(shape), stride, width
        self.everywhere = everywhere
        if kind == "cols":
            k, n = shape
            self.half_shape, self.part_shape, self.final_shape = (k // 2, n), (k // 2, width), (k, width)
        else:
            _, _, h, n = shape
            self.half_shape, self.part_shape, self.final_shape = (N_CHIPS, h, n), (h, n), (2 * h, n)

    def half(self, ref, core):
        if self.kind == "cols":
            return ref.at[pl.ds(core * self.half_shape[0], self.half_shape[0]), :]
        return ref.at[:, core]

    def part(self, ref, chip):
        if self.kind == "cols":
            return ref.at[:, pl.ds(chip * self.stride, self.width)]
        return ref.at[chip]

    def final_half(self, ref, layer, core):
        rows = self.part_shape[0]
        return ref.at[layer, pl.ds(core * rows, rows), :]


def _reduce_a_comm(geoms, arrs):
    n = len(arrs)

    def copies(in_refs, out_refs, sem_refs):
        x, y, c = _place()
        cps = [_remote(geoms[a].half(in_refs[a], 1 - c), out_refs[a], sem_refs[0].at[a], sem_refs[1].at[a], (x, y, 1 - c))
               for a in range(n)]
        return [], cps, cps

    return _Comm(arrs, [jax.ShapeDtypeStruct(g.half_shape, F32) for g in geoms],
                 [pltpu.SemaphoreType.DMA((n,)), pltpu.SemaphoreType.DMA((n,))], copies)


def _reduce_b_comm(geoms, halves):
    n = len(halves)

    def copies(in_refs, out_refs, sem_refs):
        send_sems, recv_sems, local_sems = sem_refs
        x, y, c = _place()
        me = 2 * x + y
        local, sends, recvs = [], [], []
        for a in range(n):
            g = geoms[a]
            local.append(pltpu.make_async_copy(g.part(in_refs[a], me), out_refs[a].at[me], local_sems.at[a]))
            for j, (px, py) in enumerate(_chip_peers(x, y)):
                peer = 2 * px + py
                sends.append(_remote(g.part(in_refs[a], peer), out_refs[a].at[me], send_sems.at[a, j], recv_sems.at[a, j],
                                     (px, py, c)))
                recvs.append(_remote(g.part(in_refs[a], me), out_refs[a].at[peer], send_sems.at[a, j], recv_sems.at[a, j],
                                     (px, py, c)))
        return local, sends, recvs

    sems = [pltpu.SemaphoreType.DMA((n, 3)), pltpu.SemaphoreType.DMA((n, 3)), pltpu.SemaphoreType.DMA((n,))]
    return _Comm(halves, [jax.ShapeDtypeStruct((N_CHIPS,) + g.part_shape, h.dtype) for g, h in zip(geoms, halves)], sems,
                 copies)


def _reduce_c_comm(geoms, finals, layer):
    n = len(finals)

    def copies(in_refs, out_refs, sem_refs):
        send_sems, recv_sems = sem_refs
        x, y, c = _place()
        me = 2 * x + y
        sends, recvs = [], []
        for a in range(n):
            g = geoms[a]
            at = me if g.everywhere else layer
            mine = g.final_half(in_refs[a], at, c)
            sends.append(_remote(mine, g.final_half(out_refs[a], at, c), send_sems.at[a, 0], recv_sems.at[a, 0],
                                 (x, y, 1 - c)))
            recvs.append(_remote(mine, g.final_half(out_refs[a], at, 1 - c), send_sems.at[a, 0], recv_sems.at[a, 0],
                                 (x, y, 1 - c)))
            if not g.everywhere:
                continue
            for j, (px, py) in enumerate(_chip_peers(x, y)):
                for core, slot in ((c, 1 + 2 * j), (1 - c, 2 + 2 * j)):
                    sends.append(_remote(mine, g.final_half(out_refs[a], me, c), send_sems.at[a, slot],
                                         recv_sems.at[a, slot], (px, py, core)))
                    recvs.append(_remote(mine, g.final_half(out_refs[a], 2 * px + py, core), send_sems.at[a, slot],
                                         recv_sems.at[a, slot], (px, py, core)))
        return [], sends, recvs

    sems = [pltpu.SemaphoreType.DMA((n, 7)), pltpu.SemaphoreType.DMA((n, 7))]
    return _Comm(finals, [jax.ShapeDtypeStruct(f.shape, f.dtype) for f in finals], sems, copies,
                 aliases={a: a for a in range(n)})


def _run_comm(comm, name):
    def body(*refs):
        ni, no = len(comm.ins), len(comm.out_shapes)
        in_refs, out_refs, sem_refs = refs[:ni], refs[ni:ni + no], refs[ni + no:]
        comm.start(in_refs, out_refs, sem_refs)
        comm.wait(in_refs, out_refs, sem_refs)

    return pl.pallas_call(
        body, name=name, in_specs=[_ANY] * len(comm.ins), out_specs=[_ANY] * len(comm.out_shapes),
        out_shape=comm.out_shapes, scratch_shapes=comm.sems, input_output_aliases=comm.aliases,
        compiler_params=pltpu.CompilerParams(has_side_effects=True),
    )(*comm.ins)


def _tile_call(body, name, nt, in_specs, out_specs, out_shape, scratch, args, comm):
    if comm is None:
        outs = pl.pallas_call(body, name=name, grid=(nt,), in_specs=in_specs, out_specs=out_specs, out_shape=out_shape,
                              scratch_shapes=scratch, compiler_params=_params("arbitrary"))(*args)
        return outs, []
    n_in, n_out, n_scr = len(in_specs), len(out_specs), len(scratch)
    ci, co = len(comm.ins), len(comm.out_shapes)

    def hosted(*refs):
        in_refs = refs[:n_in]
        cin = refs[n_in:n_in + ci]
        out_refs = refs[n_in + ci:n_in + ci + n_out]
        cout = refs[n_in + ci + n_out:n_in + ci + n_out + co]
        scr = refs[n_in + ci + n_out + co:n_in + ci + n_out + co + n_scr]
        sems = refs[n_in + ci + n_out + co + n_scr:]
        i = pl.program_id(0)

        @pl.when(i == 0)
        def _():
            comm.start(cin, cout, sems)

        body(*in_refs, *out_refs, *scr)

        @pl.when(i == nt - 1)
        def _():
            comm.wait(cin, cout, sems)

    outs = pl.pallas_call(
        hosted, name=name + "_comm", grid=(nt,),
        in_specs=list(in_specs) + [_ANY] * ci, out_specs=list(out_specs) + [_ANY] * co,
        out_shape=list(out_shape) + comm.out_shapes, scratch_shapes=list(scratch) + comm.sems,
        input_output_aliases={n_in + i: n_out + o for i, o in comm.aliases.items()},
        compiler_params=_params("arbitrary"),
    )(*args, *comm.ins)
    return outs[:n_out], outs[n_out:]


def mix_fwd(x, g_mix, w_in, w_out, wcat, bmat, ln_g, ln_b, avg, conv_w, pool_bd, pool_scale, *, tm, comm=None):
    t = x.shape[0]
    nt = t // tm

    def body(x_ref, g_ref, win_ref, wout_ref, wcat_ref, bmat_ref, lng_ref, lnb_ref, avg_ref, cw_ref, pw_ref, ps_ref,
             proj_ref, ycat_ref, x1_ref, hbuf, zbuf):
        i = pl.program_id(0)

        @pl.when(i == 0)
        def _():
            hbuf[0:HALO, :] = jnp.zeros((HALO, D_B), F32)
            zbuf[0:HALO, :] = jnp.zeros((HALO, D_C), F32)

        xv = x_ref[...]
        n = xv * lax.rsqrt(jnp.mean(xv * xv, axis=-1, keepdims=True) + RMS_EPS)
        h1 = (n * g_ref[...]).astype(BF16)
        proj_ref[...] = _dot(h1, win_ref[...])

        lo_mask = _lane_lt((CHUNK, CHUNK), HEAD_DIM)
        avg = avg_ref[...]
        gu = _gelu(proj_ref[:, 0:D_A])
        gv = _gelu(proj_ref[:, D_A:2 * D_A])
        dv = gv - _group_mean(gv, avg)
        var = _group_mean(dv * dv, avg)
        vnb = (dv * lax.rsqrt(var + LN_EPS) * lng_ref[...] + lnb_ref[...]).astype(BF16)
        for c in range(tm // CHUNK):
            rows = slice(c * CHUNK, (c + 1) * CHUNK)
            for j in range(3):
                cols = slice(j * CHUNK, (j + 1) * CHUNK)
                mixed = _sgu_mix(vnb[rows, cols], wcat_ref[j], lo_mask) + bmat_ref[:, cols]
                ycat_ref[rows, cols] = (gu[rows, cols] * mixed).astype(BF16)

        o = 2 * D_A
        hcur = proj_ref[:, o + 2 * D_B:o + 3 * D_B] * proj_ref[:, o:o + D_B]
        hbuf[HALO:HALO + tm, :] = hcur
        y = (cw_ref[2:3, :] * hcur + cw_ref[1:2, :] * hbuf[pl.ds(HALO - 1, tm), :]
             + cw_ref[0:1, :] * hbuf[pl.ds(HALO - 2, tm), :])
        ycat_ref[:, D_A:D_A + D_B] = (proj_ref[:, o + D_B:o + 2 * D_B] * y).astype(BF16)
        hbuf[0:HALO, :] = hbuf[tm:tm + HALO, :]

        zc = proj_ref[:, o + 3 * D_B:D_IN]
        zbuf[HALO:HALO + tm, :] = zc
        mean, _ = _pool_means(zbuf[...], tm, i * tm)
        pooled = (mean - zc).astype(BF16)
        ycat_ref[:, D_A + D_B:D_MODEL] = (_dot(pooled, pw_ref[...]) * ps_ref[...]).astype(BF16)
        zbuf[0:HALO, :] = zbuf[tm:tm + HALO, :]

        x1_ref[...] = xv + _dot(ycat_ref[...], wout_ref[...])

    row = lambda w: pl.BlockSpec((tm, w), lambda i: (i, 0))
    return _tile_call(
        body, "mix_fwd", nt,
        [row(D_MODEL), _const_spec((1, D_MODEL)), _const_spec((D_MODEL, D_IN)), _const_spec((D_MODEL, D_MODEL)),
         _const_spec((3, CHUNK, 2 * CHUNK)), _const_spec((CHUNK, D_A)), _const_spec((1, D_A)), _const_spec((1, D_A)),
         _const_spec((D_A, D_A)), _const_spec((8, D_B)), _const_spec((D_C, D_C)), _const_spec((1, D_C))],
        [row(D_IN), row(D_MODEL), row(D_MODEL)],
        [jax.ShapeDtypeStruct((t, D_IN), F32), jax.ShapeDtypeStruct((t, D_MODEL), BF16),
         jax.ShapeDtypeStruct((t, D_MODEL), F32)],
        [pltpu.VMEM((tm + HALO, D_B), F32), pltpu.VMEM((tm + HALO, D_C), F32)],
        (x, g_mix, w_in, w_out, wcat, bmat, ln_g, ln_b, avg, conv_w, pool_bd, pool_scale), comm)


def ffn_ple_fwd(x1, p, g_ff, w_ff1, w_ff2, g_ple, w_gate, w_proj, *, tm, comm=None):
    t = x1.shape[0]
    nt = t // tm
    nc = D_FF // D_MODEL

    def body(x1_ref, p_ref, gff_ref, w1_ref, w2_ref, gple_ref, wg_ref, wp_ref, a_ref, x2_ref, x3_ref):
        x1v = x1_ref[...]
        n2 = x1v * lax.rsqrt(jnp.mean(x1v * x1v, axis=-1, keepdims=True) + RMS_EPS)
        h2 = (n2 * gff_ref[...]).astype(BF16)
        acc = x1v
        for c in range(nc):
            cols = slice(c * D_MODEL, (c + 1) * D_MODEL)
            a = _dot(h2, w1_ref[:, cols])
            a_ref[:, cols] = a.astype(BF16)
            ra = jnp.maximum(a, 0.0)
            acc = acc + _dot((ra * ra).astype(BF16), w2_ref[cols, :])
        x2_ref[...] = acc
        n3 = acc * lax.rsqrt(jnp.mean(acc * acc, axis=-1, keepdims=True) + RMS_EPS)
        h3 = (n3 * gple_ref[...]).astype(BF16)
        gate = jax.nn.sigmoid(_dot(h3, wg_ref[...]))
        pp = _dot(p_ref[...].astype(BF16), wp_ref[...])
        x3_ref[...] = acc + pp * gate

    row = lambda w: pl.BlockSpec((tm, w), lambda i: (i, 0))
    return _tile_call(
        body, "ffn_ple_fwd", nt,
        [row(D_MODEL), _layer_rows(p[1], tm), _const_spec((1, D_MODEL)), _const_spec((D_MODEL, D_FF)),
         _const_spec((D_FF, D_MODEL)), _const_spec((1, D_MODEL)), _const_spec((D_MODEL, D_MODEL)),
         _const_spec((D_PLE, D_MODEL))],
        [row(D_FF), row(D_MODEL), row(D_MODEL)],
        [jax.ShapeDtypeStruct((t, D_FF), BF16), jax.ShapeDtypeStruct((t, D_MODEL), F32),
         jax.ShapeDtypeStruct((t, D_MODEL), F32)],
        [], (x1, p[0], g_ff, w_ff1, w_ff2, g_ple, w_gate, w_proj), comm)


def ffn_ple_loss_fwd(x1, p, g_ff, w_ff1, w_ff2, g_ple, w_gate, w_proj, target, g_final, *, tm):
    t = x1.shape[0]
    nt = t // tm
    nc = D_FF // D_MODEL

    def body(x1_ref, p_ref, gff_ref, w1_ref, w2_ref, gple_ref, wg_ref, wp_ref, t_ref, gfin_ref,
             a_ref, x2_ref, loss_ref, dg_ref, dx_ref, sq_acc):
        i = pl.program_id(0)

        @pl.when(i == 0)
        def _():
            sq_acc[...] = jnp.zeros_like(sq_acc)
            dg_ref[...] = jnp.zeros_like(dg_ref)

        x1v = x1_ref[...]
        n2 = x1v * lax.rsqrt(jnp.mean(x1v * x1v, axis=-1, keepdims=True) + RMS_EPS)
        h2 = (n2 * gff_ref[...]).astype(BF16)
        acc = x1v
        for c in range(nc):
            cols = slice(c * D_MODEL, (c + 1) * D_MODEL)
            a = _dot(h2, w1_ref[:, cols])
            a_ref[:, cols] = a.astype(BF16)
            ra = jnp.maximum(a, 0.0)
            acc = acc + _dot((ra * ra).astype(BF16), w2_ref[cols, :])
        x2_ref[...] = acc
        n3 = acc * lax.rsqrt(jnp.mean(acc * acc, axis=-1, keepdims=True) + RMS_EPS)
        h3 = (n3 * gple_ref[...]).astype(BF16)
        gate = jax.nn.sigmoid(_dot(h3, wg_ref[...]))
        pp = _dot(p_ref[...].astype(BF16), wp_ref[...])
        x3 = acc + pp * gate

        rs = lax.rsqrt(jnp.mean(x3 * x3, axis=-1, keepdims=True) + RMS_EPS)
        n = x3 * rs
        gv = gfin_ref[...]
        err = n * gv - t_ref[...]
        sq_acc[...] += jnp.sum(err * err, axis=0, keepdims=True)
        dy = err * (1.0 / D_MODEL)
        dg_ref[...] += jnp.sum(dy * n, axis=0, keepdims=True)
        dx_ref[...] = _rms_bwd(dy, n, rs, gv)

        @pl.when(i == nt - 1)
        def _():
            total = jnp.sum(sq_acc[...], axis=1, keepdims=True) * (0.5 / D_MODEL)
            loss_ref[...] = jnp.broadcast_to(total, loss_ref.shape)

    row = lambda w: pl.BlockSpec((tm, w), lambda i: (i, 0))
    outs, _ = _tile_call(
        body, "ffn_ple_loss_fwd", nt,
        [row(D_MODEL), _layer_rows(p[1], tm), _const_spec((1, D_MODEL)), _const_spec((D_MODEL, D_FF)),
         _const_spec((D_FF, D_MODEL)), _const_spec((1, D_MODEL)), _const_spec((D_MODEL, D_MODEL)),
         _const_spec((D_PLE, D_MODEL)), row(D_MODEL), _const_spec((1, D_MODEL))],
        [row(D_FF), row(D_MODEL), _acc_spec((8, 128)), _acc_spec((1, D_MODEL)), row(D_MODEL)],
        [jax.ShapeDtypeStruct((t, D_FF), BF16), jax.ShapeDtypeStruct((t, D_MODEL), F32),
         jax.ShapeDtypeStruct((8, 128), F32), jax.ShapeDtypeStruct((1, D_MODEL), F32),
         jax.ShapeDtypeStruct((t, D_MODEL), F32)],
        [pltpu.VMEM((1, D_MODEL), F32)], (x1, p[0], g_ff, w_ff1, w_ff2, g_ple, w_gate, w_proj, target, g_final), None)
    return outs


def ple_bwd(d, x2, p, g_ple, w_gate, w_proj, *, tm, comm=None):
    t = d.shape[0]
    nt = t // tm

    def body(d_ref, x2_ref, p_ref, g_ref, wg_ref, wp_ref, dx2_ref, dg_ref, dwg_ref, dwp_ref):
        i = pl.program_id(0)

        @pl.when(i == 0)
        def _():
            dg_ref[...] = jnp.zeros_like(dg_ref)
            dwg_ref[...] = jnp.zeros_like(dwg_ref)
            dwp_ref[...] = jnp.zeros_like(dwp_ref)

        dv = d_ref[...]
        x2v = x2_ref[...]
        rs = lax.rsqrt(jnp.mean(x2v * x2v, axis=-1, keepdims=True) + RMS_EPS)
        n3 = x2v * rs
        gv = g_ref[...]
        h3 = (n3 * gv).astype(BF16)
        gate = jax.nn.sigmoid(_dot(h3, wg_ref[...]))
        pb = p_ref[...].astype(BF16)
        pp = _dot(pb, wp_ref[...])
        dwp_ref[...] += _dot_tn(pb, (dv * gate).astype(BF16))
        dpre = (dv * pp * gate * (1.0 - gate)).astype(BF16)
        dwg_ref[...] += _dot_tn(h3, dpre)
        dh3 = _dot_nt(dpre, wg_ref[...])
        dg_ref[...] += jnp.sum(dh3 * n3, axis=0, keepdims=True)
        dx2_ref[...] = dv + _rms_bwd(dh3, n3, rs, gv)

    row = lambda w: pl.BlockSpec((tm, w), lambda i: (i, 0))
    return _tile_call(
        body, "ple_bwd", nt,
        [row(D_MODEL), row(D_MODEL), _layer_rows(p[1], tm), _const_spec((1, D_MODEL)), _const_spec((D_MODEL, D_MODEL)),
         _const_spec((D_PLE, D_MODEL))],
        [row(D_MODEL), _acc_spec((1, D_MODEL)), _acc_spec((D_MODEL, D_MODEL)), _acc_spec((D_PLE, D_MODEL))],
        [jax.ShapeDtypeStruct((t, D_MODEL), F32), jax.ShapeDtypeStruct((1, D_MODEL), F32),
         jax.ShapeDtypeStruct((D_MODEL, D_MODEL), F32), jax.ShapeDtypeStruct((D_PLE, D_MODEL), F32)],
        [], (d, x2, p[0], g_ple, w_gate, w_proj), comm)


def ffn_bwd(dx2, x1, a, g_ff, w_ff1, w_ff2, *, tm, comm=None):
    t = dx2.shape[0]
    nt = t // tm
    nc = D_FF // D_MODEL

    def body(dx2_ref, x1_ref, a_ref, g_ref, w1_ref, w2_ref, dx1_ref, h2_ref, da_ref, dg_ref):
        i = pl.program_id(0)

        @pl.when(i == 0)
        def _():
            dg_ref[...] = jnp.zeros_like(dg_ref)

        dv = dx2_ref[...]
        x1v = x1_ref[...]
        rs = lax.rsqrt(jnp.mean(x1v * x1v, axis=-1, keepdims=True) + RMS_EPS)
        n2 = x1v * rs
        gv = g_ref[...]
        h2_ref[...] = (n2 * gv).astype(BF16)
        dvb = dv.astype(BF16)
        dh2 = jnp.zeros((tm, D_MODEL), F32)
        for c in range(nc):
            cols = slice(c * D_MODEL, (c + 1) * D_MODEL)
            ra = jnp.maximum(a_ref[:, cols].astype(F32), 0.0)
            da = (_dot_nt(dvb, w2_ref[cols, :]) * (2.0 * ra)).astype(BF16)
            da_ref[:, cols] = da
            dh2 = dh2 + _dot_nt(da, w1_ref[:, cols])
        dg_ref[...] += jnp.sum(dh2 * n2, axis=0, keepdims=True)
        dx1_ref[...] = dv + _rms_bwd(dh2, n2, rs, gv)

    row = lambda w: pl.BlockSpec((tm, w), lambda i: (i, 0))
    return _tile_call(
        body, "ffn_bwd", nt,
        [row(D_MODEL), row(D_MODEL), row(D_FF), _const_spec((1, D_MODEL)), _const_spec((D_MODEL, D_FF)),
         _const_spec((D_FF, D_MODEL))],
        [row(D_MODEL), row(D_MODEL), row(D_FF), _acc_spec((1, D_MODEL))],
        [jax.ShapeDtypeStruct((t, D_MODEL), F32), jax.ShapeDtypeStruct((t, D_MODEL), BF16),
         jax.ShapeDtypeStruct((t, D_FF), BF16), jax.ShapeDtypeStruct((1, D_MODEL), F32)],
        [], (dx2, x1, a, g_ff, w_ff1, w_ff2), comm)


def mix_bwd(dx1, x, proj, g_mix, w_in, w_out, wcat, wcat_t, bmat, ln_g, ln_b, avg, conv_w, pool_bd, pool_bd_t,
            pool_scale, *, tm, comm=None):
    t = dx1.shape[0]
    nt = t // tm
    prev_blocks = tm // HALO

    def body(dx1_ref, x_ref, proj_ref, prev_ref, g_ref, win_ref, wout_ref, wcat_ref, wcatt_ref, bmat_ref, lng_ref,
             lnb_ref, avg_ref, cw_ref, pw_ref, pwt_ref, ps_ref,
             dx_ref, h1_ref, dproj_ref, dg_ref, dws_ref, dbm_ref, dlng_ref, dlnb_ref, dcw_ref, dpw_ref, dps_ref,
             dyc, dpj, hbuf, zbuf, dybuf, qbuf):
        i = pl.program_id(0)
        ti = nt - 1 - i

        @pl.when(i == 0)
        def _():
            for ref in (dg_ref, dws_ref, dbm_ref, dlng_ref, dlnb_ref, dcw_ref, dpw_ref, dps_ref):
                ref[...] = jnp.zeros_like(ref)
            dybuf[tm:tm + HALO, :] = jnp.zeros((HALO, D_B), F32)
            qbuf[tm:tm + HALO, :] = jnp.zeros((HALO, D_C), F32)

        dx1v = dx1_ref[...]
        dyc[...] = _dot_nt(dx1v.astype(BF16), wout_ref[...])

        lo_mask = _lane_lt((CHUNK, CHUNK), HEAD_DIM)
        avg = avg_ref[...]
        lng = lng_ref[...]
        gu, dgu = _gelu_and_grad(proj_ref[:, 0:D_A])
        gv, dgv = _gelu_and_grad(proj_ref[:, D_A:2 * D_A])
        cen = gv - _group_mean(gv, avg)
        rstd = lax.rsqrt(_group_mean(cen * cen, avg) + LN_EPS)
        vhat = cen * rstd
        vnb = (vhat * lng + lnb_ref[...]).astype(BF16)
        dya = dyc[:, 0:D_A]
        dm = dya * gu
        dmb = dm.astype(BF16)
        dvn_rows = []
        for c in range(tm // CHUNK):
            rows = slice(c * CHUNK, (c + 1) * CHUNK)
            dbm_ref[...] += dm[rows]
            dvn_parts = []
            for j in range(3):
                cols = slice(j * CHUNK, (j + 1) * CHUNK)
                vnb2 = vnb[rows, cols]
                mixed = _sgu_mix(vnb2, wcat_ref[j], lo_mask) + bmat_ref[:, cols]
                dpj[rows, cols] = dya[rows, cols] * mixed * dgu[rows, cols]
                dmb2 = dmb[rows, cols]
                zero = jnp.zeros_like(dmb2)
                dm_st = jnp.concatenate([jnp.where(lo_mask, dmb2, zero), jnp.where(lo_mask, zero, dmb2)], axis=0)
                dws_ref[j] += _dot_nt(dm_st, vnb2)
                dvn_st = _dot(wcatt_ref[j], dmb2)
                dvn_parts.append(jnp.where(lo_mask, dvn_st[0:CHUNK], dvn_st[CHUNK:2 * CHUNK]))
            dvn_rows.append(jnp.concatenate(dvn_parts, axis=1))
        dvn = jnp.concatenate(dvn_rows, axis=0)
        dlng_ref[...] += jnp.sum(dvn * vhat, axis=0, keepdims=True)
        dlnb_ref[...] += jnp.sum(dvn, axis=0, keepdims=True)
        dvh = dvn * lng
        dgv_in = rstd * (dvh - _group_mean(dvh, avg) - vhat * _group_mean(dvh * vhat, avg))
        dpj[:, D_A:2 * D_A] = dgv_in * dgv

        o = 2 * D_A
        live = (ti > 0).astype(F32)
        zb = proj_ref[:, o:o + D_B]
        gb = proj_ref[:, o + D_B:o + 2 * D_B]
        gc = proj_ref[:, o + 2 * D_B:o + 3 * D_B]
        hcur = gc * zb
        hbuf[0:HALO, :] = prev_ref[:, o + 2 * D_B:o + 3 * D_B] * prev_ref[:, o:o + D_B] * live
        hbuf[HALO:HALO + tm, :] = hcur
        hm1 = hbuf[pl.ds(HALO - 1, tm), :]
        hm2 = hbuf[pl.ds(HALO - 2, tm), :]
        y = cw_ref[2:3, :] * hcur + cw_ref[1:2, :] * hm1 + cw_ref[0:1, :] * hm2
        dout = dyc[:, D_A:D_A + D_B]
        dpj[:, o + D_B:o + 2 * D_B] = dout * y
        dy = dout * gb
        dcw_ref[2:3, :] += jnp.sum(dy * hcur, axis=0, keepdims=True)
        dcw_ref[1:2, :] += jnp.sum(dy * hm1, axis=0, keepdims=True)
        dcw_ref[0:1, :] += jnp.sum(dy * hm2, axis=0, keepdims=True)
        dybuf[0:tm, :] = dy
        dh = (cw_ref[2:3, :] * dy + cw_ref[1:2, :] * dybuf[pl.ds(1, tm), :] + cw_ref[0:1, :] * dybuf[pl.ds(2, tm), :])
        dybuf[tm:tm + HALO, :] = dybuf[0:HALO, :]
        dpj[:, o:o + D_B] = dh * gc
        dpj[:, o + 2 * D_B:o + 3 * D_B] = dh * zb

        zc = proj_ref[:, o + 3 * D_B:D_IN]
        zbuf[0:HALO, :] = prev_ref[:, o + 3 * D_B:D_IN] * live
        zbuf[HALO:HALO + tm, :] = zc
        mean, inv = _pool_means(zbuf[...], tm, ti * tm)
        pooled = (mean - zc).astype(BF16)
        dyp = dyc[:, D_A + D_B:D_MODEL]
        ps = ps_ref[...]
        dps_ref[...] += jnp.sum(dyp * _dot(pooled, pw_ref[...]), axis=0, keepdims=True)
        dpw = (dyp * ps).astype(BF16)
        dpw_ref[...] += _dot_tn(pooled, dpw)
        dpooled = _dot(dpw, pwt_ref[...])
        qbuf[0:tm, :] = dpooled * inv
        q = qbuf[...]
        nrows = tm + HALO
        f2 = q + pltpu.roll(q, nrows - 1, 0)
        f4 = f2 + pltpu.roll(f2, nrows - 2, 0)
        f8 = f4 + pltpu.roll(f4, nrows - 4, 0)
        f16 = f8 + pltpu.roll(f8, nrows - 8, 0)
        lane = lax.broadcasted_iota(jnp.int32, (tm, D_C), 1)
        ahead = jnp.where(lane < 64, f2[0:tm], jnp.where(lane < 128, f4[0:tm], jnp.where(lane < 192, f8[0:tm], f16[0:tm])))
        dpj[:, o + 3 * D_B:D_IN] = ahead - dpooled
        qbuf[tm:tm + HALO, :] = qbuf[0:HALO, :]

        dprojb = dpj[...].astype(BF16)
        dproj_ref[...] = dprojb
        dh1 = _dot_nt(dprojb, win_ref[...])
        xv = x_ref[...]
        rs = lax.rsqrt(jnp.mean(xv * xv, axis=-1, keepdims=True) + RMS_EPS)
        n1 = xv * rs
        gv1 = g_ref[...]
        h1_ref[...] = (n1 * gv1).astype(BF16)
        dg_ref[...] += jnp.sum(dh1 * n1, axis=0, keepdims=True)
        dx_ref[...] = dx1v + _rms_bwd(dh1, n1, rs, gv1)

        @pl.when(i == nt - 1)
        def _():
            tril = (lax.broadcasted_iota(jnp.int32, (2 * CHUNK, CHUNK), 0) % CHUNK
                    >= lax.broadcasted_iota(jnp.int32, (2 * CHUNK, CHUNK), 1))
            for j in range(3):
                dws_ref[j] = jnp.where(tril, dws_ref[j], 0.0)
            dbm_ref[...] = _group_mean_split(dbm_ref[...], avg) * float(HEAD_DIM)

    rev = lambda w: pl.BlockSpec((tm, w), lambda i: (nt - 1 - i, 0))
    prev = pl.BlockSpec((HALO, D_IN), lambda i: (jnp.maximum((nt - 1 - i) * prev_blocks - 1, 0), 0))
    acc_shapes = [(1, D_MODEL), (3, 2 * CHUNK, CHUNK), (CHUNK, D_A), (1, D_A), (1, D_A), (8, D_B), (D_C, D_C), (1, D_C)]
    return _tile_call(
        body, "mix_bwd", nt,
        [rev(D_MODEL), rev(D_MODEL), rev(D_IN), prev, _const_spec((1, D_MODEL)), _const_spec((D_MODEL, D_IN)),
         _const_spec((D_MODEL, D_MODEL)), _const_spec((3, CHUNK, 2 * CHUNK)), _const_spec((3, 2 * CHUNK, CHUNK)),
         _const_spec((CHUNK, D_A)), _const_spec((1, D_A)), _const_spec((1, D_A)), _const_spec((D_A, D_A)),
         _const_spec((8, D_B)), _const_spec((D_C, D_C)), _const_spec((D_C, D_C)), _const_spec((1, D_C))],
        [rev(D_MODEL), rev(D_MODEL), rev(D_IN)] + [_acc_spec(s) for s in acc_shapes],
        [jax.ShapeDtypeStruct((t, D_MODEL), F32), jax.ShapeDtypeStruct((t, D_MODEL), BF16),
         jax.ShapeDtypeStruct((t, D_IN), BF16)] + [jax.ShapeDtypeStruct(s, F32) for s in acc_shapes],
        [pltpu.VMEM((tm, D_MODEL), F32), pltpu.VMEM((tm, D_IN), F32),
         pltpu.VMEM((tm + HALO, D_B), F32), pltpu.VMEM((tm + HALO, D_C), F32),
         pltpu.VMEM((tm + HALO, D_B), F32), pltpu.VMEM((tm + HALO, D_C), F32)],
        (dx1, x, proj, proj, g_mix, w_in, w_out, wcat, wcat_t, bmat, ln_g, ln_b, avg, conv_w, pool_bd, pool_bd_t,
         pool_scale), comm)


def wgrad(a, b, *, tk, a_layer=None, relu_sq=False, comm=None):
    t, m = a.shape[-2:]
    n = b.shape[1]
    bm = min(m, 1024)
    bn = 1024 if n % 1024 == 0 else n
    nk = t // tk
    if a_layer is None:
        a_spec = pl.BlockSpec((tk, bm), lambda i, j, k: (k, i))
    else:
        a_spec = pl.BlockSpec((None, None, tk, bm), lambda i, j, k: (a_layer, 0, k, i))

    def body(a_ref, b_ref, o_ref):
        k = pl.program_id(2)

        @pl.when(k == 0)
        def _():
            o_ref[...] = jnp.zeros_like(o_ref)

        av = a_ref[...]
        if relu_sq:
            ra = jnp.maximum(av.astype(F32), 0.0)
            av = ra * ra
        o_ref[...] += _dot_tn(av.astype(BF16), b_ref[...].astype(BF16))

    name = f"wgrad_{m}x{n}" + ("_relu_sq" if relu_sq else "")
    grid = (m // bm, n // bn, nk)
    specs = [a_spec, pl.BlockSpec((tk, bn), lambda i, j, k: (k, j))]
    out_spec = pl.BlockSpec((bm, bn), lambda i, j, k: (i, j))
    if comm is None:
        return pl.pallas_call(body, name=name, grid=grid, in_specs=specs, out_specs=out_spec,
                              out_shape=jax.ShapeDtypeStruct((m, n), F32),
                              compiler_params=_params("parallel", "parallel", "arbitrary"))(a, b)
    ci, co = len(comm.ins), len(comm.out_shapes)

    def hosted(*refs):
        cin, cout, sems = refs[2:2 + ci], refs[3 + ci:3 + ci + co], refs[3 + ci + co:]
        step = (pl.program_id(0) * grid[1] + pl.program_id(1)) * grid[2] + pl.program_id(2)

        @pl.when(step == 0)
        def _():
            comm.start(cin, cout, sems)

        body(refs[0], refs[1], refs[2 + ci])

        @pl.when(step == grid[0] * grid[1] * grid[2] - 1)
        def _():
            comm.wait(cin, cout, sems)

    outs = pl.pallas_call(
        hosted, name=name + "_comm", grid=grid, in_specs=specs + [_ANY] * ci, out_specs=[out_spec] + [_ANY] * co,
        out_shape=[jax.ShapeDtypeStruct((m, n), F32)] + comm.out_shapes, scratch_shapes=comm.sems,
        input_output_aliases={2 + i: 1 + o for i, o in comm.aliases.items()},
        compiler_params=_params("arbitrary", "arbitrary", "arbitrary"),
    )(a, b, *comm.ins)
    return outs[0], outs[1:]


def _row_block(rows, cols, target_bytes):
    target = max(8, target_bytes // (4 * cols))
    if rows <= target:
        return rows
    best = None
    for br in range(8, target + 1, 8):
        if rows % br == 0:
            best = br
    return best if best is not None else rows


def adamw(w, g, m, v):
    shape = w.shape
    cols = shape[-1]
    rows = math.prod(shape[:-1]) if len(shape) > 1 else 1
    br = _row_block(rows, cols, 1 << 20)

    def body(w_ref, g_ref, m_ref, v_ref, d_ref, nm_ref, nv_ref):
        gv = g_ref[...]
        nm = ADAM_B1 * m_ref[...] + (1.0 - ADAM_B1) * gv
        nv = ADAM_B2 * v_ref[...] + (1.0 - ADAM_B2) * jnp.square(gv)
        m_hat = nm / (1.0 - ADAM_B1 ** ADAM_STEP)
        v_hat = nv / (1.0 - ADAM_B2 ** ADAM_STEP)
        d_ref[...] = -ADAM_LR * (m_hat / (jnp.sqrt(v_hat) + ADAM_EPS) + ADAM_WD * w_ref[...])
        nm_ref[...] = nm
        nv_ref[...] = nv

    spec = pl.BlockSpec((br, cols), lambda i: (i, 0))
    outs = pl.pallas_call(
        body, name="adamw", grid=(rows // br,),
        in_specs=[spec] * 4, out_specs=[spec] * 3,
        out_shape=[jax.ShapeDtypeStruct((rows, cols), F32)] * 3,
        compiler_params=pltpu.CompilerParams(dimension_semantics=("parallel",)),
    )(*(a.reshape(rows, cols) for a in (w, g, m, v)))
    return tuple(o.reshape(shape) for o in outs)


ADD_STEPS = 4


def add_halves(geoms, arrs, received, c_idx, dtypes):
    n = len(arrs)

    def body(c_ref, *refs):
        del c_ref
        for a in range(n):
            refs[2 * n + a][...] = (refs[a][...] + refs[n + a][...]).astype(dtypes[a])

    own_specs, half_specs = [], []
    for g in geoms:
        if g.kind == "cols":
            rows, cols = g.half_shape[0] // ADD_STEPS, g.half_shape[1]
            own_specs.append(pl.BlockSpec((rows, cols), lambda i, c_ref: (ADD_STEPS * c_ref[0] + i, 0)))
            half_specs.append(pl.BlockSpec((rows, cols), lambda i, c_ref: (i, 0)))
        else:
            _, h, cols = g.half_shape
            own_specs.append(pl.BlockSpec((None, None, h, cols), lambda i, c_ref: (i, c_ref[0], 0, 0)))
            half_specs.append(pl.BlockSpec((None, h, cols), lambda i, c_ref: (i, 0, 0)))
    return pl.pallas_call(
        body, name="add_halves",
        grid_spec=pltpu.PrefetchScalarGridSpec(num_scalar_prefetch=1, grid=(ADD_STEPS,),
                                               in_specs=own_specs + half_specs, out_specs=half_specs),
        out_shape=[jax.ShapeDtypeStruct(g.half_shape, dt) for g, dt in zip(geoms, dtypes)],
        compiler_params=_params("parallel"),
    )(c_idx, *arrs, *received)


def add_parts(geoms, landed, finals, layer, c_idx):
    n = len(landed)

    def body(c_ref, *refs):
        del c_ref
        for a in range(n):
            p_ref = refs[a]
            parts = [p_ref[j].astype(F32) for j in range(N_CHIPS)]
            refs[2 * n + a][...] = ((parts[0] + parts[1]) + parts[2]) + parts[3]

    in_specs, out_specs = [], []
    for g in geoms:
        rows, cols = g.part_shape[0] // ADD_STEPS, g.part_shape[1]
        in_specs.append(pl.BlockSpec((N_CHIPS, rows, cols), lambda i, c_ref: (0, i, 0)))
        if g.everywhere:
            out_specs.append(pl.BlockSpec((None, rows, cols), lambda i, c_ref: (c_ref[1], ADD_STEPS * c_ref[0] + i, 0)))
        else:
            out_specs.append(pl.BlockSpec((None, rows, cols), lambda i, c_ref: (layer, ADD_STEPS * c_ref[0] + i, 0)))
    return pl.pallas_call(
        body, name="add_parts",
        grid_spec=pltpu.PrefetchScalarGridSpec(num_scalar_prefetch=1, grid=(ADD_STEPS,),
                                               in_specs=in_specs + [_ANY] * n, out_specs=out_specs),
        out_shape=[jax.ShapeDtypeStruct(f.shape, F32) for f in finals],
        input_output_aliases={1 + n + a: a for a in range(n)},
        compiler_params=_params("parallel"),
    )(c_idx, *landed, *finals)


def _shard_dims(k, n, axis):
    return (k // N_CHIPS, n) if axis == 0 else (k, n // N_CHIPS)


W_IN_STRIDE = 512
W_IN_WINDOW = 640


def _big_geoms():
    geoms = []
    for name, k, n, axis in BIG:
        if axis == 0:
            geoms.append(_Geom("rows", (N_CHIPS, 2, k // N_CHIPS // 2, n)))
        elif name == "w_in":
            geoms.append(_Geom("cols", (k, n), W_IN_STRIDE, W_IN_WINDOW))
        else:
            geoms.append(_Geom("cols", (k, n), n // N_CHIPS, n // N_CHIPS))
    return geoms


def _grad_views(gb, geoms):
    return [gb[name].reshape(g.shape) for (name, _, _, _), g in zip(BIG, geoms)]


def _round_up(v, m):
    return (v + m - 1) // m * m


def _prep_small(small):
    tril = jnp.tril(jnp.ones((CHUNK, CHUNK), bool))
    wm = jnp.where(tril, small["sgu_w"], 0.0).astype(BF16).reshape(DEPTH, 3, 2, CHUNK, CHUNK)
    head = jnp.arange(D_A) // HEAD_DIM
    grp = jnp.arange(D_C) // HEAD_DIM
    pw_rows = small["pool_w"].reshape(DEPTH, D_C, HEAD_DIM)
    pool_bd = jnp.where((grp[:, None] == grp[None, :])[None], jnp.tile(pw_rows, (1, 1, D_C // HEAD_DIM)), 0.0).astype(BF16)
    return dict(
        wcat=wm.transpose(0, 1, 3, 2, 4).reshape(DEPTH, 3, CHUNK, 2 * CHUNK),
        wcat_t=wm.transpose(0, 1, 2, 4, 3).reshape(DEPTH, 3, 2 * CHUNK, CHUNK),
        bmat=jnp.repeat(jnp.swapaxes(small["sgu_b"], 1, 2), HEAD_DIM, axis=2),
        avg=jnp.where(head[:, None] == head[None, :], 1.0 / HEAD_DIM, 0.0).astype(BF16),
        pool_bd=pool_bd, pool_bd_t=jnp.swapaxes(pool_bd, 1, 2),
        conv8=jnp.pad(small["conv_w"], ((0, 0), (0, 8 - 3), (0, 0))),
    )


def _row(a):
    return a.reshape(1, -1)


MIX_WEIGHTS = ("w_in", "w_out")
MLP_WEIGHTS = ("w_ff1", "w_ff2", "w_ple_gate", "w_ple_proj")
ALL_BIG = MIX_WEIGHTS + MLP_WEIGHTS
FFN_BWD_TILE = 512
PLE_BWD_TILE = 1024


def _fwd_layer(h, p, wl, small, prep, l, tm, comm_mix=None, comm_mlp=None, target=None):
    (proj, ycat, x1), got = mix_fwd(h, _row(small["norm_mix_g"][l]), wl["w_in"], wl["w_out"], prep["wcat"][l],
                                    prep["bmat"][l], _row(small["sgu_ln_g"][l]), _row(small["sgu_ln_b"][l]), prep["avg"],
                                    prep["conv8"][l], prep["pool_bd"][l], _row(small["pool_scale"][l]),
                                    tm=min(2 * tm, h.shape[0]), comm=comm_mix)
    if comm_mix is not None:
        wl = {**wl, **_weights_of(got, MLP_WEIGHTS)}
    if target is not None:
        a, x2, loss_blk, d_final_g, d = ffn_ple_loss_fwd(
            x1, (p, l), _row(small["norm_ff_g"][l]), wl["w_ff1"], wl["w_ff2"], _row(small["norm_ple_g"][l]),
            wl["w_ple_gate"], wl["w_ple_proj"], target, _row(small["final_g"]), tm=tm)
        return (h, proj, ycat, x1, a, x2), (loss_blk, d_final_g, d), [], wl
    (a, x2, x3), couts = ffn_ple_fwd(x1, (p, l), _row(small["norm_ff_g"][l]), wl["w_ff1"], wl["w_ff2"],
                                     _row(small["norm_ple_g"][l]), wl["w_ple_gate"], wl["w_ple_proj"], tm=tm,
                                     comm=comm_mlp)
    return (h, proj, ycat, x1, a, x2), x3, couts, wl


def _merge_comms(comms):
    comms = [cm for cm in comms if cm is not None]
    if len(comms) <= 1:
        return comms[0] if comms else None
    spans, ni, no, ns = [], 0, 0, 0
    for cm in comms:
        spans.append((ni, no, ns))
        ni, no, ns = ni + len(cm.ins), no + len(cm.out_shapes), ns + len(cm.sems)

    def copies(in_refs, out_refs, sem_refs):
        local, sends, recvs = [], [], []
        for cm, (i0, o0, s0) in zip(comms, spans):
            got = cm.copies(in_refs[i0:i0 + len(cm.ins)], out_refs[o0:o0 + len(cm.out_shapes)],
                            sem_refs[s0:s0 + len(cm.sems)])
            local, sends, recvs = local + got[0], sends + got[1], recvs + got[2]
        return local, sends, recvs

    aliases = {i0 + i: o0 + o for cm, (i0, o0, _) in zip(comms, spans) for i, o in cm.aliases.items()}
    assert all(cm.forwards is None for cm in comms)
    return _Comm(sum((cm.ins for cm in comms), []), sum((cm.out_shapes for cm in comms), []),
                 sum((cm.sems for cm in comms), []), copies, aliases)


def _split_results(results, comms):
    out, at = [], 0
    for cm in comms:
        if cm is None:
            out.append(None)
        else:
            out.append(results[at:at + len(cm.out_shapes)])
            at += len(cm.out_shapes)
    return out


class _Reduction:
    def __init__(self, layer, names, geoms, arrs, finals, c_arr, narrow=()):
        self.layer, self.names, self.geoms, self.arrs = layer, list(names), list(geoms), list(arrs)
        self.finals, self.c_arr = finals, c_arr
        self.dtypes = [BF16 if n in narrow else F32 for n in self.names]

    def comm_a(self):
        return _reduce_a_comm(self.geoms, self.arrs)

    def comm_b(self, received):
        return _reduce_b_comm(self.geoms, add_halves(self.geoms, self.arrs, received, self.c_arr, self.dtypes))

    def comm_c(self, landed):
        mine = add_parts(self.geoms, landed, [self.finals[n] for n in self.names], self.layer, self.c_arr)
        return _reduce_c_comm(self.geoms, mine, self.layer)

    def done(self, results):
        self.finals.update(zip(self.names, results))


class _Plan:
    def ple(self):
        return None

    def after_ple(self, results):
        pass

    def ffn(self):
        return None

    def after_ffn(self, results):
        pass

    def out_grad(self, gb):
        return None

    def after_out_grad(self, results):
        pass

    def before_mix(self, gb):
        pass

    def mix(self):
        return None

    def after_mix(self, results):
        pass


class _CarryPlan(_Plan):
    def __init__(self, above):
        self.above = above

    def ple(self):
        return self.above.comm_a()

    def after_ple(self, results):
        self.received = results

    def ffn(self):
        return self.above.comm_b(self.received)

    def after_ffn(self, results):
        self.landed = results

    def mix(self):
        return self.above.comm_c(self.landed)

    def after_mix(self, results):
        self.above.done(results)


class _LastPlan(_CarryPlan):
    def __init__(self, above, make):
        super().__init__(above)
        self.make = make

    def out_grad(self, gb):
        self.early = [self.make(MLP_WEIGHTS, gb)]
        return self.early[0].comm_a()

    def after_out_grad(self, results):
        self.early_received = [results]

    def before_mix(self, gb):
        self.early.append(self.make(("w_out",), gb))
        self.early_received.append(_run_comm(self.early[1].comm_a(), "reduce_a_early"))

    def mix(self):
        self.parts = [self.above.comm_c(self.landed)] + [r.comm_b(got) for r, got in zip(self.early, self.early_received)]
        return _merge_comms(self.parts)

    def after_mix(self, results):
        above_res, *self.early_landed = _split_results(results, self.parts)
        self.above.done(above_res)


def _bwd_layer(d, saved, p, wl, small, prep, l, tm, tk, plan=None):
    plan = plan or _Plan()
    xin, proj, ycat, x1, a, x2 = saved
    (dx2, dg_ple, dw_gate, dw_proj), res = ple_bwd(d, x2, (p, l), _row(small["norm_ple_g"][l]), wl["w_ple_gate"],
                                                   wl["w_ple_proj"], tm=min(PLE_BWD_TILE, d.shape[0]), comm=plan.ple())
    plan.after_ple(res)
    gb = {"w_ple_gate": dw_gate, "w_ple_proj": dw_proj}
    (dx1, h2, da, dg_ff), res = ffn_bwd(dx2, x1, a, _row(small["norm_ff_g"][l]), wl["w_ff1"], wl["w_ff2"],
                                        tm=FFN_BWD_TILE if tm >= FFN_BWD_TILE else tm, comm=plan.ffn())
    plan.after_ffn(res)
    gb["w_ff2"] = wgrad(a, dx2, tk=tk, relu_sq=True)
    gb["w_ff1"] = wgrad(h2, da, tk=min(2 * tk, h2.shape[0]))
    comm = plan.out_grad(gb)
    if comm is None:
        gb["w_out"] = wgrad(ycat, dx1, tk=tk)
    else:
        gb["w_out"], res = wgrad(ycat, dx1, tk=tk, comm=comm)
        plan.after_out_grad(res)
    plan.before_mix(gb)
    (dprev, h1, dproj, dg_mix, dws, dbm, dlng, dlnb, dcw, dpw, dps), res = mix_bwd(
        dx1, xin, proj, _row(small["norm_mix_g"][l]), wl["w_in"], wl["w_out"], prep["wcat"][l], prep["wcat_t"][l],
        prep["bmat"][l], _row(small["sgu_ln_g"][l]), _row(small["sgu_ln_b"][l]), prep["avg"], prep["conv8"][l],
        prep["pool_bd"][l], prep["pool_bd_t"][l], _row(small["pool_scale"][l]), tm=tm, comm=plan.mix())
    plan.after_mix(res)
    gb["w_in"] = wgrad(h1, dproj, tk=tk)
    gs = {
        "norm_ple_g": dg_ple[0], "norm_ff_g": dg_ff[0], "norm_mix_g": dg_mix[0],
        "sgu_w": dws.reshape(2 * 3, CHUNK, CHUNK), "sgu_b": dbm[:, ::HEAD_DIM].T,
        "sgu_ln_g": dlng[0], "sgu_ln_b": dlnb[0], "conv_w": dcw[0:3], "pool_scale": dps[0],
        "pool_w": jnp.stack([dpw[g * HEAD_DIM:(g + 1) * HEAD_DIM, g * HEAD_DIM:(g + 1) * HEAD_DIM]
                             for g in range(D_C // HEAD_DIM)]),
    }
    return dprev, gb, gs


def _local_step(x, p, target, full, small, *, tm, tk):
    prep = _prep_small(small)
    p = p[:, None]
    saved, h = [], x
    for l in range(DEPTH):
        wl = {name: full[name][l] for name in full}
        s, h, _, _ = _fwd_layer(h, p, wl, small, prep, l, tm, target=target if l == DEPTH - 1 else None)
        saved.append(s)
    loss_blk, d_final_g, d = h
    gbig, gsm = [None] * DEPTH, [None] * DEPTH
    for l in reversed(range(DEPTH)):
        wl = {name: full[name][l] for name in full}
        d, gbig[l], gsm[l] = _bwd_layer(d, saved[l], p, wl, small, prep, l, tm, tk)
    big = {name: jnp.stack([gbig[l][name] for l in range(DEPTH)]) for name in gbig[0]}
    sm = {name: jnp.stack([gsm[l][name] for l in range(DEPTH)]) for name in gsm[0]}
    sm["final_g"] = d_final_g[0]
    return loss_blk[0, 0], d, big, sm


def _weights_of(gathered, names):
    wl = dict(zip([b[0] for b in BIG if b[0] in names], gathered))
    if "w_in" in wl:
        wl["w_in"] = wl["w_in"].transpose(1, 0, 2).reshape(D_MODEL, D_IN)
    return wl


def kernel(x, p, norm_mix_g, w_in, sgu_w, sgu_b, sgu_ln_g, sgu_ln_b, conv_w, pool_w, pool_scale, w_out, norm_ff_g, w_ff1, w_ff2, norm_ple_g, w_ple_gate, w_ple_proj, final_g, loss_target, m_norm_mix_g, m_w_in, m_sgu_w, m_sgu_b, m_sgu_ln_g, m_sgu_ln_b, m_conv_w, m_pool_w, m_pool_scale, m_w_out, m_norm_ff_g, m_w_ff1, m_w_ff2, m_norm_ple_g, m_w_ple_gate, m_w_ple_proj, m_final_g, v_norm_mix_g, v_w_in, v_sgu_w, v_sgu_b, v_sgu_ln_g, v_sgu_ln_b, v_conv_w, v_pool_w, v_pool_scale, v_w_out, v_norm_ff_g, v_w_ff1, v_w_ff2, v_norm_ple_g, v_w_ple_gate, v_w_ple_proj, v_final_g):
    args = dict(locals())
    w = {name: args[name] for name in WEIGHTS}
    m = {name: args["m_" + name] for name in WEIGHTS}
    v = {name: args["v_" + name] for name in WEIGHTS}
    t = x.shape[1]
    tm = min(512, t)
    tk = min(2048, t)
    x_idx, y_idx, c_idx = _place()
    chip = 2 * x_idx + y_idx
    c_arr = jnp.stack([c_idx, chip]).astype(jnp.int32)
    xs, target = x[0], loss_target[0]

    shards = {name: w[name].astype(BF16) for name, _, _, _ in BIG}
    conv_rows = _round_up(CONV_SHARD, 8 * 128) // 128
    conv_flat = jnp.pad(w["conv_w"].reshape(-1), (0, conv_rows * 128 - CONV_SHARD)).reshape(conv_rows, 128)
    first = _run_comm(_gather_halved_comm(shards, 0, MIX_WEIGHTS, conv_flat), "gather_first")
    conv_full = (first[len(MIX_WEIGHTS)].reshape(N_CHIPS, -1)[:, :CONV_SHARD]
                 .reshape(N_CHIPS, DEPTH, 3, D_B // N_CHIPS).transpose(1, 2, 0, 3).reshape(DEPTH, 3, D_B))
    small = {name: w[name] for name in SMALL}
    small["conv_w"] = conv_full
    prep = _prep_small(small)

    wl = [None] * DEPTH
    wl[0] = _weights_of(first, MIX_WEIGHTS)
    saved, h = [], xs
    for l in range(DEPTH):
        comm_mix = None
        if l == 0:
            comm_mix = _gather_halved_comm(shards, 0, MLP_WEIGHTS)
        comm_mlp = _gather_comm(shards, l + 1, ALL_BIG) if l + 1 < DEPTH else None
        s, h, got, wl[l] = _fwd_layer(h, p, wl[l], small, prep, l, tm, comm_mix, comm_mlp,
                                      target=target if l == DEPTH - 1 else None)
        saved.append(s)
        if comm_mlp is not None:
            wl[l + 1] = _weights_of(got, ALL_BIG)
    loss_blk, d_final_g, d = h

    geoms = dict(zip([b[0] for b in BIG], _big_geoms()))
    finals = {name: jnp.zeros((DEPTH,) + g.final_shape, F32) for name, g in geoms.items()}

    def reduction(layer, names, gb, narrow=()):
        return _Reduction(layer, names, [geoms[n] for n in names], [gb[n].reshape(geoms[n].shape) for n in names],
                          finals, c_arr, narrow)

    gsm = [None] * DEPTH
    above = None
    for l in reversed(range(DEPTH)):
        if above is None:
            plan = _Plan()
        elif l > 0:
            plan = _CarryPlan(above)
        else:
            plan = _LastPlan(above, lambda names, gb: reduction(0, names, gb))
        d, gb, gsm[l] = _bwd_layer(d, saved[l], p, wl[l], small, prep, l, tm, tk, plan)
        if l > 0:
            above = reduction(l, ALL_BIG, gb)

    sm = {name: jnp.stack([gsm[i][name] for i in range(DEPTH)]) for name in gsm[0]}
    sm["final_g"] = d_final_g[0]
    sizes = [sm[name].size for name in SMALL]
    small_rows = _round_up(-(-sum(sizes) // (2 * N_CHIPS * LANES)), 8 * ADD_STEPS)
    small_flat = jnp.pad(jnp.concatenate([sm[name].reshape(-1) for name in SMALL]),
                         (0, 2 * N_CHIPS * small_rows * LANES - sum(sizes)))
    geoms["small"] = _Geom("rows", (N_CHIPS, 2, small_rows, LANES), everywhere=True)
    finals["small"] = jnp.zeros((N_CHIPS,) + geoms["small"].final_shape, F32)
    late = reduction(0, ("w_in", "small"), {**gb, "small": small_flat}, narrow=("w_in",))
    late_landed = _run_comm(late.comm_b(_run_comm(late.comm_a(), "reduce_a_late")), "reduce_b_late")
    finishing = plan.early + [late]
    last = [r.comm_c(got) for r, got in zip(finishing, plan.early_landed + [late_landed])]
    for r, res in zip(finishing, _split_results(_run_comm(_merge_comms(last), "reduce_c_last"), last)):
        r.done(res)

    grads = {name: finals[name] for name, _, _, _ in BIG}
    grads["w_in"] = lax.dynamic_slice_in_dim(grads["w_in"], chip * (D_IN // N_CHIPS - W_IN_STRIDE), D_IN // N_CHIPS, axis=2)
    small_red = finals["small"].reshape(-1)
    off = 0
    for name, size in zip(SMALL, sizes):
        grads[name] = small_red[off:off + size].reshape(sm[name].shape)
        off += size
    grads["conv_w"] = lax.dynamic_slice_in_dim(grads["conv_w"], chip * (D_B // N_CHIPS), D_B // N_CHIPS, axis=2)

    loss = lax.psum(loss_blk[0, 0], ("x", "y", "c"))
    delta, new_m, new_v = {}, {}, {}
    for name in WEIGHTS:
        delta[name], new_m[name], new_v[name] = adamw(w[name], grads[name], m[name], v[name])
    return (loss, d[None], *[grads[n] for n in WEIGHTS], *[delta[n] for n in WEIGHTS],
            *[new_m[n] for n in WEIGHTS], *[new_v[n] for n in WEIGHTS])
```

```python
import math

import jax
import jax.numpy as jnp
from jax import lax
from jax.experimental import pallas as pl
from jax.experimental.pallas import tpu as pltpu

F32 = jnp.float32
BF16 = jnp.bfloat16

D_MODEL = 1024
DEPTH = 4
D_PLE = 256
D_FF = 4096
HEAD_DIM = 64
D_A = 384
D_B = 384
D_C = 256
D_IN = 2176
CHUNK = 128
HALO = 16
RMS_EPS = 1e-6
LN_EPS = 1e-5
N_CHIPS = 4
LANES = 1024

ADAM_LR = 0.001
ADAM_B1 = 0.9
ADAM_B2 = 0.999
ADAM_EPS = 1e-08
ADAM_WD = 0.01
ADAM_STEP = 10

VMEM_LIMIT_BYTES = 60 * 1024 * 1024

_RSQRT2 = 0.7071067811865476
_INV_SQRT_2PI = 0.3989422804014327

BIG = (
    ("w_in", D_MODEL, D_IN, 1),
    ("w_out", D_MODEL, D_MODEL, 0),
    ("w_ff1", D_MODEL, D_FF, 1),
    ("w_ff2", D_FF, D_MODEL, 0),
    ("w_ple_gate", D_MODEL, D_MODEL, 0),
    ("w_ple_proj", D_PLE, D_MODEL, 1),
)
SMALL = ("norm_mix_g", "sgu_w", "sgu_b", "sgu_ln_g", "sgu_ln_b", "conv_w", "pool_w", "pool_scale",
         "norm_ff_g", "norm_ple_g", "final_g")
WEIGHTS = ("norm_mix_g", "w_in", "sgu_w", "sgu_b", "sgu_ln_g", "sgu_ln_b", "conv_w", "pool_w", "pool_scale",
           "w_out", "norm_ff_g", "w_ff1", "w_ff2", "norm_ple_g", "w_ple_gate", "w_ple_proj", "final_g")
CONV_SHARD = DEPTH * 3 * (D_B // N_CHIPS)


def _dot(a, b):
    return jnp.dot(a, b, preferred_element_type=F32)


def _dot_nt(a, b):
    return lax.dot_general(a, b, (((1,), (1,)), ((), ())), preferred_element_type=F32)


def _dot_tn(a, b):
    return lax.dot_general(a, b, (((0,), (0,)), ((), ())), preferred_element_type=F32)


def _const_spec(shape):
    nd = len(shape)
    return pl.BlockSpec(shape, lambda i: (0,) * nd, pipeline_mode=pl.Buffered(1))


def _acc_spec(shape):
    nd = len(shape)
    return pl.BlockSpec(shape, lambda i: (0,) * nd)


def _layer_rows(layer, tm):
    return pl.BlockSpec((None, None, tm, D_PLE), lambda i: (layer, 0, i, 0))


def _params(*sem):
    return pltpu.CompilerParams(dimension_semantics=sem, vmem_limit_bytes=VMEM_LIMIT_BYTES)


def _rms_bwd(dh, n, rs, g):
    dn = dh * g
    return rs * (dn - n * jnp.mean(dn * n, axis=-1, keepdims=True))


def _gelu(x):
    return x * (0.5 * (1.0 + lax.erf(x * _RSQRT2)))


def _gelu_and_grad(x):
    cdf = 0.5 * (1.0 + lax.erf(x * _RSQRT2))
    return x * cdf, cdf + x * (jnp.exp(-0.5 * x * x) * _INV_SQRT_2PI)


def _group_mean(v, avg):
    vb = v.astype(BF16)
    split = 2 * CHUNK
    return jnp.concatenate([_dot(vb[:, :split], avg[:split, :split]), _dot(vb[:, split:], avg[split:, split:])], axis=1)


def _group_mean_split(v, avg):
    hi = v.astype(BF16)
    lo = (v - hi.astype(F32)).astype(BF16)
    return _dot(hi, avg) + _dot(lo, avg)


def _lane_lt(shape, bound):
    return lax.broadcasted_iota(jnp.int32, shape, 1) < bound


def _sgu_mix(vnb2, wcat_j, lo_mask):
    zero = jnp.zeros_like(vnb2)
    stacked = jnp.concatenate([jnp.where(lo_mask, vnb2, zero), jnp.where(lo_mask, zero, vnb2)], axis=0)
    return _dot(wcat_j, stacked)


def _pool_means(ext, tile_rows, first_pos):
    s2 = ext + pltpu.roll(ext, 1, 0)
    s4 = s2 + pltpu.roll(s2, 2, 0)
    s8 = s4 + pltpu.roll(s4, 4, 0)
    s16 = s8 + pltpu.roll(s8, 8, 0)
    pos = (first_pos + lax.broadcasted_iota(jnp.int32, (tile_rows, 1), 0) + 1).astype(F32)
    lane = lax.broadcasted_iota(jnp.int32, (tile_rows, D_C), 1)
    sums = jnp.where(lane < 64, s2[HALO:], jnp.where(lane < 128, s4[HALO:], jnp.where(lane < 192, s8[HALO:], s16[HALO:])))
    win = jnp.where(lane < 64, 2.0, jnp.where(lane < 128, 4.0, jnp.where(lane < 192, 8.0, 16.0)))
    inv = 1.0 / jnp.minimum(pos, win)
    return sums * inv, inv


MESH = pl.DeviceIdType.MESH
_ANY = pl.BlockSpec(memory_space=pl.ANY)


def _place():
    return lax.axis_index("x"), lax.axis_index("y"), lax.axis_index("c")


def _chip_peers(x, y):
    return [(1 - x, y), (x, 1 - y), (1 - x, 1 - y)]


class _Comm:
    def __init__(self, ins, out_shapes, sems, copies, aliases=None, forwards=None):
        self.ins, self.out_shapes, self.sems, self.copies = list(ins), list(out_shapes), list(sems), copies
        self.aliases = dict(aliases or {})
        self.forwards = forwards

    def start(self, in_refs, out_refs, sem_refs):
        local, sends, _ = self.copies(in_refs, out_refs, sem_refs)
        for cp in local + sends:
            cp.start()

    def wait(self, in_refs, out_refs, sem_refs):
        local, sends, recvs = self.copies(in_refs, out_refs, sem_refs)
        for cp in recvs:
            cp.wait_recv()
        passed, passed_in = self.forwards(in_refs, out_refs, sem_refs) if self.forwards else ([], [])
        for cp in passed:
            cp.start()
        for cp in sends:
            cp.wait_send()
        for cp in local:
            cp.wait()
        for cp in passed_in:
            cp.wait_recv()
        for cp in passed:
            cp.wait_send()


def _remote(src, dst, send_sem, recv_sem, device):
    return pltpu.make_async_remote_copy(src_ref=src, dst_ref=dst, send_sem=send_sem, recv_sem=recv_sem,
                                        device_id=device, device_id_type=MESH)


def _gather_comm(shards, layer, names, conv=None):
    mats = [b for b in BIG if b[0] in names]
    ins = [shards[name] for name, _, _, _ in mats] + ([conv] if conv is not None else [])
    out_shapes = []
    for name, k, n, axis in mats:
        shape = (N_CHIPS, k, n // N_CHIPS) if name == "w_in" else (k, n)
        out_shapes.append(jax.ShapeDtypeStruct(shape, BF16))
    if conv is not None:
        out_shapes.append(jax.ShapeDtypeStruct((N_CHIPS,) + conv.shape, conv.dtype))
    n_arr = len(ins)

    def block(a, out_ref, chip):
        if a == len(mats) or mats[a][0] == "w_in":
            return out_ref.at[chip]
        _, k, n, axis = mats[a]
        if axis == 0:
            return out_ref.at[pl.ds(chip * (k // N_CHIPS), k // N_CHIPS), :]
        return out_ref.at[:, pl.ds(chip * (n // N_CHIPS), n // N_CHIPS)]

    def copies(in_refs, out_refs, sem_refs):
        send_sems, recv_sems = sem_refs
        x, y, c = _place()
        me = 2 * x + y
        sends, recvs = [], []
        for a in range(n_arr):
            src = in_refs[a].at[layer] if a < len(mats) else in_refs[a]
            own = _remote(src, block(a, out_refs[a], me), send_sems.at[a, 3], recv_sems.at[a, 3], (x, y, 1 - c))
            sends.append(own)
            recvs.append(own)
            for j, (px, py) in enumerate(_chip_peers(x, y)):
                sends.append(_remote(src, block(a, out_refs[a], me), send_sems.at[a, j], recv_sems.at[a, j], (px, py, c)))
                recvs.append(_remote(src, block(a, out_refs[a], 2 * px + py), send_sems.at[a, j], recv_sems.at[a, j],
                                     (px, py, c)))
        return [], sends, recvs

    sems = [pltpu.SemaphoreType.DMA((n_arr, 4)), pltpu.SemaphoreType.DMA((n_arr, 4))]
    return _Comm(ins, out_shapes, sems, copies)


def _gather_halved_comm(shards, layer, names, conv=None):
    mats = [b for b in BIG if b[0] in names]
    ins = [shards[name] for name, _, _, _ in mats] + ([conv] if conv is not None else [])
    out_shapes = []
    for name, k, n, axis in mats:
        out_shapes.append(jax.ShapeDtypeStruct((N_CHIPS, k, n // N_CHIPS) if name == "w_in" else (k, n), BF16))
    if conv is not None:
        out_shapes.append(jax.ShapeDtypeStruct((N_CHIPS,) + conv.shape, conv.dtype))
    n_arr = len(ins)

    def whole(a, ref, chip):
        if a == len(mats) or mats[a][0] == "w_in":
            return ref.at[chip]
        _, k, n, axis = mats[a]
        if axis == 0:
            return ref.at[pl.ds(chip * (k // N_CHIPS), k // N_CHIPS), :]
        return ref.at[:, pl.ds(chip * (n // N_CHIPS), n // N_CHIPS)]

    def src_half(a, in_ref, core):
        if a == len(mats):
            rows = conv.shape[0] // 2
            return in_ref.at[pl.ds(core * rows, rows), :]
        rows = shards[mats[a][0]].shape[1] // 2
        return in_ref.at[layer, pl.ds(core * rows, rows), :]

    def half(a, out_ref, chip, core):
        if a == len(mats):
            rows = conv.shape[0] // 2
            return out_ref.at[chip, pl.ds(core * rows, rows), :]
        name, k, n, axis = mats[a]
        if name == "w_in":
            return out_ref.at[chip, pl.ds(core * (k // 2), k // 2), :]
        if axis == 0:
            rows = k // N_CHIPS // 2
            return out_ref.at[pl.ds(chip * 2 * rows + core * rows, rows), :]
        return out_ref.at[pl.ds(core * (k // 2), k // 2), pl.ds(chip * (n // N_CHIPS), n // N_CHIPS)]

    def copies(in_refs, out_refs, sem_refs):
        send_sems, recv_sems, own_send, own_recv = sem_refs[0], sem_refs[1], sem_refs[4], sem_refs[5]
        x, y, c = _place()
        me = 2 * x + y
        sends, recvs = [], []
        for a in range(n_arr):
            own = in_refs[a].at[layer] if a < len(mats) else in_refs[a]
            cp = _remote(own, whole(a, out_refs[a], me), own_send.at[a], own_recv.at[a], (x, y, 1 - c))
            sends.append(cp)
            recvs.append(cp)
            for j, (px, py) in enumerate(_chip_peers(x, y)):
                sends.append(_remote(src_half(a, in_refs[a], c), half(a, out_refs[a], me, c), send_sems.at[a, j],
                                     recv_sems.at[a, j], (px, py, c)))
                recvs.append(_remote(src_half(a, in_refs[a], c), half(a, out_refs[a], 2 * px + py, c), send_sems.at[a, j],
                                     recv_sems.at[a, j], (px, py, c)))
        return [], sends, recvs

    def forwards(in_refs, out_refs, sem_refs):
        send_sems, recv_sems = sem_refs[2], sem_refs[3]
        x, y, c = _place()
        sends, recvs = [], []
        for a in range(n_arr):
            for j, (px, py) in enumerate(_chip_peers(x, y)):
                peer = 2 * px + py
                sends.append(_remote(half(a, out_refs[a], peer, c), half(a, out_refs[a], peer, c), send_sems.at[a, j],
                                     recv_sems.at[a, j], (x, y, 1 - c)))
                recvs.append(_remote(half(a, out_refs[a], peer, c), half(a, out_refs[a], peer, 1 - c), send_sems.at[a, j],
                                     recv_sems.at[a, j], (x, y, 1 - c)))
        return sends, recvs

    sems = [pltpu.SemaphoreType.DMA((n_arr, 3))] * 4 + [pltpu.SemaphoreType.DMA((n_arr,))] * 2
    return _Comm(ins, out_shapes, sems, copies, forwards=forwards)


class _Geom:
    def __init__(self, kind, shape, stride=None, width=None, everywhere=False):
        self.kind, self.shape, self.stride, self.width = kind, tuple(shape), stride, width
        self.everywhere = everywhere
        if kind == "cols":
            k, n = shape
            self.half_shape, self.part_shape, self.final_shape = (k // 2, n), (k // 2, width), (k, width)
        else:
            _, _, h, n = shape
            self.half_shape, self.part_shape, self.final_shape = (N_CHIPS, h, n), (h, n), (2 * h, n)

    def half(self, ref, core):
        if self.kind == "cols":
            return ref.at[pl.ds(core * self.half_shape[0], self.half_shape[0]), :]
        return ref.at[:, core]

    def part(self, ref, chip):
        if self.kind == "cols":
            return ref.at[:, pl.ds(chip * self.stride, self.width)]
        return ref.at[chip]

    def final_half(self, ref, layer, core):
        rows = self.part_shape[0]
        return ref.at[layer, pl.ds(core * rows, rows), :]


def _reduce_a_comm(geoms, arrs):
    n = len(arrs)

    def copies(in_refs, out_refs, sem_refs):
        x, y, c = _place()
        cps = [_remote(geoms[a].half(in_refs[a], 1 - c), out_refs[a], sem_refs[0].at[a], sem_refs[1].at[a], (x, y, 1 - c))
               for a in range(n)]
        return [], cps, cps

    return _Comm(arrs, [jax.ShapeDtypeStruct(g.half_shape, F32) for g in geoms],
                 [pltpu.SemaphoreType.DMA((n,)), pltpu.SemaphoreType.DMA((n,))], copies)


def _reduce_b_comm(geoms, halves):
    n = len(halves)

    def copies(in_refs, out_refs, sem_refs):
        send_sems, recv_sems, local_sems = sem_refs
        x, y, c = _place()
        me = 2 * x + y
        local, sends, recvs = [], [], []
        for a in range(n):
            g = geoms[a]
            local.append(pltpu.make_async_copy(g.part(in_refs[a], me), out_refs[a].at[me], local_sems.at[a]))
            for j, (px, py) in enumerate(_chip_peers(x, y)):
                peer = 2 * px + py
                sends.append(_remote(g.part(in_refs[a], peer), out_refs[a].at[me], send_sems.at[a, j], recv_sems.at[a, j],
                                     (px, py, c)))
                recvs.append(_remote(g.part(in_refs[a], me), out_refs[a].at[peer], send_sems.at[a, j], recv_sems.at[a, j],
                                     (px, py, c)))
        return local, sends, recvs

    sems = [pltpu.SemaphoreType.DMA((n, 3)), pltpu.SemaphoreType.DMA((n, 3)), pltpu.SemaphoreType.DMA((n,))]
    return _Comm(halves, [jax.ShapeDtypeStruct((N_CHIPS,) + g.part_shape, h.dtype) for g, h in zip(geoms, halves)], sems,
                 copies)


def _reduce_c_comm(geoms, finals, layer):
    n = len(finals)

    def copies(in_refs, out_refs, sem_refs):
        send_sems, recv_sems = sem_refs
        x, y, c = _place()
        me = 2 * x + y
        sends, recvs = [], []
        for a in range(n):
            g = geoms[a]
            at = me if g.everywhere else layer
            mine = g.final_half(in_refs[a], at, c)
            sends.append(_remote(mine, g.final_half(out_refs[a], at, c), send_sems.at[a, 0], recv_sems.at[a, 0],
                                 (x, y, 1 - c)))
            recvs.append(_remote(mine, g.final_half(out_refs[a], at, 1 - c), send_sems.at[a, 0], recv_sems.at[a, 0],
                                 (x, y, 1 - c)))
            if not g.everywhere:
                continue
            for j, (px, py) in enumerate(_chip_peers(x, y)):
                for core, slot in ((c, 1 + 2 * j), (1 - c, 2 + 2 * j)):
                    sends.append(_remote(mine, g.final_half(out_refs[a], me, c), send_sems.at[a, slot],
                                         recv_sems.at[a, slot], (px, py, core)))
                    recvs.append(_remote(mine, g.final_half(out_refs[a], 2 * px + py, core), send_sems.at[a, slot],
                                         recv_sems.at[a, slot], (px, py, core)))
        return [], sends, recvs

    sems = [pltpu.SemaphoreType.DMA((n, 7)), pltpu.SemaphoreType.DMA((n, 7))]
    return _Comm(finals, [jax.ShapeDtypeStruct(f.shape, f.dtype) for f in finals], sems, copies,
                 aliases={a: a for a in range(n)})


def _run_comm(comm, name):
    def body(*refs):
        ni, no = len(comm.ins), len(comm.out_shapes)
        in_refs, out_refs, sem_refs = refs[:ni], refs[ni:ni + no], refs[ni + no:]
        comm.start(in_refs, out_refs, sem_refs)
        comm.wait(in_refs, out_refs, sem_refs)

    return pl.pallas_call(
        body, name=name, in_specs=[_ANY] * len(comm.ins), out_specs=[_ANY] * len(comm.out_shapes),
        out_shape=comm.out_shapes, scratch_shapes=comm.sems, input_output_aliases=comm.aliases,
        compiler_params=pltpu.CompilerParams(has_side_effects=True),
    )(*comm.ins)


def _tile_call(body, name, nt, in_specs, out_specs, out_shape, scratch, args, comm):
    if comm is None:
        outs = pl.pallas_call(body, name=name, grid=(nt,), in_specs=in_specs, out_specs=out_specs, out_shape=out_shape,
                              scratch_shapes=scratch, compiler_params=_params("arbitrary"))(*args)
        return outs, []
    n_in, n_out, n_scr = len(in_specs), len(out_specs), len(scratch)
    ci, co = len(comm.ins), len(comm.out_shapes)

    def hosted(*refs):
        in_refs = refs[:n_in]
        cin = refs[n_in:n_in + ci]
        out_refs = refs[n_in + ci:n_in + ci + n_out]
        cout = refs[n_in + ci + n_out:n_in + ci + n_out + co]
        scr = refs[n_in + ci + n_out + co:n_in + ci + n_out + co + n_scr]
        sems = refs[n_in + ci + n_out + co + n_scr:]
        i = pl.program_id(0)

        @pl.when(i == 0)
        def _():
            comm.start(cin, cout, sems)

        body(*in_refs, *out_refs, *scr)

        @pl.when(i == nt - 1)
        def _():
            comm.wait(cin, cout, sems)

    outs = pl.pallas_call(
        hosted, name=name + "_comm", grid=(nt,),
        in_specs=list(in_specs) + [_ANY] * ci, out_specs=list(out_specs) + [_ANY] * co,
        out_shape=list(out_shape) + comm.out_shapes, scratch_shapes=list(scratch) + comm.sems,
        input_output_aliases={n_in + i: n_out + o for i, o in comm.aliases.items()},
        compiler_params=_params("arbitrary"),
    )(*args, *comm.ins)
    return outs[:n_out], outs[n_out:]


def mix_fwd(x, g_mix, w_in, w_out, wcat, bmat, ln_g, ln_b, avg, conv_w, pool_bd, pool_scale, *, tm, comm=None):
    t = x.shape[0]
    nt = t // tm

    def body(x_ref, g_ref, win_ref, wout_ref, wcat_ref, bmat_ref, lng_ref, lnb_ref, avg_ref, cw_ref, pw_ref, ps_ref,
             proj_ref, ycat_ref, x1_ref, hbuf, zbuf):
        i = pl.program_id(0)

        @pl.when(i == 0)
        def _():
            hbuf[0:HALO, :] = jnp.zeros((HALO, D_B), F32)
            zbuf[0:HALO, :] = jnp.zeros((HALO, D_C), F32)

        xv = x_ref[...]
        n = xv * lax.rsqrt(jnp.mean(xv * xv, axis=-1, keepdims=True) + RMS_EPS)
        h1 = (n * g_ref[...]).astype(BF16)
        proj_ref[...] = _dot(h1, win_ref[...])

        lo_mask = _lane_lt((CHUNK, CHUNK), HEAD_DIM)
        avg = avg_ref[...]
        gu = _gelu(proj_ref[:, 0:D_A])
        gv = _gelu(proj_ref[:, D_A:2 * D_A])
        dv = gv - _group_mean(gv, avg)
        var = _group_mean(dv * dv, avg)
        vnb = (dv * lax.rsqrt(var + LN_EPS) * lng_ref[...] + lnb_ref[...]).astype(BF16)
        for c in range(tm // CHUNK):
            rows = slice(c * CHUNK, (c + 1) * CHUNK)
            for j in range(3):
                cols = slice(j * CHUNK, (j + 1) * CHUNK)
                mixed = _sgu_mix(vnb[rows, cols], wcat_ref[j], lo_mask) + bmat_ref[:, cols]
                ycat_ref[rows, cols] = (gu[rows, cols] * mixed).astype(BF16)

        o = 2 * D_A
        hcur = proj_ref[:, o + 2 * D_B:o + 3 * D_B] * proj_ref[:, o:o + D_B]
        hbuf[HALO:HALO + tm, :] = hcur
        y = (cw_ref[2:3, :] * hcur + cw_ref[1:2, :] * hbuf[pl.ds(HALO - 1, tm), :]
             + cw_ref[0:1, :] * hbuf[pl.ds(HALO - 2, tm), :])
        ycat_ref[:, D_A:D_A + D_B] = (proj_ref[:, o + D_B:o + 2 * D_B] * y).astype(BF16)
        hbuf[0:HALO, :] = hbuf[tm:tm + HALO, :]

        zc = proj_ref[:, o + 3 * D_B:D_IN]
        zbuf[HALO:HALO + tm, :] = zc
        mean, _ = _pool_means(zbuf[...], tm, i * tm)
        pooled = (mean - zc).astype(BF16)
        ycat_ref[:, D_A + D_B:D_MODEL] = (_dot(pooled, pw_ref[...]) * ps_ref[...]).astype(BF16)
        zbuf[0:HALO, :] = zbuf[tm:tm + HALO, :]

        x1_ref[...] = xv + _dot(ycat_ref[...], wout_ref[...])

    row = lambda w: pl.BlockSpec((tm, w), lambda i: (i, 0))
    return _tile_call(
        body, "mix_fwd", nt,
        [row(D_MODEL), _const_spec((1, D_MODEL)), _const_spec((D_MODEL, D_IN)), _const_spec((D_MODEL, D_MODEL)),
         _const_spec((3, CHUNK, 2 * CHUNK)), _const_spec((CHUNK, D_A)), _const_spec((1, D_A)), _const_spec((1, D_A)),
         _const_spec((D_A, D_A)), _const_spec((8, D_B)), _const_spec((D_C, D_C)), _const_spec((1, D_C))],
        [row(D_IN), row(D_MODEL), row(D_MODEL)],
        [jax.ShapeDtypeStruct((t, D_IN), F32), jax.ShapeDtypeStruct((t, D_MODEL), BF16),
         jax.ShapeDtypeStruct((t, D_MODEL), F32)],
        [pltpu.VMEM((tm + HALO, D_B), F32), pltpu.VMEM((tm + HALO, D_C), F32)],
        (x, g_mix, w_in, w_out, wcat, bmat, ln_g, ln_b, avg, conv_w, pool_bd, pool_scale), comm)


def ffn_ple_fwd(x1, p, g_ff, w_ff1, w_ff2, g_ple, w_gate, w_proj, *, tm, comm=None):
    t = x1.shape[0]
    nt = t // tm
    nc = D_FF // D_MODEL

    def body(x1_ref, p_ref, gff_ref, w1_ref, w2_ref, gple_ref, wg_ref, wp_ref, a_ref, x2_ref, x3_ref):
        x1v = x1_ref[...]
        n2 = x1v * lax.rsqrt(jnp.mean(x1v * x1v, axis=-1, keepdims=True) + RMS_EPS)
        h2 = (n2 * gff_ref[...]).astype(BF16)
        acc = x1v
        for c in range(nc):
            cols = slice(c * D_MODEL, (c + 1) * D_MODEL)
            a = _dot(h2, w1_ref[:, cols])
            a_ref[:, cols] = a.astype(BF16)
            ra = jnp.maximum(a, 0.0)
            acc = acc + _dot((ra * ra).astype(BF16), w2_ref[cols, :])
        x2_ref[...] = acc
        n3 = acc * lax.rsqrt(jnp.mean(acc * acc, axis=-1, keepdims=True) + RMS_EPS)
        h3 = (n3 * gple_ref[...]).astype(BF16)
        gate = jax.nn.sigmoid(_dot(h3, wg_ref[...]))
        pp = _dot(p_ref[...].astype(BF16), wp_ref[...])
        x3_ref[...] = acc + pp * gate

    row = lambda w: pl.BlockSpec((tm, w), lambda i: (i, 0))
    return _tile_call(
        body, "ffn_ple_fwd", nt,
        [row(D_MODEL), _layer_rows(p[1], tm), _const_spec((1, D_MODEL)), _const_spec((D_MODEL, D_FF)),
         _const_spec((D_FF, D_MODEL)), _const_spec((1, D_MODEL)), _const_spec((D_MODEL, D_MODEL)),
         _const_spec((D_PLE, D_MODEL))],
        [row(D_FF), row(D_MODEL), row(D_MODEL)],
        [jax.ShapeDtypeStruct((t, D_FF), BF16), jax.ShapeDtypeStruct((t, D_MODEL), F32),
         jax.ShapeDtypeStruct((t, D_MODEL), F32)],
        [], (x1, p[0], g_ff, w_ff1, w_ff2, g_ple, w_gate, w_proj), comm)


def ffn_ple_loss_fwd(x1, p, g_ff, w_ff1, w_ff2, g_ple, w_gate, w_proj, target, g_final, *, tm):
    t = x1.shape[0]
    nt = t // tm
    nc = D_FF // D_MODEL

    def body(x1_ref, p_ref, gff_ref, w1_ref, w2_ref, gple_ref, wg_ref, wp_ref, t_ref, gfin_ref,
             a_ref, x2_ref, loss_ref, dg_ref, dx_ref, sq_acc):
        i = pl.program_id(0)

        @pl.when(i == 0)
        def _():
            sq_acc[...] = jnp.zeros_like(sq_acc)
            dg_ref[...] = jnp.zeros_like(dg_ref)

        x1v = x1_ref[...]
        n2 = x1v * lax.rsqrt(jnp.mean(x1v * x1v, axis=-1, keepdims=True) + RMS_EPS)
        h2 = (n2 * gff_ref[...]).astype(BF16)
        acc = x1v
        for c in range(nc):
            cols = slice(c * D_MODEL, (c + 1) * D_MODEL)
            a = _dot(h2, w1_ref[:, cols])
            a_ref[:, cols] = a.astype(BF16)
            ra = jnp.maximum(a, 0.0)
            acc = acc + _dot((ra * ra).astype(BF16), w2_ref[cols, :])
        x2_ref[...] = acc
        n3 = acc * lax.rsqrt(jnp.mean(acc * acc, axis=-1, keepdims=True) + RMS_EPS)
        h3 = (n3 * gple_ref[...]).astype(BF16)
        gate = jax.nn.sigmoid(_dot(h3, wg_ref[...]))
        pp = _dot(p_ref[...].astype(BF16), wp_ref[...])
        x3 = acc + pp * gate

        rs = lax.rsqrt(jnp.mean(x3 * x3, axis=-1, keepdims=True) + RMS_EPS)
        n = x3 * rs
        gv = gfin_ref[...]
        err = n * gv - t_ref[...]
        sq_acc[...] += jnp.sum(err * err, axis=0, keepdims=True)
        dy = err * (1.0 / D_MODEL)
        dg_ref[...] += jnp.sum(dy * n, axis=0, keepdims=True)
        dx_ref[...] = _rms_bwd(dy, n, rs, gv)

        @pl.when(i == nt - 1)
        def _():
            total = jnp.sum(sq_acc[...], axis=1, keepdims=True) * (0.5 / D_MODEL)
            loss_ref[...] = jnp.broadcast_to(total, loss_ref.shape)

    row = lambda w: pl.BlockSpec((tm, w), lambda i: (i, 0))
    outs, _ = _tile_call(
        body, "ffn_ple_loss_fwd", nt,
        [row(D_MODEL), _layer_rows(p[1], tm), _const_spec((1, D_MODEL)), _const_spec((D_MODEL, D_FF)),
         _const_spec((D_FF, D_MODEL)), _const_spec((1, D_MODEL)), _const_spec((D_MODEL, D_MODEL)),
         _const_spec((D_PLE, D_MODEL)), row(D_MODEL), _const_spec((1, D_MODEL))],
        [row(D_FF), row(D_MODEL), _acc_spec((8, 128)), _acc_spec((1, D_MODEL)), row(D_MODEL)],
        [jax.ShapeDtypeStruct((t, D_FF), BF16), jax.ShapeDtypeStruct((t, D_MODEL), F32),
         jax.ShapeDtypeStruct((8, 128), F32), jax.ShapeDtypeStruct((1, D_MODEL), F32),
         jax.ShapeDtypeStruct((t, D_MODEL), F32)],
        [pltpu.VMEM((1, D_MODEL), F32)], (x1, p[0], g_ff, w_ff1, w_ff2, g_ple, w_gate, w_proj, target, g_final), None)
    return outs


def ple_bwd(d, x2, p, g_ple, w_gate, w_proj, *, tm, comm=None):
    t = d.shape[0]
    nt = t // tm

    def body(d_ref, x2_ref, p_ref, g_ref, wg_ref, wp_ref, dx2_ref, dg_ref, dwg_ref, dwp_ref):
        i = pl.program_id(0)

        @pl.when(i == 0)
        def _():
            dg_ref[...] = jnp.zeros_like(dg_ref)
            dwg_ref[...] = jnp.zeros_like(dwg_ref)
            dwp_ref[...] = jnp.zeros_like(dwp_ref)

        dv = d_ref[...]
        x2v = x2_ref[...]
        rs = lax.rsqrt(jnp.mean(x2v * x2v, axis=-1, keepdims=True) + RMS_EPS)
        n3 = x2v * rs
        gv = g_ref[...]
        h3 = (n3 * gv).astype(BF16)
        gate = jax.nn.sigmoid(_dot(h3, wg_ref[...]))
        pb = p_ref[...].astype(BF16)
        pp = _dot(pb, wp_ref[...])
        dwp_ref[...] += _dot_tn(pb, (dv * gate).astype(BF16))
        dpre = (dv * pp * gate * (1.0 - gate)).astype(BF16)
        dwg_ref[...] += _dot_tn(h3, dpre)
        dh3 = _dot_nt(dpre, wg_ref[...])
        dg_ref[...] += jnp.sum(dh3 * n3, axis=0, keepdims=True)
        dx2_ref[...] = dv + _rms_bwd(dh3, n3, rs, gv)

    row = lambda w: pl.BlockSpec((tm, w), lambda i: (i, 0))
    return _tile_call(
        body, "ple_bwd", nt,
        [row(D_MODEL), row(D_MODEL), _layer_rows(p[1], tm), _const_spec((1, D_MODEL)), _const_spec((D_MODEL, D_MODEL)),
         _const_spec((D_PLE, D_MODEL))],
        [row(D_MODEL), _acc_spec((1, D_MODEL)), _acc_spec((D_MODEL, D_MODEL)), _acc_spec((D_PLE, D_MODEL))],
        [jax.ShapeDtypeStruct((t, D_MODEL), F32), jax.ShapeDtypeStruct((1, D_MODEL), F32),
         jax.ShapeDtypeStruct((D_MODEL, D_MODEL), F32), jax.ShapeDtypeStruct((D_PLE, D_MODEL), F32)],
        [], (d, x2, p[0], g_ple, w_gate, w_proj), comm)


def ffn_bwd(dx2, x1, a, g_ff, w_ff1, w_ff2, *, tm, comm=None):
    t = dx2.shape[0]
    nt = t // tm
    nc = D_FF // D_MODEL

    def body(dx2_ref, x1_ref, a_ref, g_ref, w1_ref, w2_ref, dx1_ref, h2_ref, da_ref, dg_ref):
        i = pl.program_id(0)

        @pl.when(i == 0)
        def _():
            dg_ref[...] = jnp.zeros_like(dg_ref)

        dv = dx2_ref[...]
        x1v = x1_ref[...]
        rs = lax.rsqrt(jnp.mean(x1v * x1v, axis=-1, keepdims=True) + RMS_EPS)
        n2 = x1v * rs
        gv = g_ref[...]
        h2_ref[...] = (n2 * gv).astype(BF16)
        dvb = dv.astype(BF16)
        dh2 = jnp.zeros((tm, D_MODEL), F32)
        for c in range(nc):
            cols = slice(c * D_MODEL, (c + 1) * D_MODEL)
            ra = jnp.maximum(a_ref[:, cols].astype(F32), 0.0)
            da = (_dot_nt(dvb, w2_ref[cols, :]) * (2.0 * ra)).astype(BF16)
            da_ref[:, cols] = da
            dh2 = dh2 + _dot_nt(da, w1_ref[:, cols])
        dg_ref[...] += jnp.sum(dh2 * n2, axis=0, keepdims=True)
        dx1_ref[...] = dv + _rms_bwd(dh2, n2, rs, gv)

    row = lambda w: pl.BlockSpec((tm, w), lambda i: (i, 0))
    return _tile_call(
        body, "ffn_bwd", nt,
        [row(D_MODEL), row(D_MODEL), row(D_FF), _const_spec((1, D_MODEL)), _const_spec((D_MODEL, D_FF)),
         _const_spec((D_FF, D_MODEL))],
        [row(D_MODEL), row(D_MODEL), row(D_FF), _acc_spec((1, D_MODEL))],
        [jax.ShapeDtypeStruct((t, D_MODEL), F32), jax.ShapeDtypeStruct((t, D_MODEL), BF16),
         jax.ShapeDtypeStruct((t, D_FF), BF16), jax.ShapeDtypeStruct((1, D_MODEL), F32)],
        [], (dx2, x1, a, g_ff, w_ff1, w_ff2), comm)


def mix_bwd(dx1, x, proj, g_mix, w_in, w_out, wcat, wcat_t, bmat, ln_g, ln_b, avg, conv_w, pool_bd, pool_bd_t,
            pool_scale, *, tm, comm=None):
    t = dx1.shape[0]
    nt = t // tm
    prev_blocks = tm // HALO

    def body(dx1_ref, x_ref, proj_ref, prev_ref, g_ref, win_ref, wout_ref, wcat_ref, wcatt_ref, bmat_ref, lng_ref,
             lnb_ref, avg_ref, cw_ref, pw_ref, pwt_ref, ps_ref,
             dx_ref, h1_ref, dproj_ref, dg_ref, dws_ref, dbm_ref, dlng_ref, dlnb_ref, dcw_ref, dpw_ref, dps_ref,
             dyc, dpj, hbuf, zbuf, dybuf, qbuf):
        i = pl.program_id(0)
        ti = nt - 1 - i

        @pl.when(i == 0)
        def _():
            for ref in (dg_ref, dws_ref, dbm_ref, dlng_ref, dlnb_ref, dcw_ref, dpw_ref, dps_ref):
                ref[...] = jnp.zeros_like(ref)
            dybuf[tm:tm + HALO, :] = jnp.zeros((HALO, D_B), F32)
            qbuf[tm:tm + HALO, :] = jnp.zeros((HALO, D_C), F32)

        dx1v = dx1_ref[...]
        dyc[...] = _dot_nt(dx1v.astype(BF16), wout_ref[...])

        lo_mask = _lane_lt((CHUNK, CHUNK), HEAD_DIM)
        avg = avg_ref[...]
        lng = lng_ref[...]
        gu, dgu = _gelu_and_grad(proj_ref[:, 0:D_A])
        gv, dgv = _gelu_and_grad(proj_ref[:, D_A:2 * D_A])
        cen = gv - _group_mean(gv, avg)
        rstd = lax.rsqrt(_group_mean(cen * cen, avg) + LN_EPS)
        vhat = cen * rstd
        vnb = (vhat * lng + lnb_ref[...]).astype(BF16)
        dya = dyc[:, 0:D_A]
        dm = dya * gu
        dmb = dm.astype(BF16)
        dvn_rows = []
        for c in range(tm // CHUNK):
            rows = slice(c * CHUNK, (c + 1) * CHUNK)
            dbm_ref[...] += dm[rows]
            dvn_parts = []
            for j in range(3):
                cols = slice(j * CHUNK, (j + 1) * CHUNK)
                vnb2 = vnb[rows, cols]
                mixed = _sgu_mix(vnb2, wcat_ref[j], lo_mask) + bmat_ref[:, cols]
                dpj[rows, cols] = dya[rows, cols] * mixed * dgu[rows, cols]
                dmb2 = dmb[rows, cols]
                zero = jnp.zeros_like(dmb2)
                dm_st = jnp.concatenate([jnp.where(lo_mask, dmb2, zero), jnp.where(lo_mask, zero, dmb2)], axis=0)
                dws_ref[j] += _dot_nt(dm_st, vnb2)
                dvn_st = _dot(wcatt_ref[j], dmb2)
                dvn_parts.append(jnp.where(lo_mask, dvn_st[0:CHUNK], dvn_st[CHUNK:2 * CHUNK]))
            dvn_rows.append(jnp.concatenate(dvn_parts, axis=1))
        dvn = jnp.concatenate(dvn_rows, axis=0)
        dlng_ref[...] += jnp.sum(dvn * vhat, axis=0, keepdims=True)
        dlnb_ref[...] += jnp.sum(dvn, axis=0, keepdims=True)
        dvh = dvn * lng
        dgv_in = rstd * (dvh - _group_mean(dvh, avg) - vhat * _group_mean(dvh * vhat, avg))
        dpj[:, D_A:2 * D_A] = dgv_in * dgv

        o = 2 * D_A
        live = (ti > 0).astype(F32)
        zb = proj_ref[:, o:o + D_B]
        gb = proj_ref[:, o + D_B:o + 2 * D_B]
        gc = proj_ref[:, o + 2 * D_B:o + 3 * D_B]
        hcur = gc * zb
        hbuf[0:HALO, :] = prev_ref[:, o + 2 * D_B:o + 3 * D_B] * prev_ref[:, o:o + D_B] * live
        hbuf[HALO:HALO + tm, :] = hcur
        hm1 = hbuf[pl.ds(HALO - 1, tm), :]
        hm2 = hbuf[pl.ds(HALO - 2, tm), :]
        y = cw_ref[2:3, :] * hcur + cw_ref[1:2, :] * hm1 + cw_ref[0:1, :] * hm2
        dout = dyc[:, D_A:D_A + D_B]
        dpj[:, o + D_B:o + 2 * D_B] = dout * y
        dy = dout * gb
        dcw_ref[2:3, :] += jnp.sum(dy * hcur, axis=0, keepdims=True)
        dcw_ref[1:2, :] += jnp.sum(dy * hm1, axis=0, keepdims=True)
        dcw_ref[0:1, :] += jnp.sum(dy * hm2, axis=0, keepdims=True)
        dybuf[0:tm, :] = dy
        dh = (cw_ref[2:3, :] * dy + cw_ref[1:2, :] * dybuf[pl.ds(1, tm), :] + cw_ref[0:1, :] * dybuf[pl.ds(2, tm), :])
        dybuf[tm:tm + HALO, :] = dybuf[0:HALO, :]
        dpj[:, o:o + D_B] = dh * gc
        dpj[:, o + 2 * D_B:o + 3 * D_B] = dh * zb

        zc = proj_ref[:, o + 3 * D_B:D_IN]
        zbuf[0:HALO, :] = prev_ref[:, o + 3 * D_B:D_IN] * live
        zbuf[HALO:HALO + tm, :] = zc
        mean, inv = _pool_means(zbuf[...], tm, ti * tm)
        pooled = (mean - zc).astype(BF16)
        dyp = dyc[:, D_A + D_B:D_MODEL]
        ps = ps_ref[...]
        dps_ref[...] += jnp.sum(dyp * _dot(pooled, pw_ref[...]), axis=0, keepdims=True)
        dpw = (dyp * ps).astype(BF16)
        dpw_ref[...] += _dot_tn(pooled, dpw)
        dpooled = _dot(dpw, pwt_ref[...])
        qbuf[0:tm, :] = dpooled * inv
        q = qbuf[...]
        nrows = tm + HALO
        f2 = q + pltpu.roll(q, nrows - 1, 0)
        f4 = f2 + pltpu.roll(f2, nrows - 2, 0)
        f8 = f4 + pltpu.roll(f4, nrows - 4, 0)
        f16 = f8 + pltpu.roll(f8, nrows - 8, 0)
        lane = lax.broadcasted_iota(jnp.int32, (tm, D_C), 1)
        ahead = jnp.where(lane < 64, f2[0:tm], jnp.where(lane < 128, f4[0:tm], jnp.where(lane < 192, f8[0:tm], f16[0:tm])))
        dpj[:, o + 3 * D_B:D_IN] = ahead - dpooled
        qbuf[tm:tm + HALO, :] = qbuf[0:HALO, :]

        dprojb = dpj[...].astype(BF16)
        dproj_ref[...] = dprojb
        dh1 = _dot_nt(dprojb, win_ref[...])
        xv = x_ref[...]
        rs = lax.rsqrt(jnp.mean(xv * xv, axis=-1, keepdims=True) + RMS_EPS)
        n1 = xv * rs
        gv1 = g_ref[...]
        h1_ref[...] = (n1 * gv1).astype(BF16)
        dg_ref[...] += jnp.sum(dh1 * n1, axis=0, keepdims=True)
        dx_ref[...] = dx1v + _rms_bwd(dh1, n1, rs, gv1)

        @pl.when(i == nt - 1)
        def _():
            tril = (lax.broadcasted_iota(jnp.int32, (2 * CHUNK, CHUNK), 0) % CHUNK
                    >= lax.broadcasted_iota(jnp.int32, (2 * CHUNK, CHUNK), 1))
            for j in range(3):
                dws_ref[j] = jnp.where(tril, dws_ref[j], 0.0)
            dbm_ref[...] = _group_mean_split(dbm_ref[...], avg) * float(HEAD_DIM)

    rev = lambda w: pl.BlockSpec((tm, w), lambda i: (nt - 1 - i, 0))
    prev = pl.BlockSpec((HALO, D_IN), lambda i: (jnp.maximum((nt - 1 - i) * prev_blocks - 1, 0), 0))
    acc_shapes = [(1, D_MODEL), (3, 2 * CHUNK, CHUNK), (CHUNK, D_A), (1, D_A), (1, D_A), (8, D_B), (D_C, D_C), (1, D_C)]
    return _tile_call(
        body, "mix_bwd", nt,
        [rev(D_MODEL), rev(D_MODEL), rev(D_IN), prev, _const_spec((1, D_MODEL)), _const_spec((D_MODEL, D_IN)),
         _const_spec((D_MODEL, D_MODEL)), _const_spec((3, CHUNK, 2 * CHUNK)), _const_spec((3, 2 * CHUNK, CHUNK)),
         _const_spec((CHUNK, D_A)), _const_spec((1, D_A)), _const_spec((1, D_A)), _const_spec((D_A, D_A)),
         _const_spec((8, D_B)), _const_spec((D_C, D_C)), _const_spec((D_C, D_C)), _const_spec((1, D_C))],
        [rev(D_MODEL), rev(D_MODEL), rev(D_IN)] + [_acc_spec(s) for s in acc_shapes],
        [jax.ShapeDtypeStruct((t, D_MODEL), F32), jax.ShapeDtypeStruct((t, D_MODEL), BF16),
         jax.ShapeDtypeStruct((t, D_IN), BF16)] + [jax.ShapeDtypeStruct(s, F32) for s in acc_shapes],
        [pltpu.VMEM((tm, D_MODEL), F32), pltpu.VMEM((tm, D_IN), F32),
         pltpu.VMEM((tm + HALO, D_B), F32), pltpu.VMEM((tm + HALO, D_C), F32),
         pltpu.VMEM((tm + HALO, D_B), F32), pltpu.VMEM((tm + HALO, D_C), F32)],
        (dx1, x, proj, proj, g_mix, w_in, w_out, wcat, wcat_t, bmat, ln_g, ln_b, avg, conv_w, pool_bd, pool_bd_t,
         pool_scale), comm)


def wgrad(a, b, *, tk, a_layer=None, relu_sq=False, comm=None):
    t, m = a.shape[-2:]
    n = b.shape[1]
    bm = min(m, 1024)
    bn = 1024 if n % 1024 == 0 else n
    nk = t // tk
    if a_layer is None:
        a_spec = pl.BlockSpec((tk, bm), lambda i, j, k: (k, i))
    else:
        a_spec = pl.BlockSpec((None, None, tk, bm), lambda i, j, k: (a_layer, 0, k, i))

    def body(a_ref, b_ref, o_ref):
        k = pl.program_id(2)

        @pl.when(k == 0)
        def _():
            o_ref[...] = jnp.zeros_like(o_ref)

        av = a_ref[...]
        if relu_sq:
            ra = jnp.maximum(av.astype(F32), 0.0)
            av = ra * ra
        o_ref[...] += _dot_tn(av.astype(BF16), b_ref[...].astype(BF16))

    name = f"wgrad_{m}x{n}" + ("_relu_sq" if relu_sq else "")
    grid = (m // bm, n // bn, nk)
    specs = [a_spec, pl.BlockSpec((tk, bn), lambda i, j, k: (k, j))]
    out_spec = pl.BlockSpec((bm, bn), lambda i, j, k: (i, j))
    if comm is None:
        return pl.pallas_call(body, name=name, grid=grid, in_specs=specs, out_specs=out_spec,
                              out_shape=jax.ShapeDtypeStruct((m, n), F32),
                              compiler_params=_params("parallel", "parallel", "arbitrary"))(a, b)
    ci, co = len(comm.ins), len(comm.out_shapes)

    def hosted(*refs):
        cin, cout, sems = refs[2:2 + ci], refs[3 + ci:3 + ci + co], refs[3 + ci + co:]
        step = (pl.program_id(0) * grid[1] + pl.program_id(1)) * grid[2] + pl.program_id(2)

        @pl.when(step == 0)
        def _():
            comm.start(cin, cout, sems)

        body(refs[0], refs[1], refs[2 + ci])

        @pl.when(step == grid[0] * grid[1] * grid[2] - 1)
        def _():
            comm.wait(cin, cout, sems)

    outs = pl.pallas_call(
        hosted, name=name + "_comm", grid=grid, in_specs=specs + [_ANY] * ci, out_specs=[out_spec] + [_ANY] * co,
        out_shape=[jax.ShapeDtypeStruct((m, n), F32)] + comm.out_shapes, scratch_shapes=comm.sems,
        input_output_aliases={2 + i: 1 + o for i, o in comm.aliases.items()},
        compiler_params=_params("arbitrary", "arbitrary", "arbitrary"),
    )(a, b, *comm.ins)
    return outs[0], outs[1:]


def _row_block(rows, cols, target_bytes):
    target = max(8, target_bytes // (4 * cols))
    if rows <= target:
        return rows
    best = None
    for br in range(8, target + 1, 8):
        if rows % br == 0:
            best = br
    return best if best is not None else rows


def adamw(w, g, m, v):
    shape = w.shape
    cols = shape[-1]
    rows = math.prod(shape[:-1]) if len(shape) > 1 else 1
    br = _row_block(rows, cols, 1 << 20)

    def body(w_ref, g_ref, m_ref, v_ref, d_ref, nm_ref, nv_ref):
        gv = g_ref[...]
        nm = ADAM_B1 * m_ref[...] + (1.0 - ADAM_B1) * gv
        nv = ADAM_B2 * v_ref[...] + (1.0 - ADAM_B2) * jnp.square(gv)
        m_hat = nm / (1.0 - ADAM_B1 ** ADAM_STEP)
        v_hat = nv / (1.0 - ADAM_B2 ** ADAM_STEP)
        d_ref[...] = -ADAM_LR * (m_hat / (jnp.sqrt(v_hat) + ADAM_EPS) + ADAM_WD * w_ref[...])
        nm_ref[...] = nm
        nv_ref[...] = nv

    spec = pl.BlockSpec((br, cols), lambda i: (i, 0))
    outs = pl.pallas_call(
        body, name="adamw", grid=(rows // br,),
        in_specs=[spec] * 4, out_specs=[spec] * 3,
        out_shape=[jax.ShapeDtypeStruct((rows, cols), F32)] * 3,
        compiler_params=pltpu.CompilerParams(dimension_semantics=("parallel",)),
    )(*(a.reshape(rows, cols) for a in (w, g, m, v)))
    return tuple(o.reshape(shape) for o in outs)


ADD_STEPS = 4


def add_halves(geoms, arrs, received, c_idx, dtypes):
    n = len(arrs)

    def body(c_ref, *refs):
        del c_ref
        for a in range(n):
            refs[2 * n + a][...] = (refs[a][...] + refs[n + a][...]).astype(dtypes[a])

    own_specs, half_specs = [], []
    for g in geoms:
        if g.kind == "cols":
            rows, cols = g.half_shape[0] // ADD_STEPS, g.half_shape[1]
            own_specs.append(pl.BlockSpec((rows, cols), lambda i, c_ref: (ADD_STEPS * c_ref[0] + i, 0)))
            half_specs.append(pl.BlockSpec((rows, cols), lambda i, c_ref: (i, 0)))
        else:
            _, h, cols = g.half_shape
            own_specs.append(pl.BlockSpec((None, None, h, cols), lambda i, c_ref: (i, c_ref[0], 0, 0)))
            half_specs.append(pl.BlockSpec((None, h, cols), lambda i, c_ref: (i, 0, 0)))
    return pl.pallas_call(
        body, name="add_halves",
        grid_spec=pltpu.PrefetchScalarGridSpec(num_scalar_prefetch=1, grid=(ADD_STEPS,),
                                               in_specs=own_specs + half_specs, out_specs=half_specs),
        out_shape=[jax.ShapeDtypeStruct(g.half_shape, dt) for g, dt in zip(geoms, dtypes)],
        compiler_params=_params("parallel"),
    )(c_idx, *arrs, *received)


def add_parts(geoms, landed, finals, layer, c_idx):
    n = len(landed)

    def body(c_ref, *refs):
        del c_ref
        for a in range(n):
            p_ref = refs[a]
            parts = [p_ref[j].astype(F32) for j in range(N_CHIPS)]
            refs[2 * n + a][...] = ((parts[0] + parts[1]) + parts[2]) + parts[3]

    in_specs, out_specs = [], []
    for g in geoms:
        rows, cols = g.part_shape[0] // ADD_STEPS, g.part_shape[1]
        in_specs.append(pl.BlockSpec((N_CHIPS, rows, cols), lambda i, c_ref: (0, i, 0)))
        if g.everywhere:
            out_specs.append(pl.BlockSpec((None, rows, cols), lambda i, c_ref: (c_ref[1], ADD_STEPS * c_ref[0] + i, 0)))
        else:
            out_specs.append(pl.BlockSpec((None, rows, cols), lambda i, c_ref: (layer, ADD_STEPS * c_ref[0] + i, 0)))
    return pl.pallas_call(
        body, name="add_parts",
        grid_spec=pltpu.PrefetchScalarGridSpec(num_scalar_prefetch=1, grid=(ADD_STEPS,),
                                               in_specs=in_specs + [_ANY] * n, out_specs=out_specs),
        out_shape=[jax.ShapeDtypeStruct(f.shape, F32) for f in finals],
        input_output_aliases={1 + n + a: a for a in range(n)},
        compiler_params=_params("parallel"),
    )(c_idx, *landed, *finals)


W_IN_STRIDE = 512
W_IN_WINDOW = 640


def _big_geoms():
    geoms = []
    for name, k, n, axis in BIG:
        if axis == 0:
            geoms.append(_Geom("rows", (N_CHIPS, 2, k // N_CHIPS // 2, n)))
        elif name == "w_in":
            geoms.append(_Geom("cols", (k, n), W_IN_STRIDE, W_IN_WINDOW))
        else:
            geoms.append(_Geom("cols", (k, n), n // N_CHIPS, n // N_CHIPS))
    return geoms


def _round_up(v, m):
    return (v + m - 1) // m * m


def _prep_small(small):
    tril = jnp.tril(jnp.ones((CHUNK, CHUNK), bool))
    wm = jnp.where(tril, small["sgu_w"], 0.0).astype(BF16).reshape(DEPTH, 3, 2, CHUNK, CHUNK)
    head = jnp.arange(D_A) // HEAD_DIM
    grp = jnp.arange(D_C) // HEAD_DIM
    pw_rows = small["pool_w"].reshape(DEPTH, D_C, HEAD_DIM)
    pool_bd = jnp.where((grp[:, None] == grp[None, :])[None], jnp.tile(pw_rows, (1, 1, D_C // HEAD_DIM)), 0.0).astype(BF16)
    return dict(
        wcat=wm.transpose(0, 1, 3, 2, 4).reshape(DEPTH, 3, CHUNK, 2 * CHUNK),
        wcat_t=wm.transpose(0, 1, 2, 4, 3).reshape(DEPTH, 3, 2 * CHUNK, CHUNK),
        bmat=jnp.repeat(jnp.swapaxes(small["sgu_b"], 1, 2), HEAD_DIM, axis=2),
        avg=jnp.where(head[:, None] == head[None, :], 1.0 / HEAD_DIM, 0.0).astype(BF16),
        pool_bd=pool_bd, pool_bd_t=jnp.swapaxes(pool_bd, 1, 2),
        conv8=jnp.pad(small["conv_w"], ((0, 0), (0, 8 - 3), (0, 0))),
    )


def _row(a):
    return a.reshape(1, -1)


MIX_WEIGHTS = ("w_in", "w_out")
MLP_WEIGHTS = ("w_ff1", "w_ff2", "w_ple_gate", "w_ple_proj")
ALL_BIG = MIX_WEIGHTS + MLP_WEIGHTS
FFN_BWD_TILE = 512
PLE_BWD_TILE = 1024


def _fwd_layer(h, p, wl, small, prep, l, tm, comm_mix=None, comm_mlp=None, target=None):
    (proj, ycat, x1), got = mix_fwd(h, _row(small["norm_mix_g"][l]), wl["w_in"], wl["w_out"], prep["wcat"][l],
                                    prep["bmat"][l], _row(small["sgu_ln_g"][l]), _row(small["sgu_ln_b"][l]), prep["avg"],
                                    prep["conv8"][l], prep["pool_bd"][l], _row(small["pool_scale"][l]),
                                    tm=min(2 * tm, h.shape[0]), comm=comm_mix)
    if comm_mix is not None:
        wl = {**wl, **_weights_of(got, MLP_WEIGHTS)}
    if target is not None:
        a, x2, loss_blk, d_final_g, d = ffn_ple_loss_fwd(
            x1, (p, l), _row(small["norm_ff_g"][l]), wl["w_ff1"], wl["w_ff2"], _row(small["norm_ple_g"][l]),
            wl["w_ple_gate"], wl["w_ple_proj"], target, _row(small["final_g"]), tm=tm)
        return (h, proj, ycat, x1, a, x2), (loss_blk, d_final_g, d), [], wl
    (a, x2, x3), couts = ffn_ple_fwd(x1, (p, l), _row(small["norm_ff_g"][l]), wl["w_ff1"], wl["w_ff2"],
                                     _row(small["norm_ple_g"][l]), wl["w_ple_gate"], wl["w_ple_proj"], tm=tm,
                                     comm=comm_mlp)
    return (h, proj, ycat, x1, a, x2), x3, couts, wl


def _merge_comms(comms):
    comms = [cm for cm in comms if cm is not None]
    if len(comms) <= 1:
        return comms[0] if comms else None
    spans, ni, no, ns = [], 0, 0, 0
    for cm in comms:
        spans.append((ni, no, ns))
        ni, no, ns = ni + len(cm.ins), no + len(cm.out_shapes), ns + len(cm.sems)

    def copies(in_refs, out_refs, sem_refs):
        local, sends, recvs = [], [], []
        for cm, (i0, o0, s0) in zip(comms, spans):
            got = cm.copies(in_refs[i0:i0 + len(cm.ins)], out_refs[o0:o0 + len(cm.out_shapes)],
                            sem_refs[s0:s0 + len(cm.sems)])
            local, sends, recvs = local + got[0], sends + got[1], recvs + got[2]
        return local, sends, recvs

    aliases = {i0 + i: o0 + o for cm, (i0, o0, _) in zip(comms, spans) for i, o in cm.aliases.items()}
    assert all(cm.forwards is None for cm in comms)
    return _Comm(sum((cm.ins for cm in comms), []), sum((cm.out_shapes for cm in comms), []),
                 sum((cm.sems for cm in comms), []), copies, aliases)


def _split_results(results, comms):
    out, at = [], 0
    for cm in comms:
        if cm is None:
            out.append(None)
        else:
            out.append(results[at:at + len(cm.out_shapes)])
            at += len(cm.out_shapes)
    return out


class _Reduction:
    def __init__(self, layer, names, geoms, arrs, finals, c_arr, narrow=()):
        self.layer, self.names, self.geoms, self.arrs = layer, list(names), list(geoms), list(arrs)
        self.finals, self.c_arr = finals, c_arr
        self.dtypes = [BF16 if n in narrow else F32 for n in self.names]

    def comm_a(self):
        return _reduce_a_comm(self.geoms, self.arrs)

    def comm_b(self, received):
        return _reduce_b_comm(self.geoms, add_halves(self.geoms, self.arrs, received, self.c_arr, self.dtypes))

    def comm_c(self, landed):
        mine = add_parts(self.geoms, landed, [self.finals[n] for n in self.names], self.layer, self.c_arr)
        return _reduce_c_comm(self.geoms, mine, self.layer)

    def done(self, results):
        self.finals.update(zip(self.names, results))


class _Plan:
    def ple(self):
        return None

    def after_ple(self, results):
        pass

    def ffn(self):
        return None

    def after_ffn(self, results):
        pass

    def out_grad(self, gb):
        return None

    def after_out_grad(self, results):
        pass

    def before_mix(self, gb):
        pass

    def mix(self):
        return None

    def after_mix(self, results):
        pass


class _CarryPlan(_Plan):
    def __init__(self, above):
        self.above = above

    def ple(self):
        return self.above.comm_a()

    def after_ple(self, results):
        self.received = results

    def ffn(self):
        return self.above.comm_b(self.received)

    def after_ffn(self, results):
        self.landed = results

    def mix(self):
        return self.above.comm_c(self.landed)

    def after_mix(self, results):
        self.above.done(results)


class _LastPlan(_CarryPlan):
    def __init__(self, above, make):
        super().__init__(above)
        self.make = make

    def out_grad(self, gb):
        self.early = [self.make(MLP_WEIGHTS, gb)]
        return self.early[0].comm_a()

    def after_out_grad(self, results):
        self.early_received = [results]

    def before_mix(self, gb):
        self.early.append(self.make(("w_out",), gb))
        self.early_received.append(_run_comm(self.early[1].comm_a(), "reduce_a_early"))

    def mix(self):
        self.parts = [self.above.comm_c(self.landed)] + [r.comm_b(got) for r, got in zip(self.early, self.early_received)]
        return _merge_comms(self.parts)

    def after_mix(self, results):
        above_res, *self.early_landed = _split_results(results, self.parts)
        self.above.done(above_res)


def _bwd_layer(d, saved, p, wl, small, prep, l, tm, tk, plan=None):
    plan = plan or _Plan()
    xin, proj, ycat, x1, a, x2 = saved
    (dx2, dg_ple, dw_gate, dw_proj), res = ple_bwd(d, x2, (p, l), _row(small["norm_ple_g"][l]), wl["w_ple_gate"],
                                                   wl["w_ple_proj"], tm=min(PLE_BWD_TILE, d.shape[0]), comm=plan.ple())
    plan.after_ple(res)
    gb = {"w_ple_gate": dw_gate, "w_ple_proj": dw_proj}
    (dx1, h2, da, dg_ff), res = ffn_bwd(dx2, x1, a, _row(small["norm_ff_g"][l]), wl["w_ff1"], wl["w_ff2"],
                                        tm=FFN_BWD_TILE if tm >= FFN_BWD_TILE else tm, comm=plan.ffn())
    plan.after_ffn(res)
    gb["w_ff2"] = wgrad(a, dx2, tk=tk, relu_sq=True)
    gb["w_ff1"] = wgrad(h2, da, tk=min(2 * tk, h2.shape[0]))
    comm = plan.out_grad(gb)
    if comm is None:
        gb["w_out"] = wgrad(ycat, dx1, tk=tk)
    else:
        gb["w_out"], res = wgrad(ycat, dx1, tk=tk, comm=comm)
        plan.after_out_grad(res)
    plan.before_mix(gb)
    (dprev, h1, dproj, dg_mix, dws, dbm, dlng, dlnb, dcw, dpw, dps), res = mix_bwd(
        dx1, xin, proj, _row(small["norm_mix_g"][l]), wl["w_in"], wl["w_out"], prep["wcat"][l], prep["wcat_t"][l],
        prep["bmat"][l], _row(small["sgu_ln_g"][l]), _row(small["sgu_ln_b"][l]), prep["avg"], prep["conv8"][l],
        prep["pool_bd"][l], prep["pool_bd_t"][l], _row(small["pool_scale"][l]), tm=tm, comm=plan.mix())
    plan.after_mix(res)
    gb["w_in"] = wgrad(h1, dproj, tk=tk)
    gs = {
        "norm_ple_g": dg_ple[0], "norm_ff_g": dg_ff[0], "norm_mix_g": dg_mix[0],
        "sgu_w": dws.reshape(2 * 3, CHUNK, CHUNK), "sgu_b": dbm[:, ::HEAD_DIM].T,
        "sgu_ln_g": dlng[0], "sgu_ln_b": dlnb[0], "conv_w": dcw[0:3], "pool_scale": dps[0],
        "pool_w": jnp.stack([dpw[g * HEAD_DIM:(g + 1) * HEAD_DIM, g * HEAD_DIM:(g + 1) * HEAD_DIM]
                             for g in range(D_C // HEAD_DIM)]),
    }
    return dprev, gb, gs


def _local_step(x, p, target, full, small, *, tm, tk):
    prep = _prep_small(small)
    p = p[:, None]
    saved, h = [], x
    for l in range(DEPTH):
        wl = {name: full[name][l] for name in full}
        s, h, _, _ = _fwd_layer(h, p, wl, small, prep, l, tm, target=target if l == DEPTH - 1 else None)
        saved.append(s)
    loss_blk, d_final_g, d = h
    gbig, gsm = [None] * DEPTH, [None] * DEPTH
    for l in reversed(range(DEPTH)):
        wl = {name: full[name][l] for name in full}
        d, gbig[l], gsm[l] = _bwd_layer(d, saved[l], p, wl, small, prep, l, tm, tk)
    big = {name: jnp.stack([gbig[l][name] for l in range(DEPTH)]) for name in gbig[0]}
    sm = {name: jnp.stack([gsm[l][name] for l in range(DEPTH)]) for name in gsm[0]}
    sm["final_g"] = d_final_g[0]
    return loss_blk[0, 0], d, big, sm


def _weights_of(gathered, names):
    wl = dict(zip([b[0] for b in BIG if b[0] in names], gathered))
    if "w_in" in wl:
        wl["w_in"] = wl["w_in"].transpose(1, 0, 2).reshape(D_MODEL, D_IN)
    return wl


def kernel(x, p, norm_mix_g, w_in, sgu_w, sgu_b, sgu_ln_g, sgu_ln_b, conv_w, pool_w, pool_scale, w_out, norm_ff_g, w_ff1, w_ff2, norm_ple_g, w_ple_gate, w_ple_proj, final_g, loss_target, m_norm_mix_g, m_w_in, m_sgu_w, m_sgu_b, m_sgu_ln_g, m_sgu_ln_b, m_conv_w, m_pool_w, m_pool_scale, m_w_out, m_norm_ff_g, m_w_ff1, m_w_ff2, m_norm_ple_g, m_w_ple_gate, m_w_ple_proj, m_final_g, v_norm_mix_g, v_w_in, v_sgu_w, v_sgu_b, v_sgu_ln_g, v_sgu_ln_b, v_conv_w, v_pool_w, v_pool_scale, v_w_out, v_norm_ff_g, v_w_ff1, v_w_ff2, v_norm_ple_g, v_w_ple_gate, v_w_ple_proj, v_final_g):
    args = dict(locals())
    w = {name: args[name] for name in WEIGHTS}
    m = {name: args["m_" + name] for name in WEIGHTS}
    v = {name: args["v_" + name] for name in WEIGHTS}
    t = x.shape[1]
    tm = min(512, t)
    tk = min(2048, t)
    x_idx, y_idx, c_idx = _place()
    chip = 2 * x_idx + y_idx
    c_arr = jnp.stack([c_idx, chip]).astype(jnp.int32)
    xs, target = x[0], loss_target[0]

    shards = {name: w[name].astype(BF16) for name, _, _, _ in BIG}
    conv_rows = _round_up(CONV_SHARD, 8 * 128) // 128
    conv_flat = jnp.pad(w["conv_w"].reshape(-1), (0, conv_rows * 128 - CONV_SHARD)).reshape(conv_rows, 128)
    first = _run_comm(_gather_halved_comm(shards, 0, MIX_WEIGHTS, conv_flat), "gather_first")
    conv_full = (first[len(MIX_WEIGHTS)].reshape(N_CHIPS, -1)[:, :CONV_SHARD]
                 .reshape(N_CHIPS, DEPTH, 3, D_B // N_CHIPS).transpose(1, 2, 0, 3).reshape(DEPTH, 3, D_B))
    small = {name: w[name] for name in SMALL}
    small["conv_w"] = conv_full
    prep = _prep_small(small)

    wl = [None] * DEPTH
    wl[0] = _weights_of(first, MIX_WEIGHTS)
    saved, h = [], xs
    for l in range(DEPTH):
        comm_mix = None
        if l == 0:
            comm_mix = _gather_halved_comm(shards, 0, MLP_WEIGHTS)
        comm_mlp = _gather_comm(shards, l + 1, ALL_BIG) if l + 1 < DEPTH else None
        s, h, got, wl[l] = _fwd_layer(h, p, wl[l], small, prep, l, tm, comm_mix, comm_mlp,
                                      target=target if l == DEPTH - 1 else None)
        saved.append(s)
        if comm_mlp is not None:
            wl[l + 1] = _weights_of(got, ALL_BIG)
    loss_blk, d_final_g, d = h

    geoms = dict(zip([b[0] for b in BIG], _big_geoms()))
    finals = {name: jnp.zeros((DEPTH,) + g.final_shape, F32) for name, g in geoms.items()}

    def reduction(layer, names, gb, narrow=()):
        return _Reduction(layer, names, [geoms[n] for n in names], [gb[n].reshape(geoms[n].shape) for n in names],
                          finals, c_arr, narrow)

    gsm = [None] * DEPTH
    above = None
    for l in reversed(range(DEPTH)):
        if above is None:
            plan = _Plan()
        elif l > 0:
            plan = _CarryPlan(above)
        else:
            plan = _LastPlan(above, lambda names, gb: reduction(0, names, gb))
        d, gb, gsm[l] = _bwd_layer(d, saved[l], p, wl[l], small, prep, l, tm, tk, plan)
        if l > 0:
            above = reduction(l, ALL_BIG, gb)

    sm = {name: jnp.stack([gsm[i][name] for i in range(DEPTH)]) for name in gsm[0]}
    sm["final_g"] = d_final_g[0]
    sizes = [sm[name].size for name in SMALL]
    small_rows = _round_up(-(-sum(sizes) // (2 * N_CHIPS * LANES)), 8 * ADD_STEPS)
    small_flat = jnp.pad(jnp.concatenate([sm[name].reshape(-1) for name in SMALL]),
                         (0, 2 * N_CHIPS * small_rows * LANES - sum(sizes)))
    geoms["small"] = _Geom("rows", (N_CHIPS, 2, small_rows, LANES), everywhere=True)
    finals["small"] = jnp.zeros((N_CHIPS,) + geoms["small"].final_shape, F32)
    late = reduction(0, ("w_in", "small"), {**gb, "small": small_flat}, narrow=("w_in",))
    late_landed = _run_comm(late.comm_b(_run_comm(late.comm_a(), "reduce_a_late")), "reduce_b_late")
    finishing = plan.early + [late]
    last = [r.comm_c(got) for r, got in zip(finishing, plan.early_landed + [late_landed])]
    for r, res in zip(finishing, _split_results(_run_comm(_merge_comms(last), "reduce_c_last"), last)):
        r.done(res)

    grads = {name: finals[name] for name, _, _, _ in BIG}
    grads["w_in"] = lax.dynamic_slice_in_dim(grads["w_in"], chip * (D_IN // N_CHIPS - W_IN_STRIDE), D_IN // N_CHIPS, axis=2)
    small_red = finals["small"].reshape(-1)
    off = 0
    for name, size in zip(SMALL, sizes):
        grads[name] = small_red[off:off + size].reshape(sm[name].shape)
        off += size
    grads["conv_w"] = lax.dynamic_slice_in_dim(grads["conv_w"], chip * (D_B // N_CHIPS), D_B // N_CHIPS, axis=2)

    loss = lax.psum(loss_blk[0, 0], ("x", "y", "c"))
    delta, new_m, new_v = {}, {}, {}
    for name in WEIGHTS:
        delta[name], new_m[name], new_v[name] = adamw(w[name], grads[name], m[name], v[name])
    return (loss, d[None], *[grads[n] for n in WEIGHTS], *[delta[n] for n in WEIGHTS],
            *[new_m[n] for n in WEIGHTS], *[new_v[n] for n in WEIGHTS])
```

```python
import math

import jax
import jax.numpy as jnp
from jax import lax
from jax.experimental import pallas as pl
from jax.experimental.pallas import tpu as pltpu

F32 = jnp.float32
BF16 = jnp.bfloat16

D_MODEL = 1024
DEPTH = 4
D_PLE = 256
D_FF = 4096
HEAD_DIM = 64
D_A = 384
D_B = 384
D_C = 256
D_IN = 2176
CHUNK = 128
HALO = 16
RMS_EPS = 1e-6
LN_EPS = 1e-5
N_CHIPS = 4
LANES = 1024

ADAM_LR = 0.001
ADAM_B1 = 0.9
ADAM_B2 = 0.999
ADAM_EPS = 1e-08
ADAM_WD = 0.01
ADAM_STEP = 10

VMEM_LIMIT_BYTES = 60 * 1024 * 1024

_RSQRT2 = 0.7071067811865476
_INV_SQRT_2PI = 0.3989422804014327

BIG = (
    ("w_in", D_MODEL, D_IN, 1),
    ("w_out", D_MODEL, D_MODEL, 0),
    ("w_ff1", D_MODEL, D_FF, 1),
    ("w_ff2", D_FF, D_MODEL, 0),
    ("w_ple_gate", D_MODEL, D_MODEL, 0),
    ("w_ple_proj", D_PLE, D_MODEL, 1),
)
SMALL = ("norm_mix_g", "sgu_w", "sgu_b", "sgu_ln_g", "sgu_ln_b", "conv_w", "pool_w", "pool_scale",
         "norm_ff_g", "norm_ple_g", "final_g")
WEIGHTS = ("norm_mix_g", "w_in", "sgu_w", "sgu_b", "sgu_ln_g", "sgu_ln_b", "conv_w", "pool_w", "pool_scale",
           "w_out", "norm_ff_g", "w_ff1", "w_ff2", "norm_ple_g", "w_ple_gate", "w_ple_proj", "final_g")
CONV_SHARD = DEPTH * 3 * (D_B // N_CHIPS)


def _dot(a, b):
    return jnp.dot(a, b, preferred_element_type=F32)


def _dot_nt(a, b):
    return lax.dot_general(a, b, (((1,), (1,)), ((), ())), preferred_element_type=F32)


def _dot_tn(a, b):
    return lax.dot_general(a, b, (((0,), (0,)), ((), ())), preferred_element_type=F32)


def _const_spec(shape):
    nd = len(shape)
    return pl.BlockSpec(shape, lambda i: (0,) * nd, pipeline_mode=pl.Buffered(1))


def _acc_spec(shape):
    nd = len(shape)
    return pl.BlockSpec(shape, lambda i: (0,) * nd)


def _layer_rows(layer, tm):
    return pl.BlockSpec((None, None, tm, D_PLE), lambda i: (layer, 0, i, 0))


def _params(*sem):
    return pltpu.CompilerParams(dimension_semantics=sem, vmem_limit_bytes=VMEM_LIMIT_BYTES)


def _rms_bwd(dh, n, rs, g):
    dn = dh * g
    return rs * (dn - n * jnp.mean(dn * n, axis=-1, keepdims=True))


def _gelu(x):
    return x * (0.5 * (1.0 + lax.erf(x * _RSQRT2)))


def _gelu_and_grad(x):
    cdf = 0.5 * (1.0 + lax.erf(x * _RSQRT2))
    return x * cdf, cdf + x * (jnp.exp(-0.5 * x * x) * _INV_SQRT_2PI)


def _group_mean(v, avg):
    vb = v.astype(BF16)
    split = 2 * CHUNK
    return jnp.concatenate([_dot(vb[:, :split], avg[:split, :split]), _dot(vb[:, split:], avg[split:, split:])], axis=1)


def _group_mean_split(v, avg):
    hi = v.astype(BF16)
    lo = (v - hi.astype(F32)).astype(BF16)
    return _dot(hi, avg) + _dot(lo, avg)


def _lane_lt(shape, bound):
    return lax.broadcasted_iota(jnp.int32, shape, 1) < bound


def _sgu_mix(vnb2, wcat_j, lo_mask):
    zero = jnp.zeros_like(vnb2)
    stacked = jnp.concatenate([jnp.where(lo_mask, vnb2, zero), jnp.where(lo_mask, zero, vnb2)], axis=0)
    return _dot(wcat_j, stacked)


def _pool_means(ext, tile_rows, first_pos):
    s2 = ext + pltpu.roll(ext, 1, 0)
    s4 = s2 + pltpu.roll(s2, 2, 0)
    s8 = s4 + pltpu.roll(s4, 4, 0)
    s16 = s8 + pltpu.roll(s8, 8, 0)
    pos = (first_pos + lax.broadcasted_iota(jnp.int32, (tile_rows, 1), 0) + 1).astype(F32)
    lane = lax.broadcasted_iota(jnp.int32, (tile_rows, D_C), 1)
    sums = jnp.where(lane < 64, s2[HALO:], jnp.where(lane < 128, s4[HALO:], jnp.where(lane < 192, s8[HALO:], s16[HALO:])))
    win = jnp.where(lane < 64, 2.0, jnp.where(lane < 128, 4.0, jnp.where(lane < 192, 8.0, 16.0)))
    inv = 1.0 / jnp.minimum(pos, win)
    return sums * inv, inv


MESH = pl.DeviceIdType.MESH
_ANY = pl.BlockSpec(memory_space=pl.ANY)


def _place():
    return lax.axis_index("x"), lax.axis_index("y"), lax.axis_index("c")


def _chip_peers(x, y):
    return [(1 - x, y), (x, 1 - y), (1 - x, 1 - y)]


class _Comm:
    def __init__(self, ins, out_shapes, sems, copies, aliases=None, forwards=None):
        self.ins, self.out_shapes, self.sems, self.copies = list(ins), list(out_shapes), list(sems), copies
        self.aliases = dict(aliases or {})
        self.forwards = forwards

    def start(self, in_refs, out_refs, sem_refs):
        local, sends, _ = self.copies(in_refs, out_refs, sem_refs)
        for cp in local + sends:
            cp.start()

    def wait(self, in_refs, out_refs, sem_refs):
        local, sends, recvs = self.copies(in_refs, out_refs, sem_refs)
        for cp in recvs:
            cp.wait_recv()
        passed, passed_in = self.forwards(in_refs, out_refs, sem_refs) if self.forwards else ([], [])
        for cp in passed:
            cp.start()
        for cp in sends:
            cp.wait_send()
        for cp in local:
            cp.wait()
        for cp in passed_in:
            cp.wait_recv()
        for cp in passed:
            cp.wait_send()


def _remote(src, dst, send_sem, recv_sem, device):
    return pltpu.make_async_remote_copy(src_ref=src, dst_ref=dst, send_sem=send_sem, recv_sem=recv_sem,
                                        device_id=device, device_id_type=MESH)


def _gather_comm(shards, layer, names, conv=None):
    mats = [b for b in BIG if b[0] in names]
    ins = [shards[name] for name, _, _, _ in mats] + ([conv] if conv is not None else [])
    out_shapes = []
    for name, k, n, axis in mats:
        shape = (N_CHIPS, k, n // N_CHIPS) if name == "w_in" else (k, n)
        out_shapes.append(jax.ShapeDtypeStruct(shape, BF16))
    if conv is not None:
        out_shapes.append(jax.ShapeDtypeStruct((N_CHIPS,) + conv.shape, conv.dtype))
    n_arr = len(ins)

    def block(a, out_ref, chip):
        if a == len(mats) or mats[a][0] == "w_in":
            return out_ref.at[chip]
        _, k, n, axis = mats[a]
        if axis == 0:
            return out_ref.at[pl.ds(chip * (k // N_CHIPS), k // N_CHIPS), :]
        return out_ref.at[:, pl.ds(chip * (n // N_CHIPS), n // N_CHIPS)]

    def copies(in_refs, out_refs, sem_refs):
        send_sems, recv_sems, local_sems = sem_refs
        x, y, c = _place()
        me = 2 * x + y
        local, sends, recvs = [], [], []
        for a in range(n_arr):
            src = in_refs[a].at[layer] if a < len(mats) else in_refs[a]
            local.append(pltpu.make_async_copy(src, block(a, out_refs[a], me), local_sems.at[a]))
            for j, (px, py) in enumerate(_chip_peers(x, y)):
                sends.append(_remote(src, block(a, out_refs[a], me), send_sems.at[a, j], recv_sems.at[a, j], (px, py, c)))
                recvs.append(_remote(src, block(a, out_refs[a], 2 * px + py), send_sems.at[a, j], recv_sems.at[a, j],
                                     (px, py, c)))
        return local, sends, recvs

    sems = [pltpu.SemaphoreType.DMA((n_arr, 3)), pltpu.SemaphoreType.DMA((n_arr, 3)), pltpu.SemaphoreType.DMA((n_arr,))]
    return _Comm(ins, out_shapes, sems, copies)


def _gather_halved_comm(shards, layer, names, conv=None):
    mats = [b for b in BIG if b[0] in names]
    ins = [shards[name] for name, _, _, _ in mats] + ([conv] if conv is not None else [])
    out_shapes = []
    for name, k, n, axis in mats:
        out_shapes.append(jax.ShapeDtypeStruct((N_CHIPS, k, n // N_CHIPS) if name == "w_in" else (k, n), BF16))
    if conv is not None:
        out_shapes.append(jax.ShapeDtypeStruct((N_CHIPS,) + conv.shape, conv.dtype))
    n_arr = len(ins)

    def whole(a, ref, chip):
        if a == len(mats) or mats[a][0] == "w_in":
            return ref.at[chip]
        _, k, n, axis = mats[a]
        if axis == 0:
            return ref.at[pl.ds(chip * (k // N_CHIPS), k // N_CHIPS), :]
        return ref.at[:, pl.ds(chip * (n // N_CHIPS), n // N_CHIPS)]

    def src_half(a, in_ref, core):
        if a == len(mats):
            rows = conv.shape[0] // 2
            return in_ref.at[pl.ds(core * rows, rows), :]
        rows = shards[mats[a][0]].shape[1] // 2
        return in_ref.at[layer, pl.ds(core * rows, rows), :]

    def half(a, out_ref, chip, core):
        if a == len(mats):
            rows = conv.shape[0] // 2
            return out_ref.at[chip, pl.ds(core * rows, rows), :]
        name, k, n, axis = mats[a]
        if name == "w_in":
            return out_ref.at[chip, pl.ds(core * (k // 2), k // 2), :]
        if axis == 0:
            rows = k // N_CHIPS // 2
            return out_ref.at[pl.ds(chip * 2 * rows + core * rows, rows), :]
        return out_ref.at[pl.ds(core * (k // 2), k // 2), pl.ds(chip * (n // N_CHIPS), n // N_CHIPS)]

    def copies(in_refs, out_refs, sem_refs):
        send_sems, recv_sems, own_send, own_recv = sem_refs[0], sem_refs[1], sem_refs[4], sem_refs[5]
        x, y, c = _place()
        me = 2 * x + y
        sends, recvs = [], []
        for a in range(n_arr):
            own = in_refs[a].at[layer] if a < len(mats) else in_refs[a]
            cp = _remote(own, whole(a, out_refs[a], me), own_send.at[a], own_recv.at[a], (x, y, 1 - c))
            sends.append(cp)
            recvs.append(cp)
            for j, (px, py) in enumerate(_chip_peers(x, y)):
                sends.append(_remote(src_half(a, in_refs[a], c), half(a, out_refs[a], me, c), send_sems.at[a, j],
                                     recv_sems.at[a, j], (px, py, c)))
                recvs.append(_remote(src_half(a, in_refs[a], c), half(a, out_refs[a], 2 * px + py, c), send_sems.at[a, j],
                                     recv_sems.at[a, j], (px, py, c)))
        return [], sends, recvs

    def forwards(in_refs, out_refs, sem_refs):
        send_sems, recv_sems = sem_refs[2], sem_refs[3]
        x, y, c = _place()
        sends, recvs = [], []
        for a in range(n_arr):
            for j, (px, py) in enumerate(_chip_peers(x, y)):
                peer = 2 * px + py
                sends.append(_remote(half(a, out_refs[a], peer, c), half(a, out_refs[a], peer, c), send_sems.at[a, j],
                                     recv_sems.at[a, j], (x, y, 1 - c)))
                recvs.append(_remote(half(a, out_refs[a], peer, c), half(a, out_refs[a], peer, 1 - c), send_sems.at[a, j],
                                     recv_sems.at[a, j], (x, y, 1 - c)))
        return sends, recvs

    sems = [pltpu.SemaphoreType.DMA((n_arr, 3))] * 4 + [pltpu.SemaphoreType.DMA((n_arr,))] * 2
    return _Comm(ins, out_shapes, sems, copies, forwards=forwards)


class _Geom:
    def __init__(self, kind, shape, stride=None, width=None, everywhere=False):
        self.kind, self.shape, self.stride, self.width = kind, tuple(shape), stride, width
        self.everywhere = everywhere
        if kind == "cols":
            k, n = shape
            self.half_shape, self.part_shape, self.final_shape = (k // 2, n), (k // 2, width), (k, width)
        else:
            _, _, h, n = shape
            self.half_shape, self.part_shape, self.final_shape = (N_CHIPS, h, n), (h, n), (2 * h, n)

    def half(self, ref, core):
        if self.kind == "cols":
            return ref.at[pl.ds(core * self.half_shape[0], self.half_shape[0]), :]
        return ref.at[:, core]

    def part(self, ref, chip):
        if self.kind == "cols":
            return ref.at[:, pl.ds(chip * self.stride, self.width)]
        return ref.at[chip]

    def final_half(self, ref, layer, core):
        rows = self.part_shape[0]
        return ref.at[layer, pl.ds(core * rows, rows), :]


def _reduce_a_comm(geoms, arrs):
    n = len(arrs)

    def copies(in_refs, out_refs, sem_refs):
        x, y, c = _place()
        cps = [_remote(geoms[a].half(in_refs[a], 1 - c), out_refs[a], sem_refs[0].at[a], sem_refs[1].at[a], (x, y, 1 - c))
               for a in range(n)]
        return [], cps, cps

    return _Comm(arrs, [jax.ShapeDtypeStruct(g.half_shape, F32) for g in geoms],
                 [pltpu.SemaphoreType.DMA((n,)), pltpu.SemaphoreType.DMA((n,))], copies)


def _reduce_b_comm(geoms, halves):
    n = len(halves)

    def copies(in_refs, out_refs, sem_refs):
        send_sems, recv_sems, local_sems = sem_refs
        x, y, c = _place()
        me = 2 * x + y
        local, sends, recvs = [], [], []
        for a in range(n):
            g = geoms[a]
            local.append(pltpu.make_async_copy(g.part(in_refs[a], me), out_refs[a].at[me], local_sems.at[a]))
            for j, (px, py) in enumerate(_chip_peers(x, y)):
                peer = 2 * px + py
                sends.append(_remote(g.part(in_refs[a], peer), out_refs[a].at[me], send_sems.at[a, j], recv_sems.at[a, j],
                                     (px, py, c)))
                recvs.append(_remote(g.part(in_refs[a], me), out_refs[a].at[peer], send_sems.at[a, j], recv_sems.at[a, j],
                                     (px, py, c)))
        return local, sends, recvs

    sems = [pltpu.SemaphoreType.DMA((n, 3)), pltpu.SemaphoreType.DMA((n, 3)), pltpu.SemaphoreType.DMA((n,))]
    return _Comm(halves, [jax.ShapeDtypeStruct((N_CHIPS,) + g.part_shape, h.dtype) for g, h in zip(geoms, halves)], sems,
                 copies)


def _reduce_c_comm(geoms, finals, layer):
    n = len(finals)

    def copies(in_refs, out_refs, sem_refs):
        send_sems, recv_sems = sem_refs
        x, y, c = _place()
        me = 2 * x + y
        sends, recvs = [], []
        for a in range(n):
            g = geoms[a]
            at = me if g.everywhere else layer
            mine = g.final_half(in_refs[a], at, c)
            sends.append(_remote(mine, g.final_half(out_refs[a], at, c), send_sems.at[a, 0], recv_sems.at[a, 0],
                                 (x, y, 1 - c)))
            recvs.append(_remote(mine, g.final_half(out_refs[a], at, 1 - c), send_sems.at[a, 0], recv_sems.at[a, 0],
                                 (x, y, 1 - c)))
            if not g.everywhere:
                continue
            for j, (px, py) in enumerate(_chip_peers(x, y)):
                for core, slot in ((c, 1 + 2 * j), (1 - c, 2 + 2 * j)):
                    sends.append(_remote(mine, g.final_half(out_refs[a], me, c), send_sems.at[a, slot],
                                         recv_sems.at[a, slot], (px, py, core)))
                    recvs.append(_remote(mine, g.final_half(out_refs[a], 2 * px + py, core), send_sems.at[a, slot],
                                         recv_sems.at[a, slot], (px, py, core)))
        return [], sends, recvs

    sems = [pltpu.SemaphoreType.DMA((n, 7)), pltpu.SemaphoreType.DMA((n, 7))]
    return _Comm(finals, [jax.ShapeDtypeStruct(f.shape, f.dtype) for f in finals], sems, copies,
                 aliases={a: a for a in range(n)})


def _run_comm(comm, name):
    def body(*refs):
        ni, no = len(comm.ins), len(comm.out_shapes)
        in_refs, out_refs, sem_refs = refs[:ni], refs[ni:ni + no], refs[ni + no:]
        comm.start(in_refs, out_refs, sem_refs)
        comm.wait(in_refs, out_refs, sem_refs)

    return pl.pallas_call(
        body, name=name, in_specs=[_ANY] * len(comm.ins), out_specs=[_ANY] * len(comm.out_shapes),
        out_shape=comm.out_shapes, scratch_shapes=comm.sems, input_output_aliases=comm.aliases,
        compiler_params=pltpu.CompilerParams(has_side_effects=True),
    )(*comm.ins)


def _tile_call(body, name, nt, in_specs, out_specs, out_shape, scratch, args, comm):
    if comm is None:
        outs = pl.pallas_call(body, name=name, grid=(nt,), in_specs=in_specs, out_specs=out_specs, out_shape=out_shape,
                              scratch_shapes=scratch, compiler_params=_params("arbitrary"))(*args)
        return outs, []
    n_in, n_out, n_scr = len(in_specs), len(out_specs), len(scratch)
    ci, co = len(comm.ins), len(comm.out_shapes)

    def hosted(*refs):
        in_refs = refs[:n_in]
        cin = refs[n_in:n_in + ci]
        out_refs = refs[n_in + ci:n_in + ci + n_out]
        cout = refs[n_in + ci + n_out:n_in + ci + n_out + co]
        scr = refs[n_in + ci + n_out + co:n_in + ci + n_out + co + n_scr]
        sems = refs[n_in + ci + n_out + co + n_scr:]
        i = pl.program_id(0)

        @pl.when(i == 0)
        def _():
            comm.start(cin, cout, sems)

        body(*in_refs, *out_refs, *scr)

        @pl.when(i == nt - 1)
        def _():
            comm.wait(cin, cout, sems)

    outs = pl.pallas_call(
        hosted, name=name + "_comm", grid=(nt,),
        in_specs=list(in_specs) + [_ANY] * ci, out_specs=list(out_specs) + [_ANY] * co,
        out_shape=list(out_shape) + comm.out_shapes, scratch_shapes=list(scratch) + comm.sems,
        input_output_aliases={n_in + i: n_out + o for i, o in comm.aliases.items()},
        compiler_params=_params("arbitrary"),
    )(*args, *comm.ins)
    return outs[:n_out], outs[n_out:]


def mix_fwd(x, g_mix, w_in, w_out, wcat, bmat, ln_g, ln_b, avg, conv_w, pool_bd, pool_scale, *, tm, comm=None):
    t = x.shape[0]
    nt = t // tm

    def body(x_ref, g_ref, win_ref, wout_ref, wcat_ref, bmat_ref, lng_ref, lnb_ref, avg_ref, cw_ref, pw_ref, ps_ref,
             projb_ref, ycat_ref, x1_ref, hbuf, zbuf, proj_ref):
        i = pl.program_id(0)

        @pl.when(i == 0)
        def _():
            hbuf[0:HALO, :] = jnp.zeros((HALO, D_B), F32)
            zbuf[0:HALO, :] = jnp.zeros((HALO, D_C), F32)

        xv = x_ref[...]
        n = xv * lax.rsqrt(jnp.mean(xv * xv, axis=-1, keepdims=True) + RMS_EPS)
        h1 = (n * g_ref[...]).astype(BF16)
        proj_ref[...] = _dot(h1, win_ref[...])
        projb_ref[...] = proj_ref[...].astype(BF16)

        lo_mask = _lane_lt((CHUNK, CHUNK), HEAD_DIM)
        avg = avg_ref[...]
        gu = _gelu(proj_ref[:, 0:D_A])
        gv = _gelu(proj_ref[:, D_A:2 * D_A])
        dv = gv - _group_mean(gv, avg)
        var = _group_mean(dv * dv, avg)
        vnb = (dv * lax.rsqrt(var + LN_EPS) * lng_ref[...] + lnb_ref[...]).astype(BF16)
        for c in range(tm // CHUNK):
            rows = slice(c * CHUNK, (c + 1) * CHUNK)
            for j in range(3):
                cols = slice(j * CHUNK, (j + 1) * CHUNK)
                mixed = _sgu_mix(vnb[rows, cols], wcat_ref[j], lo_mask) + bmat_ref[:, cols]
                ycat_ref[rows, cols] = (gu[rows, cols] * mixed).astype(BF16)

        o = 2 * D_A
        hcur = proj_ref[:, o + 2 * D_B:o + 3 * D_B] * proj_ref[:, o:o + D_B]
        hbuf[HALO:HALO + tm, :] = hcur
        y = (cw_ref[2:3, :] * hcur + cw_ref[1:2, :] * hbuf[pl.ds(HALO - 1, tm), :]
             + cw_ref[0:1, :] * hbuf[pl.ds(HALO - 2, tm), :])
        ycat_ref[:, D_A:D_A + D_B] = (proj_ref[:, o + D_B:o + 2 * D_B] * y).astype(BF16)
        hbuf[0:HALO, :] = hbuf[tm:tm + HALO, :]

        zc = proj_ref[:, o + 3 * D_B:D_IN]
        zbuf[HALO:HALO + tm, :] = zc
        mean, _ = _pool_means(zbuf[...], tm, i * tm)
        pooled = (mean - zc).astype(BF16)
        ycat_ref[:, D_A + D_B:D_MODEL] = (_dot(pooled, pw_ref[...]) * ps_ref[...]).astype(BF16)
        zbuf[0:HALO, :] = zbuf[tm:tm + HALO, :]

        x1_ref[...] = xv + _dot(ycat_ref[...], wout_ref[...])

    row = lambda w: pl.BlockSpec((tm, w), lambda i: (i, 0))
    return _tile_call(
        body, "mix_fwd", nt,
        [row(D_MODEL), _const_spec((1, D_MODEL)), _const_spec((D_MODEL, D_IN)), _const_spec((D_MODEL, D_MODEL)),
         _const_spec((3, CHUNK, 2 * CHUNK)), _const_spec((CHUNK, D_A)), _const_spec((1, D_A)), _const_spec((1, D_A)),
         _const_spec((D_A, D_A)), _const_spec((8, D_B)), _const_spec((D_C, D_C)), _const_spec((1, D_C))],
        [row(D_IN), row(D_MODEL), row(D_MODEL)],
        [jax.ShapeDtypeStruct((t, D_IN), BF16), jax.ShapeDtypeStruct((t, D_MODEL), BF16),
         jax.ShapeDtypeStruct((t, D_MODEL), F32)],
        [pltpu.VMEM((tm + HALO, D_B), F32), pltpu.VMEM((tm + HALO, D_C), F32), pltpu.VMEM((tm, D_IN), F32)],
        (x, g_mix, w_in, w_out, wcat, bmat, ln_g, ln_b, avg, conv_w, pool_bd, pool_scale), comm)


def ffn_ple_fwd(x1, p, g_ff, w_ff1, w_ff2, g_ple, w_gate, w_proj, *, tm, comm=None):
    t = x1.shape[0]
    nt = t // tm
    nc = D_FF // D_MODEL

    def body(x1_ref, p_ref, gff_ref, w1_ref, w2_ref, gple_ref, wg_ref, wp_ref, a_ref, x2_ref, x3_ref):
        x1v = x1_ref[...]
        n2 = x1v * lax.rsqrt(jnp.mean(x1v * x1v, axis=-1, keepdims=True) + RMS_EPS)
        h2 = (n2 * gff_ref[...]).astype(BF16)
        acc = x1v
        for c in range(nc):
            cols = slice(c * D_MODEL, (c + 1) * D_MODEL)
            a = _dot(h2, w1_ref[:, cols])
            a_ref[:, cols] = a.astype(BF16)
            ra = jnp.maximum(a, 0.0)
            acc = acc + _dot((ra * ra).astype(BF16), w2_ref[cols, :])
        x2_ref[...] = acc
        n3 = acc * lax.rsqrt(jnp.mean(acc * acc, axis=-1, keepdims=True) + RMS_EPS)
        h3 = (n3 * gple_ref[...]).astype(BF16)
        gate = jax.nn.sigmoid(_dot(h3, wg_ref[...]))
        pp = _dot(p_ref[...].astype(BF16), wp_ref[...])
        x3_ref[...] = acc + pp * gate

    row = lambda w: pl.BlockSpec((tm, w), lambda i: (i, 0))
    return _tile_call(
        body, "ffn_ple_fwd", nt,
        [row(D_MODEL), _layer_rows(p[1], tm), _const_spec((1, D_MODEL)), _const_spec((D_MODEL, D_FF)),
         _const_spec((D_FF, D_MODEL)), _const_spec((1, D_MODEL)), _const_spec((D_MODEL, D_MODEL)),
         _const_spec((D_PLE, D_MODEL))],
        [row(D_FF), row(D_MODEL), row(D_MODEL)],
        [jax.ShapeDtypeStruct((t, D_FF), BF16), jax.ShapeDtypeStruct((t, D_MODEL), F32),
         jax.ShapeDtypeStruct((t, D_MODEL), F32)],
        [], (x1, p[0], g_ff, w_ff1, w_ff2, g_ple, w_gate, w_proj), comm)


def ffn_ple_loss_fwd(x1, p, g_ff, w_ff1, w_ff2, g_ple, w_gate, w_proj, target, g_final, *, tm):
    t = x1.shape[0]
    nt = t // tm
    nc = D_FF // D_MODEL

    def body(x1_ref, p_ref, gff_ref, w1_ref, w2_ref, gple_ref, wg_ref, wp_ref, t_ref, gfin_ref,
             a_ref, x2_ref, loss_ref, dg_ref, dx_ref, sq_acc):
        i = pl.program_id(0)

        @pl.when(i == 0)
        def _():
            sq_acc[...] = jnp.zeros_like(sq_acc)
            dg_ref[...] = jnp.zeros_like(dg_ref)

        x1v = x1_ref[...]
        n2 = x1v * lax.rsqrt(jnp.mean(x1v * x1v, axis=-1, keepdims=True) + RMS_EPS)
        h2 = (n2 * gff_ref[...]).astype(BF16)
        acc = x1v
        for c in range(nc):
            cols = slice(c * D_MODEL, (c + 1) * D_MODEL)
            a = _dot(h2, w1_ref[:, cols])
            a_ref[:, cols] = a.astype(BF16)
            ra = jnp.maximum(a, 0.0)
            acc = acc + _dot((ra * ra).astype(BF16), w2_ref[cols, :])
        x2_ref[...] = acc
        n3 = acc * lax.rsqrt(jnp.mean(acc * acc, axis=-1, keepdims=True) + RMS_EPS)
        h3 = (n3 * gple_ref[...]).astype(BF16)
        gate = jax.nn.sigmoid(_dot(h3, wg_ref[...]))
        pp = _dot(p_ref[...].astype(BF16), wp_ref[...])
        x3 = acc + pp * gate

        rs = lax.rsqrt(jnp.mean(x3 * x3, axis=-1, keepdims=True) + RMS_EPS)
        n = x3 * rs
        gv = gfin_ref[...]
        err = n * gv - t_ref[...]
        sq_acc[...] += jnp.sum(err * err, axis=0, keepdims=True)
        dy = err * (1.0 / D_MODEL)
        dg_ref[...] += jnp.sum(dy * n, axis=0, keepdims=True)
        dx_ref[...] = _rms_bwd(dy, n, rs, gv)

        @pl.when(i == nt - 1)
        def _():
            total = jnp.sum(sq_acc[...], axis=1, keepdims=True) * (0.5 / D_MODEL)
            loss_ref[...] = jnp.broadcast_to(total, loss_ref.shape)

    row = lambda w: pl.BlockSpec((tm, w), lambda i: (i, 0))
    outs, _ = _tile_call(
        body, "ffn_ple_loss_fwd", nt,
        [row(D_MODEL), _layer_rows(p[1], tm), _const_spec((1, D_MODEL)), _const_spec((D_MODEL, D_FF)),
         _const_spec((D_FF, D_MODEL)), _const_spec((1, D_MODEL)), _const_spec((D_MODEL, D_MODEL)),
         _const_spec((D_PLE, D_MODEL)), row(D_MODEL), _const_spec((1, D_MODEL))],
        [row(D_FF), row(D_MODEL), _acc_spec((8, 128)), _acc_spec((1, D_MODEL)), row(D_MODEL)],
        [jax.ShapeDtypeStruct((t, D_FF), BF16), jax.ShapeDtypeStruct((t, D_MODEL), F32),
         jax.ShapeDtypeStruct((8, 128), F32), jax.ShapeDtypeStruct((1, D_MODEL), F32),
         jax.ShapeDtypeStruct((t, D_MODEL), F32)],
        [pltpu.VMEM((1, D_MODEL), F32)], (x1, p[0], g_ff, w_ff1, w_ff2, g_ple, w_gate, w_proj, target, g_final), None)
    return outs


def ple_bwd(d, x2, p, g_ple, w_gate, w_proj, *, tm, comm=None):
    t = d.shape[0]
    nt = t // tm

    def body(d_ref, x2_ref, p_ref, g_ref, wg_ref, wp_ref, dx2_ref, dg_ref, dwg_ref, dwp_ref):
        i = pl.program_id(0)

        @pl.when(i == 0)
        def _():
            dg_ref[...] = jnp.zeros_like(dg_ref)
            dwg_ref[...] = jnp.zeros_like(dwg_ref)
            dwp_ref[...] = jnp.zeros_like(dwp_ref)

        dv = d_ref[...]
        x2v = x2_ref[...]
        rs = lax.rsqrt(jnp.mean(x2v * x2v, axis=-1, keepdims=True) + RMS_EPS)
        n3 = x2v * rs
        gv = g_ref[...]
        h3 = (n3 * gv).astype(BF16)
        gate = jax.nn.sigmoid(_dot(h3, wg_ref[...]))
        pb = p_ref[...].astype(BF16)
        pp = _dot(pb, wp_ref[...])
        dwp_ref[...] += _dot_tn(pb, (dv * gate).astype(BF16))
        dpre = (dv * pp * gate * (1.0 - gate)).astype(BF16)
        dwg_ref[...] += _dot_tn(h3, dpre)
        dh3 = _dot_nt(dpre, wg_ref[...])
        dg_ref[...] += jnp.sum(dh3 * n3, axis=0, keepdims=True)
        dx2_ref[...] = dv + _rms_bwd(dh3, n3, rs, gv)

    row = lambda w: pl.BlockSpec((tm, w), lambda i: (i, 0))
    return _tile_call(
        body, "ple_bwd", nt,
        [row(D_MODEL), row(D_MODEL), _layer_rows(p[1], tm), _const_spec((1, D_MODEL)), _const_spec((D_MODEL, D_MODEL)),
         _const_spec((D_PLE, D_MODEL))],
        [row(D_MODEL), _acc_spec((1, D_MODEL)), _acc_spec((D_MODEL, D_MODEL)), _acc_spec((D_PLE, D_MODEL))],
        [jax.ShapeDtypeStruct((t, D_MODEL), F32), jax.ShapeDtypeStruct((1, D_MODEL), F32),
         jax.ShapeDtypeStruct((D_MODEL, D_MODEL), F32), jax.ShapeDtypeStruct((D_PLE, D_MODEL), F32)],
        [], (d, x2, p[0], g_ple, w_gate, w_proj), comm)


def ffn_bwd(dx2, x1, a, g_ff, w_ff1, w_ff2, *, tm, comm=None):
    t = dx2.shape[0]
    nt = t // tm
    nc = D_FF // D_MODEL

    def body(dx2_ref, x1_ref, a_ref, g_ref, w1_ref, w2_ref, dx1_ref, h2_ref, da_ref, dg_ref):
        i = pl.program_id(0)

        @pl.when(i == 0)
        def _():
            dg_ref[...] = jnp.zeros_like(dg_ref)

        dv = dx2_ref[...]
        x1v = x1_ref[...]
        rs = lax.rsqrt(jnp.mean(x1v * x1v, axis=-1, keepdims=True) + RMS_EPS)
        n2 = x1v * rs
        gv = g_ref[...]
        h2_ref[...] = (n2 * gv).astype(BF16)
        dvb = dv.astype(BF16)
        dh2 = jnp.zeros((tm, D_MODEL), F32)
        for c in range(nc):
            cols = slice(c * D_MODEL, (c + 1) * D_MODEL)
            ra = jnp.maximum(a_ref[:, cols].astype(F32), 0.0)
            da = (_dot_nt(dvb, w2_ref[cols, :]) * (2.0 * ra)).astype(BF16)
            da_ref[:, cols] = da
            dh2 = dh2 + _dot_nt(da, w1_ref[:, cols])
        dg_ref[...] += jnp.sum(dh2 * n2, axis=0, keepdims=True)
        dx1_ref[...] = dv + _rms_bwd(dh2, n2, rs, gv)

    row = lambda w: pl.BlockSpec((tm, w), lambda i: (i, 0))
    return _tile_call(
        body, "ffn_bwd", nt,
        [row(D_MODEL), row(D_MODEL), row(D_FF), _const_spec((1, D_MODEL)), _const_spec((D_MODEL, D_FF)),
         _const_spec((D_FF, D_MODEL))],
        [row(D_MODEL), row(D_MODEL), row(D_FF), _acc_spec((1, D_MODEL))],
        [jax.ShapeDtypeStruct((t, D_MODEL), F32), jax.ShapeDtypeStruct((t, D_MODEL), BF16),
         jax.ShapeDtypeStruct((t, D_FF), BF16), jax.ShapeDtypeStruct((1, D_MODEL), F32)],
        [], (dx2, x1, a, g_ff, w_ff1, w_ff2), comm)


def mix_bwd(dx1, x, proj, g_mix, w_in, w_out, wcat, wcat_t, bmat, ln_g, ln_b, avg, conv_w, pool_bd, pool_bd_t,
            pool_scale, *, tm, comm=None):
    t = dx1.shape[0]
    nt = t // tm
    prev_blocks = tm // HALO

    def body(dx1_ref, x_ref, proj_ref, prev_ref, g_ref, win_ref, wout_ref, wcat_ref, wcatt_ref, bmat_ref, lng_ref,
             lnb_ref, avg_ref, cw_ref, pw_ref, pwt_ref, ps_ref,
             dx_ref, h1_ref, dproj_ref, dg_ref, dws_ref, dbm_ref, dlng_ref, dlnb_ref, dcw_ref, dpw_ref, dps_ref,
             dyc, dpj, hbuf, zbuf, dybuf, qbuf):
        i = pl.program_id(0)
        ti = nt - 1 - i

        @pl.when(i == 0)
        def _():
            for ref in (dg_ref, dws_ref, dbm_ref, dlng_ref, dlnb_ref, dcw_ref, dpw_ref, dps_ref):
                ref[...] = jnp.zeros_like(ref)
            dybuf[tm:tm + HALO, :] = jnp.zeros((HALO, D_B), F32)
            qbuf[tm:tm + HALO, :] = jnp.zeros((HALO, D_C), F32)

        dx1v = dx1_ref[...]
        dyc[...] = _dot_nt(dx1v.astype(BF16), wout_ref[...])

        lo_mask = _lane_lt((CHUNK, CHUNK), HEAD_DIM)
        avg = avg_ref[...]
        lng = lng_ref[...]
        gu, dgu = _gelu_and_grad(proj_ref[:, 0:D_A].astype(F32))
        gv, dgv = _gelu_and_grad(proj_ref[:, D_A:2 * D_A].astype(F32))
        cen = gv - _group_mean(gv, avg)
        rstd = lax.rsqrt(_group_mean(cen * cen, avg) + LN_EPS)
        vhat = cen * rstd
        vnb = (vhat * lng + lnb_ref[...]).astype(BF16)
        dya = dyc[:, 0:D_A]
        dm = dya * gu
        dmb = dm.astype(BF16)
        dvn_rows = []
        for c in range(tm // CHUNK):
            rows = slice(c * CHUNK, (c + 1) * CHUNK)
            dbm_ref[...] += dm[rows]
            dvn_parts = []
            for j in range(3):
                cols = slice(j * CHUNK, (j + 1) * CHUNK)
                vnb2 = vnb[rows, cols]
                mixed = _sgu_mix(vnb2, wcat_ref[j], lo_mask) + bmat_ref[:, cols]
                dpj[rows, cols] = dya[rows, cols] * mixed * dgu[rows, cols]
                dmb2 = dmb[rows, cols]
                zero = jnp.zeros_like(dmb2)
                dm_st = jnp.concatenate([jnp.where(lo_mask, dmb2, zero), jnp.where(lo_mask, zero, dmb2)], axis=0)
                dws_ref[j] += _dot_nt(dm_st, vnb2)
                dvn_st = _dot(wcatt_ref[j], dmb2)
                dvn_parts.append(jnp.where(lo_mask, dvn_st[0:CHUNK], dvn_st[CHUNK:2 * CHUNK]))
            dvn_rows.append(jnp.concatenate(dvn_parts, axis=1))
        dvn = jnp.concatenate(dvn_rows, axis=0)
        dlng_ref[...] += jnp.sum(dvn * vhat, axis=0, keepdims=True)
        dlnb_ref[...] += jnp.sum(dvn, axis=0, keepdims=True)
        dvh = dvn * lng
        dgv_in = rstd * (dvh - _group_mean(dvh, avg) - vhat * _group_mean(dvh * vhat, avg))
        dpj[:, D_A:2 * D_A] = dgv_in * dgv

        o = 2 * D_A
        live = (ti > 0).astype(F32)
        zb = proj_ref[:, o:o + D_B].astype(F32)
        gb = proj_ref[:, o + D_B:o + 2 * D_B].astype(F32)
        gc = proj_ref[:, o + 2 * D_B:o + 3 * D_B].astype(F32)
        hcur = gc * zb
        hbuf[0:HALO, :] = (prev_ref[:, o + 2 * D_B:o + 3 * D_B].astype(F32) * prev_ref[:, o:o + D_B].astype(F32)) * live
        hbuf[HALO:HALO + tm, :] = hcur
        hm1 = hbuf[pl.ds(HALO - 1, tm), :]
        hm2 = hbuf[pl.ds(HALO - 2, tm), :]
        y = cw_ref[2:3, :] * hcur + cw_ref[1:2, :] * hm1 + cw_ref[0:1, :] * hm2
        dout = dyc[:, D_A:D_A + D_B]
        dpj[:, o + D_B:o + 2 * D_B] = dout * y
        dy = dout * gb
        dcw_ref[2:3, :] += jnp.sum(dy * hcur, axis=0, keepdims=True)
        dcw_ref[1:2, :] += jnp.sum(dy * hm1, axis=0, keepdims=True)
        dcw_ref[0:1, :] += jnp.sum(dy * hm2, axis=0, keepdims=True)
        dybuf[0:tm, :] = dy
        dh = (cw_ref[2:3, :] * dy + cw_ref[1:2, :] * dybuf[pl.ds(1, tm), :] + cw_ref[0:1, :] * dybuf[pl.ds(2, tm), :])
        dybuf[tm:tm + HALO, :] = dybuf[0:HALO, :]
        dpj[:, o:o + D_B] = dh * gc
        dpj[:, o + 2 * D_B:o + 3 * D_B] = dh * zb

        zc = proj_ref[:, o + 3 * D_B:D_IN].astype(F32)
        zbuf[0:HALO, :] = prev_ref[:, o + 3 * D_B:D_IN].astype(F32) * live
        zbuf[HALO:HALO + tm, :] = zc
        mean, inv = _pool_means(zbuf[...], tm, ti * tm)
        pooled = (mean - zc).astype(BF16)
        dyp = dyc[:, D_A + D_B:D_MODEL]
        ps = ps_ref[...]
        dps_ref[...] += jnp.sum(dyp * _dot(pooled, pw_ref[...]), axis=0, keepdims=True)
        dpw = (dyp * ps).astype(BF16)
        dpw_ref[...] += _dot_tn(pooled, dpw)
        dpooled = _dot(dpw, pwt_ref[...])
        qbuf[0:tm, :] = dpooled * inv
        q = qbuf[...]
        nrows = tm + HALO
        f2 = q + pltpu.roll(q, nrows - 1, 0)
        f4 = f2 + pltpu.roll(f2, nrows - 2, 0)
        f8 = f4 + pltpu.roll(f4, nrows - 4, 0)
        f16 = f8 + pltpu.roll(f8, nrows - 8, 0)
        lane = lax.broadcasted_iota(jnp.int32, (tm, D_C), 1)
        ahead = jnp.where(lane < 64, f2[0:tm], jnp.where(lane < 128, f4[0:tm], jnp.where(lane < 192, f8[0:tm], f16[0:tm])))
        dpj[:, o + 3 * D_B:D_IN] = ahead - dpooled
        qbuf[tm:tm + HALO, :] = qbuf[0:HALO, :]

        dprojb = dpj[...].astype(BF16)
        dproj_ref[...] = dprojb
        dh1 = _dot_nt(dprojb, win_ref[...])
        xv = x_ref[...]
        rs = lax.rsqrt(jnp.mean(xv * xv, axis=-1, keepdims=True) + RMS_EPS)
        n1 = xv * rs
        gv1 = g_ref[...]
        h1_ref[...] = (n1 * gv1).astype(BF16)
        dg_ref[...] += jnp.sum(dh1 * n1, axis=0, keepdims=True)
        dx_ref[...] = dx1v + _rms_bwd(dh1, n1, rs, gv1)

        @pl.when(i == nt - 1)
        def _():
            tril = (lax.broadcasted_iota(jnp.int32, (2 * CHUNK, CHUNK), 0) % CHUNK
                    >= lax.broadcasted_iota(jnp.int32, (2 * CHUNK, CHUNK), 1))
            for j in range(3):
                dws_ref[j] = jnp.where(tril, dws_ref[j], 0.0)
            dbm_ref[...] = _group_mean_split(dbm_ref[...], avg) * float(HEAD_DIM)

    rev = lambda w: pl.BlockSpec((tm, w), lambda i: (nt - 1 - i, 0))
    prev = pl.BlockSpec((HALO, D_IN), lambda i: (jnp.maximum((nt - 1 - i) * prev_blocks - 1, 0), 0))
    acc_shapes = [(1, D_MODEL), (3, 2 * CHUNK, CHUNK), (CHUNK, D_A), (1, D_A), (1, D_A), (8, D_B), (D_C, D_C), (1, D_C)]
    return _tile_call(
        body, "mix_bwd", nt,
        [rev(D_MODEL), rev(D_MODEL), rev(D_IN), prev, _const_spec((1, D_MODEL)), _const_spec((D_MODEL, D_IN)),
         _const_spec((D_MODEL, D_MODEL)), _const_spec((3, CHUNK, 2 * CHUNK)), _const_spec((3, 2 * CHUNK, CHUNK)),
         _const_spec((CHUNK, D_A)), _const_spec((1, D_A)), _const_spec((1, D_A)), _const_spec((D_A, D_A)),
         _const_spec((8, D_B)), _const_spec((D_C, D_C)), _const_spec((D_C, D_C)), _const_spec((1, D_C))],
        [rev(D_MODEL), rev(D_MODEL), rev(D_IN)] + [_acc_spec(s) for s in acc_shapes],
        [jax.ShapeDtypeStruct((t, D_MODEL), F32), jax.ShapeDtypeStruct((t, D_MODEL), BF16),
         jax.ShapeDtypeStruct((t, D_IN), BF16)] + [jax.ShapeDtypeStruct(s, F32) for s in acc_shapes],
        [pltpu.VMEM((tm, D_MODEL), F32), pltpu.VMEM((tm, D_IN), F32),
         pltpu.VMEM((tm + HALO, D_B), F32), pltpu.VMEM((tm + HALO, D_C), F32),
         pltpu.VMEM((tm + HALO, D_B), F32), pltpu.VMEM((tm + HALO, D_C), F32)],
        (dx1, x, proj, proj, g_mix, w_in, w_out, wcat, wcat_t, bmat, ln_g, ln_b, avg, conv_w, pool_bd, pool_bd_t,
         pool_scale), comm)


def wgrad(a, b, *, tk, a_layer=None, relu_sq=False, comm=None):
    t, m = a.shape[-2:]
    n = b.shape[1]
    bm = min(m, 1024)
    bn = 1024 if n % 1024 == 0 else n
    nk = t // tk
    if a_layer is None:
        a_spec = pl.BlockSpec((tk, bm), lambda i, j, k: (k, i))
    else:
        a_spec = pl.BlockSpec((None, None, tk, bm), lambda i, j, k: (a_layer, 0, k, i))

    def body(a_ref, b_ref, o_ref):
        k = pl.program_id(2)

        @pl.when(k == 0)
        def _():
            o_ref[...] = jnp.zeros_like(o_ref)

        av = a_ref[...]
        if relu_sq:
            ra = jnp.maximum(av.astype(F32), 0.0)
            av = ra * ra
        o_ref[...] += _dot_tn(av.astype(BF16), b_ref[...].astype(BF16))

    name = f"wgrad_{m}x{n}" + ("_relu_sq" if relu_sq else "")
    grid = (m // bm, n // bn, nk)
    specs = [a_spec, pl.BlockSpec((tk, bn), lambda i, j, k: (k, j))]
    out_spec = pl.BlockSpec((bm, bn), lambda i, j, k: (i, j))
    if comm is None:
        return pl.pallas_call(body, name=name, grid=grid, in_specs=specs, out_specs=out_spec,
                              out_shape=jax.ShapeDtypeStruct((m, n), F32),
                              compiler_params=_params("parallel", "parallel", "arbitrary"))(a, b)
    ci, co = len(comm.ins), len(comm.out_shapes)

    def hosted(*refs):
        cin, cout, sems = refs[2:2 + ci], refs[3 + ci:3 + ci + co], refs[3 + ci + co:]
        step = (pl.program_id(0) * grid[1] + pl.program_id(1)) * grid[2] + pl.program_id(2)

        @pl.when(step == 0)
        def _():
            comm.start(cin, cout, sems)

        body(refs[0], refs[1], refs[2 + ci])

        @pl.when(step == grid[0] * grid[1] * grid[2] - 1)
        def _():
            comm.wait(cin, cout, sems)

    outs = pl.pallas_call(
        hosted, name=name + "_comm", grid=grid, in_specs=specs + [_ANY] * ci, out_specs=[out_spec] + [_ANY] * co,
        out_shape=[jax.ShapeDtypeStruct((m, n), F32)] + comm.out_shapes, scratch_shapes=comm.sems,
        input_output_aliases={2 + i: 1 + o for i, o in comm.aliases.items()},
        compiler_params=_params("arbitrary", "arbitrary", "arbitrary"),
    )(a, b, *comm.ins)
    return outs[0], outs[1:]


def _row_block(rows, cols, target_bytes):
    target = max(8, target_bytes // (4 * cols))
    if rows <= target:
        return rows
    best = None
    for br in range(8, target + 1, 8):
        if rows % br == 0:
            best = br
    return best if best is not None else rows


def adamw(w, g, m, v):
    shape = w.shape
    cols = shape[-1]
    rows = math.prod(shape[:-1]) if len(shape) > 1 else 1
    br = _row_block(rows, cols, 1 << 20)

    def body(w_ref, g_ref, m_ref, v_ref, d_ref, nm_ref, nv_ref):
        gv = g_ref[...]
        nm = ADAM_B1 * m_ref[...] + (1.0 - ADAM_B1) * gv
        nv = ADAM_B2 * v_ref[...] + (1.0 - ADAM_B2) * jnp.square(gv)
        m_hat = nm / (1.0 - ADAM_B1 ** ADAM_STEP)
        v_hat = nv / (1.0 - ADAM_B2 ** ADAM_STEP)
        d_ref[...] = -ADAM_LR * (m_hat / (jnp.sqrt(v_hat) + ADAM_EPS) + ADAM_WD * w_ref[...])
        nm_ref[...] = nm
        nv_ref[...] = nv

    spec = pl.BlockSpec((br, cols), lambda i: (i, 0))
    outs = pl.pallas_call(
        body, name="adamw", grid=(rows // br,),
        in_specs=[spec] * 4, out_specs=[spec] * 3,
        out_shape=[jax.ShapeDtypeStruct((rows, cols), F32)] * 3,
        compiler_params=pltpu.CompilerParams(dimension_semantics=("parallel",)),
    )(*(a.reshape(rows, cols) for a in (w, g, m, v)))
    return tuple(o.reshape(shape) for o in outs)


ADD_STEPS = 4


def add_halves(geoms, arrs, received, c_idx, dtypes):
    n = len(arrs)

    def body(c_ref, *refs):
        del c_ref
        for a in range(n):
            refs[2 * n + a][...] = (refs[a][...] + refs[n + a][...]).astype(dtypes[a])

    own_specs, half_specs = [], []
    for g in geoms:
        if g.kind == "cols":
            rows, cols = g.half_shape[0] // ADD_STEPS, g.half_shape[1]
            own_specs.append(pl.BlockSpec((rows, cols), lambda i, c_ref: (ADD_STEPS * c_ref[0] + i, 0)))
            half_specs.append(pl.BlockSpec((rows, cols), lambda i, c_ref: (i, 0)))
        else:
            _, h, cols = g.half_shape
            own_specs.append(pl.BlockSpec((None, None, h, cols), lambda i, c_ref: (i, c_ref[0], 0, 0)))
            half_specs.append(pl.BlockSpec((None, h, cols), lambda i, c_ref: (i, 0, 0)))
    return pl.pallas_call(
        body, name="add_halves",
        grid_spec=pltpu.PrefetchScalarGridSpec(num_scalar_prefetch=1, grid=(ADD_STEPS,),
                                               in_specs=own_specs + half_specs, out_specs=half_specs),
        out_shape=[jax.ShapeDtypeStruct(g.half_shape, dt) for g, dt in zip(geoms, dtypes)],
        compiler_params=_params("parallel"),
    )(c_idx, *arrs, *received)


def add_parts(geoms, landed, finals, layer, c_idx):
    n = len(landed)

    def body(c_ref, *refs):
        del c_ref
        for a in range(n):
            p_ref = refs[a]
            parts = [p_ref[j].astype(F32) for j in range(N_CHIPS)]
            refs[2 * n + a][...] = ((parts[0] + parts[1]) + parts[2]) + parts[3]

    in_specs, out_specs = [], []
    for g in geoms:
        rows, cols = g.part_shape[0] // ADD_STEPS, g.part_shape[1]
        in_specs.append(pl.BlockSpec((N_CHIPS, rows, cols), lambda i, c_ref: (0, i, 0)))
        if g.everywhere:
            out_specs.append(pl.BlockSpec((None, rows, cols), lambda i, c_ref: (c_ref[1], ADD_STEPS * c_ref[0] + i, 0)))
        else:
            out_specs.append(pl.BlockSpec((None, rows, cols), lambda i, c_ref: (layer, ADD_STEPS * c_ref[0] + i, 0)))
    return pl.pallas_call(
        body, name="add_parts",
        grid_spec=pltpu.PrefetchScalarGridSpec(num_scalar_prefetch=1, grid=(ADD_STEPS,),
                                               in_specs=in_specs + [_ANY] * n, out_specs=out_specs),
        out_shape=[jax.ShapeDtypeStruct(f.shape, F32) for f in finals],
        input_output_aliases={1 + n + a: a for a in range(n)},
        compiler_params=_params("parallel"),
    )(c_idx, *landed, *finals)


def _shard_dims(k, n, axis):
    return (k // N_CHIPS, n) if axis == 0 else (k, n // N_CHIPS)


W_IN_STRIDE = 512
W_IN_WINDOW = 640


def _big_geoms():
    geoms = []
    for name, k, n, axis in BIG:
        if axis == 0:
            geoms.append(_Geom("rows", (N_CHIPS, 2, k // N_CHIPS // 2, n)))
        elif name == "w_in":
            geoms.append(_Geom("cols", (k, n), W_IN_STRIDE, W_IN_WINDOW))
        else:
            geoms.append(_Geom("cols", (k, n), n // N_CHIPS, n // N_CHIPS))
    return geoms


def _grad_views(gb, geoms):
    return [gb[name].reshape(g.shape) for (name, _, _, _), g in zip(BIG, geoms)]


def _round_up(v, m):
    return (v + m - 1) // m * m


def _prep_small(small):
    tril = jnp.tril(jnp.ones((CHUNK, CHUNK), bool))
    wm = jnp.where(tril, small["sgu_w"], 0.0).astype(BF16).reshape(DEPTH, 3, 2, CHUNK, CHUNK)
    head = jnp.arange(D_A) // HEAD_DIM
    grp = jnp.arange(D_C) // HEAD_DIM
    pw_rows = small["pool_w"].reshape(DEPTH, D_C, HEAD_DIM)
    pool_bd = jnp.where((grp[:, None] == grp[None, :])[None], jnp.tile(pw_rows, (1, 1, D_C // HEAD_DIM)), 0.0).astype(BF16)
    return dict(
        wcat=wm.transpose(0, 1, 3, 2, 4).reshape(DEPTH, 3, CHUNK, 2 * CHUNK),
        wcat_t=wm.transpose(0, 1, 2, 4, 3).reshape(DEPTH, 3, 2 * CHUNK, CHUNK),
        bmat=jnp.repeat(jnp.swapaxes(small["sgu_b"], 1, 2), HEAD_DIM, axis=2),
        avg=jnp.where(head[:, None] == head[None, :], 1.0 / HEAD_DIM, 0.0).astype(BF16),
        pool_bd=pool_bd, pool_bd_t=jnp.swapaxes(pool_bd, 1, 2),
        conv8=jnp.pad(small["conv_w"], ((0, 0), (0, 8 - 3), (0, 0))),
    )


def _row(a):
    return a.reshape(1, -1)


MIX_WEIGHTS = ("w_in", "w_out")
MLP_WEIGHTS = ("w_ff1", "w_ff2", "w_ple_gate", "w_ple_proj")
ALL_BIG = MIX_WEIGHTS + MLP_WEIGHTS
FFN_BWD_TILE = 512
PLE_BWD_TILE = 1024


def _fwd_layer(h, p, wl, small, prep, l, tm, comm_mix=None, comm_mlp=None, target=None):
    (proj, ycat, x1), got = mix_fwd(h, _row(small["norm_mix_g"][l]), wl["w_in"], wl["w_out"], prep["wcat"][l],
                                    prep["bmat"][l], _row(small["sgu_ln_g"][l]), _row(small["sgu_ln_b"][l]), prep["avg"],
                                    prep["conv8"][l], prep["pool_bd"][l], _row(small["pool_scale"][l]),
                                    tm=min(2 * tm, h.shape[0]), comm=comm_mix)
    if comm_mix is not None:
        wl = {**wl, **_weights_of(got, MLP_WEIGHTS)}
    if target is not None:
        a, x2, loss_blk, d_final_g, d = ffn_ple_loss_fwd(
            x1, (p, l), _row(small["norm_ff_g"][l]), wl["w_ff1"], wl["w_ff2"], _row(small["norm_ple_g"][l]),
            wl["w_ple_gate"], wl["w_ple_proj"], target, _row(small["final_g"]), tm=tm)
        return (h, proj, ycat, x1, a, x2), (loss_blk, d_final_g, d), [], wl
    (a, x2, x3), couts = ffn_ple_fwd(x1, (p, l), _row(small["norm_ff_g"][l]), wl["w_ff1"], wl["w_ff2"],
                                     _row(small["norm_ple_g"][l]), wl["w_ple_gate"], wl["w_ple_proj"], tm=tm,
                                     comm=comm_mlp)
    return (h, proj, ycat, x1, a, x2), x3, couts, wl


def _merge_comms(comms):
    comms = [cm for cm in comms if cm is not None]
    if len(comms) <= 1:
        return comms[0] if comms else None
    spans, ni, no, ns = [], 0, 0, 0
    for cm in comms:
        spans.append((ni, no, ns))
        ni, no, ns = ni + len(cm.ins), no + len(cm.out_shapes), ns + len(cm.sems)

    def copies(in_refs, out_refs, sem_refs):
        local, sends, recvs = [], [], []
        for cm, (i0, o0, s0) in zip(comms, spans):
            got = cm.copies(in_refs[i0:i0 + len(cm.ins)], out_refs[o0:o0 + len(cm.out_shapes)],
                            sem_refs[s0:s0 + len(cm.sems)])
            local, sends, recvs = local + got[0], sends + got[1], recvs + got[2]
        return local, sends, recvs

    aliases = {i0 + i: o0 + o for cm, (i0, o0, _) in zip(comms, spans) for i, o in cm.aliases.items()}
    assert all(cm.forwards is None for cm in comms)
    return _Comm(sum((cm.ins for cm in comms), []), sum((cm.out_shapes for cm in comms), []),
                 sum((cm.sems for cm in comms), []), copies, aliases)


def _split_results(results, comms):
    out, at = [], 0
    for cm in comms:
        if cm is None:
            out.append(None)
        else:
            out.append(results[at:at + len(cm.out_shapes)])
            at += len(cm.out_shapes)
    return out


class _Reduction:
    def __init__(self, layer, names, geoms, arrs, finals, c_arr, narrow=()):
        self.layer, self.names, self.geoms, self.arrs = layer, list(names), list(geoms), list(arrs)
        self.finals, self.c_arr = finals, c_arr
        self.dtypes = [BF16 if n in narrow else F32 for n in self.names]

    def comm_a(self):
        return _reduce_a_comm(self.geoms, self.arrs)

    def comm_b(self, received):
        return _reduce_b_comm(self.geoms, add_halves(self.geoms, self.arrs, received, self.c_arr, self.dtypes))

    def comm_c(self, landed):
        mine = add_parts(self.geoms, landed, [self.finals[n] for n in self.names], self.layer, self.c_arr)
        return _reduce_c_comm(self.geoms, mine, self.layer)

    def done(self, results):
        self.finals.update(zip(self.names, results))


class _Plan:
    def ple(self):
        return None

    def after_ple(self, results):
        pass

    def ffn(self):
        return None

    def after_ffn(self, results):
        pass

    def out_grad(self, gb):
        return None

    def after_out_grad(self, results):
        pass

    def before_mix(self, gb):
        pass

    def mix(self):
        return None

    def after_mix(self, results):
        pass


class _CarryPlan(_Plan):
    def __init__(self, above):
        self.above = above

    def ple(self):
        return self.above.comm_a()

    def after_ple(self, results):
        self.received = results

    def ffn(self):
        return self.above.comm_b(self.received)

    def after_ffn(self, results):
        self.landed = results

    def mix(self):
        return self.above.comm_c(self.landed)

    def after_mix(self, results):
        self.above.done(results)


class _LastPlan(_CarryPlan):
    def __init__(self, above, make):
        super().__init__(above)
        self.make = make

    def out_grad(self, gb):
        self.early = [self.make(MLP_WEIGHTS, gb)]
        return self.early[0].comm_a()

    def after_out_grad(self, results):
        self.early_received = [results]

    def before_mix(self, gb):
        self.early.append(self.make(("w_out",), gb))
        self.early_received.append(_run_comm(self.early[1].comm_a(), "reduce_a_early"))

    def mix(self):
        self.parts = [self.above.comm_c(self.landed)] + [r.comm_b(got) for r, got in zip(self.early, self.early_received)]
        return _merge_comms(self.parts)

    def after_mix(self, results):
        above_res, *self.early_landed = _split_results(results, self.parts)
        self.above.done(above_res)


def _bwd_layer(d, saved, p, wl, small, prep, l, tm, tk, plan=None):
    plan = plan or _Plan()
    xin, proj, ycat, x1, a, x2 = saved
    (dx2, dg_ple, dw_gate, dw_proj), res = ple_bwd(d, x2, (p, l), _row(small["norm_ple_g"][l]), wl["w_ple_gate"],
                                                   wl["w_ple_proj"], tm=min(PLE_BWD_TILE, d.shape[0]), comm=plan.ple())
    plan.after_ple(res)
    gb = {"w_ple_gate": dw_gate, "w_ple_proj": dw_proj}
    (dx1, h2, da, dg_ff), res = ffn_bwd(dx2, x1, a, _row(small["norm_ff_g"][l]), wl["w_ff1"], wl["w_ff2"],
                                        tm=FFN_BWD_TILE if tm >= FFN_BWD_TILE else tm, comm=plan.ffn())
    plan.after_ffn(res)
    gb["w_ff2"] = wgrad(a, dx2, tk=tk, relu_sq=True)
    gb["w_ff1"] = wgrad(h2, da, tk=min(2 * tk, h2.shape[0]))
    comm = plan.out_grad(gb)
    if comm is None:
        gb["w_out"] = wgrad(ycat, dx1, tk=tk)
    else:
        gb["w_out"], res = wgrad(ycat, dx1, tk=tk, comm=comm)
        plan.after_out_grad(res)
    plan.before_mix(gb)
    (dprev, h1, dproj, dg_mix, dws, dbm, dlng, dlnb, dcw, dpw, dps), res = mix_bwd(
        dx1, xin, proj, _row(small["norm_mix_g"][l]), wl["w_in"], wl["w_out"], prep["wcat"][l], prep["wcat_t"][l],
        prep["bmat"][l], _row(small["sgu_ln_g"][l]), _row(small["sgu_ln_b"][l]), prep["avg"], prep["conv8"][l],
        prep["pool_bd"][l], prep["pool_bd_t"][l], _row(small["pool_scale"][l]), tm=tm, comm=plan.mix())
    plan.after_mix(res)
    gb["w_in"] = wgrad(h1, dproj, tk=tk)
    gs = {
        "norm_ple_g": dg_ple[0], "norm_ff_g": dg_ff[0], "norm_mix_g": dg_mix[0],
        "sgu_w": dws.reshape(2 * 3, CHUNK, CHUNK), "sgu_b": dbm[:, ::HEAD_DIM].T,
        "sgu_ln_g": dlng[0], "sgu_ln_b": dlnb[0], "conv_w": dcw[0:3], "pool_scale": dps[0],
        "pool_w": jnp.stack([dpw[g * HEAD_DIM:(g + 1) * HEAD_DIM, g * HEAD_DIM:(g + 1) * HEAD_DIM]
                             for g in range(D_C // HEAD_DIM)]),
    }
    return dprev, gb, gs


def _local_step(x, p, target, full, small, *, tm, tk):
    prep = _prep_small(small)
    p = p[:, None]
    saved, h = [], x
    for l in range(DEPTH):
        wl = {name: full[name][l] for name in full}
        s, h, _, _ = _fwd_layer(h, p, wl, small, prep, l, tm, target=target if l == DEPTH - 1 else None)
        saved.append(s)
    loss_blk, d_final_g, d = h
    gbig, gsm = [None] * DEPTH, [None] * DEPTH
    for l in reversed(range(DEPTH)):
        wl = {name: full[name][l] for name in full}
        d, gbig[l], gsm[l] = _bwd_layer(d, saved[l], p, wl, small, prep, l, tm, tk)
    big = {name: jnp.stack([gbig[l][name] for l in range(DEPTH)]) for name in gbig[0]}
    sm = {name: jnp.stack([gsm[l][name] for l in range(DEPTH)]) for name in gsm[0]}
    sm["final_g"] = d_final_g[0]
    return loss_blk[0, 0], d, big, sm


def _weights_of(gathered, names):
    wl = dict(zip([b[0] for b in BIG if b[0] in names], gathered))
    if "w_in" in wl:
        wl["w_in"] = wl["w_in"].transpose(1, 0, 2).reshape(D_MODEL, D_IN)
    return wl


def kernel(x, p, norm_mix_g, w_in, sgu_w, sgu_b, sgu_ln_g, sgu_ln_b, conv_w, pool_w, pool_scale, w_out, norm_ff_g, w_ff1, w_ff2, norm_ple_g, w_ple_gate, w_ple_proj, final_g, loss_target, m_norm_mix_g, m_w_in, m_sgu_w, m_sgu_b, m_sgu_ln_g, m_sgu_ln_b, m_conv_w, m_pool_w, m_pool_scale, m_w_out, m_norm_ff_g, m_w_ff1, m_w_ff2, m_norm_ple_g, m_w_ple_gate, m_w_ple_proj, m_final_g, v_norm_mix_g, v_w_in, v_sgu_w, v_sgu_b, v_sgu_ln_g, v_sgu_ln_b, v_conv_w, v_pool_w, v_pool_scale, v_w_out, v_norm_ff_g, v_w_ff1, v_w_ff2, v_norm_ple_g, v_w_ple_gate, v_w_ple_proj, v_final_g):
    args = dict(locals())
    w = {name: args[name] for name in WEIGHTS}
    m = {name: args["m_" + name] for name in WEIGHTS}
    v = {name: args["v_" + name] for name in WEIGHTS}
    t = x.shape[1]
    tm = min(512, t)
    tk = min(2048, t)
    x_idx, y_idx, c_idx = _place()
    chip = 2 * x_idx + y_idx
    c_arr = jnp.stack([c_idx, chip]).astype(jnp.int32)
    xs, target = x[0], loss_target[0]

    shards = {name: w[name].astype(BF16) for name, _, _, _ in BIG}
    conv_rows = _round_up(CONV_SHARD, 8 * 128) // 128
    conv_flat = jnp.pad(w["conv_w"].reshape(-1), (0, conv_rows * 128 - CONV_SHARD)).reshape(conv_rows, 128)
    first = _run_comm(_gather_halved_comm(shards, 0, MIX_WEIGHTS, conv_flat), "gather_first")
    conv_full = (first[len(MIX_WEIGHTS)].reshape(N_CHIPS, -1)[:, :CONV_SHARD]
                 .reshape(N_CHIPS, DEPTH, 3, D_B // N_CHIPS).transpose(1, 2, 0, 3).reshape(DEPTH, 3, D_B))
    small = {name: w[name] for name in SMALL}
    small["conv_w"] = conv_full
    prep = _prep_small(small)

    wl = [None] * DEPTH
    wl[0] = _weights_of(first, MIX_WEIGHTS)
    saved, h = [], xs
    for l in range(DEPTH):
        comm_mix = None
        if l == 0:
            comm_mix = _gather_halved_comm(shards, 0, MLP_WEIGHTS)
        comm_mlp = _gather_comm(shards, l + 1, ALL_BIG) if l + 1 < DEPTH else None
        s, h, got, wl[l] = _fwd_layer(h, p, wl[l], small, prep, l, tm, comm_mix, comm_mlp,
                                      target=target if l == DEPTH - 1 else None)
        saved.append(s)
        if comm_mlp is not None:
            wl[l + 1] = _weights_of(got, ALL_BIG)
    loss_blk, d_final_g, d = h

    geoms = dict(zip([b[0] for b in BIG], _big_geoms()))
    finals = {name: jnp.zeros((DEPTH,) + g.final_shape, F32) for name, g in geoms.items()}

    def reduction(layer, names, gb, narrow=()):
        return _Reduction(layer, names, [geoms[n] for n in names], [gb[n].reshape(geoms[n].shape) for n in names],
                          finals, c_arr, narrow)

    gsm = [None] * DEPTH
    above = None
    for l in reversed(range(DEPTH)):
        if above is None:
            plan = _Plan()
        elif l > 0:
            plan = _CarryPlan(above)
        else:
            plan = _LastPlan(above, lambda names, gb: reduction(0, names, gb))
        d, gb, gsm[l] = _bwd_layer(d, saved[l], p, wl[l], small, prep, l, tm, tk, plan)
        if l > 0:
            above = reduction(l, ALL_BIG, gb)

    sm = {name: jnp.stack([gsm[i][name] for i in range(DEPTH)]) for name in gsm[0]}
    sm["final_g"] = d_final_g[0]
    sizes = [sm[name].size for name in SMALL]
    small_rows = _round_up(-(-sum(sizes) // (2 * N_CHIPS * LANES)), 8 * ADD_STEPS)
    small_flat = jnp.pad(jnp.concatenate([sm[name].reshape(-1) for name in SMALL]),
                         (0, 2 * N_CHIPS * small_rows * LANES - sum(sizes)))
    geoms["small"] = _Geom("rows", (N_CHIPS, 2, small_rows, LANES), everywhere=True)
    finals["small"] = jnp.zeros((N_CHIPS,) + geoms["small"].final_shape, F32)
    late = reduction(0, ("w_in", "small"), {**gb, "small": small_flat}, narrow=("w_in",))
    late_landed = _run_comm(late.comm_b(_run_comm(late.comm_a(), "reduce_a_late")), "reduce_b_late")
    finishing = plan.early + [late]
    last = [r.comm_c(got) for r, got in zip(finishing, plan.early_landed + [late_landed])]
    for r, res in zip(finishing, _split_results(_run_comm(_merge_comms(last), "reduce_c_last"), last)):
        r.done(res)

    grads = {name: finals[name] for name, _, _, _ in BIG}
    grads["w_in"] = lax.dynamic_slice_in_dim(grads["w_in"], chip * (D_IN // N_CHIPS - W_IN_STRIDE), D_IN // N_CHIPS, axis=2)
    small_red = finals["small"].reshape(-1)
    off = 0
    for name, size in zip(SMALL, sizes):
        grads[name] = small_red[off:off + size].reshape(sm[name].shape)
        off += size
    grads["conv_w"] = lax.dynamic_slice_in_dim(grads["conv_w"], chip * (D_B // N_CHIPS), D_B // N_CHIPS, axis=2)

    loss = lax.psum(loss_blk[0, 0], ("x", "y", "c"))
    delta, new_m, new_v = {}, {}, {}
    for name in WEIGHTS:
        delta[name], new_m[name], new_v[name] = adamw(w[name], grads[name], m[name], v[name])
    return (loss, d[None], *[grads[n] for n in WEIGHTS], *[delta[n] for n in WEIGHTS],
            *[new_m[n] for n in WEIGHTS], *[new_v[n] for n in WEIGHTS])
```
